```python
import math
import jax, jax.numpy as jnp
from jax import lax
import numpy as np

D_MODEL = 1024
BATCH = 16
SEQ = 2048
DEPTH = 4

PLE_DIM = 256
EPS = 1e-6
BLOCK = 128
SB_HEADS = 8
SB_DIM = 64
SB_WIDTH = SB_HEADS * SB_DIM
HG_HEADS = 4
HG_DK = 128
HG_DV = 128
HG_QK = HG_HEADS * HG_DK
HG_V = HG_HEADS * HG_DV
HG_CHUNK = 64
AB_SPLITS = [SB_WIDTH] * 3 + [HG_QK] * 2 + [HG_V] * 2
AB_IN = sum(AB_SPLITS)
AB_OUT = SB_WIDTH + HG_V
SW_HEADS = 16
SW_KV_HEADS = 4
SW_DIM = 64
SW_GROUP = SW_HEADS // SW_KV_HEADS
WINDOW = 128
C_IN = (SW_HEADS + 2 * SW_KV_HEADS) * SW_DIM
C_OUT = SW_HEADS * SW_DIM
N_BUCKETS = 32
MAX_DISTANCE = 128
D_FF = 2816
CONV_W = 3
N_EVEN = (DEPTH + 1) // 2
N_ODD = DEPTH // 2

kernel_name = 'hybrid_sb_hgrn2_swa_convffn'


def rmsnorm(x, g):
    xf = x.astype(jnp.float32)
    y = xf * lax.rsqrt(jnp.mean(xf * xf, axis=-1, keepdims=True) + EPS)
    return (y * g.astype(jnp.float32)).astype(x.dtype)


def stick_breaking_attention(q, k, v):
    S = q.shape[1]
    qf = jnp.swapaxes(q, 1, 2).astype(jnp.float32)
    kf = jnp.swapaxes(k, 1, 2).astype(jnp.float32)
    vf = jnp.swapaxes(v, 1, 2).astype(jnp.float32)
    scale = SB_DIM ** -0.5
    outs = []
    for n in range(S // BLOCK):
        t0 = n * BLOCK
        kn = t0 + BLOCK
        z = jnp.einsum('bhtd,bhsd->bhts', qf[:, :, t0:kn], kf[:, :, :kn]) * scale
        t_pos = t0 + jnp.arange(BLOCK)[:, None]
        s_pos = jnp.arange(kn)[None, :]
        mask = s_pos < t_pos
        log_keep = jnp.where(mask, jax.nn.log_sigmoid(-z), 0.0)
        later = lax.cumsum(log_keep, axis=3, reverse=True) - log_keep
        w = jnp.where(mask, jnp.exp(jax.nn.log_sigmoid(z) + later), 0.0)
        outs.append(jnp.einsum('bhts,bhsd->bhtd', w, vf[:, :, :kn]))
    o = jnp.concatenate(outs, axis=2)
    return jnp.swapaxes(o, 1, 2)


def hgrn2(q, f_pre, i, lb):
    B, S = q.shape[:2]
    lb = lb.reshape(HG_HEADS, HG_DK).astype(jnp.float32)
    fp = f_pre.astype(jnp.float32)
    log_f = jnp.log(lb + (1.0 - lb) * jax.nn.sigmoid(fp))
    kk = (1.0 - lb) * jax.nn.sigmoid(-fp)
    qf = jax.nn.silu(q.astype(jnp.float32))
    nc = S // HG_CHUNK

    def to_chunks(a):
        return a.reshape(B, nc, HG_CHUNK, HG_HEADS, a.shape[-1]).transpose(1, 0, 3, 2, 4)

    causal = jnp.tril(jnp.ones((HG_CHUNK, HG_CHUNK), dtype=bool))

    def step(state, xs):
        qc, kc, lfc, ic = xs
        b = jnp.cumsum(lfc, axis=2)
        o_inter = jnp.einsum('bhtk,bhkv->bhtv', qc * jnp.exp(b), state)
        rel = jnp.where(causal[:, :, None], b[:, :, :, None, :] - b[:, :, None, :, :], -jnp.inf)
        scores = jnp.einsum('bhtk,bhsk,bhtsk->bhts', qc, kc, jnp.exp(rel))
        o = o_inter + jnp.einsum('bhts,bhsv->bhtv', scores, ic)
        b_last = b[:, :, -1:, :]
        state = jnp.exp(b_last[:, :, 0, :, None]) * state + jnp.einsum('bhsk,bhsv->bhkv', kc * jnp.exp(b_last - b), ic)
        return state, o

    s0 = jnp.zeros((B, HG_HEADS, HG_DK, HG_DV), jnp.float32)
    _, o = lax.scan(step, s0, (to_chunks(qf), to_chunks(kk), to_chunks(log_f), to_chunks(i.astype(jnp.float32))))
    return o.transpose(1, 0, 3, 2, 4).reshape(B, S, HG_HEADS, HG_DV)


def t5_band_buckets():
    t = np.arange(WINDOW)[:, None]
    s = np.arange(2 * WINDOW)[None, :]
    dist = t + WINDOW - s
    band = (dist >= 0) & (dist < WINDOW)
    max_exact = N_BUCKETS // 2
    large = max_exact + (np.log(np.maximum(dist, max_exact) / max_exact) / math.log(MAX_DISTANCE / max_exact) * (N_BUCKETS - max_exact)).astype(np.int32)
    large = np.minimum(large, N_BUCKETS - 1)
    bucket = np.where(dist < max_exact, np.maximum(dist, 0), large).astype(np.int32)
    return bucket, band


def sliding_window_attention(q, k, v, sinks, rel_bias):
    B, S = q.shape[:2]
    nb = S // WINDOW
    bucket, band = t5_band_buckets()
    bias = rel_bias.astype(jnp.float32)[bucket]
    bias = bias.transpose(2, 0, 1).reshape(SW_KV_HEADS, SW_GROUP, WINDOW, 2 * WINDOW)
    key_pos = np.arange(nb)[:, None] * WINDOW - WINDOW + np.arange(2 * WINDOW)[None, :]
    mask = jnp.asarray(band[None] & (key_pos >= 0)[:, None, :])
    qb = q.astype(jnp.float32).reshape(B, nb, WINDOW, SW_KV_HEADS, SW_GROUP, SW_DIM).transpose(1, 0, 2, 3, 4, 5)

    def band_keys(a):
        ap = jnp.pad(a.astype(jnp.float32), ((0, 0), (WINDOW, 0), (0, 0), (0, 0)))
        ap = ap.reshape(B, nb + 1, WINDOW, SW_KV_HEADS, SW_DIM)
        return jnp.concatenate([ap[:, :-1], ap[:, 1:]], axis=2).transpose(1, 0, 2, 3, 4)

    kb, vb = band_keys(k), band_keys(v)
    sink = sinks.astype(jnp.float32).reshape(SW_KV_HEADS, SW_GROUP, 1, 1)
    scale = SW_DIM ** -0.5

    def block_attn(args):
        qn, kn, vn, mn = args
        logits = jnp.einsum('bqhgd,bkhd->bhgqk', qn, kn) * scale + bias
        logits = jnp.where(mn, logits, -jnp.inf)
        m = jnp.maximum(jnp.max(logits, axis=-1, keepdims=True), sink)
        e = jnp.exp(logits - m)
        w = e / (jnp.sum(e, axis=-1, keepdims=True) + jnp.exp(sink - m))
        return jnp.einsum('bhgqk,bkhd->bqhgd', w, vn)

    o = lax.map(block_attn, (qb, kb, vb, mask))
    return o.transpose(1, 0, 2, 3, 4, 5).reshape(B, S, SW_HEADS, SW_DIM)


def mixer_ab(h, w_in, lb, hg_norm, w_out):
    B, S, _ = h.shape
    proj = h @ w_in
    qa, ka, va, qb, fb, ib, gb = jnp.split(proj, np.cumsum(AB_SPLITS)[:-1].tolist(), axis=-1)
    sb_shape = (B, S, SB_HEADS, SB_DIM)
    o_a = stick_breaking_attention(qa.reshape(sb_shape), ka.reshape(sb_shape), va.reshape(sb_shape))
    o_a = o_a.astype(h.dtype).reshape(B, S, SB_WIDTH)
    o_b = hgrn2(qb.reshape(B, S, HG_HEADS, HG_DK), fb.reshape(B, S, HG_HEADS, HG_DK), ib.reshape(B, S, HG_HEADS, HG_DV), lb)
    o_b = rmsnorm(o_b.astype(h.dtype), hg_norm) * jax.nn.silu(gb.reshape(B, S, HG_HEADS, HG_DV))
    o_b = o_b.reshape(B, S, HG_V)
    return jnp.concatenate([o_a, o_b], axis=-1) @ w_out


def mixer_c(h, w_in, q_norm, k_norm, sinks, rel_bias, w_out):
    B, S, _ = h.shape
    proj = h @ w_in
    q, k, v = jnp.split(proj, [SW_HEADS * SW_DIM, (SW_HEADS + SW_KV_HEADS) * SW_DIM], axis=-1)
    q = rmsnorm(q.reshape(B, S, SW_HEADS, SW_DIM), q_norm)
    k = rmsnorm(k.reshape(B, S, SW_KV_HEADS, SW_DIM), k_norm)
    v = v.reshape(B, S, SW_KV_HEADS, SW_DIM)
    o = sliding_window_attention(q, k, v, sinks, rel_bias).astype(h.dtype)
    return o.reshape(B, S, C_OUT) @ w_out


def conv_glu_ffn(h, w_up, conv_w, conv_b, w_down):
    u = h @ w_up
    u = lax.conv_general_dilated(u, conv_w[:, None, :], window_strides=(1,), padding=[(CONV_W - 1, 0)],
                                 dimension_numbers=('NWC', 'WIO', 'NWC'), feature_group_count=2 * D_FF) + conv_b
    gate, up = jnp.split(u, 2, axis=-1)
    return (jax.nn.silu(gate) * up) @ w_down


def _fwd_setup_inputs(seed: int = 0) -> dict:
    key = jax.random.key(seed)
    ks = jax.random.split(key, 21)

    def nrm(k, shape):
        return jax.random.normal(k, shape, jnp.float32)

    def w(k, shape, fan_in):
        return nrm(k, shape) * fan_in ** -0.5

    def gain(k, shape):
        return 1.0 + 0.02 * nrm(k, shape)

    F2 = 2 * D_FF
    return {
        'x': nrm(ks[0], (BATCH, SEQ, D_MODEL)),
        'p': nrm(ks[1], (DEPTH, BATCH, SEQ, PLE_DIM)),
        'mix_norm': gain(ks[2], (DEPTH, D_MODEL)),
        'ab_w_in': w(ks[3], (N_EVEN, D_MODEL, AB_IN), D_MODEL),
        'hg_lb_logits': 0.5 * nrm(ks[4], (N_EVEN, HG_QK)),
        'hg_out_norm': gain(ks[5], (N_EVEN, HG_DV)),
        'ab_w_out': w(ks[6], (N_EVEN, AB_OUT, D_MODEL), AB_OUT),
        'c_w_in': w(ks[7], (N_ODD, D_MODEL, C_IN), D_MODEL),
        'q_norm': gain(ks[8], (N_ODD, SW_DIM)),
        'k_norm': gain(ks[9], (N_ODD, SW_DIM)),
        'sinks': 0.5 * nrm(ks[10], (N_ODD, SW_HEADS)),
        'rel_bias': 0.5 * nrm(ks[11], (N_BUCKETS, SW_HEADS)),
        'c_w_out': w(ks[12], (N_ODD, C_OUT, D_MODEL), C_OUT),
        'ffn_norm': gain(ks[13], (DEPTH, D_MODEL)),
        'ffn_up': w(ks[14], (DEPTH, D_MODEL, F2), D_MODEL),
        'ffn_conv': w(ks[15], (DEPTH, CONV_W, F2), CONV_W),
        'ffn_conv_b': 0.02 * nrm(ks[16], (DEPTH, F2)),
        'ffn_down': w(ks[17], (DEPTH, D_FF, D_MODEL), D_FF),
        'ple_norm': gain(ks[18], (DEPTH, D_MODEL)),
        'ple_gate': w(ks[19], (DEPTH, D_MODEL, D_MODEL), D_MODEL),
        'ple_proj': w(ks[20], (DEPTH, PLE_DIM, D_MODEL), PLE_DIM),
    }


def _fwd_reference(x, p, mix_norm, ab_w_in, hg_lb_logits, hg_out_norm, ab_w_out, c_w_in, q_norm, k_norm,
              sinks, rel_bias, c_w_out, ffn_norm, ffn_up, ffn_conv, ffn_conv_b, ffn_down,
              ple_norm, ple_gate, ple_proj):
    lb_cum = jnp.cumsum(jax.nn.softmax(hg_lb_logits.astype(jnp.float32), axis=0), axis=0)
    lower_bounds = lb_cum - lb_cum[0]
    h = x
    for i in range(DEPTH):
        j = i // 2
        hn = rmsnorm(h, mix_norm[i])
        if i % 2 == 0:
            h = h + mixer_ab(hn, ab_w_in[j], lower_bounds[j], hg_out_norm[j], ab_w_out[j])
        else:
            h = h + mixer_c(hn, c_w_in[j], q_norm[j], k_norm[j], sinks[j], rel_bias, c_w_out[j])
        h = h + conv_glu_ffn(rmsnorm(h, ffn_norm[i]), ffn_up[i], ffn_conv[i], ffn_conv_b[i], ffn_down[i])
        gate = jax.nn.sigmoid(rmsnorm(h, ple_norm[i]) @ ple_gate[i])
        h = h + gate * (p[i] @ ple_proj[i])
    return h


import jax as _jax
import jax.numpy as _jnp

TWIN_FORMAT = 'train_step'
FWD_PARAMS = ['x', 'p', 'mix_norm', 'ab_w_in', 'hg_lb_logits', 'hg_out_norm', 'ab_w_out', 'c_w_in', 'q_norm', 'k_norm', 'sinks', 'rel_bias', 'c_w_out', 'ffn_norm', 'ffn_up', 'ffn_conv', 'ffn_conv_b', 'ffn_down', 'ple_norm', 'ple_gate', 'ple_proj']
TWIN_WEIGHTS = ['mix_norm', 'ab_w_in', 'hg_lb_logits', 'hg_out_norm', 'ab_w_out', 'c_w_in', 'q_norm', 'k_norm', 'sinks', 'rel_bias', 'c_w_out', 'ffn_norm', 'ffn_up', 'ffn_conv', 'ffn_conv_b', 'ffn_down', 'ple_norm', 'ple_gate', 'ple_proj']
TWIN_DIFF_INPUT = 'x'
TWIN_INPUTS = ['x', 'p', 'mix_norm', 'ab_w_in', 'hg_lb_logits', 'hg_out_norm', 'ab_w_out', 'c_w_in', 'q_norm', 'k_norm', 'sinks', 'rel_bias', 'c_w_out', 'ffn_norm', 'ffn_up', 'ffn_conv', 'ffn_conv_b', 'ffn_down', 'ple_norm', 'ple_gate', 'ple_proj', 'loss_target', 'm_mix_norm', 'm_ab_w_in', 'm_hg_lb_logits', 'm_hg_out_norm', 'm_ab_w_out', 'm_c_w_in', 'm_q_norm', 'm_k_norm', 'm_sinks', 'm_rel_bias', 'm_c_w_out', 'm_ffn_norm', 'm_ffn_up', 'm_ffn_conv', 'm_ffn_conv_b', 'm_ffn_down', 'm_ple_norm', 'm_ple_gate', 'm_ple_proj', 'v_mix_norm', 'v_ab_w_in', 'v_hg_lb_logits', 'v_hg_out_norm', 'v_ab_w_out', 'v_c_w_in', 'v_q_norm', 'v_k_norm', 'v_sinks', 'v_rel_bias', 'v_c_w_out', 'v_ffn_norm', 'v_ffn_up', 'v_ffn_conv', 'v_ffn_conv_b', 'v_ffn_down', 'v_ple_norm', 'v_ple_gate', 'v_ple_proj']
TWIN_OUTPUTS = ['loss', 'grad_x', 'grad_mix_norm', 'grad_ab_w_in', 'grad_hg_lb_logits', 'grad_hg_out_norm', 'grad_ab_w_out', 'grad_c_w_in', 'grad_q_norm', 'grad_k_norm', 'grad_sinks', 'grad_rel_bias', 'grad_c_w_out', 'grad_ffn_norm', 'grad_ffn_up', 'grad_ffn_conv', 'grad_ffn_conv_b', 'grad_ffn_down', 'grad_ple_norm', 'grad_ple_gate', 'grad_ple_proj', 'delta_mix_norm', 'delta_ab_w_in', 'delta_hg_lb_logits', 'delta_hg_out_norm', 'delta_ab_w_out', 'delta_c_w_in', 'delta_q_norm', 'delta_k_norm', 'delta_sinks', 'delta_rel_bias', 'delta_c_w_out', 'delta_ffn_norm', 'delta_ffn_up', 'delta_ffn_conv', 'delta_ffn_conv_b', 'delta_ffn_down', 'delta_ple_norm', 'delta_ple_gate', 'delta_ple_proj', 'new_m_mix_norm', 'new_m_ab_w_in', 'new_m_hg_lb_logits', 'new_m_hg_out_norm', 'new_m_ab_w_out', 'new_m_c_w_in', 'new_m_q_norm', 'new_m_k_norm', 'new_m_sinks', 'new_m_rel_bias', 'new_m_c_w_out', 'new_m_ffn_norm', 'new_m_ffn_up', 'new_m_ffn_conv', 'new_m_ffn_conv_b', 'new_m_ffn_down', 'new_m_ple_norm', 'new_m_ple_gate', 'new_m_ple_proj', 'new_v_mix_norm', 'new_v_ab_w_in', 'new_v_hg_lb_logits', 'new_v_hg_out_norm', 'new_v_ab_w_out', 'new_v_c_w_in', 'new_v_q_norm', 'new_v_k_norm', 'new_v_sinks', 'new_v_rel_bias', 'new_v_c_w_out', 'new_v_ffn_norm', 'new_v_ffn_up', 'new_v_ffn_conv', 'new_v_ffn_conv_b', 'new_v_ffn_down', 'new_v_ple_norm', 'new_v_ple_gate', 'new_v_ple_proj']
TWIN_LEAF_KINDS = {'loss': 'loss', 'grad_x': 'grad_x', 'grad_mix_norm': 'grad_w', 'grad_ab_w_in': 'grad_w', 'grad_hg_lb_logits': 'grad_w', 'grad_hg_out_norm': 'grad_w', 'grad_ab_w_out': 'grad_w', 'grad_c_w_in': 'grad_w', 'grad_q_norm': 'grad_w', 'grad_k_norm': 'grad_w', 'grad_sinks': 'grad_w', 'grad_rel_bias': 'grad_w', 'grad_c_w_out': 'grad_w', 'grad_ffn_norm': 'grad_w', 'grad_ffn_up': 'grad_w', 'grad_ffn_conv': 'grad_w', 'grad_ffn_conv_b': 'grad_w', 'grad_ffn_down': 'grad_w', 'grad_ple_norm': 'grad_w', 'grad_ple_gate': 'grad_w', 'grad_ple_proj': 'grad_w', 'delta_mix_norm': 'delta_w', 'delta_ab_w_in': 'delta_w', 'delta_hg_lb_logits': 'delta_w', 'delta_hg_out_norm': 'delta_w', 'delta_ab_w_out': 'delta_w', 'delta_c_w_in': 'delta_w', 'delta_q_norm': 'delta_w', 'delta_k_norm': 'delta_w', 'delta_sinks': 'delta_w', 'delta_rel_bias': 'delta_w', 'delta_c_w_out': 'delta_w', 'delta_ffn_norm': 'delta_w', 'delta_ffn_up': 'delta_w', 'delta_ffn_conv': 'delta_w', 'delta_ffn_conv_b': 'delta_w', 'delta_ffn_down': 'delta_w', 'delta_ple_norm': 'delta_w', 'delta_ple_gate': 'delta_w', 'delta_ple_proj': 'delta_w', 'new_m_mix_norm': 'new_m', 'new_m_ab_w_in': 'new_m', 'new_m_hg_lb_logits': 'new_m', 'new_m_hg_out_norm': 'new_m', 'new_m_ab_w_out': 'new_m', 'new_m_c_w_in': 'new_m', 'new_m_q_norm': 'new_m', 'new_m_k_norm': 'new_m', 'new_m_sinks': 'new_m', 'new_m_rel_bias': 'new_m', 'new_m_c_w_out': 'new_m', 'new_m_ffn_norm': 'new_m', 'new_m_ffn_up': 'new_m', 'new_m_ffn_conv': 'new_m', 'new_m_ffn_conv_b': 'new_m', 'new_m_ffn_down': 'new_m', 'new_m_ple_norm': 'new_m', 'new_m_ple_gate': 'new_m', 'new_m_ple_proj': 'new_m', 'new_v_mix_norm': 'new_v', 'new_v_ab_w_in': 'new_v', 'new_v_hg_lb_logits': 'new_v', 'new_v_hg_out_norm': 'new_v', 'new_v_ab_w_out': 'new_v', 'new_v_c_w_in': 'new_v', 'new_v_q_norm': 'new_v', 'new_v_k_norm': 'new_v', 'new_v_sinks': 'new_v', 'new_v_rel_bias': 'new_v', 'new_v_c_w_out': 'new_v', 'new_v_ffn_norm': 'new_v', 'new_v_ffn_up': 'new_v', 'new_v_ffn_conv': 'new_v', 'new_v_ffn_conv_b': 'new_v', 'new_v_ffn_down': 'new_v', 'new_v_ple_norm': 'new_v', 'new_v_ple_gate': 'new_v', 'new_v_ple_proj': 'new_v'}


def _forward(args):
    return _fwd_reference(*[args[k] for k in FWD_PARAMS])


def _output_shape():
    out = _jax.eval_shape(lambda: _forward(_fwd_setup_inputs(0)))
    return out.shape, out.dtype

N_MICROBATCH = 1
ADAM_LR = 0.001
ADAM_B1 = 0.9
ADAM_B2 = 0.999
ADAM_EPS = 1e-08
ADAM_WD = 0.01
ADAM_STEP = 10
PER_EXAMPLE_BATCH_AXIS = {'x': 0, 'p': 1, 'loss_target': 0}
SHARED_INPUTS = []
_WEIGHT_DTYPES = {'mix_norm': _jnp.float32, 'ab_w_in': _jnp.float32, 'hg_lb_logits': _jnp.float32, 'hg_out_norm': _jnp.float32, 'ab_w_out': _jnp.float32, 'c_w_in': _jnp.float32, 'q_norm': _jnp.float32, 'k_norm': _jnp.float32, 'sinks': _jnp.float32, 'rel_bias': _jnp.float32, 'c_w_out': _jnp.float32, 'ffn_norm': _jnp.float32, 'ffn_up': _jnp.float32, 'ffn_conv': _jnp.float32, 'ffn_conv_b': _jnp.float32, 'ffn_down': _jnp.float32, 'ple_norm': _jnp.float32, 'ple_gate': _jnp.float32, 'ple_proj': _jnp.float32}
MOMENT_SCALE = {'mix_norm': 1.070403e+01, 'ab_w_in': 4.226628e-01, 'hg_lb_logits': 3.433341e-02, 'hg_out_norm': 4.557929e+01, 'ab_w_out': 6.107042e-01, 'c_w_in': 2.395350e-01, 'q_norm': 7.295522e+00, 'k_norm': 7.294055e+00, 'sinks': 8.717667e-01, 'rel_bias': 1.747635e+00, 'c_w_out': 2.082920e-01, 'ffn_norm': 2.529193e+01, 'ffn_up': 3.042142e-01, 'ffn_conv': 3.388825e+00, 'ffn_conv_b': 3.106548e+00, 'ffn_down': 4.543616e-01, 'ple_norm': 8.454502e-01, 'ple_gate': 1.339735e-01, 'ple_proj': 4.649833e-01}


def _to_microbatches(a, axis):
    t = _jnp.moveaxis(a, axis, 0)
    t = t.reshape((N_MICROBATCH, t.shape[0] // N_MICROBATCH) + t.shape[1:])
    return _jnp.moveaxis(t, 1, axis + 1)


def setup_inputs(seed: int = 0) -> dict:
    inp = _fwd_setup_inputs(seed)
    key = _jax.random.fold_in(_jax.random.key(seed), 7919)
    shape, _ = _output_shape()
    out = dict(inp)
    out["loss_target"] = _jax.random.normal(_jax.random.fold_in(key, 0), shape, _jnp.float32)
    for i, name in enumerate(TWIN_WEIGHTS):
        w = inp[name].astype(_jnp.float32)
        if MOMENT_SCALE is None:
            s = _jnp.sqrt(_jnp.mean(_jnp.square(w)) + 1e-30)
        else:
            s = MOMENT_SCALE[name]
        km, kv = _jax.random.split(_jax.random.fold_in(key, i + 1))
        out[name] = w
        out["m_" + name] = s * _jax.random.normal(km, w.shape, _jnp.float32)
        out["v_" + name] = (s * s) * _jax.random.uniform(kv, w.shape, _jnp.float32, 0.5, 1.5)
    if N_MICROBATCH > 1:
        for name, axis in PER_EXAMPLE_BATCH_AXIS.items():
            out[name] = _to_microbatches(out[name], axis)
    return {'x': out['x'], 'p': out['p'], 'mix_norm': out['mix_norm'], 'ab_w_in': out['ab_w_in'], 'hg_lb_logits': out['hg_lb_logits'], 'hg_out_norm': out['hg_out_norm'], 'ab_w_out': out['ab_w_out'], 'c_w_in': out['c_w_in'], 'q_norm': out['q_norm'], 'k_norm': out['k_norm'], 'sinks': out['sinks'], 'rel_bias': out['rel_bias'], 'c_w_out': out['c_w_out'], 'ffn_norm': out['ffn_norm'], 'ffn_up': out['ffn_up'], 'ffn_conv': out['ffn_conv'], 'ffn_conv_b': out['ffn_conv_b'], 'ffn_down': out['ffn_down'], 'ple_norm': out['ple_norm'], 'ple_gate': out['ple_gate'], 'ple_proj': out['ple_proj'], 'loss_target': out['loss_target'], 'm_mix_norm': out['m_mix_norm'], 'm_ab_w_in': out['m_ab_w_in'], 'm_hg_lb_logits': out['m_hg_lb_logits'], 'm_hg_out_norm': out['m_hg_out_norm'], 'm_ab_w_out': out['m_ab_w_out'], 'm_c_w_in': out['m_c_w_in'], 'm_q_norm': out['m_q_norm'], 'm_k_norm': out['m_k_norm'], 'm_sinks': out['m_sinks'], 'm_rel_bias': out['m_rel_bias'], 'm_c_w_out': out['m_c_w_out'], 'm_ffn_norm': out['m_ffn_norm'], 'm_ffn_up': out['m_ffn_up'], 'm_ffn_conv': out['m_ffn_conv'], 'm_ffn_conv_b': out['m_ffn_conv_b'], 'm_ffn_down': out['m_ffn_down'], 'm_ple_norm': out['m_ple_norm'], 'm_ple_gate': out['m_ple_gate'], 'm_ple_proj': out['m_ple_proj'], 'v_mix_norm': out['v_mix_norm'], 'v_ab_w_in': out['v_ab_w_in'], 'v_hg_lb_logits': out['v_hg_lb_logits'], 'v_hg_out_norm': out['v_hg_out_norm'], 'v_ab_w_out': out['v_ab_w_out'], 'v_c_w_in': out['v_c_w_in'], 'v_q_norm': out['v_q_norm'], 'v_k_norm': out['v_k_norm'], 'v_sinks': out['v_sinks'], 'v_rel_bias': out['v_rel_bias'], 'v_c_w_out': out['v_c_w_out'], 'v_ffn_norm': out['v_ffn_norm'], 'v_ffn_up': out['v_ffn_up'], 'v_ffn_conv': out['v_ffn_conv'], 'v_ffn_conv_b': out['v_ffn_conv_b'], 'v_ffn_down': out['v_ffn_down'], 'v_ple_norm': out['v_ple_norm'], 'v_ple_gate': out['v_ple_gate'], 'v_ple_proj': out['v_ple_proj']}


def _loss(weights, diff, rest, loss_target):
    with _jax.named_scope("forward"):
        args = {**rest, TWIN_DIFF_INPUT: diff, **{k: w.astype(_WEIGHT_DTYPES[k]) for k, w in weights.items()}}
        y = _forward(args)
    with _jax.named_scope("loss_head"):
        err = _jnp.square(y.astype(_jnp.float32) - loss_target)
        return 0.5 * _jnp.sum(_jnp.mean(err, axis=-1)) if err.ndim else 0.5 * err


def _adamw(w, g, m, v):
    m = ADAM_B1 * m + (1.0 - ADAM_B1) * g
    v = ADAM_B2 * v + (1.0 - ADAM_B2) * _jnp.square(g)
    m_hat = m / (1.0 - ADAM_B1 ** ADAM_STEP)
    v_hat = v / (1.0 - ADAM_B2 ** ADAM_STEP)
    delta = -ADAM_LR * (m_hat / (_jnp.sqrt(v_hat) + ADAM_EPS) + ADAM_WD * w)
    return delta, m, v


def reference(x, p, mix_norm, ab_w_in, hg_lb_logits, hg_out_norm, ab_w_out, c_w_in, q_norm, k_norm, sinks, rel_bias, c_w_out, ffn_norm, ffn_up, ffn_conv, ffn_conv_b, ffn_down, ple_norm, ple_gate, ple_proj, loss_target, m_mix_norm, m_ab_w_in, m_hg_lb_logits, m_hg_out_norm, m_ab_w_out, m_c_w_in, m_q_norm, m_k_norm, m_sinks, m_rel_bias, m_c_w_out, m_ffn_norm, m_ffn_up, m_ffn_conv, m_ffn_conv_b, m_ffn_down, m_ple_norm, m_ple_gate, m_ple_proj, v_mix_norm, v_ab_w_in, v_hg_lb_logits, v_hg_out_norm, v_ab_w_out, v_c_w_in, v_q_norm, v_k_norm, v_sinks, v_rel_bias, v_c_w_out, v_ffn_norm, v_ffn_up, v_ffn_conv, v_ffn_conv_b, v_ffn_down, v_ple_norm, v_ple_gate, v_ple_proj):
    given = dict(x=x, p=p, mix_norm=mix_norm, ab_w_in=ab_w_in, hg_lb_logits=hg_lb_logits, hg_out_norm=hg_out_norm, ab_w_out=ab_w_out, c_w_in=c_w_in, q_norm=q_norm, k_norm=k_norm, sinks=sinks, rel_bias=rel_bias, c_w_out=c_w_out, ffn_norm=ffn_norm, ffn_up=ffn_up, ffn_conv=ffn_conv, ffn_conv_b=ffn_conv_b, ffn_down=ffn_down, ple_norm=ple_norm, ple_gate=ple_gate, ple_proj=ple_proj, loss_target=loss_target, m_mix_norm=m_mix_norm, m_ab_w_in=m_ab_w_in, m_hg_lb_logits=m_hg_lb_logits, m_hg_out_norm=m_hg_out_norm, m_ab_w_out=m_ab_w_out, m_c_w_in=m_c_w_in, m_q_norm=m_q_norm, m_k_norm=m_k_norm, m_sinks=m_sinks, m_rel_bias=m_rel_bias, m_c_w_out=m_c_w_out, m_ffn_norm=m_ffn_norm, m_ffn_up=m_ffn_up, m_ffn_conv=m_ffn_conv, m_ffn_conv_b=m_ffn_conv_b, m_ffn_down=m_ffn_down, m_ple_norm=m_ple_norm, m_ple_gate=m_ple_gate, m_ple_proj=m_ple_proj, v_mix_norm=v_mix_norm, v_ab_w_in=v_ab_w_in, v_hg_lb_logits=v_hg_lb_logits, v_hg_out_norm=v_hg_out_norm, v_ab_w_out=v_ab_w_out, v_c_w_in=v_c_w_in, v_q_norm=v_q_norm, v_k_norm=v_k_norm, v_sinks=v_sinks, v_rel_bias=v_rel_bias, v_c_w_out=v_c_w_out, v_ffn_norm=v_ffn_norm, v_ffn_up=v_ffn_up, v_ffn_conv=v_ffn_conv, v_ffn_conv_b=v_ffn_conv_b, v_ffn_down=v_ffn_down, v_ple_norm=v_ple_norm, v_ple_gate=v_ple_gate, v_ple_proj=v_ple_proj)
    weights = {n: given[n] for n in TWIN_WEIGHTS}
    shared = {n: given[n] for n in SHARED_INPUTS}
    per_example = {n: given[n] for n in ['x', 'p']}
    grad_fn = _jax.value_and_grad(_loss, argnums=(0, 1))

    def one_microbatch(ex, loss_target):
        ex = dict(ex)
        diff = ex.pop(TWIN_DIFF_INPUT)
        return grad_fn(weights, diff, {**shared, **ex}, loss_target)

    if N_MICROBATCH == 1:
        loss, (grad_w, grad_x) = one_microbatch(per_example, given["loss_target"])
    else:
        def body(carry, xs):
            loss_sum, grad_sum = carry
            l_k, (gw_k, gx_k) = one_microbatch(xs[0], xs[1])
            with _jax.named_scope("update"):
                return (loss_sum + l_k, _jax.tree.map(_jnp.add, grad_sum, gw_k)), gx_k

        init = (_jnp.zeros((), _jnp.float32), _jax.tree.map(_jnp.zeros_like, weights))
        (loss, grad_w), grad_x = _jax.lax.scan(body, init, (per_example, given["loss_target"]))
    with _jax.named_scope("update"):
        delta_w, new_m, new_v = {}, {}, {}
        for n in TWIN_WEIGHTS:
            delta_w[n], new_m[n], new_v[n] = _adamw(weights[n], grad_w[n], given["m_" + n], given["v_" + n])
    return (loss, grad_x, *[grad_w[n] for n in TWIN_WEIGHTS], *[delta_w[n] for n in TWIN_WEIGHTS],
            *[new_m[n] for n in TWIN_WEIGHTS], *[new_v[n] for n in TWIN_WEIGHTS])
```

```python
import math

import numpy as np
import jax
import jax.numpy as jnp
from jax import lax
from jax.experimental import pallas as pl
from jax.experimental.pallas import tpu as pltpu

F32 = jnp.float32
BF16 = jnp.bfloat16
MESH = pl.DeviceIdType.MESH
ANY = pl.BlockSpec(memory_space=pl.ANY)

D_MODEL = 1024
EPS = 1e-6
SB_HEADS, SB_DIM = 8, 64
HG_HEADS, HG_DK = 4, 128
HG_CHUNK = 32
SW_HEADS, SW_KV, SW_DIM, WINDOW = 16, 4, 64, 128
N_BUCKETS, MAX_DISTANCE = 32, 128
D_FF = 2816
ATT_BLOCK = 128
LANES = 128
NEG = -1e30

ADAM_LR, ADAM_B1, ADAM_B2, ADAM_EPS, ADAM_WD, ADAM_STEP = 0.001, 0.9, 0.999, 1e-08, 0.01, 10

VMEM_LIMIT = 56 * 1024 * 1024


def _cparams(*sem):
    return pltpu.CompilerParams(dimension_semantics=sem, vmem_limit_bytes=VMEM_LIMIT)


def _pick(n, cap):
    if n <= cap:
        return n
    best = None
    for d in range(LANES, cap + 1, LANES):
        if n % d == 0:
            best = d
    assert best is not None, (n, cap)
    return best


def _dot(a, b, ca, cb):
    return lax.dot_general(a.astype(BF16), b.astype(BF16), (((ca,), (cb,)), ((), ())),
                           preferred_element_type=F32)


def _split(x, terms):
    parts = []
    for _ in range(terms):
        hi = x.astype(BF16)
        parts.append(hi)
        x = x - hi.astype(F32)
    return parts


def _dot_exact_l(x, m, terms=2):
    out = None
    for p in _split(x, terms):
        t = lax.dot_general(p, m, (((1,), (0,)), ((), ())), preferred_element_type=F32)
        out = t if out is None else out + t
    return out


def _dot_exact_r(m, x, terms=3, cm=1):
    out = None
    for p in _split(x, terms):
        t = lax.dot_general(m, p, (((cm,), (0,)), ((), ())), preferred_element_type=F32)
        out = t if out is None else out + t
    return out


def _sig(x):
    return 1.0 / (1.0 + jnp.exp(-x))


def _iota2(shape, dim):
    return lax.broadcasted_iota(jnp.int32, shape, dim)


def _matmul(a, b, mode, name, out_dtype=F32, res=None):
    if mode == "nn":
        (M, K), N = a.shape, b.shape[1]
    elif mode == "nt":
        (M, K), N = a.shape, b.shape[0]
    else:
        (K, M), N = a.shape, b.shape[1]
    tm, tn, tk = _pick(M, 1024), _pick(N, 512), _pick(K, 2048)
    nk = K // tk
    if mode == "tn":
        a_spec = pl.BlockSpec((tk, tm), lambda i, j, k: (k, i))
    else:
        a_spec = pl.BlockSpec((tm, tk), lambda i, j, k: (i, k))
    if mode == "nt":
        b_spec = pl.BlockSpec((tn, tk), lambda i, j, k: (j, k))
    else:
        b_spec = pl.BlockSpec((tk, tn), lambda i, j, k: (k, j))
    ca, cb = {"nn": (1, 0), "nt": (1, 1), "tn": (0, 0)}[mode]
    o_spec = pl.BlockSpec((tm, tn), lambda i, j, k: (i, j))

    def body(*refs):
        if res is None:
            a_ref, b_ref, o_ref, acc = refs
        else:
            a_ref, b_ref, r_ref, o_ref, acc = refs
        k = pl.program_id(2)

        @pl.when(k == 0)
        def _():
            acc[...] = jnp.zeros_like(acc)

        acc[...] += _dot(a_ref[...], b_ref[...], ca, cb)

        @pl.when(k == nk - 1)
        def _():
            r = acc[...]
            if res is not None:
                r = r + r_ref[...]
            o_ref[...] = r.astype(out_dtype)

    ins = [a, b] + ([] if res is None else [res])
    in_specs = [a_spec, b_spec] + ([] if res is None else [o_spec])
    return pl.pallas_call(
        body, name=name, grid=(M // tm, N // tn, nk), in_specs=in_specs, out_specs=o_spec,
        out_shape=jax.ShapeDtypeStruct((M, N), out_dtype),
        scratch_shapes=[pltpu.VMEM((tm, tn), F32)],
        compiler_params=_cparams("parallel", "parallel", "arbitrary"))(*ins)


ROW_TILE = 512


def _row_spec(width):
    return pl.BlockSpec((ROW_TILE, width), lambda i: (i, 0))


def _vec_spec(width):
    return pl.BlockSpec((1, width), lambda i: (0, 0))


def _rmsnorm_fwd(h, g, name):
    T, Dm = h.shape

    def body(h_ref, g_ref, o_ref):
        x = h_ref[...]
        r = lax.rsqrt(jnp.mean(x * x, axis=1, keepdims=True) + EPS)
        o_ref[...] = (x * r * g_ref[...]).astype(BF16)

    return pl.pallas_call(
        body, name=name, grid=(T // ROW_TILE,), in_specs=[_row_spec(Dm), _vec_spec(Dm)],
        out_specs=_row_spec(Dm), out_shape=jax.ShapeDtypeStruct((T, Dm), BF16),
        compiler_params=_cparams("parallel"))(h, g)


def _rmsnorm_bwd(h, g, dhn, dres, name):
    T, Dm = h.shape

    def body(h_ref, g_ref, dy_ref, dr_ref, dh_ref, dg_ref):
        i = pl.program_id(0)
        x = h_ref[...]
        dy = dy_ref[...]
        r = lax.rsqrt(jnp.mean(x * x, axis=1, keepdims=True) + EPS)
        gdy = dy * g_ref[...]
        m = jnp.mean(x * gdy, axis=1, keepdims=True)
        dh_ref[...] = dr_ref[...] + r * gdy - x * (r * r * r * m)
        part = jnp.sum(dy * x * r, axis=0, keepdims=True)

        @pl.when(i == 0)
        def _():
            dg_ref[...] = part

        @pl.when(i > 0)
        def _():
            dg_ref[...] += part

    return pl.pallas_call(
        body, name=name, grid=(T // ROW_TILE,),
        in_specs=[_row_spec(Dm), _vec_spec(Dm), _row_spec(Dm), _row_spec(Dm)],
        out_specs=[_row_spec(Dm), _vec_spec(Dm)],
        out_shape=[jax.ShapeDtypeStruct((T, Dm), F32), jax.ShapeDtypeStruct((1, Dm), F32)],
        compiler_params=_cparams("arbitrary"))(h, g, dhn, dres)


def _ple_fwd(h, z, pp, name):
    T, Dm = h.shape

    def body(h_ref, z_ref, p_ref, o_ref):
        o_ref[...] = h_ref[...] + _sig(z_ref[...]) * p_ref[...]

    return pl.pallas_call(
        body, name=name, grid=(T // ROW_TILE,), in_specs=[_row_spec(Dm)] * 3, out_specs=_row_spec(Dm),
        out_shape=jax.ShapeDtypeStruct((T, Dm), F32), compiler_params=_cparams("parallel"))(h, z, pp)


def _ple_bwd(dh, z, pp, name):
    T, Dm = dh.shape

    def body(dh_ref, z_ref, p_ref, dz_ref, dp_ref):
        s = _sig(z_ref[...])
        d = dh_ref[...]
        dz_ref[...] = d * p_ref[...] * s * (1.0 - s)
        dp_ref[...] = d * s

    return pl.pallas_call(
        body, name=name, grid=(T // ROW_TILE,), in_specs=[_row_spec(Dm)] * 3, out_specs=[_row_spec(Dm)] * 2,
        out_shape=[jax.ShapeDtypeStruct((T, Dm), F32)] * 2, compiler_params=_cparams("parallel"))(dh, z, pp)


def _loss_fwd_bwd(y, target, name):
    T, Dm = y.shape

    def body(y_ref, t_ref, l_ref, d_ref):
        i = pl.program_id(0)
        e = y_ref[...] - t_ref[...]
        d_ref[...] = e * (1.0 / Dm)
        part = jnp.full((8, LANES), 0.5 / Dm, F32) * jnp.sum(e * e)

        @pl.when(i == 0)
        def _():
            l_ref[...] = part

        @pl.when(i > 0)
        def _():
            l_ref[...] += part

    return pl.pallas_call(
        body, name=name, grid=(T // ROW_TILE,), in_specs=[_row_spec(Dm)] * 2,
        out_specs=[pl.BlockSpec((8, LANES), lambda i: (0, 0)), _row_spec(Dm)],
        out_shape=[jax.ShapeDtypeStruct((8, LANES), F32), jax.ShapeDtypeStruct((T, Dm), F32)],
        compiler_params=_cparams("arbitrary"))(y, target)


def _head_mean_matrix():
    r = _iota2((LANES, LANES), 0) >= SW_DIM
    c = _iota2((LANES, LANES), 1) >= SW_DIM
    return jnp.where(r == c, 1.0 / SW_DIM, 0.0).astype(BF16)


def _headnorm_fwd(x, g_lane, name):
    T = x.shape[0]
    C = g_lane.shape[1]

    def body(x_ref, g_ref, y_ref):
        xv = x_ref[...]
        ms = _dot_exact_l(xv * xv, _head_mean_matrix())
        y_ref[...] = xv * lax.rsqrt(ms + EPS) * g_ref[...]

    spec = pl.BlockSpec((ROW_TILE, LANES), lambda j, i: (i, j))
    return pl.pallas_call(
        body, name=name, grid=(C // LANES, T // ROW_TILE),
        in_specs=[spec, pl.BlockSpec((1, LANES), lambda j, i: (0, j))], out_specs=spec,
        out_shape=jax.ShapeDtypeStruct((T, C), F32), compiler_params=_cparams("parallel", "parallel"))(x, g_lane)


def _headnorm_bwd(x, g_lane, dy_full, name):
    T = x.shape[0]
    C = g_lane.shape[1]

    def body(x_ref, g_ref, dy_ref, dx_ref, dg_ref):
        i = pl.program_id(1)
        xv = x_ref[...]
        dy = dy_ref[...]
        bd = _head_mean_matrix()
        r = lax.rsqrt(_dot_exact_l(xv * xv, bd) + EPS)
        gdy = dy * g_ref[...]
        m = _dot_exact_l(xv * gdy, bd)
        dx_ref[...] = r * gdy - xv * (r * r * r * m)
        part = jnp.sum(dy * xv * r, axis=0, keepdims=True)

        @pl.when(i == 0)
        def _():
            dg_ref[...] = part

        @pl.when(i > 0)
        def _():
            dg_ref[...] += part

    spec = pl.BlockSpec((ROW_TILE, LANES), lambda j, i: (i, j))
    vspec = pl.BlockSpec((1, LANES), lambda j, i: (0, j))
    return pl.pallas_call(
        body, name=name, grid=(C // LANES, T // ROW_TILE), in_specs=[spec, vspec, spec],
        out_specs=[spec, vspec],
        out_shape=[jax.ShapeDtypeStruct(dy_full.shape, F32), jax.ShapeDtypeStruct((1, C), F32)],
        input_output_aliases={2: 0}, compiler_params=_cparams("parallel", "arbitrary"))(x, g_lane, dy_full)


CONV_TILE = 128


def _shift_down(x, k):
    rows = _iota2(x.shape, 0)
    return jnp.where(rows >= k, pltpu.roll(x, k, 0), 0.0)


def _shift_up(x, k):
    n = x.shape[0]
    rows = _iota2(x.shape, 0)
    return jnp.where(rows < n - k, pltpu.roll(x, n - k, 0), 0.0)


def _conv3(u, w_ref, b_ref):
    return w_ref[2:3, :] * u + w_ref[1:2, :] * _shift_down(u, 1) + w_ref[0:1, :] * _shift_down(u, 2) + b_ref[...]


def _convglu_fwd(u, cw, cb, S, name):
    T = u.shape[0]
    nf = D_FF // CONV_TILE

    def body(ug_ref, uu_ref, wg_ref, wu_ref, bg_ref, bu_ref, a_ref):
        yg = _conv3(ug_ref[...], wg_ref, bg_ref)
        yu = _conv3(uu_ref[...], wu_ref, bu_ref)
        a_ref[...] = (yg * _sig(yg) * yu).astype(BF16)

    def blk(rows, off):
        return pl.BlockSpec((rows, CONV_TILE), (lambda b, j: (b, j + off)) if rows == S else (lambda b, j: (0, j + off)))

    return pl.pallas_call(
        body, name=name, grid=(T // S, nf),
        in_specs=[blk(S, 0), blk(S, nf), blk(3, 0), blk(3, nf), blk(1, 0), blk(1, nf)],
        out_specs=blk(S, 0), out_shape=jax.ShapeDtypeStruct((T, D_FF), BF16),
        compiler_params=_cparams("parallel", "parallel"))(u, u, cw, cw, cb, cb)


def _convglu_bwd(u, da, cw, cb, S, name):
    T = u.shape[0]
    nf = D_FF // CONV_TILE

    def body(uo_ref, up_ref, da_ref, wo_ref, wp_ref, bo_ref, bp_ref, du_ref, dw_ref, db_ref):
        j = pl.program_id(0)
        b = pl.program_id(1)
        uo = uo_ref[...]
        yo = _conv3(uo, wo_ref, bo_ref)
        yp = _conv3(up_ref[...], wp_ref, bp_ref)
        own_is_gate = j < nf
        gate = jnp.where(own_is_gate, yo, yp)
        up = jnp.where(own_is_gate, yp, yo)
        s = _sig(gate)
        da_v = da_ref[...]
        dy = jnp.where(own_is_gate, da_v * up * (s * (1.0 + gate * (1.0 - s))), da_v * gate * s)
        du_ref[...] = wo_ref[2:3, :] * dy + wo_ref[1:2, :] * _shift_up(dy, 1) + wo_ref[0:1, :] * _shift_up(dy, 2)
        dws = [jnp.sum(dy * _shift_down(uo, 2), axis=0, keepdims=True),
               jnp.sum(dy * _shift_down(uo, 1), axis=0, keepdims=True),
               jnp.sum(dy * uo, axis=0, keepdims=True)]
        dbv = jnp.sum(dy, axis=0, keepdims=True)

        @pl.when(b == 0)
        def _():
            for k in range(3):
                dw_ref[k:k + 1, :] = dws[k]
            db_ref[...] = dbv

        @pl.when(b > 0)
        def _():
            for k in range(3):
                dw_ref[k:k + 1, :] += dws[k]
            db_ref[...] += dbv

    def own(rows):
        return pl.BlockSpec((rows, CONV_TILE), (lambda j, b: (b, j)) if rows == S else (lambda j, b: (0, j)))

    def partner(rows):
        return pl.BlockSpec((rows, CONV_TILE), (lambda j, b: (b, (j + nf) % (2 * nf))) if rows == S
                            else (lambda j, b: (0, (j + nf) % (2 * nf))))

    return pl.pallas_call(
        body, name=name, grid=(2 * nf, T // S),
        in_specs=[own(S), partner(S), pl.BlockSpec((S, CONV_TILE), lambda j, b: (b, j % nf)),
                  own(3), partner(3), own(1), partner(1)],
        out_specs=[own(S), own(3), own(1)],
        out_shape=[jax.ShapeDtypeStruct((T, 2 * D_FF), F32), jax.ShapeDtypeStruct((3, 2 * D_FF), F32),
                   jax.ShapeDtypeStruct((1, 2 * D_FF), F32)],
        compiler_params=_cparams("parallel", "arbitrary"))(u, u, da, cw, cw, cb, cb)


def _sb_scores(qb, kblk, on_diag_mask):
    z = _dot(qb, kblk, 1, 1) * (SB_DIM ** -0.5)
    l1 = jnp.log(1.0 + jnp.exp(-jnp.abs(z)))
    ls = jnp.minimum(z, 0.0) - l1
    lk = jnp.where(on_diag_mask, ls - z, 0.0)
    return ls, lk


def _sb_fwd(proj, S, name):
    T = proj.shape[0]
    BQ = ATT_BLOCK
    nq = S // BQ
    nhp = SB_HEADS // 2

    def body(q_ref, k_ref, v_ref, o_ref, tot_ref):
        row, col = _iota2((BQ, BQ), 0), _iota2((BQ, BQ), 1)
        diag = col < row
        upper = (row > col).astype(BF16)
        for h in range(2):
            sl = slice(h * SB_DIM, (h + 1) * SB_DIM)

            def qloop(iq, carry):
                q0 = pl.multiple_of(iq * BQ, BQ)
                qb = q_ref[pl.ds(q0, BQ), sl]

                def kloop(jj, kc):
                    run, acc = kc
                    k0 = pl.multiple_of((iq - jj) * BQ, BQ)
                    mask = jnp.logical_or(jj > 0, diag)
                    ls, lk = _sb_scores(qb, k_ref[pl.ds(k0, BQ), sl], mask)
                    later = _dot_exact_l(lk, upper)
                    w = jnp.where(mask, jnp.exp(ls + later + run), 0.0)
                    acc = acc + _dot(w, v_ref[pl.ds(k0, BQ), sl], 1, 0)
                    return run + jnp.sum(lk, axis=1, keepdims=True), acc

                run, acc = lax.fori_loop(0, iq + 1, kloop,
                                         (jnp.zeros((BQ, 1), F32), jnp.zeros((BQ, SB_DIM), F32)))
                o_ref[pl.ds(q0, BQ), sl] = acc
                tot_ref[pl.ds(q0, BQ), h:h + 1] = run
                return carry

            lax.fori_loop(0, nq, qloop, 0)

    def spec(off):
        return pl.BlockSpec((S, LANES), lambda b, hp: (b, hp + off))

    return pl.pallas_call(
        body, name=name, grid=(T // S, nhp), in_specs=[spec(0), spec(nhp), spec(2 * nhp)],
        out_specs=[spec(0), pl.BlockSpec((None, S, 2), lambda b, hp: (hp, b, 0))],
        out_shape=[jax.ShapeDtypeStruct((T, 2 * SB_HEADS * SB_DIM), F32), jax.ShapeDtypeStruct((nhp, T, 2), F32)],
        compiler_params=_cparams("parallel", "parallel"))(proj, proj, proj)


def _sb_bwd(proj, tot, dcat, S, name):
    T, width = proj.shape
    BQ = ATT_BLOCK
    nq = S // BQ
    nhp = SB_HEADS // 2
    scale = SB_DIM ** -0.5

    def body(q_ref, k_ref, v_ref, tot_ref, do_ref, dp_hbm, dq_s, dk_s, dv_s, sems):
        b, hp = pl.program_id(0), pl.program_id(1)
        row, col = _iota2((BQ, BQ), 0), _iota2((BQ, BQ), 1)
        diag = col < row
        upto = (row <= col).astype(BF16)
        earlier = (row < col).astype(BF16)
        dk_s[...] = jnp.zeros_like(dk_s)
        dv_s[...] = jnp.zeros_like(dv_s)
        for h in range(2):
            sl = slice(h * SB_DIM, (h + 1) * SB_DIM)

            def qloop(iq, carry):
                q0 = pl.multiple_of(iq * BQ, BQ)
                rows = pl.ds(q0, BQ)
                qb = q_ref[rows, sl]
                dob = do_ref[rows, sl]
                total = tot_ref[rows, h:h + 1]

                def kloop(kb, kc):
                    run, grun, dq = kc
                    k0 = pl.multiple_of(kb * BQ, BQ)
                    krows = pl.ds(k0, BQ)
                    mask = jnp.logical_or(kb < iq, diag)
                    kblk = k_ref[krows, sl]
                    vblk = v_ref[krows, sl]
                    ls, lk = _sb_scores(qb, kblk, mask)
                    later = total - (_dot_exact_l(lk, upto) + run)
                    w = jnp.where(mask, jnp.exp(ls + later), 0.0)
                    g = w * _dot(dob, vblk, 1, 1)
                    before = _dot_exact_l(g, earlier) + grun
                    beta = jnp.exp(ls)
                    dz = jnp.where(mask, g * (1.0 - beta) - beta * before, 0.0) * scale
                    dv_s[krows, sl] += _dot(w, dob, 0, 0)
                    dk_s[krows, sl] += _dot(dz, qb, 0, 0)
                    dq = dq + _dot(dz, kblk, 1, 0)
                    return (run + jnp.sum(lk, axis=1, keepdims=True),
                            grun + jnp.sum(g, axis=1, keepdims=True), dq)

                zero = jnp.zeros((BQ, 1), F32)
                _, _, dq = lax.fori_loop(0, iq + 1, kloop, (zero, zero, jnp.zeros((BQ, SB_DIM), F32)))
                dq_s[rows, sl] = dq
                return carry

            lax.fori_loop(0, nq, qloop, 0)
        r0 = pl.multiple_of(b * S, S)
        copies = []
        for n, buf in enumerate((dq_s, dk_s, dv_s)):
            c0 = pl.multiple_of((hp + n * nhp) * LANES, LANES)
            copies.append(pltpu.make_async_copy(buf, dp_hbm.at[pl.ds(r0, S), pl.ds(c0, LANES)], sems.at[n]))
        for cp in copies:
            cp.start()
        for cp in copies:
            cp.wait()

    def spec(off):
        return pl.BlockSpec((S, LANES), lambda b, hp: (b, hp + off))

    return pl.pallas_call(
        body, name=name, grid=(T // S, nhp),
        in_specs=[spec(0), spec(nhp), spec(2 * nhp), pl.BlockSpec((None, S, 2), lambda b, hp: (hp, b, 0)), spec(0)],
        out_specs=ANY, out_shape=jax.ShapeDtypeStruct((T, width), F32),
        scratch_shapes=[pltpu.VMEM((S, LANES), F32)] * 3 + [pltpu.SemaphoreType.DMA((3,))],
        compiler_params=_cparams("arbitrary", "arbitrary"))(proj, proj, proj, tot, dcat)


HG_COL0 = 3 * SB_HEADS * SB_DIM // LANES


def _hg_gates(q, fp, lbv):
    sg = _sig(fp)
    f = lbv + (1.0 - lbv) * sg
    kk = (1.0 - lbv) * _sig(-fp)
    sq = _sig(q)
    return sg, f, kk, sq


def _hg_chunk(qs, kk, lf, incl):
    C = HG_CHUNK
    b = _dot_exact_r(incl, lf)
    bl = b[C - 1:C, :]
    bm = b[C // 2 - 1:C // 2, :]
    e_t = jnp.exp(b - bm)
    e_s = jnp.exp(bm - b)
    e_i = jnp.exp(b)
    e_e = jnp.exp(bl - b)
    return bl, e_t, e_s, e_i, e_e


def _hgrn_fwd(proj, cat, lb, hgn, S, name):
    T = proj.shape[0]
    B = T // S
    C = HG_CHUNK
    NC = S // C

    def body(q_ref, f_ref, i_ref, g_ref, lb_ref, hgn_ref, cat_hbm, ob_ref, oraw_ref, st_ref, state):
        del cat_hbm
        state[...] = jnp.zeros_like(state)
        row, col = _iota2((C, C), 0), _iota2((C, C), 1)
        causal = row >= col
        incl = causal.astype(BF16)
        lbv = lb_ref[...]

        def chunk(c, carry):
            rows = pl.ds(pl.multiple_of(c * C, C), C)
            q, iv, gv = q_ref[rows, :], i_ref[rows, :], g_ref[rows, :]
            _, f, kk, sq = _hg_gates(q, f_ref[rows, :], lbv)
            qs = q * sq
            bl, e_t, e_s, e_i, e_e = _hg_chunk(qs, kk, jnp.log(f), incl)
            p = jnp.where(causal, _dot(qs * e_t, kk * e_s, 1, 1), 0.0)
            st = state[...]
            st_ref[0, 0, c] = st
            o = _dot(qs * e_i, st, 1, 1) + _dot(p, iv, 1, 0)
            state[...] = st * jnp.exp(bl) + _dot(iv, kk * e_e, 0, 0)
            oraw_ref[rows, :] = o
            r = lax.rsqrt(jnp.mean(o * o, axis=1, keepdims=True) + EPS)
            ob_ref[rows, :] = o * r * hgn_ref[...] * (gv * _sig(gv))
            return carry

        lax.fori_loop(0, NC, chunk, 0)

    def spec(off):
        return pl.BlockSpec((S, LANES), lambda b, h: (b, h + off))

    return pl.pallas_call(
        body, name=name, grid=(B, HG_HEADS),
        in_specs=[spec(HG_COL0), spec(HG_COL0 + 4), spec(HG_COL0 + 8), spec(HG_COL0 + 12),
                  pl.BlockSpec((1, LANES), lambda b, h: (0, h)), pl.BlockSpec((1, LANES), lambda b, h: (0, 0)), ANY],
        out_specs=[spec(4), spec(0), pl.BlockSpec((1, 1, NC, LANES, LANES), lambda b, h: (b, h, 0, 0, 0))],
        out_shape=[jax.ShapeDtypeStruct(cat.shape, F32), jax.ShapeDtypeStruct((T, HG_HEADS * LANES), F32),
                   jax.ShapeDtypeStruct((B, HG_HEADS, NC, LANES, LANES), F32)],
        scratch_shapes=[pltpu.VMEM((LANES, LANES), F32)],
        input_output_aliases={6: 0},
        compiler_params=_cparams("parallel", "parallel"))(proj, proj, proj, proj, lb, hgn, cat)


def _hgrn_bwd(proj, oraw, dcat, states, lb, hgn, dproj, S, name):
    T = proj.shape[0]
    B = T // S
    C = HG_CHUNK
    NC = S // C

    def body(q_ref, f_ref, i_ref, g_ref, oraw_ref, dy_ref, st_ref, lb_ref, hgn_ref, dp_in,
             dp_hbm, dlb_ref, dhgn_ref, dstate, dq_s, df_s, di_s, dg_s, sems):
        del dp_in
        h, b = pl.program_id(0), pl.program_id(1)
        row, col = _iota2((C, C), 0), _iota2((C, C), 1)
        causal = row >= col
        incl = causal.astype(BF16)
        last_row = _iota2((C, LANES), 0) == C - 1
        lbv = lb_ref[...]
        hg = hgn_ref[...]
        dstate[...] = jnp.zeros_like(dstate)

        @pl.when(b == 0)
        def _():
            dlb_ref[...] = jnp.zeros_like(dlb_ref)

        @pl.when(jnp.logical_and(b == 0, h == 0))
        def _():
            dhgn_ref[...] = jnp.zeros_like(dhgn_ref)

        def chunk(cc, carry):
            c = NC - 1 - cc
            rows = pl.ds(pl.multiple_of(c * C, C), C)
            q, fp, iv, gv = q_ref[rows, :], f_ref[rows, :], i_ref[rows, :], g_ref[rows, :]
            o = oraw_ref[rows, :]
            dy = dy_ref[rows, :]
            r = lax.rsqrt(jnp.mean(o * o, axis=1, keepdims=True) + EPS)
            on = o * r
            sgv = _sig(gv)
            silu_g = gv * sgv
            dg_s[rows, :] = dy * on * hg * (sgv * (1.0 + gv * (1.0 - sgv)))
            dhgn_ref[...] += jnp.sum(dy * on * silu_g, axis=0, keepdims=True)
            dn = dy * hg * silu_g
            do = r * dn - o * (r * r * r * jnp.mean(o * dn, axis=1, keepdims=True))
            sg, f, kk, sq = _hg_gates(q, fp, lbv)
            qs = q * sq
            bl, e_t, e_s, e_i, e_e = _hg_chunk(qs, kk, jnp.log(f), incl)
            qd, kd, qi, ke = qs * e_t, kk * e_s, qs * e_i, kk * e_e
            p = jnp.where(causal, _dot(qd, kd, 1, 1), 0.0)
            st = st_ref[0, 0, c]
            dst = dstate[...]
            ebl = jnp.exp(bl)
            dqi = _dot(do, st, 1, 0)
            dp = jnp.where(causal, _dot(do, iv, 1, 1), 0.0)
            di_s[rows, :] = _dot(p, do, 0, 0) + _dot(ke, dst, 1, 1)
            dqd = _dot(dp, kd, 1, 0)
            dkd = _dot(dp, qd, 0, 0)
            dke = _dot(iv, dst, 1, 0)
            dbl = jnp.sum(st * dst, axis=0, keepdims=True) * ebl + jnp.sum(dke * ke, axis=0, keepdims=True)
            db = dqd * qd - dkd * kd + dqi * qi - dke * ke + jnp.where(last_row, dbl, 0.0)
            dqs = dqd * e_t + dqi * e_i
            dkk = dkd * e_s + dke * e_e
            dlf = _dot_exact_r(incl, db, cm=0)
            dstate[...] = _dot(do, qi, 0, 0) + dst * ebl
            oms = 1.0 - sg
            dfd = dlf / f
            df_s[rows, :] = (dfd - dkk) * (1.0 - lbv) * sg * oms
            dq_s[rows, :] = dqs * (sq * (1.0 + q * (1.0 - sq)))
            dlb_ref[...] += jnp.sum((dfd - dkk) * oms, axis=0, keepdims=True)
            return carry

        lax.fori_loop(0, NC, chunk, 0)
        r0 = pl.multiple_of(b * S, S)
        copies = []
        for n, buf in enumerate((dq_s, df_s, di_s, dg_s)):
            c0 = pl.multiple_of((HG_COL0 + 4 * n + h) * LANES, LANES)
            copies.append(pltpu.make_async_copy(buf, dp_hbm.at[pl.ds(r0, S), pl.ds(c0, LANES)], sems.at[n]))
        for cp in copies:
            cp.start()
        for cp in copies:
            cp.wait()

    def spec(off):
        return pl.BlockSpec((S, LANES), lambda h, b: (b, h + off))

    return pl.pallas_call(
        body, name=name, grid=(HG_HEADS, B),
        in_specs=[spec(HG_COL0), spec(HG_COL0 + 4), spec(HG_COL0 + 8), spec(HG_COL0 + 12), spec(0), spec(4),
                  pl.BlockSpec((1, 1, NC, LANES, LANES), lambda h, b: (b, h, 0, 0, 0)),
                  pl.BlockSpec((1, LANES), lambda h, b: (0, h)), pl.BlockSpec((1, LANES), lambda h, b: (0, 0)), ANY],
        out_specs=[ANY, pl.BlockSpec((1, LANES), lambda h, b: (0, h)), pl.BlockSpec((1, LANES), lambda h, b: (0, 0))],
        out_shape=[jax.ShapeDtypeStruct(dproj.shape, F32), jax.ShapeDtypeStruct((1, HG_HEADS * LANES), F32),
                   jax.ShapeDtypeStruct((1, LANES), F32)],
        scratch_shapes=[pltpu.VMEM((LANES, LANES), F32)] + [pltpu.VMEM((S, LANES), F32)] * 4
        + [pltpu.SemaphoreType.DMA((4,))],
        input_output_aliases={9: 0},
        compiler_params=_cparams("arbitrary", "arbitrary"))(proj, proj, proj, proj, oraw, dcat, states, lb, hgn, dproj)


def _lower_bound_fwd(logits, name):
    assert logits.shape[0] == 2

    def body(l_ref, o_ref):
        l0, l1 = l_ref[0:1, :], l_ref[1:2, :]
        m = jnp.maximum(l0, l1)
        e0, e1 = jnp.exp(l0 - m), jnp.exp(l1 - m)
        o_ref[0:1, :] = jnp.zeros_like(l0)
        o_ref[1:2, :] = e1 / (e0 + e1)

    return pl.pallas_call(body, name=name, out_shape=jax.ShapeDtypeStruct(logits.shape, F32))(logits)


def _lower_bound_bwd(logits, dlb, name):
    def body(l_ref, d_ref, o_ref):
        l0, l1 = l_ref[0:1, :], l_ref[1:2, :]
        m = jnp.maximum(l0, l1)
        e0, e1 = jnp.exp(l0 - m), jnp.exp(l1 - m)
        s1 = e1 / (e0 + e1)
        t = s1 * (1.0 - s1) * d_ref[1:2, :]
        o_ref[0:1, :] = -t
        o_ref[1:2, :] = t

    return pl.pallas_call(body, name=name, out_shape=jax.ShapeDtypeStruct(logits.shape, F32))(logits, dlb)


def _bucket_thresholds():
    dist = np.arange(WINDOW)
    max_exact = N_BUCKETS // 2
    large = max_exact + (np.log(np.maximum(dist, max_exact) / max_exact) / math.log(MAX_DISTANCE / max_exact)
                         * (N_BUCKETS - max_exact)).astype(np.int32)
    bucket = np.where(dist < max_exact, dist, np.minimum(large, N_BUCKETS - 1))
    assert np.all(np.diff(bucket) >= 0)
    return [int(np.argmax(bucket >= k)) if np.any(bucket >= k) else 10 ** 6 for k in range(1, N_BUCKETS)]


def _band_bucket():
    dist = _iota2((WINDOW, 2 * WINDOW), 0) + WINDOW - _iota2((WINDOW, 2 * WINDOW), 1)
    bucket = jnp.zeros((WINDOW, 2 * WINDOW), jnp.int32)
    for thr in _bucket_thresholds():
        bucket = bucket + (dist >= thr).astype(jnp.int32)
    band = jnp.logical_and(dist >= 0, dist < WINDOW)
    return bucket, band


def _bias_build(rel_bias, name):
    def body(rb_ref, o_ref):
        h = pl.program_id(0)
        bucket, _ = _band_bucket()
        bias = jnp.zeros((WINDOW, 2 * WINDOW), F32)
        for k in range(N_BUCKETS):
            bias = jnp.where(bucket == k, rb_ref[k, h], bias)
        o_ref[0] = bias

    return pl.pallas_call(
        body, name=name, grid=(SW_HEADS,), in_specs=[pl.BlockSpec(memory_space=pltpu.SMEM)],
        out_specs=pl.BlockSpec((1, WINDOW, 2 * WINDOW), lambda h: (h, 0, 0)),
        out_shape=jax.ShapeDtypeStruct((SW_HEADS, WINDOW, 2 * WINDOW), F32),
        compiler_params=_cparams("parallel"))(rel_bias)


def _bias_reduce(dbias, name):
    def body(d_ref, o_ref):
        bucket, band = _band_bucket()
        d = jnp.where(band, d_ref[0], 0.0)
        lane = _iota2((1, LANES), 1)
        out = jnp.zeros((1, LANES), F32)
        for k in range(N_BUCKETS):
            out = jnp.where(lane == k, jnp.sum(jnp.where(bucket == k, d, 0.0)), out)
        o_ref[0] = out

    return pl.pallas_call(
        body, name=name, grid=(SW_HEADS,), in_specs=[pl.BlockSpec((1, WINDOW, 2 * WINDOW), lambda h: (h, 0, 0))],
        out_specs=pl.BlockSpec((1, 1, LANES), lambda h: (h, 0, 0)),
        out_shape=jax.ShapeDtypeStruct((SW_HEADS, 1, LANES), F32), compiler_params=_cparams("parallel"))(dbias)


SW_Q_COLS = SW_HEADS * SW_DIM
SW_K_BLOCK0 = SW_Q_COLS // LANES
SW_V_BLOCK0 = SW_K_BLOCK0 + SW_KV * SW_DIM // LANES
SW_STEP_HEADS = 8


def _swa_probs(qb, kprev, kcur, bias_ref, hl, sink, mprev, mcur):
    scale = SW_DIM ** -0.5
    lp = jnp.where(mprev, _dot(qb, kprev, 1, 1) * scale + bias_ref[hl, :, 0:WINDOW], NEG)
    lc = jnp.where(mcur, _dot(qb, kcur, 1, 1) * scale + bias_ref[hl, :, WINDOW:2 * WINDOW], NEG)
    m = jnp.maximum(jnp.maximum(jnp.max(lp, axis=1, keepdims=True), jnp.max(lc, axis=1, keepdims=True)), sink)
    ep, ec = jnp.exp(lp - m), jnp.exp(lc - m)
    es = jnp.exp(sink - m)
    den = jnp.sum(ep, axis=1, keepdims=True) + jnp.sum(ec, axis=1, keepdims=True) + es
    return ep, ec, es, den


def _swa_fwd(qkn, proj, bias, sinks, S, name):
    T = qkn.shape[0]
    W = WINDOW
    nb = S // W

    def body(q_ref, k_ref, v_ref, bias_ref, sink_ref, o_ref):
        kp = pl.program_id(1)
        row, col = _iota2((W, W), 0), _iota2((W, W), 1)
        mcur = col <= row
        above = col > row

        def blk(n, carry):
            rows = pl.ds(pl.multiple_of(n * W, W), W)
            prow = pl.ds(pl.multiple_of(jnp.maximum(n - 1, 0) * W, W), W)
            mprev = jnp.logical_and(above, n > 0)
            for kvh in range(2):
                ksl = slice(kvh * SW_DIM, (kvh + 1) * SW_DIM)
                kcur, kprev = k_ref[rows, ksl], k_ref[prow, ksl]
                vcur, vprev = v_ref[rows, ksl], v_ref[prow, ksl]
                for g in range(4):
                    hl = kvh * 4 + g
                    qsl = slice(hl * SW_DIM, (hl + 1) * SW_DIM)
                    sink = sink_ref[kp * SW_STEP_HEADS + hl]
                    ep, ec, _, den = _swa_probs(q_ref[rows, qsl], kprev, kcur, bias_ref, hl, sink, mprev, mcur)
                    o_ref[rows, qsl] = (_dot(ep, vprev, 1, 0) + _dot(ec, vcur, 1, 0)) / den
            return carry

        lax.fori_loop(0, nb, blk, 0)

    return pl.pallas_call(
        body, name=name, grid=(T // S, 2),
        in_specs=[pl.BlockSpec((S, 4 * LANES), lambda b, kp: (b, kp)),
                  pl.BlockSpec((S, LANES), lambda b, kp: (b, SW_K_BLOCK0 + kp)),
                  pl.BlockSpec((S, LANES), lambda b, kp: (b, SW_V_BLOCK0 + kp)),
                  pl.BlockSpec((SW_STEP_HEADS, W, 2 * W), lambda b, kp: (kp, 0, 0)),
                  pl.BlockSpec(memory_space=pltpu.SMEM)],
        out_specs=pl.BlockSpec((S, 4 * LANES), lambda b, kp: (b, kp)),
        out_shape=jax.ShapeDtypeStruct((T, SW_Q_COLS), F32),
        compiler_params=_cparams("parallel", "parallel"))(qkn, qkn, proj, bias, sinks)


def _swa_bwd(qkn, proj, bias, sinks, do, S, name):
    T, width = proj.shape
    W = WINDOW
    nb = S // W
    scale = SW_DIM ** -0.5

    def body(q_ref, k_ref, v_ref, bias_ref, sink_ref, do_ref, dp_hbm, dbias_ref, dsink_ref,
             dq_s, dk_s, dv_s, sems):
        kp, b = pl.program_id(0), pl.program_id(1)
        row, col = _iota2((W, W), 0), _iota2((W, W), 1)
        mcur = col <= row
        above = col > row
        dk_s[...] = jnp.zeros_like(dk_s)
        dv_s[...] = jnp.zeros_like(dv_s)

        @pl.when(b == 0)
        def _():
            dbias_ref[...] = jnp.zeros_like(dbias_ref)
            dsink_ref[...] = jnp.zeros_like(dsink_ref)

        def blk(n, carry):
            rows = pl.ds(pl.multiple_of(n * W, W), W)
            prow = pl.ds(pl.multiple_of(jnp.maximum(n - 1, 0) * W, W), W)
            mprev = jnp.logical_and(above, n > 0)
            for kvh in range(2):
                ksl = slice(kvh * SW_DIM, (kvh + 1) * SW_DIM)
                kcur, kprev = k_ref[rows, ksl], k_ref[prow, ksl]
                vcur, vprev = v_ref[rows, ksl], v_ref[prow, ksl]
                for g in range(4):
                    hl = kvh * 4 + g
                    qsl = slice(hl * SW_DIM, (hl + 1) * SW_DIM)
                    sink = sink_ref[kp * SW_STEP_HEADS + hl]
                    qb = q_ref[rows, qsl]
                    ep, ec, es, den = _swa_probs(qb, kprev, kcur, bias_ref, hl, sink, mprev, mcur)
                    inv = 1.0 / den
                    pp, pc = ep * inv, ec * inv
                    dob = do_ref[rows, qsl]
                    dpp, dpc = _dot(dob, vprev, 1, 1), _dot(dob, vcur, 1, 1)
                    total = jnp.sum(pp * dpp, axis=1, keepdims=True) + jnp.sum(pc * dpc, axis=1, keepdims=True)
                    dlp = pp * (dpp - total)
                    dlc = pc * (dpc - total)
                    dsink_ref[hl:hl + 1, :] += jnp.zeros((1, LANES), F32) - jnp.sum(es * inv * total)
                    dbias_ref[hl, :, 0:W] += dlp
                    dbias_ref[hl, :, W:2 * W] += dlc
                    dq_s[rows, qsl] = (_dot(dlp, kprev, 1, 0) + _dot(dlc, kcur, 1, 0)) * scale
                    dk_s[prow, ksl] += _dot(dlp, qb, 0, 0) * scale
                    dk_s[rows, ksl] += _dot(dlc, qb, 0, 0) * scale
                    dv_s[prow, ksl] += _dot(pp, dob, 0, 0)
                    dv_s[rows, ksl] += _dot(pc, dob, 0, 0)
            return carry

        lax.fori_loop(0, nb, blk, 0)
        r0 = pl.multiple_of(b * S, S)
        cq = pl.multiple_of(kp * 4 * LANES, LANES)
        ck = pl.multiple_of((SW_K_BLOCK0 + kp) * LANES, LANES)
        cv = pl.multiple_of((SW_V_BLOCK0 + kp) * LANES, LANES)
        copies = [pltpu.make_async_copy(dq_s, dp_hbm.at[pl.ds(r0, S), pl.ds(cq, 4 * LANES)], sems.at[0]),
                  pltpu.make_async_copy(dk_s, dp_hbm.at[pl.ds(r0, S), pl.ds(ck, LANES)], sems.at[1]),
                  pltpu.make_async_copy(dv_s, dp_hbm.at[pl.ds(r0, S), pl.ds(cv, LANES)], sems.at[2])]
        for cp in copies:
            cp.start()
        for cp in copies:
            cp.wait()

    qspec = pl.BlockSpec((S, 4 * LANES), lambda kp, b: (b, kp))
    return pl.pallas_call(
        body, name=name, grid=(2, T // S),
        in_specs=[qspec, pl.BlockSpec((S, LANES), lambda kp, b: (b, SW_K_BLOCK0 + kp)),
                  pl.BlockSpec((S, LANES), lambda kp, b: (b, SW_V_BLOCK0 + kp)),
                  pl.BlockSpec((SW_STEP_HEADS, W, 2 * W), lambda kp, b: (kp, 0, 0)),
                  pl.BlockSpec(memory_space=pltpu.SMEM), qspec],
        out_specs=[ANY, pl.BlockSpec((SW_STEP_HEADS, W, 2 * W), lambda kp, b: (kp, 0, 0)),
                   pl.BlockSpec((SW_STEP_HEADS, LANES), lambda kp, b: (kp, 0))],
        out_shape=[jax.ShapeDtypeStruct((T, width), F32), jax.ShapeDtypeStruct((SW_HEADS, W, 2 * W), F32),
                   jax.ShapeDtypeStruct((SW_HEADS, LANES), F32)],
        scratch_shapes=[pltpu.VMEM((S, 4 * LANES), F32), pltpu.VMEM((S, LANES), F32), pltpu.VMEM((S, LANES), F32),
                        pltpu.SemaphoreType.DMA((3,))],
        compiler_params=_cparams("arbitrary", "arbitrary"))(qkn, qkn, proj, bias, sinks, do)


CHIP_FLIPS = ((1, 0), (0, 1), (1, 1))


def _chip_slice(ref, axis, j, size):
    if axis == 1:
        return ref.at[:, pl.ds(j * size, size), :]
    return ref.at[:, :, pl.ds(j * size, size)]


def _flip(v, f):
    return 1 - v if f else v


def _gather_weights(shards, axes, name):
    n = len(shards)
    sizes = [s.shape[a] for s, a in zip(shards, axes)]
    out_shapes = [jax.ShapeDtypeStruct(tuple(4 * d if i == a else d for i, d in enumerate(s.shape)), s.dtype)
                  for s, a in zip(shards, axes)]

    def body(*refs):
        ins, outs = refs[:n], refs[n:2 * n]
        send, recv, loc = refs[2 * n:]
        x, y, c = lax.axis_index("x"), lax.axis_index("y"), lax.axis_index("c")
        j = 2 * x + y
        for jj in range(4):
            @pl.when(j == jj)
            def _():
                for t in range(n):
                    dst = _chip_slice(outs[t], axes[t], jj, sizes[t])
                    pltpu.make_async_copy(ins[t], dst, loc.at[t]).start()
                    for r, (fx, fy) in enumerate(CHIP_FLIPS):
                        pltpu.make_async_remote_copy(ins[t], dst, send.at[t, r], recv.at[t, r],
                                                     device_id=(_flip(x, fx), _flip(y, fy), c),
                                                     device_id_type=MESH).start()
        for t in range(n):
            dst = _chip_slice(outs[t], axes[t], 0, sizes[t])
            pltpu.make_async_copy(ins[t], dst, loc.at[t]).wait()
            for r in range(3):
                pltpu.make_async_remote_copy(ins[t], dst, send.at[t, r], recv.at[t, r],
                                             device_id=(x, y, c), device_id_type=MESH).wait()

    return pl.pallas_call(
        body, name=name, in_specs=[ANY] * n, out_specs=[ANY] * n, out_shape=out_shapes,
        scratch_shapes=[pltpu.SemaphoreType.DMA((n, 3)), pltpu.SemaphoreType.DMA((n, 3)),
                        pltpu.SemaphoreType.DMA((n,))])(*shards)


def _scatter_grads(grads, axes, name):
    n = len(grads)
    sizes = [g.shape[a] // 4 for g, a in zip(grads, axes)]
    out_shapes = [jax.ShapeDtypeStruct((4,) + tuple(sizes[t] if i == axes[t] else d for i, d in enumerate(g.shape)),
                                       g.dtype) for t, g in enumerate(grads)]

    def body(*refs):
        ins, outs = refs[:n], refs[n:2 * n]
        send, recv, loc = refs[2 * n:]
        x, y, c = lax.axis_index("x"), lax.axis_index("y"), lax.axis_index("c")
        j = 2 * x + y
        for jj in range(4):
            @pl.when(j == jj)
            def _():
                for t in range(n):
                    pltpu.make_async_copy(_chip_slice(ins[t], axes[t], jj, sizes[t]), outs[t].at[3], loc.at[t]).start()
                    for r, (fx, fy) in enumerate(CHIP_FLIPS):
                        target = jj ^ (2 * fx + fy)
                        pltpu.make_async_remote_copy(_chip_slice(ins[t], axes[t], target, sizes[t]), outs[t].at[r],
                                                     send.at[t, r], recv.at[t, r],
                                                     device_id=(_flip(x, fx), _flip(y, fy), c),
                                                     device_id_type=MESH).start()
        for t in range(n):
            src = _chip_slice(ins[t], axes[t], 0, sizes[t])
            pltpu.make_async_copy(src, outs[t].at[3], loc.at[t]).wait()
            for r in range(3):
                pltpu.make_async_remote_copy(src, outs[t].at[r], send.at[t, r], recv.at[t, r],
                                             device_id=(x, y, c), device_id_type=MESH).wait()

    return pl.pallas_call(
        body, name=name, in_specs=[ANY] * n, out_specs=[ANY] * n, out_shape=out_shapes,
        scratch_shapes=[pltpu.SemaphoreType.DMA((n, 3)), pltpu.SemaphoreType.DMA((n, 3)),
                        pltpu.SemaphoreType.DMA((n,))])(*grads)


def _swap_with_sibling(parts, name):
    n = len(parts)

    def body(*refs):
        ins, outs = refs[:n], refs[n:2 * n]
        send, recv = refs[2 * n:]
        peer = (lax.axis_index("x"), lax.axis_index("y"), 1 - lax.axis_index("c"))
        copies = [pltpu.make_async_remote_copy(ins[t], outs[t], send.at[t], recv.at[t], device_id=peer,
                                               device_id_type=MESH) for t in range(n)]
        for cp in copies:
            cp.start()
        for cp in copies:
            cp.wait()

    return pl.pallas_call(
        body, name=name, in_specs=[ANY] * n, out_specs=[ANY] * n,
        out_shape=[jax.ShapeDtypeStruct(p.shape, p.dtype) for p in parts],
        scratch_shapes=[pltpu.SemaphoreType.DMA((n,)), pltpu.SemaphoreType.DMA((n,))])(*parts)


def _allreduce_small(v, name):
    R = v.shape[0]
    ND = 8

    def body(v_ref, o_ref, buf, send, recv):
        x, y, c = lax.axis_index("x"), lax.axis_index("y"), lax.axis_index("c")
        me = 4 * x + 2 * y + c
        copies = []
        for d in range(1, ND):
            peer = (_flip(x, d >> 2 & 1), _flip(y, d >> 1 & 1), _flip(c, d & 1))
            copies.append(pltpu.make_async_remote_copy(v_ref, buf.at[me], send.at[d], recv.at[me], device_id=peer,
                                                       device_id_type=MESH))
        for cp in copies:
            cp.start()
        buf[pl.ds(me, 1)] = v_ref[...][None]
        for k in range(ND):
            @pl.when(me != k)
            def _():
                pltpu.make_async_remote_copy(v_ref, buf.at[k], send.at[0], recv.at[k], device_id=(x, y, c),
                                             device_id_type=MESH).wait_recv()
        for cp in copies:
            cp.wait_send()
        total = buf[0]
        for k in range(1, ND):
            total = total + buf[k]
        o_ref[...] = total

    vm = pl.BlockSpec(memory_space=pltpu.VMEM)
    return pl.pallas_call(
        body, name=name, in_specs=[vm], out_specs=vm, out_shape=jax.ShapeDtypeStruct((R, LANES), F32),
        scratch_shapes=[pltpu.VMEM((ND, R, LANES), F32), pltpu.SemaphoreType.DMA((ND,)),
                        pltpu.SemaphoreType.DMA((ND,))],
        compiler_params=pltpu.CompilerParams(vmem_limit_bytes=VMEM_LIMIT))(v)


def _tile2(R, Cn):
    tc = _pick(Cn, 2048)
    tr = R
    for cand in (256, 128, 64, 32, 16, 8):
        if R % cand == 0:
            tr = cand
            break
    return tr, tc


def _sum4(stack, name):
    _, R, Cn = stack.shape
    tr, tc = _tile2(R, Cn)

    def body(s_ref, o_ref):
        o_ref[...] = ((s_ref[0].astype(F32) + s_ref[1].astype(F32)) + s_ref[2].astype(F32)) + s_ref[3].astype(F32)

    return pl.pallas_call(
        body, name=name, grid=(R // tr, Cn // tc), in_specs=[pl.BlockSpec((4, tr, tc), lambda i, j: (0, i, j))],
        out_specs=pl.BlockSpec((tr, tc), lambda i, j: (i, j)), out_shape=jax.ShapeDtypeStruct((R, Cn), F32),
        compiler_params=_cparams("parallel", "parallel"))(stack)


def _adamw(w, m, v, g_parts, name):
    R, Cn = w.shape
    tr, tc = _tile2(R, Cn)
    npart = len(g_parts)
    c1 = 1.0 / (1.0 - ADAM_B1 ** ADAM_STEP)
    c2 = 1.0 / (1.0 - ADAM_B2 ** ADAM_STEP)

    def body(*refs):
        w_ref, m_ref, v_ref = refs[:3]
        g_refs = refs[3:3 + npart]
        g_out, d_out, m_out, v_out = refs[3 + npart:]
        g = g_refs[0][...]
        for r in g_refs[1:]:
            g = g + r[...]
        mn = ADAM_B1 * m_ref[...] + (1.0 - ADAM_B1) * g
        vn = ADAM_B2 * v_ref[...] + (1.0 - ADAM_B2) * (g * g)
        g_out[...] = g
        m_out[...] = mn
        v_out[...] = vn
        d_out[...] = -ADAM_LR * ((mn * c1) / (jnp.sqrt(vn * c2) + ADAM_EPS) + ADAM_WD * w_ref[...])

    spec = pl.BlockSpec((tr, tc), lambda i, j: (i, j))
    return pl.pallas_call(
        body, name=name, grid=(R // tr, Cn // tc), in_specs=[spec] * (3 + npart), out_specs=[spec] * 4,
        out_shape=[jax.ShapeDtypeStruct((R, Cn), F32)] * 4,
        compiler_params=_cparams("parallel", "parallel"))(w, m, v, *g_parts)


SHARDED = (("ab_w_in", 2), ("ab_w_out", 1), ("c_w_in", 2), ("c_w_out", 1), ("ffn_up", 2), ("ffn_conv", 2),
           ("ffn_down", 1), ("ple_gate", 1), ("ple_proj", 2))
SMALL = ("mix_norm", "hg_lb_logits", "hg_out_norm", "q_norm", "k_norm", "sinks", "rel_bias", "ffn_norm",
         "ffn_conv_b", "ple_norm")
WEIGHTS = ("mix_norm", "ab_w_in", "hg_lb_logits", "hg_out_norm", "ab_w_out", "c_w_in", "q_norm", "k_norm", "sinks",
           "rel_bias", "c_w_out", "ffn_norm", "ffn_up", "ffn_conv", "ffn_conv_b", "ffn_down", "ple_norm", "ple_gate",
           "ple_proj")
PACK_ALIGN = 8 * LANES


def _pack(arrs):
    pieces = []
    for a in arrs:
        flat = a.reshape(-1)
        pad = -flat.shape[0] % PACK_ALIGN
        pieces.append(jnp.pad(flat, (0, pad)).reshape(-1, LANES))
    return jnp.concatenate(pieces, axis=0)


def _unpack(packed, like):
    out, r = [], 0
    for a in like:
        size = int(np.prod(a.shape))
        rows = (size + PACK_ALIGN - 1) // PACK_ALIGN * 8
        out.append(packed[r:r + rows].reshape(-1)[:size].reshape(a.shape))
        r += rows
    return out


def _forward_backward(x, p, target, W, S):
    T = x.shape[0]
    depth = p.shape[0]
    lb = _lower_bound_fwd(W["hg_lb_logits"], "lower_bound_fwd")
    bias = _bias_build(W["rel_bias"], "bias_build")
    qk_gain = jnp.concatenate([jnp.tile(W["q_norm"], (1, SW_HEADS)), jnp.tile(W["k_norm"], (1, SW_KV))], axis=1)

    saved = []
    h = x
    for i in range(depth):
        j = i // 2
        s = {"h0": h}
        s["hn"] = _rmsnorm_fwd(h, W["mix_norm"][i:i + 1], f"mix_norm_fwd_{i}")
        if i % 2 == 0:
            s["proj"] = _matmul(s["hn"], W["ab_w_in"][j], "nn", f"ab_in_{i}")
            cat, s["sb_tot"] = _sb_fwd(s["proj"], S, f"sb_fwd_{i}")
            s["cat"], s["oraw"], s["states"] = _hgrn_fwd(s["proj"], cat, lb[j:j + 1], W["hg_out_norm"][j:j + 1], S,
                                                         f"hgrn_fwd_{i}")
            h = _matmul(s["cat"], W["ab_w_out"][j], "nn", f"ab_out_{i}", res=h)
        else:
            s["proj"] = _matmul(s["hn"], W["c_w_in"][j], "nn", f"c_in_{i}")
            s["qkn"] = _headnorm_fwd(s["proj"], qk_gain[j:j + 1], f"qk_norm_fwd_{i}")
            s["o"] = _swa_fwd(s["qkn"], s["proj"], bias, W["sinks"][j], S, f"swa_fwd_{i}")
            h = _matmul(s["o"], W["c_w_out"][j], "nn", f"c_out_{i}", res=h)
        s["h1"] = h
        s["hn2"] = _rmsnorm_fwd(h, W["ffn_norm"][i:i + 1], f"ffn_norm_fwd_{i}")
        s["u"] = _matmul(s["hn2"], W["ffn_up"][i], "nn", f"ffn_up_{i}")
        s["a"] = _convglu_fwd(s["u"], W["ffn_conv"][i], W["ffn_conv_b"][i:i + 1], S, f"convglu_fwd_{i}")
        h = _matmul(s["a"], W["ffn_down"][i], "nn", f"ffn_down_{i}", res=h)
        s["h2"] = h
        s["hn3"] = _rmsnorm_fwd(h, W["ple_norm"][i:i + 1], f"ple_norm_fwd_{i}")
        s["z"] = _matmul(s["hn3"], W["ple_gate"][i], "nn", f"ple_gate_{i}")
        s["pp"] = _matmul(p[i], W["ple_proj"][i], "nn", f"ple_proj_{i}")
        h = _ple_fwd(h, s["z"], s["pp"], f"ple_fwd_{i}")
        saved.append(s)

    loss, dh = _loss_fwd_bwd(h, target, "loss")

    G = {k: [None] * depth for k in ("mix_norm", "ffn_norm", "ple_norm", "ffn_up", "ffn_conv", "ffn_conv_b",
                                     "ffn_down", "ple_gate", "ple_proj")}
    for k in ("ab_w_in", "ab_w_out", "c_w_in", "c_w_out", "hg_out_norm", "q_norm", "k_norm", "sinks", "lb"):
        G[k] = [None] * (depth // 2)
    dbias_total = None
    for i in reversed(range(depth)):
        j = i // 2
        s = saved[i]
        dz, dpp = _ple_bwd(dh, s["z"], s["pp"], f"ple_bwd_{i}")
        G["ple_proj"][i] = _matmul(p[i], dpp, "tn", f"d_ple_proj_{i}", out_dtype=BF16)
        G["ple_gate"][i] = _matmul(s["hn3"], dz, "tn", f"d_ple_gate_{i}", out_dtype=BF16)
        dhn = _matmul(dz, W["ple_gate"][i], "nt", f"d_hn3_{i}")
        dh, G["ple_norm"][i] = _rmsnorm_bwd(s["h2"], W["ple_norm"][i:i + 1], dhn, dh, f"ple_norm_bwd_{i}")

        da = _matmul(dh, W["ffn_down"][i], "nt", f"d_a_{i}")
        G["ffn_down"][i] = _matmul(s["a"], dh, "tn", f"d_ffn_down_{i}", out_dtype=BF16)
        du, G["ffn_conv"][i], G["ffn_conv_b"][i] = _convglu_bwd(s["u"], da, W["ffn_conv"][i],
                                                                 W["ffn_conv_b"][i:i + 1], S, f"convglu_bwd_{i}")
        G["ffn_up"][i] = _matmul(s["hn2"], du, "tn", f"d_ffn_up_{i}", out_dtype=BF16)
        dhn = _matmul(du, W["ffn_up"][i], "nt", f"d_hn2_{i}")
        dh, G["ffn_norm"][i] = _rmsnorm_bwd(s["h1"], W["ffn_norm"][i:i + 1], dhn, dh, f"ffn_norm_bwd_{i}")

        if i % 2 == 0:
            dcat = _matmul(dh, W["ab_w_out"][j], "nt", f"d_cat_{i}")
            G["ab_w_out"][j] = _matmul(s["cat"], dh, "tn", f"d_ab_out_{i}", out_dtype=BF16)
            dproj = _sb_bwd(s["proj"], s["sb_tot"], dcat, S, f"sb_bwd_{i}")
            dproj, G["lb"][j], G["hg_out_norm"][j] = _hgrn_bwd(s["proj"], s["oraw"], dcat, s["states"], lb[j:j + 1],
                                                               W["hg_out_norm"][j:j + 1], dproj, S, f"hgrn_bwd_{i}")
            G["ab_w_in"][j] = _matmul(s["hn"], dproj, "tn", f"d_ab_in_{i}", out_dtype=BF16)
            dhn = _matmul(dproj, W["ab_w_in"][j], "nt", f"d_hn_{i}")
        else:
            do = _matmul(dh, W["c_w_out"][j], "nt", f"d_o_{i}")
            G["c_w_out"][j] = _matmul(s["o"], dh, "tn", f"d_c_out_{i}", out_dtype=BF16)
            dqkv, dbias, dsink = _swa_bwd(s["qkn"], s["proj"], bias, W["sinks"][j], do, S, f"swa_bwd_{i}")
            dbias_total = dbias if dbias_total is None else dbias_total + dbias
            G["sinks"][j] = dsink[:, 0]
            dproj, dgain = _headnorm_bwd(s["proj"], qk_gain[j:j + 1], dqkv, f"qk_norm_bwd_{i}")
            G["q_norm"][j] = dgain[0, :SW_Q_COLS].reshape(SW_HEADS, SW_DIM).sum(axis=0)
            G["k_norm"][j] = dgain[0, SW_Q_COLS:].reshape(SW_KV, SW_DIM).sum(axis=0)
            G["c_w_in"][j] = _matmul(s["hn"], dproj, "tn", f"d_c_in_{i}", out_dtype=BF16)
            dhn = _matmul(dproj, W["c_w_in"][j], "nt", f"d_hn_{i}")
        dh, G["mix_norm"][i] = _rmsnorm_bwd(s["h0"], W["mix_norm"][i:i + 1], dhn, dh, f"mix_norm_bwd_{i}")

    grads = {k: jnp.stack(G[k]) for k in ("ab_w_in", "ab_w_out", "c_w_in", "c_w_out", "ffn_up", "ffn_down",
                                            "ple_gate", "ple_proj", "q_norm", "k_norm", "sinks")}
    grads["ffn_conv"] = jnp.stack(G["ffn_conv"])
    for k in ("mix_norm", "ffn_norm", "ple_norm", "ffn_conv_b", "hg_out_norm"):
        grads[k] = jnp.concatenate(G[k], axis=0)
    grads["hg_lb_logits"] = _lower_bound_bwd(W["hg_lb_logits"], jnp.concatenate(G["lb"], axis=0), "lower_bound_bwd")
    grads["rel_bias"] = _bias_reduce(dbias_total, "bias_reduce")[:, 0, :N_BUCKETS].T
    return loss, dh, grads


def kernel(x, p, mix_norm, ab_w_in, hg_lb_logits, hg_out_norm, ab_w_out, c_w_in, q_norm, k_norm, sinks, rel_bias, c_w_out, ffn_norm, ffn_up, ffn_conv, ffn_conv_b, ffn_down, ple_norm, ple_gate, ple_proj, loss_target, m_mix_norm, m_ab_w_in, m_hg_lb_logits, m_hg_out_norm, m_ab_w_out, m_c_w_in, m_q_norm, m_k_norm, m_sinks, m_rel_bias, m_c_w_out, m_ffn_norm, m_ffn_up, m_ffn_conv, m_ffn_conv_b, m_ffn_down, m_ple_norm, m_ple_gate, m_ple_proj, v_mix_norm, v_ab_w_in, v_hg_lb_logits, v_hg_out_norm, v_ab_w_out, v_c_w_in, v_q_norm, v_k_norm, v_sinks, v_rel_bias, v_c_w_out, v_ffn_norm, v_ffn_up, v_ffn_conv, v_ffn_conv_b, v_ffn_down, v_ple_norm, v_ple_gate, v_ple_proj):
    args = dict(locals())
    w = {k: args[k] for k in WEIGHTS}
    m = {k: args["m_" + k] for k in WEIGHTS}
    v = {k: args["v_" + k] for k in WEIGHTS}
    B, S, Dm = x.shape
    T = B * S
    names = [k for k, _ in SHARDED]
    axes = [a for _, a in SHARDED]

    wire = {k: (F32 if k == "ffn_conv" else BF16) for k in names}
    full = _gather_weights([w[k].astype(wire[k]) for k in names], axes, "gather_weights")
    W = dict(zip(names, full))
    for k in SMALL:
        W[k] = w[k]

    loss, dx, grads = _forward_backward(x.reshape(T, Dm), p.reshape(p.shape[0], T, p.shape[-1]),
                                        loss_target.reshape(T, Dm), W, S)
    loss = lax.psum(loss[0, 0], ("x", "y", "c"))

    stacks = _scatter_grads([grads[k] for k in names], axes, "scatter_grads")
    partial = [_sum4(st.reshape(4, -1, st.shape[-1]), f"sum_chips_{k}") for k, st in zip(names, stacks)]
    other = _swap_with_sibling(partial, "swap_core_sums")
    small_sum = _allreduce_small(_pack([grads[k] for k in SMALL]), "allreduce_small")

    out_g, out_d, out_m, out_v = {}, {}, {}, {}
    for k, mine, theirs in zip(names, partial, other):
        shp = w[k].shape
        r = [a.reshape(shp) for a in _adamw(w[k].reshape(mine.shape), m[k].reshape(mine.shape),
                                            v[k].reshape(mine.shape), [mine, theirs], f"adamw_{k}")]
        out_g[k], out_d[k], out_m[k], out_v[k] = r
    sm = _adamw(_pack([w[k] for k in SMALL]), _pack([m[k] for k in SMALL]), _pack([v[k] for k in SMALL]),
                [small_sum], "adamw_small")
    like = [w[k] for k in SMALL]
    for dst, packed in zip((out_g, out_d, out_m, out_v), sm):
        for k, a in zip(SMALL, _unpack(packed, like)):
            dst[k] = a

    return (loss, dx.reshape(B, S, Dm), *[out_g[k] for k in WEIGHTS], *[out_d[k] for k in WEIGHTS],
            *[out_m[k] for k in WEIGHTS], *[out_v[k] for k in WEIGHTS])
```

```python
import math

import numpy as np
import jax
import jax.numpy as jnp
from jax import lax
from jax.experimental import pallas as pl
from jax.experimental.pallas import tpu as pltpu

F32 = jnp.float32
BF16 = jnp.bfloat16
MESH = pl.DeviceIdType.MESH
ANY = pl.BlockSpec(memory_space=pl.ANY)

D_MODEL = 1024
EPS = 1e-6
SB_HEADS, SB_DIM = 8, 64
HG_HEADS, HG_DK = 4, 128
HG_CHUNK = 32
SW_HEADS, SW_KV, SW_DIM, WINDOW = 16, 4, 64, 128
N_BUCKETS, MAX_DISTANCE = 32, 128
D_FF = 2816
ATT_BLOCK = 128
SB_QBLOCK = 256
LANES = 128
NEG = -1e30

ADAM_LR, ADAM_B1, ADAM_B2, ADAM_EPS, ADAM_WD, ADAM_STEP = 0.001, 0.9, 0.999, 1e-08, 0.01, 10

VMEM_LIMIT = 56 * 1024 * 1024


def _cparams(*sem):
    return pltpu.CompilerParams(dimension_semantics=sem, vmem_limit_bytes=VMEM_LIMIT)


def _pick(n, cap):
    if n <= cap:
        return n
    best = None
    for d in range(LANES, cap + 1, LANES):
        if n % d == 0:
            best = d
    assert best is not None, (n, cap)
    return best


def _dot(a, b, ca, cb):
    return lax.dot_general(a.astype(BF16), b.astype(BF16), (((ca,), (cb,)), ((), ())),
                           preferred_element_type=F32)


def _split(x, terms):
    parts = []
    for _ in range(terms):
        hi = x.astype(BF16)
        parts.append(hi)
        x = x - hi.astype(F32)
    return parts


def _dot_exact_l(x, m, terms=2):
    out = None
    for p in _split(x, terms):
        t = lax.dot_general(p, m, (((1,), (0,)), ((), ())), preferred_element_type=F32)
        out = t if out is None else out + t
    return out


def _dot_exact_r(m, x, terms=3, cm=1):
    out = None
    for p in _split(x, terms):
        t = lax.dot_general(m, p, (((cm,), (0,)), ((), ())), preferred_element_type=F32)
        out = t if out is None else out + t
    return out


def _sig(x):
    return 1.0 / (1.0 + jnp.exp(-x))


def _iota2(shape, dim):
    return lax.broadcasted_iota(jnp.int32, shape, dim)


def _matmul(a, b, mode, name, out_dtype=F32, res=None):
    if mode == "nn":
        (M, K), N = a.shape, b.shape[1]
    elif mode == "nt":
        (M, K), N = a.shape, b.shape[0]
    else:
        (K, M), N = a.shape, b.shape[1]
    tm, tn, tk = _pick(M, 1024), _pick(N, 512), _pick(K, 2048)
    nk = K // tk
    if mode == "tn":
        a_spec = pl.BlockSpec((tk, tm), lambda i, j, k: (k, i))
    else:
        a_spec = pl.BlockSpec((tm, tk), lambda i, j, k: (i, k))
    if mode == "nt":
        b_spec = pl.BlockSpec((tn, tk), lambda i, j, k: (j, k))
    else:
        b_spec = pl.BlockSpec((tk, tn), lambda i, j, k: (k, j))
    ca, cb = {"nn": (1, 0), "nt": (1, 1), "tn": (0, 0)}[mode]
    o_spec = pl.BlockSpec((tm, tn), lambda i, j, k: (i, j))

    def body(*refs):
        if res is None:
            a_ref, b_ref, o_ref, acc = refs
        else:
            a_ref, b_ref, r_ref, o_ref, acc = refs
        k = pl.program_id(2)

        @pl.when(k == 0)
        def _():
            acc[...] = jnp.zeros_like(acc)

        acc[...] += _dot(a_ref[...], b_ref[...], ca, cb)

        @pl.when(k == nk - 1)
        def _():
            r = acc[...]
            if res is not None:
                r = r + r_ref[...]
            o_ref[...] = r.astype(out_dtype)

    ins = [a, b] + ([] if res is None else [res])
    in_specs = [a_spec, b_spec] + ([] if res is None else [o_spec])
    return pl.pallas_call(
        body, name=name, grid=(M // tm, N // tn, nk), in_specs=in_specs, out_specs=o_spec,
        out_shape=jax.ShapeDtypeStruct((M, N), out_dtype),
        scratch_shapes=[pltpu.VMEM((tm, tn), F32)],
        compiler_params=_cparams("parallel", "parallel", "arbitrary"))(*ins)


ROW_TILE = 512


def _row_spec(width):
    return pl.BlockSpec((ROW_TILE, width), lambda i: (i, 0))


def _vec_spec(width):
    return pl.BlockSpec((1, width), lambda i: (0, 0))


def _rmsnorm_fwd(h, g, name):
    T, Dm = h.shape

    def body(h_ref, g_ref, o_ref):
        x = h_ref[...]
        r = lax.rsqrt(jnp.mean(x * x, axis=1, keepdims=True) + EPS)
        o_ref[...] = (x * r * g_ref[...]).astype(BF16)

    return pl.pallas_call(
        body, name=name, grid=(T // ROW_TILE,), in_specs=[_row_spec(Dm), _vec_spec(Dm)],
        out_specs=_row_spec(Dm), out_shape=jax.ShapeDtypeStruct((T, Dm), BF16),
        compiler_params=_cparams("parallel"))(h, g)


def _rmsnorm_bwd(h, g, dhn, dres, name):
    T, Dm = h.shape

    def body(h_ref, g_ref, dy_ref, dr_ref, dh_ref, dg_ref):
        i = pl.program_id(0)
        x = h_ref[...]
        dy = dy_ref[...]
        r = lax.rsqrt(jnp.mean(x * x, axis=1, keepdims=True) + EPS)
        gdy = dy * g_ref[...]
        m = jnp.mean(x * gdy, axis=1, keepdims=True)
        dh_ref[...] = dr_ref[...] + r * gdy - x * (r * r * r * m)
        part = jnp.sum(dy * x * r, axis=0, keepdims=True)

        @pl.when(i == 0)
        def _():
            dg_ref[...] = part

        @pl.when(i > 0)
        def _():
            dg_ref[...] += part

    return pl.pallas_call(
        body, name=name, grid=(T // ROW_TILE,),
        in_specs=[_row_spec(Dm), _vec_spec(Dm), _row_spec(Dm), _row_spec(Dm)],
        out_specs=[_row_spec(Dm), _vec_spec(Dm)],
        out_shape=[jax.ShapeDtypeStruct((T, Dm), F32), jax.ShapeDtypeStruct((1, Dm), F32)],
        compiler_params=_cparams("arbitrary"))(h, g, dhn, dres)


def _ple_fwd(h, z, pp, name):
    T, Dm = h.shape

    def body(h_ref, z_ref, p_ref, o_ref):
        o_ref[...] = h_ref[...] + _sig(z_ref[...]) * p_ref[...]

    return pl.pallas_call(
        body, name=name, grid=(T // ROW_TILE,), in_specs=[_row_spec(Dm)] * 3, out_specs=_row_spec(Dm),
        out_shape=jax.ShapeDtypeStruct((T, Dm), F32), compiler_params=_cparams("parallel"))(h, z, pp)


def _ple_bwd(dh, z, pp, name):
    T, Dm = dh.shape

    def body(dh_ref, z_ref, p_ref, dz_ref, dp_ref):
        s = _sig(z_ref[...])
        d = dh_ref[...]
        dz_ref[...] = d * p_ref[...] * s * (1.0 - s)
        dp_ref[...] = d * s

    return pl.pallas_call(
        body, name=name, grid=(T // ROW_TILE,), in_specs=[_row_spec(Dm)] * 3, out_specs=[_row_spec(Dm)] * 2,
        out_shape=[jax.ShapeDtypeStruct((T, Dm), F32)] * 2, compiler_params=_cparams("parallel"))(dh, z, pp)


def _loss_fwd_bwd(y, target, name):
    T, Dm = y.shape

    def body(y_ref, t_ref, l_ref, d_ref):
        i = pl.program_id(0)
        e = y_ref[...] - t_ref[...]
        d_ref[...] = e * (1.0 / Dm)
        part = jnp.full((8, LANES), 0.5 / Dm, F32) * jnp.sum(e * e)

        @pl.when(i == 0)
        def _():
            l_ref[...] = part

        @pl.when(i > 0)
        def _():
            l_ref[...] += part

    return pl.pallas_call(
        body, name=name, grid=(T // ROW_TILE,), in_specs=[_row_spec(Dm)] * 2,
        out_specs=[pl.BlockSpec((8, LANES), lambda i: (0, 0)), _row_spec(Dm)],
        out_shape=[jax.ShapeDtypeStruct((8, LANES), F32), jax.ShapeDtypeStruct((T, Dm), F32)],
        compiler_params=_cparams("arbitrary"))(y, target)


def _head_mean_matrix():
    r = _iota2((LANES, LANES), 0) >= SW_DIM
    c = _iota2((LANES, LANES), 1) >= SW_DIM
    return jnp.where(r == c, 1.0 / SW_DIM, 0.0).astype(BF16)


def _headnorm_fwd(x, g_lane, name):
    T = x.shape[0]
    C = g_lane.shape[1]

    def body(x_ref, g_ref, y_ref):
        xv = x_ref[...]
        ms = _dot_exact_l(xv * xv, _head_mean_matrix())
        y_ref[...] = xv * lax.rsqrt(ms + EPS) * g_ref[...]

    spec = pl.BlockSpec((ROW_TILE, LANES), lambda j, i: (i, j))
    return pl.pallas_call(
        body, name=name, grid=(C // LANES, T // ROW_TILE),
        in_specs=[spec, pl.BlockSpec((1, LANES), lambda j, i: (0, j))], out_specs=spec,
        out_shape=jax.ShapeDtypeStruct((T, C), F32), compiler_params=_cparams("parallel", "parallel"))(x, g_lane)


def _headnorm_bwd(x, g_lane, dy_full, name):
    T = x.shape[0]
    C = g_lane.shape[1]

    def body(x_ref, g_ref, dy_ref, dx_ref, dg_ref):
        i = pl.program_id(1)
        xv = x_ref[...]
        dy = dy_ref[...]
        bd = _head_mean_matrix()
        r = lax.rsqrt(_dot_exact_l(xv * xv, bd) + EPS)
        gdy = dy * g_ref[...]
        m = _dot_exact_l(xv * gdy, bd)
        dx_ref[...] = r * gdy - xv * (r * r * r * m)
        part = jnp.sum(dy * xv * r, axis=0, keepdims=True)

        @pl.when(i == 0)
        def _():
            dg_ref[...] = part

        @pl.when(i > 0)
        def _():
            dg_ref[...] += part

    spec = pl.BlockSpec((ROW_TILE, LANES), lambda j, i: (i, j))
    vspec = pl.BlockSpec((1, LANES), lambda j, i: (0, j))
    return pl.pallas_call(
        body, name=name, grid=(C // LANES, T // ROW_TILE), in_specs=[spec, vspec, spec],
        out_specs=[spec, vspec],
        out_shape=[jax.ShapeDtypeStruct(dy_full.shape, F32), jax.ShapeDtypeStruct((1, C), F32)],
        input_output_aliases={2: 0}, compiler_params=_cparams("parallel", "arbitrary"))(x, g_lane, dy_full)


CONV_TILE = 128


def _shift_down(x, k):
    rows = _iota2(x.shape, 0)
    return jnp.where(rows >= k, pltpu.roll(x, k, 0), 0.0)


def _shift_up(x, k):
    n = x.shape[0]
    rows = _iota2(x.shape, 0)
    return jnp.where(rows < n - k, pltpu.roll(x, n - k, 0), 0.0)


def _conv3(u, w_ref, b_ref):
    return w_ref[2:3, :] * u + w_ref[1:2, :] * _shift_down(u, 1) + w_ref[0:1, :] * _shift_down(u, 2) + b_ref[...]


def _convglu_fwd(u, cw, cb, S, name):
    T = u.shape[0]
    nf = D_FF // CONV_TILE

    def body(ug_ref, uu_ref, wg_ref, wu_ref, bg_ref, bu_ref, a_ref):
        yg = _conv3(ug_ref[...], wg_ref, bg_ref)
        yu = _conv3(uu_ref[...], wu_ref, bu_ref)
        a_ref[...] = (yg * _sig(yg) * yu).astype(BF16)

    def blk(rows, off):
        return pl.BlockSpec((rows, CONV_TILE), (lambda b, j: (b, j + off)) if rows == S else (lambda b, j: (0, j + off)))

    return pl.pallas_call(
        body, name=name, grid=(T // S, nf),
        in_specs=[blk(S, 0), blk(S, nf), blk(3, 0), blk(3, nf), blk(1, 0), blk(1, nf)],
        out_specs=blk(S, 0), out_shape=jax.ShapeDtypeStruct((T, D_FF), BF16),
        compiler_params=_cparams("parallel", "parallel"))(u, u, cw, cw, cb, cb)


def _convglu_bwd(u, da, cw, cb, S, name):
    T = u.shape[0]
    nf = D_FF // CONV_TILE

    def body(uo_ref, up_ref, da_ref, wo_ref, wp_ref, bo_ref, bp_ref, du_ref, dw_ref, db_ref):
        j = pl.program_id(0)
        b = pl.program_id(1)
        uo = uo_ref[...]
        yo = _conv3(uo, wo_ref, bo_ref)
        yp = _conv3(up_ref[...], wp_ref, bp_ref)
        own_is_gate = j < nf
        gate = jnp.where(own_is_gate, yo, yp)
        up = jnp.where(own_is_gate, yp, yo)
        s = _sig(gate)
        da_v = da_ref[...]
        dy = jnp.where(own_is_gate, da_v * up * (s * (1.0 + gate * (1.0 - s))), da_v * gate * s)
        du_ref[...] = wo_ref[2:3, :] * dy + wo_ref[1:2, :] * _shift_up(dy, 1) + wo_ref[0:1, :] * _shift_up(dy, 2)
        dws = [jnp.sum(dy * _shift_down(uo, 2), axis=0, keepdims=True),
               jnp.sum(dy * _shift_down(uo, 1), axis=0, keepdims=True),
               jnp.sum(dy * uo, axis=0, keepdims=True)]
        dbv = jnp.sum(dy, axis=0, keepdims=True)

        @pl.when(b == 0)
        def _():
            for k in range(3):
                dw_ref[k:k + 1, :] = dws[k]
            db_ref[...] = dbv

        @pl.when(b > 0)
        def _():
            for k in range(3):
                dw_ref[k:k + 1, :] += dws[k]
            db_ref[...] += dbv

    def own(rows):
        return pl.BlockSpec((rows, CONV_TILE), (lambda j, b: (b, j)) if rows == S else (lambda j, b: (0, j)))

    def partner(rows):
        return pl.BlockSpec((rows, CONV_TILE), (lambda j, b: (b, (j + nf) % (2 * nf))) if rows == S
                            else (lambda j, b: (0, (j + nf) % (2 * nf))))

    return pl.pallas_call(
        body, name=name, grid=(2 * nf, T // S),
        in_specs=[own(S), partner(S), pl.BlockSpec((S, CONV_TILE), lambda j, b: (b, j % nf)),
                  own(3), partner(3), own(1), partner(1)],
        out_specs=[own(S), own(3), own(1)],
        out_shape=[jax.ShapeDtypeStruct((T, 2 * D_FF), F32), jax.ShapeDtypeStruct((3, 2 * D_FF), F32),
                   jax.ShapeDtypeStruct((1, 2 * D_FF), F32)],
        compiler_params=_cparams("parallel", "arbitrary"))(u, u, da, cw, cw, cb, cb)


def _sb_scores(qb, kblk, on_diag_mask):
    z = _dot(qb, kblk, 1, 1) * (SB_DIM ** -0.5)
    l1 = jnp.log(1.0 + jnp.exp(-jnp.abs(z)))
    ls = jnp.minimum(z, 0.0) - l1
    lk = jnp.where(on_diag_mask, ls - z, 0.0)
    return ls, lk


def _sb_fwd(proj, S, name):
    T = proj.shape[0]
    BQ, BK = SB_QBLOCK, ATT_BLOCK
    nq = S // BQ
    nhp = SB_HEADS // 2
    heads = [slice(h * SB_DIM, (h + 1) * SB_DIM) for h in range(2)]

    def body(q_ref, k_ref, v_ref, o_ref, tot_ref):
        ahead = _iota2((BQ, BK), 1) - _iota2((BQ, BK), 0)
        upper = (_iota2((BK, BK), 0) > _iota2((BK, BK), 1)).astype(BF16)

        def qloop(iq, carry):
            q0 = pl.multiple_of(iq * BQ, BQ)
            rows = pl.ds(q0, BQ)
            qbs = [q_ref[rows, sl] for sl in heads]
            nkb = (iq + 1) * (BQ // BK)

            def kloop(jj, kc):
                k0 = pl.multiple_of((nkb - 1 - jj) * BK, BK)
                krows = pl.ds(k0, BK)
                mask = ahead < q0 - k0
                out = []
                for sl, qb, (run, acc) in zip(heads, qbs, kc):
                    ls, lk = _sb_scores(qb, k_ref[krows, sl], mask)
                    later = _dot_exact_l(lk, upper)
                    w = jnp.where(mask, jnp.exp(ls + later + run), 0.0)
                    out.append((run + jnp.sum(lk, axis=1, keepdims=True), acc + _dot(w, v_ref[krows, sl], 1, 0)))
                return tuple(out)

            init = (jnp.zeros((BQ, 1), F32), jnp.zeros((BQ, SB_DIM), F32))
            res = lax.fori_loop(0, nkb, kloop, (init, init))
            for h, (sl, (run, acc)) in enumerate(zip(heads, res)):
                o_ref[rows, sl] = acc
                tot_ref[rows, h:h + 1] = run
            return carry

        lax.fori_loop(0, nq, qloop, 0)

    def spec(off):
        return pl.BlockSpec((S, LANES), lambda b, hp: (b, hp + off))

    return pl.pallas_call(
        body, name=name, grid=(T // S, nhp), in_specs=[spec(0), spec(nhp), spec(2 * nhp)],
        out_specs=[spec(0), pl.BlockSpec((None, S, 2), lambda b, hp: (hp, b, 0))],
        out_shape=[jax.ShapeDtypeStruct((T, 2 * SB_HEADS * SB_DIM), F32), jax.ShapeDtypeStruct((nhp, T, 2), F32)],
        compiler_params=_cparams("parallel", "parallel"))(proj, proj, proj)


def _sb_bwd(proj, tot, dcat, S, name):
    T, width = proj.shape
    BQ, BK = SB_QBLOCK, ATT_BLOCK
    nq = S // BQ
    nhp = SB_HEADS // 2
    scale = SB_DIM ** -0.5
    heads = [slice(h * SB_DIM, (h + 1) * SB_DIM) for h in range(2)]

    def body(q_ref, k_ref, v_ref, tot_ref, do_ref, dp_hbm, dq_s, dk_s, dv_s, sems):
        b, hp = pl.program_id(0), pl.program_id(1)
        ahead = _iota2((BQ, BK), 1) - _iota2((BQ, BK), 0)
        r, c = _iota2((BK, BK), 0), _iota2((BK, BK), 1)
        upto = (r <= c).astype(BF16)
        earlier = (r < c).astype(BF16)
        dk_s[...] = jnp.zeros_like(dk_s)
        dv_s[...] = jnp.zeros_like(dv_s)

        def qloop(iq, carry):
            q0 = pl.multiple_of(iq * BQ, BQ)
            rows = pl.ds(q0, BQ)
            qbs = [q_ref[rows, sl] for sl in heads]
            dobs = [do_ref[rows, sl] for sl in heads]
            totals = [tot_ref[rows, h:h + 1] for h in range(2)]

            def kloop(kb, kc):
                k0 = pl.multiple_of(kb * BK, BK)
                krows = pl.ds(k0, BK)
                mask = ahead < q0 - k0
                out = []
                for sl, qb, dob, total, (run, grun, dq) in zip(heads, qbs, dobs, totals, kc):
                    kblk = k_ref[krows, sl]
                    ls, lk = _sb_scores(qb, kblk, mask)
                    later = total - (_dot_exact_l(lk, upto) + run)
                    w = jnp.where(mask, jnp.exp(ls + later), 0.0)
                    g = w * _dot(dob, v_ref[krows, sl], 1, 1)
                    before = _dot_exact_l(g, earlier) + grun
                    beta = jnp.exp(ls)
                    dz = jnp.where(mask, g * (1.0 - beta) - beta * before, 0.0) * scale
                    dv_s[krows, sl] += _dot(w, dob, 0, 0)
                    dk_s[krows, sl] += _dot(dz, qb, 0, 0)
                    out.append((run + jnp.sum(lk, axis=1, keepdims=True),
                                grun + jnp.sum(g, axis=1, keepdims=True), dq + _dot(dz, kblk, 1, 0)))
                return tuple(out)

            zero = jnp.zeros((BQ, 1), F32)
            init = (zero, zero, jnp.zeros((BQ, SB_DIM), F32))
            res = lax.fori_loop(0, (iq + 1) * (BQ // BK), kloop, (init, init))
            for sl, (_, _, dq) in zip(heads, res):
                dq_s[rows, sl] = dq
            return carry

        lax.fori_loop(0, nq, qloop, 0)
        r0 = pl.multiple_of(b * S, S)
        copies = []
        for n, buf in enumerate((dq_s, dk_s, dv_s)):
            c0 = pl.multiple_of((hp + n * nhp) * LANES, LANES)
            copies.append(pltpu.make_async_copy(buf, dp_hbm.at[pl.ds(r0, S), pl.ds(c0, LANES)], sems.at[n]))
        for cp in copies:
            cp.start()
        for cp in copies:
            cp.wait()

    def spec(off):
        return pl.BlockSpec((S, LANES), lambda b, hp: (b, hp + off))

    return pl.pallas_call(
        body, name=name, grid=(T // S, nhp),
        in_specs=[spec(0), spec(nhp), spec(2 * nhp), pl.BlockSpec((None, S, 2), lambda b, hp: (hp, b, 0)), spec(0)],
        out_specs=ANY, out_shape=jax.ShapeDtypeStruct((T, width), F32),
        scratch_shapes=[pltpu.VMEM((S, LANES), F32)] * 3 + [pltpu.SemaphoreType.DMA((3,))],
        compiler_params=_cparams("arbitrary", "arbitrary"))(proj, proj, proj, tot, dcat)


HG_COL0 = 3 * SB_HEADS * SB_DIM // LANES


def _hg_gates(q, fp, lbv):
    sg = _sig(fp)
    f = lbv + (1.0 - lbv) * sg
    kk = (1.0 - lbv) * _sig(-fp)
    sq = _sig(q)
    return sg, f, kk, sq


def _hg_chunk(qs, kk, lf, incl):
    C = HG_CHUNK
    b = _dot_exact_r(incl, lf)
    bl = b[C - 1:C, :]
    bm = b[C // 2 - 1:C // 2, :]
    e_t = jnp.exp(b - bm)
    e_s = jnp.exp(bm - b)
    e_i = jnp.exp(b)
    e_e = jnp.exp(bl - b)
    return bl, e_t, e_s, e_i, e_e


def _hgrn_fwd(proj, cat, lb, hgn, S, name):
    T = proj.shape[0]
    B = T // S
    C = HG_CHUNK
    NC = S // C

    def body(q_ref, f_ref, i_ref, g_ref, lb_ref, hgn_ref, cat_hbm, ob_ref, oraw_ref, st_ref, state):
        del cat_hbm
        state[...] = jnp.zeros_like(state)
        row, col = _iota2((C, C), 0), _iota2((C, C), 1)
        causal = row >= col
        incl = causal.astype(BF16)
        lbv = lb_ref[...]

        def chunk(c, carry):
            rows = pl.ds(pl.multiple_of(c * C, C), C)
            q, iv, gv = q_ref[rows, :], i_ref[rows, :], g_ref[rows, :]
            _, f, kk, sq = _hg_gates(q, f_ref[rows, :], lbv)
            qs = q * sq
            bl, e_t, e_s, e_i, e_e = _hg_chunk(qs, kk, jnp.log(f), incl)
            p = jnp.where(causal, _dot(qs * e_t, kk * e_s, 1, 1), 0.0)
            st = state[...]
            st_ref[0, 0, c] = st
            o = _dot(qs * e_i, st, 1, 1) + _dot(p, iv, 1, 0)
            state[...] = st * jnp.exp(bl) + _dot(iv, kk * e_e, 0, 0)
            oraw_ref[rows, :] = o
            r = lax.rsqrt(jnp.mean(o * o, axis=1, keepdims=True) + EPS)
            ob_ref[rows, :] = o * r * hgn_ref[...] * (gv * _sig(gv))
            return carry

        lax.fori_loop(0, NC, chunk, 0)

    def spec(off):
        return pl.BlockSpec((S, LANES), lambda b, h: (b, h + off))

    return pl.pallas_call(
        body, name=name, grid=(B, HG_HEADS),
        in_specs=[spec(HG_COL0), spec(HG_COL0 + 4), spec(HG_COL0 + 8), spec(HG_COL0 + 12),
                  pl.BlockSpec((1, LANES), lambda b, h: (0, h)), pl.BlockSpec((1, LANES), lambda b, h: (0, 0)), ANY],
        out_specs=[spec(4), spec(0), pl.BlockSpec((1, 1, NC, LANES, LANES), lambda b, h: (b, h, 0, 0, 0))],
        out_shape=[jax.ShapeDtypeStruct(cat.shape, F32), jax.ShapeDtypeStruct((T, HG_HEADS * LANES), F32),
                   jax.ShapeDtypeStruct((B, HG_HEADS, NC, LANES, LANES), F32)],
        scratch_shapes=[pltpu.VMEM((LANES, LANES), F32)],
        input_output_aliases={6: 0},
        compiler_params=_cparams("parallel", "parallel"))(proj, proj, proj, proj, lb, hgn, cat)


def _hgrn_bwd(proj, oraw, dcat, states, lb, hgn, dproj, S, name):
    T = proj.shape[0]
    B = T // S
    C = HG_CHUNK
    NC = S // C

    def body(q_ref, f_ref, i_ref, g_ref, oraw_ref, dy_ref, st_ref, lb_ref, hgn_ref, dp_in,
             dp_hbm, dlb_ref, dhgn_ref, dstate, dq_s, df_s, di_s, dg_s, sems):
        del dp_in
        h, b = pl.program_id(0), pl.program_id(1)
        row, col = _iota2((C, C), 0), _iota2((C, C), 1)
        causal = row >= col
        incl = causal.astype(BF16)
        last_row = _iota2((C, LANES), 0) == C - 1
        lbv = lb_ref[...]
        hg = hgn_ref[...]
        dstate[...] = jnp.zeros_like(dstate)

        @pl.when(b == 0)
        def _():
            dlb_ref[...] = jnp.zeros_like(dlb_ref)

        @pl.when(jnp.logical_and(b == 0, h == 0))
        def _():
            dhgn_ref[...] = jnp.zeros_like(dhgn_ref)

        def chunk(cc, carry):
            c = NC - 1 - cc
            rows = pl.ds(pl.multiple_of(c * C, C), C)
            q, fp, iv, gv = q_ref[rows, :], f_ref[rows, :], i_ref[rows, :], g_ref[rows, :]
            o = oraw_ref[rows, :]
            dy = dy_ref[rows, :]
            r = lax.rsqrt(jnp.mean(o * o, axis=1, keepdims=True) + EPS)
            on = o * r
            sgv = _sig(gv)
            silu_g = gv * sgv
            dg_s[rows, :] = dy * on * hg * (sgv * (1.0 + gv * (1.0 - sgv)))
            dhgn_ref[...] += jnp.sum(dy * on * silu_g, axis=0, keepdims=True)
            dn = dy * hg * silu_g
            do = r * dn - o * (r * r * r * jnp.mean(o * dn, axis=1, keepdims=True))
            sg, f, kk, sq = _hg_gates(q, fp, lbv)
            qs = q * sq
            bl, e_t, e_s, e_i, e_e = _hg_chunk(qs, kk, jnp.log(f), incl)
            qd, kd, qi, ke = qs * e_t, kk * e_s, qs * e_i, kk * e_e
            p = jnp.where(causal, _dot(qd, kd, 1, 1), 0.0)
            st = st_ref[0, 0, c]
            dst = dstate[...]
            ebl = jnp.exp(bl)
            dqi = _dot(do, st, 1, 0)
            dp = jnp.where(causal, _dot(do, iv, 1, 1), 0.0)
            di_s[rows, :] = _dot(p, do, 0, 0) + _dot(ke, dst, 1, 1)
            dqd = _dot(dp, kd, 1, 0)
            dkd = _dot(dp, qd, 0, 0)
            dke = _dot(iv, dst, 1, 0)
            dbl = jnp.sum(st * dst, axis=0, keepdims=True) * ebl + jnp.sum(dke * ke, axis=0, keepdims=True)
            db = dqd * qd - dkd * kd + dqi * qi - dke * ke + jnp.where(last_row, dbl, 0.0)
            dqs = dqd * e_t + dqi * e_i
            dkk = dkd * e_s + dke * e_e
            dlf = _dot_exact_r(incl, db, cm=0)
            dstate[...] = _dot(do, qi, 0, 0) + dst * ebl
            oms = 1.0 - sg
            dfd = dlf / f
            df_s[rows, :] = (dfd - dkk) * (1.0 - lbv) * sg * oms
            dq_s[rows, :] = dqs * (sq * (1.0 + q * (1.0 - sq)))
            dlb_ref[...] += jnp.sum((dfd - dkk) * oms, axis=0, keepdims=True)
            return carry

        lax.fori_loop(0, NC, chunk, 0)
        r0 = pl.multiple_of(b * S, S)
        copies = []
        for n, buf in enumerate((dq_s, df_s, di_s, dg_s)):
            c0 = pl.multiple_of((HG_COL0 + 4 * n + h) * LANES, LANES)
            copies.append(pltpu.make_async_copy(buf, dp_hbm.at[pl.ds(r0, S), pl.ds(c0, LANES)], sems.at[n]))
        for cp in copies:
            cp.start()
        for cp in copies:
            cp.wait()

    def spec(off):
        return pl.BlockSpec((S, LANES), lambda h, b: (b, h + off))

    return pl.pallas_call(
        body, name=name, grid=(HG_HEADS, B),
        in_specs=[spec(HG_COL0), spec(HG_COL0 + 4), spec(HG_COL0 + 8), spec(HG_COL0 + 12), spec(0), spec(4),
                  pl.BlockSpec((1, 1, NC, LANES, LANES), lambda h, b: (b, h, 0, 0, 0)),
                  pl.BlockSpec((1, LANES), lambda h, b: (0, h)), pl.BlockSpec((1, LANES), lambda h, b: (0, 0)), ANY],
        out_specs=[ANY, pl.BlockSpec((1, LANES), lambda h, b: (0, h)), pl.BlockSpec((1, LANES), lambda h, b: (0, 0))],
        out_shape=[jax.ShapeDtypeStruct(dproj.shape, F32), jax.ShapeDtypeStruct((1, HG_HEADS * LANES), F32),
                   jax.ShapeDtypeStruct((1, LANES), F32)],
        scratch_shapes=[pltpu.VMEM((LANES, LANES), F32)] + [pltpu.VMEM((S, LANES), F32)] * 4
        + [pltpu.SemaphoreType.DMA((4,))],
        input_output_aliases={9: 0},
        compiler_params=_cparams("arbitrary", "arbitrary"))(proj, proj, proj, proj, oraw, dcat, states, lb, hgn, dproj)


def _lower_bound_fwd(logits, name):
    assert logits.shape[0] == 2

    def body(l_ref, o_ref):
        l0, l1 = l_ref[0:1, :], l_ref[1:2, :]
        m = jnp.maximum(l0, l1)
        e0, e1 = jnp.exp(l0 - m), jnp.exp(l1 - m)
        o_ref[0:1, :] = jnp.zeros_like(l0)
        o_ref[1:2, :] = e1 / (e0 + e1)

    return pl.pallas_call(body, name=name, out_shape=jax.ShapeDtypeStruct(logits.shape, F32))(logits)


def _lower_bound_bwd(logits, dlb, name):
    def body(l_ref, d_ref, o_ref):
        l0, l1 = l_ref[0:1, :], l_ref[1:2, :]
        m = jnp.maximum(l0, l1)
        e0, e1 = jnp.exp(l0 - m), jnp.exp(l1 - m)
        s1 = e1 / (e0 + e1)
        t = s1 * (1.0 - s1) * d_ref[1:2, :]
        o_ref[0:1, :] = -t
        o_ref[1:2, :] = t

    return pl.pallas_call(body, name=name, out_shape=jax.ShapeDtypeStruct(logits.shape, F32))(logits, dlb)


def _bucket_thresholds():
    dist = np.arange(WINDOW)
    max_exact = N_BUCKETS // 2
    large = max_exact + (np.log(np.maximum(dist, max_exact) / max_exact) / math.log(MAX_DISTANCE / max_exact)
                         * (N_BUCKETS - max_exact)).astype(np.int32)
    bucket = np.where(dist < max_exact, dist, np.minimum(large, N_BUCKETS - 1))
    assert np.all(np.diff(bucket) >= 0)
    return [int(np.argmax(bucket >= k)) if np.any(bucket >= k) else 10 ** 6 for k in range(1, N_BUCKETS)]


def _band_bucket():
    dist = _iota2((WINDOW, 2 * WINDOW), 0) + WINDOW - _iota2((WINDOW, 2 * WINDOW), 1)
    bucket = jnp.zeros((WINDOW, 2 * WINDOW), jnp.int32)
    for thr in _bucket_thresholds():
        bucket = bucket + (dist >= thr).astype(jnp.int32)
    band = jnp.logical_and(dist >= 0, dist < WINDOW)
    return bucket, band


def _bias_build(rel_bias, name):
    def body(rb_ref, o_ref):
        h = pl.program_id(0)
        bucket, _ = _band_bucket()
        bias = jnp.zeros((WINDOW, 2 * WINDOW), F32)
        for k in range(N_BUCKETS):
            bias = jnp.where(bucket == k, rb_ref[k, h], bias)
        o_ref[0] = bias

    return pl.pallas_call(
        body, name=name, grid=(SW_HEADS,), in_specs=[pl.BlockSpec(memory_space=pltpu.SMEM)],
        out_specs=pl.BlockSpec((1, WINDOW, 2 * WINDOW), lambda h: (h, 0, 0)),
        out_shape=jax.ShapeDtypeStruct((SW_HEADS, WINDOW, 2 * WINDOW), F32),
        compiler_params=_cparams("parallel"))(rel_bias)


def _bias_reduce(dbias, name):
    def body(d_ref, o_ref):
        bucket, band = _band_bucket()
        d = jnp.where(band, d_ref[0], 0.0)
        lane = _iota2((1, LANES), 1)
        out = jnp.zeros((1, LANES), F32)
        for k in range(N_BUCKETS):
            out = jnp.where(lane == k, jnp.sum(jnp.where(bucket == k, d, 0.0)), out)
        o_ref[0] = out

    return pl.pallas_call(
        body, name=name, grid=(SW_HEADS,), in_specs=[pl.BlockSpec((1, WINDOW, 2 * WINDOW), lambda h: (h, 0, 0))],
        out_specs=pl.BlockSpec((1, 1, LANES), lambda h: (h, 0, 0)),
        out_shape=jax.ShapeDtypeStruct((SW_HEADS, 1, LANES), F32), compiler_params=_cparams("parallel"))(dbias)


SW_Q_COLS = SW_HEADS * SW_DIM
SW_K_BLOCK0 = SW_Q_COLS // LANES
SW_V_BLOCK0 = SW_K_BLOCK0 + SW_KV * SW_DIM // LANES
SW_STEP_HEADS = 8


def _swa_probs(qb, kprev, kcur, bias_ref, hl, sink, mprev, mcur):
    scale = SW_DIM ** -0.5
    lp = jnp.where(mprev, _dot(qb, kprev, 1, 1) * scale + bias_ref[hl, :, 0:WINDOW], NEG)
    lc = jnp.where(mcur, _dot(qb, kcur, 1, 1) * scale + bias_ref[hl, :, WINDOW:2 * WINDOW], NEG)
    m = jnp.maximum(jnp.maximum(jnp.max(lp, axis=1, keepdims=True), jnp.max(lc, axis=1, keepdims=True)), sink)
    ep, ec = jnp.exp(lp - m), jnp.exp(lc - m)
    es = jnp.exp(sink - m)
    den = jnp.sum(ep, axis=1, keepdims=True) + jnp.sum(ec, axis=1, keepdims=True) + es
    return ep, ec, es, den


def _swa_fwd(qkn, proj, bias, sinks, S, name):
    T = qkn.shape[0]
    W = WINDOW
    nb = S // W

    def body(q_ref, k_ref, v_ref, bias_ref, sink_ref, o_ref):
        kp = pl.program_id(1)
        row, col = _iota2((W, W), 0), _iota2((W, W), 1)
        mcur = col <= row
        above = col > row

        def blk(n, carry):
            rows = pl.ds(pl.multiple_of(n * W, W), W)
            prow = pl.ds(pl.multiple_of(jnp.maximum(n - 1, 0) * W, W), W)
            mprev = jnp.logical_and(above, n > 0)
            for kvh in range(2):
                ksl = slice(kvh * SW_DIM, (kvh + 1) * SW_DIM)
                kcur, kprev = k_ref[rows, ksl], k_ref[prow, ksl]
                vcur, vprev = v_ref[rows, ksl], v_ref[prow, ksl]
                for g in range(4):
                    hl = kvh * 4 + g
                    qsl = slice(hl * SW_DIM, (hl + 1) * SW_DIM)
                    sink = sink_ref[kp * SW_STEP_HEADS + hl]
                    ep, ec, _, den = _swa_probs(q_ref[rows, qsl], kprev, kcur, bias_ref, hl, sink, mprev, mcur)
                    o_ref[rows, qsl] = (_dot(ep, vprev, 1, 0) + _dot(ec, vcur, 1, 0)) / den
            return carry

        lax.fori_loop(0, nb, blk, 0)

    return pl.pallas_call(
        body, name=name, grid=(T // S, 2),
        in_specs=[pl.BlockSpec((S, 4 * LANES), lambda b, kp: (b, kp)),
                  pl.BlockSpec((S, LANES), lambda b, kp: (b, SW_K_BLOCK0 + kp)),
                  pl.BlockSpec((S, LANES), lambda b, kp: (b, SW_V_BLOCK0 + kp)),
                  pl.BlockSpec((SW_STEP_HEADS, W, 2 * W), lambda b, kp: (kp, 0, 0)),
                  pl.BlockSpec(memory_space=pltpu.SMEM)],
        out_specs=pl.BlockSpec((S, 4 * LANES), lambda b, kp: (b, kp)),
        out_shape=jax.ShapeDtypeStruct((T, SW_Q_COLS), F32),
        compiler_params=_cparams("parallel", "parallel"))(qkn, qkn, proj, bias, sinks)


def _swa_bwd(qkn, proj, bias, sinks, do, S, name):
    T, width = proj.shape
    W = WINDOW
    nb = S // W
    scale = SW_DIM ** -0.5

    def body(q_ref, k_ref, v_ref, bias_ref, sink_ref, do_ref, dp_hbm, dbias_ref, dsink_ref,
             dq_s, dk_s, dv_s, sems):
        kp, b = pl.program_id(0), pl.program_id(1)
        row, col = _iota2((W, W), 0), _iota2((W, W), 1)
        mcur = col <= row
        above = col > row
        dk_s[...] = jnp.zeros_like(dk_s)
        dv_s[...] = jnp.zeros_like(dv_s)

        @pl.when(b == 0)
        def _():
            dbias_ref[...] = jnp.zeros_like(dbias_ref)
            dsink_ref[...] = jnp.zeros_like(dsink_ref)

        def blk(n, carry):
            rows = pl.ds(pl.multiple_of(n * W, W), W)
            prow = pl.ds(pl.multiple_of(jnp.maximum(n - 1, 0) * W, W), W)
            mprev = jnp.logical_and(above, n > 0)
            for kvh in range(2):
                ksl = slice(kvh * SW_DIM, (kvh + 1) * SW_DIM)
                kcur, kprev = k_ref[rows, ksl], k_ref[prow, ksl]
                vcur, vprev = v_ref[rows, ksl], v_ref[prow, ksl]
                for g in range(4):
                    hl = kvh * 4 + g
                    qsl = slice(hl * SW_DIM, (hl + 1) * SW_DIM)
                    sink = sink_ref[kp * SW_STEP_HEADS + hl]
                    qb = q_ref[rows, qsl]
                    ep, ec, es, den = _swa_probs(qb, kprev, kcur, bias_ref, hl, sink, mprev, mcur)
                    inv = 1.0 / den
                    pp, pc = ep * inv, ec * inv
                    dob = do_ref[rows, qsl]
                    dpp, dpc = _dot(dob, vprev, 1, 1), _dot(dob, vcur, 1, 1)
                    total = jnp.sum(pp * dpp, axis=1, keepdims=True) + jnp.sum(pc * dpc, axis=1, keepdims=True)
                    dlp = pp * (dpp - total)
                    dlc = pc * (dpc - total)
                    dsink_ref[hl:hl + 1, :] += jnp.zeros((1, LANES), F32) - jnp.sum(es * inv * total)
                    dbias_ref[hl, :, 0:W] += dlp
                    dbias_ref[hl, :, W:2 * W] += dlc
                    dq_s[rows, qsl] = (_dot(dlp, kprev, 1, 0) + _dot(dlc, kcur, 1, 0)) * scale
                    dk_s[prow, ksl] += _dot(dlp, qb, 0, 0) * scale
                    dk_s[rows, ksl] += _dot(dlc, qb, 0, 0) * scale
                    dv_s[prow, ksl] += _dot(pp, dob, 0, 0)
                    dv_s[rows, ksl] += _dot(pc, dob, 0, 0)
            return carry

        lax.fori_loop(0, nb, blk, 0)
        r0 = pl.multiple_of(b * S, S)
        cq = pl.multiple_of(kp * 4 * LANES, LANES)
        ck = pl.multiple_of((SW_K_BLOCK0 + kp) * LANES, LANES)
        cv = pl.multiple_of((SW_V_BLOCK0 + kp) * LANES, LANES)
        copies = [pltpu.make_async_copy(dq_s, dp_hbm.at[pl.ds(r0, S), pl.ds(cq, 4 * LANES)], sems.at[0]),
                  pltpu.make_async_copy(dk_s, dp_hbm.at[pl.ds(r0, S), pl.ds(ck, LANES)], sems.at[1]),
                  pltpu.make_async_copy(dv_s, dp_hbm.at[pl.ds(r0, S), pl.ds(cv, LANES)], sems.at[2])]
        for cp in copies:
            cp.start()
        for cp in copies:
            cp.wait()

    qspec = pl.BlockSpec((S, 4 * LANES), lambda kp, b: (b, kp))
    return pl.pallas_call(
        body, name=name, grid=(2, T // S),
        in_specs=[qspec, pl.BlockSpec((S, LANES), lambda kp, b: (b, SW_K_BLOCK0 + kp)),
                  pl.BlockSpec((S, LANES), lambda kp, b: (b, SW_V_BLOCK0 + kp)),
                  pl.BlockSpec((SW_STEP_HEADS, W, 2 * W), lambda kp, b: (kp, 0, 0)),
                  pl.BlockSpec(memory_space=pltpu.SMEM), qspec],
        out_specs=[ANY, pl.BlockSpec((SW_STEP_HEADS, W, 2 * W), lambda kp, b: (kp, 0, 0)),
                   pl.BlockSpec((SW_STEP_HEADS, LANES), lambda kp, b: (kp, 0))],
        out_shape=[jax.ShapeDtypeStruct((T, width), F32), jax.ShapeDtypeStruct((SW_HEADS, W, 2 * W), F32),
                   jax.ShapeDtypeStruct((SW_HEADS, LANES), F32)],
        scratch_shapes=[pltpu.VMEM((S, 4 * LANES), F32), pltpu.VMEM((S, LANES), F32), pltpu.VMEM((S, LANES), F32),
                        pltpu.SemaphoreType.DMA((3,))],
        compiler_params=_cparams("arbitrary", "arbitrary"))(qkn, qkn, proj, bias, sinks, do)


CHIP_FLIPS = ((1, 0), (0, 1), (1, 1))


def _chip_slice(ref, axis, j, size):
    if axis == 1:
        return ref.at[:, pl.ds(j * size, size), :]
    return ref.at[:, :, pl.ds(j * size, size)]


def _flip(v, f):
    return 1 - v if f else v


def _gather_weights(shards, axes, name):
    n = len(shards)
    sizes = [s.shape[a] for s, a in zip(shards, axes)]
    out_shapes = [jax.ShapeDtypeStruct(tuple(4 * d if i == a else d for i, d in enumerate(s.shape)), s.dtype)
                  for s, a in zip(shards, axes)]

    def body(*refs):
        ins, outs = refs[:n], refs[n:2 * n]
        send, recv, loc = refs[2 * n:]
        x, y, c = lax.axis_index("x"), lax.axis_index("y"), lax.axis_index("c")
        j = 2 * x + y
        for jj in range(4):
            @pl.when(j == jj)
            def _():
                for t in range(n):
                    dst = _chip_slice(outs[t], axes[t], jj, sizes[t])
                    pltpu.make_async_copy(ins[t], dst, loc.at[t]).start()
                    for r, (fx, fy) in enumerate(CHIP_FLIPS):
                        pltpu.make_async_remote_copy(ins[t], dst, send.at[t, r], recv.at[t, r],
                                                     device_id=(_flip(x, fx), _flip(y, fy), c),
                                                     device_id_type=MESH).start()
        for t in range(n):
            dst = _chip_slice(outs[t], axes[t], 0, sizes[t])
            pltpu.make_async_copy(ins[t], dst, loc.at[t]).wait()
            for r in range(3):
                pltpu.make_async_remote_copy(ins[t], dst, send.at[t, r], recv.at[t, r],
                                             device_id=(x, y, c), device_id_type=MESH).wait()

    return pl.pallas_call(
        body, name=name, in_specs=[ANY] * n, out_specs=[ANY] * n, out_shape=out_shapes,
        scratch_shapes=[pltpu.SemaphoreType.DMA((n, 3)), pltpu.SemaphoreType.DMA((n, 3)),
                        pltpu.SemaphoreType.DMA((n,))])(*shards)


def _scatter_grads(grads, axes, name):
    n = len(grads)
    sizes = [g.shape[a] // 4 for g, a in zip(grads, axes)]
    out_shapes = [jax.ShapeDtypeStruct((4,) + tuple(sizes[t] if i == axes[t] else d for i, d in enumerate(g.shape)),
                                       g.dtype) for t, g in enumerate(grads)]

    def body(*refs):
        ins, outs = refs[:n], refs[n:2 * n]
        send, recv, loc = refs[2 * n:]
        x, y, c = lax.axis_index("x"), lax.axis_index("y"), lax.axis_index("c")
        j = 2 * x + y
        for jj in range(4):
            @pl.when(j == jj)
            def _():
                for t in range(n):
                    pltpu.make_async_copy(_chip_slice(ins[t], axes[t], jj, sizes[t]), outs[t].at[3], loc.at[t]).start()
                    for r, (fx, fy) in enumerate(CHIP_FLIPS):
                        target = jj ^ (2 * fx + fy)
                        pltpu.make_async_remote_copy(_chip_slice(ins[t], axes[t], target, sizes[t]), outs[t].at[r],
                                                     send.at[t, r], recv.at[t, r],
                                                     device_id=(_flip(x, fx), _flip(y, fy), c),
                                                     device_id_type=MESH).start()
        for t in range(n):
            src = _chip_slice(ins[t], axes[t], 0, sizes[t])
            pltpu.make_async_copy(src, outs[t].at[3], loc.at[t]).wait()
            for r in range(3):
                pltpu.make_async_remote_copy(src, outs[t].at[r], send.at[t, r], recv.at[t, r],
                                             device_id=(x, y, c), device_id_type=MESH).wait()

    return pl.pallas_call(
        body, name=name, in_specs=[ANY] * n, out_specs=[ANY] * n, out_shape=out_shapes,
        scratch_shapes=[pltpu.SemaphoreType.DMA((n, 3)), pltpu.SemaphoreType.DMA((n, 3)),
                        pltpu.SemaphoreType.DMA((n,))])(*grads)


def _swap_with_sibling(parts, name):
    n = len(parts)

    def body(*refs):
        ins, outs = refs[:n], refs[n:2 * n]
        send, recv = refs[2 * n:]
        peer = (lax.axis_index("x"), lax.axis_index("y"), 1 - lax.axis_index("c"))
        copies = [pltpu.make_async_remote_copy(ins[t], outs[t], send.at[t], recv.at[t], device_id=peer,
                                               device_id_type=MESH) for t in range(n)]
        for cp in copies:
            cp.start()
        for cp in copies:
            cp.wait()

    return pl.pallas_call(
        body, name=name, in_specs=[ANY] * n, out_specs=[ANY] * n,
        out_shape=[jax.ShapeDtypeStruct(p.shape, p.dtype) for p in parts],
        scratch_shapes=[pltpu.SemaphoreType.DMA((n,)), pltpu.SemaphoreType.DMA((n,))])(*parts)


def _allreduce_small(v, name):
    R = v.shape[0]
    ND = 8

    def body(v_ref, o_ref, buf, send, recv):
        x, y, c = lax.axis_index("x"), lax.axis_index("y"), lax.axis_index("c")
        me = 4 * x + 2 * y + c
        copies = []
        for d in range(1, ND):
            peer = (_flip(x, d >> 2 & 1), _flip(y, d >> 1 & 1), _flip(c, d & 1))
            copies.append(pltpu.make_async_remote_copy(v_ref, buf.at[me], send.at[d], recv.at[me], device_id=peer,
                                                       device_id_type=MESH))
        for cp in copies:
            cp.start()
        buf[pl.ds(me, 1)] = v_ref[...][None]
        for k in range(ND):
            @pl.when(me != k)
            def _():
                pltpu.make_async_remote_copy(v_ref, buf.at[k], send.at[0], recv.at[k], device_id=(x, y, c),
                                             device_id_type=MESH).wait_recv()
        for cp in copies:
            cp.wait_send()
        total = buf[0]
        for k in range(1, ND):
            total = total + buf[k]
        o_ref[...] = total

    vm = pl.BlockSpec(memory_space=pltpu.VMEM)
    return pl.pallas_call(
        body, name=name, in_specs=[vm], out_specs=vm, out_shape=jax.ShapeDtypeStruct((R, LANES), F32),
        scratch_shapes=[pltpu.VMEM((ND, R, LANES), F32), pltpu.SemaphoreType.DMA((ND,)),
                        pltpu.SemaphoreType.DMA((ND,))],
        compiler_params=pltpu.CompilerParams(vmem_limit_bytes=VMEM_LIMIT))(v)


def _tile2(R, Cn):
    tc = _pick(Cn, 2048)
    tr = R
    for cand in (256, 128, 64, 32, 16, 8):
        if R % cand == 0:
            tr = cand
            break
    return tr, tc


def _sum4(stack, name):
    _, R, Cn = stack.shape
    tr, tc = _tile2(R, Cn)

    def body(s_ref, o_ref):
        o_ref[...] = ((s_ref[0].astype(F32) + s_ref[1].astype(F32)) + s_ref[2].astype(F32)) + s_ref[3].astype(F32)

    return pl.pallas_call(
        body, name=name, grid=(R // tr, Cn // tc), in_specs=[pl.BlockSpec((4, tr, tc), lambda i, j: (0, i, j))],
        out_specs=pl.BlockSpec((tr, tc), lambda i, j: (i, j)), out_shape=jax.ShapeDtypeStruct((R, Cn), F32),
        compiler_params=_cparams("parallel", "parallel"))(stack)


def _adamw(w, m, v, g_parts, name):
    R, Cn = w.shape
    tr, tc = _tile2(R, Cn)
    npart = len(g_parts)
    c1 = 1.0 / (1.0 - ADAM_B1 ** ADAM_STEP)
    c2 = 1.0 / (1.0 - ADAM_B2 ** ADAM_STEP)

    def body(*refs):
        w_ref, m_ref, v_ref = refs[:3]
        g_refs = refs[3:3 + npart]
        g_out, d_out, m_out, v_out = refs[3 + npart:]
        g = g_refs[0][...]
        for r in g_refs[1:]:
            g = g + r[...]
        mn = ADAM_B1 * m_ref[...] + (1.0 - ADAM_B1) * g
        vn = ADAM_B2 * v_ref[...] + (1.0 - ADAM_B2) * (g * g)
        g_out[...] = g
        m_out[...] = mn
        v_out[...] = vn
        d_out[...] = -ADAM_LR * ((mn * c1) / (jnp.sqrt(vn * c2) + ADAM_EPS) + ADAM_WD * w_ref[...])

    spec = pl.BlockSpec((tr, tc), lambda i, j: (i, j))
    return pl.pallas_call(
        body, name=name, grid=(R // tr, Cn // tc), in_specs=[spec] * (3 + npart), out_specs=[spec] * 4,
        out_shape=[jax.ShapeDtypeStruct((R, Cn), F32)] * 4,
        compiler_params=_cparams("parallel", "parallel"))(w, m, v, *g_parts)


SHARDED = (("ab_w_in", 2), ("ab_w_out", 1), ("c_w_in", 2), ("c_w_out", 1), ("ffn_up", 2), ("ffn_conv", 2),
           ("ffn_down", 1), ("ple_gate", 1), ("ple_proj", 2))
SMALL = ("mix_norm", "hg_lb_logits", "hg_out_norm", "q_norm", "k_norm", "sinks", "rel_bias", "ffn_norm",
         "ffn_conv_b", "ple_norm")
WEIGHTS = ("mix_norm", "ab_w_in", "hg_lb_logits", "hg_out_norm", "ab_w_out", "c_w_in", "q_norm", "k_norm", "sinks",
           "rel_bias", "c_w_out", "ffn_norm", "ffn_up", "ffn_conv", "ffn_conv_b", "ffn_down", "ple_norm", "ple_gate",
           "ple_proj")
PACK_ALIGN = 8 * LANES


def _pack(arrs):
    pieces = []
    for a in arrs:
        flat = a.reshape(-1)
        pad = -flat.shape[0] % PACK_ALIGN
        pieces.append(jnp.pad(flat, (0, pad)).reshape(-1, LANES))
    return jnp.concatenate(pieces, axis=0)


def _unpack(packed, like):
    out, r = [], 0
    for a in like:
        size = int(np.prod(a.shape))
        rows = (size + PACK_ALIGN - 1) // PACK_ALIGN * 8
        out.append(packed[r:r + rows].reshape(-1)[:size].reshape(a.shape))
        r += rows
    return out


def _forward_backward(x, p, target, W, S):
    T = x.shape[0]
    depth = p.shape[0]
    lb = _lower_bound_fwd(W["hg_lb_logits"], "lower_bound_fwd")
    bias = _bias_build(W["rel_bias"], "bias_build")
    qk_gain = jnp.concatenate([jnp.tile(W["q_norm"], (1, SW_HEADS)), jnp.tile(W["k_norm"], (1, SW_KV))], axis=1)

    saved = []
    h = x
    for i in range(depth):
        j = i // 2
        s = {"h0": h}
        s["hn"] = _rmsnorm_fwd(h, W["mix_norm"][i:i + 1], f"mix_norm_fwd_{i}")
        if i % 2 == 0:
            s["proj"] = _matmul(s["hn"], W["ab_w_in"][j], "nn", f"ab_in_{i}")
            cat, s["sb_tot"] = _sb_fwd(s["proj"], S, f"sb_fwd_{i}")
            s["cat"], s["oraw"], s["states"] = _hgrn_fwd(s["proj"], cat, lb[j:j + 1], W["hg_out_norm"][j:j + 1], S,
                                                         f"hgrn_fwd_{i}")
            h = _matmul(s["cat"], W["ab_w_out"][j], "nn", f"ab_out_{i}", res=h)
        else:
            s["proj"] = _matmul(s["hn"], W["c_w_in"][j], "nn", f"c_in_{i}")
            s["qkn"] = _headnorm_fwd(s["proj"], qk_gain[j:j + 1], f"qk_norm_fwd_{i}")
            s["o"] = _swa_fwd(s["qkn"], s["proj"], bias, W["sinks"][j], S, f"swa_fwd_{i}")
            h = _matmul(s["o"], W["c_w_out"][j], "nn", f"c_out_{i}", res=h)
        s["h1"] = h
        s["hn2"] = _rmsnorm_fwd(h, W["ffn_norm"][i:i + 1], f"ffn_norm_fwd_{i}")
        s["u"] = _matmul(s["hn2"], W["ffn_up"][i], "nn", f"ffn_up_{i}")
        s["a"] = _convglu_fwd(s["u"], W["ffn_conv"][i], W["ffn_conv_b"][i:i + 1], S, f"convglu_fwd_{i}")
        h = _matmul(s["a"], W["ffn_down"][i], "nn", f"ffn_down_{i}", res=h)
        s["h2"] = h
        s["hn3"] = _rmsnorm_fwd(h, W["ple_norm"][i:i + 1], f"ple_norm_fwd_{i}")
        s["z"] = _matmul(s["hn3"], W["ple_gate"][i], "nn", f"ple_gate_{i}")
        s["pp"] = _matmul(p[i], W["ple_proj"][i], "nn", f"ple_proj_{i}")
        h = _ple_fwd(h, s["z"], s["pp"], f"ple_fwd_{i}")
        saved.append(s)

    loss, dh = _loss_fwd_bwd(h, target, "loss")

    G = {k: [None] * depth for k in ("mix_norm", "ffn_norm", "ple_norm", "ffn_up", "ffn_conv", "ffn_conv_b",
                                     "ffn_down", "ple_gate", "ple_proj")}
    for k in ("ab_w_in", "ab_w_out", "c_w_in", "c_w_out", "hg_out_norm", "q_norm", "k_norm", "sinks", "lb"):
        G[k] = [None] * (depth // 2)
    dbias_total = None
    for i in reversed(range(depth)):
        j = i // 2
        s = saved[i]
        dz, dpp = _ple_bwd(dh, s["z"], s["pp"], f"ple_bwd_{i}")
        G["ple_proj"][i] = _matmul(p[i], dpp, "tn", f"d_ple_proj_{i}", out_dtype=BF16)
        G["ple_gate"][i] = _matmul(s["hn3"], dz, "tn", f"d_ple_gate_{i}", out_dtype=BF16)
        dhn = _matmul(dz, W["ple_gate"][i], "nt", f"d_hn3_{i}")
        dh, G["ple_norm"][i] = _rmsnorm_bwd(s["h2"], W["ple_norm"][i:i + 1], dhn, dh, f"ple_norm_bwd_{i}")

        da = _matmul(dh, W["ffn_down"][i], "nt", f"d_a_{i}")
        G["ffn_down"][i] = _matmul(s["a"], dh, "tn", f"d_ffn_down_{i}", out_dtype=BF16)
        du, G["ffn_conv"][i], G["ffn_conv_b"][i] = _convglu_bwd(s["u"], da, W["ffn_conv"][i],
                                                                 W["ffn_conv_b"][i:i + 1], S, f"convglu_bwd_{i}")
        G["ffn_up"][i] = _matmul(s["hn2"], du, "tn", f"d_ffn_up_{i}", out_dtype=BF16)
        dhn = _matmul(du, W["ffn_up"][i], "nt", f"d_hn2_{i}")
        dh, G["ffn_norm"][i] = _rmsnorm_bwd(s["h1"], W["ffn_norm"][i:i + 1], dhn, dh, f"ffn_norm_bwd_{i}")

        if i % 2 == 0:
            dcat = _matmul(dh, W["ab_w_out"][j], "nt", f"d_cat_{i}")
            G["ab_w_out"][j] = _matmul(s["cat"], dh, "tn", f"d_ab_out_{i}", out_dtype=BF16)
            dproj = _sb_bwd(s["proj"], s["sb_tot"], dcat, S, f"sb_bwd_{i}")
            dproj, G["lb"][j], G["hg_out_norm"][j] = _hgrn_bwd(s["proj"], s["oraw"], dcat, s["states"], lb[j:j + 1],
                                                               W["hg_out_norm"][j:j + 1], dproj, S, f"hgrn_bwd_{i}")
            G["ab_w_in"][j] = _matmul(s["hn"], dproj, "tn", f"d_ab_in_{i}", out_dtype=BF16)
            dhn = _matmul(dproj, W["ab_w_in"][j], "nt", f"d_hn_{i}")
        else:
            do = _matmul(dh, W["c_w_out"][j], "nt", f"d_o_{i}")
            G["c_w_out"][j] = _matmul(s["o"], dh, "tn", f"d_c_out_{i}", out_dtype=BF16)
            dqkv, dbias, dsink = _swa_bwd(s["qkn"], s["proj"], bias, W["sinks"][j], do, S, f"swa_bwd_{i}")
            dbias_total = dbias if dbias_total is None else dbias_total + dbias
            G["sinks"][j] = dsink[:, 0]
            dproj, dgain = _headnorm_bwd(s["proj"], qk_gain[j:j + 1], dqkv, f"qk_norm_bwd_{i}")
            G["q_norm"][j] = dgain[0, :SW_Q_COLS].reshape(SW_HEADS, SW_DIM).sum(axis=0)
            G["k_norm"][j] = dgain[0, SW_Q_COLS:].reshape(SW_KV, SW_DIM).sum(axis=0)
            G["c_w_in"][j] = _matmul(s["hn"], dproj, "tn", f"d_c_in_{i}", out_dtype=BF16)
            dhn = _matmul(dproj, W["c_w_in"][j], "nt", f"d_hn_{i}")
        dh, G["mix_norm"][i] = _rmsnorm_bwd(s["h0"], W["mix_norm"][i:i + 1], dhn, dh, f"mix_norm_bwd_{i}")

    grads = {k: jnp.stack(G[k]) for k in ("ab_w_in", "ab_w_out", "c_w_in", "c_w_out", "ffn_up", "ffn_down",
                                            "ple_gate", "ple_proj", "q_norm", "k_norm", "sinks")}
    grads["ffn_conv"] = jnp.stack(G["ffn_conv"])
    for k in ("mix_norm", "ffn_norm", "ple_norm", "ffn_conv_b", "hg_out_norm"):
        grads[k] = jnp.concatenate(G[k], axis=0)
    grads["hg_lb_logits"] = _lower_bound_bwd(W["hg_lb_logits"], jnp.concatenate(G["lb"], axis=0), "lower_bound_bwd")
    grads["rel_bias"] = _bias_reduce(dbias_total, "bias_reduce")[:, 0, :N_BUCKETS].T
    return loss, dh, grads


def kernel(x, p, mix_norm, ab_w_in, hg_lb_logits, hg_out_norm, ab_w_out, c_w_in, q_norm, k_norm, sinks, rel_bias, c_w_out, ffn_norm, ffn_up, ffn_conv, ffn_conv_b, ffn_down, ple_norm, ple_gate, ple_proj, loss_target, m_mix_norm, m_ab_w_in, m_hg_lb_logits, m_hg_out_norm, m_ab_w_out, m_c_w_in, m_q_norm, m_k_norm, m_sinks, m_rel_bias, m_c_w_out, m_ffn_norm, m_ffn_up, m_ffn_conv, m_ffn_conv_b, m_ffn_down, m_ple_norm, m_ple_gate, m_ple_proj, v_mix_norm, v_ab_w_in, v_hg_lb_logits, v_hg_out_norm, v_ab_w_out, v_c_w_in, v_q_norm, v_k_norm, v_sinks, v_rel_bias, v_c_w_out, v_ffn_norm, v_ffn_up, v_ffn_conv, v_ffn_conv_b, v_ffn_down, v_ple_norm, v_ple_gate, v_ple_proj):
    args = dict(locals())
    w = {k: args[k] for k in WEIGHTS}
    m = {k: args["m_" + k] for k in WEIGHTS}
    v = {k: args["v_" + k] for k in WEIGHTS}
    B, S, Dm = x.shape
    T = B * S
    names = [k for k, _ in SHARDED]
    axes = [a for _, a in SHARDED]

    wire = {k: (F32 if k == "ffn_conv" else BF16) for k in names}
    full = _gather_weights([w[k].astype(wire[k]) for k in names], axes, "gather_weights")
    W = dict(zip(names, full))
    for k in SMALL:
        W[k] = w[k]

    loss, dx, grads = _forward_backward(x.reshape(T, Dm), p.reshape(p.shape[0], T, p.shape[-1]),
                                        loss_target.reshape(T, Dm), W, S)
    loss = lax.psum(loss[0, 0], ("x", "y", "c"))

    stacks = _scatter_grads([grads[k] for k in names], axes, "scatter_grads")
    partial = [_sum4(st.reshape(4, -1, st.shape[-1]), f"sum_chips_{k}") for k, st in zip(names, stacks)]
    other = _swap_with_sibling(partial, "swap_core_sums")
    small_sum = _allreduce_small(_pack([grads[k] for k in SMALL]), "allreduce_small")

    out_g, out_d, out_m, out_v = {}, {}, {}, {}
    for k, mine, theirs in zip(names, partial, other):
        shp = w[k].shape
        r = [a.reshape(shp) for a in _adamw(w[k].reshape(mine.shape), m[k].reshape(mine.shape),
                                            v[k].reshape(mine.shape), [mine, theirs], f"adamw_{k}")]
        out_g[k], out_d[k], out_m[k], out_v[k] = r
    sm = _adamw(_pack([w[k] for k in SMALL]), _pack([m[k] for k in SMALL]), _pack([v[k] for k in SMALL]),
                [small_sum], "adamw_small")
    like = [w[k] for k in SMALL]
    for dst, packed in zip((out_g, out_d, out_m, out_v), sm):
        for k, a in zip(SMALL, _unpack(packed, like)):
            dst[k] = a

    return (loss, dx.reshape(B, S, Dm), *[out_g[k] for k in WEIGHTS], *[out_d[k] for k in WEIGHTS],
            *[out_m[k] for k in WEIGHTS], *[out_v[k] for k in WEIGHTS])
```

```python
import math

import numpy as np
import jax
import jax.numpy as jnp
from jax import lax
from jax.experimental import pallas as pl
from jax.experimental.pallas import tpu as pltpu

F32 = jnp.float32
BF16 = jnp.bfloat16
MESH = pl.DeviceIdType.MESH
ANY = pl.BlockSpec(memory_space=pl.ANY)

D_MODEL = 1024
EPS = 1e-6
SB_HEADS, SB_DIM = 8, 64
HG_HEADS, HG_DK = 4, 128
HG_CHUNK = 32
HG_STEP = 2
SW_HEADS, SW_KV, SW_DIM, WINDOW = 16, 4, 64, 128
N_BUCKETS, MAX_DISTANCE = 32, 128
D_FF = 2816
ATT_BLOCK = 128
SB_QBLOCK = 256
LANES = 128
NEG = -1e30

ADAM_LR, ADAM_B1, ADAM_B2, ADAM_EPS, ADAM_WD, ADAM_STEP = 0.001, 0.9, 0.999, 1e-08, 0.01, 10

VMEM_LIMIT = 56 * 1024 * 1024


def _cparams(*sem):
    return pltpu.CompilerParams(dimension_semantics=sem, vmem_limit_bytes=VMEM_LIMIT)


def _pick(n, cap):
    if n <= cap:
        return n
    best = None
    for d in range(LANES, cap + 1, LANES):
        if n % d == 0:
            best = d
    assert best is not None, (n, cap)
    return best


def _dot(a, b, ca, cb):
    return lax.dot_general(a.astype(BF16), b.astype(BF16), (((ca,), (cb,)), ((), ())),
                           preferred_element_type=F32)


def _split(x, terms):
    parts = []
    for _ in range(terms):
        hi = x.astype(BF16)
        parts.append(hi)
        x = x - hi.astype(F32)
    return parts


def _dot_exact_l(x, m, terms=2):
    out = None
    for p in _split(x, terms):
        t = lax.dot_general(p, m, (((1,), (0,)), ((), ())), preferred_element_type=F32)
        out = t if out is None else out + t
    return out


def _dot_exact_r(m, x, terms=3, cm=1):
    out = None
    for p in _split(x, terms):
        t = lax.dot_general(m, p, (((cm,), (0,)), ((), ())), preferred_element_type=F32)
        out = t if out is None else out + t
    return out


def _sig(x):
    return 1.0 / (1.0 + jnp.exp(-x))


def _iota2(shape, dim):
    return lax.broadcasted_iota(jnp.int32, shape, dim)


def _operand_spec(arr, layer, blk, index):
    if arr.ndim == 2:
        return pl.BlockSpec(blk, index)
    if layer is not None:
        return pl.BlockSpec((None,) + blk, lambda i, j, k: (layer,) + index(i, j, k))
    per_half = arr.shape[2] // blk[1]

    def halves(i, j, k):
        r, c = index(i, j, k)
        return (c // per_half, r, c % per_half)

    return pl.BlockSpec((None,) + blk, halves)


def _matmul(a, b, mode, name, out_dtype=F32, res=None, a_layer=None, b_layer=None):
    def dims(arr, layer):
        if arr.ndim == 2:
            return arr.shape
        return arr.shape[1:] if layer is not None else (arr.shape[1], 2 * arr.shape[2])

    (a0, a1), (b0, b1) = dims(a, a_layer), dims(b, b_layer)
    if mode == "nn":
        M, K, N = a0, a1, b1
    elif mode == "nt":
        M, K, N = a0, a1, b0
    else:
        K, M, N = a0, a1, b1
    cap_m, cap_n, cap_k = 1024, 512, 2048
    tm, tn, tk = _pick(M, cap_m), _pick(N, cap_n), _pick(K, cap_k)
    if a.ndim == 3 and a_layer is None:
        if mode == "tn":
            tm = _pick(a.shape[2], cap_m)
        else:
            tk = _pick(a.shape[2], cap_k)
    if b.ndim == 3 and b_layer is None:
        if mode == "nt":
            tk = _pick(b.shape[2], cap_k)
        else:
            tn = _pick(b.shape[2], cap_n)
    nk = K // tk
    if mode == "tn":
        a_spec = _operand_spec(a, a_layer, (tk, tm), lambda i, j, k: (k, i))
    else:
        a_spec = _operand_spec(a, a_layer, (tm, tk), lambda i, j, k: (i, k))
    if mode == "nt":
        b_spec = _operand_spec(b, b_layer, (tn, tk), lambda i, j, k: (j, k))
    else:
        b_spec = _operand_spec(b, b_layer, (tk, tn), lambda i, j, k: (k, j))
    ca, cb = {"nn": (1, 0), "nt": (1, 1), "tn": (0, 0)}[mode]
    o_spec = pl.BlockSpec((tm, tn), lambda i, j, k: (i, j))

    def body(*refs):
        if res is None:
            a_ref, b_ref, o_ref, acc = refs
        else:
            a_ref, b_ref, r_ref, o_ref, acc = refs
        k = pl.program_id(2)

        @pl.when(k == 0)
        def _():
            acc[...] = jnp.zeros_like(acc)

        acc[...] += _dot(a_ref[...], b_ref[...], ca, cb)

        @pl.when(k == nk - 1)
        def _():
            r = acc[...]
            if res is not None:
                r = r + r_ref[...]
            o_ref[...] = r.astype(out_dtype)

    ins = [a, b] + ([] if res is None else [res])
    in_specs = [a_spec, b_spec] + ([] if res is None else [o_spec])
    return pl.pallas_call(
        body, name=name, grid=(M // tm, N // tn, nk), in_specs=in_specs, out_specs=o_spec,
        out_shape=jax.ShapeDtypeStruct((M, N), out_dtype),
        scratch_shapes=[pltpu.VMEM((tm, tn), F32)],
        compiler_params=_cparams("parallel", "parallel", "arbitrary"))(*ins)


ROW_TILE = 512


def _row_spec(width):
    return pl.BlockSpec((ROW_TILE, width), lambda i: (i, 0))


def _vec_spec(width):
    return pl.BlockSpec((1, width), lambda i: (0, 0))


def _rmsnorm_fwd(h, g, name):
    T, Dm = h.shape

    def body(h_ref, g_ref, o_ref):
        x = h_ref[...]
        r = lax.rsqrt(jnp.mean(x * x, axis=1, keepdims=True) + EPS)
        o_ref[...] = (x * r * g_ref[...]).astype(BF16)

    return pl.pallas_call(
        body, name=name, grid=(T // ROW_TILE,), in_specs=[_row_spec(Dm), _vec_spec(Dm)],
        out_specs=_row_spec(Dm), out_shape=jax.ShapeDtypeStruct((T, Dm), BF16),
        compiler_params=_cparams("parallel"))(h, g)


def _rmsnorm_bwd(h, g, dhn, dres, name):
    T, Dm = h.shape

    def body(h_ref, g_ref, dy_ref, dr_ref, dh_ref, dg_ref):
        i = pl.program_id(0)
        x = h_ref[...]
        dy = dy_ref[...]
        r = lax.rsqrt(jnp.mean(x * x, axis=1, keepdims=True) + EPS)
        gdy = dy * g_ref[...]
        m = jnp.mean(x * gdy, axis=1, keepdims=True)
        dh_ref[...] = dr_ref[...] + r * gdy - x * (r * r * r * m)
        part = jnp.sum(dy * x * r, axis=0, keepdims=True)

        @pl.when(i == 0)
        def _():
            dg_ref[...] = part

        @pl.when(i > 0)
        def _():
            dg_ref[...] += part

    return pl.pallas_call(
        body, name=name, grid=(T // ROW_TILE,),
        in_specs=[_row_spec(Dm), _vec_spec(Dm), _row_spec(Dm), _row_spec(Dm)],
        out_specs=[_row_spec(Dm), _vec_spec(Dm)],
        out_shape=[jax.ShapeDtypeStruct((T, Dm), F32), jax.ShapeDtypeStruct((1, Dm), F32)],
        compiler_params=_cparams("arbitrary"))(h, g, dhn, dres)


def _ple_fwd(h, z, pp, name):
    T, Dm = h.shape

    def body(h_ref, z_ref, p_ref, o_ref):
        o_ref[...] = h_ref[...] + _sig(z_ref[...]) * p_ref[...]

    return pl.pallas_call(
        body, name=name, grid=(T // ROW_TILE,), in_specs=[_row_spec(Dm)] * 3, out_specs=_row_spec(Dm),
        out_shape=jax.ShapeDtypeStruct((T, Dm), F32), compiler_params=_cparams("parallel"))(h, z, pp)


def _ple_bwd(dh, z, pp, name):
    T, Dm = dh.shape

    def body(dh_ref, z_ref, p_ref, dz_ref, dp_ref):
        s = _sig(z_ref[...])
        d = dh_ref[...]
        dz_ref[...] = d * p_ref[...] * s * (1.0 - s)
        dp_ref[...] = d * s

    return pl.pallas_call(
        body, name=name, grid=(T // ROW_TILE,), in_specs=[_row_spec(Dm)] * 3, out_specs=[_row_spec(Dm)] * 2,
        out_shape=[jax.ShapeDtypeStruct((T, Dm), F32)] * 2, compiler_params=_cparams("parallel"))(dh, z, pp)


def _loss_fwd_bwd(y, target, name):
    T, Dm = y.shape

    def body(y_ref, t_ref, l_ref, d_ref):
        i = pl.program_id(0)
        e = y_ref[...] - t_ref[...]
        d_ref[...] = e * (1.0 / Dm)
        part = jnp.full((8, LANES), 0.5 / Dm, F32) * jnp.sum(e * e)

        @pl.when(i == 0)
        def _():
            l_ref[...] = part

        @pl.when(i > 0)
        def _():
            l_ref[...] += part

    return pl.pallas_call(
        body, name=name, grid=(T // ROW_TILE,), in_specs=[_row_spec(Dm)] * 2,
        out_specs=[pl.BlockSpec((8, LANES), lambda i: (0, 0)), _row_spec(Dm)],
        out_shape=[jax.ShapeDtypeStruct((8, LANES), F32), jax.ShapeDtypeStruct((T, Dm), F32)],
        compiler_params=_cparams("arbitrary"))(y, target)


def _head_mean_matrix():
    r = _iota2((LANES, LANES), 0) >= SW_DIM
    c = _iota2((LANES, LANES), 1) >= SW_DIM
    return jnp.where(r == c, 1.0 / SW_DIM, 0.0).astype(BF16)


def _headnorm_fwd(x, g_lane, name):
    T = x.shape[0]
    C = g_lane.shape[1]

    def body(x_ref, g_ref, y_ref):
        xv = x_ref[...]
        ms = _dot_exact_l(xv * xv, _head_mean_matrix())
        y_ref[...] = xv * lax.rsqrt(ms + EPS) * g_ref[...]

    spec = pl.BlockSpec((ROW_TILE, LANES), lambda j, i: (i, j))
    return pl.pallas_call(
        body, name=name, grid=(C // LANES, T // ROW_TILE),
        in_specs=[spec, pl.BlockSpec((1, LANES), lambda j, i: (0, j))], out_specs=spec,
        out_shape=jax.ShapeDtypeStruct((T, C), F32), compiler_params=_cparams("parallel", "parallel"))(x, g_lane)


def _headnorm_bwd(x, g_lane, dy_full, name):
    T = x.shape[0]
    C = g_lane.shape[1]

    def body(x_ref, g_ref, dy_ref, dx_ref, dg_ref):
        i = pl.program_id(1)
        xv = x_ref[...]
        dy = dy_ref[...]
        bd = _head_mean_matrix()
        r = lax.rsqrt(_dot_exact_l(xv * xv, bd) + EPS)
        gdy = dy * g_ref[...]
        m = _dot_exact_l(xv * gdy, bd)
        dx_ref[...] = r * gdy - xv * (r * r * r * m)
        part = jnp.sum(dy * xv * r, axis=0, keepdims=True)

        @pl.when(i == 0)
        def _():
            dg_ref[...] = part

        @pl.when(i > 0)
        def _():
            dg_ref[...] += part

    spec = pl.BlockSpec((ROW_TILE, LANES), lambda j, i: (i, j))
    vspec = pl.BlockSpec((1, LANES), lambda j, i: (0, j))
    return pl.pallas_call(
        body, name=name, grid=(C // LANES, T // ROW_TILE), in_specs=[spec, vspec, spec],
        out_specs=[spec, vspec],
        out_shape=[jax.ShapeDtypeStruct(dy_full.shape, F32), jax.ShapeDtypeStruct((1, C), F32)],
        input_output_aliases={2: 0}, compiler_params=_cparams("parallel", "arbitrary"))(x, g_lane, dy_full)


CONV_TILE = 128


def _shift_down(x, k):
    rows = _iota2(x.shape, 0)
    return jnp.where(rows >= k, pltpu.roll(x, k, 0), 0.0)


def _shift_up(x, k):
    n = x.shape[0]
    rows = _iota2(x.shape, 0)
    return jnp.where(rows < n - k, pltpu.roll(x, n - k, 0), 0.0)


def _conv3(u, w_ref, b_ref):
    return w_ref[2:3, :] * u + w_ref[1:2, :] * _shift_down(u, 1) + w_ref[0:1, :] * _shift_down(u, 2) + b_ref[...]


def _convglu_fwd(u, cw, cb, S, name):
    T = u.shape[0]
    nf = D_FF // CONV_TILE

    def body(ug_ref, uu_ref, wg_ref, wu_ref, bg_ref, bu_ref, a_ref):
        yg = _conv3(ug_ref[...], wg_ref, bg_ref)
        yu = _conv3(uu_ref[...], wu_ref, bu_ref)
        a_ref[...] = (yg * _sig(yg) * yu).astype(BF16)

    def blk(rows, off):
        return pl.BlockSpec((rows, CONV_TILE), (lambda b, j: (b, j + off)) if rows == S else (lambda b, j: (0, j + off)))

    return pl.pallas_call(
        body, name=name, grid=(T // S, nf),
        in_specs=[blk(S, 0), blk(S, nf), blk(3, 0), blk(3, nf), blk(1, 0), blk(1, nf)],
        out_specs=blk(S, 0), out_shape=jax.ShapeDtypeStruct((T, D_FF), BF16),
        compiler_params=_cparams("parallel", "parallel"))(u, u, cw, cw, cb, cb)


def _convglu_bwd(u, da, cw, cb, S, name):
    T = u.shape[0]
    nf = D_FF // CONV_TILE

    def body(ug_ref, uu_ref, da_ref, wg_ref, wu_ref, bg_ref, bu_ref, du_ref, dw_ref, db_ref):
        b = pl.program_id(1)
        ug, uu = ug_ref[...], uu_ref[...]
        yg = _conv3(ug, wg_ref, bg_ref)
        yu = _conv3(uu, wu_ref, bu_ref)
        s = _sig(yg)
        da_v = da_ref[...]
        for half, (uv, w_ref, dy) in enumerate(((ug, wg_ref, da_v * yu * (s * (1.0 + yg * (1.0 - s)))),
                                                (uu, wu_ref, da_v * yg * s))):
            du_ref[half] = w_ref[2:3, :] * dy + w_ref[1:2, :] * _shift_up(dy, 1) + w_ref[0:1, :] * _shift_up(dy, 2)
            dws = [jnp.sum(dy * _shift_down(uv, 2), axis=0, keepdims=True),
                   jnp.sum(dy * _shift_down(uv, 1), axis=0, keepdims=True),
                   jnp.sum(dy * uv, axis=0, keepdims=True)]
            dbv = jnp.sum(dy, axis=0, keepdims=True)

            @pl.when(b == 0)
            def _():
                for k in range(3):
                    dw_ref[half, k:k + 1, :] = dws[k]
                db_ref[half] = dbv

            @pl.when(b > 0)
            def _():
                for k in range(3):
                    dw_ref[half, k:k + 1, :] += dws[k]
                db_ref[half] += dbv

    def blk(rows, off):
        return pl.BlockSpec((rows, CONV_TILE), (lambda j, b: (b, j + off)) if rows == S else (lambda j, b: (0, j + off)))

    def both(rows):
        return pl.BlockSpec((2, rows, CONV_TILE), (lambda j, b: (0, b, j)) if rows == S else (lambda j, b: (0, 0, j)))

    return pl.pallas_call(
        body, name=name, grid=(nf, T // S),
        in_specs=[blk(S, 0), blk(S, nf), blk(S, 0), blk(3, 0), blk(3, nf), blk(1, 0), blk(1, nf)],
        out_specs=[both(S), both(3), both(1)],
        out_shape=[jax.ShapeDtypeStruct((2, T, D_FF), F32), jax.ShapeDtypeStruct((2, 3, D_FF), F32),
                   jax.ShapeDtypeStruct((2, 1, D_FF), F32)],
        compiler_params=_cparams("parallel", "arbitrary"))(u, u, da, cw, cw, cb, cb)


def _sb_scores(qb, kblk, on_diag_mask):
    z = _dot(qb, kblk, 1, 1) * (SB_DIM ** -0.5)
    l1 = jnp.log(1.0 + jnp.exp(-jnp.abs(z)))
    ls = jnp.minimum(z, 0.0) - l1
    lk = jnp.where(on_diag_mask, ls - z, 0.0)
    return ls, lk


def _sb_fwd(proj, S, name):
    T = proj.shape[0]
    BQ, BK = SB_QBLOCK, ATT_BLOCK
    nq = S // BQ
    nhp = SB_HEADS // 2
    heads = [slice(h * SB_DIM, (h + 1) * SB_DIM) for h in range(2)]

    def body(q_ref, k_ref, v_ref, o_ref, tot_ref):
        ahead = _iota2((BQ, BK), 1) - _iota2((BQ, BK), 0)
        upper = (_iota2((BK, BK), 0) > _iota2((BK, BK), 1)).astype(BF16)

        def qloop(iq, carry):
            q0 = pl.multiple_of(iq * BQ, BQ)
            rows = pl.ds(q0, BQ)
            qbs = [q_ref[rows, sl] for sl in heads]
            nkb = (iq + 1) * (BQ // BK)

            def kloop(jj, kc):
                k0 = pl.multiple_of((nkb - 1 - jj) * BK, BK)
                krows = pl.ds(k0, BK)
                mask = ahead < q0 - k0
                out = []
                for sl, qb, (run, acc) in zip(heads, qbs, kc):
                    ls, lk = _sb_scores(qb, k_ref[krows, sl], mask)
                    later = _dot_exact_l(lk, upper)
                    w = jnp.where(mask, jnp.exp(ls + later + run), 0.0)
                    out.append((run + jnp.sum(lk, axis=1, keepdims=True), acc + _dot(w, v_ref[krows, sl], 1, 0)))
                return tuple(out)

            init = (jnp.zeros((BQ, 1), F32), jnp.zeros((BQ, SB_DIM), F32))
            res = lax.fori_loop(0, nkb, kloop, (init, init))
            for h, (sl, (run, acc)) in enumerate(zip(heads, res)):
                o_ref[rows, sl] = acc
                tot_ref[rows, h:h + 1] = run
            return carry

        lax.fori_loop(0, nq, qloop, 0)

    def spec(off):
        return pl.BlockSpec((S, LANES), lambda b, hp: (b, hp + off))

    return pl.pallas_call(
        body, name=name, grid=(T // S, nhp), in_specs=[spec(0), spec(nhp), spec(2 * nhp)],
        out_specs=[spec(0), pl.BlockSpec((None, S, 2), lambda b, hp: (hp, b, 0))],
        out_shape=[jax.ShapeDtypeStruct((T, 2 * SB_HEADS * SB_DIM), F32), jax.ShapeDtypeStruct((nhp, T, 2), F32)],
        compiler_params=_cparams("parallel", "parallel"))(proj, proj, proj)


def _sb_bwd(proj, tot, dcat, S, name):
    T, width = proj.shape
    BQ, BK = SB_QBLOCK, ATT_BLOCK
    nq = S // BQ
    nhp = SB_HEADS // 2
    scale = SB_DIM ** -0.5
    heads = [slice(h * SB_DIM, (h + 1) * SB_DIM) for h in range(2)]

    def body(q_ref, k_ref, v_ref, tot_ref, do_ref, dp_hbm, dq_s, dk_s, dv_s, sems):
        b, hp = pl.program_id(0), pl.program_id(1)
        ahead = _iota2((BQ, BK), 1) - _iota2((BQ, BK), 0)
        r, c = _iota2((BK, BK), 0), _iota2((BK, BK), 1)
        upto = (r <= c).astype(BF16)
        earlier = (r < c).astype(BF16)
        dk_s[...] = jnp.zeros_like(dk_s)
        dv_s[...] = jnp.zeros_like(dv_s)

        def qloop(iq, carry):
            q0 = pl.multiple_of(iq * BQ, BQ)
            rows = pl.ds(q0, BQ)
            qbs = [q_ref[rows, sl] for sl in heads]
            dobs = [do_ref[rows, sl] for sl in heads]
            totals = [tot_ref[rows, h:h + 1] for h in range(2)]

            def kloop(kb, kc):
                k0 = pl.multiple_of(kb * BK, BK)
                krows = pl.ds(k0, BK)
                mask = ahead < q0 - k0
                out = []
                for sl, qb, dob, total, (run, grun, dq) in zip(heads, qbs, dobs, totals, kc):
                    kblk = k_ref[krows, sl]
                    ls, lk = _sb_scores(qb, kblk, mask)
                    later = total - (_dot_exact_l(lk, upto) + run)
                    w = jnp.where(mask, jnp.exp(ls + later), 0.0)
                    g = w * _dot(dob, v_ref[krows, sl], 1, 1)
                    before = _dot_exact_l(g, earlier) + grun
                    beta = jnp.exp(ls)
                    dz = jnp.where(mask, g * (1.0 - beta) - beta * before, 0.0) * scale
                    dv_s[krows, sl] += _dot(w, dob, 0, 0)
                    dk_s[krows, sl] += _dot(dz, qb, 0, 0)
                    out.append((run + jnp.sum(lk, axis=1, keepdims=True),
                                grun + jnp.sum(g, axis=1, keepdims=True), dq + _dot(dz, kblk, 1, 0)))
                return tuple(out)

            zero = jnp.zeros((BQ, 1), F32)
            init = (zero, zero, jnp.zeros((BQ, SB_DIM), F32))
            res = lax.fori_loop(0, (iq + 1) * (BQ // BK), kloop, (init, init))
            for sl, (_, _, dq) in zip(heads, res):
                dq_s[rows, sl] = dq
            return carry

        lax.fori_loop(0, nq, qloop, 0)
        r0 = pl.multiple_of(b * S, S)
        copies = []
        for n, buf in enumerate((dq_s, dk_s, dv_s)):
            c0 = pl.multiple_of((hp + n * nhp) * LANES, LANES)
            copies.append(pltpu.make_async_copy(buf, dp_hbm.at[pl.ds(r0, S), pl.ds(c0, LANES)], sems.at[n]))
        for cp in copies:
            cp.start()
        for cp in copies:
            cp.wait()

    def spec(off):
        return pl.BlockSpec((S, LANES), lambda b, hp: (b, hp + off))

    return pl.pallas_call(
        body, name=name, grid=(T // S, nhp),
        in_specs=[spec(0), spec(nhp), spec(2 * nhp), pl.BlockSpec((None, S, 2), lambda b, hp: (hp, b, 0)), spec(0)],
        out_specs=ANY, out_shape=jax.ShapeDtypeStruct((T, width), F32),
        scratch_shapes=[pltpu.VMEM((S, LANES), F32)] * 3 + [pltpu.SemaphoreType.DMA((3,))],
        compiler_params=_cparams("arbitrary", "arbitrary"))(proj, proj, proj, tot, dcat)


HG_COL0 = 3 * SB_HEADS * SB_DIM // LANES


def _hg_gates(q, fp, lbv):
    sg = _sig(fp)
    f = lbv + (1.0 - lbv) * sg
    kk = (1.0 - lbv) * _sig(-fp)
    sq = _sig(q)
    return sg, f, kk, sq


def _hg_chunk(qs, kk, lf, incl):
    C = HG_CHUNK
    b = _dot_exact_r(incl, lf)
    bl = b[C - 1:C, :]
    bm = b[C // 2 - 1:C // 2, :]
    e_t = jnp.exp(b - bm)
    e_s = jnp.exp(bm - b)
    e_i = jnp.exp(b)
    e_e = jnp.exp(bl - b)
    return bl, e_t, e_s, e_i, e_e


def _hgrn_fwd(proj, cat, lb, hgn, S, name):
    T = proj.shape[0]
    B = T // S
    C = HG_CHUNK
    NC = S // C

    def body(q_ref, f_ref, i_ref, g_ref, lb_ref, hgn_ref, cat_hbm, ob_ref, oraw_ref, st_ref, state):
        del cat_hbm
        state[...] = jnp.zeros_like(state)
        row, col = _iota2((C, C), 0), _iota2((C, C), 1)
        causal = row >= col
        incl = causal.astype(BF16)

        def chunk(c, carry):
            rows = pl.ds(pl.multiple_of(c * C, C), C)
            for hh in range(HG_STEP):
                hs = slice(hh * LANES, (hh + 1) * LANES)
                q, iv, gv = q_ref[rows, hs], i_ref[rows, hs], g_ref[rows, hs]
                _, f, kk, sq = _hg_gates(q, f_ref[rows, hs], lb_ref[:, hs])
                qs = q * sq
                bl, e_t, e_s, e_i, e_e = _hg_chunk(qs, kk, jnp.log(f), incl)
                p = jnp.where(causal, _dot(qs * e_t, kk * e_s, 1, 1), 0.0)
                st = state[hh]
                st_ref[0, hh, c] = st
                o = _dot(qs * e_i, st, 1, 1) + _dot(p, iv, 1, 0)
                state[hh] = st * jnp.exp(bl) + _dot(iv, kk * e_e, 0, 0)
                oraw_ref[rows, hs] = o
                r = lax.rsqrt(jnp.mean(o * o, axis=1, keepdims=True) + EPS)
                ob_ref[rows, hs] = o * r * hgn_ref[...] * (gv * _sig(gv))
            return carry

        lax.fori_loop(0, NC, chunk, 0)

    width = HG_STEP * LANES
    col0 = HG_COL0 * LANES // width
    nstep = HG_HEADS // HG_STEP

    def spec(off):
        return pl.BlockSpec((S, width), lambda b, h: (b, h + off))

    return pl.pallas_call(
        body, name=name, grid=(B, nstep),
        in_specs=[spec(col0), spec(col0 + nstep), spec(col0 + 2 * nstep), spec(col0 + 3 * nstep),
                  pl.BlockSpec((1, width), lambda b, h: (0, h)), pl.BlockSpec((1, LANES), lambda b, h: (0, 0)), ANY],
        out_specs=[spec(nstep), spec(0),
                   pl.BlockSpec((1, HG_STEP, NC, LANES, LANES), lambda b, h: (b, h, 0, 0, 0))],
        out_shape=[jax.ShapeDtypeStruct(cat.shape, F32), jax.ShapeDtypeStruct((T, HG_HEADS * LANES), F32),
                   jax.ShapeDtypeStruct((B, HG_HEADS, NC, LANES, LANES), F32)],
        scratch_shapes=[pltpu.VMEM((HG_STEP, LANES, LANES), F32)],
        input_output_aliases={6: 0},
        compiler_params=_cparams("parallel", "parallel"))(proj, proj, proj, proj, lb, hgn, cat)


def _hgrn_bwd(proj, oraw, dcat, states, lb, hgn, dproj, S, name):
    T = proj.shape[0]
    B = T // S
    C = HG_CHUNK
    NC = S // C

    def body(q_ref, f_ref, i_ref, g_ref, oraw_ref, dy_ref, st_ref, lb_ref, hgn_ref, dp_in,
             dp_hbm, dlb_ref, dhgn_ref, dstate, dq_s, df_s, di_s, dg_s, sems):
        del dp_in
        h, b = pl.program_id(0), pl.program_id(1)
        row, col = _iota2((C, C), 0), _iota2((C, C), 1)
        causal = row >= col
        incl = causal.astype(BF16)
        last_row = _iota2((C, LANES), 0) == C - 1
        hg = hgn_ref[...]
        dstate[...] = jnp.zeros_like(dstate)

        @pl.when(b == 0)
        def _():
            dlb_ref[...] = jnp.zeros_like(dlb_ref)

        @pl.when(jnp.logical_and(b == 0, h == 0))
        def _():
            dhgn_ref[...] = jnp.zeros_like(dhgn_ref)

        def chunk(cc, carry):
            c = NC - 1 - cc
            rows = pl.ds(pl.multiple_of(c * C, C), C)
            for hh in range(HG_STEP):
                hs = slice(hh * LANES, (hh + 1) * LANES)
                lbv = lb_ref[:, hs]
                q, fp, iv, gv = q_ref[rows, hs], f_ref[rows, hs], i_ref[rows, hs], g_ref[rows, hs]
                o = oraw_ref[rows, hs]
                dy = dy_ref[rows, hs]
                r = lax.rsqrt(jnp.mean(o * o, axis=1, keepdims=True) + EPS)
                on = o * r
                sgv = _sig(gv)
                silu_g = gv * sgv
                dg_s[rows, hs] = dy * on * hg * (sgv * (1.0 + gv * (1.0 - sgv)))
                dhgn_ref[...] += jnp.sum(dy * on * silu_g, axis=0, keepdims=True)
                dn = dy * hg * silu_g
                do = r * dn - o * (r * r * r * jnp.mean(o * dn, axis=1, keepdims=True))
                sg, f, kk, sq = _hg_gates(q, fp, lbv)
                qs = q * sq
                bl, e_t, e_s, e_i, e_e = _hg_chunk(qs, kk, jnp.log(f), incl)
                qd, kd, qi, ke = qs * e_t, kk * e_s, qs * e_i, kk * e_e
                p = jnp.where(causal, _dot(qd, kd, 1, 1), 0.0)
                st = st_ref[0, hh, c]
                dst = dstate[hh]
                ebl = jnp.exp(bl)
                dqi = _dot(do, st, 1, 0)
                dp = jnp.where(causal, _dot(do, iv, 1, 1), 0.0)
                di_s[rows, hs] = _dot(p, do, 0, 0) + _dot(ke, dst, 1, 1)
                dqd = _dot(dp, kd, 1, 0)
                dkd = _dot(dp, qd, 0, 0)
                dke = _dot(iv, dst, 1, 0)
                dbl = jnp.sum(st * dst, axis=0, keepdims=True) * ebl + jnp.sum(dke * ke, axis=0, keepdims=True)
                db = dqd * qd - dkd * kd + dqi * qi - dke * ke + jnp.where(last_row, dbl, 0.0)
                dqs = dqd * e_t + dqi * e_i
                dkk = dkd * e_s + dke * e_e
                dlf = _dot_exact_r(incl, db, cm=0)
                dstate[hh] = _dot(do, qi, 0, 0) + dst * ebl
                oms = 1.0 - sg
                dfd = dlf / f
                df_s[rows, hs] = (dfd - dkk) * (1.0 - lbv) * sg * oms
                dq_s[rows, hs] = dqs * (sq * (1.0 + q * (1.0 - sq)))
                dlb_ref[:, hs] += jnp.sum((dfd - dkk) * oms, axis=0, keepdims=True)
            return carry

        lax.fori_loop(0, NC, chunk, 0)
        r0 = pl.multiple_of(b * S, S)
        copies = []
        for n, buf in enumerate((dq_s, df_s, di_s, dg_s)):
            c0 = pl.multiple_of((col0 + n * nstep + h) * width, width)
            copies.append(pltpu.make_async_copy(buf, dp_hbm.at[pl.ds(r0, S), pl.ds(c0, width)], sems.at[n]))
        for cp in copies:
            cp.start()
        for cp in copies:
            cp.wait()

    width = HG_STEP * LANES
    col0 = HG_COL0 * LANES // width
    nstep = HG_HEADS // HG_STEP

    def spec(off):
        return pl.BlockSpec((S, width), lambda h, b: (b, h + off))

    return pl.pallas_call(
        body, name=name, grid=(nstep, B),
        in_specs=[spec(col0), spec(col0 + nstep), spec(col0 + 2 * nstep), spec(col0 + 3 * nstep), spec(0), spec(nstep),
                  pl.BlockSpec((1, HG_STEP, NC, LANES, LANES), lambda h, b: (b, h, 0, 0, 0)),
                  pl.BlockSpec((1, width), lambda h, b: (0, h)), pl.BlockSpec((1, LANES), lambda h, b: (0, 0)), ANY],
        out_specs=[ANY, pl.BlockSpec((1, width), lambda h, b: (0, h)), pl.BlockSpec((1, LANES), lambda h, b: (0, 0))],
        out_shape=[jax.ShapeDtypeStruct(dproj.shape, F32), jax.ShapeDtypeStruct((1, HG_HEADS * LANES), F32),
                   jax.ShapeDtypeStruct((1, LANES), F32)],
        scratch_shapes=[pltpu.VMEM((HG_STEP, LANES, LANES), F32)] + [pltpu.VMEM((S, width), F32)] * 4
        + [pltpu.SemaphoreType.DMA((4,))],
        input_output_aliases={9: 0},
        compiler_params=_cparams("arbitrary", "arbitrary"))(proj, proj, proj, proj, oraw, dcat, states, lb, hgn, dproj)


def _lower_bound_fwd(logits, name):
    assert logits.shape[0] == 2

    def body(l_ref, o_ref):
        l0, l1 = l_ref[0:1, :], l_ref[1:2, :]
        m = jnp.maximum(l0, l1)
        e0, e1 = jnp.exp(l0 - m), jnp.exp(l1 - m)
        o_ref[0:1, :] = jnp.zeros_like(l0)
        o_ref[1:2, :] = e1 / (e0 + e1)

    return pl.pallas_call(body, name=name, out_shape=jax.ShapeDtypeStruct(logits.shape, F32))(logits)


def _lower_bound_bwd(logits, dlb, name):
    def body(l_ref, d_ref, o_ref):
        l0, l1 = l_ref[0:1, :], l_ref[1:2, :]
        m = jnp.maximum(l0, l1)
        e0, e1 = jnp.exp(l0 - m), jnp.exp(l1 - m)
        s1 = e1 / (e0 + e1)
        t = s1 * (1.0 - s1) * d_ref[1:2, :]
        o_ref[0:1, :] = -t
        o_ref[1:2, :] = t

    return pl.pallas_call(body, name=name, out_shape=jax.ShapeDtypeStruct(logits.shape, F32))(logits, dlb)


def _bucket_thresholds():
    dist = np.arange(WINDOW)
    max_exact = N_BUCKETS // 2
    large = max_exact + (np.log(np.maximum(dist, max_exact) / max_exact) / math.log(MAX_DISTANCE / max_exact)
                         * (N_BUCKETS - max_exact)).astype(np.int32)
    bucket = np.where(dist < max_exact, dist, np.minimum(large, N_BUCKETS - 1))
    assert np.all(np.diff(bucket) >= 0)
    return [int(np.argmax(bucket >= k)) if np.any(bucket >= k) else 10 ** 6 for k in range(1, N_BUCKETS)]


def _band_bucket():
    dist = _iota2((WINDOW, 2 * WINDOW), 0) + WINDOW - _iota2((WINDOW, 2 * WINDOW), 1)
    bucket = jnp.zeros((WINDOW, 2 * WINDOW), jnp.int32)
    for thr in _bucket_thresholds():
        bucket = bucket + (dist >= thr).astype(jnp.int32)
    band = jnp.logical_and(dist >= 0, dist < WINDOW)
    return bucket, band


def _bias_build(rel_bias, name):
    def body(rb_ref, o_ref):
        h = pl.program_id(0)
        bucket, _ = _band_bucket()
        bias = jnp.zeros((WINDOW, 2 * WINDOW), F32)
        for k in range(N_BUCKETS):
            bias = jnp.where(bucket == k, rb_ref[k, h], bias)
        o_ref[0] = bias

    return pl.pallas_call(
        body, name=name, grid=(SW_HEADS,), in_specs=[pl.BlockSpec(memory_space=pltpu.SMEM)],
        out_specs=pl.BlockSpec((1, WINDOW, 2 * WINDOW), lambda h: (h, 0, 0)),
        out_shape=jax.ShapeDtypeStruct((SW_HEADS, WINDOW, 2 * WINDOW), F32),
        compiler_params=_cparams("parallel"))(rel_bias)


def _bias_reduce(dbias, name):
    def body(d_ref, o_ref):
        bucket, band = _band_bucket()
        d = jnp.where(band, d_ref[0], 0.0)
        lane = _iota2((1, LANES), 1)
        out = jnp.zeros((1, LANES), F32)
        for k in range(N_BUCKETS):
            out = jnp.where(lane == k, jnp.sum(jnp.where(bucket == k, d, 0.0)), out)
        o_ref[0] = out

    return pl.pallas_call(
        body, name=name, grid=(SW_HEADS,), in_specs=[pl.BlockSpec((1, WINDOW, 2 * WINDOW), lambda h: (h, 0, 0))],
        out_specs=pl.BlockSpec((1, 1, LANES), lambda h: (h, 0, 0)),
        out_shape=jax.ShapeDtypeStruct((SW_HEADS, 1, LANES), F32), compiler_params=_cparams("parallel"))(dbias)


SW_Q_COLS = SW_HEADS * SW_DIM
SW_K_BLOCK0 = SW_Q_COLS // LANES
SW_V_BLOCK0 = SW_K_BLOCK0 + SW_KV * SW_DIM // LANES
SW_STEP_HEADS = 8


def _swa_probs(qb, kprev, kcur, bias_ref, hl, sink, mprev, mcur):
    scale = SW_DIM ** -0.5
    lp = jnp.where(mprev, _dot(qb, kprev, 1, 1) * scale + bias_ref[hl, :, 0:WINDOW], NEG)
    lc = jnp.where(mcur, _dot(qb, kcur, 1, 1) * scale + bias_ref[hl, :, WINDOW:2 * WINDOW], NEG)
    m = jnp.maximum(jnp.maximum(jnp.max(lp, axis=1, keepdims=True), jnp.max(lc, axis=1, keepdims=True)), sink)
    ep, ec = jnp.exp(lp - m), jnp.exp(lc - m)
    es = jnp.exp(sink - m)
    den = jnp.sum(ep, axis=1, keepdims=True) + jnp.sum(ec, axis=1, keepdims=True) + es
    return ep, ec, es, den


def _swa_fwd(qkn, proj, bias, sinks, S, name):
    T = qkn.shape[0]
    W = WINDOW
    nb = S // W

    def body(q_ref, k_ref, v_ref, bias_ref, sink_ref, o_ref):
        kp = pl.program_id(1)
        row, col = _iota2((W, W), 0), _iota2((W, W), 1)
        mcur = col <= row
        above = col > row

        def blk(n, carry):
            rows = pl.ds(pl.multiple_of(n * W, W), W)
            prow = pl.ds(pl.multiple_of(jnp.maximum(n - 1, 0) * W, W), W)
            mprev = jnp.logical_and(above, n > 0)
            for kvh in range(2):
                ksl = slice(kvh * SW_DIM, (kvh + 1) * SW_DIM)
                kcur, kprev = k_ref[rows, ksl], k_ref[prow, ksl]
                vcur, vprev = v_ref[rows, ksl], v_ref[prow, ksl]
                for g in range(4):
                    hl = kvh * 4 + g
                    qsl = slice(hl * SW_DIM, (hl + 1) * SW_DIM)
                    sink = sink_ref[kp * SW_STEP_HEADS + hl]
                    ep, ec, _, den = _swa_probs(q_ref[rows, qsl], kprev, kcur, bias_ref, hl, sink, mprev, mcur)
                    o_ref[rows, qsl] = (_dot(ep, vprev, 1, 0) + _dot(ec, vcur, 1, 0)) / den
            return carry

        lax.fori_loop(0, nb, blk, 0)

    return pl.pallas_call(
        body, name=name, grid=(T // S, 2),
        in_specs=[pl.BlockSpec((S, 4 * LANES), lambda b, kp: (b, kp)),
                  pl.BlockSpec((S, LANES), lambda b, kp: (b, SW_K_BLOCK0 + kp)),
                  pl.BlockSpec((S, LANES), lambda b, kp: (b, SW_V_BLOCK0 + kp)),
                  pl.BlockSpec((SW_STEP_HEADS, W, 2 * W), lambda b, kp: (kp, 0, 0)),
                  pl.BlockSpec(memory_space=pltpu.SMEM)],
        out_specs=pl.BlockSpec((S, 4 * LANES), lambda b, kp: (b, kp)),
        out_shape=jax.ShapeDtypeStruct((T, SW_Q_COLS), F32),
        compiler_params=_cparams("parallel", "parallel"))(qkn, qkn, proj, bias, sinks)


def _swa_bwd(qkn, proj, bias, sinks, do, S, name):
    T, width = proj.shape
    W = WINDOW
    nb = S // W
    scale = SW_DIM ** -0.5

    def body(q_ref, k_ref, v_ref, bias_ref, sink_ref, do_ref, dp_hbm, dbias_ref, dsink_ref,
             dq_s, dk_s, dv_s, sems):
        kp, b = pl.program_id(0), pl.program_id(1)
        row, col = _iota2((W, W), 0), _iota2((W, W), 1)
        mcur = col <= row
        above = col > row
        dk_s[...] = jnp.zeros_like(dk_s)
        dv_s[...] = jnp.zeros_like(dv_s)

        @pl.when(b == 0)
        def _():
            dbias_ref[...] = jnp.zeros_like(dbias_ref)
            dsink_ref[...] = jnp.zeros_like(dsink_ref)

        def blk(n, carry):
            rows = pl.ds(pl.multiple_of(n * W, W), W)
            prow = pl.ds(pl.multiple_of(jnp.maximum(n - 1, 0) * W, W), W)
            mprev = jnp.logical_and(above, n > 0)
            for kvh in range(2):
                ksl = slice(kvh * SW_DIM, (kvh + 1) * SW_DIM)
                kcur, kprev = k_ref[rows, ksl], k_ref[prow, ksl]
                vcur, vprev = v_ref[rows, ksl], v_ref[prow, ksl]
                for g in range(4):
                    hl = kvh * 4 + g
                    qsl = slice(hl * SW_DIM, (hl + 1) * SW_DIM)
                    sink = sink_ref[kp * SW_STEP_HEADS + hl]
                    qb = q_ref[rows, qsl]
                    ep, ec, es, den = _swa_probs(qb, kprev, kcur, bias_ref, hl, sink, mprev, mcur)
                    inv = 1.0 / den
                    pp, pc = ep * inv, ec * inv
                    dob = do_ref[rows, qsl]
                    dpp, dpc = _dot(dob, vprev, 1, 1), _dot(dob, vcur, 1, 1)
                    total = jnp.sum(pp * dpp, axis=1, keepdims=True) + jnp.sum(pc * dpc, axis=1, keepdims=True)
                    dlp = pp * (dpp - total)
                    dlc = pc * (dpc - total)
                    dsink_ref[hl:hl + 1, :] += jnp.zeros((1, LANES), F32) - jnp.sum(es * inv * total)
                    dbias_ref[hl, :, 0:W] += dlp
                    dbias_ref[hl, :, W:2 * W] += dlc
                    dq_s[rows, qsl] = (_dot(dlp, kprev, 1, 0) + _dot(dlc, kcur, 1, 0)) * scale
                    dk_s[prow, ksl] += _dot(dlp, qb, 0, 0) * scale
                    dk_s[rows, ksl] += _dot(dlc, qb, 0, 0) * scale
                    dv_s[prow, ksl] += _dot(pp, dob, 0, 0)
                    dv_s[rows, ksl] += _dot(pc, dob, 0, 0)
            return carry

        lax.fori_loop(0, nb, blk, 0)
        r0 = pl.multiple_of(b * S, S)
        cq = pl.multiple_of(kp * 4 * LANES, LANES)
        ck = pl.multiple_of((SW_K_BLOCK0 + kp) * LANES, LANES)
        cv = pl.multiple_of((SW_V_BLOCK0 + kp) * LANES, LANES)
        copies = [pltpu.make_async_copy(dq_s, dp_hbm.at[pl.ds(r0, S), pl.ds(cq, 4 * LANES)], sems.at[0]),
                  pltpu.make_async_copy(dk_s, dp_hbm.at[pl.ds(r0, S), pl.ds(ck, LANES)], sems.at[1]),
                  pltpu.make_async_copy(dv_s, dp_hbm.at[pl.ds(r0, S), pl.ds(cv, LANES)], sems.at[2])]
        for cp in copies:
            cp.start()
        for cp in copies:
            cp.wait()

    qspec = pl.BlockSpec((S, 4 * LANES), lambda kp, b: (b, kp))
    return pl.pallas_call(
        body, name=name, grid=(2, T // S),
        in_specs=[qspec, pl.BlockSpec((S, LANES), lambda kp, b: (b, SW_K_BLOCK0 + kp)),
                  pl.BlockSpec((S, LANES), lambda kp, b: (b, SW_V_BLOCK0 + kp)),
                  pl.BlockSpec((SW_STEP_HEADS, W, 2 * W), lambda kp, b: (kp, 0, 0)),
                  pl.BlockSpec(memory_space=pltpu.SMEM), qspec],
        out_specs=[ANY, pl.BlockSpec((SW_STEP_HEADS, W, 2 * W), lambda kp, b: (kp, 0, 0)),
                   pl.BlockSpec((SW_STEP_HEADS, LANES), lambda kp, b: (kp, 0))],
        out_shape=[jax.ShapeDtypeStruct((T, width), F32), jax.ShapeDtypeStruct((SW_HEADS, W, 2 * W), F32),
                   jax.ShapeDtypeStruct((SW_HEADS, LANES), F32)],
        scratch_shapes=[pltpu.VMEM((S, 4 * LANES), F32), pltpu.VMEM((S, LANES), F32), pltpu.VMEM((S, LANES), F32),
                        pltpu.SemaphoreType.DMA((3,))],
        compiler_params=_cparams("arbitrary", "arbitrary"))(qkn, qkn, proj, bias, sinks, do)


CHIP_FLIPS = ((1, 0), (0, 1), (1, 1))


def _chip_slice(ref, axis, j, size):
    if axis == 1:
        return ref.at[:, pl.ds(j * size, size), :]
    return ref.at[:, :, pl.ds(j * size, size)]


def _flip(v, f):
    return 1 - v if f else v


def _gather_weights(shards, axes, name):
    n = len(shards)
    sizes = [s.shape[a] for s, a in zip(shards, axes)]
    out_shapes = [jax.ShapeDtypeStruct(tuple(4 * d if i == a else d for i, d in enumerate(s.shape)), s.dtype)
                  for s, a in zip(shards, axes)]

    def body(*refs):
        ins, outs = refs[:n], refs[n:2 * n]
        send, recv, loc = refs[2 * n:]
        x, y, c = lax.axis_index("x"), lax.axis_index("y"), lax.axis_index("c")
        j = 2 * x + y
        for jj in range(4):
            @pl.when(j == jj)
            def _():
                for t in range(n):
                    dst = _chip_slice(outs[t], axes[t], jj, sizes[t])
                    pltpu.make_async_copy(ins[t], dst, loc.at[t]).start()
                    for r, (fx, fy) in enumerate(CHIP_FLIPS):
                        pltpu.make_async_remote_copy(ins[t], dst, send.at[t, r], recv.at[t, r],
                                                     device_id=(_flip(x, fx), _flip(y, fy), c),
                                                     device_id_type=MESH).start()
        for t in range(n):
            dst = _chip_slice(outs[t], axes[t], 0, sizes[t])
            pltpu.make_async_copy(ins[t], dst, loc.at[t]).wait()
            for r in range(3):
                pltpu.make_async_remote_copy(ins[t], dst, send.at[t, r], recv.at[t, r],
                                             device_id=(x, y, c), device_id_type=MESH).wait()

    return pl.pallas_call(
        body, name=name, in_specs=[ANY] * n, out_specs=[ANY] * n, out_shape=out_shapes,
        scratch_shapes=[pltpu.SemaphoreType.DMA((n, 3)), pltpu.SemaphoreType.DMA((n, 3)),
                        pltpu.SemaphoreType.DMA((n,))])(*shards)


def _scatter_grads(grads, axes, name):
    n = len(grads)
    sizes = [g.shape[a] // 4 for g, a in zip(grads, axes)]
    out_shapes = [jax.ShapeDtypeStruct((4,) + tuple(sizes[t] if i == axes[t] else d for i, d in enumerate(g.shape)),
                                       g.dtype) for t, g in enumerate(grads)]

    def body(*refs):
        ins, outs = refs[:n], refs[n:2 * n]
        send, recv, loc = refs[2 * n:]
        x, y, c = lax.axis_index("x"), lax.axis_index("y"), lax.axis_index("c")
        j = 2 * x + y
        for jj in range(4):
            @pl.when(j == jj)
            def _():
                for t in range(n):
                    pltpu.make_async_copy(_chip_slice(ins[t], axes[t], jj, sizes[t]), outs[t].at[3], loc.at[t]).start()
                    for r, (fx, fy) in enumerate(CHIP_FLIPS):
                        target = jj ^ (2 * fx + fy)
                        pltpu.make_async_remote_copy(_chip_slice(ins[t], axes[t], target, sizes[t]), outs[t].at[r],
                                                     send.at[t, r], recv.at[t, r],
                                                     device_id=(_flip(x, fx), _flip(y, fy), c),
                                                     device_id_type=MESH).start()
        for t in range(n):
            src = _chip_slice(ins[t], axes[t], 0, sizes[t])
            pltpu.make_async_copy(src, outs[t].at[3], loc.at[t]).wait()
            for r in range(3):
                pltpu.make_async_remote_copy(src, outs[t].at[r], send.at[t, r], recv.at[t, r],
                                             device_id=(x, y, c), device_id_type=MESH).wait()

    return pl.pallas_call(
        body, name=name, in_specs=[ANY] * n, out_specs=[ANY] * n, out_shape=out_shapes,
        scratch_shapes=[pltpu.SemaphoreType.DMA((n, 3)), pltpu.SemaphoreType.DMA((n, 3)),
                        pltpu.SemaphoreType.DMA((n,))])(*grads)


def _swap_with_sibling(parts, name):
    n = len(parts)

    def body(*refs):
        ins, outs = refs[:n], refs[n:2 * n]
        send, recv = refs[2 * n:]
        peer = (lax.axis_index("x"), lax.axis_index("y"), 1 - lax.axis_index("c"))
        copies = [pltpu.make_async_remote_copy(ins[t], outs[t], send.at[t], recv.at[t], device_id=peer,
                                               device_id_type=MESH) for t in range(n)]
        for cp in copies:
            cp.start()
        for cp in copies:
            cp.wait()

    return pl.pallas_call(
        body, name=name, in_specs=[ANY] * n, out_specs=[ANY] * n,
        out_shape=[jax.ShapeDtypeStruct(p.shape, p.dtype) for p in parts],
        scratch_shapes=[pltpu.SemaphoreType.DMA((n,)), pltpu.SemaphoreType.DMA((n,))])(*parts)


def _allreduce_small(v, name):
    R = v.shape[0]
    ND = 8

    def body(v_ref, o_ref, buf, send, recv):
        x, y, c = lax.axis_index("x"), lax.axis_index("y"), lax.axis_index("c")
        me = 4 * x + 2 * y + c
        copies = []
        for d in range(1, ND):
            peer = (_flip(x, d >> 2 & 1), _flip(y, d >> 1 & 1), _flip(c, d & 1))
            copies.append(pltpu.make_async_remote_copy(v_ref, buf.at[me], send.at[d], recv.at[me], device_id=peer,
                                                       device_id_type=MESH))
        for cp in copies:
            cp.start()
        buf[pl.ds(me, 1)] = v_ref[...][None]
        for k in range(ND):
            @pl.when(me != k)
            def _():
                pltpu.make_async_remote_copy(v_ref, buf.at[k], send.at[0], recv.at[k], device_id=(x, y, c),
                                             device_id_type=MESH).wait_recv()
        for cp in copies:
            cp.wait_send()
        total = buf[0]
        for k in range(1, ND):
            total = total + buf[k]
        o_ref[...] = total

    vm = pl.BlockSpec(memory_space=pltpu.VMEM)
    return pl.pallas_call(
        body, name=name, in_specs=[vm], out_specs=vm, out_shape=jax.ShapeDtypeStruct((R, LANES), F32),
        scratch_shapes=[pltpu.VMEM((ND, R, LANES), F32), pltpu.SemaphoreType.DMA((ND,)),
                        pltpu.SemaphoreType.DMA((ND,))],
        compiler_params=pltpu.CompilerParams(vmem_limit_bytes=VMEM_LIMIT))(v)


def _tile2(R, Cn):
    tc = _pick(Cn, 2048)
    tr = R
    for cand in (256, 128, 64, 32, 16, 8):
        if R % cand == 0:
            tr = cand
            break
    return tr, tc


def _sum4(stack, name):
    _, R, Cn = stack.shape
    tr, tc = _tile2(R, Cn)

    def body(s_ref, o_ref):
        o_ref[...] = ((s_ref[0].astype(F32) + s_ref[1].astype(F32)) + s_ref[2].astype(F32)) + s_ref[3].astype(F32)

    return pl.pallas_call(
        body, name=name, grid=(R // tr, Cn // tc), in_specs=[pl.BlockSpec((4, tr, tc), lambda i, j: (0, i, j))],
        out_specs=pl.BlockSpec((tr, tc), lambda i, j: (i, j)), out_shape=jax.ShapeDtypeStruct((R, Cn), F32),
        compiler_params=_cparams("parallel", "parallel"))(stack)


def _adamw(w, m, v, g_parts, name):
    R, Cn = w.shape
    tr, tc = _tile2(R, Cn)
    npart = len(g_parts)
    c1 = 1.0 / (1.0 - ADAM_B1 ** ADAM_STEP)
    c2 = 1.0 / (1.0 - ADAM_B2 ** ADAM_STEP)

    def body(*refs):
        w_ref, m_ref, v_ref = refs[:3]
        g_refs = refs[3:3 + npart]
        g_out, d_out, m_out, v_out = refs[3 + npart:]
        g = g_refs[0][...]
        for r in g_refs[1:]:
            g = g + r[...]
        mn = ADAM_B1 * m_ref[...] + (1.0 - ADAM_B1) * g
        vn = ADAM_B2 * v_ref[...] + (1.0 - ADAM_B2) * (g * g)
        g_out[...] = g
        m_out[...] = mn
        v_out[...] = vn
        d_out[...] = -ADAM_LR * ((mn * c1) / (jnp.sqrt(vn * c2) + ADAM_EPS) + ADAM_WD * w_ref[...])

    spec = pl.BlockSpec((tr, tc), lambda i, j: (i, j))
    return pl.pallas_call(
        body, name=name, grid=(R // tr, Cn // tc), in_specs=[spec] * (3 + npart), out_specs=[spec] * 4,
        out_shape=[jax.ShapeDtypeStruct((R, Cn), F32)] * 4,
        compiler_params=_cparams("parallel", "parallel"))(w, m, v, *g_parts)


SHARDED = (("ab_w_in", 2), ("ab_w_out", 1), ("c_w_in", 2), ("c_w_out", 1), ("ffn_up", 2), ("ffn_conv", 2),
           ("ffn_down", 1), ("ple_gate", 1), ("ple_proj", 2))
SMALL = ("mix_norm", "hg_lb_logits", "hg_out_norm", "q_norm", "k_norm", "sinks", "rel_bias", "ffn_norm",
         "ffn_conv_b", "ple_norm")
WEIGHTS = ("mix_norm", "ab_w_in", "hg_lb_logits", "hg_out_norm", "ab_w_out", "c_w_in", "q_norm", "k_norm", "sinks",
           "rel_bias", "c_w_out", "ffn_norm", "ffn_up", "ffn_conv", "ffn_conv_b", "ffn_down", "ple_norm", "ple_gate",
           "ple_proj")
PACK_ALIGN = 8 * LANES


def _pack(arrs):
    pieces = []
    for a in arrs:
        flat = a.reshape(-1)
        pad = -flat.shape[0] % PACK_ALIGN
        pieces.append(jnp.pad(flat, (0, pad)).reshape(-1, LANES))
    return jnp.concatenate(pieces, axis=0)


def _unpack(packed, like):
    out, r = [], 0
    for a in like:
        size = int(np.prod(a.shape))
        rows = (size + PACK_ALIGN - 1) // PACK_ALIGN * 8
        out.append(packed[r:r + rows].reshape(-1)[:size].reshape(a.shape))
        r += rows
    return out


def _forward_backward(x, p, target, W, S):
    T = x.shape[0]
    depth = p.shape[0]
    lb = _lower_bound_fwd(W["hg_lb_logits"], "lower_bound_fwd")
    bias = _bias_build(W["rel_bias"], "bias_build")
    qk_gain = jnp.concatenate([jnp.tile(W["q_norm"], (1, SW_HEADS)), jnp.tile(W["k_norm"], (1, SW_KV))], axis=1)

    def mm(a, wname, layer, mode, name, **kw):
        return _matmul(a, W[wname], mode, name, b_layer=layer, **kw)

    saved = []
    h = x
    for i in range(depth):
        j = i // 2
        s = {"h0": h}
        s["hn"] = _rmsnorm_fwd(h, W["mix_norm"][i:i + 1], f"mix_norm_fwd_{i}")
        if i % 2 == 0:
            s["proj"] = mm(s["hn"], "ab_w_in", j, "nn", f"ab_in_{i}")
            cat, s["sb_tot"] = _sb_fwd(s["proj"], S, f"sb_fwd_{i}")
            s["cat"], s["oraw"], s["states"] = _hgrn_fwd(s["proj"], cat, lb[j:j + 1], W["hg_out_norm"][j:j + 1], S,
                                                         f"hgrn_fwd_{i}")
            h = mm(s["cat"], "ab_w_out", j, "nn", f"ab_out_{i}", res=h)
        else:
            s["proj"] = mm(s["hn"], "c_w_in", j, "nn", f"c_in_{i}")
            s["qkn"] = _headnorm_fwd(s["proj"], qk_gain[j:j + 1], f"qk_norm_fwd_{i}")
            s["o"] = _swa_fwd(s["qkn"], s["proj"], bias, W["sinks"][j], S, f"swa_fwd_{i}")
            h = mm(s["o"], "c_w_out", j, "nn", f"c_out_{i}", res=h)
        s["h1"] = h
        s["hn2"] = _rmsnorm_fwd(h, W["ffn_norm"][i:i + 1], f"ffn_norm_fwd_{i}")
        s["u"] = mm(s["hn2"], "ffn_up", i, "nn", f"ffn_up_{i}")
        s["a"] = _convglu_fwd(s["u"], W["ffn_conv"][i], W["ffn_conv_b"][i:i + 1], S, f"convglu_fwd_{i}")
        h = mm(s["a"], "ffn_down", i, "nn", f"ffn_down_{i}", res=h)
        s["h2"] = h
        s["hn3"] = _rmsnorm_fwd(h, W["ple_norm"][i:i + 1], f"ple_norm_fwd_{i}")
        s["z"] = mm(s["hn3"], "ple_gate", i, "nn", f"ple_gate_{i}")
        s["pp"] = mm(p, "ple_proj", i, "nn", f"ple_proj_{i}", a_layer=i)
        h = _ple_fwd(h, s["z"], s["pp"], f"ple_fwd_{i}")
        saved.append(s)

    loss, dh = _loss_fwd_bwd(h, target, "loss")

    G = {k: [None] * depth for k in ("mix_norm", "ffn_norm", "ple_norm", "ffn_up", "ffn_conv", "ffn_conv_b",
                                     "ffn_down", "ple_gate", "ple_proj")}
    for k in ("ab_w_in", "ab_w_out", "c_w_in", "c_w_out", "hg_out_norm", "q_norm", "k_norm", "sinks", "lb"):
        G[k] = [None] * (depth // 2)
    dbias_total = None
    for i in reversed(range(depth)):
        j = i // 2
        s = saved[i]
        dz, dpp = _ple_bwd(dh, s["z"], s["pp"], f"ple_bwd_{i}")
        G["ple_proj"][i] = _matmul(p, dpp, "tn", f"d_ple_proj_{i}", out_dtype=BF16, a_layer=i)
        G["ple_gate"][i] = _matmul(s["hn3"], dz, "tn", f"d_ple_gate_{i}", out_dtype=BF16)
        dhn = mm(dz, "ple_gate", i, "nt", f"d_hn3_{i}")
        dh, G["ple_norm"][i] = _rmsnorm_bwd(s["h2"], W["ple_norm"][i:i + 1], dhn, dh, f"ple_norm_bwd_{i}")

        da = mm(dh, "ffn_down", i, "nt", f"d_a_{i}")
        G["ffn_down"][i] = _matmul(s["a"], dh, "tn", f"d_ffn_down_{i}", out_dtype=BF16)
        du, dcw, dcb = _convglu_bwd(s["u"], da, W["ffn_conv"][i], W["ffn_conv_b"][i:i + 1], S, f"convglu_bwd_{i}")
        G["ffn_conv"][i] = jnp.swapaxes(dcw, 0, 1).reshape(3, 2 * D_FF)
        G["ffn_conv_b"][i] = dcb.reshape(1, 2 * D_FF)
        G["ffn_up"][i] = _matmul(s["hn2"], du, "tn", f"d_ffn_up_{i}", out_dtype=BF16)
        dhn = mm(du, "ffn_up", i, "nt", f"d_hn2_{i}")
        dh, G["ffn_norm"][i] = _rmsnorm_bwd(s["h1"], W["ffn_norm"][i:i + 1], dhn, dh, f"ffn_norm_bwd_{i}")

        if i % 2 == 0:
            dcat = mm(dh, "ab_w_out", j, "nt", f"d_cat_{i}")
            G["ab_w_out"][j] = _matmul(s["cat"], dh, "tn", f"d_ab_out_{i}", out_dtype=BF16)
            dproj = _sb_bwd(s["proj"], s["sb_tot"], dcat, S, f"sb_bwd_{i}")
            dproj, G["lb"][j], G["hg_out_norm"][j] = _hgrn_bwd(s["proj"], s["oraw"], dcat, s["states"], lb[j:j + 1],
                                                               W["hg_out_norm"][j:j + 1], dproj, S, f"hgrn_bwd_{i}")
            G["ab_w_in"][j] = _matmul(s["hn"], dproj, "tn", f"d_ab_in_{i}", out_dtype=BF16)
            dhn = mm(dproj, "ab_w_in", j, "nt", f"d_hn_{i}")
        else:
            do = mm(dh, "c_w_out", j, "nt", f"d_o_{i}")
            G["c_w_out"][j] = _matmul(s["o"], dh, "tn", f"d_c_out_{i}", out_dtype=BF16)
            dqkv, dbias, dsink = _swa_bwd(s["qkn"], s["proj"], bias, W["sinks"][j], do, S, f"swa_bwd_{i}")
            dbias_total = dbias if dbias_total is None else dbias_total + dbias
            G["sinks"][j] = dsink[:, 0]
            dproj, dgain = _headnorm_bwd(s["proj"], qk_gain[j:j + 1], dqkv, f"qk_norm_bwd_{i}")
            G["q_norm"][j] = dgain[0, :SW_Q_COLS].reshape(SW_HEADS, SW_DIM).sum(axis=0)
            G["k_norm"][j] = dgain[0, SW_Q_COLS:].reshape(SW_KV, SW_DIM).sum(axis=0)
            G["c_w_in"][j] = _matmul(s["hn"], dproj, "tn", f"d_c_in_{i}", out_dtype=BF16)
            dhn = mm(dproj, "c_w_in", j, "nt", f"d_hn_{i}")
        dh, G["mix_norm"][i] = _rmsnorm_bwd(s["h0"], W["mix_norm"][i:i + 1], dhn, dh, f"mix_norm_bwd_{i}")

    grads = {k: jnp.stack(G[k]) for k in ("ab_w_in", "ab_w_out", "c_w_in", "c_w_out", "ffn_up", "ffn_down",
                                            "ple_gate", "ple_proj", "q_norm", "k_norm", "sinks")}
    grads["ffn_conv"] = jnp.stack(G["ffn_conv"])
    for k in ("mix_norm", "ffn_norm", "ple_norm", "ffn_conv_b", "hg_out_norm"):
        grads[k] = jnp.concatenate(G[k], axis=0)
    grads["hg_lb_logits"] = _lower_bound_bwd(W["hg_lb_logits"], jnp.concatenate(G["lb"], axis=0), "lower_bound_bwd")
    grads["rel_bias"] = _bias_reduce(dbias_total, "bias_reduce")[:, 0, :N_BUCKETS].T
    return loss, dh, grads


def kernel(x, p, mix_norm, ab_w_in, hg_lb_logits, hg_out_norm, ab_w_out, c_w_in, q_norm, k_norm, sinks, rel_bias, c_w_out, ffn_norm, ffn_up, ffn_conv, ffn_conv_b, ffn_down, ple_norm, ple_gate, ple_proj, loss_target, m_mix_norm, m_ab_w_in, m_hg_lb_logits, m_hg_out_norm, m_ab_w_out, m_c_w_in, m_q_norm, m_k_norm, m_sinks, m_rel_bias, m_c_w_out, m_ffn_norm, m_ffn_up, m_ffn_conv, m_ffn_conv_b, m_ffn_down, m_ple_norm, m_ple_gate, m_ple_proj, v_mix_norm, v_ab_w_in, v_hg_lb_logits, v_hg_out_norm, v_ab_w_out, v_c_w_in, v_q_norm, v_k_norm, v_sinks, v_rel_bias, v_c_w_out, v_ffn_norm, v_ffn_up, v_ffn_conv, v_ffn_conv_b, v_ffn_down, v_ple_norm, v_ple_gate, v_ple_proj):
    args = dict(locals())
    w = {k: args[k] for k in WEIGHTS}
    m = {k: args["m_" + k] for k in WEIGHTS}
    v = {k: args["v_" + k] for k in WEIGHTS}
    B, S, Dm = x.shape
    T = B * S
    names = [k for k, _ in SHARDED]
    axes = [a for _, a in SHARDED]

    wire = {k: (F32 if k == "ffn_conv" else BF16) for k in names}
    full = _gather_weights([w[k].astype(wire[k]) for k in names], axes, "gather_weights")
    W = dict(zip(names, full))
    for k in SMALL:
        W[k] = w[k]

    loss, dx, grads = _forward_backward(x.reshape(T, Dm), p.reshape(p.shape[0], T, p.shape[-1]),
                                        loss_target.reshape(T, Dm), W, S)
    loss = lax.psum(loss[0, 0], ("x", "y", "c"))

    stacks = _scatter_grads([grads[k] for k in names], axes, "scatter_grads")
    partial = [_sum4(st.reshape(4, -1, st.shape[-1]), f"sum_chips_{k}") for k, st in zip(names, stacks)]
    other = _swap_with_sibling(partial, "swap_core_sums")
    small_sum = _allreduce_small(_pack([grads[k] for k in SMALL]), "allreduce_small")

    out_g, out_d, out_m, out_v = {}, {}, {}, {}
    for k, mine, theirs in zip(names, partial, other):
        shp = w[k].shape
        r = [a.reshape(shp) for a in _adamw(w[k].reshape(mine.shape), m[k].reshape(mine.shape),
                                            v[k].reshape(mine.shape), [mine, theirs], f"adamw_{k}")]
        out_g[k], out_d[k], out_m[k], out_v[k] = r
    sm = _adamw(_pack([w[k] for k in SMALL]), _pack([m[k] for k in SMALL]), _pack([v[k] for k in SMALL]),
                [small_sum], "adamw_small")
    like = [w[k] for k in SMALL]
    for dst, packed in zip((out_g, out_d, out_m, out_v), sm):
        for k, a in zip(SMALL, _unpack(packed, like)):
            dst[k] = a

    return (loss, dx.reshape(B, S, Dm), *[out_g[k] for k in WEIGHTS], *[out_d[k] for k in WEIGHTS],
            *[out_m[k] for k in WEIGHTS], *[out_v[k] for k in WEIGHTS])
```

```python
import math

import numpy as np
import jax
import jax.numpy as jnp
from jax import lax
from jax.experimental import pallas as pl
from jax.experimental.pallas import tpu as pltpu

F32 = jnp.float32
BF16 = jnp.bfloat16
MESH = pl.DeviceIdType.MESH
ANY = pl.BlockSpec(memory_space=pl.ANY)

D_MODEL = 1024
EPS = 1e-6
SB_HEADS, SB_DIM = 8, 64
HG_HEADS, HG_DK = 4, 128
HG_CHUNK = 32
HG_STEP = 2
SW_HEADS, SW_KV, SW_DIM, WINDOW = 16, 4, 64, 128
N_BUCKETS, MAX_DISTANCE = 32, 128
D_FF = 2816
ATT_BLOCK = 128
SB_QBLOCK = 256
LANES = 128
NEG = -1e30

ADAM_LR, ADAM_B1, ADAM_B2, ADAM_EPS, ADAM_WD, ADAM_STEP = 0.001, 0.9, 0.999, 1e-08, 0.01, 10

VMEM_LIMIT = 56 * 1024 * 1024


def _cparams(*sem):
    return pltpu.CompilerParams(dimension_semantics=sem, vmem_limit_bytes=VMEM_LIMIT)


def _pick(n, cap):
    if n <= cap:
        return n
    best = None
    for d in range(LANES, cap + 1, LANES):
        if n % d == 0:
            best = d
    assert best is not None, (n, cap)
    return best


def _dot(a, b, ca, cb):
    return lax.dot_general(a.astype(BF16), b.astype(BF16), (((ca,), (cb,)), ((), ())),
                           preferred_element_type=F32)


def _split(x, terms):
    parts = []
    for _ in range(terms):
        hi = x.astype(BF16)
        parts.append(hi)
        x = x - hi.astype(F32)
    return parts


def _dot_exact_l(x, m, terms=2):
    out = None
    for p in _split(x, terms):
        t = lax.dot_general(p, m, (((1,), (0,)), ((), ())), preferred_element_type=F32)
        out = t if out is None else out + t
    return out


def _dot_exact_r(m, x, terms=3, cm=1):
    out = None
    for p in _split(x, terms):
        t = lax.dot_general(m, p, (((cm,), (0,)), ((), ())), preferred_element_type=F32)
        out = t if out is None else out + t
    return out


def _sig(x):
    return 1.0 / (1.0 + jnp.exp(-x))


def _iota2(shape, dim):
    return lax.broadcasted_iota(jnp.int32, shape, dim)


def _operand_spec(arr, layer, blk, index):
    if arr.ndim == 2:
        return pl.BlockSpec(blk, index)
    if layer is not None:
        return pl.BlockSpec((None,) + blk, lambda i, j, k: (layer,) + index(i, j, k))
    per_half = arr.shape[2] // blk[1]

    def halves(i, j, k):
        r, c = index(i, j, k)
        return (c // per_half, r, c % per_half)

    return pl.BlockSpec((None,) + blk, halves)


def _matmul(a, b, mode, name, out_dtype=F32, res=None, a_layer=None, b_layer=None, tiles=None):
    def dims(arr, layer):
        if arr.ndim == 2:
            return arr.shape
        return arr.shape[1:] if layer is not None else (arr.shape[1], 2 * arr.shape[2])

    (a0, a1), (b0, b1) = dims(a, a_layer), dims(b, b_layer)
    if mode == "nn":
        M, K, N = a0, a1, b1
    elif mode == "nt":
        M, K, N = a0, a1, b0
    else:
        K, M, N = a0, a1, b1
    cap_m, cap_n, cap_k = 1024, 512, 2048
    tm, tn, tk = _pick(M, cap_m), _pick(N, cap_n), _pick(K, cap_k)
    if a.ndim == 3 and a_layer is None:
        if mode == "tn":
            tm = _pick(a.shape[2], cap_m)
        else:
            tk = _pick(a.shape[2], cap_k)
    if b.ndim == 3 and b_layer is None:
        if mode == "nt":
            tk = _pick(b.shape[2], cap_k)
        else:
            tn = _pick(b.shape[2], cap_n)
    if tiles is not None:
        tm, tn, tk = tiles
    assert M % tm == 0 and N % tn == 0 and K % tk == 0, (name, M, N, K, tm, tn, tk)
    nk = K // tk
    if mode == "tn":
        a_spec = _operand_spec(a, a_layer, (tk, tm), lambda i, j, k: (k, i))
    else:
        a_spec = _operand_spec(a, a_layer, (tm, tk), lambda i, j, k: (i, k))
    if mode == "nt":
        b_spec = _operand_spec(b, b_layer, (tn, tk), lambda i, j, k: (j, k))
    else:
        b_spec = _operand_spec(b, b_layer, (tk, tn), lambda i, j, k: (k, j))
    ca, cb = {"nn": (1, 0), "nt": (1, 1), "tn": (0, 0)}[mode]
    o_spec = pl.BlockSpec((tm, tn), lambda i, j, k: (i, j))

    def body(*refs):
        if res is None:
            a_ref, b_ref, o_ref, acc = refs
        else:
            a_ref, b_ref, r_ref, o_ref, acc = refs
        k = pl.program_id(2)

        @pl.when(k == 0)
        def _():
            acc[...] = jnp.zeros_like(acc)

        acc[...] += _dot(a_ref[...], b_ref[...], ca, cb)

        @pl.when(k == nk - 1)
        def _():
            r = acc[...]
            if res is not None:
                r = r + r_ref[...]
            o_ref[...] = r.astype(out_dtype)

    ins = [a, b] + ([] if res is None else [res])
    in_specs = [a_spec, b_spec] + ([] if res is None else [o_spec])
    return pl.pallas_call(
        body, name=name, grid=(M // tm, N // tn, nk), in_specs=in_specs, out_specs=o_spec,
        out_shape=jax.ShapeDtypeStruct((M, N), out_dtype),
        scratch_shapes=[pltpu.VMEM((tm, tn), F32)],
        compiler_params=_cparams("parallel", "parallel", "arbitrary"))(*ins)


ROW_TILE = 512


def _row_spec(width):
    return pl.BlockSpec((ROW_TILE, width), lambda i: (i, 0))


def _vec_spec(width):
    return pl.BlockSpec((1, width), lambda i: (0, 0))


def _rmsnorm_fwd(h, g, name):
    T, Dm = h.shape

    def body(h_ref, g_ref, o_ref):
        x = h_ref[...]
        r = lax.rsqrt(jnp.mean(x * x, axis=1, keepdims=True) + EPS)
        o_ref[...] = (x * r * g_ref[...]).astype(BF16)

    return pl.pallas_call(
        body, name=name, grid=(T // ROW_TILE,), in_specs=[_row_spec(Dm), _vec_spec(Dm)],
        out_specs=_row_spec(Dm), out_shape=jax.ShapeDtypeStruct((T, Dm), BF16),
        compiler_params=_cparams("parallel"))(h, g)


def _rmsnorm_bwd(h, g, dhn, dres, name):
    T, Dm = h.shape

    def body(h_ref, g_ref, dy_ref, dr_ref, dh_ref, dg_ref):
        i = pl.program_id(0)
        x = h_ref[...]
        dy = dy_ref[...]
        r = lax.rsqrt(jnp.mean(x * x, axis=1, keepdims=True) + EPS)
        gdy = dy * g_ref[...]
        m = jnp.mean(x * gdy, axis=1, keepdims=True)
        dh_ref[...] = dr_ref[...] + r * gdy - x * (r * r * r * m)
        part = jnp.sum(dy * x * r, axis=0, keepdims=True)

        @pl.when(i == 0)
        def _():
            dg_ref[...] = part

        @pl.when(i > 0)
        def _():
            dg_ref[...] += part

    return pl.pallas_call(
        body, name=name, grid=(T // ROW_TILE,),
        in_specs=[_row_spec(Dm), _vec_spec(Dm), _row_spec(Dm), _row_spec(Dm)],
        out_specs=[_row_spec(Dm), _vec_spec(Dm)],
        out_shape=[jax.ShapeDtypeStruct((T, Dm), F32), jax.ShapeDtypeStruct((1, Dm), F32)],
        compiler_params=_cparams("arbitrary"))(h, g, dhn, dres)


def _ple_fwd(h, z, pp, name):
    T, Dm = h.shape

    def body(h_ref, z_ref, p_ref, o_ref):
        o_ref[...] = h_ref[...] + _sig(z_ref[...]) * p_ref[...]

    return pl.pallas_call(
        body, name=name, grid=(T // ROW_TILE,), in_specs=[_row_spec(Dm)] * 3, out_specs=_row_spec(Dm),
        out_shape=jax.ShapeDtypeStruct((T, Dm), F32), compiler_params=_cparams("parallel"))(h, z, pp)


def _ple_bwd(dh, z, pp, name):
    T, Dm = dh.shape

    def body(dh_ref, z_ref, p_ref, dz_ref, dp_ref):
        s = _sig(z_ref[...])
        d = dh_ref[...]
        dz_ref[...] = d * p_ref[...] * s * (1.0 - s)
        dp_ref[...] = d * s

    return pl.pallas_call(
        body, name=name, grid=(T // ROW_TILE,), in_specs=[_row_spec(Dm)] * 3, out_specs=[_row_spec(Dm)] * 2,
        out_shape=[jax.ShapeDtypeStruct((T, Dm), F32)] * 2, compiler_params=_cparams("parallel"))(dh, z, pp)


def _loss_fwd_bwd(y, target, name):
    T, Dm = y.shape

    def body(y_ref, t_ref, l_ref, d_ref):
        i = pl.program_id(0)
        e = y_ref[...] - t_ref[...]
        d_ref[...] = e * (1.0 / Dm)
        part = jnp.full((8, LANES), 0.5 / Dm, F32) * jnp.sum(e * e)

        @pl.when(i == 0)
        def _():
            l_ref[...] = part

        @pl.when(i > 0)
        def _():
            l_ref[...] += part

    return pl.pallas_call(
        body, name=name, grid=(T // ROW_TILE,), in_specs=[_row_spec(Dm)] * 2,
        out_specs=[pl.BlockSpec((8, LANES), lambda i: (0, 0)), _row_spec(Dm)],
        out_shape=[jax.ShapeDtypeStruct((8, LANES), F32), jax.ShapeDtypeStruct((T, Dm), F32)],
        compiler_params=_cparams("arbitrary"))(y, target)


def _head_mean_matrix():
    r = _iota2((LANES, LANES), 0) >= SW_DIM
    c = _iota2((LANES, LANES), 1) >= SW_DIM
    return jnp.where(r == c, 1.0 / SW_DIM, 0.0).astype(BF16)


def _headnorm_fwd(x, g_lane, name):
    T = x.shape[0]
    C = g_lane.shape[1]

    def body(x_ref, g_ref, y_ref):
        xv = x_ref[...]
        ms = _dot_exact_l(xv * xv, _head_mean_matrix())
        y_ref[...] = xv * lax.rsqrt(ms + EPS) * g_ref[...]

    spec = pl.BlockSpec((ROW_TILE, LANES), lambda j, i: (i, j))
    return pl.pallas_call(
        body, name=name, grid=(C // LANES, T // ROW_TILE),
        in_specs=[spec, pl.BlockSpec((1, LANES), lambda j, i: (0, j))], out_specs=spec,
        out_shape=jax.ShapeDtypeStruct((T, C), F32), compiler_params=_cparams("parallel", "parallel"))(x, g_lane)


def _headnorm_bwd(x, g_lane, dy_full, name):
    T = x.shape[0]
    C = g_lane.shape[1]

    def body(x_ref, g_ref, dy_ref, dx_ref, dg_ref):
        i = pl.program_id(1)
        xv = x_ref[...]
        dy = dy_ref[...]
        bd = _head_mean_matrix()
        r = lax.rsqrt(_dot_exact_l(xv * xv, bd) + EPS)
        gdy = dy * g_ref[...]
        m = _dot_exact_l(xv * gdy, bd)
        dx_ref[...] = r * gdy - xv * (r * r * r * m)
        part = jnp.sum(dy * xv * r, axis=0, keepdims=True)

        @pl.when(i == 0)
        def _():
            dg_ref[...] = part

        @pl.when(i > 0)
        def _():
            dg_ref[...] += part

    spec = pl.BlockSpec((ROW_TILE, LANES), lambda j, i: (i, j))
    vspec = pl.BlockSpec((1, LANES), lambda j, i: (0, j))
    return pl.pallas_call(
        body, name=name, grid=(C // LANES, T // ROW_TILE), in_specs=[spec, vspec, spec],
        out_specs=[spec, vspec],
        out_shape=[jax.ShapeDtypeStruct(dy_full.shape, F32), jax.ShapeDtypeStruct((1, C), F32)],
        input_output_aliases={2: 0}, compiler_params=_cparams("parallel", "arbitrary"))(x, g_lane, dy_full)


CONV_TILE = 128


def _shift_down(x, k):
    rows = _iota2(x.shape, 0)
    return jnp.where(rows >= k, pltpu.roll(x, k, 0), 0.0)


def _shift_up(x, k):
    n = x.shape[0]
    rows = _iota2(x.shape, 0)
    return jnp.where(rows < n - k, pltpu.roll(x, n - k, 0), 0.0)


def _conv3(u, w_ref, b_ref):
    return w_ref[2:3, :] * u + w_ref[1:2, :] * _shift_down(u, 1) + w_ref[0:1, :] * _shift_down(u, 2) + b_ref[...]


def _convglu_fwd(u, cw, cb, S, name):
    T = u.shape[0]
    nf = D_FF // CONV_TILE

    def body(ug_ref, uu_ref, wg_ref, wu_ref, bg_ref, bu_ref, a_ref):
        yg = _conv3(ug_ref[...], wg_ref, bg_ref)
        yu = _conv3(uu_ref[...], wu_ref, bu_ref)
        a_ref[...] = (yg * _sig(yg) * yu).astype(BF16)

    def blk(rows, off):
        return pl.BlockSpec((rows, CONV_TILE), (lambda b, j: (b, j + off)) if rows == S else (lambda b, j: (0, j + off)))

    return pl.pallas_call(
        body, name=name, grid=(T // S, nf),
        in_specs=[blk(S, 0), blk(S, nf), blk(3, 0), blk(3, nf), blk(1, 0), blk(1, nf)],
        out_specs=blk(S, 0), out_shape=jax.ShapeDtypeStruct((T, D_FF), BF16),
        compiler_params=_cparams("parallel", "parallel"))(u, u, cw, cw, cb, cb)


def _convglu_bwd(u, da, cw, cb, S, name):
    T = u.shape[0]
    nf = D_FF // CONV_TILE

    def body(ug_ref, uu_ref, da_ref, wg_ref, wu_ref, bg_ref, bu_ref, du_ref, dw_ref, db_ref):
        b = pl.program_id(1)
        ug, uu = ug_ref[...], uu_ref[...]
        yg = _conv3(ug, wg_ref, bg_ref)
        yu = _conv3(uu, wu_ref, bu_ref)
        s = _sig(yg)
        da_v = da_ref[...]
        for half, (uv, w_ref, dy) in enumerate(((ug, wg_ref, da_v * yu * (s * (1.0 + yg * (1.0 - s)))),
                                                (uu, wu_ref, da_v * yg * s))):
            du_ref[half] = w_ref[2:3, :] * dy + w_ref[1:2, :] * _shift_up(dy, 1) + w_ref[0:1, :] * _shift_up(dy, 2)
            dws = [jnp.sum(dy * _shift_down(uv, 2), axis=0, keepdims=True),
                   jnp.sum(dy * _shift_down(uv, 1), axis=0, keepdims=True),
                   jnp.sum(dy * uv, axis=0, keepdims=True)]
            dbv = jnp.sum(dy, axis=0, keepdims=True)

            @pl.when(b == 0)
            def _():
                for k in range(3):
                    dw_ref[half, k:k + 1, :] = dws[k]
                db_ref[half] = dbv

            @pl.when(b > 0)
            def _():
                for k in range(3):
                    dw_ref[half, k:k + 1, :] += dws[k]
                db_ref[half] += dbv

    def blk(rows, off):
        return pl.BlockSpec((rows, CONV_TILE), (lambda j, b: (b, j + off)) if rows == S else (lambda j, b: (0, j + off)))

    def both(rows):
        return pl.BlockSpec((2, rows, CONV_TILE), (lambda j, b: (0, b, j)) if rows == S else (lambda j, b: (0, 0, j)))

    return pl.pallas_call(
        body, name=name, grid=(nf, T // S),
        in_specs=[blk(S, 0), blk(S, nf), blk(S, 0), blk(3, 0), blk(3, nf), blk(1, 0), blk(1, nf)],
        out_specs=[both(S), both(3), both(1)],
        out_shape=[jax.ShapeDtypeStruct((2, T, D_FF), F32), jax.ShapeDtypeStruct((2, 3, D_FF), F32),
                   jax.ShapeDtypeStruct((2, 1, D_FF), F32)],
        compiler_params=_cparams("parallel", "arbitrary"))(u, u, da, cw, cw, cb, cb)


def _sb_scores(qb, kblk, on_diag_mask):
    z = _dot(qb, kblk, 1, 1) * (SB_DIM ** -0.5)
    l1 = jnp.log(1.0 + jnp.exp(-jnp.abs(z)))
    ls = jnp.minimum(z, 0.0) - l1
    lk = jnp.where(on_diag_mask, ls - z, 0.0)
    return ls, lk


def _sb_fwd(proj, S, name, exchange=None):
    T = proj.shape[0]
    BQ, BK = SB_QBLOCK, ATT_BLOCK
    nq = S // BQ
    nhp = SB_HEADS // 2
    heads = [slice(h * SB_DIM, (h + 1) * SB_DIM) for h in range(2)]

    grid = (T // S, nhp)

    def body(*refs):
        (q_ref, k_ref, v_ref, o_ref, tot_ref), carried = _carried(exchange, refs, 3, 2, 0)
        _start_carried(exchange, carried, grid)
        ahead = _iota2((BQ, BK), 1) - _iota2((BQ, BK), 0)
        upper = (_iota2((BK, BK), 0) > _iota2((BK, BK), 1)).astype(BF16)

        def qloop(iq, carry):
            q0 = pl.multiple_of(iq * BQ, BQ)
            rows = pl.ds(q0, BQ)
            qbs = [q_ref[rows, sl] for sl in heads]
            nkb = (iq + 1) * (BQ // BK)

            def kloop(jj, kc):
                k0 = pl.multiple_of((nkb - 1 - jj) * BK, BK)
                krows = pl.ds(k0, BK)
                mask = ahead < q0 - k0
                out = []
                for sl, qb, (run, acc) in zip(heads, qbs, kc):
                    ls, lk = _sb_scores(qb, k_ref[krows, sl], mask)
                    later = _dot_exact_l(lk, upper)
                    w = jnp.where(mask, jnp.exp(ls + later + run), 0.0)
                    out.append((run + jnp.sum(lk, axis=1, keepdims=True), acc + _dot(w, v_ref[krows, sl], 1, 0)))
                return tuple(out)

            init = (jnp.zeros((BQ, 1), F32), jnp.zeros((BQ, SB_DIM), F32))
            res = lax.fori_loop(0, nkb, kloop, (init, init))
            for h, (sl, (run, acc)) in enumerate(zip(heads, res)):
                o_ref[rows, sl] = acc
                tot_ref[rows, h:h + 1] = run
            return carry

        lax.fori_loop(0, nq, qloop, 0)
        _wait_carried(exchange, carried, grid)

    def spec(off):
        return pl.BlockSpec((S, LANES), lambda b, hp: (b, hp + off))

    (cat, tot), targets = _carrier_call(
        body, name, grid, [spec(0), spec(nhp), spec(2 * nhp)],
        [spec(0), pl.BlockSpec((None, S, 2), lambda b, hp: (hp, b, 0))],
        [jax.ShapeDtypeStruct((T, 2 * SB_HEADS * SB_DIM), F32), jax.ShapeDtypeStruct((nhp, T, 2), F32)],
        [], {}, (proj, proj, proj), exchange)
    return cat, tot, targets


def _sb_bwd(proj, tot, dcat, S, name, exchange=None):
    T, width = proj.shape
    BQ, BK = SB_QBLOCK, ATT_BLOCK
    nq = S // BQ
    nhp = SB_HEADS // 2
    scale = SB_DIM ** -0.5
    heads = [slice(h * SB_DIM, (h + 1) * SB_DIM) for h in range(2)]

    grid = (T // S, nhp)

    def body(*refs):
        (q_ref, k_ref, v_ref, tot_ref, do_ref, dp_hbm, dq_s, dk_s, dv_s, sems), carried = _carried(
            exchange, refs, 5, 1, 4)
        _start_carried(exchange, carried, grid)
        b, hp = pl.program_id(0), pl.program_id(1)
        ahead = _iota2((BQ, BK), 1) - _iota2((BQ, BK), 0)
        r, c = _iota2((BK, BK), 0), _iota2((BK, BK), 1)
        upto = (r <= c).astype(BF16)
        earlier = (r < c).astype(BF16)
        dk_s[...] = jnp.zeros_like(dk_s)
        dv_s[...] = jnp.zeros_like(dv_s)

        def qloop(iq, carry):
            q0 = pl.multiple_of(iq * BQ, BQ)
            rows = pl.ds(q0, BQ)
            qbs = [q_ref[rows, sl] for sl in heads]
            dobs = [do_ref[rows, sl] for sl in heads]
            totals = [tot_ref[rows, h:h + 1] for h in range(2)]

            def kloop(kb, kc):
                k0 = pl.multiple_of(kb * BK, BK)
                krows = pl.ds(k0, BK)
                mask = ahead < q0 - k0
                out = []
                for sl, qb, dob, total, (run, grun, dq) in zip(heads, qbs, dobs, totals, kc):
                    kblk = k_ref[krows, sl]
                    ls, lk = _sb_scores(qb, kblk, mask)
                    later = total - (_dot_exact_l(lk, upto) + run)
                    w = jnp.where(mask, jnp.exp(ls + later), 0.0)
                    g = w * _dot(dob, v_ref[krows, sl], 1, 1)
                    before = _dot_exact_l(g, earlier) + grun
                    beta = jnp.exp(ls)
                    dz = jnp.where(mask, g * (1.0 - beta) - beta * before, 0.0) * scale
                    dv_s[krows, sl] += _dot(w, dob, 0, 0)
                    dk_s[krows, sl] += _dot(dz, qb, 0, 0)
                    out.append((run + jnp.sum(lk, axis=1, keepdims=True),
                                grun + jnp.sum(g, axis=1, keepdims=True), dq + _dot(dz, kblk, 1, 0)))
                return tuple(out)

            zero = jnp.zeros((BQ, 1), F32)
            init = (zero, zero, jnp.zeros((BQ, SB_DIM), F32))
            res = lax.fori_loop(0, (iq + 1) * (BQ // BK), kloop, (init, init))
            for sl, (_, _, dq) in zip(heads, res):
                dq_s[rows, sl] = dq
            return carry

        lax.fori_loop(0, nq, qloop, 0)
        r0 = pl.multiple_of(b * S, S)
        copies = []
        for n, buf in enumerate((dq_s, dk_s, dv_s)):
            c0 = pl.multiple_of((hp + n * nhp) * LANES, LANES)
            copies.append(pltpu.make_async_copy(buf, dp_hbm.at[pl.ds(r0, S), pl.ds(c0, LANES)], sems.at[n]))
        for cp in copies:
            cp.start()
        for cp in copies:
            cp.wait()
        _wait_carried(exchange, carried, grid)

    def spec(off):
        return pl.BlockSpec((S, LANES), lambda b, hp: (b, hp + off))

    (dproj,), targets = _carrier_call(
        body, name, grid,
        [spec(0), spec(nhp), spec(2 * nhp), pl.BlockSpec((None, S, 2), lambda b, hp: (hp, b, 0)), spec(0)],
        [ANY], [jax.ShapeDtypeStruct((T, width), F32)],
        [pltpu.VMEM((S, LANES), F32)] * 3 + [pltpu.SemaphoreType.DMA((3,))], {}, (proj, proj, proj, tot, dcat),
        exchange)
    return dproj, targets


HG_COL0 = 3 * SB_HEADS * SB_DIM // LANES


def _hg_gates(q, fp, lbv):
    sg = _sig(fp)
    f = lbv + (1.0 - lbv) * sg
    kk = (1.0 - lbv) * _sig(-fp)
    sq = _sig(q)
    return sg, f, kk, sq


def _hg_chunk(qs, kk, lf, incl):
    C = HG_CHUNK
    b = _dot_exact_r(incl, lf)
    bl = b[C - 1:C, :]
    bm = b[C // 2 - 1:C // 2, :]
    e_t = jnp.exp(b - bm)
    e_s = jnp.exp(bm - b)
    e_i = jnp.exp(b)
    e_e = jnp.exp(bl - b)
    return bl, e_t, e_s, e_i, e_e


def _hgrn_fwd(proj, cat, lb, hgn, S, name):
    T = proj.shape[0]
    B = T // S
    C = HG_CHUNK
    NC = S // C

    def body(q_ref, f_ref, i_ref, g_ref, lb_ref, hgn_ref, cat_hbm, ob_ref, oraw_ref, st_ref, state):
        del cat_hbm
        state[...] = jnp.zeros_like(state)
        row, col = _iota2((C, C), 0), _iota2((C, C), 1)
        causal = row >= col
        incl = causal.astype(BF16)

        def chunk(c, carry):
            rows = pl.ds(pl.multiple_of(c * C, C), C)
            for hh in range(HG_STEP):
                hs = slice(hh * LANES, (hh + 1) * LANES)
                q, iv, gv = q_ref[rows, hs], i_ref[rows, hs], g_ref[rows, hs]
                _, f, kk, sq = _hg_gates(q, f_ref[rows, hs], lb_ref[:, hs])
                qs = q * sq
                bl, e_t, e_s, e_i, e_e = _hg_chunk(qs, kk, jnp.log(f), incl)
                p = jnp.where(causal, _dot(qs * e_t, kk * e_s, 1, 1), 0.0)
                st = state[hh]
                st_ref[0, hh, c] = st
                o = _dot(qs * e_i, st, 1, 1) + _dot(p, iv, 1, 0)
                state[hh] = st * jnp.exp(bl) + _dot(iv, kk * e_e, 0, 0)
                oraw_ref[rows, hs] = o
                r = lax.rsqrt(jnp.mean(o * o, axis=1, keepdims=True) + EPS)
                ob_ref[rows, hs] = o * r * hgn_ref[...] * (gv * _sig(gv))
            return carry

        lax.fori_loop(0, NC, chunk, 0)

    width = HG_STEP * LANES
    col0 = HG_COL0 * LANES // width
    nstep = HG_HEADS // HG_STEP

    def spec(off):
        return pl.BlockSpec((S, width), lambda b, h: (b, h + off))

    return pl.pallas_call(
        body, name=name, grid=(B, nstep),
        in_specs=[spec(col0), spec(col0 + nstep), spec(col0 + 2 * nstep), spec(col0 + 3 * nstep),
                  pl.BlockSpec((1, width), lambda b, h: (0, h)), pl.BlockSpec((1, LANES), lambda b, h: (0, 0)), ANY],
        out_specs=[spec(nstep), spec(0),
                   pl.BlockSpec((1, HG_STEP, NC, LANES, LANES), lambda b, h: (b, h, 0, 0, 0))],
        out_shape=[jax.ShapeDtypeStruct(cat.shape, F32), jax.ShapeDtypeStruct((T, HG_HEADS * LANES), F32),
                   jax.ShapeDtypeStruct((B, HG_HEADS, NC, LANES, LANES), F32)],
        scratch_shapes=[pltpu.VMEM((HG_STEP, LANES, LANES), F32)],
        input_output_aliases={6: 0},
        compiler_params=_cparams("parallel", "parallel"))(proj, proj, proj, proj, lb, hgn, cat)


def _hgrn_bwd(proj, oraw, dcat, states, lb, hgn, dproj, S, name):
    T = proj.shape[0]
    B = T // S
    C = HG_CHUNK
    NC = S // C

    def body(q_ref, f_ref, i_ref, g_ref, oraw_ref, dy_ref, st_ref, lb_ref, hgn_ref, dp_in,
             dp_hbm, dlb_ref, dhgn_ref, dstate, dq_s, df_s, di_s, dg_s, sems):
        del dp_in
        h, b = pl.program_id(0), pl.program_id(1)
        row, col = _iota2((C, C), 0), _iota2((C, C), 1)
        causal = row >= col
        incl = causal.astype(BF16)
        last_row = _iota2((C, LANES), 0) == C - 1
        hg = hgn_ref[...]
        dstate[...] = jnp.zeros_like(dstate)

        @pl.when(b == 0)
        def _():
            dlb_ref[...] = jnp.zeros_like(dlb_ref)

        @pl.when(jnp.logical_and(b == 0, h == 0))
        def _():
            dhgn_ref[...] = jnp.zeros_like(dhgn_ref)

        def chunk(cc, carry):
            c = NC - 1 - cc
            rows = pl.ds(pl.multiple_of(c * C, C), C)
            for hh in range(HG_STEP):
                hs = slice(hh * LANES, (hh + 1) * LANES)
                lbv = lb_ref[:, hs]
                q, fp, iv, gv = q_ref[rows, hs], f_ref[rows, hs], i_ref[rows, hs], g_ref[rows, hs]
                o = oraw_ref[rows, hs]
                dy = dy_ref[rows, hs]
                r = lax.rsqrt(jnp.mean(o * o, axis=1, keepdims=True) + EPS)
                on = o * r
                sgv = _sig(gv)
                silu_g = gv * sgv
                dg_s[rows, hs] = dy * on * hg * (sgv * (1.0 + gv * (1.0 - sgv)))
                dhgn_ref[...] += jnp.sum(dy * on * silu_g, axis=0, keepdims=True)
                dn = dy * hg * silu_g
                do = r * dn - o * (r * r * r * jnp.mean(o * dn, axis=1, keepdims=True))
                sg, f, kk, sq = _hg_gates(q, fp, lbv)
                qs = q * sq
                bl, e_t, e_s, e_i, e_e = _hg_chunk(qs, kk, jnp.log(f), incl)
                qd, kd, qi, ke = qs * e_t, kk * e_s, qs * e_i, kk * e_e
                p = jnp.where(causal, _dot(qd, kd, 1, 1), 0.0)
                st = st_ref[0, hh, c]
                dst = dstate[hh]
                ebl = jnp.exp(bl)
                dqi = _dot(do, st, 1, 0)
                dp = jnp.where(causal, _dot(do, iv, 1, 1), 0.0)
                di_s[rows, hs] = _dot(p, do, 0, 0) + _dot(ke, dst, 1, 1)
                dqd = _dot(dp, kd, 1, 0)
                dkd = _dot(dp, qd, 0, 0)
                dke = _dot(iv, dst, 1, 0)
                dbl = jnp.sum(st * dst, axis=0, keepdims=True) * ebl + jnp.sum(dke * ke, axis=0, keepdims=True)
                db = dqd * qd - dkd * kd + dqi * qi - dke * ke + jnp.where(last_row, dbl, 0.0)
                dqs = dqd * e_t + dqi * e_i
                dkk = dkd * e_s + dke * e_e
                dlf = _dot_exact_r(incl, db, cm=0)
                dstate[hh] = _dot(do, qi, 0, 0) + dst * ebl
                oms = 1.0 - sg
                dfd = dlf / f
                df_s[rows, hs] = (dfd - dkk) * (1.0 - lbv) * sg * oms
                dq_s[rows, hs] = dqs * (sq * (1.0 + q * (1.0 - sq)))
                dlb_ref[:, hs] += jnp.sum((dfd - dkk) * oms, axis=0, keepdims=True)
            return carry

        lax.fori_loop(0, NC, chunk, 0)
        r0 = pl.multiple_of(b * S, S)
        copies = []
        for n, buf in enumerate((dq_s, df_s, di_s, dg_s)):
            c0 = pl.multiple_of((col0 + n * nstep + h) * width, width)
            copies.append(pltpu.make_async_copy(buf, dp_hbm.at[pl.ds(r0, S), pl.ds(c0, width)], sems.at[n]))
        for cp in copies:
            cp.start()
        for cp in copies:
            cp.wait()

    width = HG_STEP * LANES
    col0 = HG_COL0 * LANES // width
    nstep = HG_HEADS // HG_STEP

    def spec(off):
        return pl.BlockSpec((S, width), lambda h, b: (b, h + off))

    return pl.pallas_call(
        body, name=name, grid=(nstep, B),
        in_specs=[spec(col0), spec(col0 + nstep), spec(col0 + 2 * nstep), spec(col0 + 3 * nstep), spec(0), spec(nstep),
                  pl.BlockSpec((1, HG_STEP, NC, LANES, LANES), lambda h, b: (b, h, 0, 0, 0)),
                  pl.BlockSpec((1, width), lambda h, b: (0, h)), pl.BlockSpec((1, LANES), lambda h, b: (0, 0)), ANY],
        out_specs=[ANY, pl.BlockSpec((1, width), lambda h, b: (0, h)), pl.BlockSpec((1, LANES), lambda h, b: (0, 0))],
        out_shape=[jax.ShapeDtypeStruct(dproj.shape, F32), jax.ShapeDtypeStruct((1, HG_HEADS * LANES), F32),
                   jax.ShapeDtypeStruct((1, LANES), F32)],
        scratch_shapes=[pltpu.VMEM((HG_STEP, LANES, LANES), F32)] + [pltpu.VMEM((S, width), F32)] * 4
        + [pltpu.SemaphoreType.DMA((4,))],
        input_output_aliases={9: 0},
        compiler_params=_cparams("arbitrary", "arbitrary"))(proj, proj, proj, proj, oraw, dcat, states, lb, hgn, dproj)


def _lower_bound_fwd(logits, name):
    assert logits.shape[0] == 2

    def body(l_ref, o_ref):
        l0, l1 = l_ref[0:1, :], l_ref[1:2, :]
        m = jnp.maximum(l0, l1)
        e0, e1 = jnp.exp(l0 - m), jnp.exp(l1 - m)
        o_ref[0:1, :] = jnp.zeros_like(l0)
        o_ref[1:2, :] = e1 / (e0 + e1)

    return pl.pallas_call(body, name=name, out_shape=jax.ShapeDtypeStruct(logits.shape, F32))(logits)


def _lower_bound_bwd(logits, dlb, name):
    def body(l_ref, d_ref, o_ref):
        l0, l1 = l_ref[0:1, :], l_ref[1:2, :]
        m = jnp.maximum(l0, l1)
        e0, e1 = jnp.exp(l0 - m), jnp.exp(l1 - m)
        s1 = e1 / (e0 + e1)
        t = s1 * (1.0 - s1) * d_ref[1:2, :]
        o_ref[0:1, :] = -t
        o_ref[1:2, :] = t

    return pl.pallas_call(body, name=name, out_shape=jax.ShapeDtypeStruct(logits.shape, F32))(logits, dlb)


def _bucket_thresholds():
    dist = np.arange(WINDOW)
    max_exact = N_BUCKETS // 2
    large = max_exact + (np.log(np.maximum(dist, max_exact) / max_exact) / math.log(MAX_DISTANCE / max_exact)
                         * (N_BUCKETS - max_exact)).astype(np.int32)
    bucket = np.where(dist < max_exact, dist, np.minimum(large, N_BUCKETS - 1))
    assert np.all(np.diff(bucket) >= 0)
    return [int(np.argmax(bucket >= k)) if np.any(bucket >= k) else 10 ** 6 for k in range(1, N_BUCKETS)]


def _band_bucket():
    dist = _iota2((WINDOW, 2 * WINDOW), 0) + WINDOW - _iota2((WINDOW, 2 * WINDOW), 1)
    bucket = jnp.zeros((WINDOW, 2 * WINDOW), jnp.int32)
    for thr in _bucket_thresholds():
        bucket = bucket + (dist >= thr).astype(jnp.int32)
    band = jnp.logical_and(dist >= 0, dist < WINDOW)
    return bucket, band


def _bias_build(rel_bias, name):
    def body(rb_ref, o_ref):
        h = pl.program_id(0)
        bucket, _ = _band_bucket()
        bias = jnp.zeros((WINDOW, 2 * WINDOW), F32)
        for k in range(N_BUCKETS):
            bias = jnp.where(bucket == k, rb_ref[k, h], bias)
        o_ref[0] = bias

    return pl.pallas_call(
        body, name=name, grid=(SW_HEADS,), in_specs=[pl.BlockSpec(memory_space=pltpu.SMEM)],
        out_specs=pl.BlockSpec((1, WINDOW, 2 * WINDOW), lambda h: (h, 0, 0)),
        out_shape=jax.ShapeDtypeStruct((SW_HEADS, WINDOW, 2 * WINDOW), F32),
        compiler_params=_cparams("parallel"))(rel_bias)


def _bias_reduce(dbias, name):
    def body(d_ref, o_ref):
        bucket, band = _band_bucket()
        d = jnp.where(band, d_ref[0], 0.0)
        lane = _iota2((1, LANES), 1)
        out = jnp.zeros((1, LANES), F32)
        for k in range(N_BUCKETS):
            out = jnp.where(lane == k, jnp.sum(jnp.where(bucket == k, d, 0.0)), out)
        o_ref[0] = out

    return pl.pallas_call(
        body, name=name, grid=(SW_HEADS,), in_specs=[pl.BlockSpec((1, WINDOW, 2 * WINDOW), lambda h: (h, 0, 0))],
        out_specs=pl.BlockSpec((1, 1, LANES), lambda h: (h, 0, 0)),
        out_shape=jax.ShapeDtypeStruct((SW_HEADS, 1, LANES), F32), compiler_params=_cparams("parallel"))(dbias)


SW_Q_COLS = SW_HEADS * SW_DIM
SW_K_BLOCK0 = SW_Q_COLS // LANES
SW_V_BLOCK0 = SW_K_BLOCK0 + SW_KV * SW_DIM // LANES
SW_STEP_HEADS = 8


def _swa_probs(qb, kprev, kcur, bias_ref, hl, sink, mprev, mcur):
    scale = SW_DIM ** -0.5
    lp = jnp.where(mprev, _dot(qb, kprev, 1, 1) * scale + bias_ref[hl, :, 0:WINDOW], NEG)
    lc = jnp.where(mcur, _dot(qb, kcur, 1, 1) * scale + bias_ref[hl, :, WINDOW:2 * WINDOW], NEG)
    m = jnp.maximum(jnp.maximum(jnp.max(lp, axis=1, keepdims=True), jnp.max(lc, axis=1, keepdims=True)), sink)
    ep, ec = jnp.exp(lp - m), jnp.exp(lc - m)
    es = jnp.exp(sink - m)
    den = jnp.sum(ep, axis=1, keepdims=True) + jnp.sum(ec, axis=1, keepdims=True) + es
    return ep, ec, es, den


def _swa_fwd(qkn, proj, bias, sinks, S, name):
    T = qkn.shape[0]
    W = WINDOW
    nb = S // W

    def body(q_ref, k_ref, v_ref, bias_ref, sink_ref, o_ref):
        kp = pl.program_id(1)
        row, col = _iota2((W, W), 0), _iota2((W, W), 1)
        mcur = col <= row
        above = col > row

        def blk(n, carry):
            rows = pl.ds(pl.multiple_of(n * W, W), W)
            prow = pl.ds(pl.multiple_of(jnp.maximum(n - 1, 0) * W, W), W)
            mprev = jnp.logical_and(above, n > 0)
            for kvh in range(2):
                ksl = slice(kvh * SW_DIM, (kvh + 1) * SW_DIM)
                kcur, kprev = k_ref[rows, ksl], k_ref[prow, ksl]
                vcur, vprev = v_ref[rows, ksl], v_ref[prow, ksl]
                for g in range(4):
                    hl = kvh * 4 + g
                    qsl = slice(hl * SW_DIM, (hl + 1) * SW_DIM)
                    sink = sink_ref[kp * SW_STEP_HEADS + hl]
                    ep, ec, _, den = _swa_probs(q_ref[rows, qsl], kprev, kcur, bias_ref, hl, sink, mprev, mcur)
                    o_ref[rows, qsl] = (_dot(ep, vprev, 1, 0) + _dot(ec, vcur, 1, 0)) / den
            return carry

        lax.fori_loop(0, nb, blk, 0)

    return pl.pallas_call(
        body, name=name, grid=(T // S, 2),
        in_specs=[pl.BlockSpec((S, 4 * LANES), lambda b, kp: (b, kp)),
                  pl.BlockSpec((S, LANES), lambda b, kp: (b, SW_K_BLOCK0 + kp)),
                  pl.BlockSpec((S, LANES), lambda b, kp: (b, SW_V_BLOCK0 + kp)),
                  pl.BlockSpec((SW_STEP_HEADS, W, 2 * W), lambda b, kp: (kp, 0, 0)),
                  pl.BlockSpec(memory_space=pltpu.SMEM)],
        out_specs=pl.BlockSpec((S, 4 * LANES), lambda b, kp: (b, kp)),
        out_shape=jax.ShapeDtypeStruct((T, SW_Q_COLS), F32),
        compiler_params=_cparams("parallel", "parallel"))(qkn, qkn, proj, bias, sinks)


def _swa_bwd(qkn, proj, bias, sinks, do, S, name, exchange=None):
    T, width = proj.shape
    W = WINDOW
    nb = S // W
    scale = SW_DIM ** -0.5

    grid = (2, T // S)

    def body(*refs):
        (q_ref, k_ref, v_ref, bias_ref, sink_ref, do_ref, dp_hbm, dbias_ref, dsink_ref,
         dq_s, dk_s, dv_s, sems), carried = _carried(exchange, refs, 6, 3, 4)
        _start_carried(exchange, carried, grid)
        kp, b = pl.program_id(0), pl.program_id(1)
        row, col = _iota2((W, W), 0), _iota2((W, W), 1)
        mcur = col <= row
        above = col > row
        dk_s[...] = jnp.zeros_like(dk_s)
        dv_s[...] = jnp.zeros_like(dv_s)

        @pl.when(b == 0)
        def _():
            dbias_ref[...] = jnp.zeros_like(dbias_ref)
            dsink_ref[...] = jnp.zeros_like(dsink_ref)

        def blk(n, carry):
            rows = pl.ds(pl.multiple_of(n * W, W), W)
            prow = pl.ds(pl.multiple_of(jnp.maximum(n - 1, 0) * W, W), W)
            mprev = jnp.logical_and(above, n > 0)
            for kvh in range(2):
                ksl = slice(kvh * SW_DIM, (kvh + 1) * SW_DIM)
                kcur, kprev = k_ref[rows, ksl], k_ref[prow, ksl]
                vcur, vprev = v_ref[rows, ksl], v_ref[prow, ksl]
                for g in range(4):
                    hl = kvh * 4 + g
                    qsl = slice(hl * SW_DIM, (hl + 1) * SW_DIM)
                    sink = sink_ref[kp * SW_STEP_HEADS + hl]
                    qb = q_ref[rows, qsl]
                    ep, ec, es, den = _swa_probs(qb, kprev, kcur, bias_ref, hl, sink, mprev, mcur)
                    inv = 1.0 / den
                    pp, pc = ep * inv, ec * inv
                    dob = do_ref[rows, qsl]
                    dpp, dpc = _dot(dob, vprev, 1, 1), _dot(dob, vcur, 1, 1)
                    total = jnp.sum(pp * dpp, axis=1, keepdims=True) + jnp.sum(pc * dpc, axis=1, keepdims=True)
                    dlp = pp * (dpp - total)
                    dlc = pc * (dpc - total)
                    dsink_ref[hl:hl + 1, :] += jnp.zeros((1, LANES), F32) - jnp.sum(es * inv * total)
                    dbias_ref[hl, :, 0:W] += dlp
                    dbias_ref[hl, :, W:2 * W] += dlc
                    dq_s[rows, qsl] = (_dot(dlp, kprev, 1, 0) + _dot(dlc, kcur, 1, 0)) * scale
                    dk_s[prow, ksl] += _dot(dlp, qb, 0, 0) * scale
                    dk_s[rows, ksl] += _dot(dlc, qb, 0, 0) * scale
                    dv_s[prow, ksl] += _dot(pp, dob, 0, 0)
                    dv_s[rows, ksl] += _dot(pc, dob, 0, 0)
            return carry

        lax.fori_loop(0, nb, blk, 0)
        r0 = pl.multiple_of(b * S, S)
        cq = pl.multiple_of(kp * 4 * LANES, LANES)
        ck = pl.multiple_of((SW_K_BLOCK0 + kp) * LANES, LANES)
        cv = pl.multiple_of((SW_V_BLOCK0 + kp) * LANES, LANES)
        copies = [pltpu.make_async_copy(dq_s, dp_hbm.at[pl.ds(r0, S), pl.ds(cq, 4 * LANES)], sems.at[0]),
                  pltpu.make_async_copy(dk_s, dp_hbm.at[pl.ds(r0, S), pl.ds(ck, LANES)], sems.at[1]),
                  pltpu.make_async_copy(dv_s, dp_hbm.at[pl.ds(r0, S), pl.ds(cv, LANES)], sems.at[2])]
        for cp in copies:
            cp.start()
        for cp in copies:
            cp.wait()
        _wait_carried(exchange, carried, grid)

    qspec = pl.BlockSpec((S, 4 * LANES), lambda kp, b: (b, kp))
    outs, targets = _carrier_call(
        body, name, grid,
        [qspec, pl.BlockSpec((S, LANES), lambda kp, b: (b, SW_K_BLOCK0 + kp)),
         pl.BlockSpec((S, LANES), lambda kp, b: (b, SW_V_BLOCK0 + kp)),
         pl.BlockSpec((SW_STEP_HEADS, W, 2 * W), lambda kp, b: (kp, 0, 0)),
         pl.BlockSpec(memory_space=pltpu.SMEM), qspec],
        [ANY, pl.BlockSpec((SW_STEP_HEADS, W, 2 * W), lambda kp, b: (kp, 0, 0)),
         pl.BlockSpec((SW_STEP_HEADS, LANES), lambda kp, b: (kp, 0))],
        [jax.ShapeDtypeStruct((T, width), F32), jax.ShapeDtypeStruct((SW_HEADS, W, 2 * W), F32),
         jax.ShapeDtypeStruct((SW_HEADS, LANES), F32)],
        [pltpu.VMEM((S, 4 * LANES), F32), pltpu.VMEM((S, LANES), F32), pltpu.VMEM((S, LANES), F32),
         pltpu.SemaphoreType.DMA((3,))], {}, (qkn, qkn, proj, bias, sinks, do), exchange)
    return (*outs, targets)


CHIP_FLIPS = ((1, 0), (0, 1), (1, 1))


def _flip(v, f):
    return 1 - v if f else v


class _Exchange:
    def __init__(self, sources, targets, copies):
        self.sources, self.targets = list(sources), list(targets)
        self._copies = copies
        n = len(self.sources)
        self.scratch = [pltpu.SemaphoreType.DMA((n, 3)), pltpu.SemaphoreType.DMA((n, 3)),
                        pltpu.SemaphoreType.DMA((n,))]

    def _descriptors(self, srcs, dsts, sems, chip, peers):
        send, recv, loc = sems
        out = []
        for t, (local, remote) in enumerate(self._copies(srcs, dsts, chip)):
            out.append(pltpu.make_async_copy(local[0], local[1], loc.at[t]))
            for r, (src, dst) in enumerate(remote):
                out.append(pltpu.make_async_remote_copy(src, dst, send.at[t, r], recv.at[t, r],
                                                        device_id=peers[r], device_id_type=MESH))
        return out

    def start(self, srcs, dsts, sems):
        x, y, c = lax.axis_index("x"), lax.axis_index("y"), lax.axis_index("c")
        peers = [(_flip(x, fx), _flip(y, fy), c) for fx, fy in CHIP_FLIPS]
        for chip in range(4):
            @pl.when(2 * x + y == chip)
            def _():
                for cp in self._descriptors(srcs, dsts, sems, chip, peers):
                    cp.start()

    def wait(self, srcs, dsts, sems):
        me = (lax.axis_index("x"), lax.axis_index("y"), lax.axis_index("c"))
        for cp in self._descriptors(srcs, dsts, sems, 0, [me] * 3):
            cp.wait()

    def operands(self):
        return self.sources + self.targets

    def specs(self):
        ns, nt = len(self.sources), len(self.targets)
        return [ANY] * (ns + nt), [ANY] * nt, [jax.ShapeDtypeStruct(t.shape, t.dtype) for t in self.targets]

    def aliases(self, n_in, n_out):
        ns = len(self.sources)
        return {n_in + ns + i: n_out + i for i in range(len(self.targets))}

    def split(self, refs, n_in, n_out, n_scr):
        ns, nt = len(self.sources), len(self.targets)
        o0 = n_in + ns + nt
        s0 = o0 + n_out + nt
        own = list(refs[:n_in]) + list(refs[o0:o0 + n_out]) + list(refs[s0:s0 + n_scr])
        return own, (refs[n_in:n_in + ns], refs[o0 + n_out:o0 + n_out + nt], refs[s0 + n_scr:])


def _carried(exchange, refs, n_in, n_out, n_scr):
    if exchange is None:
        return list(refs), None
    return exchange.split(refs, n_in, n_out, n_scr)


def _grid_edge(grid, last):
    conds = [pl.program_id(d) == (n - 1 if last else 0) for d, n in enumerate(grid)]
    out = conds[0]
    for cnd in conds[1:]:
        out = jnp.logical_and(out, cnd)
    return out


def _start_carried(exchange, parts, grid):
    if parts is not None:
        @pl.when(_grid_edge(grid, False))
        def _():
            exchange.start(*parts)


def _wait_carried(exchange, parts, grid):
    if parts is not None:
        @pl.when(_grid_edge(grid, True))
        def _():
            exchange.wait(*parts)


def _carrier_call(body, name, grid, in_specs, out_specs, out_shape, scratch, aliases, operands, exchange):
    n_out = len(out_shape)
    aliases = dict(aliases)
    if exchange is not None:
        ex_in, ex_out, ex_shape = exchange.specs()
        aliases.update(exchange.aliases(len(in_specs), n_out))
        in_specs, out_specs, out_shape = in_specs + ex_in, out_specs + ex_out, out_shape + ex_shape
        scratch = scratch + exchange.scratch
        operands = list(operands) + exchange.operands()
    outs = pl.pallas_call(body, name=name, grid=grid, in_specs=in_specs, out_specs=out_specs, out_shape=out_shape,
                          scratch_shapes=scratch, input_output_aliases=aliases,
                          compiler_params=_cparams(*["arbitrary"] * len(grid)))(*operands)
    return outs[:n_out], outs[n_out:]


def _exchange_call(exchange, name):
    in_specs, out_specs, out_shape = exchange.specs()

    def body(*refs):
        _, parts = exchange.split(refs, 0, 0, 0)
        exchange.start(*parts)
        exchange.wait(*parts)

    return pl.pallas_call(body, name=name, in_specs=in_specs, out_specs=out_specs, out_shape=out_shape,
                          scratch_shapes=exchange.scratch, input_output_aliases=exchange.aliases(0, 0))(
        *exchange.operands())


def _gather_exchange(shards, fulls, axes, layers):
    sizes = [s.shape[a] for s, a in zip(shards, axes)]

    def copies(srcs, dsts, chip):
        out = []
        for src, full, axis, size, (l0, l1) in zip(srcs, dsts, axes, sizes, layers):
            part = src.at[l0:l1]
            cut = pl.ds(chip * size, size)
            dst = full.at[l0:l1, cut, :] if axis == 1 else full.at[l0:l1, :, cut]
            out.append(((part, dst), [(part, dst)] * 3))
        return out

    return _Exchange(shards, fulls, copies)


def _scatter_exchange(grads, stacks, axes, layers):
    sizes = [g.shape[a - 1] // 4 for g, a in zip(grads, axes)]

    def copies(srcs, dsts, chip):
        out = []
        for g, stack, axis, size, layer in zip(srcs, dsts, axes, sizes, layers):
            def cut(j, g=g, axis=axis, size=size):
                return g.at[pl.ds(j * size, size), :] if axis == 1 else g.at[:, pl.ds(j * size, size)]

            remote = [(cut(chip ^ (2 * fx + fy)), stack.at[r, layer]) for r, (fx, fy) in enumerate(CHIP_FLIPS)]
            out.append(((cut(chip), stack.at[3, layer]), remote))
        return out

    return _Exchange(grads, stacks, copies)


def _swap_with_sibling(parts, name):
    n = len(parts)

    def body(*refs):
        ins, outs = refs[:n], refs[n:2 * n]
        send, recv = refs[2 * n:]
        peer = (lax.axis_index("x"), lax.axis_index("y"), 1 - lax.axis_index("c"))
        copies = [pltpu.make_async_remote_copy(ins[t], outs[t], send.at[t], recv.at[t], device_id=peer,
                                               device_id_type=MESH) for t in range(n)]
        for cp in copies:
            cp.start()
        for cp in copies:
            cp.wait()

    return pl.pallas_call(
        body, name=name, in_specs=[ANY] * n, out_specs=[ANY] * n,
        out_shape=[jax.ShapeDtypeStruct(p.shape, p.dtype) for p in parts],
        scratch_shapes=[pltpu.SemaphoreType.DMA((n,)), pltpu.SemaphoreType.DMA((n,))])(*parts)


def _allreduce_small(v, name):
    R = v.shape[0]
    ND = 8

    def body(v_ref, o_ref, buf, send, recv):
        x, y, c = lax.axis_index("x"), lax.axis_index("y"), lax.axis_index("c")
        me = 4 * x + 2 * y + c
        copies = []
        for d in range(1, ND):
            peer = (_flip(x, d >> 2 & 1), _flip(y, d >> 1 & 1), _flip(c, d & 1))
            copies.append(pltpu.make_async_remote_copy(v_ref, buf.at[me], send.at[d], recv.at[me], device_id=peer,
                                                       device_id_type=MESH))
        for cp in copies:
            cp.start()
        buf[pl.ds(me, 1)] = v_ref[...][None]
        for k in range(ND):
            @pl.when(me != k)
            def _():
                pltpu.make_async_remote_copy(v_ref, buf.at[k], send.at[0], recv.at[k], device_id=(x, y, c),
                                             device_id_type=MESH).wait_recv()
        for cp in copies:
            cp.wait_send()
        total = buf[0]
        for k in range(1, ND):
            total = total + buf[k]
        o_ref[...] = total

    vm = pl.BlockSpec(memory_space=pltpu.VMEM)
    return pl.pallas_call(
        body, name=name, in_specs=[vm], out_specs=vm, out_shape=jax.ShapeDtypeStruct((R, LANES), F32),
        scratch_shapes=[pltpu.VMEM((ND, R, LANES), F32), pltpu.SemaphoreType.DMA((ND,)),
                        pltpu.SemaphoreType.DMA((ND,))],
        compiler_params=pltpu.CompilerParams(vmem_limit_bytes=VMEM_LIMIT))(v)


def _tile2(R, Cn):
    tc = _pick(Cn, 2048)
    tr = R
    for cand in (256, 128, 64, 32, 16, 8):
        if R % cand == 0:
            tr = cand
            break
    return tr, tc


def _sum4(stack, name):
    _, R, Cn = stack.shape
    tr, tc = _tile2(R, Cn)

    def body(s_ref, o_ref):
        o_ref[...] = ((s_ref[0].astype(F32) + s_ref[1].astype(F32)) + s_ref[2].astype(F32)) + s_ref[3].astype(F32)

    return pl.pallas_call(
        body, name=name, grid=(R // tr, Cn // tc), in_specs=[pl.BlockSpec((4, tr, tc), lambda i, j: (0, i, j))],
        out_specs=pl.BlockSpec((tr, tc), lambda i, j: (i, j)), out_shape=jax.ShapeDtypeStruct((R, Cn), F32),
        compiler_params=_cparams("parallel", "parallel"))(stack)


def _adamw(w, m, v, g_parts, name):
    R, Cn = w.shape
    tr, tc = _tile2(R, Cn)
    npart = len(g_parts)
    c1 = 1.0 / (1.0 - ADAM_B1 ** ADAM_STEP)
    c2 = 1.0 / (1.0 - ADAM_B2 ** ADAM_STEP)

    def body(*refs):
        w_ref, m_ref, v_ref = refs[:3]
        g_refs = refs[3:3 + npart]
        g_out, d_out, m_out, v_out = refs[3 + npart:]
        g = g_refs[0][...]
        for r in g_refs[1:]:
            g = g + r[...]
        mn = ADAM_B1 * m_ref[...] + (1.0 - ADAM_B1) * g
        vn = ADAM_B2 * v_ref[...] + (1.0 - ADAM_B2) * (g * g)
        g_out[...] = g
        m_out[...] = mn
        v_out[...] = vn
        d_out[...] = -ADAM_LR * ((mn * c1) / (jnp.sqrt(vn * c2) + ADAM_EPS) + ADAM_WD * w_ref[...])

    spec = pl.BlockSpec((tr, tc), lambda i, j: (i, j))
    return pl.pallas_call(
        body, name=name, grid=(R // tr, Cn // tc), in_specs=[spec] * (3 + npart), out_specs=[spec] * 4,
        out_shape=[jax.ShapeDtypeStruct((R, Cn), F32)] * 4,
        compiler_params=_cparams("parallel", "parallel"))(w, m, v, *g_parts)


SHARDED = (("ab_w_in", 2), ("ab_w_out", 1), ("c_w_in", 2), ("c_w_out", 1), ("ffn_up", 2), ("ffn_conv", 2),
           ("ffn_down", 1), ("ple_gate", 1), ("ple_proj", 2))
SMALL = ("mix_norm", "hg_lb_logits", "hg_out_norm", "q_norm", "k_norm", "sinks", "rel_bias", "ffn_norm",
         "ffn_conv_b", "ple_norm")
WEIGHTS = ("mix_norm", "ab_w_in", "hg_lb_logits", "hg_out_norm", "ab_w_out", "c_w_in", "q_norm", "k_norm", "sinks",
           "rel_bias", "c_w_out", "ffn_norm", "ffn_up", "ffn_conv", "ffn_conv_b", "ffn_down", "ple_norm", "ple_gate",
           "ple_proj")
PACK_ALIGN = 8 * LANES


def _pack(arrs):
    pieces = []
    for a in arrs:
        flat = a.reshape(-1)
        pad = -flat.shape[0] % PACK_ALIGN
        pieces.append(jnp.pad(flat, (0, pad)).reshape(-1, LANES))
    return jnp.concatenate(pieces, axis=0)


def _unpack(packed, like):
    out, r = [], 0
    for a in like:
        size = int(np.prod(a.shape))
        rows = (size + PACK_ALIGN - 1) // PACK_ALIGN * 8
        out.append(packed[r:r + rows].reshape(-1)[:size].reshape(a.shape))
        r += rows
    return out


def _family_range(name, lo, hi):
    if name.startswith("ab_"):
        idx = [i // 2 for i in range(lo, hi) if i % 2 == 0]
    elif name.startswith("c_"):
        idx = [i // 2 for i in range(lo, hi) if i % 2 == 1]
    else:
        idx = list(range(lo, hi))
    return (idx[0], idx[-1] + 1) if idx else None


class _StepExchanges:
    def __init__(self, shards, W):
        self.shards, self.W = shards, W
        self.stacks = {}
        for k, axis in SHARDED:
            shp = shards[k].shape
            self.W[k] = lax.empty(tuple(4 * d if i == axis else d for i, d in enumerate(shp)), shards[k].dtype)
            self.stacks[k] = lax.empty((4,) + shp, shards[k].dtype)

    def gather(self, lo, hi):
        sel = [(k, axis, _family_range(k, lo, hi)) for k, axis in SHARDED]
        sel = [s for s in sel if s[2] is not None]
        if not sel:
            return None, None
        names = [k for k, _, _ in sel]
        ex = _gather_exchange([self.shards[k] for k in names], [self.W[k] for k in names],
                              [a for _, a, _ in sel], [r for _, _, r in sel])
        return ex, (self.W, names)

    def scatter(self, layer, G):
        sel = [(k, axis, _family_range(k, layer, layer + 1)) for k, axis in SHARDED]
        sel = [s for s in sel if s[2] is not None]
        names = [k for k, _, _ in sel]
        ex = _scatter_exchange([G[k][r[0]] for k, _, r in sel], [self.stacks[k] for k in names],
                               [a for _, a, _ in sel], [r[0] for _, _, r in sel])
        return ex, (self.stacks, names)

    @staticmethod
    def adopt(where, targets):
        if where is not None:
            book, names = where
            for k, t in zip(names, targets):
                book[k] = t


def _forward_backward(x, p, target, W, S, exchanges=None):
    T = x.shape[0]
    depth = p.shape[0]
    lb = _lower_bound_fwd(W["hg_lb_logits"], "lower_bound_fwd")
    bias = _bias_build(W["rel_bias"], "bias_build")
    qk_gain = jnp.concatenate([jnp.tile(W["q_norm"], (1, SW_HEADS)), jnp.tile(W["k_norm"], (1, SW_KV))], axis=1)

    def mm(a, wname, layer, mode, name, **kw):
        return _matmul(a, W[wname], mode, name, b_layer=layer, **kw)

    saved = []
    h = x
    for i in range(depth):
        j = i // 2
        s = {"h0": h}
        s["hn"] = _rmsnorm_fwd(h, W["mix_norm"][i:i + 1], f"mix_norm_fwd_{i}")
        if i % 2 == 0:
            s["proj"] = mm(s["hn"], "ab_w_in", j, "nn", f"ab_in_{i}")
            ex, where = exchanges.gather(i + 1, min(i + 3, depth)) if exchanges else (None, None)
            cat, s["sb_tot"], arrived = _sb_fwd(s["proj"], S, f"sb_fwd_{i}", ex)
            _StepExchanges.adopt(where, arrived)
            s["cat"], s["oraw"], s["states"] = _hgrn_fwd(s["proj"], cat, lb[j:j + 1], W["hg_out_norm"][j:j + 1], S,
                                                         f"hgrn_fwd_{i}")
            h = mm(s["cat"], "ab_w_out", j, "nn", f"ab_out_{i}", res=h)
        else:
            s["proj"] = mm(s["hn"], "c_w_in", j, "nn", f"c_in_{i}")
            s["qkn"] = _headnorm_fwd(s["proj"], qk_gain[j:j + 1], f"qk_norm_fwd_{i}")
            s["o"] = _swa_fwd(s["qkn"], s["proj"], bias, W["sinks"][j], S, f"swa_fwd_{i}")
            h = mm(s["o"], "c_w_out", j, "nn", f"c_out_{i}", res=h)
        s["h1"] = h
        s["hn2"] = _rmsnorm_fwd(h, W["ffn_norm"][i:i + 1], f"ffn_norm_fwd_{i}")
        s["u"] = mm(s["hn2"], "ffn_up", i, "nn", f"ffn_up_{i}")
        s["a"] = _convglu_fwd(s["u"], W["ffn_conv"][i], W["ffn_conv_b"][i:i + 1], S, f"convglu_fwd_{i}")
        h = mm(s["a"], "ffn_down", i, "nn", f"ffn_down_{i}", res=h)
        s["h2"] = h
        s["hn3"] = _rmsnorm_fwd(h, W["ple_norm"][i:i + 1], f"ple_norm_fwd_{i}")
        s["z"] = mm(s["hn3"], "ple_gate", i, "nn", f"ple_gate_{i}")
        s["pp"] = mm(p, "ple_proj", i, "nn", f"ple_proj_{i}", a_layer=i)
        h = _ple_fwd(h, s["z"], s["pp"], f"ple_fwd_{i}")
        saved.append(s)

    loss, dh = _loss_fwd_bwd(h, target, "loss")

    G = {k: [None] * depth for k in ("mix_norm", "ffn_norm", "ple_norm", "ffn_up", "ffn_conv", "ffn_conv_b",
                                     "ffn_down", "ple_gate", "ple_proj")}
    for k in ("ab_w_in", "ab_w_out", "c_w_in", "c_w_out", "hg_out_norm", "q_norm", "k_norm", "sinks", "lb"):
        G[k] = [None] * (depth // 2)
    dbias_total = None
    for i in reversed(range(depth)):
        j = i // 2
        s = saved[i]
        dz, dpp = _ple_bwd(dh, s["z"], s["pp"], f"ple_bwd_{i}")
        G["ple_proj"][i] = _matmul(p, dpp, "tn", f"d_ple_proj_{i}", out_dtype=BF16, a_layer=i)
        G["ple_gate"][i] = _matmul(s["hn3"], dz, "tn", f"d_ple_gate_{i}", out_dtype=BF16)
        dhn = mm(dz, "ple_gate", i, "nt", f"d_hn3_{i}")
        dh, G["ple_norm"][i] = _rmsnorm_bwd(s["h2"], W["ple_norm"][i:i + 1], dhn, dh, f"ple_norm_bwd_{i}")

        half_ff = D_FF // 2
        da = mm(dh, "ffn_down", i, "nt", f"d_a_{i}", tiles=(min(T, 1024), half_ff, D_MODEL))
        G["ffn_down"][i] = _matmul(s["a"], dh, "tn", f"d_ffn_down_{i}", out_dtype=BF16,
                                   tiles=(half_ff, 512, min(T, 2048)))
        du, dcw, dcb = _convglu_bwd(s["u"], da, W["ffn_conv"][i], W["ffn_conv_b"][i:i + 1], S, f"convglu_bwd_{i}")
        G["ffn_conv"][i] = jnp.swapaxes(dcw, 0, 1).reshape(3, 2 * D_FF)
        G["ffn_conv_b"][i] = dcb.reshape(1, 2 * D_FF)
        G["ffn_up"][i] = _matmul(s["hn2"], du, "tn", f"d_ffn_up_{i}", out_dtype=BF16,
                                 tiles=(D_MODEL, half_ff, min(T, 1024)))
        dhn = mm(du, "ffn_up", i, "nt", f"d_hn2_{i}")
        dh, G["ffn_norm"][i] = _rmsnorm_bwd(s["h1"], W["ffn_norm"][i:i + 1], dhn, dh, f"ffn_norm_bwd_{i}")

        ex, where = exchanges.scatter(i + 1, G) if exchanges and i + 1 < depth else (None, None)
        if i % 2 == 0:
            dcat = mm(dh, "ab_w_out", j, "nt", f"d_cat_{i}")
            G["ab_w_out"][j] = _matmul(s["cat"], dh, "tn", f"d_ab_out_{i}", out_dtype=BF16)
            dproj, sent = _sb_bwd(s["proj"], s["sb_tot"], dcat, S, f"sb_bwd_{i}", ex)
            dproj, G["lb"][j], G["hg_out_norm"][j] = _hgrn_bwd(s["proj"], s["oraw"], dcat, s["states"], lb[j:j + 1],
                                                               W["hg_out_norm"][j:j + 1], dproj, S, f"hgrn_bwd_{i}")
            G["ab_w_in"][j] = _matmul(s["hn"], dproj, "tn", f"d_ab_in_{i}", out_dtype=BF16)
            dhn = mm(dproj, "ab_w_in", j, "nt", f"d_hn_{i}")
        else:
            do = mm(dh, "c_w_out", j, "nt", f"d_o_{i}")
            G["c_w_out"][j] = _matmul(s["o"], dh, "tn", f"d_c_out_{i}", out_dtype=BF16)
            dqkv, dbias, dsink, sent = _swa_bwd(s["qkn"], s["proj"], bias, W["sinks"][j], do, S, f"swa_bwd_{i}", ex)
            dbias_total = dbias if dbias_total is None else dbias_total + dbias
            G["sinks"][j] = dsink[:, 0]
            dproj, dgain = _headnorm_bwd(s["proj"], qk_gain[j:j + 1], dqkv, f"qk_norm_bwd_{i}")
            G["q_norm"][j] = dgain[0, :SW_Q_COLS].reshape(SW_HEADS, SW_DIM).sum(axis=0)
            G["k_norm"][j] = dgain[0, SW_Q_COLS:].reshape(SW_KV, SW_DIM).sum(axis=0)
            G["c_w_in"][j] = _matmul(s["hn"], dproj, "tn", f"d_c_in_{i}", out_dtype=BF16)
            dhn = mm(dproj, "c_w_in", j, "nt", f"d_hn_{i}")
        _StepExchanges.adopt(where, sent)
        dh, G["mix_norm"][i] = _rmsnorm_bwd(s["h0"], W["mix_norm"][i:i + 1], dhn, dh, f"mix_norm_bwd_{i}")

    grads = {k: G[k] for k, _ in SHARDED}
    for k in ("q_norm", "k_norm", "sinks"):
        grads[k] = jnp.stack(G[k])
    for k in ("mix_norm", "ffn_norm", "ple_norm", "ffn_conv_b", "hg_out_norm"):
        grads[k] = jnp.concatenate(G[k], axis=0)
    grads["hg_lb_logits"] = _lower_bound_bwd(W["hg_lb_logits"], jnp.concatenate(G["lb"], axis=0), "lower_bound_bwd")
    grads["rel_bias"] = _bias_reduce(dbias_total, "bias_reduce")[:, 0, :N_BUCKETS].T
    return loss, dh, grads


def kernel(x, p, mix_norm, ab_w_in, hg_lb_logits, hg_out_norm, ab_w_out, c_w_in, q_norm, k_norm, sinks, rel_bias, c_w_out, ffn_norm, ffn_up, ffn_conv, ffn_conv_b, ffn_down, ple_norm, ple_gate, ple_proj, loss_target, m_mix_norm, m_ab_w_in, m_hg_lb_logits, m_hg_out_norm, m_ab_w_out, m_c_w_in, m_q_norm, m_k_norm, m_sinks, m_rel_bias, m_c_w_out, m_ffn_norm, m_ffn_up, m_ffn_conv, m_ffn_conv_b, m_ffn_down, m_ple_norm, m_ple_gate, m_ple_proj, v_mix_norm, v_ab_w_in, v_hg_lb_logits, v_hg_out_norm, v_ab_w_out, v_c_w_in, v_q_norm, v_k_norm, v_sinks, v_rel_bias, v_c_w_out, v_ffn_norm, v_ffn_up, v_ffn_conv, v_ffn_conv_b, v_ffn_down, v_ple_norm, v_ple_gate, v_ple_proj):
    args = dict(locals())
    w = {k: args[k] for k in WEIGHTS}
    m = {k: args["m_" + k] for k in WEIGHTS}
    v = {k: args["v_" + k] for k in WEIGHTS}
    B, S, Dm = x.shape
    T = B * S
    names = [k for k, _ in SHARDED]

    W = {k: w[k] for k in SMALL}
    exchanges = _StepExchanges({k: w[k].astype(F32 if k == "ffn_conv" else BF16) for k in names}, W)
    first, where = exchanges.gather(0, 1)
    exchanges.adopt(where, _exchange_call(first, "gather_layer0"))

    loss, dx, grads = _forward_backward(x.reshape(T, Dm), p.reshape(p.shape[0], T, p.shape[-1]),
                                        loss_target.reshape(T, Dm), W, S, exchanges)
    loss = lax.psum(loss[0, 0], ("x", "y", "c"))

    last, where = exchanges.scatter(0, grads)
    exchanges.adopt(where, _exchange_call(last, "scatter_layer0"))
    stacks = [exchanges.stacks[k] for k in names]
    partial = [_sum4(st.reshape(4, -1, st.shape[-1]), f"sum_chips_{k}") for k, st in zip(names, stacks)]
    other = _swap_with_sibling(partial, "swap_core_sums")
    small_sum = _allreduce_small(_pack([grads[k] for k in SMALL]), "allreduce_small")

    out_g, out_d, out_m, out_v = {}, {}, {}, {}
    for k, mine, theirs in zip(names, partial, other):
        shp = w[k].shape
        r = [a.reshape(shp) for a in _adamw(w[k].reshape(mine.shape), m[k].reshape(mine.shape),
                                            v[k].reshape(mine.shape), [mine, theirs], f"adamw_{k}")]
        out_g[k], out_d[k], out_m[k], out_v[k] = r
    sm = _adamw(_pack([w[k] for k in SMALL]), _pack([m[k] for k in SMALL]), _pack([v[k] for k in SMALL]),
                [small_sum], "adamw_small")
    like = [w[k] for k in SMALL]
    for dst, packed in zip((out_g, out_d, out_m, out_v), sm):
        for k, a in zip(SMALL, _unpack(packed, like)):
            dst[k] = a

    return (loss, dx.reshape(B, S, Dm), *[out_g[k] for k in WEIGHTS], *[out_d[k] for k in WEIGHTS],
            *[out_m[k] for k in WEIGHTS], *[out_v[k] for k in WEIGHTS])
```

```python
import math

import numpy as np
import jax
import jax.numpy as jnp
from jax import lax
from jax.experimental import pallas as pl
from jax.experimental.pallas import tpu as pltpu

F32 = jnp.float32
BF16 = jnp.bfloat16
MESH = pl.DeviceIdType.MESH
ANY = pl.BlockSpec(memory_space=pl.ANY)

D_MODEL = 1024
EPS = 1e-6
SB_HEADS, SB_DIM = 8, 64
HG_HEADS, HG_DK = 4, 128
HG_CHUNK = 32
HG_STEP = 2
SW_HEADS, SW_KV, SW_DIM, WINDOW = 16, 4, 64, 128
N_BUCKETS, MAX_DISTANCE = 32, 128
D_FF = 2816
ATT_BLOCK = 128
SB_QBLOCK = 256
LANES = 128
NEG = -1e30

ADAM_LR, ADAM_B1, ADAM_B2, ADAM_EPS, ADAM_WD, ADAM_STEP = 0.001, 0.9, 0.999, 1e-08, 0.01, 10

VMEM_LIMIT = 56 * 1024 * 1024


def _cparams(*sem):
    return pltpu.CompilerParams(dimension_semantics=sem, vmem_limit_bytes=VMEM_LIMIT)


def _pick(n, cap):
    if n <= cap:
        return n
    best = None
    for d in range(LANES, cap + 1, LANES):
        if n % d == 0:
            best = d
    assert best is not None, (n, cap)
    return best


def _dot(a, b, ca, cb):
    return lax.dot_general(a.astype(BF16), b.astype(BF16), (((ca,), (cb,)), ((), ())),
                           preferred_element_type=F32)


def _split(x, terms):
    parts = []
    for _ in range(terms):
        hi = x.astype(BF16)
        parts.append(hi)
        x = x - hi.astype(F32)
    return parts


def _dot_exact_l(x, m, terms=2):
    out = None
    for p in _split(x, terms):
        t = lax.dot_general(p, m, (((1,), (0,)), ((), ())), preferred_element_type=F32)
        out = t if out is None else out + t
    return out


def _dot_exact_r(m, x, terms=3, cm=1):
    out = None
    for p in _split(x, terms):
        t = lax.dot_general(m, p, (((cm,), (0,)), ((), ())), preferred_element_type=F32)
        out = t if out is None else out + t
    return out


def _sig(x):
    return 1.0 / (1.0 + jnp.exp(-x))


def _iota2(shape, dim):
    return lax.broadcasted_iota(jnp.int32, shape, dim)


def _operand_spec(arr, layer, blk, index):
    if arr.ndim == 2:
        return pl.BlockSpec(blk, index)
    if layer is not None:
        return pl.BlockSpec((None,) + blk, lambda i, j, k: (layer,) + index(i, j, k))
    per_half = arr.shape[2] // blk[1]

    def halves(i, j, k):
        r, c = index(i, j, k)
        return (c // per_half, r, c % per_half)

    return pl.BlockSpec((None,) + blk, halves)


def _matmul(a, b, mode, name, out_dtype=F32, res=None, a_layer=None, b_layer=None, tiles=None):
    def dims(arr, layer):
        if arr.ndim == 2:
            return arr.shape
        return arr.shape[1:] if layer is not None else (arr.shape[1], 2 * arr.shape[2])

    (a0, a1), (b0, b1) = dims(a, a_layer), dims(b, b_layer)
    if mode == "nn":
        M, K, N = a0, a1, b1
    elif mode == "nt":
        M, K, N = a0, a1, b0
    else:
        K, M, N = a0, a1, b1
    cap_m, cap_n, cap_k = 1024, 1024, (1024 if mode == "tn" else 2048)
    tm, tn, tk = _pick(M, cap_m), _pick(N, cap_n), _pick(K, cap_k)
    if a.ndim == 3 and a_layer is None:
        if mode == "tn":
            tm = _pick(a.shape[2], cap_m)
        else:
            tk = _pick(a.shape[2], cap_k)
    if b.ndim == 3 and b_layer is None:
        if mode == "nt":
            tk = _pick(b.shape[2], cap_k)
        else:
            tn = _pick(b.shape[2], cap_n)
    if tiles is not None:
        tm, tn, tk = tiles
    assert M % tm == 0 and N % tn == 0 and K % tk == 0, (name, M, N, K, tm, tn, tk)
    nk = K // tk
    if mode == "tn":
        a_spec = _operand_spec(a, a_layer, (tk, tm), lambda i, j, k: (k, i))
    else:
        a_spec = _operand_spec(a, a_layer, (tm, tk), lambda i, j, k: (i, k))
    if mode == "nt":
        b_spec = _operand_spec(b, b_layer, (tn, tk), lambda i, j, k: (j, k))
    else:
        b_spec = _operand_spec(b, b_layer, (tk, tn), lambda i, j, k: (k, j))
    ca, cb = {"nn": (1, 0), "nt": (1, 1), "tn": (0, 0)}[mode]
    o_spec = pl.BlockSpec((tm, tn), lambda i, j, k: (i, j))

    def body(*refs):
        if res is None:
            a_ref, b_ref, o_ref, acc = refs
        else:
            a_ref, b_ref, r_ref, o_ref, acc = refs
        k = pl.program_id(2)

        @pl.when(k == 0)
        def _():
            acc[...] = jnp.zeros_like(acc)

        acc[...] += _dot(a_ref[...], b_ref[...], ca, cb)

        @pl.when(k == nk - 1)
        def _():
            r = acc[...]
            if res is not None:
                r = r + r_ref[...]
            o_ref[...] = r.astype(out_dtype)

    ins = [a, b] + ([] if res is None else [res])
    in_specs = [a_spec, b_spec] + ([] if res is None else [o_spec])
    return pl.pallas_call(
        body, name=name, grid=(M // tm, N // tn, nk), in_specs=in_specs, out_specs=o_spec,
        out_shape=jax.ShapeDtypeStruct((M, N), out_dtype),
        scratch_shapes=[pltpu.VMEM((tm, tn), F32)],
        compiler_params=_cparams("parallel", "parallel", "arbitrary"))(*ins)


ROW_TILE = 512


def _row_spec(width):
    return pl.BlockSpec((ROW_TILE, width), lambda i: (i, 0))


def _vec_spec(width):
    return pl.BlockSpec((1, width), lambda i: (0, 0))


def _rmsnorm_fwd(h, g, name):
    T, Dm = h.shape

    def body(h_ref, g_ref, o_ref):
        x = h_ref[...]
        r = lax.rsqrt(jnp.mean(x * x, axis=1, keepdims=True) + EPS)
        o_ref[...] = (x * r * g_ref[...]).astype(BF16)

    return pl.pallas_call(
        body, name=name, grid=(T // ROW_TILE,), in_specs=[_row_spec(Dm), _vec_spec(Dm)],
        out_specs=_row_spec(Dm), out_shape=jax.ShapeDtypeStruct((T, Dm), BF16),
        compiler_params=_cparams("parallel"))(h, g)


def _rmsnorm_bwd(h, g, dhn, dres, name):
    T, Dm = h.shape

    def body(h_ref, g_ref, dy_ref, dr_ref, dh_ref, dg_ref):
        i = pl.program_id(0)
        x = h_ref[...]
        dy = dy_ref[...]
        r = lax.rsqrt(jnp.mean(x * x, axis=1, keepdims=True) + EPS)
        gdy = dy * g_ref[...]
        m = jnp.mean(x * gdy, axis=1, keepdims=True)
        dh_ref[...] = dr_ref[...] + r * gdy - x * (r * r * r * m)
        part = jnp.sum(dy * x * r, axis=0, keepdims=True)

        @pl.when(i == 0)
        def _():
            dg_ref[...] = part

        @pl.when(i > 0)
        def _():
            dg_ref[...] += part

    return pl.pallas_call(
        body, name=name, grid=(T // ROW_TILE,),
        in_specs=[_row_spec(Dm), _vec_spec(Dm), _row_spec(Dm), _row_spec(Dm)],
        out_specs=[_row_spec(Dm), _vec_spec(Dm)],
        out_shape=[jax.ShapeDtypeStruct((T, Dm), F32), jax.ShapeDtypeStruct((1, Dm), F32)],
        compiler_params=_cparams("arbitrary"))(h, g, dhn, dres)


def _ple_fwd(h, z, pp, name):
    T, Dm = h.shape

    def body(h_ref, z_ref, p_ref, o_ref):
        o_ref[...] = h_ref[...] + _sig(z_ref[...]) * p_ref[...]

    return pl.pallas_call(
        body, name=name, grid=(T // ROW_TILE,), in_specs=[_row_spec(Dm)] * 3, out_specs=_row_spec(Dm),
        out_shape=jax.ShapeDtypeStruct((T, Dm), F32), compiler_params=_cparams("parallel"))(h, z, pp)


def _ple_bwd(dh, z, pp, name):
    T, Dm = dh.shape

    def body(dh_ref, z_ref, p_ref, dz_ref, dp_ref):
        s = _sig(z_ref[...])
        d = dh_ref[...]
        dz_ref[...] = d * p_ref[...] * s * (1.0 - s)
        dp_ref[...] = d * s

    return pl.pallas_call(
        body, name=name, grid=(T // ROW_TILE,), in_specs=[_row_spec(Dm)] * 3, out_specs=[_row_spec(Dm)] * 2,
        out_shape=[jax.ShapeDtypeStruct((T, Dm), F32)] * 2, compiler_params=_cparams("parallel"))(dh, z, pp)


def _loss_fwd_bwd(y, target, name):
    T, Dm = y.shape

    def body(y_ref, t_ref, l_ref, d_ref):
        i = pl.program_id(0)
        e = y_ref[...] - t_ref[...]
        d_ref[...] = e * (1.0 / Dm)
        part = jnp.full((8, LANES), 0.5 / Dm, F32) * jnp.sum(e * e)

        @pl.when(i == 0)
        def _():
            l_ref[...] = part

        @pl.when(i > 0)
        def _():
            l_ref[...] += part

    return pl.pallas_call(
        body, name=name, grid=(T // ROW_TILE,), in_specs=[_row_spec(Dm)] * 2,
        out_specs=[pl.BlockSpec((8, LANES), lambda i: (0, 0)), _row_spec(Dm)],
        out_shape=[jax.ShapeDtypeStruct((8, LANES), F32), jax.ShapeDtypeStruct((T, Dm), F32)],
        compiler_params=_cparams("arbitrary"))(y, target)


def _head_mean_matrix():
    r = _iota2((LANES, LANES), 0) >= SW_DIM
    c = _iota2((LANES, LANES), 1) >= SW_DIM
    return jnp.where(r == c, 1.0 / SW_DIM, 0.0).astype(BF16)


def _headnorm_fwd(x, g_lane, name):
    T = x.shape[0]
    C = g_lane.shape[1]

    def body(x_ref, g_ref, y_ref):
        xv = x_ref[...]
        ms = _dot_exact_l(xv * xv, _head_mean_matrix())
        y_ref[...] = xv * lax.rsqrt(ms + EPS) * g_ref[...]

    spec = pl.BlockSpec((ROW_TILE, LANES), lambda j, i: (i, j))
    return pl.pallas_call(
        body, name=name, grid=(C // LANES, T // ROW_TILE),
        in_specs=[spec, pl.BlockSpec((1, LANES), lambda j, i: (0, j))], out_specs=spec,
        out_shape=jax.ShapeDtypeStruct((T, C), F32), compiler_params=_cparams("parallel", "parallel"))(x, g_lane)


def _headnorm_bwd(x, g_lane, dy_full, name):
    T = x.shape[0]
    C = g_lane.shape[1]

    def body(x_ref, g_ref, dy_ref, dx_ref, dg_ref):
        i = pl.program_id(1)
        xv = x_ref[...]
        dy = dy_ref[...]
        bd = _head_mean_matrix()
        r = lax.rsqrt(_dot_exact_l(xv * xv, bd) + EPS)
        gdy = dy * g_ref[...]
        m = _dot_exact_l(xv * gdy, bd)
        dx_ref[...] = r * gdy - xv * (r * r * r * m)
        part = jnp.sum(dy * xv * r, axis=0, keepdims=True)

        @pl.when(i == 0)
        def _():
            dg_ref[...] = part

        @pl.when(i > 0)
        def _():
            dg_ref[...] += part

    spec = pl.BlockSpec((ROW_TILE, LANES), lambda j, i: (i, j))
    vspec = pl.BlockSpec((1, LANES), lambda j, i: (0, j))
    return pl.pallas_call(
        body, name=name, grid=(C // LANES, T // ROW_TILE), in_specs=[spec, vspec, spec],
        out_specs=[spec, vspec],
        out_shape=[jax.ShapeDtypeStruct(dy_full.shape, F32), jax.ShapeDtypeStruct((1, C), F32)],
        input_output_aliases={2: 0}, compiler_params=_cparams("parallel", "arbitrary"))(x, g_lane, dy_full)


CONV_TILE = 128


def _shift_down(x, k):
    rows = _iota2(x.shape, 0)
    return jnp.where(rows >= k, pltpu.roll(x, k, 0), 0.0)


def _shift_up(x, k):
    n = x.shape[0]
    rows = _iota2(x.shape, 0)
    return jnp.where(rows < n - k, pltpu.roll(x, n - k, 0), 0.0)


def _conv3(u, w_ref, b_ref):
    return w_ref[2:3, :] * u + w_ref[1:2, :] * _shift_down(u, 1) + w_ref[0:1, :] * _shift_down(u, 2) + b_ref[...]


def _convglu_fwd(u, cw, cb, S, name):
    T = u.shape[0]
    nf = D_FF // CONV_TILE

    def body(ug_ref, uu_ref, wg_ref, wu_ref, bg_ref, bu_ref, a_ref):
        yg = _conv3(ug_ref[...], wg_ref, bg_ref)
        yu = _conv3(uu_ref[...], wu_ref, bu_ref)
        a_ref[...] = (yg * _sig(yg) * yu).astype(BF16)

    def blk(rows, off):
        return pl.BlockSpec((rows, CONV_TILE), (lambda b, j: (b, j + off)) if rows == S else (lambda b, j: (0, j + off)))

    return pl.pallas_call(
        body, name=name, grid=(T // S, nf),
        in_specs=[blk(S, 0), blk(S, nf), blk(3, 0), blk(3, nf), blk(1, 0), blk(1, nf)],
        out_specs=blk(S, 0), out_shape=jax.ShapeDtypeStruct((T, D_FF), BF16),
        compiler_params=_cparams("parallel", "parallel"))(u, u, cw, cw, cb, cb)


def _convglu_bwd(u, da, cw, cb, S, name):
    T = u.shape[0]
    nf = D_FF // CONV_TILE

    def body(ug_ref, uu_ref, da_ref, wg_ref, wu_ref, bg_ref, bu_ref, du_ref, dw_ref, db_ref):
        b = pl.program_id(1)
        ug, uu = ug_ref[...], uu_ref[...]
        yg = _conv3(ug, wg_ref, bg_ref)
        yu = _conv3(uu, wu_ref, bu_ref)
        s = _sig(yg)
        da_v = da_ref[...]
        for half, (uv, w_ref, dy) in enumerate(((ug, wg_ref, da_v * yu * (s * (1.0 + yg * (1.0 - s)))),
                                                (uu, wu_ref, da_v * yg * s))):
            du_ref[half] = w_ref[2:3, :] * dy + w_ref[1:2, :] * _shift_up(dy, 1) + w_ref[0:1, :] * _shift_up(dy, 2)
            dws = [jnp.sum(dy * _shift_down(uv, 2), axis=0, keepdims=True),
                   jnp.sum(dy * _shift_down(uv, 1), axis=0, keepdims=True),
                   jnp.sum(dy * uv, axis=0, keepdims=True)]
            dbv = jnp.sum(dy, axis=0, keepdims=True)

            @pl.when(b == 0)
            def _():
                for k in range(3):
                    dw_ref[half, k:k + 1, :] = dws[k]
                db_ref[half] = dbv

            @pl.when(b > 0)
            def _():
                for k in range(3):
                    dw_ref[half, k:k + 1, :] += dws[k]
                db_ref[half] += dbv

    def blk(rows, off):
        return pl.BlockSpec((rows, CONV_TILE), (lambda j, b: (b, j + off)) if rows == S else (lambda j, b: (0, j + off)))

    def both(rows):
        return pl.BlockSpec((2, rows, CONV_TILE), (lambda j, b: (0, b, j)) if rows == S else (lambda j, b: (0, 0, j)))

    return pl.pallas_call(
        body, name=name, grid=(nf, T // S),
        in_specs=[blk(S, 0), blk(S, nf), blk(S, 0), blk(3, 0), blk(3, nf), blk(1, 0), blk(1, nf)],
        out_specs=[both(S), both(3), both(1)],
        out_shape=[jax.ShapeDtypeStruct((2, T, D_FF), F32), jax.ShapeDtypeStruct((2, 3, D_FF), F32),
                   jax.ShapeDtypeStruct((2, 1, D_FF), F32)],
        compiler_params=_cparams("parallel", "arbitrary"))(u, u, da, cw, cw, cb, cb)


def _sb_scores(qb, kblk, on_diag_mask):
    z = _dot(qb, kblk, 1, 1) * (SB_DIM ** -0.5)
    l1 = jnp.log(1.0 + jnp.exp(-jnp.abs(z)))
    ls = jnp.minimum(z, 0.0) - l1
    lk = jnp.where(on_diag_mask, ls - z, 0.0)
    return ls, lk


def _sb_fwd(proj, S, name, exchange=None):
    T = proj.shape[0]
    BQ, BK = SB_QBLOCK, ATT_BLOCK
    nq = S // BQ
    nhp = SB_HEADS // 2
    heads = [slice(h * SB_DIM, (h + 1) * SB_DIM) for h in range(2)]

    grid = (T // S, nhp)

    def body(*refs):
        (q_ref, k_ref, v_ref, o_ref, tot_ref), carried = _carried(exchange, refs, 3, 2, 0)
        _start_carried(exchange, carried, grid)
        ahead = _iota2((BQ, BK), 1) - _iota2((BQ, BK), 0)
        upper = (_iota2((BK, BK), 0) > _iota2((BK, BK), 1)).astype(BF16)

        def qloop(iq, carry):
            q0 = pl.multiple_of(iq * BQ, BQ)
            rows = pl.ds(q0, BQ)
            qbs = [q_ref[rows, sl] for sl in heads]
            nkb = (iq + 1) * (BQ // BK)

            def kloop(jj, kc):
                k0 = pl.multiple_of((nkb - 1 - jj) * BK, BK)
                krows = pl.ds(k0, BK)
                mask = ahead < q0 - k0
                out = []
                for sl, qb, (run, acc) in zip(heads, qbs, kc):
                    ls, lk = _sb_scores(qb, k_ref[krows, sl], mask)
                    later = _dot_exact_l(lk, upper)
                    w = jnp.where(mask, jnp.exp(ls + later + run), 0.0)
                    out.append((run + jnp.sum(lk, axis=1, keepdims=True), acc + _dot(w, v_ref[krows, sl], 1, 0)))
                return tuple(out)

            init = (jnp.zeros((BQ, 1), F32), jnp.zeros((BQ, SB_DIM), F32))
            res = lax.fori_loop(0, nkb, kloop, (init, init))
            for h, (sl, (run, acc)) in enumerate(zip(heads, res)):
                o_ref[rows, sl] = acc
                tot_ref[rows, h:h + 1] = run
            return carry

        lax.fori_loop(0, nq, qloop, 0)
        _wait_carried(exchange, carried, grid)

    def spec(off):
        return pl.BlockSpec((S, LANES), lambda b, hp: (b, hp + off))

    (cat, tot), targets = _carrier_call(
        body, name, grid, [spec(0), spec(nhp), spec(2 * nhp)],
        [spec(0), pl.BlockSpec((None, S, 2), lambda b, hp: (hp, b, 0))],
        [jax.ShapeDtypeStruct((T, 2 * SB_HEADS * SB_DIM), F32), jax.ShapeDtypeStruct((nhp, T, 2), F32)],
        [], {}, (proj, proj, proj), exchange)
    return cat, tot, targets


def _sb_bwd(proj, tot, dcat, S, name, exchange=None):
    T, width = proj.shape
    BQ, BK = SB_QBLOCK, ATT_BLOCK
    nq = S // BQ
    nhp = SB_HEADS // 2
    scale = SB_DIM ** -0.5
    heads = [slice(h * SB_DIM, (h + 1) * SB_DIM) for h in range(2)]

    grid = (T // S, nhp)

    def body(*refs):
        (q_ref, k_ref, v_ref, tot_ref, do_ref, dp_hbm, dq_s, dk_s, dv_s, sems), carried = _carried(
            exchange, refs, 5, 1, 4)
        _start_carried(exchange, carried, grid)
        b, hp = pl.program_id(0), pl.program_id(1)
        ahead = _iota2((BQ, BK), 1) - _iota2((BQ, BK), 0)
        r, c = _iota2((BK, BK), 0), _iota2((BK, BK), 1)
        upto = (r <= c).astype(BF16)
        earlier = (r < c).astype(BF16)
        dk_s[...] = jnp.zeros_like(dk_s)
        dv_s[...] = jnp.zeros_like(dv_s)

        def qloop(iq, carry):
            q0 = pl.multiple_of(iq * BQ, BQ)
            rows = pl.ds(q0, BQ)
            qbs = [q_ref[rows, sl] for sl in heads]
            dobs = [do_ref[rows, sl] for sl in heads]
            totals = [tot_ref[rows, h:h + 1] for h in range(2)]

            def kloop(kb, kc):
                k0 = pl.multiple_of(kb * BK, BK)
                krows = pl.ds(k0, BK)
                mask = ahead < q0 - k0
                out = []
                for sl, qb, dob, total, (run, grun, dq) in zip(heads, qbs, dobs, totals, kc):
                    kblk = k_ref[krows, sl]
                    ls, lk = _sb_scores(qb, kblk, mask)
                    later = total - (_dot_exact_l(lk, upto) + run)
                    w = jnp.where(mask, jnp.exp(ls + later), 0.0)
                    g = w * _dot(dob, v_ref[krows, sl], 1, 1)
                    before = _dot_exact_l(g, earlier) + grun
                    beta = jnp.exp(ls)
                    dz = jnp.where(mask, g * (1.0 - beta) - beta * before, 0.0) * scale
                    dv_s[krows, sl] += _dot(w, dob, 0, 0)
                    dk_s[krows, sl] += _dot(dz, qb, 0, 0)
                    out.append((run + jnp.sum(lk, axis=1, keepdims=True),
                                grun + jnp.sum(g, axis=1, keepdims=True), dq + _dot(dz, kblk, 1, 0)))
                return tuple(out)

            zero = jnp.zeros((BQ, 1), F32)
            init = (zero, zero, jnp.zeros((BQ, SB_DIM), F32))
            res = lax.fori_loop(0, (iq + 1) * (BQ // BK), kloop, (init, init))
            for sl, (_, _, dq) in zip(heads, res):
                dq_s[rows, sl] = dq
            return carry

        lax.fori_loop(0, nq, qloop, 0)
        r0 = pl.multiple_of(b * S, S)
        copies = []
        for n, buf in enumerate((dq_s, dk_s, dv_s)):
            c0 = pl.multiple_of((hp + n * nhp) * LANES, LANES)
            copies.append(pltpu.make_async_copy(buf, dp_hbm.at[pl.ds(r0, S), pl.ds(c0, LANES)], sems.at[n]))
        for cp in copies:
            cp.start()
        for cp in copies:
            cp.wait()
        _wait_carried(exchange, carried, grid)

    def spec(off):
        return pl.BlockSpec((S, LANES), lambda b, hp: (b, hp + off))

    (dproj,), targets = _carrier_call(
        body, name, grid,
        [spec(0), spec(nhp), spec(2 * nhp), pl.BlockSpec((None, S, 2), lambda b, hp: (hp, b, 0)), spec(0)],
        [ANY], [jax.ShapeDtypeStruct((T, width), F32)],
        [pltpu.VMEM((S, LANES), F32)] * 3 + [pltpu.SemaphoreType.DMA((3,))], {}, (proj, proj, proj, tot, dcat),
        exchange)
    return dproj, targets


HG_COL0 = 3 * SB_HEADS * SB_DIM // LANES


def _hg_gates(q, fp, lbv):
    sg = _sig(fp)
    f = lbv + (1.0 - lbv) * sg
    kk = (1.0 - lbv) * _sig(-fp)
    sq = _sig(q)
    return sg, f, kk, sq


def _hg_chunk(qs, kk, lf, incl):
    C = HG_CHUNK
    b = _dot_exact_r(incl, lf)
    bl = b[C - 1:C, :]
    bm = b[C // 2 - 1:C // 2, :]
    e_t = jnp.exp(b - bm)
    e_s = jnp.exp(bm - b)
    e_i = jnp.exp(b)
    e_e = jnp.exp(bl - b)
    return bl, e_t, e_s, e_i, e_e


def _hgrn_fwd(proj, cat, lb, hgn, S, name, exchange=None):
    T = proj.shape[0]
    B = T // S
    C = HG_CHUNK
    NC = S // C

    grid = (B, HG_HEADS // HG_STEP)

    def body(*refs):
        (q_ref, f_ref, i_ref, g_ref, lb_ref, hgn_ref, _, ob_ref, oraw_ref, st_ref, state), carried = _carried(
            exchange, refs, 7, 3, 1)
        _start_carried(exchange, carried, grid)
        state[...] = jnp.zeros_like(state)
        row, col = _iota2((C, C), 0), _iota2((C, C), 1)
        causal = row >= col
        incl = causal.astype(BF16)

        def chunk(c, carry):
            rows = pl.ds(pl.multiple_of(c * C, C), C)
            for hh in range(HG_STEP):
                hs = slice(hh * LANES, (hh + 1) * LANES)
                q, iv, gv = q_ref[rows, hs], i_ref[rows, hs], g_ref[rows, hs]
                _, f, kk, sq = _hg_gates(q, f_ref[rows, hs], lb_ref[:, hs])
                qs = q * sq
                bl, e_t, e_s, e_i, e_e = _hg_chunk(qs, kk, jnp.log(f), incl)
                p = jnp.where(causal, _dot(qs * e_t, kk * e_s, 1, 1), 0.0)
                st = state[hh]
                st_ref[0, hh, c] = st
                o = _dot(qs * e_i, st, 1, 1) + _dot(p, iv, 1, 0)
                state[hh] = st * jnp.exp(bl) + _dot(iv, kk * e_e, 0, 0)
                oraw_ref[rows, hs] = o
                r = lax.rsqrt(jnp.mean(o * o, axis=1, keepdims=True) + EPS)
                ob_ref[rows, hs] = o * r * hgn_ref[...] * (gv * _sig(gv))
            return carry

        lax.fori_loop(0, NC, chunk, 0)
        _wait_carried(exchange, carried, grid)

    width = HG_STEP * LANES
    col0 = HG_COL0 * LANES // width
    nstep = HG_HEADS // HG_STEP

    def spec(off):
        return pl.BlockSpec((S, width), lambda b, h: (b, h + off))

    outs, targets = _carrier_call(
        body, name, grid,
        [spec(col0), spec(col0 + nstep), spec(col0 + 2 * nstep), spec(col0 + 3 * nstep),
         pl.BlockSpec((1, width), lambda b, h: (0, h)), pl.BlockSpec((1, LANES), lambda b, h: (0, 0)), ANY],
        [spec(nstep), spec(0), pl.BlockSpec((1, HG_STEP, NC, LANES, LANES), lambda b, h: (b, h, 0, 0, 0))],
        [jax.ShapeDtypeStruct(cat.shape, F32), jax.ShapeDtypeStruct((T, HG_HEADS * LANES), F32),
         jax.ShapeDtypeStruct((B, HG_HEADS, NC, LANES, LANES), F32)],
        [pltpu.VMEM((HG_STEP, LANES, LANES), F32)], {6: 0}, (proj, proj, proj, proj, lb, hgn, cat), exchange)
    return (*outs, targets)


def _hgrn_bwd(proj, oraw, dcat, states, lb, hgn, dproj, S, name):
    T = proj.shape[0]
    B = T // S
    C = HG_CHUNK
    NC = S // C

    def body(q_ref, f_ref, i_ref, g_ref, oraw_ref, dy_ref, st_ref, lb_ref, hgn_ref, dp_in,
             dp_hbm, dlb_ref, dhgn_ref, dstate, dq_s, df_s, di_s, dg_s, sems):
        del dp_in
        h, b = pl.program_id(0), pl.program_id(1)
        row, col = _iota2((C, C), 0), _iota2((C, C), 1)
        causal = row >= col
        incl = causal.astype(BF16)
        last_row = _iota2((C, LANES), 0) == C - 1
        hg = hgn_ref[...]
        dstate[...] = jnp.zeros_like(dstate)

        @pl.when(b == 0)
        def _():
            dlb_ref[...] = jnp.zeros_like(dlb_ref)

        @pl.when(jnp.logical_and(b == 0, h == 0))
        def _():
            dhgn_ref[...] = jnp.zeros_like(dhgn_ref)

        def chunk(cc, carry):
            c = NC - 1 - cc
            rows = pl.ds(pl.multiple_of(c * C, C), C)
            for hh in range(HG_STEP):
                hs = slice(hh * LANES, (hh + 1) * LANES)
                lbv = lb_ref[:, hs]
                q, fp, iv, gv = q_ref[rows, hs], f_ref[rows, hs], i_ref[rows, hs], g_ref[rows, hs]
                o = oraw_ref[rows, hs]
                dy = dy_ref[rows, hs]
                r = lax.rsqrt(jnp.mean(o * o, axis=1, keepdims=True) + EPS)
                on = o * r
                sgv = _sig(gv)
                silu_g = gv * sgv
                dg_s[rows, hs] = dy * on * hg * (sgv * (1.0 + gv * (1.0 - sgv)))
                dhgn_ref[...] += jnp.sum(dy * on * silu_g, axis=0, keepdims=True)
                dn = dy * hg * silu_g
                do = r * dn - o * (r * r * r * jnp.mean(o * dn, axis=1, keepdims=True))
                sg, f, kk, sq = _hg_gates(q, fp, lbv)
                qs = q * sq
                bl, e_t, e_s, e_i, e_e = _hg_chunk(qs, kk, jnp.log(f), incl)
                qd, kd, qi, ke = qs * e_t, kk * e_s, qs * e_i, kk * e_e
                p = jnp.where(causal, _dot(qd, kd, 1, 1), 0.0)
                st = st_ref[0, hh, c]
                dst = dstate[hh]
                ebl = jnp.exp(bl)
                dqi = _dot(do, st, 1, 0)
                dp = jnp.where(causal, _dot(do, iv, 1, 1), 0.0)
                di_s[rows, hs] = _dot(p, do, 0, 0) + _dot(ke, dst, 1, 1)
                dqd = _dot(dp, kd, 1, 0)
                dkd = _dot(dp, qd, 0, 0)
                dke = _dot(iv, dst, 1, 0)
                dbl = jnp.sum(st * dst, axis=0, keepdims=True) * ebl + jnp.sum(dke * ke, axis=0, keepdims=True)
                db = dqd * qd - dkd * kd + dqi * qi - dke * ke + jnp.where(last_row, dbl, 0.0)
                dqs = dqd * e_t + dqi * e_i
                dkk = dkd * e_s + dke * e_e
                dlf = _dot_exact_r(incl, db, cm=0)
                dstate[hh] = _dot(do, qi, 0, 0) + dst * ebl
                oms = 1.0 - sg
                dfd = dlf / f
                df_s[rows, hs] = (dfd - dkk) * (1.0 - lbv) * sg * oms
                dq_s[rows, hs] = dqs * (sq * (1.0 + q * (1.0 - sq)))
                dlb_ref[:, hs] += jnp.sum((dfd - dkk) * oms, axis=0, keepdims=True)
            return carry

        lax.fori_loop(0, NC, chunk, 0)
        r0 = pl.multiple_of(b * S, S)
        copies = []
        for n, buf in enumerate((dq_s, df_s, di_s, dg_s)):
            c0 = pl.multiple_of((col0 + n * nstep + h) * width, width)
            copies.append(pltpu.make_async_copy(buf, dp_hbm.at[pl.ds(r0, S), pl.ds(c0, width)], sems.at[n]))
        for cp in copies:
            cp.start()
        for cp in copies:
            cp.wait()

    width = HG_STEP * LANES
    col0 = HG_COL0 * LANES // width
    nstep = HG_HEADS // HG_STEP

    def spec(off):
        return pl.BlockSpec((S, width), lambda h, b: (b, h + off))

    return pl.pallas_call(
        body, name=name, grid=(nstep, B),
        in_specs=[spec(col0), spec(col0 + nstep), spec(col0 + 2 * nstep), spec(col0 + 3 * nstep), spec(0), spec(nstep),
                  pl.BlockSpec((1, HG_STEP, NC, LANES, LANES), lambda h, b: (b, h, 0, 0, 0)),
                  pl.BlockSpec((1, width), lambda h, b: (0, h)), pl.BlockSpec((1, LANES), lambda h, b: (0, 0)), ANY],
        out_specs=[ANY, pl.BlockSpec((1, width), lambda h, b: (0, h)), pl.BlockSpec((1, LANES), lambda h, b: (0, 0))],
        out_shape=[jax.ShapeDtypeStruct(dproj.shape, F32), jax.ShapeDtypeStruct((1, HG_HEADS * LANES), F32),
                   jax.ShapeDtypeStruct((1, LANES), F32)],
        scratch_shapes=[pltpu.VMEM((HG_STEP, LANES, LANES), F32)] + [pltpu.VMEM((S, width), F32)] * 4
        + [pltpu.SemaphoreType.DMA((4,))],
        input_output_aliases={9: 0},
        compiler_params=_cparams("arbitrary", "arbitrary"))(proj, proj, proj, proj, oraw, dcat, states, lb, hgn, dproj)


def _lower_bound_fwd(logits, name):
    assert logits.shape[0] == 2

    def body(l_ref, o_ref):
        l0, l1 = l_ref[0:1, :], l_ref[1:2, :]
        m = jnp.maximum(l0, l1)
        e0, e1 = jnp.exp(l0 - m), jnp.exp(l1 - m)
        o_ref[0:1, :] = jnp.zeros_like(l0)
        o_ref[1:2, :] = e1 / (e0 + e1)

    return pl.pallas_call(body, name=name, out_shape=jax.ShapeDtypeStruct(logits.shape, F32))(logits)


def _lower_bound_bwd(logits, dlb, name):
    def body(l_ref, d_ref, o_ref):
        l0, l1 = l_ref[0:1, :], l_ref[1:2, :]
        m = jnp.maximum(l0, l1)
        e0, e1 = jnp.exp(l0 - m), jnp.exp(l1 - m)
        s1 = e1 / (e0 + e1)
        t = s1 * (1.0 - s1) * d_ref[1:2, :]
        o_ref[0:1, :] = -t
        o_ref[1:2, :] = t

    return pl.pallas_call(body, name=name, out_shape=jax.ShapeDtypeStruct(logits.shape, F32))(logits, dlb)


def _bucket_thresholds():
    dist = np.arange(WINDOW)
    max_exact = N_BUCKETS // 2
    large = max_exact + (np.log(np.maximum(dist, max_exact) / max_exact) / math.log(MAX_DISTANCE / max_exact)
                         * (N_BUCKETS - max_exact)).astype(np.int32)
    bucket = np.where(dist < max_exact, dist, np.minimum(large, N_BUCKETS - 1))
    assert np.all(np.diff(bucket) >= 0)
    return [int(np.argmax(bucket >= k)) if np.any(bucket >= k) else 10 ** 6 for k in range(1, N_BUCKETS)]


def _band_bucket():
    dist = _iota2((WINDOW, 2 * WINDOW), 0) + WINDOW - _iota2((WINDOW, 2 * WINDOW), 1)
    bucket = jnp.zeros((WINDOW, 2 * WINDOW), jnp.int32)
    for thr in _bucket_thresholds():
        bucket = bucket + (dist >= thr).astype(jnp.int32)
    band = jnp.logical_and(dist >= 0, dist < WINDOW)
    return bucket, band


def _bias_build(rel_bias, name):
    def body(rb_ref, o_ref):
        h = pl.program_id(0)
        bucket, _ = _band_bucket()
        bias = jnp.zeros((WINDOW, 2 * WINDOW), F32)
        for k in range(N_BUCKETS):
            bias = jnp.where(bucket == k, rb_ref[k, h], bias)
        o_ref[0] = bias

    return pl.pallas_call(
        body, name=name, grid=(SW_HEADS,), in_specs=[pl.BlockSpec(memory_space=pltpu.SMEM)],
        out_specs=pl.BlockSpec((1, WINDOW, 2 * WINDOW), lambda h: (h, 0, 0)),
        out_shape=jax.ShapeDtypeStruct((SW_HEADS, WINDOW, 2 * WINDOW), F32),
        compiler_params=_cparams("parallel"))(rel_bias)


def _bias_reduce(dbias, name):
    def body(d_ref, o_ref):
        bucket, band = _band_bucket()
        d = jnp.where(band, d_ref[0], 0.0)
        lane = _iota2((1, LANES), 1)
        out = jnp.zeros((1, LANES), F32)
        for k in range(N_BUCKETS):
            out = jnp.where(lane == k, jnp.sum(jnp.where(bucket == k, d, 0.0)), out)
        o_ref[0] = out

    return pl.pallas_call(
        body, name=name, grid=(SW_HEADS,), in_specs=[pl.BlockSpec((1, WINDOW, 2 * WINDOW), lambda h: (h, 0, 0))],
        out_specs=pl.BlockSpec((1, 1, LANES), lambda h: (h, 0, 0)),
        out_shape=jax.ShapeDtypeStruct((SW_HEADS, 1, LANES), F32), compiler_params=_cparams("parallel"))(dbias)


SW_Q_COLS = SW_HEADS * SW_DIM
SW_K_BLOCK0 = SW_Q_COLS // LANES
SW_V_BLOCK0 = SW_K_BLOCK0 + SW_KV * SW_DIM // LANES
SW_STEP_HEADS = 8


def _swa_probs(qb, kprev, kcur, bias_ref, hl, sink, mprev, mcur):
    scale = SW_DIM ** -0.5
    lp = jnp.where(mprev, _dot(qb, kprev, 1, 1) * scale + bias_ref[hl, :, 0:WINDOW], NEG)
    lc = jnp.where(mcur, _dot(qb, kcur, 1, 1) * scale + bias_ref[hl, :, WINDOW:2 * WINDOW], NEG)
    m = jnp.maximum(jnp.maximum(jnp.max(lp, axis=1, keepdims=True), jnp.max(lc, axis=1, keepdims=True)), sink)
    ep, ec = jnp.exp(lp - m), jnp.exp(lc - m)
    es = jnp.exp(sink - m)
    den = jnp.sum(ep, axis=1, keepdims=True) + jnp.sum(ec, axis=1, keepdims=True) + es
    return ep, ec, es, den


def _swa_fwd(qkn, proj, bias, sinks, S, name, exchange=None):
    T = qkn.shape[0]
    W = WINDOW
    nb = S // W

    grid = (T // S, 2)

    def body(*refs):
        (q_ref, k_ref, v_ref, bias_ref, sink_ref, o_ref), carried = _carried(exchange, refs, 5, 1, 0)
        _start_carried(exchange, carried, grid)
        kp = pl.program_id(1)
        row, col = _iota2((W, W), 0), _iota2((W, W), 1)
        mcur = col <= row
        above = col > row

        def blk(n, carry):
            rows = pl.ds(pl.multiple_of(n * W, W), W)
            prow = pl.ds(pl.multiple_of(jnp.maximum(n - 1, 0) * W, W), W)
            mprev = jnp.logical_and(above, n > 0)
            for kvh in range(2):
                ksl = slice(kvh * SW_DIM, (kvh + 1) * SW_DIM)
                kcur, kprev = k_ref[rows, ksl], k_ref[prow, ksl]
                vcur, vprev = v_ref[rows, ksl], v_ref[prow, ksl]
                for g in range(4):
                    hl = kvh * 4 + g
                    qsl = slice(hl * SW_DIM, (hl + 1) * SW_DIM)
                    sink = sink_ref[kp * SW_STEP_HEADS + hl]
                    ep, ec, _, den = _swa_probs(q_ref[rows, qsl], kprev, kcur, bias_ref, hl, sink, mprev, mcur)
                    o_ref[rows, qsl] = (_dot(ep, vprev, 1, 0) + _dot(ec, vcur, 1, 0)) / den
            return carry

        lax.fori_loop(0, nb, blk, 0)
        _wait_carried(exchange, carried, grid)

    (o,), targets = _carrier_call(
        body, name, grid,
        [pl.BlockSpec((S, 4 * LANES), lambda b, kp: (b, kp)),
         pl.BlockSpec((S, LANES), lambda b, kp: (b, SW_K_BLOCK0 + kp)),
         pl.BlockSpec((S, LANES), lambda b, kp: (b, SW_V_BLOCK0 + kp)),
         pl.BlockSpec((SW_STEP_HEADS, W, 2 * W), lambda b, kp: (kp, 0, 0)),
         pl.BlockSpec(memory_space=pltpu.SMEM)],
        [pl.BlockSpec((S, 4 * LANES), lambda b, kp: (b, kp))], [jax.ShapeDtypeStruct((T, SW_Q_COLS), F32)],
        [], {}, (qkn, qkn, proj, bias, sinks), exchange)
    return o, targets


def _swa_bwd(qkn, proj, bias, sinks, do, S, name, exchange=None):
    T, width = proj.shape
    W = WINDOW
    nb = S // W
    scale = SW_DIM ** -0.5

    grid = (2, T // S)

    def body(*refs):
        (q_ref, k_ref, v_ref, bias_ref, sink_ref, do_ref, dp_hbm, dbias_ref, dsink_ref,
         dq_s, dk_s, dv_s, sems), carried = _carried(exchange, refs, 6, 3, 4)
        _start_carried(exchange, carried, grid)
        kp, b = pl.program_id(0), pl.program_id(1)
        row, col = _iota2((W, W), 0), _iota2((W, W), 1)
        mcur = col <= row
        above = col > row
        dk_s[...] = jnp.zeros_like(dk_s)
        dv_s[...] = jnp.zeros_like(dv_s)

        @pl.when(b == 0)
        def _():
            dbias_ref[...] = jnp.zeros_like(dbias_ref)
            dsink_ref[...] = jnp.zeros_like(dsink_ref)

        def blk(n, carry):
            rows = pl.ds(pl.multiple_of(n * W, W), W)
            prow = pl.ds(pl.multiple_of(jnp.maximum(n - 1, 0) * W, W), W)
            mprev = jnp.logical_and(above, n > 0)
            for kvh in range(2):
                ksl = slice(kvh * SW_DIM, (kvh + 1) * SW_DIM)
                kcur, kprev = k_ref[rows, ksl], k_ref[prow, ksl]
                vcur, vprev = v_ref[rows, ksl], v_ref[prow, ksl]
                for g in range(4):
                    hl = kvh * 4 + g
                    qsl = slice(hl * SW_DIM, (hl + 1) * SW_DIM)
                    sink = sink_ref[kp * SW_STEP_HEADS + hl]
                    qb = q_ref[rows, qsl]
                    ep, ec, es, den = _swa_probs(qb, kprev, kcur, bias_ref, hl, sink, mprev, mcur)
                    inv = 1.0 / den
                    pp, pc = ep * inv, ec * inv
                    dob = do_ref[rows, qsl]
                    dpp, dpc = _dot(dob, vprev, 1, 1), _dot(dob, vcur, 1, 1)
                    total = jnp.sum(pp * dpp, axis=1, keepdims=True) + jnp.sum(pc * dpc, axis=1, keepdims=True)
                    dlp = pp * (dpp - total)
                    dlc = pc * (dpc - total)
                    dsink_ref[hl:hl + 1, :] += jnp.zeros((1, LANES), F32) - jnp.sum(es * inv * total)
                    dbias_ref[hl, :, 0:W] += dlp
                    dbias_ref[hl, :, W:2 * W] += dlc
                    dq_s[rows, qsl] = (_dot(dlp, kprev, 1, 0) + _dot(dlc, kcur, 1, 0)) * scale
                    dk_s[prow, ksl] += _dot(dlp, qb, 0, 0) * scale
                    dk_s[rows, ksl] += _dot(dlc, qb, 0, 0) * scale
                    dv_s[prow, ksl] += _dot(pp, dob, 0, 0)
                    dv_s[rows, ksl] += _dot(pc, dob, 0, 0)
            return carry

        lax.fori_loop(0, nb, blk, 0)
        r0 = pl.multiple_of(b * S, S)
        cq = pl.multiple_of(kp * 4 * LANES, LANES)
        ck = pl.multiple_of((SW_K_BLOCK0 + kp) * LANES, LANES)
        cv = pl.multiple_of((SW_V_BLOCK0 + kp) * LANES, LANES)
        copies = [pltpu.make_async_copy(dq_s, dp_hbm.at[pl.ds(r0, S), pl.ds(cq, 4 * LANES)], sems.at[0]),
                  pltpu.make_async_copy(dk_s, dp_hbm.at[pl.ds(r0, S), pl.ds(ck, LANES)], sems.at[1]),
                  pltpu.make_async_copy(dv_s, dp_hbm.at[pl.ds(r0, S), pl.ds(cv, LANES)], sems.at[2])]
        for cp in copies:
            cp.start()
        for cp in copies:
            cp.wait()
        _wait_carried(exchange, carried, grid)

    qspec = pl.BlockSpec((S, 4 * LANES), lambda kp, b: (b, kp))
    outs, targets = _carrier_call(
        body, name, grid,
        [qspec, pl.BlockSpec((S, LANES), lambda kp, b: (b, SW_K_BLOCK0 + kp)),
         pl.BlockSpec((S, LANES), lambda kp, b: (b, SW_V_BLOCK0 + kp)),
         pl.BlockSpec((SW_STEP_HEADS, W, 2 * W), lambda kp, b: (kp, 0, 0)),
         pl.BlockSpec(memory_space=pltpu.SMEM), qspec],
        [ANY, pl.BlockSpec((SW_STEP_HEADS, W, 2 * W), lambda kp, b: (kp, 0, 0)),
         pl.BlockSpec((SW_STEP_HEADS, LANES), lambda kp, b: (kp, 0))],
        [jax.ShapeDtypeStruct((T, width), F32), jax.ShapeDtypeStruct((SW_HEADS, W, 2 * W), F32),
         jax.ShapeDtypeStruct((SW_HEADS, LANES), F32)],
        [pltpu.VMEM((S, 4 * LANES), F32), pltpu.VMEM((S, LANES), F32), pltpu.VMEM((S, LANES), F32),
         pltpu.SemaphoreType.DMA((3,))], {}, (qkn, qkn, proj, bias, sinks, do), exchange)
    return (*outs, targets)


CHIP_FLIPS = ((1, 0), (0, 1), (1, 1))


def _flip(v, f):
    return 1 - v if f else v


class _Exchange:
    def __init__(self, sources, targets, copies):
        self.sources, self.targets = list(sources), list(targets)
        self._copies = copies
        n = len(self.sources)
        self.scratch = [pltpu.SemaphoreType.DMA((n, 3)), pltpu.SemaphoreType.DMA((n, 3)),
                        pltpu.SemaphoreType.DMA((n,))]

    def _descriptors(self, srcs, dsts, sems, chip, peers):
        send, recv, loc = sems
        out = []
        for t, (local, remote) in enumerate(self._copies(srcs, dsts, chip)):
            out.append(pltpu.make_async_copy(local[0], local[1], loc.at[t]))
            for r, (src, dst) in enumerate(remote):
                out.append(pltpu.make_async_remote_copy(src, dst, send.at[t, r], recv.at[t, r],
                                                        device_id=peers[r], device_id_type=MESH))
        return out

    def start(self, srcs, dsts, sems):
        x, y, c = lax.axis_index("x"), lax.axis_index("y"), lax.axis_index("c")
        peers = [(_flip(x, fx), _flip(y, fy), c) for fx, fy in CHIP_FLIPS]
        for chip in range(4):
            @pl.when(2 * x + y == chip)
            def _():
                for cp in self._descriptors(srcs, dsts, sems, chip, peers):
                    cp.start()

    def wait(self, srcs, dsts, sems):
        me = (lax.axis_index("x"), lax.axis_index("y"), lax.axis_index("c"))
        for cp in self._descriptors(srcs, dsts, sems, 0, [me] * 3):
            cp.wait()

    def operands(self):
        return self.sources + self.targets

    def specs(self):
        ns, nt = len(self.sources), len(self.targets)
        return [ANY] * (ns + nt), [ANY] * nt, [jax.ShapeDtypeStruct(t.shape, t.dtype) for t in self.targets]

    def aliases(self, n_in, n_out):
        ns = len(self.sources)
        return {n_in + ns + i: n_out + i for i in range(len(self.targets))}

    def split(self, refs, n_in, n_out, n_scr):
        ns, nt = len(self.sources), len(self.targets)
        o0 = n_in + ns + nt
        s0 = o0 + n_out + nt
        own = list(refs[:n_in]) + list(refs[o0:o0 + n_out]) + list(refs[s0:s0 + n_scr])
        return own, (refs[n_in:n_in + ns], refs[o0 + n_out:o0 + n_out + nt], refs[s0 + n_scr:])


def _carried(exchange, refs, n_in, n_out, n_scr):
    if exchange is None:
        return list(refs), None
    return exchange.split(refs, n_in, n_out, n_scr)


def _grid_edge(grid, last):
    conds = [pl.program_id(d) == (n - 1 if last else 0) for d, n in enumerate(grid)]
    out = conds[0]
    for cnd in conds[1:]:
        out = jnp.logical_and(out, cnd)
    return out


def _start_carried(exchange, parts, grid):
    if parts is not None:
        @pl.when(_grid_edge(grid, False))
        def _():
            exchange.start(*parts)


def _wait_carried(exchange, parts, grid):
    if parts is not None:
        @pl.when(_grid_edge(grid, True))
        def _():
            exchange.wait(*parts)


def _carrier_call(body, name, grid, in_specs, out_specs, out_shape, scratch, aliases, operands, exchange):
    n_out = len(out_shape)
    aliases = dict(aliases)
    if exchange is not None:
        ex_in, ex_out, ex_shape = exchange.specs()
        aliases.update(exchange.aliases(len(in_specs), n_out))
        in_specs, out_specs, out_shape = in_specs + ex_in, out_specs + ex_out, out_shape + ex_shape
        scratch = scratch + exchange.scratch
        operands = list(operands) + exchange.operands()
    outs = pl.pallas_call(body, name=name, grid=grid, in_specs=in_specs, out_specs=out_specs, out_shape=out_shape,
                          scratch_shapes=scratch, input_output_aliases=aliases,
                          compiler_params=_cparams(*["arbitrary"] * len(grid)))(*operands)
    return outs[:n_out], outs[n_out:]


def _exchange_call(exchange, name):
    in_specs, out_specs, out_shape = exchange.specs()

    def body(*refs):
        _, parts = exchange.split(refs, 0, 0, 0)
        exchange.start(*parts)
        exchange.wait(*parts)

    return pl.pallas_call(body, name=name, in_specs=in_specs, out_specs=out_specs, out_shape=out_shape,
                          scratch_shapes=exchange.scratch, input_output_aliases=exchange.aliases(0, 0))(
        *exchange.operands())


def _gather_exchange(shards, fulls, axes, layers):
    sizes = [s.shape[a] for s, a in zip(shards, axes)]

    def copies(srcs, dsts, chip):
        out = []
        for src, full, axis, size, (l0, l1) in zip(srcs, dsts, axes, sizes, layers):
            part = src.at[l0:l1]
            cut = pl.ds(chip * size, size)
            dst = full.at[l0:l1, cut, :] if axis == 1 else full.at[l0:l1, :, cut]
            out.append(((part, dst), [(part, dst)] * 3))
        return out

    return _Exchange(shards, fulls, copies)


def _scatter_exchange(grads, stacks, axes, layers):
    sizes = [g.shape[a - 1] // 4 for g, a in zip(grads, axes)]

    def copies(srcs, dsts, chip):
        out = []
        for g, stack, axis, size, layer in zip(srcs, dsts, axes, sizes, layers):
            def cut(j, g=g, axis=axis, size=size):
                return g.at[pl.ds(j * size, size), :] if axis == 1 else g.at[:, pl.ds(j * size, size)]

            remote = [(cut(chip ^ (2 * fx + fy)), stack.at[r, layer]) for r, (fx, fy) in enumerate(CHIP_FLIPS)]
            out.append(((cut(chip), stack.at[3, layer]), remote))
        return out

    return _Exchange(grads, stacks, copies)


def _swap_with_sibling(parts, name):
    n = len(parts)

    def body(*refs):
        ins, outs = refs[:n], refs[n:2 * n]
        send, recv = refs[2 * n:]
        peer = (lax.axis_index("x"), lax.axis_index("y"), 1 - lax.axis_index("c"))
        copies = [pltpu.make_async_remote_copy(ins[t], outs[t], send.at[t], recv.at[t], device_id=peer,
                                               device_id_type=MESH) for t in range(n)]
        for cp in copies:
            cp.start()
        for cp in copies:
            cp.wait()

    return pl.pallas_call(
        body, name=name, in_specs=[ANY] * n, out_specs=[ANY] * n,
        out_shape=[jax.ShapeDtypeStruct(p.shape, p.dtype) for p in parts],
        scratch_shapes=[pltpu.SemaphoreType.DMA((n,)), pltpu.SemaphoreType.DMA((n,))])(*parts)


def _allreduce_small(v, name):
    R = v.shape[0]
    ND = 8

    def body(v_ref, o_ref, buf, send, recv):
        x, y, c = lax.axis_index("x"), lax.axis_index("y"), lax.axis_index("c")
        me = 4 * x + 2 * y + c
        copies = []
        for d in range(1, ND):
            peer = (_flip(x, d >> 2 & 1), _flip(y, d >> 1 & 1), _flip(c, d & 1))
            copies.append(pltpu.make_async_remote_copy(v_ref, buf.at[me], send.at[d], recv.at[me], device_id=peer,
                                                       device_id_type=MESH))
        for cp in copies:
            cp.start()
        buf[pl.ds(me, 1)] = v_ref[...][None]
        for k in range(ND):
            @pl.when(me != k)
            def _():
                pltpu.make_async_remote_copy(v_ref, buf.at[k], send.at[0], recv.at[k], device_id=(x, y, c),
                                             device_id_type=MESH).wait_recv()
        for cp in copies:
            cp.wait_send()
        total = buf[0]
        for k in range(1, ND):
            total = total + buf[k]
        o_ref[...] = total

    vm = pl.BlockSpec(memory_space=pltpu.VMEM)
    return pl.pallas_call(
        body, name=name, in_specs=[vm], out_specs=vm, out_shape=jax.ShapeDtypeStruct((R, LANES), F32),
        scratch_shapes=[pltpu.VMEM((ND, R, LANES), F32), pltpu.SemaphoreType.DMA((ND,)),
                        pltpu.SemaphoreType.DMA((ND,))],
        compiler_params=pltpu.CompilerParams(vmem_limit_bytes=VMEM_LIMIT))(v)


def _tile2(R, Cn):
    tc = _pick(Cn, 2048)
    tr = R
    for cand in (256, 128, 64, 32, 16, 8):
        if R % cand == 0:
            tr = cand
            break
    return tr, tc


def _sum4(stack, name):
    _, R, Cn = stack.shape
    tr, tc = _tile2(R, Cn)

    def body(s_ref, o_ref):
        o_ref[...] = ((s_ref[0].astype(F32) + s_ref[1].astype(F32)) + s_ref[2].astype(F32)) + s_ref[3].astype(F32)

    return pl.pallas_call(
        body, name=name, grid=(R // tr, Cn // tc), in_specs=[pl.BlockSpec((4, tr, tc), lambda i, j: (0, i, j))],
        out_specs=pl.BlockSpec((tr, tc), lambda i, j: (i, j)), out_shape=jax.ShapeDtypeStruct((R, Cn), F32),
        compiler_params=_cparams("parallel", "parallel"))(stack)


def _adamw(w, m, v, g_parts, name):
    R, Cn = w.shape
    tr, tc = _tile2(R, Cn)
    npart = len(g_parts)
    c1 = 1.0 / (1.0 - ADAM_B1 ** ADAM_STEP)
    c2 = 1.0 / (1.0 - ADAM_B2 ** ADAM_STEP)

    def body(*refs):
        w_ref, m_ref, v_ref = refs[:3]
        g_refs = refs[3:3 + npart]
        g_out, d_out, m_out, v_out = refs[3 + npart:]
        g = g_refs[0][...]
        for r in g_refs[1:]:
            g = g + r[...]
        mn = ADAM_B1 * m_ref[...] + (1.0 - ADAM_B1) * g
        vn = ADAM_B2 * v_ref[...] + (1.0 - ADAM_B2) * (g * g)
        g_out[...] = g
        m_out[...] = mn
        v_out[...] = vn
        d_out[...] = -ADAM_LR * ((mn * c1) / (jnp.sqrt(vn * c2) + ADAM_EPS) + ADAM_WD * w_ref[...])

    spec = pl.BlockSpec((tr, tc), lambda i, j: (i, j))
    return pl.pallas_call(
        body, name=name, grid=(R // tr, Cn // tc), in_specs=[spec] * (3 + npart), out_specs=[spec] * 4,
        out_shape=[jax.ShapeDtypeStruct((R, Cn), F32)] * 4,
        compiler_params=_cparams("parallel", "parallel"))(w, m, v, *g_parts)


SHARDED = (("ab_w_in", 2), ("ab_w_out", 1), ("c_w_in", 2), ("c_w_out", 1), ("ffn_up", 2), ("ffn_conv", 2),
           ("ffn_down", 1), ("ple_gate", 1), ("ple_proj", 2))
SMALL = ("mix_norm", "hg_lb_logits", "hg_out_norm", "q_norm", "k_norm", "sinks", "rel_bias", "ffn_norm",
         "ffn_conv_b", "ple_norm")
WEIGHTS = ("mix_norm", "ab_w_in", "hg_lb_logits", "hg_out_norm", "ab_w_out", "c_w_in", "q_norm", "k_norm", "sinks",
           "rel_bias", "c_w_out", "ffn_norm", "ffn_up", "ffn_conv", "ffn_conv_b", "ffn_down", "ple_norm", "ple_gate",
           "ple_proj")
PACK_ALIGN = 8 * LANES


def _pack(arrs):
    pieces = []
    for a in arrs:
        flat = a.reshape(-1)
        pad = -flat.shape[0] % PACK_ALIGN
        pieces.append(jnp.pad(flat, (0, pad)).reshape(-1, LANES))
    return jnp.concatenate(pieces, axis=0)


def _unpack(packed, like):
    out, r = [], 0
    for a in like:
        size = int(np.prod(a.shape))
        rows = (size + PACK_ALIGN - 1) // PACK_ALIGN * 8
        out.append(packed[r:r + rows].reshape(-1)[:size].reshape(a.shape))
        r += rows
    return out


def _family_range(name, lo, hi):
    if name.startswith("ab_"):
        idx = [i // 2 for i in range(lo, hi) if i % 2 == 0]
    elif name.startswith("c_"):
        idx = [i // 2 for i in range(lo, hi) if i % 2 == 1]
    else:
        idx = list(range(lo, hi))
    return (idx[0], idx[-1] + 1) if idx else None


W_IN = ("ab_w_in", "c_w_in")
REST = tuple(k for k, _ in SHARDED if k not in W_IN)
GATHER_PLAN = {
    "gather_first": [(0, W_IN)],
    "sb_fwd_0": [(0, REST), (1, None)],
    "hgrn_fwd_0": [(2, ("ab_w_in", "ab_w_out", "ffn_conv", "ffn_down", "ple_gate", "ple_proj"))],
    "swa_fwd_1": [(2, ("ffn_up",))],
    "sb_fwd_2": [(3, None)],
}
SCATTER_PLAN = {
    "swa_bwd_3": [(3, REST)],
    "sb_bwd_2": [(3, W_IN), (2, REST)],
    "swa_bwd_1": [(2, W_IN), (1, REST)],
    "sb_bwd_0": [(1, W_IN), (0, REST)],
    "scatter_last": [(0, W_IN)],
}


class _StepExchanges:
    def __init__(self, shards, W):
        self.shards, self.W = shards, W
        self.stacks = {}
        for k, axis in SHARDED:
            shp = shards[k].shape
            self.W[k] = lax.empty(tuple(4 * d if i == axis else d for i, d in enumerate(shp)), shards[k].dtype)
            self.stacks[k] = lax.empty((4,) + shp, shards[k].dtype)

    @staticmethod
    def _select(plan):
        idx = {}
        for layer, fams in plan:
            for k, _ in SHARDED:
                r = _family_range(k, layer, layer + 1)
                if r is not None and (fams is None or k in fams):
                    idx.setdefault(k, []).append(r[0])
        return [(k, axis, sorted(idx[k])) for k, axis in SHARDED if k in idx]

    def gather(self, call):
        sel = self._select(GATHER_PLAN.get(call, ()))
        if not sel:
            return None, None
        for _, _, ii in sel:
            assert ii == list(range(ii[0], ii[-1] + 1)), "one copy per family takes a contiguous layer range"
        names = [k for k, _, _ in sel]
        ex = _gather_exchange([self.shards[k] for k in names], [self.W[k] for k in names],
                              [a for _, a, _ in sel], [(ii[0], ii[-1] + 1) for _, _, ii in sel])
        return ex, (self.W, names)

    def scatter(self, call, G):
        sel = self._select(SCATTER_PLAN.get(call, ()))
        if not sel:
            return None, None
        assert all(len(ii) == 1 for _, _, ii in sel)
        names = [k for k, _, _ in sel]
        ex = _scatter_exchange([G[k][ii[0]] for k, _, ii in sel], [self.stacks[k] for k in names],
                               [a for _, a, _ in sel], [ii[0] for _, _, ii in sel])
        return ex, (self.stacks, names)

    @staticmethod
    def adopt(where, targets):
        if where is not None:
            book, names = where
            for k, t in zip(names, targets):
                book[k] = t


def _forward_backward(x, p, target, W, S, exchanges=None):
    T = x.shape[0]
    depth = p.shape[0]
    lb = _lower_bound_fwd(W["hg_lb_logits"], "lower_bound_fwd")
    bias = _bias_build(W["rel_bias"], "bias_build")
    qk_gain = jnp.concatenate([jnp.tile(W["q_norm"], (1, SW_HEADS)), jnp.tile(W["k_norm"], (1, SW_KV))], axis=1)

    def mm(a, wname, layer, mode, name, **kw):
        return _matmul(a, W[wname], mode, name, b_layer=layer, **kw)

    def gathering(call):
        return exchanges.gather(call) if exchanges else (None, None)

    def scattering(call):
        return exchanges.scatter(call, G) if exchanges else (None, None)

    saved = []
    h = x
    for i in range(depth):
        j = i // 2
        s = {"h0": h}
        s["hn"] = _rmsnorm_fwd(h, W["mix_norm"][i:i + 1], f"mix_norm_fwd_{i}")
        if i % 2 == 0:
            s["proj"] = mm(s["hn"], "ab_w_in", j, "nn", f"ab_in_{i}")
            ex, where = gathering(f"sb_fwd_{i}")
            cat, s["sb_tot"], arrived = _sb_fwd(s["proj"], S, f"sb_fwd_{i}", ex)
            _StepExchanges.adopt(where, arrived)
            ex, where = gathering(f"hgrn_fwd_{i}")
            s["cat"], s["oraw"], s["states"], arrived = _hgrn_fwd(s["proj"], cat, lb[j:j + 1],
                                                                  W["hg_out_norm"][j:j + 1], S, f"hgrn_fwd_{i}", ex)
            _StepExchanges.adopt(where, arrived)
            h = mm(s["cat"], "ab_w_out", j, "nn", f"ab_out_{i}", res=h)
        else:
            s["proj"] = mm(s["hn"], "c_w_in", j, "nn", f"c_in_{i}")
            s["qkn"] = _headnorm_fwd(s["proj"], qk_gain[j:j + 1], f"qk_norm_fwd_{i}")
            ex, where = gathering(f"swa_fwd_{i}")
            s["o"], arrived = _swa_fwd(s["qkn"], s["proj"], bias, W["sinks"][j], S, f"swa_fwd_{i}", ex)
            _StepExchanges.adopt(where, arrived)
            h = mm(s["o"], "c_w_out", j, "nn", f"c_out_{i}", res=h)
        s["h1"] = h
        s["hn2"] = _rmsnorm_fwd(h, W["ffn_norm"][i:i + 1], f"ffn_norm_fwd_{i}")
        s["u"] = mm(s["hn2"], "ffn_up", i, "nn", f"ffn_up_{i}")
        s["a"] = _convglu_fwd(s["u"], W["ffn_conv"][i], W["ffn_conv_b"][i:i + 1], S, f"convglu_fwd_{i}")
        h = mm(s["a"], "ffn_down", i, "nn", f"ffn_down_{i}", res=h)
        s["h2"] = h
        s["hn3"] = _rmsnorm_fwd(h, W["ple_norm"][i:i + 1], f"ple_norm_fwd_{i}")
        s["z"] = mm(s["hn3"], "ple_gate", i, "nn", f"ple_gate_{i}")
        s["pp"] = mm(p, "ple_proj", i, "nn", f"ple_proj_{i}", a_layer=i)
        h = _ple_fwd(h, s["z"], s["pp"], f"ple_fwd_{i}")
        saved.append(s)

    loss, dh = _loss_fwd_bwd(h, target, "loss")

    G = {k: [None] * depth for k in ("mix_norm", "ffn_norm", "ple_norm", "ffn_up", "ffn_conv", "ffn_conv_b",
                                     "ffn_down", "ple_gate", "ple_proj")}
    for k in ("ab_w_in", "ab_w_out", "c_w_in", "c_w_out", "hg_out_norm", "q_norm", "k_norm", "sinks", "lb"):
        G[k] = [None] * (depth // 2)
    dbias_total = None
    for i in reversed(range(depth)):
        j = i // 2
        s = saved[i]
        dz, dpp = _ple_bwd(dh, s["z"], s["pp"], f"ple_bwd_{i}")
        G["ple_proj"][i] = _matmul(p, dpp, "tn", f"d_ple_proj_{i}", out_dtype=BF16, a_layer=i)
        G["ple_gate"][i] = _matmul(s["hn3"], dz, "tn", f"d_ple_gate_{i}", out_dtype=BF16)
        dhn = mm(dz, "ple_gate", i, "nt", f"d_hn3_{i}")
        dh, G["ple_norm"][i] = _rmsnorm_bwd(s["h2"], W["ple_norm"][i:i + 1], dhn, dh, f"ple_norm_bwd_{i}")

        half_ff = D_FF // 2
        da = mm(dh, "ffn_down", i, "nt", f"d_a_{i}", tiles=(min(T, 1024), half_ff, D_MODEL))
        G["ffn_down"][i] = _matmul(s["a"], dh, "tn", f"d_ffn_down_{i}", out_dtype=BF16,
                                   tiles=(half_ff, 512, min(T, 2048)))
        du, dcw, dcb = _convglu_bwd(s["u"], da, W["ffn_conv"][i], W["ffn_conv_b"][i:i + 1], S, f"convglu_bwd_{i}")
        G["ffn_conv"][i] = jnp.swapaxes(dcw, 0, 1).reshape(3, 2 * D_FF)
        G["ffn_conv_b"][i] = dcb.reshape(1, 2 * D_FF)
        G["ffn_up"][i] = _matmul(s["hn2"], du, "tn", f"d_ffn_up_{i}", out_dtype=BF16,
                                 tiles=(D_MODEL, half_ff, min(T, 1024)))
        dhn = mm(du, "ffn_up", i, "nt", f"d_hn2_{i}")
        dh, G["ffn_norm"][i] = _rmsnorm_bwd(s["h1"], W["ffn_norm"][i:i + 1], dhn, dh, f"ffn_norm_bwd_{i}")

        if i % 2 == 0:
            dcat = mm(dh, "ab_w_out", j, "nt", f"d_cat_{i}")
            G["ab_w_out"][j] = _matmul(s["cat"], dh, "tn", f"d_ab_out_{i}", out_dtype=BF16)
            ex, where = scattering(f"sb_bwd_{i}")
            dproj, sent = _sb_bwd(s["proj"], s["sb_tot"], dcat, S, f"sb_bwd_{i}", ex)
            dproj, G["lb"][j], G["hg_out_norm"][j] = _hgrn_bwd(s["proj"], s["oraw"], dcat, s["states"], lb[j:j + 1],
                                                               W["hg_out_norm"][j:j + 1], dproj, S, f"hgrn_bwd_{i}")
            G["ab_w_in"][j] = _matmul(s["hn"], dproj, "tn", f"d_ab_in_{i}", out_dtype=BF16)
            dhn = mm(dproj, "ab_w_in", j, "nt", f"d_hn_{i}")
        else:
            do = mm(dh, "c_w_out", j, "nt", f"d_o_{i}")
            G["c_w_out"][j] = _matmul(s["o"], dh, "tn", f"d_c_out_{i}", out_dtype=BF16)
            ex, where = scattering(f"swa_bwd_{i}")
            dqkv, dbias, dsink, sent = _swa_bwd(s["qkn"], s["proj"], bias, W["sinks"][j], do, S, f"swa_bwd_{i}", ex)
            dbias_total = dbias if dbias_total is None else dbias_total + dbias
            G["sinks"][j] = dsink[:, 0]
            dproj, dgain = _headnorm_bwd(s["proj"], qk_gain[j:j + 1], dqkv, f"qk_norm_bwd_{i}")
            G["q_norm"][j] = dgain[0, :SW_Q_COLS].reshape(SW_HEADS, SW_DIM).sum(axis=0)
            G["k_norm"][j] = dgain[0, SW_Q_COLS:].reshape(SW_KV, SW_DIM).sum(axis=0)
            G["c_w_in"][j] = _matmul(s["hn"], dproj, "tn", f"d_c_in_{i}", out_dtype=BF16)
            dhn = mm(dproj, "c_w_in", j, "nt", f"d_hn_{i}")
        _StepExchanges.adopt(where, sent)
        dh, G["mix_norm"][i] = _rmsnorm_bwd(s["h0"], W["mix_norm"][i:i + 1], dhn, dh, f"mix_norm_bwd_{i}")

    grads = {k: G[k] for k, _ in SHARDED}
    for k in ("q_norm", "k_norm", "sinks"):
        grads[k] = jnp.stack(G[k])
    for k in ("mix_norm", "ffn_norm", "ple_norm", "ffn_conv_b", "hg_out_norm"):
        grads[k] = jnp.concatenate(G[k], axis=0)
    grads["hg_lb_logits"] = _lower_bound_bwd(W["hg_lb_logits"], jnp.concatenate(G["lb"], axis=0), "lower_bound_bwd")
    grads["rel_bias"] = _bias_reduce(dbias_total, "bias_reduce")[:, 0, :N_BUCKETS].T
    return loss, dh, grads


def kernel(x, p, mix_norm, ab_w_in, hg_lb_logits, hg_out_norm, ab_w_out, c_w_in, q_norm, k_norm, sinks, rel_bias, c_w_out, ffn_norm, ffn_up, ffn_conv, ffn_conv_b, ffn_down, ple_norm, ple_gate, ple_proj, loss_target, m_mix_norm, m_ab_w_in, m_hg_lb_logits, m_hg_out_norm, m_ab_w_out, m_c_w_in, m_q_norm, m_k_norm, m_sinks, m_rel_bias, m_c_w_out, m_ffn_norm, m_ffn_up, m_ffn_conv, m_ffn_conv_b, m_ffn_down, m_ple_norm, m_ple_gate, m_ple_proj, v_mix_norm, v_ab_w_in, v_hg_lb_logits, v_hg_out_norm, v_ab_w_out, v_c_w_in, v_q_norm, v_k_norm, v_sinks, v_rel_bias, v_c_w_out, v_ffn_norm, v_ffn_up, v_ffn_conv, v_ffn_conv_b, v_ffn_down, v_ple_norm, v_ple_gate, v_ple_proj):
    args = dict(locals())
    w = {k: args[k] for k in WEIGHTS}
    m = {k: args["m_" + k] for k in WEIGHTS}
    v = {k: args["v_" + k] for k in WEIGHTS}
    B, S, Dm = x.shape
    T = B * S
    names = [k for k, _ in SHARDED]

    W = {k: w[k] for k in SMALL}
    exchanges = _StepExchanges({k: w[k].astype(F32 if k == "ffn_conv" else BF16) for k in names}, W)
    first, where = exchanges.gather("gather_first")
    exchanges.adopt(where, _exchange_call(first, "gather_first"))

    loss, dx, grads = _forward_backward(x.reshape(T, Dm), p.reshape(p.shape[0], T, p.shape[-1]),
                                        loss_target.reshape(T, Dm), W, S, exchanges)
    loss = lax.psum(loss[0, 0], ("x", "y", "c"))

    last, where = exchanges.scatter("scatter_last", grads)
    exchanges.adopt(where, _exchange_call(last, "scatter_last"))
    stacks = [exchanges.stacks[k] for k in names]
    partial = [_sum4(st.reshape(4, -1, st.shape[-1]), f"sum_chips_{k}") for k, st in zip(names, stacks)]
    other = _swap_with_sibling(partial, "swap_core_sums")
    small_sum = _allreduce_small(_pack([grads[k] for k in SMALL]), "allreduce_small")

    out_g, out_d, out_m, out_v = {}, {}, {}, {}
    for k, mine, theirs in zip(names, partial, other):
        shp = w[k].shape
        r = [a.reshape(shp) for a in _adamw(w[k].reshape(mine.shape), m[k].reshape(mine.shape),
                                            v[k].reshape(mine.shape), [mine, theirs], f"adamw_{k}")]
        out_g[k], out_d[k], out_m[k], out_v[k] = r
    sm = _adamw(_pack([w[k] for k in SMALL]), _pack([m[k] for k in SMALL]), _pack([v[k] for k in SMALL]),
                [small_sum], "adamw_small")
    like = [w[k] for k in SMALL]
    for dst, packed in zip((out_g, out_d, out_m, out_v), sm):
        for k, a in zip(SMALL, _unpack(packed, like)):
            dst[k] = a

    return (loss, dx.reshape(B, S, Dm), *[out_g[k] for k in WEIGHTS], *[out_d[k] for k in WEIGHTS],
            *[out_m[k] for k in WEIGHTS], *[out_v[k] for k in WEIGHTS])
```

```python
import math

import numpy as np
import jax
import jax.numpy as jnp
from jax import lax
from jax.experimental import pallas as pl
from jax.experimental.pallas import tpu as pltpu

F32 = jnp.float32
BF16 = jnp.bfloat16
MESH = pl.DeviceIdType.MESH
ANY = pl.BlockSpec(memory_space=pl.ANY)

D_MODEL = 1024
EPS = 1e-6
SB_HEADS, SB_DIM = 8, 64
HG_HEADS, HG_DK = 4, 128
HG_CHUNK = 32
HG_STEP = 2
SW_HEADS, SW_KV, SW_DIM, WINDOW = 16, 4, 64, 128
N_BUCKETS, MAX_DISTANCE = 32, 128
D_FF = 2816
ATT_BLOCK = 128
SB_QBLOCK = 256
LANES = 128
NEG = -1e30

ADAM_LR, ADAM_B1, ADAM_B2, ADAM_EPS, ADAM_WD, ADAM_STEP = 0.001, 0.9, 0.999, 1e-08, 0.01, 10

VMEM_LIMIT = 56 * 1024 * 1024


def _cparams(*sem):
    return pltpu.CompilerParams(dimension_semantics=sem, vmem_limit_bytes=VMEM_LIMIT)


def _pick(n, cap):
    if n <= cap:
        return n
    best = None
    for d in range(LANES, cap + 1, LANES):
        if n % d == 0:
            best = d
    assert best is not None, (n, cap)
    return best


def _dot(a, b, ca, cb):
    return lax.dot_general(a.astype(BF16), b.astype(BF16), (((ca,), (cb,)), ((), ())),
                           preferred_element_type=F32)


def _split(x, terms):
    parts = []
    for _ in range(terms):
        hi = x.astype(BF16)
        parts.append(hi)
        x = x - hi.astype(F32)
    return parts


def _dot_exact_l(x, m, terms=2):
    out = None
    for p in _split(x, terms):
        t = lax.dot_general(p, m, (((1,), (0,)), ((), ())), preferred_element_type=F32)
        out = t if out is None else out + t
    return out


def _dot_exact_r(m, x, terms=3, cm=1):
    out = None
    for p in _split(x, terms):
        t = lax.dot_general(m, p, (((cm,), (0,)), ((), ())), preferred_element_type=F32)
        out = t if out is None else out + t
    return out


def _sig(x):
    return 1.0 / (1.0 + jnp.exp(-x))


def _iota2(shape, dim):
    return lax.broadcasted_iota(jnp.int32, shape, dim)


def _operand_spec(arr, layer, blk, index):
    if arr.ndim == 2:
        return pl.BlockSpec(blk, index)
    if layer is not None:
        return pl.BlockSpec((None,) + blk, lambda i, j, k: (layer,) + index(i, j, k))
    per_half = arr.shape[2] // blk[1]

    def halves(i, j, k):
        r, c = index(i, j, k)
        return (c // per_half, r, c % per_half)

    return pl.BlockSpec((None,) + blk, halves)


def _matmul(a, b, mode, name, out_dtype=F32, res=None, a_layer=None, b_layer=None, tiles=None):
    def dims(arr, layer):
        if arr.ndim == 2:
            return arr.shape
        return arr.shape[1:] if layer is not None else (arr.shape[1], 2 * arr.shape[2])

    (a0, a1), (b0, b1) = dims(a, a_layer), dims(b, b_layer)
    if mode == "nn":
        M, K, N = a0, a1, b1
    elif mode == "nt":
        M, K, N = a0, a1, b0
    else:
        K, M, N = a0, a1, b1
    cap_m, cap_n, cap_k = 1024, 1024, (1024 if mode == "tn" else 2048)
    tm, tn, tk = _pick(M, cap_m), _pick(N, cap_n), _pick(K, cap_k)
    if a.ndim == 3 and a_layer is None:
        if mode == "tn":
            tm = _pick(a.shape[2], cap_m)
        else:
            tk = _pick(a.shape[2], cap_k)
    if b.ndim == 3 and b_layer is None:
        if mode == "nt":
            tk = _pick(b.shape[2], cap_k)
        else:
            tn = _pick(b.shape[2], cap_n)
    if tiles is not None:
        tm, tn, tk = tiles
    assert M % tm == 0 and N % tn == 0 and K % tk == 0, (name, M, N, K, tm, tn, tk)
    nk = K // tk
    if mode == "tn":
        a_spec = _operand_spec(a, a_layer, (tk, tm), lambda i, j, k: (k, i))
    else:
        a_spec = _operand_spec(a, a_layer, (tm, tk), lambda i, j, k: (i, k))
    if mode == "nt":
        b_spec = _operand_spec(b, b_layer, (tn, tk), lambda i, j, k: (j, k))
    else:
        b_spec = _operand_spec(b, b_layer, (tk, tn), lambda i, j, k: (k, j))
    ca, cb = {"nn": (1, 0), "nt": (1, 1), "tn": (0, 0)}[mode]
    o_spec = pl.BlockSpec((tm, tn), lambda i, j, k: (i, j))

    def body(*refs):
        if res is None:
            a_ref, b_ref, o_ref, acc = refs
        else:
            a_ref, b_ref, r_ref, o_ref, acc = refs
        k = pl.program_id(2)

        @pl.when(k == 0)
        def _():
            acc[...] = jnp.zeros_like(acc)

        acc[...] += _dot(a_ref[...], b_ref[...], ca, cb)

        @pl.when(k == nk - 1)
        def _():
            r = acc[...]
            if res is not None:
                r = r + r_ref[...]
            o_ref[...] = r.astype(out_dtype)

    ins = [a, b] + ([] if res is None else [res])
    in_specs = [a_spec, b_spec] + ([] if res is None else [o_spec])
    return pl.pallas_call(
        body, name=name, grid=(M // tm, N // tn, nk), in_specs=in_specs, out_specs=o_spec,
        out_shape=jax.ShapeDtypeStruct((M, N), out_dtype),
        scratch_shapes=[pltpu.VMEM((tm, tn), F32)],
        compiler_params=_cparams("parallel", "parallel", "arbitrary"))(*ins)


ROW_TILE = 512


def _row_spec(width):
    return pl.BlockSpec((ROW_TILE, width), lambda i: (i, 0))


def _vec_spec(width):
    return pl.BlockSpec((1, width), lambda i: (0, 0))


def _rmsnorm_fwd(h, g, name):
    T, Dm = h.shape

    def body(h_ref, g_ref, o_ref):
        x = h_ref[...]
        r = lax.rsqrt(jnp.mean(x * x, axis=1, keepdims=True) + EPS)
        o_ref[...] = (x * r * g_ref[...]).astype(BF16)

    return pl.pallas_call(
        body, name=name, grid=(T // ROW_TILE,), in_specs=[_row_spec(Dm), _vec_spec(Dm)],
        out_specs=_row_spec(Dm), out_shape=jax.ShapeDtypeStruct((T, Dm), BF16),
        compiler_params=_cparams("parallel"))(h, g)


def _rmsnorm_bwd(h, g, dhn, dres, name):
    T, Dm = h.shape

    def body(h_ref, g_ref, dy_ref, dr_ref, dh_ref, dg_ref):
        i = pl.program_id(0)
        x = h_ref[...]
        dy = dy_ref[...]
        r = lax.rsqrt(jnp.mean(x * x, axis=1, keepdims=True) + EPS)
        gdy = dy * g_ref[...]
        m = jnp.mean(x * gdy, axis=1, keepdims=True)
        dh_ref[...] = dr_ref[...] + r * gdy - x * (r * r * r * m)
        part = jnp.sum(dy * x * r, axis=0, keepdims=True)

        @pl.when(i == 0)
        def _():
            dg_ref[...] = part

        @pl.when(i > 0)
        def _():
            dg_ref[...] += part

    return pl.pallas_call(
        body, name=name, grid=(T // ROW_TILE,),
        in_specs=[_row_spec(Dm), _vec_spec(Dm), _row_spec(Dm), _row_spec(Dm)],
        out_specs=[_row_spec(Dm), _vec_spec(Dm)],
        out_shape=[jax.ShapeDtypeStruct((T, Dm), F32), jax.ShapeDtypeStruct((1, Dm), F32)],
        compiler_params=_cparams("arbitrary"))(h, g, dhn, dres)


def _ple_fwd(h, z, pp, name):
    T, Dm = h.shape

    def body(h_ref, z_ref, p_ref, o_ref):
        o_ref[...] = h_ref[...] + _sig(z_ref[...]) * p_ref[...]

    return pl.pallas_call(
        body, name=name, grid=(T // ROW_TILE,), in_specs=[_row_spec(Dm)] * 3, out_specs=_row_spec(Dm),
        out_shape=jax.ShapeDtypeStruct((T, Dm), F32), compiler_params=_cparams("parallel"))(h, z, pp)


def _ple_bwd(dh, z, pp, name):
    T, Dm = dh.shape

    def body(dh_ref, z_ref, p_ref, dz_ref, dp_ref):
        s = _sig(z_ref[...])
        d = dh_ref[...]
        dz_ref[...] = d * p_ref[...] * s * (1.0 - s)
        dp_ref[...] = d * s

    return pl.pallas_call(
        body, name=name, grid=(T // ROW_TILE,), in_specs=[_row_spec(Dm)] * 3, out_specs=[_row_spec(Dm)] * 2,
        out_shape=[jax.ShapeDtypeStruct((T, Dm), F32)] * 2, compiler_params=_cparams("parallel"))(dh, z, pp)


def _loss_fwd_bwd(y, target, name):
    T, Dm = y.shape

    def body(y_ref, t_ref, l_ref, d_ref):
        i = pl.program_id(0)
        e = y_ref[...] - t_ref[...]
        d_ref[...] = e * (1.0 / Dm)
        part = jnp.full((8, LANES), 0.5 / Dm, F32) * jnp.sum(e * e)

        @pl.when(i == 0)
        def _():
            l_ref[...] = part

        @pl.when(i > 0)
        def _():
            l_ref[...] += part

    return pl.pallas_call(
        body, name=name, grid=(T // ROW_TILE,), in_specs=[_row_spec(Dm)] * 2,
        out_specs=[pl.BlockSpec((8, LANES), lambda i: (0, 0)), _row_spec(Dm)],
        out_shape=[jax.ShapeDtypeStruct((8, LANES), F32), jax.ShapeDtypeStruct((T, Dm), F32)],
        compiler_params=_cparams("arbitrary"))(y, target)


def _head_mean_matrix():
    r = _iota2((LANES, LANES), 0) >= SW_DIM
    c = _iota2((LANES, LANES), 1) >= SW_DIM
    return jnp.where(r == c, 1.0 / SW_DIM, 0.0).astype(BF16)


def _headnorm_fwd(x, g_lane, name):
    T = x.shape[0]
    C = g_lane.shape[1]

    def body(x_ref, g_ref, y_ref):
        xv = x_ref[...]
        ms = _dot_exact_l(xv * xv, _head_mean_matrix())
        y_ref[...] = xv * lax.rsqrt(ms + EPS) * g_ref[...]

    spec = pl.BlockSpec((ROW_TILE, LANES), lambda j, i: (i, j))
    return pl.pallas_call(
        body, name=name, grid=(C // LANES, T // ROW_TILE),
        in_specs=[spec, pl.BlockSpec((1, LANES), lambda j, i: (0, j))], out_specs=spec,
        out_shape=jax.ShapeDtypeStruct((T, C), F32), compiler_params=_cparams("parallel", "parallel"))(x, g_lane)


def _headnorm_bwd(x, g_lane, dy_full, name):
    T = x.shape[0]
    C = g_lane.shape[1]

    def body(x_ref, g_ref, dy_ref, dx_ref, dg_ref):
        i = pl.program_id(1)
        xv = x_ref[...]
        dy = dy_ref[...]
        bd = _head_mean_matrix()
        r = lax.rsqrt(_dot_exact_l(xv * xv, bd) + EPS)
        gdy = dy * g_ref[...]
        m = _dot_exact_l(xv * gdy, bd)
        dx_ref[...] = r * gdy - xv * (r * r * r * m)
        part = jnp.sum(dy * xv * r, axis=0, keepdims=True)

        @pl.when(i == 0)
        def _():
            dg_ref[...] = part

        @pl.when(i > 0)
        def _():
            dg_ref[...] += part

    spec = pl.BlockSpec((ROW_TILE, LANES), lambda j, i: (i, j))
    vspec = pl.BlockSpec((1, LANES), lambda j, i: (0, j))
    return pl.pallas_call(
        body, name=name, grid=(C // LANES, T // ROW_TILE), in_specs=[spec, vspec, spec],
        out_specs=[spec, vspec],
        out_shape=[jax.ShapeDtypeStruct(dy_full.shape, F32), jax.ShapeDtypeStruct((1, C), F32)],
        input_output_aliases={2: 0}, compiler_params=_cparams("parallel", "arbitrary"))(x, g_lane, dy_full)


CONV_TILE = 128


def _shift_down(x, k):
    rows = _iota2(x.shape, 0)
    return jnp.where(rows >= k, pltpu.roll(x, k, 0), 0.0)


def _shift_up(x, k):
    n = x.shape[0]
    rows = _iota2(x.shape, 0)
    return jnp.where(rows < n - k, pltpu.roll(x, n - k, 0), 0.0)


def _conv3(u, w_ref, b_ref):
    return w_ref[2:3, :] * u + w_ref[1:2, :] * _shift_down(u, 1) + w_ref[0:1, :] * _shift_down(u, 2) + b_ref[...]


def _convglu_fwd(u, cw, cb, S, name):
    T = u.shape[0]
    nf = D_FF // CONV_TILE

    def body(ug_ref, uu_ref, wg_ref, wu_ref, bg_ref, bu_ref, a_ref):
        yg = _conv3(ug_ref[...], wg_ref, bg_ref)
        yu = _conv3(uu_ref[...], wu_ref, bu_ref)
        a_ref[...] = (yg * _sig(yg) * yu).astype(BF16)

    def blk(rows, off):
        return pl.BlockSpec((rows, CONV_TILE), (lambda b, j: (b, j + off)) if rows == S else (lambda b, j: (0, j + off)))

    return pl.pallas_call(
        body, name=name, grid=(T // S, nf),
        in_specs=[blk(S, 0), blk(S, nf), blk(3, 0), blk(3, nf), blk(1, 0), blk(1, nf)],
        out_specs=blk(S, 0), out_shape=jax.ShapeDtypeStruct((T, D_FF), BF16),
        compiler_params=_cparams("parallel", "parallel"))(u, u, cw, cw, cb, cb)


def _convglu_bwd(u, da, cw, cb, S, name):
    T = u.shape[0]
    nf = D_FF // CONV_TILE

    def body(ug_ref, uu_ref, da_ref, wg_ref, wu_ref, bg_ref, bu_ref, du_ref, dw_ref, db_ref):
        b = pl.program_id(1)
        ug, uu = ug_ref[...], uu_ref[...]
        yg = _conv3(ug, wg_ref, bg_ref)
        yu = _conv3(uu, wu_ref, bu_ref)
        s = _sig(yg)
        da_v = da_ref[...]
        for half, (uv, w_ref, dy) in enumerate(((ug, wg_ref, da_v * yu * (s * (1.0 + yg * (1.0 - s)))),
                                                (uu, wu_ref, da_v * yg * s))):
            du_ref[half] = w_ref[2:3, :] * dy + w_ref[1:2, :] * _shift_up(dy, 1) + w_ref[0:1, :] * _shift_up(dy, 2)
            dws = [jnp.sum(dy * _shift_down(uv, 2), axis=0, keepdims=True),
                   jnp.sum(dy * _shift_down(uv, 1), axis=0, keepdims=True),
                   jnp.sum(dy * uv, axis=0, keepdims=True)]
            dbv = jnp.sum(dy, axis=0, keepdims=True)

            @pl.when(b == 0)
            def _():
                for k in range(3):
                    dw_ref[half, k:k + 1, :] = dws[k]
                db_ref[half] = dbv

            @pl.when(b > 0)
            def _():
                for k in range(3):
                    dw_ref[half, k:k + 1, :] += dws[k]
                db_ref[half] += dbv

    def blk(rows, off):
        return pl.BlockSpec((rows, CONV_TILE), (lambda j, b: (b, j + off)) if rows == S else (lambda j, b: (0, j + off)))

    def both(rows):
        return pl.BlockSpec((2, rows, CONV_TILE), (lambda j, b: (0, b, j)) if rows == S else (lambda j, b: (0, 0, j)))

    return pl.pallas_call(
        body, name=name, grid=(nf, T // S),
        in_specs=[blk(S, 0), blk(S, nf), blk(S, 0), blk(3, 0), blk(3, nf), blk(1, 0), blk(1, nf)],
        out_specs=[both(S), both(3), both(1)],
        out_shape=[jax.ShapeDtypeStruct((2, T, D_FF), F32), jax.ShapeDtypeStruct((2, 3, D_FF), F32),
                   jax.ShapeDtypeStruct((2, 1, D_FF), F32)],
        compiler_params=_cparams("parallel", "arbitrary"))(u, u, da, cw, cw, cb, cb)


def _sb_scores(qb, kblk, on_diag_mask):
    z = _dot(qb, kblk, 1, 1) * (SB_DIM ** -0.5)
    l1 = jnp.log(1.0 + jnp.exp(-jnp.abs(z)))
    ls = jnp.minimum(z, 0.0) - l1
    lk = jnp.where(on_diag_mask, ls - z, 0.0)
    return ls, lk


def _sb_fwd(proj, S, name, exchange=None):
    T = proj.shape[0]
    BQ, BK = SB_QBLOCK, ATT_BLOCK
    unroll = BQ // BK
    nq = S // BQ
    nhp = SB_HEADS // 2
    heads = [slice(h * SB_DIM, (h + 1) * SB_DIM) for h in range(2)]

    grid = (T // S, nhp)

    def body(*refs):
        (q_ref, k_ref, v_ref, o_ref, tot_ref), carried = _carried(exchange, refs, 3, 2, 0)
        _start_carried(exchange, carried, grid)
        ahead = _iota2((BQ, BK), 1) - _iota2((BQ, BK), 0)
        upper = (_iota2((BK, BK), 0) > _iota2((BK, BK), 1)).astype(BF16)

        def qloop(iq, carry):
            q0 = pl.multiple_of(iq * BQ, BQ)
            rows = pl.ds(q0, BQ)
            qbs = [q_ref[rows, sl] for sl in heads]
            nkb = (iq + 1) * (BQ // BK)

            def kloop(jj, kc):
                blocks = []
                for u in range(unroll):
                    k0 = pl.multiple_of((nkb - 1 - unroll * jj - u) * BK, BK)
                    blocks.append((pl.ds(k0, BK), ahead < q0 - k0))
                units = [(h, krows, mask) for krows, mask in blocks for h in range(2)]
                scores = [_sb_scores(qbs[h], k_ref[krows, heads[h]], mask) for h, krows, mask in units]
                laters = [_dot_exact_l(lk, upper) for _, lk in scores]
                runs = [kc[h][0] for h in range(2)]
                accs = [kc[h][1] for h in range(2)]
                for (h, krows, mask), (ls, lk), later in zip(units, scores, laters):
                    w = jnp.where(mask, jnp.exp(ls + later + runs[h]), 0.0)
                    accs[h] = accs[h] + _dot(w, v_ref[krows, heads[h]], 1, 0)
                    runs[h] = runs[h] + jnp.sum(lk, axis=1, keepdims=True)
                return tuple((runs[h], accs[h]) for h in range(2))

            init = (jnp.zeros((BQ, 1), F32), jnp.zeros((BQ, SB_DIM), F32))
            res = lax.fori_loop(0, nkb // unroll, kloop, (init, init))
            for h, (sl, (run, acc)) in enumerate(zip(heads, res)):
                o_ref[rows, sl] = acc
                tot_ref[rows, h:h + 1] = run
            return carry

        lax.fori_loop(0, nq, qloop, 0)
        _wait_carried(exchange, carried, grid)

    def spec(off):
        return pl.BlockSpec((S, LANES), lambda b, hp: (b, hp + off))

    (cat, tot), targets = _carrier_call(
        body, name, grid, [spec(0), spec(nhp), spec(2 * nhp)],
        [spec(0), pl.BlockSpec((None, S, 2), lambda b, hp: (hp, b, 0))],
        [jax.ShapeDtypeStruct((T, 2 * SB_HEADS * SB_DIM), F32), jax.ShapeDtypeStruct((nhp, T, 2), F32)],
        [], {}, (proj, proj, proj), exchange)
    return cat, tot, targets


def _sb_bwd(proj, tot, dcat, S, name, exchange=None):
    T, width = proj.shape
    BQ, BK = SB_QBLOCK, ATT_BLOCK
    unroll = BQ // BK
    nq = S // BQ
    nhp = SB_HEADS // 2
    scale = SB_DIM ** -0.5
    heads = [slice(h * SB_DIM, (h + 1) * SB_DIM) for h in range(2)]

    grid = (T // S, nhp)

    def body(*refs):
        (q_ref, k_ref, v_ref, tot_ref, do_ref, dp_hbm, dq_s, dk_s, dv_s, sems), carried = _carried(
            exchange, refs, 5, 1, 4)
        _start_carried(exchange, carried, grid)
        b, hp = pl.program_id(0), pl.program_id(1)
        ahead = _iota2((BQ, BK), 1) - _iota2((BQ, BK), 0)
        r, c = _iota2((BK, BK), 0), _iota2((BK, BK), 1)
        upto = (r <= c).astype(BF16)
        earlier = (r < c).astype(BF16)
        dk_s[...] = jnp.zeros_like(dk_s)
        dv_s[...] = jnp.zeros_like(dv_s)

        def qloop(iq, carry):
            q0 = pl.multiple_of(iq * BQ, BQ)
            rows = pl.ds(q0, BQ)
            qbs = [q_ref[rows, sl] for sl in heads]
            dobs = [do_ref[rows, sl] for sl in heads]
            totals = [tot_ref[rows, h:h + 1] for h in range(2)]

            def kloop(m, kc):
                blocks = []
                for u in range(unroll):
                    k0 = pl.multiple_of((unroll * m + u) * BK, BK)
                    blocks.append((pl.ds(k0, BK), ahead < q0 - k0))
                units = [(h, krows, mask) for krows, mask in blocks for h in range(2)]
                kblks = [k_ref[krows, heads[h]] for h, krows, _ in units]
                scores = [_sb_scores(qbs[h], kblk, mask) for (h, _, mask), kblk in zip(units, kblks)]
                prefixes = [_dot_exact_l(lk, upto) for _, lk in scores]
                dws = [_dot(dobs[h], v_ref[krows, heads[h]], 1, 1) for h, krows, _ in units]
                runs = [kc[h][0] for h in range(2)]
                gruns = [kc[h][1] for h in range(2)]
                dqs = [kc[h][2] for h in range(2)]
                ws, gs = [], []
                for (h, _, mask), (ls, lk), prefix, dw in zip(units, scores, prefixes, dws):
                    w = jnp.where(mask, jnp.exp(ls + (totals[h] - (prefix + runs[h]))), 0.0)
                    runs[h] = runs[h] + jnp.sum(lk, axis=1, keepdims=True)
                    ws.append(w)
                    gs.append(w * dw)
                gprefixes = [_dot_exact_l(g, earlier) for g in gs]
                dzs = []
                for (h, _, mask), (ls, _), g, gprefix in zip(units, scores, gs, gprefixes):
                    beta = jnp.exp(ls)
                    dzs.append(jnp.where(mask, g * (1.0 - beta) - beta * (gprefix + gruns[h]), 0.0) * scale)
                    gruns[h] = gruns[h] + jnp.sum(g, axis=1, keepdims=True)
                for (h, krows, _), kblk, w, dz in zip(units, kblks, ws, dzs):
                    dv_s[krows, heads[h]] += _dot(w, dobs[h], 0, 0)
                    dk_s[krows, heads[h]] += _dot(dz, qbs[h], 0, 0)
                    dqs[h] = dqs[h] + _dot(dz, kblk, 1, 0)
                return tuple((runs[h], gruns[h], dqs[h]) for h in range(2))

            zero = jnp.zeros((BQ, 1), F32)
            init = (zero, zero, jnp.zeros((BQ, SB_DIM), F32))
            res = lax.fori_loop(0, (iq + 1) * (BQ // BK) // unroll, kloop, (init, init))
            for sl, (_, _, dq) in zip(heads, res):
                dq_s[rows, sl] = dq
            return carry

        lax.fori_loop(0, nq, qloop, 0)
        r0 = pl.multiple_of(b * S, S)
        copies = []
        for n, buf in enumerate((dq_s, dk_s, dv_s)):
            c0 = pl.multiple_of((hp + n * nhp) * LANES, LANES)
            copies.append(pltpu.make_async_copy(buf, dp_hbm.at[pl.ds(r0, S), pl.ds(c0, LANES)], sems.at[n]))
        for cp in copies:
            cp.start()
        for cp in copies:
            cp.wait()
        _wait_carried(exchange, carried, grid)

    def spec(off):
        return pl.BlockSpec((S, LANES), lambda b, hp: (b, hp + off))

    (dproj,), targets = _carrier_call(
        body, name, grid,
        [spec(0), spec(nhp), spec(2 * nhp), pl.BlockSpec((None, S, 2), lambda b, hp: (hp, b, 0)), spec(0)],
        [ANY], [jax.ShapeDtypeStruct((T, width), F32)],
        [pltpu.VMEM((S, LANES), F32)] * 3 + [pltpu.SemaphoreType.DMA((3,))], {}, (proj, proj, proj, tot, dcat),
        exchange)
    return dproj, targets


HG_COL0 = 3 * SB_HEADS * SB_DIM // LANES


def _hg_gates(q, fp, lbv):
    sg = _sig(fp)
    f = lbv + (1.0 - lbv) * sg
    kk = (1.0 - lbv) * _sig(-fp)
    sq = _sig(q)
    return sg, f, kk, sq


def _hg_chunk(qs, kk, lf, incl):
    C = HG_CHUNK
    b = _dot_exact_r(incl, lf)
    bl = b[C - 1:C, :]
    bm = b[C // 2 - 1:C // 2, :]
    e_t = jnp.exp(b - bm)
    e_s = jnp.exp(bm - b)
    e_i = jnp.exp(b)
    e_e = jnp.exp(bl - b)
    return bl, e_t, e_s, e_i, e_e


def _hgrn_fwd(proj, cat, lb, hgn, S, name, exchange=None):
    T = proj.shape[0]
    B = T // S
    C = HG_CHUNK
    NC = S // C

    grid = (B, HG_HEADS // HG_STEP)

    def body(*refs):
        (q_ref, f_ref, i_ref, g_ref, lb_ref, hgn_ref, _, ob_ref, oraw_ref, st_ref, state), carried = _carried(
            exchange, refs, 7, 3, 1)
        _start_carried(exchange, carried, grid)
        state[...] = jnp.zeros_like(state)
        row, col = _iota2((C, C), 0), _iota2((C, C), 1)
        causal = row >= col
        incl = causal.astype(BF16)

        def chunk(c, carry):
            rows = pl.ds(pl.multiple_of(c * C, C), C)
            for hh in range(HG_STEP):
                hs = slice(hh * LANES, (hh + 1) * LANES)
                q, iv, gv = q_ref[rows, hs], i_ref[rows, hs], g_ref[rows, hs]
                _, f, kk, sq = _hg_gates(q, f_ref[rows, hs], lb_ref[:, hs])
                qs = q * sq
                bl, e_t, e_s, e_i, e_e = _hg_chunk(qs, kk, jnp.log(f), incl)
                p = jnp.where(causal, _dot(qs * e_t, kk * e_s, 1, 1), 0.0)
                st = state[hh]
                st_ref[0, hh, c] = st
                o = _dot(qs * e_i, st, 1, 1) + _dot(p, iv, 1, 0)
                state[hh] = st * jnp.exp(bl) + _dot(iv, kk * e_e, 0, 0)
                oraw_ref[rows, hs] = o
                r = lax.rsqrt(jnp.mean(o * o, axis=1, keepdims=True) + EPS)
                ob_ref[rows, hs] = o * r * hgn_ref[...] * (gv * _sig(gv))
            return carry

        lax.fori_loop(0, NC, chunk, 0)
        _wait_carried(exchange, carried, grid)

    width = HG_STEP * LANES
    col0 = HG_COL0 * LANES // width
    nstep = HG_HEADS // HG_STEP

    def spec(off):
        return pl.BlockSpec((S, width), lambda b, h: (b, h + off))

    outs, targets = _carrier_call(
        body, name, grid,
        [spec(col0), spec(col0 + nstep), spec(col0 + 2 * nstep), spec(col0 + 3 * nstep),
         pl.BlockSpec((1, width), lambda b, h: (0, h)), pl.BlockSpec((1, LANES), lambda b, h: (0, 0)), ANY],
        [spec(nstep), spec(0), pl.BlockSpec((1, HG_STEP, NC, LANES, LANES), lambda b, h: (b, h, 0, 0, 0))],
        [jax.ShapeDtypeStruct(cat.shape, F32), jax.ShapeDtypeStruct((T, HG_HEADS * LANES), F32),
         jax.ShapeDtypeStruct((B, HG_HEADS, NC, LANES, LANES), F32)],
        [pltpu.VMEM((HG_STEP, LANES, LANES), F32)], {6: 0}, (proj, proj, proj, proj, lb, hgn, cat), exchange)
    return (*outs, targets)


def _hgrn_bwd(proj, oraw, dcat, states, lb, hgn, dproj, S, name):
    T = proj.shape[0]
    B = T // S
    C = HG_CHUNK
    NC = S // C

    def body(q_ref, f_ref, i_ref, g_ref, oraw_ref, dy_ref, st_ref, lb_ref, hgn_ref, dp_in,
             dp_hbm, dlb_ref, dhgn_ref, dstate, dq_s, df_s, di_s, dg_s, sems):
        del dp_in
        h, b = pl.program_id(0), pl.program_id(1)
        row, col = _iota2((C, C), 0), _iota2((C, C), 1)
        causal = row >= col
        incl = causal.astype(BF16)
        last_row = _iota2((C, LANES), 0) == C - 1
        hg = hgn_ref[...]
        dstate[...] = jnp.zeros_like(dstate)

        @pl.when(b == 0)
        def _():
            dlb_ref[...] = jnp.zeros_like(dlb_ref)

        @pl.when(jnp.logical_and(b == 0, h == 0))
        def _():
            dhgn_ref[...] = jnp.zeros_like(dhgn_ref)

        def chunk(cc, carry):
            c = NC - 1 - cc
            rows = pl.ds(pl.multiple_of(c * C, C), C)
            for hh in range(HG_STEP):
                hs = slice(hh * LANES, (hh + 1) * LANES)
                lbv = lb_ref[:, hs]
                q, fp, iv, gv = q_ref[rows, hs], f_ref[rows, hs], i_ref[rows, hs], g_ref[rows, hs]
                o = oraw_ref[rows, hs]
                dy = dy_ref[rows, hs]
                r = lax.rsqrt(jnp.mean(o * o, axis=1, keepdims=True) + EPS)
                on = o * r
                sgv = _sig(gv)
                silu_g = gv * sgv
                dg_s[rows, hs] = dy * on * hg * (sgv * (1.0 + gv * (1.0 - sgv)))
                dhgn_ref[...] += jnp.sum(dy * on * silu_g, axis=0, keepdims=True)
                dn = dy * hg * silu_g
                do = r * dn - o * (r * r * r * jnp.mean(o * dn, axis=1, keepdims=True))
                sg, f, kk, sq = _hg_gates(q, fp, lbv)
                qs = q * sq
                bl, e_t, e_s, e_i, e_e = _hg_chunk(qs, kk, jnp.log(f), incl)
                qd, kd, qi, ke = qs * e_t, kk * e_s, qs * e_i, kk * e_e
                p = jnp.where(causal, _dot(qd, kd, 1, 1), 0.0)
                st = st_ref[0, hh, c]
                dst = dstate[hh]
                ebl = jnp.exp(bl)
                dqi = _dot(do, st, 1, 0)
                dp = jnp.where(causal, _dot(do, iv, 1, 1), 0.0)
                di_s[rows, hs] = _dot(p, do, 0, 0) + _dot(ke, dst, 1, 1)
                dqd = _dot(dp, kd, 1, 0)
                dkd = _dot(dp, qd, 0, 0)
                dke = _dot(iv, dst, 1, 0)
                dbl = jnp.sum(st * dst, axis=0, keepdims=True) * ebl + jnp.sum(dke * ke, axis=0, keepdims=True)
                db = dqd * qd - dkd * kd + dqi * qi - dke * ke + jnp.where(last_row, dbl, 0.0)
                dqs = dqd * e_t + dqi * e_i
                dkk = dkd * e_s + dke * e_e
                dlf = _dot_exact_r(incl, db, cm=0)
                dstate[hh] = _dot(do, qi, 0, 0) + dst * ebl
                oms = 1.0 - sg
                dfd = dlf / f
                df_s[rows, hs] = (dfd - dkk) * (1.0 - lbv) * sg * oms
                dq_s[rows, hs] = dqs * (sq * (1.0 + q * (1.0 - sq)))
                dlb_ref[:, hs] += jnp.sum((dfd - dkk) * oms, axis=0, keepdims=True)
            return carry

        lax.fori_loop(0, NC, chunk, 0)
        r0 = pl.multiple_of(b * S, S)
        copies = []
        for n, buf in enumerate((dq_s, df_s, di_s, dg_s)):
            c0 = pl.multiple_of((col0 + n * nstep + h) * width, width)
            copies.append(pltpu.make_async_copy(buf, dp_hbm.at[pl.ds(r0, S), pl.ds(c0, width)], sems.at[n]))
        for cp in copies:
            cp.start()
        for cp in copies:
            cp.wait()

    width = HG_STEP * LANES
    col0 = HG_COL0 * LANES // width
    nstep = HG_HEADS // HG_STEP

    def spec(off):
        return pl.BlockSpec((S, width), lambda h, b: (b, h + off))

    return pl.pallas_call(
        body, name=name, grid=(nstep, B),
        in_specs=[spec(col0), spec(col0 + nstep), spec(col0 + 2 * nstep), spec(col0 + 3 * nstep), spec(0), spec(nstep),
                  pl.BlockSpec((1, HG_STEP, NC, LANES, LANES), lambda h, b: (b, h, 0, 0, 0)),
                  pl.BlockSpec((1, width), lambda h, b: (0, h)), pl.BlockSpec((1, LANES), lambda h, b: (0, 0)), ANY],
        out_specs=[ANY, pl.BlockSpec((1, width), lambda h, b: (0, h)), pl.BlockSpec((1, LANES), lambda h, b: (0, 0))],
        out_shape=[jax.ShapeDtypeStruct(dproj.shape, F32), jax.ShapeDtypeStruct((1, HG_HEADS * LANES), F32),
                   jax.ShapeDtypeStruct((1, LANES), F32)],
        scratch_shapes=[pltpu.VMEM((HG_STEP, LANES, LANES), F32)] + [pltpu.VMEM((S, width), F32)] * 4
        + [pltpu.SemaphoreType.DMA((4,))],
        input_output_aliases={9: 0},
        compiler_params=_cparams("arbitrary", "arbitrary"))(proj, proj, proj, proj, oraw, dcat, states, lb, hgn, dproj)


def _lower_bound_fwd(logits, name):
    assert logits.shape[0] == 2

    def body(l_ref, o_ref):
        l0, l1 = l_ref[0:1, :], l_ref[1:2, :]
        m = jnp.maximum(l0, l1)
        e0, e1 = jnp.exp(l0 - m), jnp.exp(l1 - m)
        o_ref[0:1, :] = jnp.zeros_like(l0)
        o_ref[1:2, :] = e1 / (e0 + e1)

    return pl.pallas_call(body, name=name, out_shape=jax.ShapeDtypeStruct(logits.shape, F32))(logits)


def _lower_bound_bwd(logits, dlb, name):
    def body(l_ref, d_ref, o_ref):
        l0, l1 = l_ref[0:1, :], l_ref[1:2, :]
        m = jnp.maximum(l0, l1)
        e0, e1 = jnp.exp(l0 - m), jnp.exp(l1 - m)
        s1 = e1 / (e0 + e1)
        t = s1 * (1.0 - s1) * d_ref[1:2, :]
        o_ref[0:1, :] = -t
        o_ref[1:2, :] = t

    return pl.pallas_call(body, name=name, out_shape=jax.ShapeDtypeStruct(logits.shape, F32))(logits, dlb)


def _bucket_thresholds():
    dist = np.arange(WINDOW)
    max_exact = N_BUCKETS // 2
    large = max_exact + (np.log(np.maximum(dist, max_exact) / max_exact) / math.log(MAX_DISTANCE / max_exact)
                         * (N_BUCKETS - max_exact)).astype(np.int32)
    bucket = np.where(dist < max_exact, dist, np.minimum(large, N_BUCKETS - 1))
    assert np.all(np.diff(bucket) >= 0)
    return [int(np.argmax(bucket >= k)) if np.any(bucket >= k) else 10 ** 6 for k in range(1, N_BUCKETS)]


def _band_bucket():
    dist = _iota2((WINDOW, 2 * WINDOW), 0) + WINDOW - _iota2((WINDOW, 2 * WINDOW), 1)
    bucket = jnp.zeros((WINDOW, 2 * WINDOW), jnp.int32)
    for thr in _bucket_thresholds():
        bucket = bucket + (dist >= thr).astype(jnp.int32)
    band = jnp.logical_and(dist >= 0, dist < WINDOW)
    return bucket, band


def _bias_build(rel_bias, name):
    def body(rb_ref, o_ref):
        h = pl.program_id(0)
        bucket, _ = _band_bucket()
        bias = jnp.zeros((WINDOW, 2 * WINDOW), F32)
        for k in range(N_BUCKETS):
            bias = jnp.where(bucket == k, rb_ref[k, h], bias)
        o_ref[0] = bias

    return pl.pallas_call(
        body, name=name, grid=(SW_HEADS,), in_specs=[pl.BlockSpec(memory_space=pltpu.SMEM)],
        out_specs=pl.BlockSpec((1, WINDOW, 2 * WINDOW), lambda h: (h, 0, 0)),
        out_shape=jax.ShapeDtypeStruct((SW_HEADS, WINDOW, 2 * WINDOW), F32),
        compiler_params=_cparams("parallel"))(rel_bias)


def _bias_reduce(dbias, name):
    def body(d_ref, o_ref):
        bucket, band = _band_bucket()
        d = jnp.where(band, d_ref[0], 0.0)
        lane = _iota2((1, LANES), 1)
        out = jnp.zeros((1, LANES), F32)
        for k in range(N_BUCKETS):
            out = jnp.where(lane == k, jnp.sum(jnp.where(bucket == k, d, 0.0)), out)
        o_ref[0] = out

    return pl.pallas_call(
        body, name=name, grid=(SW_HEADS,), in_specs=[pl.BlockSpec((1, WINDOW, 2 * WINDOW), lambda h: (h, 0, 0))],
        out_specs=pl.BlockSpec((1, 1, LANES), lambda h: (h, 0, 0)),
        out_shape=jax.ShapeDtypeStruct((SW_HEADS, 1, LANES), F32), compiler_params=_cparams("parallel"))(dbias)


SW_Q_COLS = SW_HEADS * SW_DIM
SW_K_BLOCK0 = SW_Q_COLS // LANES
SW_V_BLOCK0 = SW_K_BLOCK0 + SW_KV * SW_DIM // LANES
SW_STEP_HEADS = 8


def _swa_probs(qb, kprev, kcur, bias_ref, hl, sink, mprev, mcur):
    scale = SW_DIM ** -0.5
    lp = jnp.where(mprev, _dot(qb, kprev, 1, 1) * scale + bias_ref[hl, :, 0:WINDOW], NEG)
    lc = jnp.where(mcur, _dot(qb, kcur, 1, 1) * scale + bias_ref[hl, :, WINDOW:2 * WINDOW], NEG)
    m = jnp.maximum(jnp.maximum(jnp.max(lp, axis=1, keepdims=True), jnp.max(lc, axis=1, keepdims=True)), sink)
    ep, ec = jnp.exp(lp - m), jnp.exp(lc - m)
    es = jnp.exp(sink - m)
    den = jnp.sum(ep, axis=1, keepdims=True) + jnp.sum(ec, axis=1, keepdims=True) + es
    return ep, ec, es, den


def _swa_fwd(qkn, proj, bias, sinks, S, name, exchange=None):
    T = qkn.shape[0]
    W = WINDOW
    nb = S // W

    grid = (T // S, 2)

    def body(*refs):
        (q_ref, k_ref, v_ref, bias_ref, sink_ref, o_ref), carried = _carried(exchange, refs, 5, 1, 0)
        _start_carried(exchange, carried, grid)
        kp = pl.program_id(1)
        row, col = _iota2((W, W), 0), _iota2((W, W), 1)
        mcur = col <= row
        above = col > row

        def blk(n, carry):
            rows = pl.ds(pl.multiple_of(n * W, W), W)
            prow = pl.ds(pl.multiple_of(jnp.maximum(n - 1, 0) * W, W), W)
            mprev = jnp.logical_and(above, n > 0)
            for kvh in range(2):
                ksl = slice(kvh * SW_DIM, (kvh + 1) * SW_DIM)
                kcur, kprev = k_ref[rows, ksl], k_ref[prow, ksl]
                vcur, vprev = v_ref[rows, ksl], v_ref[prow, ksl]
                for g in range(4):
                    hl = kvh * 4 + g
                    qsl = slice(hl * SW_DIM, (hl + 1) * SW_DIM)
                    sink = sink_ref[kp * SW_STEP_HEADS + hl]
                    ep, ec, _, den = _swa_probs(q_ref[rows, qsl], kprev, kcur, bias_ref, hl, sink, mprev, mcur)
                    o_ref[rows, qsl] = (_dot(ep, vprev, 1, 0) + _dot(ec, vcur, 1, 0)) / den
            return carry

        lax.fori_loop(0, nb, blk, 0)
        _wait_carried(exchange, carried, grid)

    (o,), targets = _carrier_call(
        body, name, grid,
        [pl.BlockSpec((S, 4 * LANES), lambda b, kp: (b, kp)),
         pl.BlockSpec((S, LANES), lambda b, kp: (b, SW_K_BLOCK0 + kp)),
         pl.BlockSpec((S, LANES), lambda b, kp: (b, SW_V_BLOCK0 + kp)),
         pl.BlockSpec((SW_STEP_HEADS, W, 2 * W), lambda b, kp: (kp, 0, 0)),
         pl.BlockSpec(memory_space=pltpu.SMEM)],
        [pl.BlockSpec((S, 4 * LANES), lambda b, kp: (b, kp))], [jax.ShapeDtypeStruct((T, SW_Q_COLS), F32)],
        [], {}, (qkn, qkn, proj, bias, sinks), exchange)
    return o, targets


def _swa_bwd(qkn, proj, bias, sinks, do, S, name, exchange=None):
    T, width = proj.shape
    W = WINDOW
    nb = S // W
    scale = SW_DIM ** -0.5

    grid = (2, T // S)

    def body(*refs):
        (q_ref, k_ref, v_ref, bias_ref, sink_ref, do_ref, dp_hbm, dbias_ref, dsink_ref,
         dq_s, dk_s, dv_s, sems), carried = _carried(exchange, refs, 6, 3, 4)
        _start_carried(exchange, carried, grid)
        kp, b = pl.program_id(0), pl.program_id(1)
        row, col = _iota2((W, W), 0), _iota2((W, W), 1)
        mcur = col <= row
        above = col > row
        dk_s[...] = jnp.zeros_like(dk_s)
        dv_s[...] = jnp.zeros_like(dv_s)

        @pl.when(b == 0)
        def _():
            dbias_ref[...] = jnp.zeros_like(dbias_ref)
            dsink_ref[...] = jnp.zeros_like(dsink_ref)

        def blk(n, carry):
            rows = pl.ds(pl.multiple_of(n * W, W), W)
            prow = pl.ds(pl.multiple_of(jnp.maximum(n - 1, 0) * W, W), W)
            mprev = jnp.logical_and(above, n > 0)
            for kvh in range(2):
                ksl = slice(kvh * SW_DIM, (kvh + 1) * SW_DIM)
                kcur, kprev = k_ref[rows, ksl], k_ref[prow, ksl]
                vcur, vprev = v_ref[rows, ksl], v_ref[prow, ksl]
                for g in range(4):
                    hl = kvh * 4 + g
                    qsl = slice(hl * SW_DIM, (hl + 1) * SW_DIM)
                    sink = sink_ref[kp * SW_STEP_HEADS + hl]
                    qb = q_ref[rows, qsl]
                    ep, ec, es, den = _swa_probs(qb, kprev, kcur, bias_ref, hl, sink, mprev, mcur)
                    inv = 1.0 / den
                    pp, pc = ep * inv, ec * inv
                    dob = do_ref[rows, qsl]
                    dpp, dpc = _dot(dob, vprev, 1, 1), _dot(dob, vcur, 1, 1)
                    total = jnp.sum(pp * dpp, axis=1, keepdims=True) + jnp.sum(pc * dpc, axis=1, keepdims=True)
                    dlp = pp * (dpp - total)
                    dlc = pc * (dpc - total)
                    dsink_ref[hl:hl + 1, :] += jnp.zeros((1, LANES), F32) - jnp.sum(es * inv * total)
                    dbias_ref[hl, :, 0:W] += dlp
                    dbias_ref[hl, :, W:2 * W] += dlc
                    dq_s[rows, qsl] = (_dot(dlp, kprev, 1, 0) + _dot(dlc, kcur, 1, 0)) * scale
                    dk_s[prow, ksl] += _dot(dlp, qb, 0, 0) * scale
                    dk_s[rows, ksl] += _dot(dlc, qb, 0, 0) * scale
                    dv_s[prow, ksl] += _dot(pp, dob, 0, 0)
                    dv_s[rows, ksl] += _dot(pc, dob, 0, 0)
            return carry

        lax.fori_loop(0, nb, blk, 0)
        r0 = pl.multiple_of(b * S, S)
        cq = pl.multiple_of(kp * 4 * LANES, LANES)
        ck = pl.multiple_of((SW_K_BLOCK0 + kp) * LANES, LANES)
        cv = pl.multiple_of((SW_V_BLOCK0 + kp) * LANES, LANES)
        copies = [pltpu.make_async_copy(dq_s, dp_hbm.at[pl.ds(r0, S), pl.ds(cq, 4 * LANES)], sems.at[0]),
                  pltpu.make_async_copy(dk_s, dp_hbm.at[pl.ds(r0, S), pl.ds(ck, LANES)], sems.at[1]),
                  pltpu.make_async_copy(dv_s, dp_hbm.at[pl.ds(r0, S), pl.ds(cv, LANES)], sems.at[2])]
        for cp in copies:
            cp.start()
        for cp in copies:
            cp.wait()
        _wait_carried(exchange, carried, grid)

    qspec = pl.BlockSpec((S, 4 * LANES), lambda kp, b: (b, kp))
    outs, targets = _carrier_call(
        body, name, grid,
        [qspec, pl.BlockSpec((S, LANES), lambda kp, b: (b, SW_K_BLOCK0 + kp)),
         pl.BlockSpec((S, LANES), lambda kp, b: (b, SW_V_BLOCK0 + kp)),
         pl.BlockSpec((SW_STEP_HEADS, W, 2 * W), lambda kp, b: (kp, 0, 0)),
         pl.BlockSpec(memory_space=pltpu.SMEM), qspec],
        [ANY, pl.BlockSpec((SW_STEP_HEADS, W, 2 * W), lambda kp, b: (kp, 0, 0)),
         pl.BlockSpec((SW_STEP_HEADS, LANES), lambda kp, b: (kp, 0))],
        [jax.ShapeDtypeStruct((T, width), F32), jax.ShapeDtypeStruct((SW_HEADS, W, 2 * W), F32),
         jax.ShapeDtypeStruct((SW_HEADS, LANES), F32)],
        [pltpu.VMEM((S, 4 * LANES), F32), pltpu.VMEM((S, LANES), F32), pltpu.VMEM((S, LANES), F32),
         pltpu.SemaphoreType.DMA((3,))], {}, (qkn, qkn, proj, bias, sinks, do), exchange)
    return (*outs, targets)


CHIP_FLIPS = ((1, 0), (0, 1), (1, 1))


def _flip(v, f):
    return 1 - v if f else v


class _Exchange:
    def __init__(self, sources, targets, copies):
        self.sources, self.targets = list(sources), list(targets)
        self._copies = copies
        n = len(self.sources)
        self.scratch = [pltpu.SemaphoreType.DMA((n, 3)), pltpu.SemaphoreType.DMA((n, 3)),
                        pltpu.SemaphoreType.DMA((n,))]

    def _descriptors(self, srcs, dsts, sems, chip, peers):
        send, recv, loc = sems
        out = []
        for t, (local, remote) in enumerate(self._copies(srcs, dsts, chip)):
            out.append(pltpu.make_async_copy(local[0], local[1], loc.at[t]))
            for r, (src, dst) in enumerate(remote):
                out.append(pltpu.make_async_remote_copy(src, dst, send.at[t, r], recv.at[t, r],
                                                        device_id=peers[r], device_id_type=MESH))
        return out

    def start(self, srcs, dsts, sems):
        x, y, c = lax.axis_index("x"), lax.axis_index("y"), lax.axis_index("c")
        peers = [(_flip(x, fx), _flip(y, fy), c) for fx, fy in CHIP_FLIPS]
        for chip in range(4):
            @pl.when(2 * x + y == chip)
            def _():
                for cp in self._descriptors(srcs, dsts, sems, chip, peers):
                    cp.start()

    def wait(self, srcs, dsts, sems):
        me = (lax.axis_index("x"), lax.axis_index("y"), lax.axis_index("c"))
        for cp in self._descriptors(srcs, dsts, sems, 0, [me] * 3):
            cp.wait()

    def operands(self):
        return self.sources + self.targets

    def specs(self):
        ns, nt = len(self.sources), len(self.targets)
        return [ANY] * (ns + nt), [ANY] * nt, [jax.ShapeDtypeStruct(t.shape, t.dtype) for t in self.targets]

    def aliases(self, n_in, n_out):
        ns = len(self.sources)
        return {n_in + ns + i: n_out + i for i in range(len(self.targets))}

    def split(self, refs, n_in, n_out, n_scr):
        ns, nt = len(self.sources), len(self.targets)
        o0 = n_in + ns + nt
        s0 = o0 + n_out + nt
        own = list(refs[:n_in]) + list(refs[o0:o0 + n_out]) + list(refs[s0:s0 + n_scr])
        return own, (refs[n_in:n_in + ns], refs[o0 + n_out:o0 + n_out + nt], refs[s0 + n_scr:])


def _carried(exchange, refs, n_in, n_out, n_scr):
    if exchange is None:
        return list(refs), None
    return exchange.split(refs, n_in, n_out, n_scr)


def _grid_edge(grid, last):
    conds = [pl.program_id(d) == (n - 1 if last else 0) for d, n in enumerate(grid)]
    out = conds[0]
    for cnd in conds[1:]:
        out = jnp.logical_and(out, cnd)
    return out


def _start_carried(exchange, parts, grid):
    if parts is not None:
        @pl.when(_grid_edge(grid, False))
        def _():
            exchange.start(*parts)


def _wait_carried(exchange, parts, grid):
    if parts is not None:
        @pl.when(_grid_edge(grid, True))
        def _():
            exchange.wait(*parts)


def _carrier_call(body, name, grid, in_specs, out_specs, out_shape, scratch, aliases, operands, exchange):
    n_out = len(out_shape)
    aliases = dict(aliases)
    if exchange is not None:
        ex_in, ex_out, ex_shape = exchange.specs()
        aliases.update(exchange.aliases(len(in_specs), n_out))
        in_specs, out_specs, out_shape = in_specs + ex_in, out_specs + ex_out, out_shape + ex_shape
        scratch = scratch + exchange.scratch
        operands = list(operands) + exchange.operands()
    outs = pl.pallas_call(body, name=name, grid=grid, in_specs=in_specs, out_specs=out_specs, out_shape=out_shape,
                          scratch_shapes=scratch, input_output_aliases=aliases,
                          compiler_params=_cparams(*["arbitrary"] * len(grid)))(*operands)
    return outs[:n_out], outs[n_out:]


def _exchange_call(exchange, name):
    in_specs, out_specs, out_shape = exchange.specs()

    def body(*refs):
        _, parts = exchange.split(refs, 0, 0, 0)
        exchange.start(*parts)
        exchange.wait(*parts)

    return pl.pallas_call(body, name=name, in_specs=in_specs, out_specs=out_specs, out_shape=out_shape,
                          scratch_shapes=exchange.scratch, input_output_aliases=exchange.aliases(0, 0))(
        *exchange.operands())


def _gather_exchange(shards, fulls, axes, layers):
    sizes = [s.shape[a] for s, a in zip(shards, axes)]

    def copies(srcs, dsts, chip):
        out = []
        for src, full, axis, size, (l0, l1) in zip(srcs, dsts, axes, sizes, layers):
            part = src.at[l0:l1]
            cut = pl.ds(chip * size, size)
            dst = full.at[l0:l1, cut, :] if axis == 1 else full.at[l0:l1, :, cut]
            out.append(((part, dst), [(part, dst)] * 3))
        return out

    return _Exchange(shards, fulls, copies)


def _scatter_exchange(grads, stacks, axes, layers):
    sizes = [g.shape[a - 1] // 4 for g, a in zip(grads, axes)]

    def copies(srcs, dsts, chip):
        out = []
        for g, stack, axis, size, layer in zip(srcs, dsts, axes, sizes, layers):
            def cut(j, g=g, axis=axis, size=size):
                return g.at[pl.ds(j * size, size), :] if axis == 1 else g.at[:, pl.ds(j * size, size)]

            remote = [(cut(chip ^ (2 * fx + fy)), stack.at[r, layer]) for r, (fx, fy) in enumerate(CHIP_FLIPS)]
            out.append(((cut(chip), stack.at[3, layer]), remote))
        return out

    return _Exchange(grads, stacks, copies)


def _swap_with_sibling(parts, name):
    n = len(parts)

    def body(*refs):
        ins, outs = refs[:n], refs[n:2 * n]
        send, recv = refs[2 * n:]
        peer = (lax.axis_index("x"), lax.axis_index("y"), 1 - lax.axis_index("c"))
        copies = [pltpu.make_async_remote_copy(ins[t], outs[t], send.at[t], recv.at[t], device_id=peer,
                                               device_id_type=MESH) for t in range(n)]
        for cp in copies:
            cp.start()
        for cp in copies:
            cp.wait()

    return pl.pallas_call(
        body, name=name, in_specs=[ANY] * n, out_specs=[ANY] * n,
        out_shape=[jax.ShapeDtypeStruct(p.shape, p.dtype) for p in parts],
        scratch_shapes=[pltpu.SemaphoreType.DMA((n,)), pltpu.SemaphoreType.DMA((n,))])(*parts)


def _allreduce_small(v, name):
    R = v.shape[0]
    ND = 8

    def body(v_ref, o_ref, buf, send, recv):
        x, y, c = lax.axis_index("x"), lax.axis_index("y"), lax.axis_index("c")
        me = 4 * x + 2 * y + c
        copies = []
        for d in range(1, ND):
            peer = (_flip(x, d >> 2 & 1), _flip(y, d >> 1 & 1), _flip(c, d & 1))
            copies.append(pltpu.make_async_remote_copy(v_ref, buf.at[me], send.at[d], recv.at[me], device_id=peer,
                                                       device_id_type=MESH))
        for cp in copies:
            cp.start()
        buf[pl.ds(me, 1)] = v_ref[...][None]
        for k in range(ND):
            @pl.when(me != k)
            def _():
                pltpu.make_async_remote_copy(v_ref, buf.at[k], send.at[0], recv.at[k], device_id=(x, y, c),
                                             device_id_type=MESH).wait_recv()
        for cp in copies:
            cp.wait_send()
        total = buf[0]
        for k in range(1, ND):
            total = total + buf[k]
        o_ref[...] = total

    vm = pl.BlockSpec(memory_space=pltpu.VMEM)
    return pl.pallas_call(
        body, name=name, in_specs=[vm], out_specs=vm, out_shape=jax.ShapeDtypeStruct((R, LANES), F32),
        scratch_shapes=[pltpu.VMEM((ND, R, LANES), F32), pltpu.SemaphoreType.DMA((ND,)),
                        pltpu.SemaphoreType.DMA((ND,))],
        compiler_params=pltpu.CompilerParams(vmem_limit_bytes=VMEM_LIMIT))(v)


def _tile2(R, Cn):
    tc = _pick(Cn, 2048)
    tr = R
    for cand in (256, 128, 64, 32, 16, 8):
        if R % cand == 0:
            tr = cand
            break
    return tr, tc


def _sum4(stack, name):
    _, R, Cn = stack.shape
    tr, tc = _tile2(R, Cn)

    def body(s_ref, o_ref):
        o_ref[...] = ((s_ref[0].astype(F32) + s_ref[1].astype(F32)) + s_ref[2].astype(F32)) + s_ref[3].astype(F32)

    return pl.pallas_call(
        body, name=name, grid=(R // tr, Cn // tc), in_specs=[pl.BlockSpec((4, tr, tc), lambda i, j: (0, i, j))],
        out_specs=pl.BlockSpec((tr, tc), lambda i, j: (i, j)), out_shape=jax.ShapeDtypeStruct((R, Cn), F32),
        compiler_params=_cparams("parallel", "parallel"))(stack)


def _adamw(w, m, v, g_parts, name):
    R, Cn = w.shape
    tr, tc = _tile2(R, Cn)
    npart = len(g_parts)
    c1 = 1.0 / (1.0 - ADAM_B1 ** ADAM_STEP)
    c2 = 1.0 / (1.0 - ADAM_B2 ** ADAM_STEP)

    def body(*refs):
        w_ref, m_ref, v_ref = refs[:3]
        g_refs = refs[3:3 + npart]
        g_out, d_out, m_out, v_out = refs[3 + npart:]
        g = g_refs[0][...]
        for r in g_refs[1:]:
            g = g + r[...]
        mn = ADAM_B1 * m_ref[...] + (1.0 - ADAM_B1) * g
        vn = ADAM_B2 * v_ref[...] + (1.0 - ADAM_B2) * (g * g)
        g_out[...] = g
        m_out[...] = mn
        v_out[...] = vn
        d_out[...] = -ADAM_LR * ((mn * c1) / (jnp.sqrt(vn * c2) + ADAM_EPS) + ADAM_WD * w_ref[...])

    spec = pl.BlockSpec((tr, tc), lambda i, j: (i, j))
    return pl.pallas_call(
        body, name=name, grid=(R // tr, Cn // tc), in_specs=[spec] * (3 + npart), out_specs=[spec] * 4,
        out_shape=[jax.ShapeDtypeStruct((R, Cn), F32)] * 4,
        compiler_params=_cparams("parallel", "parallel"))(w, m, v, *g_parts)


SHARDED = (("ab_w_in", 2), ("ab_w_out", 1), ("c_w_in", 2), ("c_w_out", 1), ("ffn_up", 2), ("ffn_conv", 2),
           ("ffn_down", 1), ("ple_gate", 1), ("ple_proj", 2))
SMALL = ("mix_norm", "hg_lb_logits", "hg_out_norm", "q_norm", "k_norm", "sinks", "rel_bias", "ffn_norm",
         "ffn_conv_b", "ple_norm")
WEIGHTS = ("mix_norm", "ab_w_in", "hg_lb_logits", "hg_out_norm", "ab_w_out", "c_w_in", "q_norm", "k_norm", "sinks",
           "rel_bias", "c_w_out", "ffn_norm", "ffn_up", "ffn_conv", "ffn_conv_b", "ffn_down", "ple_norm", "ple_gate",
           "ple_proj")
PACK_ALIGN = 8 * LANES


def _pack(arrs):
    pieces = []
    for a in arrs:
        flat = a.reshape(-1)
        pad = -flat.shape[0] % PACK_ALIGN
        pieces.append(jnp.pad(flat, (0, pad)).reshape(-1, LANES))
    return jnp.concatenate(pieces, axis=0)


def _unpack(packed, like):
    out, r = [], 0
    for a in like:
        size = int(np.prod(a.shape))
        rows = (size + PACK_ALIGN - 1) // PACK_ALIGN * 8
        out.append(packed[r:r + rows].reshape(-1)[:size].reshape(a.shape))
        r += rows
    return out


def _family_range(name, lo, hi):
    if name.startswith("ab_"):
        idx = [i // 2 for i in range(lo, hi) if i % 2 == 0]
    elif name.startswith("c_"):
        idx = [i // 2 for i in range(lo, hi) if i % 2 == 1]
    else:
        idx = list(range(lo, hi))
    return (idx[0], idx[-1] + 1) if idx else None


W_IN = ("ab_w_in", "c_w_in")
REST = tuple(k for k, _ in SHARDED if k not in W_IN)
GATHER_PLAN = {
    "gather_first": [(0, W_IN)],
    "sb_fwd_0": [(0, REST), (1, None)],
    "hgrn_fwd_0": [(2, ("ab_w_in", "ab_w_out", "ffn_conv", "ffn_down", "ple_gate", "ple_proj"))],
    "swa_fwd_1": [(2, ("ffn_up",))],
    "sb_fwd_2": [(3, None)],
}
SCATTER_PLAN = {
    "swa_bwd_3": [(3, REST)],
    "sb_bwd_2": [(3, W_IN), (2, REST)],
    "swa_bwd_1": [(2, W_IN), (1, REST)],
    "sb_bwd_0": [(1, W_IN), (0, REST)],
    "scatter_last": [(0, W_IN)],
}


class _StepExchanges:
    def __init__(self, shards, W):
        self.shards, self.W = shards, W
        self.stacks = {}
        for k, axis in SHARDED:
            shp = shards[k].shape
            self.W[k] = lax.empty(tuple(4 * d if i == axis else d for i, d in enumerate(shp)), shards[k].dtype)
            self.stacks[k] = lax.empty((4,) + shp, shards[k].dtype)

    @staticmethod
    def _select(plan):
        idx = {}
        for layer, fams in plan:
            for k, _ in SHARDED:
                r = _family_range(k, layer, layer + 1)
                if r is not None and (fams is None or k in fams):
                    idx.setdefault(k, []).append(r[0])
        return [(k, axis, sorted(idx[k])) for k, axis in SHARDED if k in idx]

    def gather(self, call):
        sel = self._select(GATHER_PLAN.get(call, ()))
        if not sel:
            return None, None
        for _, _, ii in sel:
            assert ii == list(range(ii[0], ii[-1] + 1)), "one copy per family takes a contiguous layer range"
        names = [k for k, _, _ in sel]
        ex = _gather_exchange([self.shards[k] for k in names], [self.W[k] for k in names],
                              [a for _, a, _ in sel], [(ii[0], ii[-1] + 1) for _, _, ii in sel])
        return ex, (self.W, names)

    def scatter(self, call, G):
        sel = self._select(SCATTER_PLAN.get(call, ()))
        if not sel:
            return None, None
        assert all(len(ii) == 1 for _, _, ii in sel)
        names = [k for k, _, _ in sel]
        ex = _scatter_exchange([G[k][ii[0]] for k, _, ii in sel], [self.stacks[k] for k in names],
                               [a for _, a, _ in sel], [ii[0] for _, _, ii in sel])
        return ex, (self.stacks, names)

    @staticmethod
    def adopt(where, targets):
        if where is not None:
            book, names = where
            for k, t in zip(names, targets):
                book[k] = t


def _forward_backward(x, p, target, W, S, exchanges=None):
    T = x.shape[0]
    depth = p.shape[0]
    lb = _lower_bound_fwd(W["hg_lb_logits"], "lower_bound_fwd")
    bias = _bias_build(W["rel_bias"], "bias_build")
    qk_gain = jnp.concatenate([jnp.tile(W["q_norm"], (1, SW_HEADS)), jnp.tile(W["k_norm"], (1, SW_KV))], axis=1)

    def mm(a, wname, layer, mode, name, **kw):
        return _matmul(a, W[wname], mode, name, b_layer=layer, **kw)

    def gathering(call):
        return exchanges.gather(call) if exchanges else (None, None)

    def scattering(call):
        return exchanges.scatter(call, G) if exchanges else (None, None)

    saved = []
    h = x
    for i in range(depth):
        j = i // 2
        s = {"h0": h}
        s["hn"] = _rmsnorm_fwd(h, W["mix_norm"][i:i + 1], f"mix_norm_fwd_{i}")
        if i % 2 == 0:
            s["proj"] = mm(s["hn"], "ab_w_in", j, "nn", f"ab_in_{i}")
            ex, where = gathering(f"sb_fwd_{i}")
            cat, s["sb_tot"], arrived = _sb_fwd(s["proj"], S, f"sb_fwd_{i}", ex)
            _StepExchanges.adopt(where, arrived)
            ex, where = gathering(f"hgrn_fwd_{i}")
            s["cat"], s["oraw"], s["states"], arrived = _hgrn_fwd(s["proj"], cat, lb[j:j + 1],
                                                                  W["hg_out_norm"][j:j + 1], S, f"hgrn_fwd_{i}", ex)
            _StepExchanges.adopt(where, arrived)
            h = mm(s["cat"], "ab_w_out", j, "nn", f"ab_out_{i}", res=h)
        else:
            s["proj"] = mm(s["hn"], "c_w_in", j, "nn", f"c_in_{i}")
            s["qkn"] = _headnorm_fwd(s["proj"], qk_gain[j:j + 1], f"qk_norm_fwd_{i}")
            ex, where = gathering(f"swa_fwd_{i}")
            s["o"], arrived = _swa_fwd(s["qkn"], s["proj"], bias, W["sinks"][j], S, f"swa_fwd_{i}", ex)
            _StepExchanges.adopt(where, arrived)
            h = mm(s["o"], "c_w_out", j, "nn", f"c_out_{i}", res=h)
        s["h1"] = h
        s["hn2"] = _rmsnorm_fwd(h, W["ffn_norm"][i:i + 1], f"ffn_norm_fwd_{i}")
        s["u"] = mm(s["hn2"], "ffn_up", i, "nn", f"ffn_up_{i}")
        s["a"] = _convglu_fwd(s["u"], W["ffn_conv"][i], W["ffn_conv_b"][i:i + 1], S, f"convglu_fwd_{i}")
        h = mm(s["a"], "ffn_down", i, "nn", f"ffn_down_{i}", res=h)
        s["h2"] = h
        s["hn3"] = _rmsnorm_fwd(h, W["ple_norm"][i:i + 1], f"ple_norm_fwd_{i}")
        s["z"] = mm(s["hn3"], "ple_gate", i, "nn", f"ple_gate_{i}")
        s["pp"] = mm(p, "ple_proj", i, "nn", f"ple_proj_{i}", a_layer=i)
        h = _ple_fwd(h, s["z"], s["pp"], f"ple_fwd_{i}")
        saved.append(s)

    loss, dh = _loss_fwd_bwd(h, target, "loss")

    G = {k: [None] * depth for k in ("mix_norm", "ffn_norm", "ple_norm", "ffn_up", "ffn_conv", "ffn_conv_b",
                                     "ffn_down", "ple_gate", "ple_proj")}
    for k in ("ab_w_in", "ab_w_out", "c_w_in", "c_w_out", "hg_out_norm", "q_norm", "k_norm", "sinks", "lb"):
        G[k] = [None] * (depth // 2)
    dbias_total = None
    for i in reversed(range(depth)):
        j = i // 2
        s = saved[i]
        dz, dpp = _ple_bwd(dh, s["z"], s["pp"], f"ple_bwd_{i}")
        G["ple_proj"][i] = _matmul(p, dpp, "tn", f"d_ple_proj_{i}", out_dtype=BF16, a_layer=i)
        G["ple_gate"][i] = _matmul(s["hn3"], dz, "tn", f"d_ple_gate_{i}", out_dtype=BF16)
        dhn = mm(dz, "ple_gate", i, "nt", f"d_hn3_{i}")
        dh, G["ple_norm"][i] = _rmsnorm_bwd(s["h2"], W["ple_norm"][i:i + 1], dhn, dh, f"ple_norm_bwd_{i}")

        half_ff = D_FF // 2
        da = mm(dh, "ffn_down", i, "nt", f"d_a_{i}", tiles=(min(T, 1024), half_ff, D_MODEL))
        G["ffn_down"][i] = _matmul(s["a"], dh, "tn", f"d_ffn_down_{i}", out_dtype=BF16,
                                   tiles=(half_ff, 512, min(T, 2048)))
        du, dcw, dcb = _convglu_bwd(s["u"], da, W["ffn_conv"][i], W["ffn_conv_b"][i:i + 1], S, f"convglu_bwd_{i}")
        G["ffn_conv"][i] = jnp.swapaxes(dcw, 0, 1).reshape(3, 2 * D_FF)
        G["ffn_conv_b"][i] = dcb.reshape(1, 2 * D_FF)
        G["ffn_up"][i] = _matmul(s["hn2"], du, "tn", f"d_ffn_up_{i}", out_dtype=BF16,
                                 tiles=(D_MODEL, half_ff, min(T, 1024)))
        dhn = mm(du, "ffn_up", i, "nt", f"d_hn2_{i}")
        dh, G["ffn_norm"][i] = _rmsnorm_bwd(s["h1"], W["ffn_norm"][i:i + 1], dhn, dh, f"ffn_norm_bwd_{i}")

        if i % 2 == 0:
            dcat = mm(dh, "ab_w_out", j, "nt", f"d_cat_{i}")
            G["ab_w_out"][j] = _matmul(s["cat"], dh, "tn", f"d_ab_out_{i}", out_dtype=BF16)
            ex, where = scattering(f"sb_bwd_{i}")
            dproj, sent = _sb_bwd(s["proj"], s["sb_tot"], dcat, S, f"sb_bwd_{i}", ex)
            dproj, G["lb"][j], G["hg_out_norm"][j] = _hgrn_bwd(s["proj"], s["oraw"], dcat, s["states"], lb[j:j + 1],
                                                               W["hg_out_norm"][j:j + 1], dproj, S, f"hgrn_bwd_{i}")
            G["ab_w_in"][j] = _matmul(s["hn"], dproj, "tn", f"d_ab_in_{i}", out_dtype=BF16)
            dhn = mm(dproj, "ab_w_in", j, "nt", f"d_hn_{i}")
        else:
            do = mm(dh, "c_w_out", j, "nt", f"d_o_{i}")
            G["c_w_out"][j] = _matmul(s["o"], dh, "tn", f"d_c_out_{i}", out_dtype=BF16)
            ex, where = scattering(f"swa_bwd_{i}")
            dqkv, dbias, dsink, sent = _swa_bwd(s["qkn"], s["proj"], bias, W["sinks"][j], do, S, f"swa_bwd_{i}", ex)
            dbias_total = dbias if dbias_total is None else dbias_total + dbias
            G["sinks"][j] = dsink[:, 0]
            dproj, dgain = _headnorm_bwd(s["proj"], qk_gain[j:j + 1], dqkv, f"qk_norm_bwd_{i}")
            G["q_norm"][j] = dgain[0, :SW_Q_COLS].reshape(SW_HEADS, SW_DIM).sum(axis=0)
            G["k_norm"][j] = dgain[0, SW_Q_COLS:].reshape(SW_KV, SW_DIM).sum(axis=0)
            G["c_w_in"][j] = _matmul(s["hn"], dproj, "tn", f"d_c_in_{i}", out_dtype=BF16)
            dhn = mm(dproj, "c_w_in", j, "nt", f"d_hn_{i}")
        _StepExchanges.adopt(where, sent)
        dh, G["mix_norm"][i] = _rmsnorm_bwd(s["h0"], W["mix_norm"][i:i + 1], dhn, dh, f"mix_norm_bwd_{i}")

    grads = {k: G[k] for k, _ in SHARDED}
    for k in ("q_norm", "k_norm", "sinks"):
        grads[k] = jnp.stack(G[k])
    for k in ("mix_norm", "ffn_norm", "ple_norm", "ffn_conv_b", "hg_out_norm"):
        grads[k] = jnp.concatenate(G[k], axis=0)
    grads["hg_lb_logits"] = _lower_bound_bwd(W["hg_lb_logits"], jnp.concatenate(G["lb"], axis=0), "lower_bound_bwd")
    grads["rel_bias"] = _bias_reduce(dbias_total, "bias_reduce")[:, 0, :N_BUCKETS].T
    return loss, dh, grads


def kernel(x, p, mix_norm, ab_w_in, hg_lb_logits, hg_out_norm, ab_w_out, c_w_in, q_norm, k_norm, sinks, rel_bias, c_w_out, ffn_norm, ffn_up, ffn_conv, ffn_conv_b, ffn_down, ple_norm, ple_gate, ple_proj, loss_target, m_mix_norm, m_ab_w_in, m_hg_lb_logits, m_hg_out_norm, m_ab_w_out, m_c_w_in, m_q_norm, m_k_norm, m_sinks, m_rel_bias, m_c_w_out, m_ffn_norm, m_ffn_up, m_ffn_conv, m_ffn_conv_b, m_ffn_down, m_ple_norm, m_ple_gate, m_ple_proj, v_mix_norm, v_ab_w_in, v_hg_lb_logits, v_hg_out_norm, v_ab_w_out, v_c_w_in, v_q_norm, v_k_norm, v_sinks, v_rel_bias, v_c_w_out, v_ffn_norm, v_ffn_up, v_ffn_conv, v_ffn_conv_b, v_ffn_down, v_ple_norm, v_ple_gate, v_ple_proj):
    args = dict(locals())
    w = {k: args[k] for k in WEIGHTS}
    m = {k: args["m_" + k] for k in WEIGHTS}
    v = {k: args["v_" + k] for k in WEIGHTS}
    B, S, Dm = x.shape
    T = B * S
    names = [k for k, _ in SHARDED]

    W = {k: w[k] for k in SMALL}
    exchanges = _StepExchanges({k: w[k].astype(F32 if k == "ffn_conv" else BF16) for k in names}, W)
    first, where = exchanges.gather("gather_first")
    exchanges.adopt(where, _exchange_call(first, "gather_first"))

    loss, dx, grads = _forward_backward(x.reshape(T, Dm), p.reshape(p.shape[0], T, p.shape[-1]),
                                        loss_target.reshape(T, Dm), W, S, exchanges)
    loss = lax.psum(loss[0, 0], ("x", "y", "c"))

    last, where = exchanges.scatter("scatter_last", grads)
    exchanges.adopt(where, _exchange_call(last, "scatter_last"))
    stacks = [exchanges.stacks[k] for k in names]
    partial = [_sum4(st.reshape(4, -1, st.shape[-1]), f"sum_chips_{k}") for k, st in zip(names, stacks)]
    other = _swap_with_sibling(partial, "swap_core_sums")
    small_sum = _allreduce_small(_pack([grads[k] for k in SMALL]), "allreduce_small")

    out_g, out_d, out_m, out_v = {}, {}, {}, {}
    for k, mine, theirs in zip(names, partial, other):
        shp = w[k].shape
        r = [a.reshape(shp) for a in _adamw(w[k].reshape(mine.shape), m[k].reshape(mine.shape),
                                            v[k].reshape(mine.shape), [mine, theirs], f"adamw_{k}")]
        out_g[k], out_d[k], out_m[k], out_v[k] = r
    sm = _adamw(_pack([w[k] for k in SMALL]), _pack([m[k] for k in SMALL]), _pack([v[k] for k in SMALL]),
                [small_sum], "adamw_small")
    like = [w[k] for k in SMALL]
    for dst, packed in zip((out_g, out_d, out_m, out_v), sm):
        for k, a in zip(SMALL, _unpack(packed, like)):
            dst[k] = a

    return (loss, dx.reshape(B, S, Dm), *[out_g[k] for k in WEIGHTS], *[out_d[k] for k in WEIGHTS],
            *[out_m[k] for k in WEIGHTS], *[out_v[k] for k in WEIGHTS])
```

```python
import math

import numpy as np
import jax
import jax.numpy as jnp
from jax import lax
from jax.experimental import pallas as pl
from jax.experimental.pallas import tpu as pltpu

F32 = jnp.float32
BF16 = jnp.bfloat16
MESH = pl.DeviceIdType.MESH
ANY = pl.BlockSpec(memory_space=pl.ANY)

D_MODEL = 1024
EPS = 1e-6
SB_HEADS, SB_DIM = 8, 64
HG_HEADS, HG_DK = 4, 128
HG_CHUNK = 32
HG_STEP = 2
HG_TRIP = 2
SW_HEADS, SW_KV, SW_DIM, WINDOW = 16, 4, 64, 128
N_BUCKETS, MAX_DISTANCE = 32, 128
D_FF = 2816
ATT_BLOCK = 128
SB_QBLOCK = 256
LANES = 128
NEG = -1e30

ADAM_LR, ADAM_B1, ADAM_B2, ADAM_EPS, ADAM_WD, ADAM_STEP = 0.001, 0.9, 0.999, 1e-08, 0.01, 10

VMEM_LIMIT = 56 * 1024 * 1024


def _cparams(*sem):
    return pltpu.CompilerParams(dimension_semantics=sem, vmem_limit_bytes=VMEM_LIMIT)


def _pick(n, cap):
    if n <= cap:
        return n
    best = None
    for d in range(LANES, cap + 1, LANES):
        if n % d == 0:
            best = d
    assert best is not None, (n, cap)
    return best


def _dot(a, b, ca, cb):
    return lax.dot_general(a.astype(BF16), b.astype(BF16), (((ca,), (cb,)), ((), ())),
                           preferred_element_type=F32)


def _split(x, terms):
    parts = []
    for _ in range(terms):
        hi = x.astype(BF16)
        parts.append(hi)
        x = x - hi.astype(F32)
    return parts


def _dot_exact_l(x, m, terms=2):
    out = None
    for p in _split(x, terms):
        t = lax.dot_general(p, m, (((1,), (0,)), ((), ())), preferred_element_type=F32)
        out = t if out is None else out + t
    return out


def _dot_exact_r(m, x, terms=3, cm=1):
    out = None
    for p in _split(x, terms):
        t = lax.dot_general(m, p, (((cm,), (0,)), ((), ())), preferred_element_type=F32)
        out = t if out is None else out + t
    return out


def _sig(x):
    return 1.0 / (1.0 + jnp.exp(-x))


def _iota2(shape, dim):
    return lax.broadcasted_iota(jnp.int32, shape, dim)


def _operand_spec(arr, layer, blk, index):
    if arr.ndim == 2:
        return pl.BlockSpec(blk, index)
    if layer is not None:
        return pl.BlockSpec((None,) + blk, lambda i, j, k: (layer,) + index(i, j, k))
    per_half = arr.shape[2] // blk[1]

    def halves(i, j, k):
        r, c = index(i, j, k)
        return (c // per_half, r, c % per_half)

    return pl.BlockSpec((None,) + blk, halves)


def _matmul(a, b, mode, name, out_dtype=F32, res=None, a_layer=None, b_layer=None, tiles=None, exchange=None):
    def dims(arr, layer):
        if arr.ndim == 2:
            return arr.shape
        return arr.shape[1:] if layer is not None else (arr.shape[1], 2 * arr.shape[2])

    (a0, a1), (b0, b1) = dims(a, a_layer), dims(b, b_layer)
    if mode == "nn":
        M, K, N = a0, a1, b1
    elif mode == "nt":
        M, K, N = a0, a1, b0
    else:
        K, M, N = a0, a1, b1
    cap_m, cap_n, cap_k = 1024, 1024, (1024 if mode == "tn" else 2048)
    tm, tn, tk = _pick(M, cap_m), _pick(N, cap_n), _pick(K, cap_k)
    if a.ndim == 3 and a_layer is None:
        if mode == "tn":
            tm = _pick(a.shape[2], cap_m)
        else:
            tk = _pick(a.shape[2], cap_k)
    if b.ndim == 3 and b_layer is None:
        if mode == "nt":
            tk = _pick(b.shape[2], cap_k)
        else:
            tn = _pick(b.shape[2], cap_n)
    if tiles is not None:
        tm, tn, tk = tiles
    assert M % tm == 0 and N % tn == 0 and K % tk == 0, (name, M, N, K, tm, tn, tk)
    nk = K // tk
    if mode == "tn":
        a_spec = _operand_spec(a, a_layer, (tk, tm), lambda i, j, k: (k, i))
    else:
        a_spec = _operand_spec(a, a_layer, (tm, tk), lambda i, j, k: (i, k))
    if mode == "nt":
        b_spec = _operand_spec(b, b_layer, (tn, tk), lambda i, j, k: (j, k))
    else:
        b_spec = _operand_spec(b, b_layer, (tk, tn), lambda i, j, k: (k, j))
    ca, cb = {"nn": (1, 0), "nt": (1, 1), "tn": (0, 0)}[mode]
    o_spec = pl.BlockSpec((tm, tn), lambda i, j, k: (i, j))

    grid = (M // tm, N // tn, nk)
    n_in = 2 if res is None else 3

    def body(*refs):
        own, carried = _carried(exchange, refs, n_in, 1, 1)
        _start_carried(exchange, carried, grid)
        if res is None:
            a_ref, b_ref, o_ref, acc = own
        else:
            a_ref, b_ref, r_ref, o_ref, acc = own
        k = pl.program_id(2)

        @pl.when(k == 0)
        def _():
            acc[...] = jnp.zeros_like(acc)

        acc[...] += _dot(a_ref[...], b_ref[...], ca, cb)

        @pl.when(k == nk - 1)
        def _():
            r = acc[...]
            if res is not None:
                r = r + r_ref[...]
            o_ref[...] = r.astype(out_dtype)

        _wait_carried(exchange, carried, grid)

    ins = [a, b] + ([] if res is None else [res])
    in_specs = [a_spec, b_spec] + ([] if res is None else [o_spec])
    if exchange is None:
        return pl.pallas_call(
            body, name=name, grid=grid, in_specs=in_specs, out_specs=o_spec,
            out_shape=jax.ShapeDtypeStruct((M, N), out_dtype), scratch_shapes=[pltpu.VMEM((tm, tn), F32)],
            compiler_params=_cparams("parallel", "parallel", "arbitrary"))(*ins), ()
    (out,), targets = _carrier_call(body, name, grid, in_specs, [o_spec], [jax.ShapeDtypeStruct((M, N), out_dtype)],
                                    [pltpu.VMEM((tm, tn), F32)], {}, ins, exchange)
    return out, targets


ROW_TILE = 512


def _row_spec(width):
    return pl.BlockSpec((ROW_TILE, width), lambda i: (i, 0))


def _vec_spec(width):
    return pl.BlockSpec((1, width), lambda i: (0, 0))


def _rmsnorm_fwd(h, g, name):
    T, Dm = h.shape

    def body(h_ref, g_ref, o_ref):
        x = h_ref[...]
        r = lax.rsqrt(jnp.mean(x * x, axis=1, keepdims=True) + EPS)
        o_ref[...] = (x * r * g_ref[...]).astype(BF16)

    return pl.pallas_call(
        body, name=name, grid=(T // ROW_TILE,), in_specs=[_row_spec(Dm), _vec_spec(Dm)],
        out_specs=_row_spec(Dm), out_shape=jax.ShapeDtypeStruct((T, Dm), BF16),
        compiler_params=_cparams("parallel"))(h, g)


def _rmsnorm_bwd(h, g, dhn, dres, name):
    T, Dm = h.shape

    def body(h_ref, g_ref, dy_ref, dr_ref, dh_ref, dg_ref):
        i = pl.program_id(0)
        x = h_ref[...]
        dy = dy_ref[...]
        r = lax.rsqrt(jnp.mean(x * x, axis=1, keepdims=True) + EPS)
        gdy = dy * g_ref[...]
        m = jnp.mean(x * gdy, axis=1, keepdims=True)
        dh_ref[...] = dr_ref[...] + r * gdy - x * (r * r * r * m)
        part = jnp.sum(dy * x * r, axis=0, keepdims=True)

        @pl.when(i == 0)
        def _():
            dg_ref[...] = part

        @pl.when(i > 0)
        def _():
            dg_ref[...] += part

    return pl.pallas_call(
        body, name=name, grid=(T // ROW_TILE,),
        in_specs=[_row_spec(Dm), _vec_spec(Dm), _row_spec(Dm), _row_spec(Dm)],
        out_specs=[_row_spec(Dm), _vec_spec(Dm)],
        out_shape=[jax.ShapeDtypeStruct((T, Dm), F32), jax.ShapeDtypeStruct((1, Dm), F32)],
        compiler_params=_cparams("arbitrary"))(h, g, dhn, dres)


def _ple_fwd(h, z, pp, name):
    T, Dm = h.shape

    def body(h_ref, z_ref, p_ref, o_ref):
        o_ref[...] = h_ref[...] + _sig(z_ref[...]) * p_ref[...]

    return pl.pallas_call(
        body, name=name, grid=(T // ROW_TILE,), in_specs=[_row_spec(Dm)] * 3, out_specs=_row_spec(Dm),
        out_shape=jax.ShapeDtypeStruct((T, Dm), F32), compiler_params=_cparams("parallel"))(h, z, pp)


def _ple_bwd(dh, z, pp, name):
    T, Dm = dh.shape

    def body(dh_ref, z_ref, p_ref, dz_ref, dp_ref):
        s = _sig(z_ref[...])
        d = dh_ref[...]
        dz_ref[...] = d * p_ref[...] * s * (1.0 - s)
        dp_ref[...] = d * s

    return pl.pallas_call(
        body, name=name, grid=(T // ROW_TILE,), in_specs=[_row_spec(Dm)] * 3, out_specs=[_row_spec(Dm)] * 2,
        out_shape=[jax.ShapeDtypeStruct((T, Dm), F32)] * 2, compiler_params=_cparams("parallel"))(dh, z, pp)


def _loss_fwd_bwd(y, target, name):
    T, Dm = y.shape

    def body(y_ref, t_ref, l_ref, d_ref):
        i = pl.program_id(0)
        e = y_ref[...] - t_ref[...]
        d_ref[...] = e * (1.0 / Dm)
        part = jnp.full((8, LANES), 0.5 / Dm, F32) * jnp.sum(e * e)

        @pl.when(i == 0)
        def _():
            l_ref[...] = part

        @pl.when(i > 0)
        def _():
            l_ref[...] += part

    return pl.pallas_call(
        body, name=name, grid=(T // ROW_TILE,), in_specs=[_row_spec(Dm)] * 2,
        out_specs=[pl.BlockSpec((8, LANES), lambda i: (0, 0)), _row_spec(Dm)],
        out_shape=[jax.ShapeDtypeStruct((8, LANES), F32), jax.ShapeDtypeStruct((T, Dm), F32)],
        compiler_params=_cparams("arbitrary"))(y, target)


def _head_mean_matrix():
    r = _iota2((LANES, LANES), 0) >= SW_DIM
    c = _iota2((LANES, LANES), 1) >= SW_DIM
    return jnp.where(r == c, 1.0 / SW_DIM, 0.0).astype(BF16)


def _headnorm_fwd(x, g_lane, name):
    T = x.shape[0]
    C = g_lane.shape[1]

    def body(x_ref, g_ref, y_ref):
        xv = x_ref[...]
        ms = _dot_exact_l(xv * xv, _head_mean_matrix())
        y_ref[...] = xv * lax.rsqrt(ms + EPS) * g_ref[...]

    spec = pl.BlockSpec((ROW_TILE, LANES), lambda j, i: (i, j))
    return pl.pallas_call(
        body, name=name, grid=(C // LANES, T // ROW_TILE),
        in_specs=[spec, pl.BlockSpec((1, LANES), lambda j, i: (0, j))], out_specs=spec,
        out_shape=jax.ShapeDtypeStruct((T, C), F32), compiler_params=_cparams("parallel", "parallel"))(x, g_lane)


def _headnorm_bwd(x, g_lane, dy_full, name):
    T = x.shape[0]
    C = g_lane.shape[1]

    def body(x_ref, g_ref, dy_ref, dx_ref, dg_ref):
        i = pl.program_id(1)
        xv = x_ref[...]
        dy = dy_ref[...]
        bd = _head_mean_matrix()
        r = lax.rsqrt(_dot_exact_l(xv * xv, bd) + EPS)
        gdy = dy * g_ref[...]
        m = _dot_exact_l(xv * gdy, bd)
        dx_ref[...] = r * gdy - xv * (r * r * r * m)
        part = jnp.sum(dy * xv * r, axis=0, keepdims=True)

        @pl.when(i == 0)
        def _():
            dg_ref[...] = part

        @pl.when(i > 0)
        def _():
            dg_ref[...] += part

    spec = pl.BlockSpec((ROW_TILE, LANES), lambda j, i: (i, j))
    vspec = pl.BlockSpec((1, LANES), lambda j, i: (0, j))
    return pl.pallas_call(
        body, name=name, grid=(C // LANES, T // ROW_TILE), in_specs=[spec, vspec, spec],
        out_specs=[spec, vspec],
        out_shape=[jax.ShapeDtypeStruct(dy_full.shape, F32), jax.ShapeDtypeStruct((1, C), F32)],
        input_output_aliases={2: 0}, compiler_params=_cparams("parallel", "arbitrary"))(x, g_lane, dy_full)


CONV_TILE = 128


def _shift_down(x, k):
    rows = _iota2(x.shape, 0)
    return jnp.where(rows >= k, pltpu.roll(x, k, 0), 0.0)


def _shift_up(x, k):
    n = x.shape[0]
    rows = _iota2(x.shape, 0)
    return jnp.where(rows < n - k, pltpu.roll(x, n - k, 0), 0.0)


def _conv3(u, w_ref, b_ref):
    return w_ref[2:3, :] * u + w_ref[1:2, :] * _shift_down(u, 1) + w_ref[0:1, :] * _shift_down(u, 2) + b_ref[...]


def _convglu_fwd(u, cw, cb, S, name):
    T = u.shape[0]
    nf = D_FF // CONV_TILE

    def body(ug_ref, uu_ref, wg_ref, wu_ref, bg_ref, bu_ref, a_ref):
        yg = _conv3(ug_ref[...], wg_ref, bg_ref)
        yu = _conv3(uu_ref[...], wu_ref, bu_ref)
        a_ref[...] = (yg * _sig(yg) * yu).astype(BF16)

    def blk(rows, off):
        return pl.BlockSpec((rows, CONV_TILE), (lambda b, j: (b, j + off)) if rows == S else (lambda b, j: (0, j + off)))

    return pl.pallas_call(
        body, name=name, grid=(T // S, nf),
        in_specs=[blk(S, 0), blk(S, nf), blk(3, 0), blk(3, nf), blk(1, 0), blk(1, nf)],
        out_specs=blk(S, 0), out_shape=jax.ShapeDtypeStruct((T, D_FF), BF16),
        compiler_params=_cparams("parallel", "parallel"))(u, u, cw, cw, cb, cb)


def _convglu_bwd(u, da, cw, cb, S, name, exchange=None):
    T = u.shape[0]
    nf = D_FF // CONV_TILE

    grid = (nf, T // S)

    def body(*refs):
        (ug_ref, uu_ref, da_ref, wg_ref, wu_ref, bg_ref, bu_ref, du_ref, dw_ref, db_ref), carried = _carried(
            exchange, refs, 7, 3, 0)
        _start_carried(exchange, carried, grid)
        b = pl.program_id(1)
        ug, uu = ug_ref[...], uu_ref[...]
        yg = _conv3(ug, wg_ref, bg_ref)
        yu = _conv3(uu, wu_ref, bu_ref)
        s = _sig(yg)
        da_v = da_ref[...]
        for half, (uv, w_ref, dy) in enumerate(((ug, wg_ref, da_v * yu * (s * (1.0 + yg * (1.0 - s)))),
                                                (uu, wu_ref, da_v * yg * s))):
            du_ref[half] = w_ref[2:3, :] * dy + w_ref[1:2, :] * _shift_up(dy, 1) + w_ref[0:1, :] * _shift_up(dy, 2)
            dws = [jnp.sum(dy * _shift_down(uv, 2), axis=0, keepdims=True),
                   jnp.sum(dy * _shift_down(uv, 1), axis=0, keepdims=True),
                   jnp.sum(dy * uv, axis=0, keepdims=True)]
            dbv = jnp.sum(dy, axis=0, keepdims=True)

            @pl.when(b == 0)
            def _():
                for k in range(3):
                    dw_ref[half, k:k + 1, :] = dws[k]
                db_ref[half] = dbv

            @pl.when(b > 0)
            def _():
                for k in range(3):
                    dw_ref[half, k:k + 1, :] += dws[k]
                db_ref[half] += dbv

        _wait_carried(exchange, carried, grid)

    def blk(rows, off):
        return pl.BlockSpec((rows, CONV_TILE), (lambda j, b: (b, j + off)) if rows == S else (lambda j, b: (0, j + off)))

    def both(rows):
        return pl.BlockSpec((2, rows, CONV_TILE), (lambda j, b: (0, b, j)) if rows == S else (lambda j, b: (0, 0, j)))

    outs, targets = _carrier_call(
        body, name, grid, [blk(S, 0), blk(S, nf), blk(S, 0), blk(3, 0), blk(3, nf), blk(1, 0), blk(1, nf)],
        [both(S), both(3), both(1)],
        [jax.ShapeDtypeStruct((2, T, D_FF), F32), jax.ShapeDtypeStruct((2, 3, D_FF), F32),
         jax.ShapeDtypeStruct((2, 1, D_FF), F32)], [], {}, (u, u, da, cw, cw, cb, cb), exchange)
    return (*outs, targets)


def _sb_scores(qb, kblk, on_diag_mask):
    z = _dot(qb, kblk, 1, 1) * (SB_DIM ** -0.5)
    l1 = jnp.log(1.0 + jnp.exp(-jnp.abs(z)))
    ls = jnp.minimum(z, 0.0) - l1
    lk = jnp.where(on_diag_mask, ls - z, 0.0)
    return ls, lk


def _sb_fwd(proj, S, name, exchange=None):
    T = proj.shape[0]
    BQ, BK = SB_QBLOCK, ATT_BLOCK
    unroll = BQ // BK
    nq = S // BQ
    nhp = SB_HEADS // 2
    heads = [slice(h * SB_DIM, (h + 1) * SB_DIM) for h in range(2)]

    grid = (T // S, nhp)

    def body(*refs):
        (q_ref, k_ref, v_ref, o_ref, tot_ref), carried = _carried(exchange, refs, 3, 2, 0)
        _start_carried(exchange, carried, grid)
        ahead = _iota2((BQ, BK), 1) - _iota2((BQ, BK), 0)
        upper = (_iota2((BK, BK), 0) > _iota2((BK, BK), 1)).astype(BF16)

        def qloop(iq, carry):
            q0 = pl.multiple_of(iq * BQ, BQ)
            rows = pl.ds(q0, BQ)
            qbs = [q_ref[rows, sl] for sl in heads]
            nkb = (iq + 1) * (BQ // BK)

            def kloop(jj, kc):
                blocks = []
                for u in range(unroll):
                    k0 = pl.multiple_of((nkb - 1 - unroll * jj - u) * BK, BK)
                    blocks.append((pl.ds(k0, BK), ahead < q0 - k0))
                units = [(h, krows, mask) for krows, mask in blocks for h in range(2)]
                scores = [_sb_scores(qbs[h], k_ref[krows, heads[h]], mask) for h, krows, mask in units]
                laters = [_dot_exact_l(lk, upper) for _, lk in scores]
                runs = [kc[h][0] for h in range(2)]
                accs = [kc[h][1] for h in range(2)]
                for (h, krows, mask), (ls, lk), later in zip(units, scores, laters):
                    w = jnp.where(mask, jnp.exp(ls + later + runs[h]), 0.0)
                    accs[h] = accs[h] + _dot(w, v_ref[krows, heads[h]], 1, 0)
                    runs[h] = runs[h] + jnp.sum(lk, axis=1, keepdims=True)
                return tuple((runs[h], accs[h]) for h in range(2))

            init = (jnp.zeros((BQ, 1), F32), jnp.zeros((BQ, SB_DIM), F32))
            res = lax.fori_loop(0, nkb // unroll, kloop, (init, init))
            for h, (sl, (run, acc)) in enumerate(zip(heads, res)):
                o_ref[rows, sl] = acc
                tot_ref[rows, h:h + 1] = run
            return carry

        lax.fori_loop(0, nq, qloop, 0)
        _wait_carried(exchange, carried, grid)

    def spec(off):
        return pl.BlockSpec((S, LANES), lambda b, hp: (b, hp + off))

    (cat, tot), targets = _carrier_call(
        body, name, grid, [spec(0), spec(nhp), spec(2 * nhp)],
        [spec(0), pl.BlockSpec((None, S, 2), lambda b, hp: (hp, b, 0))],
        [jax.ShapeDtypeStruct((T, 2 * SB_HEADS * SB_DIM), F32), jax.ShapeDtypeStruct((nhp, T, 2), F32)],
        [], {}, (proj, proj, proj), exchange)
    return cat, tot, targets


def _sb_bwd(proj, tot, dcat, S, name, exchange=None):
    T, width = proj.shape
    BQ, BK = SB_QBLOCK, ATT_BLOCK
    unroll = BQ // BK
    nq = S // BQ
    nhp = SB_HEADS // 2
    scale = SB_DIM ** -0.5
    heads = [slice(h * SB_DIM, (h + 1) * SB_DIM) for h in range(2)]

    grid = (T // S, nhp)

    def body(*refs):
        (q_ref, k_ref, v_ref, tot_ref, do_ref, dp_hbm, dq_s, dk_s, dv_s, sems), carried = _carried(
            exchange, refs, 5, 1, 4)
        _start_carried(exchange, carried, grid)
        b, hp = pl.program_id(0), pl.program_id(1)
        ahead = _iota2((BQ, BK), 1) - _iota2((BQ, BK), 0)
        r, c = _iota2((BK, BK), 0), _iota2((BK, BK), 1)
        upto = (r <= c).astype(BF16)
        earlier = (r < c).astype(BF16)
        dk_s[...] = jnp.zeros_like(dk_s)
        dv_s[...] = jnp.zeros_like(dv_s)

        def qloop(iq, carry):
            q0 = pl.multiple_of(iq * BQ, BQ)
            rows = pl.ds(q0, BQ)
            qbs = [q_ref[rows, sl] for sl in heads]
            dobs = [do_ref[rows, sl] for sl in heads]
            totals = [tot_ref[rows, h:h + 1] for h in range(2)]

            def kloop(m, kc):
                blocks = []
                for u in range(unroll):
                    k0 = pl.multiple_of((unroll * m + u) * BK, BK)
                    blocks.append((pl.ds(k0, BK), ahead < q0 - k0))
                units = [(h, krows, mask) for krows, mask in blocks for h in range(2)]
                kblks = [k_ref[krows, heads[h]] for h, krows, _ in units]
                scores = [_sb_scores(qbs[h], kblk, mask) for (h, _, mask), kblk in zip(units, kblks)]
                prefixes = [_dot_exact_l(lk, upto) for _, lk in scores]
                dws = [_dot(dobs[h], v_ref[krows, heads[h]], 1, 1) for h, krows, _ in units]
                runs = [kc[h][0] for h in range(2)]
                gruns = [kc[h][1] for h in range(2)]
                dqs = [kc[h][2] for h in range(2)]
                ws, gs = [], []
                for (h, _, mask), (ls, lk), prefix, dw in zip(units, scores, prefixes, dws):
                    w = jnp.where(mask, jnp.exp(ls + (totals[h] - (prefix + runs[h]))), 0.0)
                    runs[h] = runs[h] + jnp.sum(lk, axis=1, keepdims=True)
                    ws.append(w)
                    gs.append(w * dw)
                gprefixes = [_dot_exact_l(g, earlier) for g in gs]
                dzs = []
                for (h, _, mask), (ls, _), g, gprefix in zip(units, scores, gs, gprefixes):
                    beta = jnp.exp(ls)
                    dzs.append(jnp.where(mask, g * (1.0 - beta) - beta * (gprefix + gruns[h]), 0.0) * scale)
                    gruns[h] = gruns[h] + jnp.sum(g, axis=1, keepdims=True)
                for (h, krows, _), kblk, w, dz in zip(units, kblks, ws, dzs):
                    dv_s[krows, heads[h]] += _dot(w, dobs[h], 0, 0)
                    dk_s[krows, heads[h]] += _dot(dz, qbs[h], 0, 0)
                    dqs[h] = dqs[h] + _dot(dz, kblk, 1, 0)
                return tuple((runs[h], gruns[h], dqs[h]) for h in range(2))

            zero = jnp.zeros((BQ, 1), F32)
            init = (zero, zero, jnp.zeros((BQ, SB_DIM), F32))
            res = lax.fori_loop(0, (iq + 1) * (BQ // BK) // unroll, kloop, (init, init))
            for sl, (_, _, dq) in zip(heads, res):
                dq_s[rows, sl] = dq
            return carry

        lax.fori_loop(0, nq, qloop, 0)
        r0 = pl.multiple_of(b * S, S)
        copies = []
        for n, buf in enumerate((dq_s, dk_s, dv_s)):
            c0 = pl.multiple_of((hp + n * nhp) * LANES, LANES)
            copies.append(pltpu.make_async_copy(buf, dp_hbm.at[pl.ds(r0, S), pl.ds(c0, LANES)], sems.at[n]))
        for cp in copies:
            cp.start()
        for cp in copies:
            cp.wait()
        _wait_carried(exchange, carried, grid)

    def spec(off):
        return pl.BlockSpec((S, LANES), lambda b, hp: (b, hp + off))

    (dproj,), targets = _carrier_call(
        body, name, grid,
        [spec(0), spec(nhp), spec(2 * nhp), pl.BlockSpec((None, S, 2), lambda b, hp: (hp, b, 0)), spec(0)],
        [ANY], [jax.ShapeDtypeStruct((T, width), F32)],
        [pltpu.VMEM((S, LANES), F32)] * 3 + [pltpu.SemaphoreType.DMA((3,))], {}, (proj, proj, proj, tot, dcat),
        exchange)
    return dproj, targets


HG_COL0 = 3 * SB_HEADS * SB_DIM // LANES


def _hg_gates(q, fp, lbv):
    sg = _sig(fp)
    f = lbv + (1.0 - lbv) * sg
    kk = (1.0 - lbv) * _sig(-fp)
    sq = _sig(q)
    return sg, f, kk, sq


def _hg_chunk(qs, kk, lf, incl):
    C = HG_CHUNK
    b = _dot_exact_r(incl, lf)
    bl = b[C - 1:C, :]
    bm = b[C // 2 - 1:C // 2, :]
    e_t = jnp.exp(b - bm)
    e_s = jnp.exp(bm - b)
    e_i = jnp.exp(b)
    e_e = jnp.exp(bl - b)
    return bl, e_t, e_s, e_i, e_e


def _hgrn_fwd(proj, cat, lb, hgn, S, name, exchange=None):
    T = proj.shape[0]
    B = T // S
    C = HG_CHUNK
    NC = S // C

    grid = (B, HG_HEADS // HG_STEP)

    def body(*refs):
        (q_ref, f_ref, i_ref, g_ref, lb_ref, hgn_ref, _, ob_ref, oraw_ref, st_ref, state), carried = _carried(
            exchange, refs, 7, 3, 1)
        _start_carried(exchange, carried, grid)
        state[...] = jnp.zeros_like(state)
        row, col = _iota2((C, C), 0), _iota2((C, C), 1)
        causal = row >= col
        incl = causal.astype(BF16)

        def trip(m, carry):
            units = []
            for u in range(HG_TRIP):
                c = m * HG_TRIP + u
                rows = pl.ds(pl.multiple_of(c * C, C), C)
                units += [(c, hh, rows, slice(hh * LANES, (hh + 1) * LANES)) for hh in range(HG_STEP)]
            qs_, kks, lfs = [], [], []
            for _, _, rows, hs in units:
                q = q_ref[rows, hs]
                _, f, kk, sq = _hg_gates(q, f_ref[rows, hs], lb_ref[:, hs])
                qs_.append(q * sq)
                kks.append(kk)
                lfs.append(jnp.log(f))
            decays = [_hg_chunk(None, None, lf, incl) for lf in lfs]
            ps = [jnp.where(causal, _dot(qs * e_t, kk * e_s, 1, 1), 0.0)
                  for qs, kk, (_, e_t, e_s, _, _) in zip(qs_, kks, decays)]
            outs = []
            for (c, hh, rows, hs), qs, kk, (bl, _, _, e_i, e_e), p in zip(units, qs_, kks, decays, ps):
                iv = i_ref[rows, hs]
                st = state[hh]
                st_ref[0, hh, c] = st
                outs.append(_dot(qs * e_i, st, 1, 1) + _dot(p, iv, 1, 0))
                state[hh] = st * jnp.exp(bl) + _dot(iv, kk * e_e, 0, 0)
            for (_, _, rows, hs), o in zip(units, outs):
                gv = g_ref[rows, hs]
                oraw_ref[rows, hs] = o
                r = lax.rsqrt(jnp.mean(o * o, axis=1, keepdims=True) + EPS)
                ob_ref[rows, hs] = o * r * hgn_ref[...] * (gv * _sig(gv))
            return carry

        lax.fori_loop(0, NC // HG_TRIP, trip, 0)
        _wait_carried(exchange, carried, grid)

    width = HG_STEP * LANES
    col0 = HG_COL0 * LANES // width
    nstep = HG_HEADS // HG_STEP

    def spec(off):
        return pl.BlockSpec((S, width), lambda b, h: (b, h + off))

    outs, targets = _carrier_call(
        body, name, grid,
        [spec(col0), spec(col0 + nstep), spec(col0 + 2 * nstep), spec(col0 + 3 * nstep),
         pl.BlockSpec((1, width), lambda b, h: (0, h)), pl.BlockSpec((1, LANES), lambda b, h: (0, 0)), ANY],
        [spec(nstep), spec(0), pl.BlockSpec((1, HG_STEP, NC, LANES, LANES), lambda b, h: (b, h, 0, 0, 0))],
        [jax.ShapeDtypeStruct(cat.shape, F32), jax.ShapeDtypeStruct((T, HG_HEADS * LANES), F32),
         jax.ShapeDtypeStruct((B, HG_HEADS, NC, LANES, LANES), F32)],
        [pltpu.VMEM((HG_STEP, LANES, LANES), F32)], {6: 0}, (proj, proj, proj, proj, lb, hgn, cat), exchange)
    return (*outs, targets)


def _hgrn_bwd(proj, oraw, dcat, states, lb, hgn, dproj, S, name):
    T = proj.shape[0]
    B = T // S
    C = HG_CHUNK
    NC = S // C

    def body(q_ref, f_ref, i_ref, g_ref, oraw_ref, dy_ref, st_ref, lb_ref, hgn_ref, dp_in,
             dp_hbm, dlb_ref, dhgn_ref, dstate, dq_s, df_s, di_s, dg_s, sems):
        del dp_in
        h, b = pl.program_id(0), pl.program_id(1)
        row, col = _iota2((C, C), 0), _iota2((C, C), 1)
        causal = row >= col
        incl = causal.astype(BF16)
        last_row = _iota2((C, LANES), 0) == C - 1
        hg = hgn_ref[...]
        dstate[...] = jnp.zeros_like(dstate)

        @pl.when(b == 0)
        def _():
            dlb_ref[...] = jnp.zeros_like(dlb_ref)

        @pl.when(jnp.logical_and(b == 0, h == 0))
        def _():
            dhgn_ref[...] = jnp.zeros_like(dhgn_ref)

        def trip(m, carry):
            units = []
            for u in range(HG_TRIP):
                c = NC - 1 - (m * HG_TRIP + u)
                rows = pl.ds(pl.multiple_of(c * C, C), C)
                units += [(c, hh, rows, slice(hh * LANES, (hh + 1) * LANES)) for hh in range(HG_STEP)]
            dos = []
            dhgn = jnp.zeros((1, LANES), F32)
            for _, _, rows, hs in units:
                gv, o, dy = g_ref[rows, hs], oraw_ref[rows, hs], dy_ref[rows, hs]
                r = lax.rsqrt(jnp.mean(o * o, axis=1, keepdims=True) + EPS)
                on = o * r
                sgv = _sig(gv)
                silu_g = gv * sgv
                dg_s[rows, hs] = dy * on * hg * (sgv * (1.0 + gv * (1.0 - sgv)))
                dhgn = dhgn + jnp.sum(dy * on * silu_g, axis=0, keepdims=True)
                dn = dy * hg * silu_g
                dos.append(r * dn - o * (r * r * r * jnp.mean(o * dn, axis=1, keepdims=True)))
            dhgn_ref[...] += dhgn
            pre = []
            for (_, _, rows, hs), do in zip(units, dos):
                q, iv = q_ref[rows, hs], i_ref[rows, hs]
                sg, f, kk, sq = _hg_gates(q, f_ref[rows, hs], lb_ref[:, hs])
                pre.append((q, iv, sg, f, kk, sq, q * sq))
            decays = [_hg_chunk(None, None, jnp.log(f), incl) for _, _, _, f, _, _, _ in pre]
            prods = [(qs * e_t, kk * e_s, qs * e_i, kk * e_e)
                     for (_, _, _, _, kk, _, qs), (_, e_t, e_s, e_i, e_e) in zip(pre, decays)]
            ps = [jnp.where(causal, _dot(qd, kd, 1, 1), 0.0) for qd, kd, _, _ in prods]
            dps = [jnp.where(causal, _dot(do, iv, 1, 1), 0.0) for do, (_, iv, _, _, _, _, _) in zip(dos, pre)]
            dqds = [_dot(dp, kd, 1, 0) for dp, (_, kd, _, _) in zip(dps, prods)]
            dkds = [_dot(dp, qd, 0, 0) for dp, (qd, _, _, _) in zip(dps, prods)]
            pdos = [_dot(p, do, 0, 0) for p, do in zip(ps, dos)]
            chain = []
            for (c, hh, rows, hs), do, (_, iv, _, _, _, _, _), (bl, _, _, _, _), (_, _, qi, ke), pdo in zip(
                    units, dos, pre, decays, prods, pdos):
                st = st_ref[0, hh, c]
                dst = dstate[hh]
                ebl = jnp.exp(bl)
                dqi = _dot(do, st, 1, 0)
                di_s[rows, hs] = pdo + _dot(ke, dst, 1, 1)
                dke = _dot(iv, dst, 1, 0)
                dbl = jnp.sum(st * dst, axis=0, keepdims=True) * ebl + jnp.sum(dke * ke, axis=0, keepdims=True)
                dstate[hh] = _dot(do, qi, 0, 0) + dst * ebl
                chain.append((dqi, dke, dbl))
            for (_, _, rows, hs), (q, _, sg, f, _, sq, _), (_, e_t, e_s, e_i, e_e), (qd, kd, qi, ke), dqd, dkd, (
                    dqi, dke, dbl) in zip(units, pre, decays, prods, dqds, dkds, chain):
                lbv = lb_ref[:, hs]
                db = dqd * qd - dkd * kd + dqi * qi - dke * ke + jnp.where(last_row, dbl, 0.0)
                dqs = dqd * e_t + dqi * e_i
                dkk = dkd * e_s + dke * e_e
                dlf = _dot_exact_r(incl, db, cm=0)
                oms = 1.0 - sg
                dfd = dlf / f
                df_s[rows, hs] = (dfd - dkk) * (1.0 - lbv) * sg * oms
                dq_s[rows, hs] = dqs * (sq * (1.0 + q * (1.0 - sq)))
                dlb_ref[:, hs] += jnp.sum((dfd - dkk) * oms, axis=0, keepdims=True)
            return carry

        lax.fori_loop(0, NC // HG_TRIP, trip, 0)
        r0 = pl.multiple_of(b * S, S)
        copies = []
        for n, buf in enumerate((dq_s, df_s, di_s, dg_s)):
            c0 = pl.multiple_of((col0 + n * nstep + h) * width, width)
            copies.append(pltpu.make_async_copy(buf, dp_hbm.at[pl.ds(r0, S), pl.ds(c0, width)], sems.at[n]))
        for cp in copies:
            cp.start()
        for cp in copies:
            cp.wait()

    width = HG_STEP * LANES
    col0 = HG_COL0 * LANES // width
    nstep = HG_HEADS // HG_STEP

    def spec(off):
        return pl.BlockSpec((S, width), lambda h, b: (b, h + off))

    return pl.pallas_call(
        body, name=name, grid=(nstep, B),
        in_specs=[spec(col0), spec(col0 + nstep), spec(col0 + 2 * nstep), spec(col0 + 3 * nstep), spec(0), spec(nstep),
                  pl.BlockSpec((1, HG_STEP, NC, LANES, LANES), lambda h, b: (b, h, 0, 0, 0)),
                  pl.BlockSpec((1, width), lambda h, b: (0, h)), pl.BlockSpec((1, LANES), lambda h, b: (0, 0)), ANY],
        out_specs=[ANY, pl.BlockSpec((1, width), lambda h, b: (0, h)), pl.BlockSpec((1, LANES), lambda h, b: (0, 0))],
        out_shape=[jax.ShapeDtypeStruct(dproj.shape, F32), jax.ShapeDtypeStruct((1, HG_HEADS * LANES), F32),
                   jax.ShapeDtypeStruct((1, LANES), F32)],
        scratch_shapes=[pltpu.VMEM((HG_STEP, LANES, LANES), F32)] + [pltpu.VMEM((S, width), F32)] * 4
        + [pltpu.SemaphoreType.DMA((4,))],
        input_output_aliases={9: 0},
        compiler_params=_cparams("arbitrary", "arbitrary"))(proj, proj, proj, proj, oraw, dcat, states, lb, hgn, dproj)


def _lower_bound_fwd(logits, name):
    assert logits.shape[0] == 2

    def body(l_ref, o_ref):
        l0, l1 = l_ref[0:1, :], l_ref[1:2, :]
        m = jnp.maximum(l0, l1)
        e0, e1 = jnp.exp(l0 - m), jnp.exp(l1 - m)
        o_ref[0:1, :] = jnp.zeros_like(l0)
        o_ref[1:2, :] = e1 / (e0 + e1)

    return pl.pallas_call(body, name=name, out_shape=jax.ShapeDtypeStruct(logits.shape, F32))(logits)


def _lower_bound_bwd(logits, dlb, name):
    def body(l_ref, d_ref, o_ref):
        l0, l1 = l_ref[0:1, :], l_ref[1:2, :]
        m = jnp.maximum(l0, l1)
        e0, e1 = jnp.exp(l0 - m), jnp.exp(l1 - m)
        s1 = e1 / (e0 + e1)
        t = s1 * (1.0 - s1) * d_ref[1:2, :]
        o_ref[0:1, :] = -t
        o_ref[1:2, :] = t

    return pl.pallas_call(body, name=name, out_shape=jax.ShapeDtypeStruct(logits.shape, F32))(logits, dlb)


def _bucket_thresholds():
    dist = np.arange(WINDOW)
    max_exact = N_BUCKETS // 2
    large = max_exact + (np.log(np.maximum(dist, max_exact) / max_exact) / math.log(MAX_DISTANCE / max_exact)
                         * (N_BUCKETS - max_exact)).astype(np.int32)
    bucket = np.where(dist < max_exact, dist, np.minimum(large, N_BUCKETS - 1))
    assert np.all(np.diff(bucket) >= 0)
    return [int(np.argmax(bucket >= k)) if np.any(bucket >= k) else 10 ** 6 for k in range(1, N_BUCKETS)]


def _band_bucket():
    dist = _iota2((WINDOW, 2 * WINDOW), 0) + WINDOW - _iota2((WINDOW, 2 * WINDOW), 1)
    bucket = jnp.zeros((WINDOW, 2 * WINDOW), jnp.int32)
    for thr in _bucket_thresholds():
        bucket = bucket + (dist >= thr).astype(jnp.int32)
    band = jnp.logical_and(dist >= 0, dist < WINDOW)
    return bucket, band


def _bias_build(rel_bias, name):
    def body(rb_ref, o_ref):
        h = pl.program_id(0)
        bucket, _ = _band_bucket()
        bias = jnp.zeros((WINDOW, 2 * WINDOW), F32)
        for k in range(N_BUCKETS):
            bias = jnp.where(bucket == k, rb_ref[k, h], bias)
        o_ref[0] = bias

    return pl.pallas_call(
        body, name=name, grid=(SW_HEADS,), in_specs=[pl.BlockSpec(memory_space=pltpu.SMEM)],
        out_specs=pl.BlockSpec((1, WINDOW, 2 * WINDOW), lambda h: (h, 0, 0)),
        out_shape=jax.ShapeDtypeStruct((SW_HEADS, WINDOW, 2 * WINDOW), F32),
        compiler_params=_cparams("parallel"))(rel_bias)


def _bias_reduce(dbias, name):
    def body(d_ref, o_ref):
        bucket, band = _band_bucket()
        d = jnp.where(band, d_ref[0], 0.0)
        lane = _iota2((1, LANES), 1)
        out = jnp.zeros((1, LANES), F32)
        for k in range(N_BUCKETS):
            out = jnp.where(lane == k, jnp.sum(jnp.where(bucket == k, d, 0.0)), out)
        o_ref[0] = out

    return pl.pallas_call(
        body, name=name, grid=(SW_HEADS,), in_specs=[pl.BlockSpec((1, WINDOW, 2 * WINDOW), lambda h: (h, 0, 0))],
        out_specs=pl.BlockSpec((1, 1, LANES), lambda h: (h, 0, 0)),
        out_shape=jax.ShapeDtypeStruct((SW_HEADS, 1, LANES), F32), compiler_params=_cparams("parallel"))(dbias)


SW_Q_COLS = SW_HEADS * SW_DIM
SW_K_BLOCK0 = SW_Q_COLS // LANES
SW_V_BLOCK0 = SW_K_BLOCK0 + SW_KV * SW_DIM // LANES
SW_STEP_HEADS = 8


def _swa_logits(qb, kprev, kcur, bias_ref, hl, mprev, mcur):
    scale = SW_DIM ** -0.5
    lp = jnp.where(mprev, _dot(qb, kprev, 1, 1) * scale + bias_ref[hl, :, 0:WINDOW], NEG)
    lc = jnp.where(mcur, _dot(qb, kcur, 1, 1) * scale + bias_ref[hl, :, WINDOW:2 * WINDOW], NEG)
    return lp, lc


def _swa_softmax(lp, lc, sink):
    m = jnp.maximum(jnp.maximum(jnp.max(lp, axis=1, keepdims=True), jnp.max(lc, axis=1, keepdims=True)), sink)
    ep, ec = jnp.exp(lp - m), jnp.exp(lc - m)
    es = jnp.exp(sink - m)
    den = jnp.sum(ep, axis=1, keepdims=True) + jnp.sum(ec, axis=1, keepdims=True) + es
    return ep, ec, es, den


def _swa_block_heads(n, q_ref, k_ref, v_ref, bias_ref, sink_ref, kp, above, mcur):
    W = WINDOW
    rows = pl.ds(pl.multiple_of(n * W, W), W)
    prow = pl.ds(pl.multiple_of(jnp.maximum(n - 1, 0) * W, W), W)
    mprev = jnp.logical_and(above, n > 0)
    heads = []
    for kvh in range(2):
        ksl = slice(kvh * SW_DIM, (kvh + 1) * SW_DIM)
        kv = (k_ref[rows, ksl], k_ref[prow, ksl], v_ref[rows, ksl], v_ref[prow, ksl])
        for g in range(4):
            hl = kvh * 4 + g
            heads.append((hl, ksl, slice(hl * SW_DIM, (hl + 1) * SW_DIM), kv))
    qbs = [q_ref[rows, qsl] for _, _, qsl, _ in heads]
    logits = [_swa_logits(qb, kv[1], kv[0], bias_ref, hl, mprev, mcur) for qb, (hl, _, _, kv) in zip(qbs, heads)]
    soft = [_swa_softmax(lp, lc, sink_ref[kp * SW_STEP_HEADS + hl]) for (lp, lc), (hl, _, _, _) in zip(logits, heads)]
    return rows, prow, heads, qbs, soft


def _swa_fwd(qkn, proj, bias, sinks, S, name, exchange=None):
    T = qkn.shape[0]
    W = WINDOW
    nb = S // W

    grid = (T // S, 2)

    def body(*refs):
        (q_ref, k_ref, v_ref, bias_ref, sink_ref, o_ref), carried = _carried(exchange, refs, 5, 1, 0)
        _start_carried(exchange, carried, grid)
        kp = pl.program_id(1)
        row, col = _iota2((W, W), 0), _iota2((W, W), 1)
        mcur = col <= row
        above = col > row

        def blk(n, carry):
            rows, _, heads, _, soft = _swa_block_heads(n, q_ref, k_ref, v_ref, bias_ref, sink_ref, kp, above, mcur)
            outs = [(_dot(ep, kv[3], 1, 0) + _dot(ec, kv[2], 1, 0)) / den
                    for (ep, ec, _, den), (_, _, _, kv) in zip(soft, heads)]
            for (_, _, qsl, _), o in zip(heads, outs):
                o_ref[rows, qsl] = o
            return carry

        lax.fori_loop(0, nb, blk, 0)
        _wait_carried(exchange, carried, grid)

    (o,), targets = _carrier_call(
        body, name, grid,
        [pl.BlockSpec((S, 4 * LANES), lambda b, kp: (b, kp)),
         pl.BlockSpec((S, LANES), lambda b, kp: (b, SW_K_BLOCK0 + kp)),
         pl.BlockSpec((S, LANES), lambda b, kp: (b, SW_V_BLOCK0 + kp)),
         pl.BlockSpec((SW_STEP_HEADS, W, 2 * W), lambda b, kp: (kp, 0, 0)),
         pl.BlockSpec(memory_space=pltpu.SMEM)],
        [pl.BlockSpec((S, 4 * LANES), lambda b, kp: (b, kp))], [jax.ShapeDtypeStruct((T, SW_Q_COLS), F32)],
        [], {}, (qkn, qkn, proj, bias, sinks), exchange)
    return o, targets


def _swa_bwd(qkn, proj, bias, sinks, do, S, name, exchange=None):
    T, width = proj.shape
    W = WINDOW
    nb = S // W
    scale = SW_DIM ** -0.5

    grid = (2, T // S)

    def body(*refs):
        (q_ref, k_ref, v_ref, bias_ref, sink_ref, do_ref, dp_hbm, dbias_ref, dsink_ref,
         dq_s, dk_s, dv_s, sems), carried = _carried(exchange, refs, 6, 3, 4)
        _start_carried(exchange, carried, grid)
        kp, b = pl.program_id(0), pl.program_id(1)
        row, col = _iota2((W, W), 0), _iota2((W, W), 1)
        mcur = col <= row
        above = col > row
        dk_s[...] = jnp.zeros_like(dk_s)
        dv_s[...] = jnp.zeros_like(dv_s)

        @pl.when(b == 0)
        def _():
            dbias_ref[...] = jnp.zeros_like(dbias_ref)
            dsink_ref[...] = jnp.zeros_like(dsink_ref)

        def blk(n, carry):
            rows, prow, heads, qbs, soft = _swa_block_heads(n, q_ref, k_ref, v_ref, bias_ref, sink_ref, kp, above,
                                                            mcur)
            dobs = [do_ref[rows, qsl] for _, _, qsl, _ in heads]
            dps = [(_dot(dob, kv[3], 1, 1), _dot(dob, kv[2], 1, 1)) for dob, (_, _, _, kv) in zip(dobs, heads)]
            grads = []
            for (hl, _, _, _), (ep, ec, es, den), (dpp, dpc) in zip(heads, soft, dps):
                inv = 1.0 / den
                pp, pc = ep * inv, ec * inv
                total = jnp.sum(pp * dpp, axis=1, keepdims=True) + jnp.sum(pc * dpc, axis=1, keepdims=True)
                dlp = pp * (dpp - total)
                dlc = pc * (dpc - total)
                dsink_ref[hl:hl + 1, :] += jnp.zeros((1, LANES), F32) - jnp.sum(es * inv * total)
                dbias_ref[hl, :, 0:W] += dlp
                dbias_ref[hl, :, W:2 * W] += dlc
                grads.append((pp, pc, dlp, dlc))
            for (_, _, qsl, kv), (_, _, dlp, dlc) in zip(heads, grads):
                dq_s[rows, qsl] = (_dot(dlp, kv[1], 1, 0) + _dot(dlc, kv[0], 1, 0)) * scale
            for kvh in range(2):
                group = range(4 * kvh, 4 * kvh + 4)
                ksl = heads[4 * kvh][1]
                dk_s[prow, ksl] += sum(_dot(grads[i][2], qbs[i], 0, 0) for i in group) * scale
                dk_s[rows, ksl] += sum(_dot(grads[i][3], qbs[i], 0, 0) for i in group) * scale
                dv_s[prow, ksl] += sum(_dot(grads[i][0], dobs[i], 0, 0) for i in group)
                dv_s[rows, ksl] += sum(_dot(grads[i][1], dobs[i], 0, 0) for i in group)
            return carry

        lax.fori_loop(0, nb, blk, 0)
        r0 = pl.multiple_of(b * S, S)
        cq = pl.multiple_of(kp * 4 * LANES, LANES)
        ck = pl.multiple_of((SW_K_BLOCK0 + kp) * LANES, LANES)
        cv = pl.multiple_of((SW_V_BLOCK0 + kp) * LANES, LANES)
        copies = [pltpu.make_async_copy(dq_s, dp_hbm.at[pl.ds(r0, S), pl.ds(cq, 4 * LANES)], sems.at[0]),
                  pltpu.make_async_copy(dk_s, dp_hbm.at[pl.ds(r0, S), pl.ds(ck, LANES)], sems.at[1]),
                  pltpu.make_async_copy(dv_s, dp_hbm.at[pl.ds(r0, S), pl.ds(cv, LANES)], sems.at[2])]
        for cp in copies:
            cp.start()
        for cp in copies:
            cp.wait()
        _wait_carried(exchange, carried, grid)

    qspec = pl.BlockSpec((S, 4 * LANES), lambda kp, b: (b, kp))
    outs, targets = _carrier_call(
        body, name, grid,
        [qspec, pl.BlockSpec((S, LANES), lambda kp, b: (b, SW_K_BLOCK0 + kp)),
         pl.BlockSpec((S, LANES), lambda kp, b: (b, SW_V_BLOCK0 + kp)),
         pl.BlockSpec((SW_STEP_HEADS, W, 2 * W), lambda kp, b: (kp, 0, 0)),
         pl.BlockSpec(memory_space=pltpu.SMEM), qspec],
        [ANY, pl.BlockSpec((SW_STEP_HEADS, W, 2 * W), lambda kp, b: (kp, 0, 0)),
         pl.BlockSpec((SW_STEP_HEADS, LANES), lambda kp, b: (kp, 0))],
        [jax.ShapeDtypeStruct((T, width), F32), jax.ShapeDtypeStruct((SW_HEADS, W, 2 * W), F32),
         jax.ShapeDtypeStruct((SW_HEADS, LANES), F32)],
        [pltpu.VMEM((S, 4 * LANES), F32), pltpu.VMEM((S, LANES), F32), pltpu.VMEM((S, LANES), F32),
         pltpu.SemaphoreType.DMA((3,))], {}, (qkn, qkn, proj, bias, sinks, do), exchange)
    return (*outs, targets)


CHIP_FLIPS = ((1, 0), (0, 1), (1, 1))


def _flip(v, f):
    return 1 - v if f else v


class _Exchange:
    def __init__(self, sources, targets, copies):
        self.sources, self.targets = list(sources), list(targets)
        self._copies = copies
        n = len(self.sources)
        self.scratch = [pltpu.SemaphoreType.DMA((n, 3)), pltpu.SemaphoreType.DMA((n, 3)),
                        pltpu.SemaphoreType.DMA((n,))]

    def _descriptors(self, srcs, dsts, sems, chip, peers):
        send, recv, loc = sems
        out = []
        for t, (local, remote) in enumerate(self._copies(srcs, dsts, chip)):
            out.append(pltpu.make_async_copy(local[0], local[1], loc.at[t]))
            for r, (src, dst) in enumerate(remote):
                out.append(pltpu.make_async_remote_copy(src, dst, send.at[t, r], recv.at[t, r],
                                                        device_id=peers[r], device_id_type=MESH))
        return out

    def start(self, srcs, dsts, sems):
        x, y, c = lax.axis_index("x"), lax.axis_index("y"), lax.axis_index("c")
        peers = [(_flip(x, fx), _flip(y, fy), c) for fx, fy in CHIP_FLIPS]
        for chip in range(4):
            @pl.when(2 * x + y == chip)
            def _():
                for cp in self._descriptors(srcs, dsts, sems, chip, peers):
                    cp.start()

    def wait(self, srcs, dsts, sems):
        me = (lax.axis_index("x"), lax.axis_index("y"), lax.axis_index("c"))
        for cp in self._descriptors(srcs, dsts, sems, 0, [me] * 3):
            cp.wait()

    def operands(self):
        return self.sources + self.targets

    def specs(self):
        ns, nt = len(self.sources), len(self.targets)
        return [ANY] * (ns + nt), [ANY] * nt, [jax.ShapeDtypeStruct(t.shape, t.dtype) for t in self.targets]

    def aliases(self, n_in, n_out):
        ns = len(self.sources)
        return {n_in + ns + i: n_out + i for i in range(len(self.targets))}

    def split(self, refs, n_in, n_out, n_scr):
        ns, nt = len(self.sources), len(self.targets)
        o0 = n_in + ns + nt
        s0 = o0 + n_out + nt
        own = list(refs[:n_in]) + list(refs[o0:o0 + n_out]) + list(refs[s0:s0 + n_scr])
        return own, (refs[n_in:n_in + ns], refs[o0 + n_out:o0 + n_out + nt], refs[s0 + n_scr:])


def _carried(exchange, refs, n_in, n_out, n_scr):
    if exchange is None:
        return list(refs), None
    return exchange.split(refs, n_in, n_out, n_scr)


def _grid_edge(grid, last):
    conds = [pl.program_id(d) == (n - 1 if last else 0) for d, n in enumerate(grid)]
    out = conds[0]
    for cnd in conds[1:]:
        out = jnp.logical_and(out, cnd)
    return out


def _start_carried(exchange, parts, grid):
    if parts is not None:
        @pl.when(_grid_edge(grid, False))
        def _():
            exchange.start(*parts)


def _wait_carried(exchange, parts, grid):
    if parts is not None:
        @pl.when(_grid_edge(grid, True))
        def _():
            exchange.wait(*parts)


def _carrier_call(body, name, grid, in_specs, out_specs, out_shape, scratch, aliases, operands, exchange):
    n_out = len(out_shape)
    aliases = dict(aliases)
    if exchange is not None:
        ex_in, ex_out, ex_shape = exchange.specs()
        aliases.update(exchange.aliases(len(in_specs), n_out))
        in_specs, out_specs, out_shape = in_specs + ex_in, out_specs + ex_out, out_shape + ex_shape
        scratch = scratch + exchange.scratch
        operands = list(operands) + exchange.operands()
    outs = pl.pallas_call(body, name=name, grid=grid, in_specs=in_specs, out_specs=out_specs, out_shape=out_shape,
                          scratch_shapes=scratch, input_output_aliases=aliases,
                          compiler_params=_cparams(*["arbitrary"] * len(grid)))(*operands)
    return outs[:n_out], outs[n_out:]


def _exchange_call(exchange, name):
    in_specs, out_specs, out_shape = exchange.specs()

    def body(*refs):
        _, parts = exchange.split(refs, 0, 0, 0)
        exchange.start(*parts)
        exchange.wait(*parts)

    return pl.pallas_call(body, name=name, in_specs=in_specs, out_specs=out_specs, out_shape=out_shape,
                          scratch_shapes=exchange.scratch, input_output_aliases=exchange.aliases(0, 0))(
        *exchange.operands())


def _gather_exchange(shards, fulls, axes, layers):
    sizes = [s.shape[a] for s, a in zip(shards, axes)]

    def copies(srcs, dsts, chip):
        out = []
        for src, full, axis, size, (l0, l1) in zip(srcs, dsts, axes, sizes, layers):
            part = src.at[l0:l1]
            cut = pl.ds(chip * size, size)
            dst = full.at[l0:l1, cut, :] if axis == 1 else full.at[l0:l1, :, cut]
            out.append(((part, dst), [(part, dst)] * 3))
        return out

    return _Exchange(shards, fulls, copies)


def _scatter_exchange(grads, stacks, axes, layers):
    sizes = [g.shape[a - 1] // 4 for g, a in zip(grads, axes)]

    def copies(srcs, dsts, chip):
        out = []
        for g, stack, axis, size, layer in zip(srcs, dsts, axes, sizes, layers):
            def cut(j, g=g, axis=axis, size=size):
                return g.at[pl.ds(j * size, size), :] if axis == 1 else g.at[:, pl.ds(j * size, size)]

            remote = [(cut(chip ^ (2 * fx + fy)), stack.at[r, layer]) for r, (fx, fy) in enumerate(CHIP_FLIPS)]
            out.append(((cut(chip), stack.at[3, layer]), remote))
        return out

    return _Exchange(grads, stacks, copies)


def _swap_with_sibling(parts, name):
    n = len(parts)

    def body(*refs):
        ins, outs = refs[:n], refs[n:2 * n]
        send, recv = refs[2 * n:]
        peer = (lax.axis_index("x"), lax.axis_index("y"), 1 - lax.axis_index("c"))
        copies = [pltpu.make_async_remote_copy(ins[t], outs[t], send.at[t], recv.at[t], device_id=peer,
                                               device_id_type=MESH) for t in range(n)]
        for cp in copies:
            cp.start()
        for cp in copies:
            cp.wait()

    return pl.pallas_call(
        body, name=name, in_specs=[ANY] * n, out_specs=[ANY] * n,
        out_shape=[jax.ShapeDtypeStruct(p.shape, p.dtype) for p in parts],
        scratch_shapes=[pltpu.SemaphoreType.DMA((n,)), pltpu.SemaphoreType.DMA((n,))])(*parts)


def _allreduce_small(v, name):
    R = v.shape[0]
    ND = 8

    def body(v_ref, o_ref, buf, send, recv):
        x, y, c = lax.axis_index("x"), lax.axis_index("y"), lax.axis_index("c")
        me = 4 * x + 2 * y + c
        copies = []
        for d in range(1, ND):
            peer = (_flip(x, d >> 2 & 1), _flip(y, d >> 1 & 1), _flip(c, d & 1))
            copies.append(pltpu.make_async_remote_copy(v_ref, buf.at[me], send.at[d], recv.at[me], device_id=peer,
                                                       device_id_type=MESH))
        for cp in copies:
            cp.start()
        buf[pl.ds(me, 1)] = v_ref[...][None]
        for k in range(ND):
            @pl.when(me != k)
            def _():
                pltpu.make_async_remote_copy(v_ref, buf.at[k], send.at[0], recv.at[k], device_id=(x, y, c),
                                             device_id_type=MESH).wait_recv()
        for cp in copies:
            cp.wait_send()
        total = buf[0]
        for k in range(1, ND):
            total = total + buf[k]
        o_ref[...] = total

    vm = pl.BlockSpec(memory_space=pltpu.VMEM)
    return pl.pallas_call(
        body, name=name, in_specs=[vm], out_specs=vm, out_shape=jax.ShapeDtypeStruct((R, LANES), F32),
        scratch_shapes=[pltpu.VMEM((ND, R, LANES), F32), pltpu.SemaphoreType.DMA((ND,)),
                        pltpu.SemaphoreType.DMA((ND,))],
        compiler_params=pltpu.CompilerParams(vmem_limit_bytes=VMEM_LIMIT))(v)


def _tile2(R, Cn):
    tc = _pick(Cn, 2048)
    tr = R
    for cand in (256, 128, 64, 32, 16, 8):
        if R % cand == 0:
            tr = cand
            break
    return tr, tc


def _sum4(stack, name):
    _, R, Cn = stack.shape
    tr, tc = _tile2(R, Cn)

    def body(s_ref, o_ref):
        o_ref[...] = ((s_ref[0].astype(F32) + s_ref[1].astype(F32)) + s_ref[2].astype(F32)) + s_ref[3].astype(F32)

    return pl.pallas_call(
        body, name=name, grid=(R // tr, Cn // tc), in_specs=[pl.BlockSpec((4, tr, tc), lambda i, j: (0, i, j))],
        out_specs=pl.BlockSpec((tr, tc), lambda i, j: (i, j)), out_shape=jax.ShapeDtypeStruct((R, Cn), F32),
        compiler_params=_cparams("parallel", "parallel"))(stack)


def _adamw(w, m, v, g_parts, name):
    R, Cn = w.shape
    tr, tc = _tile2(R, Cn)
    npart = len(g_parts)
    c1 = 1.0 / (1.0 - ADAM_B1 ** ADAM_STEP)
    c2 = 1.0 / (1.0 - ADAM_B2 ** ADAM_STEP)

    def body(*refs):
        w_ref, m_ref, v_ref = refs[:3]
        g_refs = refs[3:3 + npart]
        g_out, d_out, m_out, v_out = refs[3 + npart:]
        g = g_refs[0][...]
        for r in g_refs[1:]:
            g = g + r[...]
        mn = ADAM_B1 * m_ref[...] + (1.0 - ADAM_B1) * g
        vn = ADAM_B2 * v_ref[...] + (1.0 - ADAM_B2) * (g * g)
        g_out[...] = g
        m_out[...] = mn
        v_out[...] = vn
        d_out[...] = -ADAM_LR * ((mn * c1) / (jnp.sqrt(vn * c2) + ADAM_EPS) + ADAM_WD * w_ref[...])

    spec = pl.BlockSpec((tr, tc), lambda i, j: (i, j))
    return pl.pallas_call(
        body, name=name, grid=(R // tr, Cn // tc), in_specs=[spec] * (3 + npart), out_specs=[spec] * 4,
        out_shape=[jax.ShapeDtypeStruct((R, Cn), F32)] * 4,
        compiler_params=_cparams("parallel", "parallel"))(w, m, v, *g_parts)


SHARDED = (("ab_w_in", 2), ("ab_w_out", 1), ("c_w_in", 2), ("c_w_out", 1), ("ffn_up", 2), ("ffn_conv", 2),
           ("ffn_down", 1), ("ple_gate", 1), ("ple_proj", 2))
SMALL = ("mix_norm", "hg_lb_logits", "hg_out_norm", "q_norm", "k_norm", "sinks", "rel_bias", "ffn_norm",
         "ffn_conv_b", "ple_norm")
WEIGHTS = ("mix_norm", "ab_w_in", "hg_lb_logits", "hg_out_norm", "ab_w_out", "c_w_in", "q_norm", "k_norm", "sinks",
           "rel_bias", "c_w_out", "ffn_norm", "ffn_up", "ffn_conv", "ffn_conv_b", "ffn_down", "ple_norm", "ple_gate",
           "ple_proj")
PACK_ALIGN = 8 * LANES


def _pack(arrs):
    pieces = []
    for a in arrs:
        flat = a.reshape(-1)
        pad = -flat.shape[0] % PACK_ALIGN
        pieces.append(jnp.pad(flat, (0, pad)).reshape(-1, LANES))
    return jnp.concatenate(pieces, axis=0)


def _unpack(packed, like):
    out, r = [], 0
    for a in like:
        size = int(np.prod(a.shape))
        rows = (size + PACK_ALIGN - 1) // PACK_ALIGN * 8
        out.append(packed[r:r + rows].reshape(-1)[:size].reshape(a.shape))
        r += rows
    return out


def _family_range(name, lo, hi):
    if name.startswith("ab_"):
        idx = [i // 2 for i in range(lo, hi) if i % 2 == 0]
    elif name.startswith("c_"):
        idx = [i // 2 for i in range(lo, hi) if i % 2 == 1]
    else:
        idx = list(range(lo, hi))
    return (idx[0], idx[-1] + 1) if idx else None


W_IN = ("ab_w_in", "c_w_in")
REST = tuple(k for k, _ in SHARDED if k not in W_IN)
MIXER = ("ab_w_in", "ab_w_out", "c_w_in", "c_w_out")
FFN = ("ffn_up", "ffn_conv", "ffn_down")
PLE = ("ple_gate", "ple_proj")
GATHER_PLAN = {
    "gather_first": [(0, W_IN)],
    "sb_fwd_0": [(0, REST), (1, ("c_w_in",))],
    "hgrn_fwd_0": [(1, ("ffn_up", "ffn_conv"))],
    "ffn_up_0": [(1, ("ffn_down", "c_w_out"))],
    "ffn_down_0": [(1, PLE)],
    "swa_fwd_1": [(2, ("ab_w_in",))],
    "ffn_up_1": [(2, ("ab_w_out", "ffn_down"))],
    "ffn_down_1": [(2, PLE)],
    "sb_fwd_2": [(2, ("ffn_up", "ffn_conv")), (3, ("c_w_in", "c_w_out", "ffn_up", "ffn_conv"))],
    "hgrn_fwd_2": [(3, ("ffn_down",))],
    "ffn_up_2": [(3, PLE)],
}
SCATTER_PLAN = {
    "d_ffn_up_3": [(3, ("ffn_down", "ffn_conv"))],
    "d_hn2_3": [(3, PLE)],
    "swa_bwd_3": [(3, ("ffn_up", "c_w_out"))],
    "sb_bwd_2": [(3, W_IN), (2, REST)],
    "convglu_bwd_1": [(2, W_IN)],
    "d_ffn_up_1": [(1, ("ffn_down", "ffn_conv"))],
    "d_hn2_1": [(1, PLE)],
    "swa_bwd_1": [(1, ("ffn_up", "c_w_out"))],
    "sb_bwd_0": [(1, W_IN), (0, REST)],
    "scatter_last": [(0, W_IN)],
}


class _StepExchanges:
    def __init__(self, shards, W):
        self.shards, self.W = shards, W
        self.stacks = {}
        for k, axis in SHARDED:
            shp = shards[k].shape
            self.W[k] = lax.empty(tuple(4 * d if i == axis else d for i, d in enumerate(shp)), shards[k].dtype)
            self.stacks[k] = lax.empty((4,) + shp, shards[k].dtype)

    @staticmethod
    def _select(plan):
        idx = {}
        for layer, fams in plan:
            for k, _ in SHARDED:
                r = _family_range(k, layer, layer + 1)
                if r is not None and (fams is None or k in fams):
                    idx.setdefault(k, []).append(r[0])
        return [(k, axis, sorted(idx[k])) for k, axis in SHARDED if k in idx]

    def gather(self, call):
        sel = self._select(GATHER_PLAN.get(call, ()))
        if not sel:
            return None, None
        for _, _, ii in sel:
            assert ii == list(range(ii[0], ii[-1] + 1)), "one copy per family takes a contiguous layer range"
        names = [k for k, _, _ in sel]
        ex = _gather_exchange([self.shards[k] for k in names], [self.W[k] for k in names],
                              [a for _, a, _ in sel], [(ii[0], ii[-1] + 1) for _, _, ii in sel])
        return ex, (self.W, names)

    def scatter(self, call, G):
        sel = self._select(SCATTER_PLAN.get(call, ()))
        if not sel:
            return None, None
        assert all(len(ii) == 1 for _, _, ii in sel)
        names = [k for k, _, _ in sel]
        ex = _scatter_exchange([G[k][ii[0]] for k, _, ii in sel], [self.stacks[k] for k in names],
                               [a for _, a, _ in sel], [ii[0] for _, _, ii in sel])
        return ex, (self.stacks, names)

    @staticmethod
    def adopt(where, targets):
        if where is not None:
            book, names = where
            for k, t in zip(names, targets):
                book[k] = t


def _forward_backward(x, p, target, W, S, exchanges=None):
    T = x.shape[0]
    depth = p.shape[0]
    lb = _lower_bound_fwd(W["hg_lb_logits"], "lower_bound_fwd")
    bias = _bias_build(W["rel_bias"], "bias_build")
    qk_gain = jnp.concatenate([jnp.tile(W["q_norm"], (1, SW_HEADS)), jnp.tile(W["k_norm"], (1, SW_KV))], axis=1)

    def gathering(call):
        return exchanges.gather(call) if exchanges else (None, None)

    def scattering(call):
        return exchanges.scatter(call, G) if exchanges else (None, None)

    def matmul(a, b, mode, name, **kw):
        ex, where = (gathering(name) if name in GATHER_PLAN else
                     scattering(name) if name in SCATTER_PLAN else (None, None))
        out, moved = _matmul(a, b, mode, name, exchange=ex, **kw)
        _StepExchanges.adopt(where, moved)
        return out

    def mm(a, wname, layer, mode, name, **kw):
        return matmul(a, W[wname], mode, name, b_layer=layer, **kw)

    saved = []
    h = x
    for i in range(depth):
        j = i // 2
        s = {"h0": h}
        s["hn"] = _rmsnorm_fwd(h, W["mix_norm"][i:i + 1], f"mix_norm_fwd_{i}")
        if i % 2 == 0:
            s["proj"] = mm(s["hn"], "ab_w_in", j, "nn", f"ab_in_{i}")
            ex, where = gathering(f"sb_fwd_{i}")
            cat, s["sb_tot"], arrived = _sb_fwd(s["proj"], S, f"sb_fwd_{i}", ex)
            _StepExchanges.adopt(where, arrived)
            ex, where = gathering(f"hgrn_fwd_{i}")
            s["cat"], s["oraw"], s["states"], arrived = _hgrn_fwd(s["proj"], cat, lb[j:j + 1],
                                                                  W["hg_out_norm"][j:j + 1], S, f"hgrn_fwd_{i}", ex)
            _StepExchanges.adopt(where, arrived)
            h = mm(s["cat"], "ab_w_out", j, "nn", f"ab_out_{i}", res=h)
        else:
            s["proj"] = mm(s["hn"], "c_w_in", j, "nn", f"c_in_{i}")
            s["qkn"] = _headnorm_fwd(s["proj"], qk_gain[j:j + 1], f"qk_norm_fwd_{i}")
            ex, where = gathering(f"swa_fwd_{i}")
            s["o"], arrived = _swa_fwd(s["qkn"], s["proj"], bias, W["sinks"][j], S, f"swa_fwd_{i}", ex)
            _StepExchanges.adopt(where, arrived)
            h = mm(s["o"], "c_w_out", j, "nn", f"c_out_{i}", res=h)
        s["h1"] = h
        s["hn2"] = _rmsnorm_fwd(h, W["ffn_norm"][i:i + 1], f"ffn_norm_fwd_{i}")
        s["u"] = mm(s["hn2"], "ffn_up", i, "nn", f"ffn_up_{i}")
        s["a"] = _convglu_fwd(s["u"], W["ffn_conv"][i], W["ffn_conv_b"][i:i + 1], S, f"convglu_fwd_{i}")
        h = mm(s["a"], "ffn_down", i, "nn", f"ffn_down_{i}", res=h)
        s["h2"] = h
        s["hn3"] = _rmsnorm_fwd(h, W["ple_norm"][i:i + 1], f"ple_norm_fwd_{i}")
        s["z"] = mm(s["hn3"], "ple_gate", i, "nn", f"ple_gate_{i}")
        s["pp"] = mm(p, "ple_proj", i, "nn", f"ple_proj_{i}", a_layer=i)
        h = _ple_fwd(h, s["z"], s["pp"], f"ple_fwd_{i}")
        saved.append(s)

    loss, dh = _loss_fwd_bwd(h, target, "loss")

    G = {k: [None] * depth for k in ("mix_norm", "ffn_norm", "ple_norm", "ffn_up", "ffn_conv", "ffn_conv_b",
                                     "ffn_down", "ple_gate", "ple_proj")}
    for k in ("ab_w_in", "ab_w_out", "c_w_in", "c_w_out", "hg_out_norm", "q_norm", "k_norm", "sinks", "lb"):
        G[k] = [None] * (depth // 2)
    dbias_total = None
    for i in reversed(range(depth)):
        j = i // 2
        s = saved[i]
        dz, dpp = _ple_bwd(dh, s["z"], s["pp"], f"ple_bwd_{i}")
        G["ple_proj"][i] = matmul(p, dpp, "tn", f"d_ple_proj_{i}", out_dtype=BF16, a_layer=i)
        G["ple_gate"][i] = matmul(s["hn3"], dz, "tn", f"d_ple_gate_{i}", out_dtype=BF16)
        dhn = mm(dz, "ple_gate", i, "nt", f"d_hn3_{i}")
        dh, G["ple_norm"][i] = _rmsnorm_bwd(s["h2"], W["ple_norm"][i:i + 1], dhn, dh, f"ple_norm_bwd_{i}")

        half_ff = D_FF // 2
        da = mm(dh, "ffn_down", i, "nt", f"d_a_{i}", tiles=(min(T, 1024), half_ff, D_MODEL))
        G["ffn_down"][i] = matmul(s["a"], dh, "tn", f"d_ffn_down_{i}", out_dtype=BF16,
                                   tiles=(half_ff, 512, min(T, 2048)))
        ex, where = scattering(f"convglu_bwd_{i}")
        du, dcw, dcb, sent = _convglu_bwd(s["u"], da, W["ffn_conv"][i], W["ffn_conv_b"][i:i + 1], S,
                                          f"convglu_bwd_{i}", ex)
        _StepExchanges.adopt(where, sent)
        G["ffn_conv"][i] = jnp.swapaxes(dcw, 0, 1).reshape(3, 2 * D_FF)
        G["ffn_conv_b"][i] = dcb.reshape(1, 2 * D_FF)
        G["ffn_up"][i] = matmul(s["hn2"], du, "tn", f"d_ffn_up_{i}", out_dtype=BF16,
                                 tiles=(D_MODEL, half_ff, min(T, 1024)))
        dhn = mm(du, "ffn_up", i, "nt", f"d_hn2_{i}")
        dh, G["ffn_norm"][i] = _rmsnorm_bwd(s["h1"], W["ffn_norm"][i:i + 1], dhn, dh, f"ffn_norm_bwd_{i}")

        if i % 2 == 0:
            dcat = mm(dh, "ab_w_out", j, "nt", f"d_cat_{i}")
            G["ab_w_out"][j] = matmul(s["cat"], dh, "tn", f"d_ab_out_{i}", out_dtype=BF16)
            ex, where = scattering(f"sb_bwd_{i}")
            dproj, sent = _sb_bwd(s["proj"], s["sb_tot"], dcat, S, f"sb_bwd_{i}", ex)
            _StepExchanges.adopt(where, sent)
            dproj, G["lb"][j], G["hg_out_norm"][j] = _hgrn_bwd(s["proj"], s["oraw"], dcat, s["states"], lb[j:j + 1],
                                                               W["hg_out_norm"][j:j + 1], dproj, S, f"hgrn_bwd_{i}")
            G["ab_w_in"][j] = matmul(s["hn"], dproj, "tn", f"d_ab_in_{i}", out_dtype=BF16)
            dhn = mm(dproj, "ab_w_in", j, "nt", f"d_hn_{i}")
        else:
            do = mm(dh, "c_w_out", j, "nt", f"d_o_{i}")
            G["c_w_out"][j] = matmul(s["o"], dh, "tn", f"d_c_out_{i}", out_dtype=BF16)
            ex, where = scattering(f"swa_bwd_{i}")
            dqkv, dbias, dsink, sent = _swa_bwd(s["qkn"], s["proj"], bias, W["sinks"][j], do, S, f"swa_bwd_{i}", ex)
            _StepExchanges.adopt(where, sent)
            dbias_total = dbias if dbias_total is None else dbias_total + dbias
            G["sinks"][j] = dsink[:, 0]
            dproj, dgain = _headnorm_bwd(s["proj"], qk_gain[j:j + 1], dqkv, f"qk_norm_bwd_{i}")
            G["q_norm"][j] = dgain[0, :SW_Q_COLS].reshape(SW_HEADS, SW_DIM).sum(axis=0)
            G["k_norm"][j] = dgain[0, SW_Q_COLS:].reshape(SW_KV, SW_DIM).sum(axis=0)
            G["c_w_in"][j] = matmul(s["hn"], dproj, "tn", f"d_c_in_{i}", out_dtype=BF16)
            dhn = mm(dproj, "c_w_in", j, "nt", f"d_hn_{i}")
        dh, G["mix_norm"][i] = _rmsnorm_bwd(s["h0"], W["mix_norm"][i:i + 1], dhn, dh, f"mix_norm_bwd_{i}")

    grads = {k: G[k] for k, _ in SHARDED}
    for k in ("q_norm", "k_norm", "sinks"):
        grads[k] = jnp.stack(G[k])
    for k in ("mix_norm", "ffn_norm", "ple_norm", "ffn_conv_b", "hg_out_norm"):
        grads[k] = jnp.concatenate(G[k], axis=0)
    grads["hg_lb_logits"] = _lower_bound_bwd(W["hg_lb_logits"], jnp.concatenate(G["lb"], axis=0), "lower_bound_bwd")
    grads["rel_bias"] = _bias_reduce(dbias_total, "bias_reduce")[:, 0, :N_BUCKETS].T
    return loss, dh, grads


def kernel(x, p, mix_norm, ab_w_in, hg_lb_logits, hg_out_norm, ab_w_out, c_w_in, q_norm, k_norm, sinks, rel_bias, c_w_out, ffn_norm, ffn_up, ffn_conv, ffn_conv_b, ffn_down, ple_norm, ple_gate, ple_proj, loss_target, m_mix_norm, m_ab_w_in, m_hg_lb_logits, m_hg_out_norm, m_ab_w_out, m_c_w_in, m_q_norm, m_k_norm, m_sinks, m_rel_bias, m_c_w_out, m_ffn_norm, m_ffn_up, m_ffn_conv, m_ffn_conv_b, m_ffn_down, m_ple_norm, m_ple_gate, m_ple_proj, v_mix_norm, v_ab_w_in, v_hg_lb_logits, v_hg_out_norm, v_ab_w_out, v_c_w_in, v_q_norm, v_k_norm, v_sinks, v_rel_bias, v_c_w_out, v_ffn_norm, v_ffn_up, v_ffn_conv, v_ffn_conv_b, v_ffn_down, v_ple_norm, v_ple_gate, v_ple_proj):
    args = dict(locals())
    w = {k: args[k] for k in WEIGHTS}
    m = {k: args["m_" + k] for k in WEIGHTS}
    v = {k: args["v_" + k] for k in WEIGHTS}
    B, S, Dm = x.shape
    T = B * S
    names = [k for k, _ in SHARDED]

    W = {k: w[k] for k in SMALL}
    exchanges = _StepExchanges({k: w[k].astype(F32 if k == "ffn_conv" else BF16) for k in names}, W)
    first, where = exchanges.gather("gather_first")
    exchanges.adopt(where, _exchange_call(first, "gather_first"))

    loss, dx, grads = _forward_backward(x.reshape(T, Dm), p.reshape(p.shape[0], T, p.shape[-1]),
                                        loss_target.reshape(T, Dm), W, S, exchanges)
    loss = lax.psum(loss[0, 0], ("x", "y", "c"))

    last, where = exchanges.scatter("scatter_last", grads)
    exchanges.adopt(where, _exchange_call(last, "scatter_last"))
    stacks = [exchanges.stacks[k] for k in names]
    partial = [_sum4(st.reshape(4, -1, st.shape[-1]), f"sum_chips_{k}") for k, st in zip(names, stacks)]
    other = _swap_with_sibling(partial, "swap_core_sums")
    small_sum = _allreduce_small(_pack([grads[k] for k in SMALL]), "allreduce_small")

    out_g, out_d, out_m, out_v = {}, {}, {}, {}
    for k, mine, theirs in zip(names, partial, other):
        shp = w[k].shape
        r = [a.reshape(shp) for a in _adamw(w[k].reshape(mine.shape), m[k].reshape(mine.shape),
                                            v[k].reshape(mine.shape), [mine, theirs], f"adamw_{k}")]
        out_g[k], out_d[k], out_m[k], out_v[k] = r
    sm = _adamw(_pack([w[k] for k in SMALL]), _pack([m[k] for k in SMALL]), _pack([v[k] for k in SMALL]),
                [small_sum], "adamw_small")
    like = [w[k] for k in SMALL]
    for dst, packed in zip((out_g, out_d, out_m, out_v), sm):
        for k, a in zip(SMALL, _unpack(packed, like)):
            dst[k] = a

    return (loss, dx.reshape(B, S, Dm), *[out_g[k] for k in WEIGHTS], *[out_d[k] for k in WEIGHTS],
            *[out_m[k] for k in WEIGHTS], *[out_v[k] for k in WEIGHTS])
```

```python
import math

import numpy as np
import jax
import jax.numpy as jnp
from jax import lax
from jax.experimental import pallas as pl
from jax.experimental.pallas import tpu as pltpu

F32 = jnp.float32
BF16 = jnp.bfloat16
MESH = pl.DeviceIdType.MESH
ANY = pl.BlockSpec(memory_space=pl.ANY)

D_MODEL = 1024
EPS = 1e-6
SB_HEADS, SB_DIM = 8, 64
HG_HEADS, HG_DK = 4, 128
HG_CHUNK = 32
HG_STEP = 2
HG_TRIP = 2
SW_HEADS, SW_KV, SW_DIM, WINDOW = 16, 4, 64, 128
N_BUCKETS, MAX_DISTANCE = 32, 128
D_FF = 2816
ATT_BLOCK = 128
SB_QBLOCK = 256
LANES = 128
NEG = -1e30

ADAM_LR, ADAM_B1, ADAM_B2, ADAM_EPS, ADAM_WD, ADAM_STEP = 0.001, 0.9, 0.999, 1e-08, 0.01, 10

VMEM_LIMIT = 56 * 1024 * 1024


def _cparams(*sem):
    return pltpu.CompilerParams(dimension_semantics=sem, vmem_limit_bytes=VMEM_LIMIT)


def _pick(n, cap):
    if n <= cap:
        return n
    best = None
    for d in range(LANES, cap + 1, LANES):
        if n % d == 0:
            best = d
    assert best is not None, (n, cap)
    return best


def _dot(a, b, ca, cb):
    return lax.dot_general(a.astype(BF16), b.astype(BF16), (((ca,), (cb,)), ((), ())),
                           preferred_element_type=F32)


def _split(x, terms):
    parts = []
    for _ in range(terms):
        hi = x.astype(BF16)
        parts.append(hi)
        x = x - hi.astype(F32)
    return parts


def _dot_exact_l(x, m, terms=2):
    out = None
    for p in _split(x, terms):
        t = lax.dot_general(p, m, (((1,), (0,)), ((), ())), preferred_element_type=F32)
        out = t if out is None else out + t
    return out


def _dot_exact_r(m, x, terms=3, cm=1):
    out = None
    for p in _split(x, terms):
        t = lax.dot_general(m, p, (((cm,), (0,)), ((), ())), preferred_element_type=F32)
        out = t if out is None else out + t
    return out


def _sig(x):
    return 1.0 / (1.0 + jnp.exp(-x))


def _iota2(shape, dim):
    return lax.broadcasted_iota(jnp.int32, shape, dim)


def _operand_spec(arr, layer, blk, index):
    if arr.ndim == 2:
        return pl.BlockSpec(blk, index)
    if layer is not None:
        return pl.BlockSpec((None,) + blk, lambda i, j, k: (layer,) + index(i, j, k))
    per_half = arr.shape[2] // blk[1]

    def halves(i, j, k):
        r, c = index(i, j, k)
        return (c // per_half, r, c % per_half)

    return pl.BlockSpec((None,) + blk, halves)


def _matmul(a, b, mode, name, out_dtype=F32, res=None, a_layer=None, b_layer=None, tiles=None, exchange=None):
    def dims(arr, layer):
        if arr.ndim == 2:
            return arr.shape
        return arr.shape[1:] if layer is not None else (arr.shape[1], 2 * arr.shape[2])

    (a0, a1), (b0, b1) = dims(a, a_layer), dims(b, b_layer)
    if mode == "nn":
        M, K, N = a0, a1, b1
    elif mode == "nt":
        M, K, N = a0, a1, b0
    else:
        K, M, N = a0, a1, b1
    cap_m, cap_n, cap_k = 1024, 1024, (1024 if mode == "tn" else 2048)
    tm, tn, tk = _pick(M, cap_m), _pick(N, cap_n), _pick(K, cap_k)
    if a.ndim == 3 and a_layer is None:
        if mode == "tn":
            tm = _pick(a.shape[2], cap_m)
        else:
            tk = _pick(a.shape[2], cap_k)
    if b.ndim == 3 and b_layer is None:
        if mode == "nt":
            tk = _pick(b.shape[2], cap_k)
        else:
            tn = _pick(b.shape[2], cap_n)
    if tiles is not None:
        tm, tn, tk = tiles
    assert M % tm == 0 and N % tn == 0 and K % tk == 0, (name, M, N, K, tm, tn, tk)
    nk = K // tk
    if mode == "tn":
        a_spec = _operand_spec(a, a_layer, (tk, tm), lambda i, j, k: (k, i))
    else:
        a_spec = _operand_spec(a, a_layer, (tm, tk), lambda i, j, k: (i, k))
    if mode == "nt":
        b_spec = _operand_spec(b, b_layer, (tn, tk), lambda i, j, k: (j, k))
    else:
        b_spec = _operand_spec(b, b_layer, (tk, tn), lambda i, j, k: (k, j))
    ca, cb = {"nn": (1, 0), "nt": (1, 1), "tn": (0, 0)}[mode]
    o_spec = pl.BlockSpec((tm, tn), lambda i, j, k: (i, j))

    grid = (M // tm, N // tn, nk)
    n_in = 2 if res is None else 3

    def body(*refs):
        own, carried = _carried(exchange, refs, n_in, 1, 1)
        _start_carried(exchange, carried, grid)
        if res is None:
            a_ref, b_ref, o_ref, acc = own
        else:
            a_ref, b_ref, r_ref, o_ref, acc = own
        k = pl.program_id(2)

        @pl.when(k == 0)
        def _():
            acc[...] = jnp.zeros_like(acc)

        acc[...] += _dot(a_ref[...], b_ref[...], ca, cb)

        @pl.when(k == nk - 1)
        def _():
            r = acc[...]
            if res is not None:
                r = r + r_ref[...]
            o_ref[...] = r.astype(out_dtype)

        _wait_carried(exchange, carried, grid)

    ins = [a, b] + ([] if res is None else [res])
    in_specs = [a_spec, b_spec] + ([] if res is None else [o_spec])
    if exchange is None:
        return pl.pallas_call(
            body, name=name, grid=grid, in_specs=in_specs, out_specs=o_spec,
            out_shape=jax.ShapeDtypeStruct((M, N), out_dtype), scratch_shapes=[pltpu.VMEM((tm, tn), F32)],
            compiler_params=_cparams("parallel", "parallel", "arbitrary"))(*ins), ()
    (out,), targets = _carrier_call(body, name, grid, in_specs, [o_spec], [jax.ShapeDtypeStruct((M, N), out_dtype)],
                                    [pltpu.VMEM((tm, tn), F32)], {}, ins, exchange)
    return out, targets


ROW_TILE = 512
HEAD_ROWS = 2048


def _row_spec(width):
    return pl.BlockSpec((ROW_TILE, width), lambda i: (i, 0))


def _vec_spec(width):
    return pl.BlockSpec((1, width), lambda i: (0, 0))


def _rmsnorm_fwd(h, g, name):
    T, Dm = h.shape

    def body(h_ref, g_ref, o_ref):
        x = h_ref[...]
        r = lax.rsqrt(jnp.mean(x * x, axis=1, keepdims=True) + EPS)
        o_ref[...] = (x * r * g_ref[...]).astype(BF16)

    return pl.pallas_call(
        body, name=name, grid=(T // ROW_TILE,), in_specs=[_row_spec(Dm), _vec_spec(Dm)],
        out_specs=_row_spec(Dm), out_shape=jax.ShapeDtypeStruct((T, Dm), BF16),
        compiler_params=_cparams("parallel"))(h, g)


def _rmsnorm_bwd(h, g, dhn, dres, name):
    T, Dm = h.shape

    def body(h_ref, g_ref, dy_ref, dr_ref, dh_ref, dg_ref):
        i = pl.program_id(0)
        x = h_ref[...]
        dy = dy_ref[...]
        r = lax.rsqrt(jnp.mean(x * x, axis=1, keepdims=True) + EPS)
        gdy = dy * g_ref[...]
        m = jnp.mean(x * gdy, axis=1, keepdims=True)
        dh_ref[...] = dr_ref[...] + r * gdy - x * (r * r * r * m)
        part = jnp.sum(dy * x * r, axis=0, keepdims=True)

        @pl.when(i == 0)
        def _():
            dg_ref[...] = part

        @pl.when(i > 0)
        def _():
            dg_ref[...] += part

    return pl.pallas_call(
        body, name=name, grid=(T // ROW_TILE,),
        in_specs=[_row_spec(Dm), _vec_spec(Dm), _row_spec(Dm), _row_spec(Dm)],
        out_specs=[_row_spec(Dm), _vec_spec(Dm)],
        out_shape=[jax.ShapeDtypeStruct((T, Dm), F32), jax.ShapeDtypeStruct((1, Dm), F32)],
        compiler_params=_cparams("arbitrary"))(h, g, dhn, dres)


def _ple_fwd(h, z, pp, name):
    T, Dm = h.shape

    def body(h_ref, z_ref, p_ref, o_ref):
        o_ref[...] = h_ref[...] + _sig(z_ref[...]) * p_ref[...]

    return pl.pallas_call(
        body, name=name, grid=(T // ROW_TILE,), in_specs=[_row_spec(Dm)] * 3, out_specs=_row_spec(Dm),
        out_shape=jax.ShapeDtypeStruct((T, Dm), F32), compiler_params=_cparams("parallel"))(h, z, pp)


def _ple_bwd(dh, z, pp, name):
    T, Dm = dh.shape

    def body(dh_ref, z_ref, p_ref, dz_ref, dp_ref):
        s = _sig(z_ref[...])
        d = dh_ref[...]
        dz_ref[...] = d * p_ref[...] * s * (1.0 - s)
        dp_ref[...] = d * s

    return pl.pallas_call(
        body, name=name, grid=(T // ROW_TILE,), in_specs=[_row_spec(Dm)] * 3, out_specs=[_row_spec(Dm)] * 2,
        out_shape=[jax.ShapeDtypeStruct((T, Dm), F32)] * 2, compiler_params=_cparams("parallel"))(dh, z, pp)


def _loss_fwd_bwd(y, target, name):
    T, Dm = y.shape

    def body(y_ref, t_ref, l_ref, d_ref):
        i = pl.program_id(0)
        e = y_ref[...] - t_ref[...]
        d_ref[...] = e * (1.0 / Dm)
        part = jnp.full((8, LANES), 0.5 / Dm, F32) * jnp.sum(e * e)

        @pl.when(i == 0)
        def _():
            l_ref[...] = part

        @pl.when(i > 0)
        def _():
            l_ref[...] += part

    return pl.pallas_call(
        body, name=name, grid=(T // ROW_TILE,), in_specs=[_row_spec(Dm)] * 2,
        out_specs=[pl.BlockSpec((8, LANES), lambda i: (0, 0)), _row_spec(Dm)],
        out_shape=[jax.ShapeDtypeStruct((8, LANES), F32), jax.ShapeDtypeStruct((T, Dm), F32)],
        compiler_params=_cparams("arbitrary"))(y, target)


def _head_mean_matrix():
    r = _iota2((LANES, LANES), 0) >= SW_DIM
    c = _iota2((LANES, LANES), 1) >= SW_DIM
    return jnp.where(r == c, 1.0 / SW_DIM, 0.0).astype(BF16)


def _headnorm_fwd(x, g_lane, name):
    T = x.shape[0]
    C = g_lane.shape[1]

    def body(x_ref, g_ref, y_ref):
        xv = x_ref[...]
        ms = _dot_exact_l(xv * xv, _head_mean_matrix())
        y_ref[...] = xv * lax.rsqrt(ms + EPS) * g_ref[...]

    rows = min(T, HEAD_ROWS)
    spec = pl.BlockSpec((rows, LANES), lambda j, i: (i, j))
    return pl.pallas_call(
        body, name=name, grid=(C // LANES, T // rows),
        in_specs=[spec, pl.BlockSpec((1, LANES), lambda j, i: (0, j))], out_specs=spec,
        out_shape=jax.ShapeDtypeStruct((T, C), F32), compiler_params=_cparams("parallel", "parallel"))(x, g_lane)


def _headnorm_bwd(x, g_lane, dy_full, name):
    T = x.shape[0]
    C = g_lane.shape[1]

    def body(x_ref, g_ref, dy_ref, dx_ref, dg_ref):
        i = pl.program_id(1)
        xv = x_ref[...]
        dy = dy_ref[...]
        bd = _head_mean_matrix()
        r = lax.rsqrt(_dot_exact_l(xv * xv, bd) + EPS)
        gdy = dy * g_ref[...]
        m = _dot_exact_l(xv * gdy, bd)
        dx_ref[...] = r * gdy - xv * (r * r * r * m)
        part = jnp.sum(dy * xv * r, axis=0, keepdims=True)

        @pl.when(i == 0)
        def _():
            dg_ref[...] = part

        @pl.when(i > 0)
        def _():
            dg_ref[...] += part

    rows = min(T, HEAD_ROWS)
    spec = pl.BlockSpec((rows, LANES), lambda j, i: (i, j))
    vspec = pl.BlockSpec((1, LANES), lambda j, i: (0, j))
    return pl.pallas_call(
        body, name=name, grid=(C // LANES, T // rows), in_specs=[spec, vspec, spec],
        out_specs=[spec, vspec],
        out_shape=[jax.ShapeDtypeStruct(dy_full.shape, F32), jax.ShapeDtypeStruct((1, C), F32)],
        input_output_aliases={2: 0}, compiler_params=_cparams("parallel", "arbitrary"))(x, g_lane, dy_full)


CONV_TILE = 128


def _shift_down(x, k, rows):
    return jnp.where(rows >= k, pltpu.roll(x, k, 0), 0.0)


def _shift_up(x, k, rows):
    n = x.shape[0]
    return jnp.where(rows < n - k, pltpu.roll(x, n - k, 0), 0.0)


def _conv3(u, w_ref, b_ref, rows):
    return (w_ref[2:3, :] * u + w_ref[1:2, :] * _shift_down(u, 1, rows) + w_ref[0:1, :] * _shift_down(u, 2, rows)
            + b_ref[...])


def _convglu_fwd(u, cw, cb, S, name):
    T = u.shape[0]
    nf = D_FF // CONV_TILE

    def body(ug_ref, uu_ref, wg_ref, wu_ref, bg_ref, bu_ref, a_ref):
        rows = _iota2((S, CONV_TILE), 0)
        yg = _conv3(ug_ref[...], wg_ref, bg_ref, rows)
        yu = _conv3(uu_ref[...], wu_ref, bu_ref, rows)
        a_ref[...] = (yg * _sig(yg) * yu).astype(BF16)

    def blk(rows, off):
        return pl.BlockSpec((rows, CONV_TILE), (lambda b, j: (b, j + off)) if rows == S else (lambda b, j: (0, j + off)))

    return pl.pallas_call(
        body, name=name, grid=(T // S, nf),
        in_specs=[blk(S, 0), blk(S, nf), blk(3, 0), blk(3, nf), blk(1, 0), blk(1, nf)],
        out_specs=blk(S, 0), out_shape=jax.ShapeDtypeStruct((T, D_FF), BF16),
        compiler_params=_cparams("parallel", "parallel"))(u, u, cw, cw, cb, cb)


def _convglu_bwd(u, da, cw, cb, S, name, exchange=None):
    T = u.shape[0]
    nf = D_FF // CONV_TILE

    grid = (nf, T // S)

    def body(*refs):
        (ug_ref, uu_ref, da_ref, wg_ref, wu_ref, bg_ref, bu_ref, du_ref, dw_ref, db_ref), carried = _carried(
            exchange, refs, 7, 3, 0)
        _start_carried(exchange, carried, grid)
        b = pl.program_id(1)
        rows = _iota2((S, CONV_TILE), 0)
        ug, uu = ug_ref[...], uu_ref[...]
        yg = _conv3(ug, wg_ref, bg_ref, rows)
        yu = _conv3(uu, wu_ref, bu_ref, rows)
        s = _sig(yg)
        da_v = da_ref[...]
        for half, (uv, w_ref, dy) in enumerate(((ug, wg_ref, da_v * yu * (s * (1.0 + yg * (1.0 - s)))),
                                                (uu, wu_ref, da_v * yg * s))):
            up1, up2 = _shift_up(dy, 1, rows), _shift_up(dy, 2, rows)
            du_ref[half] = w_ref[2:3, :] * dy + w_ref[1:2, :] * up1 + w_ref[0:1, :] * up2
            dws = [jnp.sum(up2 * uv, axis=0, keepdims=True), jnp.sum(up1 * uv, axis=0, keepdims=True),
                   jnp.sum(dy * uv, axis=0, keepdims=True)]
            dbv = jnp.sum(dy, axis=0, keepdims=True)

            @pl.when(b == 0)
            def _():
                for k in range(3):
                    dw_ref[half, k:k + 1, :] = dws[k]
                db_ref[half] = dbv

            @pl.when(b > 0)
            def _():
                for k in range(3):
                    dw_ref[half, k:k + 1, :] += dws[k]
                db_ref[half] += dbv

        _wait_carried(exchange, carried, grid)

    def blk(rows, off):
        return pl.BlockSpec((rows, CONV_TILE), (lambda j, b: (b, j + off)) if rows == S else (lambda j, b: (0, j + off)))

    def both(rows):
        return pl.BlockSpec((2, rows, CONV_TILE), (lambda j, b: (0, b, j)) if rows == S else (lambda j, b: (0, 0, j)))

    outs, targets = _carrier_call(
        body, name, grid, [blk(S, 0), blk(S, nf), blk(S, 0), blk(3, 0), blk(3, nf), blk(1, 0), blk(1, nf)],
        [both(S), both(3), both(1)],
        [jax.ShapeDtypeStruct((2, T, D_FF), F32), jax.ShapeDtypeStruct((2, 3, D_FF), F32),
         jax.ShapeDtypeStruct((2, 1, D_FF), F32)], [], {}, (u, u, da, cw, cw, cb, cb), exchange)
    return (*outs, targets)


def _sb_scores(qb, kblk, on_diag_mask):
    z = _dot(qb, kblk, 1, 1) * (SB_DIM ** -0.5)
    l1 = jnp.log(1.0 + jnp.exp(-jnp.abs(z)))
    ls = jnp.minimum(z, 0.0) - l1
    lk = jnp.where(on_diag_mask, ls - z, 0.0)
    return ls, lk


def _sb_fwd(proj, S, name, exchange=None):
    T = proj.shape[0]
    BQ, BK = SB_QBLOCK, ATT_BLOCK
    unroll = BQ // BK
    nq = S // BQ
    nhp = SB_HEADS // 2
    heads = [slice(h * SB_DIM, (h + 1) * SB_DIM) for h in range(2)]

    grid = (T // S, nhp)

    def body(*refs):
        (q_ref, k_ref, v_ref, o_ref, tot_ref), carried = _carried(exchange, refs, 3, 2, 0)
        _start_carried(exchange, carried, grid)
        ahead = _iota2((BQ, BK), 1) - _iota2((BQ, BK), 0)
        upper = (_iota2((BK, BK), 0) > _iota2((BK, BK), 1)).astype(BF16)

        def qloop(iq, carry):
            q0 = pl.multiple_of(iq * BQ, BQ)
            rows = pl.ds(q0, BQ)
            qbs = [q_ref[rows, sl] for sl in heads]
            nkb = (iq + 1) * (BQ // BK)

            def kloop(jj, kc):
                blocks = []
                for u in range(unroll):
                    k0 = pl.multiple_of((nkb - 1 - unroll * jj - u) * BK, BK)
                    blocks.append((pl.ds(k0, BK), ahead < q0 - k0))
                units = [(h, krows, mask) for krows, mask in blocks for h in range(2)]
                scores = [_sb_scores(qbs[h], k_ref[krows, heads[h]], mask) for h, krows, mask in units]
                laters = [_dot_exact_l(lk, upper) for _, lk in scores]
                runs = [kc[h][0] for h in range(2)]
                accs = [kc[h][1] for h in range(2)]
                for (h, krows, mask), (ls, lk), later in zip(units, scores, laters):
                    w = jnp.where(mask, jnp.exp(ls + later + runs[h]), 0.0)
                    accs[h] = accs[h] + _dot(w, v_ref[krows, heads[h]], 1, 0)
                    runs[h] = runs[h] + jnp.sum(lk, axis=1, keepdims=True)
                return tuple((runs[h], accs[h]) for h in range(2))

            init = (jnp.zeros((BQ, 1), F32), jnp.zeros((BQ, SB_DIM), F32))
            res = lax.fori_loop(0, nkb // unroll, kloop, (init, init))
            for h, (sl, (run, acc)) in enumerate(zip(heads, res)):
                o_ref[rows, sl] = acc
                tot_ref[rows, h:h + 1] = run
            return carry

        lax.fori_loop(0, nq, qloop, 0)
        _wait_carried(exchange, carried, grid)

    def spec(off):
        return pl.BlockSpec((S, LANES), lambda b, hp: (b, hp + off))

    (cat, tot), targets = _carrier_call(
        body, name, grid, [spec(0), spec(nhp), spec(2 * nhp)],
        [spec(0), pl.BlockSpec((None, S, 2), lambda b, hp: (hp, b, 0))],
        [jax.ShapeDtypeStruct((T, 2 * SB_HEADS * SB_DIM), F32), jax.ShapeDtypeStruct((nhp, T, 2), F32)],
        [], {}, (proj, proj, proj), exchange)
    return cat, tot, targets


def _sb_bwd(proj, tot, dcat, S, name, exchange=None):
    T, width = proj.shape
    BQ, BK = SB_QBLOCK, ATT_BLOCK
    unroll = BQ // BK
    nq = S // BQ
    nhp = SB_HEADS // 2
    scale = SB_DIM ** -0.5
    heads = [slice(h * SB_DIM, (h + 1) * SB_DIM) for h in range(2)]

    grid = (T // S, nhp)

    def body(*refs):
        (q_ref, k_ref, v_ref, tot_ref, do_ref, dp_hbm, dq_s, dk_s, dv_s, sems), carried = _carried(
            exchange, refs, 5, 1, 4)
        _start_carried(exchange, carried, grid)
        b, hp = pl.program_id(0), pl.program_id(1)
        ahead = _iota2((BQ, BK), 1) - _iota2((BQ, BK), 0)
        r, c = _iota2((BK, BK), 0), _iota2((BK, BK), 1)
        upto = (r <= c).astype(BF16)
        earlier = (r < c).astype(BF16)
        dk_s[...] = jnp.zeros_like(dk_s)
        dv_s[...] = jnp.zeros_like(dv_s)

        def qloop(iq, carry):
            q0 = pl.multiple_of(iq * BQ, BQ)
            rows = pl.ds(q0, BQ)
            qbs = [q_ref[rows, sl] for sl in heads]
            dobs = [do_ref[rows, sl] for sl in heads]
            totals = [tot_ref[rows, h:h + 1] for h in range(2)]

            def kloop(m, kc):
                blocks = []
                for u in range(unroll):
                    k0 = pl.multiple_of((unroll * m + u) * BK, BK)
                    blocks.append((pl.ds(k0, BK), ahead < q0 - k0))
                units = [(h, krows, mask) for krows, mask in blocks for h in range(2)]
                kblks = [k_ref[krows, heads[h]] for h, krows, _ in units]
                scores = [_sb_scores(qbs[h], kblk, mask) for (h, _, mask), kblk in zip(units, kblks)]
                prefixes = [_dot_exact_l(lk, upto) for _, lk in scores]
                dws = [_dot(dobs[h], v_ref[krows, heads[h]], 1, 1) for h, krows, _ in units]
                runs = [kc[h][0] for h in range(2)]
                gruns = [kc[h][1] for h in range(2)]
                dqs = [kc[h][2] for h in range(2)]
                ws, gs = [], []
                for (h, _, mask), (ls, lk), prefix, dw in zip(units, scores, prefixes, dws):
                    w = jnp.where(mask, jnp.exp(ls + (totals[h] - (prefix + runs[h]))), 0.0)
                    runs[h] = runs[h] + jnp.sum(lk, axis=1, keepdims=True)
                    ws.append(w)
                    gs.append(w * dw)
                gprefixes = [_dot_exact_l(g, earlier) for g in gs]
                dzs = []
                for (h, _, mask), (ls, _), g, gprefix in zip(units, scores, gs, gprefixes):
                    beta = jnp.exp(ls)
                    dzs.append(jnp.where(mask, g * (1.0 - beta) - beta * (gprefix + gruns[h]), 0.0) * scale)
                    gruns[h] = gruns[h] + jnp.sum(g, axis=1, keepdims=True)
                for (h, krows, _), kblk, w, dz in zip(units, kblks, ws, dzs):
                    dv_s[krows, heads[h]] += _dot(w, dobs[h], 0, 0)
                    dk_s[krows, heads[h]] += _dot(dz, qbs[h], 0, 0)
                    dqs[h] = dqs[h] + _dot(dz, kblk, 1, 0)
                return tuple((runs[h], gruns[h], dqs[h]) for h in range(2))

            zero = jnp.zeros((BQ, 1), F32)
            init = (zero, zero, jnp.zeros((BQ, SB_DIM), F32))
            res = lax.fori_loop(0, (iq + 1) * (BQ // BK) // unroll, kloop, (init, init))
            for sl, (_, _, dq) in zip(heads, res):
                dq_s[rows, sl] = dq
            return carry

        lax.fori_loop(0, nq, qloop, 0)
        r0 = pl.multiple_of(b * S, S)
        copies = []
        for n, buf in enumerate((dq_s, dk_s, dv_s)):
            c0 = pl.multiple_of((hp + n * nhp) * LANES, LANES)
            copies.append(pltpu.make_async_copy(buf, dp_hbm.at[pl.ds(r0, S), pl.ds(c0, LANES)], sems.at[n]))
        for cp in copies:
            cp.start()
        for cp in copies:
            cp.wait()
        _wait_carried(exchange, carried, grid)

    def spec(off):
        return pl.BlockSpec((S, LANES), lambda b, hp: (b, hp + off))

    (dproj,), targets = _carrier_call(
        body, name, grid,
        [spec(0), spec(nhp), spec(2 * nhp), pl.BlockSpec((None, S, 2), lambda b, hp: (hp, b, 0)), spec(0)],
        [ANY], [jax.ShapeDtypeStruct((T, width), F32)],
        [pltpu.VMEM((S, LANES), F32)] * 3 + [pltpu.SemaphoreType.DMA((3,))], {}, (proj, proj, proj, tot, dcat),
        exchange)
    return dproj, targets


HG_COL0 = 3 * SB_HEADS * SB_DIM // LANES


def _hg_gates(q, fp, lbv):
    sg = _sig(fp)
    f = lbv + (1.0 - lbv) * sg
    kk = (1.0 - lbv) * _sig(-fp)
    sq = _sig(q)
    return sg, f, kk, sq


def _hg_chunk(qs, kk, lf, incl):
    C = HG_CHUNK
    b = _dot_exact_r(incl, lf)
    bl = b[C - 1:C, :]
    bm = b[C // 2 - 1:C // 2, :]
    e_t = jnp.exp(b - bm)
    e_s = jnp.exp(bm - b)
    e_i = jnp.exp(b)
    e_e = jnp.exp(bl - b)
    return bl, e_t, e_s, e_i, e_e


def _hgrn_fwd(proj, cat, lb, hgn, S, name, exchange=None):
    T = proj.shape[0]
    B = T // S
    C = HG_CHUNK
    NC = S // C

    grid = (B, HG_HEADS // HG_STEP)

    def body(*refs):
        (q_ref, f_ref, i_ref, g_ref, lb_ref, hgn_ref, _, ob_ref, oraw_ref, st_ref, state), carried = _carried(
            exchange, refs, 7, 3, 1)
        _start_carried(exchange, carried, grid)
        state[...] = jnp.zeros_like(state)
        row, col = _iota2((C, C), 0), _iota2((C, C), 1)
        causal = row >= col
        incl = causal.astype(BF16)

        def trip(m, carry):
            units = []
            for u in range(HG_TRIP):
                c = m * HG_TRIP + u
                rows = pl.ds(pl.multiple_of(c * C, C), C)
                units += [(c, hh, rows, slice(hh * LANES, (hh + 1) * LANES)) for hh in range(HG_STEP)]
            qs_, kks, lfs = [], [], []
            for _, _, rows, hs in units:
                q = q_ref[rows, hs]
                _, f, kk, sq = _hg_gates(q, f_ref[rows, hs], lb_ref[:, hs])
                qs_.append(q * sq)
                kks.append(kk)
                lfs.append(jnp.log(f))
            decays = [_hg_chunk(None, None, lf, incl) for lf in lfs]
            ps = [jnp.where(causal, _dot(qs * e_t, kk * e_s, 1, 1), 0.0)
                  for qs, kk, (_, e_t, e_s, _, _) in zip(qs_, kks, decays)]
            outs = []
            for (c, hh, rows, hs), qs, kk, (bl, _, _, e_i, e_e), p in zip(units, qs_, kks, decays, ps):
                iv = i_ref[rows, hs]
                st = state[hh]
                st_ref[0, hh, c] = st
                outs.append(_dot(qs * e_i, st, 1, 1) + _dot(p, iv, 1, 0))
                state[hh] = st * jnp.exp(bl) + _dot(iv, kk * e_e, 0, 0)
            for (_, _, rows, hs), o in zip(units, outs):
                gv = g_ref[rows, hs]
                oraw_ref[rows, hs] = o
                r = lax.rsqrt(jnp.mean(o * o, axis=1, keepdims=True) + EPS)
                ob_ref[rows, hs] = o * r * hgn_ref[...] * (gv * _sig(gv))
            return carry

        lax.fori_loop(0, NC // HG_TRIP, trip, 0)
        _wait_carried(exchange, carried, grid)

    width = HG_STEP * LANES
    col0 = HG_COL0 * LANES // width
    nstep = HG_HEADS // HG_STEP

    def spec(off):
        return pl.BlockSpec((S, width), lambda b, h: (b, h + off))

    outs, targets = _carrier_call(
        body, name, grid,
        [spec(col0), spec(col0 + nstep), spec(col0 + 2 * nstep), spec(col0 + 3 * nstep),
         pl.BlockSpec((1, width), lambda b, h: (0, h)), pl.BlockSpec((1, LANES), lambda b, h: (0, 0)), ANY],
        [spec(nstep), spec(0), pl.BlockSpec((1, HG_STEP, NC, LANES, LANES), lambda b, h: (b, h, 0, 0, 0))],
        [jax.ShapeDtypeStruct(cat.shape, F32), jax.ShapeDtypeStruct((T, HG_HEADS * LANES), F32),
         jax.ShapeDtypeStruct((B, HG_HEADS, NC, LANES, LANES), F32)],
        [pltpu.VMEM((HG_STEP, LANES, LANES), F32)], {6: 0}, (proj, proj, proj, proj, lb, hgn, cat), exchange)
    return (*outs, targets)


def _hgrn_bwd(proj, oraw, dcat, states, lb, hgn, dproj, S, name):
    T = proj.shape[0]
    B = T // S
    C = HG_CHUNK
    NC = S // C

    def body(q_ref, f_ref, i_ref, g_ref, oraw_ref, dy_ref, st_ref, lb_ref, hgn_ref, dp_in,
             dp_hbm, dlb_ref, dhgn_ref, dstate, dq_s, df_s, di_s, dg_s, sems):
        del dp_in
        h, b = pl.program_id(0), pl.program_id(1)
        row, col = _iota2((C, C), 0), _iota2((C, C), 1)
        causal = row >= col
        incl = causal.astype(BF16)
        last_row = _iota2((C, LANES), 0) == C - 1
        hg = hgn_ref[...]
        dstate[...] = jnp.zeros_like(dstate)

        @pl.when(b == 0)
        def _():
            dlb_ref[...] = jnp.zeros_like(dlb_ref)

        @pl.when(jnp.logical_and(b == 0, h == 0))
        def _():
            dhgn_ref[...] = jnp.zeros_like(dhgn_ref)

        def trip(m, carry):
            units = []
            for u in range(HG_TRIP):
                c = NC - 1 - (m * HG_TRIP + u)
                rows = pl.ds(pl.multiple_of(c * C, C), C)
                units += [(c, hh, rows, slice(hh * LANES, (hh + 1) * LANES)) for hh in range(HG_STEP)]
            dos = []
            dhgn = jnp.zeros((1, LANES), F32)
            for _, _, rows, hs in units:
                gv, o, dy = g_ref[rows, hs], oraw_ref[rows, hs], dy_ref[rows, hs]
                r = lax.rsqrt(jnp.mean(o * o, axis=1, keepdims=True) + EPS)
                on = o * r
                sgv = _sig(gv)
                silu_g = gv * sgv
                dg_s[rows, hs] = dy * on * hg * (sgv * (1.0 + gv * (1.0 - sgv)))
                dhgn = dhgn + jnp.sum(dy * on * silu_g, axis=0, keepdims=True)
                dn = dy * hg * silu_g
                dos.append(r * dn - o * (r * r * r * jnp.mean(o * dn, axis=1, keepdims=True)))
            dhgn_ref[...] += dhgn
            pre = []
            for (_, _, rows, hs), do in zip(units, dos):
                q, iv = q_ref[rows, hs], i_ref[rows, hs]
                sg, f, kk, sq = _hg_gates(q, f_ref[rows, hs], lb_ref[:, hs])
                pre.append((q, iv, sg, f, kk, sq, q * sq))
            decays = [_hg_chunk(None, None, jnp.log(f), incl) for _, _, _, f, _, _, _ in pre]
            prods = [(qs * e_t, kk * e_s, qs * e_i, kk * e_e)
                     for (_, _, _, _, kk, _, qs), (_, e_t, e_s, e_i, e_e) in zip(pre, decays)]
            ps = [jnp.where(causal, _dot(qd, kd, 1, 1), 0.0) for qd, kd, _, _ in prods]
            dps = [jnp.where(causal, _dot(do, iv, 1, 1), 0.0) for do, (_, iv, _, _, _, _, _) in zip(dos, pre)]
            dqds = [_dot(dp, kd, 1, 0) for dp, (_, kd, _, _) in zip(dps, prods)]
            dkds = [_dot(dp, qd, 0, 0) for dp, (qd, _, _, _) in zip(dps, prods)]
            pdos = [_dot(p, do, 0, 0) for p, do in zip(ps, dos)]
            chain = []
            for (c, hh, rows, hs), do, (_, iv, _, _, _, _, _), (bl, _, _, _, _), (_, _, qi, ke), pdo in zip(
                    units, dos, pre, decays, prods, pdos):
                st = st_ref[0, hh, c]
                dst = dstate[hh]
                ebl = jnp.exp(bl)
                dqi = _dot(do, st, 1, 0)
                di_s[rows, hs] = pdo + _dot(ke, dst, 1, 1)
                dke = _dot(iv, dst, 1, 0)
                dbl = jnp.sum(st * dst, axis=0, keepdims=True) * ebl + jnp.sum(dke * ke, axis=0, keepdims=True)
                dstate[hh] = _dot(do, qi, 0, 0) + dst * ebl
                chain.append((dqi, dke, dbl))
            for (_, _, rows, hs), (q, _, sg, f, _, sq, _), (_, e_t, e_s, e_i, e_e), (qd, kd, qi, ke), dqd, dkd, (
                    dqi, dke, dbl) in zip(units, pre, decays, prods, dqds, dkds, chain):
                lbv = lb_ref[:, hs]
                db = dqd * qd - dkd * kd + dqi * qi - dke * ke + jnp.where(last_row, dbl, 0.0)
                dqs = dqd * e_t + dqi * e_i
                dkk = dkd * e_s + dke * e_e
                dlf = _dot_exact_r(incl, db, cm=0)
                oms = 1.0 - sg
                dfd = dlf / f
                df_s[rows, hs] = (dfd - dkk) * (1.0 - lbv) * sg * oms
                dq_s[rows, hs] = dqs * (sq * (1.0 + q * (1.0 - sq)))
                dlb_ref[:, hs] += jnp.sum((dfd - dkk) * oms, axis=0, keepdims=True)
            return carry

        lax.fori_loop(0, NC // HG_TRIP, trip, 0)
        r0 = pl.multiple_of(b * S, S)
        copies = []
        for n, buf in enumerate((dq_s, df_s, di_s, dg_s)):
            c0 = pl.multiple_of((col0 + n * nstep + h) * width, width)
            copies.append(pltpu.make_async_copy(buf, dp_hbm.at[pl.ds(r0, S), pl.ds(c0, width)], sems.at[n]))
        for cp in copies:
            cp.start()
        for cp in copies:
            cp.wait()

    width = HG_STEP * LANES
    col0 = HG_COL0 * LANES // width
    nstep = HG_HEADS // HG_STEP

    def spec(off):
        return pl.BlockSpec((S, width), lambda h, b: (b, h + off))

    return pl.pallas_call(
        body, name=name, grid=(nstep, B),
        in_specs=[spec(col0), spec(col0 + nstep), spec(col0 + 2 * nstep), spec(col0 + 3 * nstep), spec(0), spec(nstep),
                  pl.BlockSpec((1, HG_STEP, NC, LANES, LANES), lambda h, b: (b, h, 0, 0, 0)),
                  pl.BlockSpec((1, width), lambda h, b: (0, h)), pl.BlockSpec((1, LANES), lambda h, b: (0, 0)), ANY],
        out_specs=[ANY, pl.BlockSpec((1, width), lambda h, b: (0, h)), pl.BlockSpec((1, LANES), lambda h, b: (0, 0))],
        out_shape=[jax.ShapeDtypeStruct(dproj.shape, F32), jax.ShapeDtypeStruct((1, HG_HEADS * LANES), F32),
                   jax.ShapeDtypeStruct((1, LANES), F32)],
        scratch_shapes=[pltpu.VMEM((HG_STEP, LANES, LANES), F32)] + [pltpu.VMEM((S, width), F32)] * 4
        + [pltpu.SemaphoreType.DMA((4,))],
        input_output_aliases={9: 0},
        compiler_params=_cparams("arbitrary", "arbitrary"))(proj, proj, proj, proj, oraw, dcat, states, lb, hgn, dproj)


def _lower_bound_fwd(logits, name):
    assert logits.shape[0] == 2

    def body(l_ref, o_ref):
        l0, l1 = l_ref[0:1, :], l_ref[1:2, :]
        m = jnp.maximum(l0, l1)
        e0, e1 = jnp.exp(l0 - m), jnp.exp(l1 - m)
        o_ref[0:1, :] = jnp.zeros_like(l0)
        o_ref[1:2, :] = e1 / (e0 + e1)

    return pl.pallas_call(body, name=name, out_shape=jax.ShapeDtypeStruct(logits.shape, F32))(logits)


def _lower_bound_bwd(logits, dlb, name):
    def body(l_ref, d_ref, o_ref):
        l0, l1 = l_ref[0:1, :], l_ref[1:2, :]
        m = jnp.maximum(l0, l1)
        e0, e1 = jnp.exp(l0 - m), jnp.exp(l1 - m)
        s1 = e1 / (e0 + e1)
        t = s1 * (1.0 - s1) * d_ref[1:2, :]
        o_ref[0:1, :] = -t
        o_ref[1:2, :] = t

    return pl.pallas_call(body, name=name, out_shape=jax.ShapeDtypeStruct(logits.shape, F32))(logits, dlb)


def _bucket_thresholds():
    dist = np.arange(WINDOW)
    max_exact = N_BUCKETS // 2
    large = max_exact + (np.log(np.maximum(dist, max_exact) / max_exact) / math.log(MAX_DISTANCE / max_exact)
                         * (N_BUCKETS - max_exact)).astype(np.int32)
    bucket = np.where(dist < max_exact, dist, np.minimum(large, N_BUCKETS - 1))
    assert np.all(np.diff(bucket) >= 0)
    return [int(np.argmax(bucket >= k)) if np.any(bucket >= k) else 10 ** 6 for k in range(1, N_BUCKETS)]


def _band_bucket():
    dist = _iota2((WINDOW, 2 * WINDOW), 0) + WINDOW - _iota2((WINDOW, 2 * WINDOW), 1)
    bucket = jnp.zeros((WINDOW, 2 * WINDOW), jnp.int32)
    for thr in _bucket_thresholds():
        bucket = bucket + (dist >= thr).astype(jnp.int32)
    band = jnp.logical_and(dist >= 0, dist < WINDOW)
    return bucket, band


def _bias_build(rel_bias, name):
    def body(rb_ref, o_ref):
        h = pl.program_id(0)
        bucket, _ = _band_bucket()
        bias = jnp.zeros((WINDOW, 2 * WINDOW), F32)
        for k in range(N_BUCKETS):
            bias = jnp.where(bucket == k, rb_ref[k, h], bias)
        o_ref[0] = bias

    return pl.pallas_call(
        body, name=name, grid=(SW_HEADS,), in_specs=[pl.BlockSpec(memory_space=pltpu.SMEM)],
        out_specs=pl.BlockSpec((1, WINDOW, 2 * WINDOW), lambda h: (h, 0, 0)),
        out_shape=jax.ShapeDtypeStruct((SW_HEADS, WINDOW, 2 * WINDOW), F32),
        compiler_params=_cparams("parallel"))(rel_bias)


def _bias_reduce(dbias, name):
    def body(d_ref, o_ref):
        bucket, band = _band_bucket()
        d = jnp.where(band, d_ref[0], 0.0)
        lane = _iota2((1, LANES), 1)
        out = jnp.zeros((1, LANES), F32)
        for k in range(N_BUCKETS):
            out = jnp.where(lane == k, jnp.sum(jnp.where(bucket == k, d, 0.0)), out)
        o_ref[0] = out

    return pl.pallas_call(
        body, name=name, grid=(SW_HEADS,), in_specs=[pl.BlockSpec((1, WINDOW, 2 * WINDOW), lambda h: (h, 0, 0))],
        out_specs=pl.BlockSpec((1, 1, LANES), lambda h: (h, 0, 0)),
        out_shape=jax.ShapeDtypeStruct((SW_HEADS, 1, LANES), F32), compiler_params=_cparams("parallel"))(dbias)


SW_Q_COLS = SW_HEADS * SW_DIM
SW_K_BLOCK0 = SW_Q_COLS // LANES
SW_V_BLOCK0 = SW_K_BLOCK0 + SW_KV * SW_DIM // LANES
SW_STEP_HEADS = 8


def _swa_logits(qb, kprev, kcur, bias_ref, hl, mprev, mcur):
    scale = SW_DIM ** -0.5
    lp = jnp.where(mprev, _dot(qb, kprev, 1, 1) * scale + bias_ref[hl, :, 0:WINDOW], NEG)
    lc = jnp.where(mcur, _dot(qb, kcur, 1, 1) * scale + bias_ref[hl, :, WINDOW:2 * WINDOW], NEG)
    return lp, lc


def _swa_softmax(lp, lc, sink):
    m = jnp.maximum(jnp.maximum(jnp.max(lp, axis=1, keepdims=True), jnp.max(lc, axis=1, keepdims=True)), sink)
    ep, ec = jnp.exp(lp - m), jnp.exp(lc - m)
    es = jnp.exp(sink - m)
    den = jnp.sum(ep, axis=1, keepdims=True) + jnp.sum(ec, axis=1, keepdims=True) + es
    return ep, ec, es, den


def _swa_block_heads(n, q_ref, k_ref, v_ref, bias_ref, sink_ref, kp, above, mcur):
    W = WINDOW
    rows = pl.ds(pl.multiple_of(n * W, W), W)
    prow = pl.ds(pl.multiple_of(jnp.maximum(n - 1, 0) * W, W), W)
    mprev = jnp.logical_and(above, n > 0)
    heads = []
    for kvh in range(2):
        ksl = slice(kvh * SW_DIM, (kvh + 1) * SW_DIM)
        kv = (k_ref[rows, ksl], k_ref[prow, ksl], v_ref[rows, ksl], v_ref[prow, ksl])
        for g in range(4):
            hl = kvh * 4 + g
            heads.append((hl, ksl, slice(hl * SW_DIM, (hl + 1) * SW_DIM), kv))
    qbs = [q_ref[rows, qsl] for _, _, qsl, _ in heads]
    logits = [_swa_logits(qb, kv[1], kv[0], bias_ref, hl, mprev, mcur) for qb, (hl, _, _, kv) in zip(qbs, heads)]
    soft = [_swa_softmax(lp, lc, sink_ref[kp * SW_STEP_HEADS + hl]) for (lp, lc), (hl, _, _, _) in zip(logits, heads)]
    return rows, prow, heads, qbs, soft


def _swa_fwd(qkn, proj, bias, sinks, S, name, exchange=None):
    T = qkn.shape[0]
    W = WINDOW
    nb = S // W

    grid = (T // S, 2)

    def body(*refs):
        (q_ref, k_ref, v_ref, bias_ref, sink_ref, o_ref), carried = _carried(exchange, refs, 5, 1, 0)
        _start_carried(exchange, carried, grid)
        kp = pl.program_id(1)
        row, col = _iota2((W, W), 0), _iota2((W, W), 1)
        mcur = col <= row
        above = col > row

        def blk(n, carry):
            rows, _, heads, _, soft = _swa_block_heads(n, q_ref, k_ref, v_ref, bias_ref, sink_ref, kp, above, mcur)
            outs = [(_dot(ep, kv[3], 1, 0) + _dot(ec, kv[2], 1, 0)) / den
                    for (ep, ec, _, den), (_, _, _, kv) in zip(soft, heads)]
            for (_, _, qsl, _), o in zip(heads, outs):
                o_ref[rows, qsl] = o
            return carry

        lax.fori_loop(0, nb, blk, 0)
        _wait_carried(exchange, carried, grid)

    (o,), targets = _carrier_call(
        body, name, grid,
        [pl.BlockSpec((S, 4 * LANES), lambda b, kp: (b, kp)),
         pl.BlockSpec((S, LANES), lambda b, kp: (b, SW_K_BLOCK0 + kp)),
         pl.BlockSpec((S, LANES), lambda b, kp: (b, SW_V_BLOCK0 + kp)),
         pl.BlockSpec((SW_STEP_HEADS, W, 2 * W), lambda b, kp: (kp, 0, 0)),
         pl.BlockSpec(memory_space=pltpu.SMEM)],
        [pl.BlockSpec((S, 4 * LANES), lambda b, kp: (b, kp))], [jax.ShapeDtypeStruct((T, SW_Q_COLS), F32)],
        [], {}, (qkn, qkn, proj, bias, sinks), exchange)
    return o, targets


def _swa_bwd(qkn, proj, bias, sinks, do, S, name, exchange=None):
    T, width = proj.shape
    W = WINDOW
    nb = S // W
    scale = SW_DIM ** -0.5

    grid = (2, T // S)

    def body(*refs):
        (q_ref, k_ref, v_ref, bias_ref, sink_ref, do_ref, dp_hbm, dbias_ref, dsink_ref,
         dq_s, dk_s, dv_s, sems), carried = _carried(exchange, refs, 6, 3, 4)
        _start_carried(exchange, carried, grid)
        kp, b = pl.program_id(0), pl.program_id(1)
        row, col = _iota2((W, W), 0), _iota2((W, W), 1)
        mcur = col <= row
        above = col > row
        dk_s[...] = jnp.zeros_like(dk_s)
        dv_s[...] = jnp.zeros_like(dv_s)

        @pl.when(b == 0)
        def _():
            dbias_ref[...] = jnp.zeros_like(dbias_ref)
            dsink_ref[...] = jnp.zeros_like(dsink_ref)

        def blk(n, carry):
            rows, prow, heads, qbs, soft = _swa_block_heads(n, q_ref, k_ref, v_ref, bias_ref, sink_ref, kp, above,
                                                            mcur)
            dobs = [do_ref[rows, qsl] for _, _, qsl, _ in heads]
            dps = [(_dot(dob, kv[3], 1, 1), _dot(dob, kv[2], 1, 1)) for dob, (_, _, _, kv) in zip(dobs, heads)]
            grads = []
            for (hl, _, _, _), (ep, ec, es, den), (dpp, dpc) in zip(heads, soft, dps):
                inv = 1.0 / den
                pp, pc = ep * inv, ec * inv
                total = jnp.sum(pp * dpp, axis=1, keepdims=True) + jnp.sum(pc * dpc, axis=1, keepdims=True)
                dlp = pp * (dpp - total)
                dlc = pc * (dpc - total)
                dsink_ref[hl:hl + 1, :] += jnp.zeros((1, LANES), F32) - jnp.sum(es * inv * total)
                dbias_ref[hl, :, 0:W] += dlp
                dbias_ref[hl, :, W:2 * W] += dlc
                grads.append((pp, pc, dlp, dlc))
            for (_, _, qsl, kv), (_, _, dlp, dlc) in zip(heads, grads):
                dq_s[rows, qsl] = (_dot(dlp, kv[1], 1, 0) + _dot(dlc, kv[0], 1, 0)) * scale
            for kvh in range(2):
                group = range(4 * kvh, 4 * kvh + 4)
                ksl = heads[4 * kvh][1]
                dk_s[prow, ksl] += sum(_dot(grads[i][2], qbs[i], 0, 0) for i in group) * scale
                dk_s[rows, ksl] += sum(_dot(grads[i][3], qbs[i], 0, 0) for i in group) * scale
                dv_s[prow, ksl] += sum(_dot(grads[i][0], dobs[i], 0, 0) for i in group)
                dv_s[rows, ksl] += sum(_dot(grads[i][1], dobs[i], 0, 0) for i in group)
            return carry

        lax.fori_loop(0, nb, blk, 0)
        r0 = pl.multiple_of(b * S, S)
        cq = pl.multiple_of(kp * 4 * LANES, LANES)
        ck = pl.multiple_of((SW_K_BLOCK0 + kp) * LANES, LANES)
        cv = pl.multiple_of((SW_V_BLOCK0 + kp) * LANES, LANES)
        copies = [pltpu.make_async_copy(dq_s, dp_hbm.at[pl.ds(r0, S), pl.ds(cq, 4 * LANES)], sems.at[0]),
                  pltpu.make_async_copy(dk_s, dp_hbm.at[pl.ds(r0, S), pl.ds(ck, LANES)], sems.at[1]),
                  pltpu.make_async_copy(dv_s, dp_hbm.at[pl.ds(r0, S), pl.ds(cv, LANES)], sems.at[2])]
        for cp in copies:
            cp.start()
        for cp in copies:
            cp.wait()
        _wait_carried(exchange, carried, grid)

    qspec = pl.BlockSpec((S, 4 * LANES), lambda kp, b: (b, kp))
    outs, targets = _carrier_call(
        body, name, grid,
        [qspec, pl.BlockSpec((S, LANES), lambda kp, b: (b, SW_K_BLOCK0 + kp)),
         pl.BlockSpec((S, LANES), lambda kp, b: (b, SW_V_BLOCK0 + kp)),
         pl.BlockSpec((SW_STEP_HEADS, W, 2 * W), lambda kp, b: (kp, 0, 0)),
         pl.BlockSpec(memory_space=pltpu.SMEM), qspec],
        [ANY, pl.BlockSpec((SW_STEP_HEADS, W, 2 * W), lambda kp, b: (kp, 0, 0)),
         pl.BlockSpec((SW_STEP_HEADS, LANES), lambda kp, b: (kp, 0))],
        [jax.ShapeDtypeStruct((T, width), F32), jax.ShapeDtypeStruct((SW_HEADS, W, 2 * W), F32),
         jax.ShapeDtypeStruct((SW_HEADS, LANES), F32)],
        [pltpu.VMEM((S, 4 * LANES), F32), pltpu.VMEM((S, LANES), F32), pltpu.VMEM((S, LANES), F32),
         pltpu.SemaphoreType.DMA((3,))], {}, (qkn, qkn, proj, bias, sinks, do), exchange)
    return (*outs, targets)


CHIP_FLIPS = ((1, 0), (0, 1), (1, 1))


def _flip(v, f):
    return 1 - v if f else v


class _Exchange:
    def __init__(self, sources, targets, copies):
        self.sources, self.targets = list(sources), list(targets)
        self._copies = copies
        n = len(self.sources)
        self.scratch = [pltpu.SemaphoreType.DMA((n, 3)), pltpu.SemaphoreType.DMA((n, 3)),
                        pltpu.SemaphoreType.DMA((n,))]

    def _descriptors(self, srcs, dsts, sems, chip, peers):
        send, recv, loc = sems
        out = []
        for t, (local, remote) in enumerate(self._copies(srcs, dsts, chip)):
            out.append(pltpu.make_async_copy(local[0], local[1], loc.at[t]))
            for r, (src, dst) in enumerate(remote):
                out.append(pltpu.make_async_remote_copy(src, dst, send.at[t, r], recv.at[t, r],
                                                        device_id=peers[r], device_id_type=MESH))
        return out

    def start(self, srcs, dsts, sems):
        x, y, c = lax.axis_index("x"), lax.axis_index("y"), lax.axis_index("c")
        peers = [(_flip(x, fx), _flip(y, fy), c) for fx, fy in CHIP_FLIPS]
        for chip in range(4):
            @pl.when(2 * x + y == chip)
            def _():
                for cp in self._descriptors(srcs, dsts, sems, chip, peers):
                    cp.start()

    def wait(self, srcs, dsts, sems):
        me = (lax.axis_index("x"), lax.axis_index("y"), lax.axis_index("c"))
        for cp in self._descriptors(srcs, dsts, sems, 0, [me] * 3):
            cp.wait()

    def operands(self):
        return self.sources + self.targets

    def specs(self):
        ns, nt = len(self.sources), len(self.targets)
        return [ANY] * (ns + nt), [ANY] * nt, [jax.ShapeDtypeStruct(t.shape, t.dtype) for t in self.targets]

    def aliases(self, n_in, n_out):
        ns = len(self.sources)
        return {n_in + ns + i: n_out + i for i in range(len(self.targets))}

    def split(self, refs, n_in, n_out, n_scr):
        ns, nt = len(self.sources), len(self.targets)
        o0 = n_in + ns + nt
        s0 = o0 + n_out + nt
        own = list(refs[:n_in]) + list(refs[o0:o0 + n_out]) + list(refs[s0:s0 + n_scr])
        return own, (refs[n_in:n_in + ns], refs[o0 + n_out:o0 + n_out + nt], refs[s0 + n_scr:])


def _carried(exchange, refs, n_in, n_out, n_scr):
    if exchange is None:
        return list(refs), None
    return exchange.split(refs, n_in, n_out, n_scr)


def _grid_edge(grid, last):
    conds = [pl.program_id(d) == (n - 1 if last else 0) for d, n in enumerate(grid)]
    out = conds[0]
    for cnd in conds[1:]:
        out = jnp.logical_and(out, cnd)
    return out


def _start_carried(exchange, parts, grid):
    if parts is not None:
        @pl.when(_grid_edge(grid, False))
        def _():
            exchange.start(*parts)


def _wait_carried(exchange, parts, grid):
    if parts is not None:
        @pl.when(_grid_edge(grid, True))
        def _():
            exchange.wait(*parts)


def _carrier_call(body, name, grid, in_specs, out_specs, out_shape, scratch, aliases, operands, exchange):
    n_out = len(out_shape)
    aliases = dict(aliases)
    if exchange is not None:
        ex_in, ex_out, ex_shape = exchange.specs()
        aliases.update(exchange.aliases(len(in_specs), n_out))
        in_specs, out_specs, out_shape = in_specs + ex_in, out_specs + ex_out, out_shape + ex_shape
        scratch = scratch + exchange.scratch
        operands = list(operands) + exchange.operands()
    outs = pl.pallas_call(body, name=name, grid=grid, in_specs=in_specs, out_specs=out_specs, out_shape=out_shape,
                          scratch_shapes=scratch, input_output_aliases=aliases,
                          compiler_params=_cparams(*["arbitrary"] * len(grid)))(*operands)
    return outs[:n_out], outs[n_out:]


def _exchange_call(exchange, name):
    in_specs, out_specs, out_shape = exchange.specs()

    def body(*refs):
        _, parts = exchange.split(refs, 0, 0, 0)
        exchange.start(*parts)
        exchange.wait(*parts)

    return pl.pallas_call(body, name=name, in_specs=in_specs, out_specs=out_specs, out_shape=out_shape,
                          scratch_shapes=exchange.scratch, input_output_aliases=exchange.aliases(0, 0))(
        *exchange.operands())


def _gather_exchange(shards, fulls, axes, layers):
    sizes = [s.shape[a] for s, a in zip(shards, axes)]

    def copies(srcs, dsts, chip):
        out = []
        for src, full, axis, size, (l0, l1) in zip(srcs, dsts, axes, sizes, layers):
            part = src.at[l0:l1]
            cut = pl.ds(chip * size, size)
            dst = full.at[l0:l1, cut, :] if axis == 1 else full.at[l0:l1, :, cut]
            out.append(((part, dst), [(part, dst)] * 3))
        return out

    return _Exchange(shards, fulls, copies)


def _scatter_exchange(grads, stacks, axes, layers):
    sizes = [g.shape[a - 1] // 4 for g, a in zip(grads, axes)]

    def copies(srcs, dsts, chip):
        out = []
        for g, stack, axis, size, layer in zip(srcs, dsts, axes, sizes, layers):
            def cut(j, g=g, axis=axis, size=size):
                return g.at[pl.ds(j * size, size), :] if axis == 1 else g.at[:, pl.ds(j * size, size)]

            remote = [(cut(chip ^ (2 * fx + fy)), stack.at[r, layer]) for r, (fx, fy) in enumerate(CHIP_FLIPS)]
            out.append(((cut(chip), stack.at[3, layer]), remote))
        return out

    return _Exchange(grads, stacks, copies)


def _swap_with_sibling(parts, name):
    n = len(parts)

    def body(*refs):
        ins, outs = refs[:n], refs[n:2 * n]
        send, recv = refs[2 * n:]
        peer = (lax.axis_index("x"), lax.axis_index("y"), 1 - lax.axis_index("c"))
        copies = [pltpu.make_async_remote_copy(ins[t], outs[t], send.at[t], recv.at[t], device_id=peer,
                                               device_id_type=MESH) for t in range(n)]
        for cp in copies:
            cp.start()
        for cp in copies:
            cp.wait()

    return pl.pallas_call(
        body, name=name, in_specs=[ANY] * n, out_specs=[ANY] * n,
        out_shape=[jax.ShapeDtypeStruct(p.shape, p.dtype) for p in parts],
        scratch_shapes=[pltpu.SemaphoreType.DMA((n,)), pltpu.SemaphoreType.DMA((n,))])(*parts)


def _allreduce_small(v, name):
    R = v.shape[0]
    ND = 8

    def body(v_ref, o_ref, buf, send, recv):
        x, y, c = lax.axis_index("x"), lax.axis_index("y"), lax.axis_index("c")
        me = 4 * x + 2 * y + c
        copies = []
        for d in range(1, ND):
            peer = (_flip(x, d >> 2 & 1), _flip(y, d >> 1 & 1), _flip(c, d & 1))
            copies.append(pltpu.make_async_remote_copy(v_ref, buf.at[me], send.at[d], recv.at[me], device_id=peer,
                                                       device_id_type=MESH))
        for cp in copies:
            cp.start()
        buf[pl.ds(me, 1)] = v_ref[...][None]
        for k in range(ND):
            @pl.when(me != k)
            def _():
                pltpu.make_async_remote_copy(v_ref, buf.at[k], send.at[0], recv.at[k], device_id=(x, y, c),
                                             device_id_type=MESH).wait_recv()
        for cp in copies:
            cp.wait_send()
        total = buf[0]
        for k in range(1, ND):
            total = total + buf[k]
        o_ref[...] = total

    vm = pl.BlockSpec(memory_space=pltpu.VMEM)
    return pl.pallas_call(
        body, name=name, in_specs=[vm], out_specs=vm, out_shape=jax.ShapeDtypeStruct((R, LANES), F32),
        scratch_shapes=[pltpu.VMEM((ND, R, LANES), F32), pltpu.SemaphoreType.DMA((ND,)),
                        pltpu.SemaphoreType.DMA((ND,))],
        compiler_params=pltpu.CompilerParams(vmem_limit_bytes=VMEM_LIMIT))(v)


def _tile2(R, Cn):
    tc = _pick(Cn, 2048)
    tr = R
    for cand in (256, 128, 64, 32, 16, 8):
        if R % cand == 0:
            tr = cand
            break
    return tr, tc


def _sum4(stack, name):
    _, R, Cn = stack.shape
    tr, tc = _tile2(R, Cn)

    def body(s_ref, o_ref):
        o_ref[...] = ((s_ref[0].astype(F32) + s_ref[1].astype(F32)) + s_ref[2].astype(F32)) + s_ref[3].astype(F32)

    return pl.pallas_call(
        body, name=name, grid=(R // tr, Cn // tc), in_specs=[pl.BlockSpec((4, tr, tc), lambda i, j: (0, i, j))],
        out_specs=pl.BlockSpec((tr, tc), lambda i, j: (i, j)), out_shape=jax.ShapeDtypeStruct((R, Cn), F32),
        compiler_params=_cparams("parallel", "parallel"))(stack)


def _adamw(w, m, v, g_parts, name):
    R, Cn = w.shape
    tr, tc = _tile2(R, Cn)
    npart = len(g_parts)
    c1 = 1.0 / (1.0 - ADAM_B1 ** ADAM_STEP)
    c2 = 1.0 / (1.0 - ADAM_B2 ** ADAM_STEP)

    def body(*refs):
        w_ref, m_ref, v_ref = refs[:3]
        g_refs = refs[3:3 + npart]
        g_out, d_out, m_out, v_out = refs[3 + npart:]
        g = g_refs[0][...]
        for r in g_refs[1:]:
            g = g + r[...]
        mn = ADAM_B1 * m_ref[...] + (1.0 - ADAM_B1) * g
        vn = ADAM_B2 * v_ref[...] + (1.0 - ADAM_B2) * (g * g)
        g_out[...] = g
        m_out[...] = mn
        v_out[...] = vn
        d_out[...] = -ADAM_LR * ((mn * c1) / (jnp.sqrt(vn * c2) + ADAM_EPS) + ADAM_WD * w_ref[...])

    spec = pl.BlockSpec((tr, tc), lambda i, j: (i, j))
    return pl.pallas_call(
        body, name=name, grid=(R // tr, Cn // tc), in_specs=[spec] * (3 + npart), out_specs=[spec] * 4,
        out_shape=[jax.ShapeDtypeStruct((R, Cn), F32)] * 4,
        compiler_params=_cparams("parallel", "parallel"))(w, m, v, *g_parts)


SHARDED = (("ab_w_in", 2), ("ab_w_out", 1), ("c_w_in", 2), ("c_w_out", 1), ("ffn_up", 2), ("ffn_conv", 2),
           ("ffn_down", 1), ("ple_gate", 1), ("ple_proj", 2))
SMALL = ("mix_norm", "hg_lb_logits", "hg_out_norm", "q_norm", "k_norm", "sinks", "rel_bias", "ffn_norm",
         "ffn_conv_b", "ple_norm")
WEIGHTS = ("mix_norm", "ab_w_in", "hg_lb_logits", "hg_out_norm", "ab_w_out", "c_w_in", "q_norm", "k_norm", "sinks",
           "rel_bias", "c_w_out", "ffn_norm", "ffn_up", "ffn_conv", "ffn_conv_b", "ffn_down", "ple_norm", "ple_gate",
           "ple_proj")
PACK_ALIGN = 8 * LANES


def _pack(arrs):
    pieces = []
    for a in arrs:
        flat = a.reshape(-1)
        pad = -flat.shape[0] % PACK_ALIGN
        pieces.append(jnp.pad(flat, (0, pad)).reshape(-1, LANES))
    return jnp.concatenate(pieces, axis=0)


def _unpack(packed, like):
    out, r = [], 0
    for a in like:
        size = int(np.prod(a.shape))
        rows = (size + PACK_ALIGN - 1) // PACK_ALIGN * 8
        out.append(packed[r:r + rows].reshape(-1)[:size].reshape(a.shape))
        r += rows
    return out


def _family_range(name, lo, hi):
    if name.startswith("ab_"):
        idx = [i // 2 for i in range(lo, hi) if i % 2 == 0]
    elif name.startswith("c_"):
        idx = [i // 2 for i in range(lo, hi) if i % 2 == 1]
    else:
        idx = list(range(lo, hi))
    return (idx[0], idx[-1] + 1) if idx else None


W_IN = ("ab_w_in", "c_w_in")
REST = tuple(k for k, _ in SHARDED if k not in W_IN)
MIXER = ("ab_w_in", "ab_w_out", "c_w_in", "c_w_out")
FFN = ("ffn_up", "ffn_conv", "ffn_down")
PLE = ("ple_gate", "ple_proj")
GATHER_PLAN = {
    "gather_first": [(0, W_IN)],
    "sb_fwd_0": [(0, REST), (1, ("c_w_in",))],
    "hgrn_fwd_0": [(1, ("ffn_up", "ffn_conv"))],
    "ffn_up_0": [(1, ("ffn_down", "c_w_out"))],
    "ffn_down_0": [(1, PLE)],
    "swa_fwd_1": [(2, ("ab_w_in",))],
    "ffn_up_1": [(2, ("ab_w_out", "ffn_down"))],
    "ffn_down_1": [(2, PLE)],
    "sb_fwd_2": [(2, ("ffn_up", "ffn_conv")), (3, ("c_w_in", "c_w_out", "ffn_up", "ffn_conv"))],
    "hgrn_fwd_2": [(3, ("ffn_down",))],
    "ffn_up_2": [(3, PLE)],
}
SCATTER_PLAN = {
    "d_ffn_up_3": [(3, ("ffn_down", "ffn_conv"))],
    "d_hn2_3": [(3, PLE)],
    "swa_bwd_3": [(3, ("ffn_up", "c_w_out"))],
    "sb_bwd_2": [(3, W_IN), (2, REST)],
    "convglu_bwd_1": [(2, W_IN)],
    "d_ffn_up_1": [(1, ("ffn_down", "ffn_conv"))],
    "d_hn2_1": [(1, PLE)],
    "swa_bwd_1": [(1, ("ffn_up", "c_w_out"))],
    "sb_bwd_0": [(1, W_IN), (0, REST)],
    "scatter_last": [(0, W_IN)],
}


class _StepExchanges:
    def __init__(self, shards, W):
        self.shards, self.W = shards, W
        self.stacks = {}
        for k, axis in SHARDED:
            shp = shards[k].shape
            self.W[k] = lax.empty(tuple(4 * d if i == axis else d for i, d in enumerate(shp)), shards[k].dtype)
            self.stacks[k] = lax.empty((4,) + shp, shards[k].dtype)

    @staticmethod
    def _select(plan):
        idx = {}
        for layer, fams in plan:
            for k, _ in SHARDED:
                r = _family_range(k, layer, layer + 1)
                if r is not None and (fams is None or k in fams):
                    idx.setdefault(k, []).append(r[0])
        return [(k, axis, sorted(idx[k])) for k, axis in SHARDED if k in idx]

    def gather(self, call):
        sel = self._select(GATHER_PLAN.get(call, ()))
        if not sel:
            return None, None
        for _, _, ii in sel:
            assert ii == list(range(ii[0], ii[-1] + 1)), "one copy per family takes a contiguous layer range"
        names = [k for k, _, _ in sel]
        ex = _gather_exchange([self.shards[k] for k in names], [self.W[k] for k in names],
                              [a for _, a, _ in sel], [(ii[0], ii[-1] + 1) for _, _, ii in sel])
        return ex, (self.W, names)

    def scatter(self, call, G):
        sel = self._select(SCATTER_PLAN.get(call, ()))
        if not sel:
            return None, None
        assert all(len(ii) == 1 for _, _, ii in sel)
        names = [k for k, _, _ in sel]
        ex = _scatter_exchange([G[k][ii[0]] for k, _, ii in sel], [self.stacks[k] for k in names],
                               [a for _, a, _ in sel], [ii[0] for _, _, ii in sel])
        return ex, (self.stacks, names)

    @staticmethod
    def adopt(where, targets):
        if where is not None:
            book, names = where
            for k, t in zip(names, targets):
                book[k] = t


def _forward_backward(x, p, target, W, S, exchanges=None):
    T = x.shape[0]
    depth = p.shape[0]
    lb = _lower_bound_fwd(W["hg_lb_logits"], "lower_bound_fwd")
    bias = _bias_build(W["rel_bias"], "bias_build")
    qk_gain = jnp.concatenate([jnp.tile(W["q_norm"], (1, SW_HEADS)), jnp.tile(W["k_norm"], (1, SW_KV))], axis=1)

    def gathering(call):
        return exchanges.gather(call) if exchanges else (None, None)

    def scattering(call):
        return exchanges.scatter(call, G) if exchanges else (None, None)

    def matmul(a, b, mode, name, **kw):
        ex, where = (gathering(name) if name in GATHER_PLAN else
                     scattering(name) if name in SCATTER_PLAN else (None, None))
        out, moved = _matmul(a, b, mode, name, exchange=ex, **kw)
        _StepExchanges.adopt(where, moved)
        return out

    def mm(a, wname, layer, mode, name, **kw):
        return matmul(a, W[wname], mode, name, b_layer=layer, **kw)

    saved = []
    h = x
    for i in range(depth):
        j = i // 2
        s = {"h0": h}
        s["hn"] = _rmsnorm_fwd(h, W["mix_norm"][i:i + 1], f"mix_norm_fwd_{i}")
        if i % 2 == 0:
            s["proj"] = mm(s["hn"], "ab_w_in", j, "nn", f"ab_in_{i}")
            ex, where = gathering(f"sb_fwd_{i}")
            cat, s["sb_tot"], arrived = _sb_fwd(s["proj"], S, f"sb_fwd_{i}", ex)
            _StepExchanges.adopt(where, arrived)
            ex, where = gathering(f"hgrn_fwd_{i}")
            s["cat"], s["oraw"], s["states"], arrived = _hgrn_fwd(s["proj"], cat, lb[j:j + 1],
                                                                  W["hg_out_norm"][j:j + 1], S, f"hgrn_fwd_{i}", ex)
            _StepExchanges.adopt(where, arrived)
            h = mm(s["cat"], "ab_w_out", j, "nn", f"ab_out_{i}", res=h)
        else:
            s["proj"] = mm(s["hn"], "c_w_in", j, "nn", f"c_in_{i}")
            s["qkn"] = _headnorm_fwd(s["proj"], qk_gain[j:j + 1], f"qk_norm_fwd_{i}")
            ex, where = gathering(f"swa_fwd_{i}")
            s["o"], arrived = _swa_fwd(s["qkn"], s["proj"], bias, W["sinks"][j], S, f"swa_fwd_{i}", ex)
            _StepExchanges.adopt(where, arrived)
            h = mm(s["o"], "c_w_out", j, "nn", f"c_out_{i}", res=h)
        s["h1"] = h
        s["hn2"] = _rmsnorm_fwd(h, W["ffn_norm"][i:i + 1], f"ffn_norm_fwd_{i}")
        s["u"] = mm(s["hn2"], "ffn_up", i, "nn", f"ffn_up_{i}", tiles=(min(T, 1024), D_FF // 2, D_MODEL))
        s["a"] = _convglu_fwd(s["u"], W["ffn_conv"][i], W["ffn_conv_b"][i:i + 1], S, f"convglu_fwd_{i}")
        h = mm(s["a"], "ffn_down", i, "nn", f"ffn_down_{i}", res=h)
        s["h2"] = h
        s["hn3"] = _rmsnorm_fwd(h, W["ple_norm"][i:i + 1], f"ple_norm_fwd_{i}")
        s["z"] = mm(s["hn3"], "ple_gate", i, "nn", f"ple_gate_{i}")
        s["pp"] = mm(p, "ple_proj", i, "nn", f"ple_proj_{i}", a_layer=i)
        h = _ple_fwd(h, s["z"], s["pp"], f"ple_fwd_{i}")
        saved.append(s)

    loss, dh = _loss_fwd_bwd(h, target, "loss")

    G = {k: [None] * depth for k in ("mix_norm", "ffn_norm", "ple_norm", "ffn_up", "ffn_conv", "ffn_conv_b",
                                     "ffn_down", "ple_gate", "ple_proj")}
    for k in ("ab_w_in", "ab_w_out", "c_w_in", "c_w_out", "hg_out_norm", "q_norm", "k_norm", "sinks", "lb"):
        G[k] = [None] * (depth // 2)
    dbias_total = None
    for i in reversed(range(depth)):
        j = i // 2
        s = saved[i]
        dz, dpp = _ple_bwd(dh, s["z"], s["pp"], f"ple_bwd_{i}")
        G["ple_proj"][i] = matmul(p, dpp, "tn", f"d_ple_proj_{i}", out_dtype=BF16, a_layer=i)
        G["ple_gate"][i] = matmul(s["hn3"], dz, "tn", f"d_ple_gate_{i}", out_dtype=BF16)
        dhn = mm(dz, "ple_gate", i, "nt", f"d_hn3_{i}")
        dh, G["ple_norm"][i] = _rmsnorm_bwd(s["h2"], W["ple_norm"][i:i + 1], dhn, dh, f"ple_norm_bwd_{i}")

        half_ff = D_FF // 2
        da = mm(dh, "ffn_down", i, "nt", f"d_a_{i}", tiles=(min(T, 1024), half_ff, D_MODEL))
        G["ffn_down"][i] = matmul(s["a"], dh, "tn", f"d_ffn_down_{i}", out_dtype=BF16,
                                   tiles=(half_ff, 512, min(T, 2048)))
        ex, where = scattering(f"convglu_bwd_{i}")
        du, dcw, dcb, sent = _convglu_bwd(s["u"], da, W["ffn_conv"][i], W["ffn_conv_b"][i:i + 1], S,
                                          f"convglu_bwd_{i}", ex)
        _StepExchanges.adopt(where, sent)
        G["ffn_conv"][i] = jnp.swapaxes(dcw, 0, 1).reshape(3, 2 * D_FF)
        G["ffn_conv_b"][i] = dcb.reshape(1, 2 * D_FF)
        G["ffn_up"][i] = matmul(s["hn2"], du, "tn", f"d_ffn_up_{i}", out_dtype=BF16,
                                 tiles=(D_MODEL, half_ff, min(T, 1024)))
        dhn = mm(du, "ffn_up", i, "nt", f"d_hn2_{i}")
        dh, G["ffn_norm"][i] = _rmsnorm_bwd(s["h1"], W["ffn_norm"][i:i + 1], dhn, dh, f"ffn_norm_bwd_{i}")

        if i % 2 == 0:
            dcat = mm(dh, "ab_w_out", j, "nt", f"d_cat_{i}")
            G["ab_w_out"][j] = matmul(s["cat"], dh, "tn", f"d_ab_out_{i}", out_dtype=BF16)
            ex, where = scattering(f"sb_bwd_{i}")
            dproj, sent = _sb_bwd(s["proj"], s["sb_tot"], dcat, S, f"sb_bwd_{i}", ex)
            _StepExchanges.adopt(where, sent)
            dproj, G["lb"][j], G["hg_out_norm"][j] = _hgrn_bwd(s["proj"], s["oraw"], dcat, s["states"], lb[j:j + 1],
                                                               W["hg_out_norm"][j:j + 1], dproj, S, f"hgrn_bwd_{i}")
            G["ab_w_in"][j] = matmul(s["hn"], dproj, "tn", f"d_ab_in_{i}", out_dtype=BF16)
            dhn = mm(dproj, "ab_w_in", j, "nt", f"d_hn_{i}")
        else:
            do = mm(dh, "c_w_out", j, "nt", f"d_o_{i}")
            G["c_w_out"][j] = matmul(s["o"], dh, "tn", f"d_c_out_{i}", out_dtype=BF16)
            ex, where = scattering(f"swa_bwd_{i}")
            dqkv, dbias, dsink, sent = _swa_bwd(s["qkn"], s["proj"], bias, W["sinks"][j], do, S, f"swa_bwd_{i}", ex)
            _StepExchanges.adopt(where, sent)
            dbias_total = dbias if dbias_total is None else dbias_total + dbias
            G["sinks"][j] = dsink[:, 0]
            dproj, dgain = _headnorm_bwd(s["proj"], qk_gain[j:j + 1], dqkv, f"qk_norm_bwd_{i}")
            G["q_norm"][j] = dgain[0, :SW_Q_COLS].reshape(SW_HEADS, SW_DIM).sum(axis=0)
            G["k_norm"][j] = dgain[0, SW_Q_COLS:].reshape(SW_KV, SW_DIM).sum(axis=0)
            G["c_w_in"][j] = matmul(s["hn"], dproj, "tn", f"d_c_in_{i}", out_dtype=BF16)
            dhn = mm(dproj, "c_w_in", j, "nt", f"d_hn_{i}")
        dh, G["mix_norm"][i] = _rmsnorm_bwd(s["h0"], W["mix_norm"][i:i + 1], dhn, dh, f"mix_norm_bwd_{i}")

    grads = {k: G[k] for k, _ in SHARDED}
    for k in ("q_norm", "k_norm", "sinks"):
        grads[k] = jnp.stack(G[k])
    for k in ("mix_norm", "ffn_norm", "ple_norm", "ffn_conv_b", "hg_out_norm"):
        grads[k] = jnp.concatenate(G[k], axis=0)
    grads["hg_lb_logits"] = _lower_bound_bwd(W["hg_lb_logits"], jnp.concatenate(G["lb"], axis=0), "lower_bound_bwd")
    grads["rel_bias"] = _bias_reduce(dbias_total, "bias_reduce")[:, 0, :N_BUCKETS].T
    return loss, dh, grads


def kernel(x, p, mix_norm, ab_w_in, hg_lb_logits, hg_out_norm, ab_w_out, c_w_in, q_norm, k_norm, sinks, rel_bias, c_w_out, ffn_norm, ffn_up, ffn_conv, ffn_conv_b, ffn_down, ple_norm, ple_gate, ple_proj, loss_target, m_mix_norm, m_ab_w_in, m_hg_lb_logits, m_hg_out_norm, m_ab_w_out, m_c_w_in, m_q_norm, m_k_norm, m_sinks, m_rel_bias, m_c_w_out, m_ffn_norm, m_ffn_up, m_ffn_conv, m_ffn_conv_b, m_ffn_down, m_ple_norm, m_ple_gate, m_ple_proj, v_mix_norm, v_ab_w_in, v_hg_lb_logits, v_hg_out_norm, v_ab_w_out, v_c_w_in, v_q_norm, v_k_norm, v_sinks, v_rel_bias, v_c_w_out, v_ffn_norm, v_ffn_up, v_ffn_conv, v_ffn_conv_b, v_ffn_down, v_ple_norm, v_ple_gate, v_ple_proj):
    args = dict(locals())
    w = {k: args[k] for k in WEIGHTS}
    m = {k: args["m_" + k] for k in WEIGHTS}
    v = {k: args["v_" + k] for k in WEIGHTS}
    B, S, Dm = x.shape
    T = B * S
    names = [k for k, _ in SHARDED]

    W = {k: w[k] for k in SMALL}
    exchanges = _StepExchanges({k: w[k].astype(F32 if k == "ffn_conv" else BF16) for k in names}, W)
    first, where = exchanges.gather("gather_first")
    exchanges.adopt(where, _exchange_call(first, "gather_first"))

    loss, dx, grads = _forward_backward(x.reshape(T, Dm), p.reshape(p.shape[0], T, p.shape[-1]),
                                        loss_target.reshape(T, Dm), W, S, exchanges)
    loss = lax.psum(loss[0, 0], ("x", "y", "c"))

    last, where = exchanges.scatter("scatter_last", grads)
    exchanges.adopt(where, _exchange_call(last, "scatter_last"))
    stacks = [exchanges.stacks[k] for k in names]
    partial = [_sum4(st.reshape(4, -1, st.shape[-1]), f"sum_chips_{k}") for k, st in zip(names, stacks)]
    other = _swap_with_sibling(partial, "swap_core_sums")
    small_sum = _allreduce_small(_pack([grads[k] for k in SMALL]), "allreduce_small")

    out_g, out_d, out_m, out_v = {}, {}, {}, {}
    for k, mine, theirs in zip(names, partial, other):
        shp = w[k].shape
        r = [a.reshape(shp) for a in _adamw(w[k].reshape(mine.shape), m[k].reshape(mine.shape),
                                            v[k].reshape(mine.shape), [mine, theirs], f"adamw_{k}")]
        out_g[k], out_d[k], out_m[k], out_v[k] = r
    sm = _adamw(_pack([w[k] for k in SMALL]), _pack([m[k] for k in SMALL]), _pack([v[k] for k in SMALL]),
                [small_sum], "adamw_small")
    like = [w[k] for k in SMALL]
    for dst, packed in zip((out_g, out_d, out_m, out_v), sm):
        for k, a in zip(SMALL, _unpack(packed, like)):
            dst[k] = a

    return (loss, dx.reshape(B, S, Dm), *[out_g[k] for k in WEIGHTS], *[out_d[k] for k in WEIGHTS],
            *[out_m[k] for k in WEIGHTS], *[out_v[k] for k in WEIGHTS])
```

```python
import math

import numpy as np
import jax
import jax.numpy as jnp
from jax import lax
from jax.experimental import pallas as pl
from jax.experimental.pallas import tpu as pltpu

F32 = jnp.float32
BF16 = jnp.bfloat16
MESH = pl.DeviceIdType.MESH
ANY = pl.BlockSpec(memory_space=pl.ANY)

D_MODEL = 1024
EPS = 1e-6
SB_HEADS, SB_DIM = 8, 64
HG_HEADS, HG_DK = 4, 128
HG_CHUNK = 32
HG_STEP = 2
HG_TRIP = 4
SW_HEADS, SW_KV, SW_DIM, WINDOW = 16, 4, 64, 128
N_BUCKETS, MAX_DISTANCE = 32, 128
D_FF = 2816
ATT_BLOCK = 128
SB_QBLOCK = 256
LANES = 128
NEG = -1e30

ADAM_LR, ADAM_B1, ADAM_B2, ADAM_EPS, ADAM_WD, ADAM_STEP = 0.001, 0.9, 0.999, 1e-08, 0.01, 10

VMEM_LIMIT = 56 * 1024 * 1024


def _cparams(*sem):
    return pltpu.CompilerParams(dimension_semantics=sem, vmem_limit_bytes=VMEM_LIMIT)


def _pick(n, cap):
    if n <= cap:
        return n
    best = None
    for d in range(LANES, cap + 1, LANES):
        if n % d == 0:
            best = d
    assert best is not None, (n, cap)
    return best


def _dot(a, b, ca, cb):
    return lax.dot_general(a.astype(BF16), b.astype(BF16), (((ca,), (cb,)), ((), ())),
                           preferred_element_type=F32)


def _split(x, terms):
    parts = []
    for _ in range(terms):
        hi = x.astype(BF16)
        parts.append(hi)
        x = x - hi.astype(F32)
    return parts


def _dot_exact_l(x, m, terms=2):
    out = None
    for p in _split(x, terms):
        t = lax.dot_general(p, m, (((1,), (0,)), ((), ())), preferred_element_type=F32)
        out = t if out is None else out + t
    return out


def _dot_exact_r(m, x, terms=3, cm=1):
    out = None
    for p in _split(x, terms):
        t = lax.dot_general(m, p, (((cm,), (0,)), ((), ())), preferred_element_type=F32)
        out = t if out is None else out + t
    return out


def _sig(x):
    return 1.0 / (1.0 + jnp.exp(-x))


def _iota2(shape, dim):
    return lax.broadcasted_iota(jnp.int32, shape, dim)


def _operand_spec(arr, layer, blk, index):
    if arr.ndim == 2:
        return pl.BlockSpec(blk, index)
    if layer is not None:
        return pl.BlockSpec((None,) + blk, lambda i, j, k: (layer,) + index(i, j, k))
    per_half = arr.shape[2] // blk[1]

    def halves(i, j, k):
        r, c = index(i, j, k)
        return (c // per_half, r, c % per_half)

    return pl.BlockSpec((None,) + blk, halves)


def _matmul(a, b, mode, name, out_dtype=F32, res=None, a_layer=None, b_layer=None, tiles=None, exchange=None,
            norm_gain=None):
    def dims(arr, layer):
        if arr.ndim == 2:
            return arr.shape
        return arr.shape[1:] if layer is not None else (arr.shape[1], 2 * arr.shape[2])

    (a0, a1), (b0, b1) = dims(a, a_layer), dims(b, b_layer)
    if mode == "nn":
        M, K, N = a0, a1, b1
    elif mode == "nt":
        M, K, N = a0, a1, b0
    else:
        K, M, N = a0, a1, b1
    cap_m, cap_n, cap_k = 1024, 1024, (1024 if mode == "tn" else 2048)
    tm, tn, tk = _pick(M, cap_m), _pick(N, cap_n), _pick(K, cap_k)
    if a.ndim == 3 and a_layer is None:
        if mode == "tn":
            tm = _pick(a.shape[2], cap_m)
        else:
            tk = _pick(a.shape[2], cap_k)
    if b.ndim == 3 and b_layer is None:
        if mode == "nt":
            tk = _pick(b.shape[2], cap_k)
        else:
            tn = _pick(b.shape[2], cap_n)
    if tiles is not None:
        tm, tn, tk = tiles
    assert M % tm == 0 and N % tn == 0 and K % tk == 0, (name, M, N, K, tm, tn, tk)
    nk = K // tk
    if mode == "tn":
        a_spec = _operand_spec(a, a_layer, (tk, tm), lambda i, j, k: (k, i))
    else:
        a_spec = _operand_spec(a, a_layer, (tm, tk), lambda i, j, k: (i, k))
    if mode == "nt":
        b_spec = _operand_spec(b, b_layer, (tn, tk), lambda i, j, k: (j, k))
    else:
        b_spec = _operand_spec(b, b_layer, (tk, tn), lambda i, j, k: (k, j))
    ca, cb = {"nn": (1, 0), "nt": (1, 1), "tn": (0, 0)}[mode]
    o_spec = pl.BlockSpec((tm, tn), lambda i, j, k: (i, j))

    grid = (M // tm, N // tn, nk)
    normed = norm_gain is not None
    assert not normed or tn == N, "the norm needs whole rows in one tile"
    n_in = 2 + (res is not None) + normed
    n_out = 1 + normed

    def body(*refs):
        own, carried = _carried(exchange, refs, n_in, n_out, 1)
        _start_carried(exchange, carried, grid)
        a_ref, b_ref = own[:2]
        r_ref = own[2] if res is not None else None
        g_ref = own[n_in - 1] if normed else None
        o_ref, acc = own[n_in], own[-1]
        k = pl.program_id(2)

        @pl.when(k == 0)
        def _():
            acc[...] = jnp.zeros_like(acc)

        acc[...] += _dot(a_ref[...], b_ref[...], ca, cb)

        @pl.when(k == nk - 1)
        def _():
            r = acc[...]
            if res is not None:
                r = r + r_ref[...]
            o_ref[...] = r.astype(out_dtype)
            if normed:
                scale = lax.rsqrt(jnp.mean(r * r, axis=1, keepdims=True) + EPS)
                own[n_in + 1][...] = (r * scale * g_ref[...]).astype(BF16)

        _wait_carried(exchange, carried, grid)

    ins = [a, b] + ([] if res is None else [res]) + ([norm_gain] if normed else [])
    in_specs = ([a_spec, b_spec] + ([] if res is None else [o_spec])
                + ([pl.BlockSpec((1, tn), lambda i, j, k: (0, j))] if normed else []))
    out_specs = [o_spec] * n_out
    out_shape = [jax.ShapeDtypeStruct((M, N), out_dtype)] + ([jax.ShapeDtypeStruct((M, N), BF16)] if normed else [])
    scratch = [pltpu.VMEM((tm, tn), F32)]
    if exchange is None:
        outs = pl.pallas_call(
            body, name=name, grid=grid, in_specs=in_specs, out_specs=out_specs, out_shape=out_shape,
            scratch_shapes=scratch, compiler_params=_cparams("parallel", "parallel", "arbitrary"))(*ins)
        targets = ()
    else:
        outs, targets = _carrier_call(body, name, grid, in_specs, out_specs, out_shape, scratch, {}, ins, exchange)
    return (tuple(outs) if normed else outs[0]), targets


ROW_TILE = 512
HEAD_ROWS = 2048


def _row_spec(width):
    return pl.BlockSpec((ROW_TILE, width), lambda i: (i, 0))


def _vec_spec(width):
    return pl.BlockSpec((1, width), lambda i: (0, 0))


def _rmsnorm_fwd(h, g, name):
    T, Dm = h.shape

    def body(h_ref, g_ref, o_ref):
        x = h_ref[...]
        r = lax.rsqrt(jnp.mean(x * x, axis=1, keepdims=True) + EPS)
        o_ref[...] = (x * r * g_ref[...]).astype(BF16)

    return pl.pallas_call(
        body, name=name, grid=(T // ROW_TILE,), in_specs=[_row_spec(Dm), _vec_spec(Dm)],
        out_specs=_row_spec(Dm), out_shape=jax.ShapeDtypeStruct((T, Dm), BF16),
        compiler_params=_cparams("parallel"))(h, g)


def _rmsnorm_bwd(h, g, dhn, dres, name):
    T, Dm = h.shape

    def body(h_ref, g_ref, dy_ref, dr_ref, dh_ref, dg_ref):
        i = pl.program_id(0)
        x = h_ref[...]
        dy = dy_ref[...]
        r = lax.rsqrt(jnp.mean(x * x, axis=1, keepdims=True) + EPS)
        gdy = dy * g_ref[...]
        m = jnp.mean(x * gdy, axis=1, keepdims=True)
        dh_ref[...] = dr_ref[...] + r * gdy - x * (r * r * r * m)
        part = jnp.sum(dy * x * r, axis=0, keepdims=True)

        @pl.when(i == 0)
        def _():
            dg_ref[...] = part

        @pl.when(i > 0)
        def _():
            dg_ref[...] += part

    return pl.pallas_call(
        body, name=name, grid=(T // ROW_TILE,),
        in_specs=[_row_spec(Dm), _vec_spec(Dm), _row_spec(Dm), _row_spec(Dm)],
        out_specs=[_row_spec(Dm), _vec_spec(Dm)],
        out_shape=[jax.ShapeDtypeStruct((T, Dm), F32), jax.ShapeDtypeStruct((1, Dm), F32)],
        compiler_params=_cparams("arbitrary"))(h, g, dhn, dres)


def _ple_fwd(h, z, pp, gain, name):
    T, Dm = h.shape
    normed = gain is not None

    def body(*refs):
        h_ref, z_ref, p_ref = refs[:3]
        y = h_ref[...] + _sig(z_ref[...]) * p_ref[...]
        refs[3 + normed][...] = y
        if normed:
            r = lax.rsqrt(jnp.mean(y * y, axis=1, keepdims=True) + EPS)
            refs[5][...] = (y * r * refs[3][...]).astype(BF16)

    outs = pl.pallas_call(
        body, name=name, grid=(T // ROW_TILE,), in_specs=[_row_spec(Dm)] * 3 + [_vec_spec(Dm)] * normed,
        out_specs=[_row_spec(Dm)] * (1 + normed),
        out_shape=[jax.ShapeDtypeStruct((T, Dm), F32)] + [jax.ShapeDtypeStruct((T, Dm), BF16)] * normed,
        compiler_params=_cparams("parallel"))(*([h, z, pp] + [gain] * normed))
    return outs if normed else (outs[0], None)


def _ple_bwd(dh, z, pp, name):
    T, Dm = dh.shape

    def body(dh_ref, z_ref, p_ref, dz_ref, dp_ref):
        s = _sig(z_ref[...])
        d = dh_ref[...]
        dz_ref[...] = d * p_ref[...] * s * (1.0 - s)
        dp_ref[...] = d * s

    return pl.pallas_call(
        body, name=name, grid=(T // ROW_TILE,), in_specs=[_row_spec(Dm)] * 3, out_specs=[_row_spec(Dm)] * 2,
        out_shape=[jax.ShapeDtypeStruct((T, Dm), F32)] * 2, compiler_params=_cparams("parallel"))(dh, z, pp)


def _loss_fwd_bwd(y, target, name):
    T, Dm = y.shape

    def body(y_ref, t_ref, l_ref, d_ref):
        i = pl.program_id(0)
        e = y_ref[...] - t_ref[...]
        d_ref[...] = e * (1.0 / Dm)
        part = jnp.full((8, LANES), 0.5 / Dm, F32) * jnp.sum(e * e)

        @pl.when(i == 0)
        def _():
            l_ref[...] = part

        @pl.when(i > 0)
        def _():
            l_ref[...] += part

    return pl.pallas_call(
        body, name=name, grid=(T // ROW_TILE,), in_specs=[_row_spec(Dm)] * 2,
        out_specs=[pl.BlockSpec((8, LANES), lambda i: (0, 0)), _row_spec(Dm)],
        out_shape=[jax.ShapeDtypeStruct((8, LANES), F32), jax.ShapeDtypeStruct((T, Dm), F32)],
        compiler_params=_cparams("arbitrary"))(y, target)


def _head_mean_matrix():
    r = _iota2((LANES, LANES), 0) >= SW_DIM
    c = _iota2((LANES, LANES), 1) >= SW_DIM
    return jnp.where(r == c, 1.0 / SW_DIM, 0.0).astype(BF16)


def _headnorm_fwd(x, g_lane, name):
    T = x.shape[0]
    C = g_lane.shape[1]

    def body(x_ref, g_ref, y_ref):
        xv = x_ref[...]
        ms = _dot_exact_l(xv * xv, _head_mean_matrix())
        y_ref[...] = xv * lax.rsqrt(ms + EPS) * g_ref[...]

    rows = min(T, HEAD_ROWS)
    spec = pl.BlockSpec((rows, LANES), lambda j, i: (i, j))
    return pl.pallas_call(
        body, name=name, grid=(C // LANES, T // rows),
        in_specs=[spec, pl.BlockSpec((1, LANES), lambda j, i: (0, j))], out_specs=spec,
        out_shape=jax.ShapeDtypeStruct((T, C), F32), compiler_params=_cparams("parallel", "parallel"))(x, g_lane)


def _headnorm_bwd(x, g_lane, dy_full, name):
    T = x.shape[0]
    C = g_lane.shape[1]

    def body(x_ref, g_ref, dy_ref, dx_ref, dg_ref):
        i = pl.program_id(1)
        xv = x_ref[...]
        dy = dy_ref[...]
        bd = _head_mean_matrix()
        r = lax.rsqrt(_dot_exact_l(xv * xv, bd) + EPS)
        gdy = dy * g_ref[...]
        m = _dot_exact_l(xv * gdy, bd)
        dx_ref[...] = r * gdy - xv * (r * r * r * m)
        part = jnp.sum(dy * xv * r, axis=0, keepdims=True)

        @pl.when(i == 0)
        def _():
            dg_ref[...] = part

        @pl.when(i > 0)
        def _():
            dg_ref[...] += part

    rows = min(T, HEAD_ROWS)
    spec = pl.BlockSpec((rows, LANES), lambda j, i: (i, j))
    vspec = pl.BlockSpec((1, LANES), lambda j, i: (0, j))
    return pl.pallas_call(
        body, name=name, grid=(C // LANES, T // rows), in_specs=[spec, vspec, spec],
        out_specs=[spec, vspec],
        out_shape=[jax.ShapeDtypeStruct(dy_full.shape, F32), jax.ShapeDtypeStruct((1, C), F32)],
        input_output_aliases={2: 0}, compiler_params=_cparams("parallel", "arbitrary"))(x, g_lane, dy_full)


CONV_TILE = 128


def _shift_down(x, k, rows):
    return jnp.where(rows >= k, pltpu.roll(x, k, 0), 0.0)


def _shift_up(x, k, rows):
    n = x.shape[0]
    return jnp.where(rows < n - k, pltpu.roll(x, n - k, 0), 0.0)


def _conv3(u, w_ref, b_ref, rows):
    return (w_ref[2:3, :] * u + w_ref[1:2, :] * _shift_down(u, 1, rows) + w_ref[0:1, :] * _shift_down(u, 2, rows)
            + b_ref[...])


def _convglu_fwd(u, cw, cb, S, name):
    T = u.shape[0]
    nf = D_FF // CONV_TILE

    def body(ug_ref, uu_ref, wg_ref, wu_ref, bg_ref, bu_ref, a_ref):
        rows = _iota2((S, CONV_TILE), 0)
        yg = _conv3(ug_ref[...], wg_ref, bg_ref, rows)
        yu = _conv3(uu_ref[...], wu_ref, bu_ref, rows)
        a_ref[...] = (yg * _sig(yg) * yu).astype(BF16)

    def blk(rows, off):
        return pl.BlockSpec((rows, CONV_TILE), (lambda b, j: (b, j + off)) if rows == S else (lambda b, j: (0, j + off)))

    return pl.pallas_call(
        body, name=name, grid=(T // S, nf),
        in_specs=[blk(S, 0), blk(S, nf), blk(3, 0), blk(3, nf), blk(1, 0), blk(1, nf)],
        out_specs=blk(S, 0), out_shape=jax.ShapeDtypeStruct((T, D_FF), BF16),
        compiler_params=_cparams("parallel", "parallel"))(u, u, cw, cw, cb, cb)


def _convglu_bwd(u, da, cw, cb, S, name, exchange=None):
    T = u.shape[0]
    nf = D_FF // CONV_TILE

    grid = (nf, T // S)

    def body(*refs):
        (ug_ref, uu_ref, da_ref, wg_ref, wu_ref, bg_ref, bu_ref, du_ref, dw_ref, db_ref), carried = _carried(
            exchange, refs, 7, 3, 0)
        _start_carried(exchange, carried, grid)
        b = pl.program_id(1)
        rows = _iota2((S, CONV_TILE), 0)
        ug, uu = ug_ref[...], uu_ref[...]
        yg = _conv3(ug, wg_ref, bg_ref, rows)
        yu = _conv3(uu, wu_ref, bu_ref, rows)
        s = _sig(yg)
        da_v = da_ref[...]
        for half, (uv, w_ref, dy) in enumerate(((ug, wg_ref, da_v * yu * (s * (1.0 + yg * (1.0 - s)))),
                                                (uu, wu_ref, da_v * yg * s))):
            up1, up2 = _shift_up(dy, 1, rows), _shift_up(dy, 2, rows)
            du_ref[half] = w_ref[2:3, :] * dy + w_ref[1:2, :] * up1 + w_ref[0:1, :] * up2
            dws = [jnp.sum(up2 * uv, axis=0, keepdims=True), jnp.sum(up1 * uv, axis=0, keepdims=True),
                   jnp.sum(dy * uv, axis=0, keepdims=True)]
            dbv = jnp.sum(dy, axis=0, keepdims=True)

            @pl.when(b == 0)
            def _():
                for k in range(3):
                    dw_ref[half, k:k + 1, :] = dws[k]
                db_ref[half] = dbv

            @pl.when(b > 0)
            def _():
                for k in range(3):
                    dw_ref[half, k:k + 1, :] += dws[k]
                db_ref[half] += dbv

        _wait_carried(exchange, carried, grid)

    def blk(rows, off):
        return pl.BlockSpec((rows, CONV_TILE), (lambda j, b: (b, j + off)) if rows == S else (lambda j, b: (0, j + off)))

    def both(rows):
        return pl.BlockSpec((2, rows, CONV_TILE), (lambda j, b: (0, b, j)) if rows == S else (lambda j, b: (0, 0, j)))

    outs, targets = _carrier_call(
        body, name, grid, [blk(S, 0), blk(S, nf), blk(S, 0), blk(3, 0), blk(3, nf), blk(1, 0), blk(1, nf)],
        [both(S), both(3), both(1)],
        [jax.ShapeDtypeStruct((2, T, D_FF), F32), jax.ShapeDtypeStruct((2, 3, D_FF), F32),
         jax.ShapeDtypeStruct((2, 1, D_FF), F32)], [], {}, (u, u, da, cw, cw, cb, cb), exchange)
    return (*outs, targets)


def _sb_scores(qb, kblk, on_diag_mask):
    z = _dot(qb, kblk, 1, 1) * (SB_DIM ** -0.5)
    l1 = jnp.log(1.0 + jnp.exp(-jnp.abs(z)))
    ls = jnp.minimum(z, 0.0) - l1
    lk = jnp.where(on_diag_mask, ls - z, 0.0)
    return ls, lk


def _sb_fwd(proj, S, name, exchange=None):
    T = proj.shape[0]
    BQ, BK = SB_QBLOCK, ATT_BLOCK
    unroll = BQ // BK
    nq = S // BQ
    nhp = SB_HEADS // 2
    heads = [slice(h * SB_DIM, (h + 1) * SB_DIM) for h in range(2)]

    grid = (T // S, nhp)

    def body(*refs):
        (q_ref, k_ref, v_ref, o_ref, tot_ref), carried = _carried(exchange, refs, 3, 2, 0)
        _start_carried(exchange, carried, grid)
        ahead = _iota2((BQ, BK), 1) - _iota2((BQ, BK), 0)
        upper = (_iota2((BK, BK), 0) > _iota2((BK, BK), 1)).astype(BF16)

        def qloop(iq, carry):
            q0 = pl.multiple_of(iq * BQ, BQ)
            rows = pl.ds(q0, BQ)
            qbs = [q_ref[rows, sl] for sl in heads]
            nkb = (iq + 1) * (BQ // BK)

            def kloop(jj, kc):
                blocks = []
                for u in range(unroll):
                    k0 = pl.multiple_of((nkb - 1 - unroll * jj - u) * BK, BK)
                    blocks.append((pl.ds(k0, BK), ahead < q0 - k0))
                units = [(h, krows, mask) for krows, mask in blocks for h in range(2)]
                scores = [_sb_scores(qbs[h], k_ref[krows, heads[h]], mask) for h, krows, mask in units]
                laters = [_dot_exact_l(lk, upper) for _, lk in scores]
                runs = [kc[h][0] for h in range(2)]
                accs = [kc[h][1] for h in range(2)]
                for (h, krows, mask), (ls, lk), later in zip(units, scores, laters):
                    w = jnp.where(mask, jnp.exp(ls + later + runs[h]), 0.0)
                    accs[h] = accs[h] + _dot(w, v_ref[krows, heads[h]], 1, 0)
                    runs[h] = runs[h] + jnp.sum(lk, axis=1, keepdims=True)
                return tuple((runs[h], accs[h]) for h in range(2))

            init = (jnp.zeros((BQ, 1), F32), jnp.zeros((BQ, SB_DIM), F32))
            res = lax.fori_loop(0, nkb // unroll, kloop, (init, init))
            for h, (sl, (run, acc)) in enumerate(zip(heads, res)):
                o_ref[rows, sl] = acc
                tot_ref[rows, h:h + 1] = run
            return carry

        lax.fori_loop(0, nq, qloop, 0)
        _wait_carried(exchange, carried, grid)

    def spec(off):
        return pl.BlockSpec((S, LANES), lambda b, hp: (b, hp + off))

    (cat, tot), targets = _carrier_call(
        body, name, grid, [spec(0), spec(nhp), spec(2 * nhp)],
        [spec(0), pl.BlockSpec((None, S, 2), lambda b, hp: (hp, b, 0))],
        [jax.ShapeDtypeStruct((T, 2 * SB_HEADS * SB_DIM), F32), jax.ShapeDtypeStruct((nhp, T, 2), F32)],
        [], {}, (proj, proj, proj), exchange)
    return cat, tot, targets


def _sb_bwd(proj, tot, dcat, S, name, exchange=None):
    T, width = proj.shape
    BQ, BK = SB_QBLOCK, ATT_BLOCK
    unroll = BQ // BK
    nq = S // BQ
    nhp = SB_HEADS // 2
    scale = SB_DIM ** -0.5
    heads = [slice(h * SB_DIM, (h + 1) * SB_DIM) for h in range(2)]

    grid = (T // S, nhp)

    def body(*refs):
        (q_ref, k_ref, v_ref, tot_ref, do_ref, dp_hbm, dq_s, dk_s, dv_s, sems), carried = _carried(
            exchange, refs, 5, 1, 4)
        _start_carried(exchange, carried, grid)
        b, hp = pl.program_id(0), pl.program_id(1)
        ahead = _iota2((BQ, BK), 1) - _iota2((BQ, BK), 0)
        r, c = _iota2((BK, BK), 0), _iota2((BK, BK), 1)
        upto = (r <= c).astype(BF16)
        earlier = (r < c).astype(BF16)
        dk_s[...] = jnp.zeros_like(dk_s)
        dv_s[...] = jnp.zeros_like(dv_s)

        def qloop(iq, carry):
            q0 = pl.multiple_of(iq * BQ, BQ)
            rows = pl.ds(q0, BQ)
            qbs = [q_ref[rows, sl] for sl in heads]
            dobs = [do_ref[rows, sl] for sl in heads]
            totals = [tot_ref[rows, h:h + 1] for h in range(2)]

            def kloop(m, kc):
                blocks = []
                for u in range(unroll):
                    k0 = pl.multiple_of((unroll * m + u) * BK, BK)
                    blocks.append((pl.ds(k0, BK), ahead < q0 - k0))
                units = [(h, krows, mask) for krows, mask in blocks for h in range(2)]
                kblks = [k_ref[krows, heads[h]] for h, krows, _ in units]
                scores = [_sb_scores(qbs[h], kblk, mask) for (h, _, mask), kblk in zip(units, kblks)]
                prefixes = [_dot_exact_l(lk, upto) for _, lk in scores]
                dws = [_dot(dobs[h], v_ref[krows, heads[h]], 1, 1) for h, krows, _ in units]
                runs = [kc[h][0] for h in range(2)]
                gruns = [kc[h][1] for h in range(2)]
                dqs = [kc[h][2] for h in range(2)]
                ws, gs = [], []
                for (h, _, mask), (ls, lk), prefix, dw in zip(units, scores, prefixes, dws):
                    w = jnp.where(mask, jnp.exp(ls + (totals[h] - (prefix + runs[h]))), 0.0)
                    runs[h] = runs[h] + jnp.sum(lk, axis=1, keepdims=True)
                    ws.append(w)
                    gs.append(w * dw)
                gprefixes = [_dot_exact_l(g, earlier) for g in gs]
                dzs = []
                for (h, _, mask), (ls, _), g, gprefix in zip(units, scores, gs, gprefixes):
                    beta = jnp.exp(ls)
                    dzs.append(jnp.where(mask, g * (1.0 - beta) - beta * (gprefix + gruns[h]), 0.0) * scale)
                    gruns[h] = gruns[h] + jnp.sum(g, axis=1, keepdims=True)
                for (h, krows, _), kblk, w, dz in zip(units, kblks, ws, dzs):
                    dv_s[krows, heads[h]] += _dot(w, dobs[h], 0, 0)
                    dk_s[krows, heads[h]] += _dot(dz, qbs[h], 0, 0)
                    dqs[h] = dqs[h] + _dot(dz, kblk, 1, 0)
                return tuple((runs[h], gruns[h], dqs[h]) for h in range(2))

            zero = jnp.zeros((BQ, 1), F32)
            init = (zero, zero, jnp.zeros((BQ, SB_DIM), F32))
            res = lax.fori_loop(0, (iq + 1) * (BQ // BK) // unroll, kloop, (init, init))
            for sl, (_, _, dq) in zip(heads, res):
                dq_s[rows, sl] = dq
            return carry

        lax.fori_loop(0, nq, qloop, 0)
        r0 = pl.multiple_of(b * S, S)
        copies = []
        for n, buf in enumerate((dq_s, dk_s, dv_s)):
            c0 = pl.multiple_of((hp + n * nhp) * LANES, LANES)
            copies.append(pltpu.make_async_copy(buf, dp_hbm.at[pl.ds(r0, S), pl.ds(c0, LANES)], sems.at[n]))
        for cp in copies:
            cp.start()
        for cp in copies:
            cp.wait()
        _wait_carried(exchange, carried, grid)

    def spec(off):
        return pl.BlockSpec((S, LANES), lambda b, hp: (b, hp + off))

    (dproj,), targets = _carrier_call(
        body, name, grid,
        [spec(0), spec(nhp), spec(2 * nhp), pl.BlockSpec((None, S, 2), lambda b, hp: (hp, b, 0)), spec(0)],
        [ANY], [jax.ShapeDtypeStruct((T, width), F32)],
        [pltpu.VMEM((S, LANES), F32)] * 3 + [pltpu.SemaphoreType.DMA((3,))], {}, (proj, proj, proj, tot, dcat),
        exchange)
    return dproj, targets


HG_COL0 = 3 * SB_HEADS * SB_DIM // LANES


def _hg_gates(q, fp, lbv):
    sg = _sig(fp)
    f = lbv + (1.0 - lbv) * sg
    kk = (1.0 - lbv) * _sig(-fp)
    sq = _sig(q)
    return sg, f, kk, sq


def _hg_chunk(qs, kk, lf, incl):
    C = HG_CHUNK
    b = _dot_exact_r(incl, lf)
    bl = b[C - 1:C, :]
    bm = b[C // 2 - 1:C // 2, :]
    e_t = jnp.exp(b - bm)
    e_s = jnp.exp(bm - b)
    e_i = jnp.exp(b)
    e_e = jnp.exp(bl - b)
    return bl, e_t, e_s, e_i, e_e


def _hgrn_fwd(proj, cat, lb, hgn, S, name, exchange=None):
    T = proj.shape[0]
    B = T // S
    C = HG_CHUNK
    NC = S // C

    grid = (B, HG_HEADS // HG_STEP)

    def body(*refs):
        (q_ref, f_ref, i_ref, g_ref, lb_ref, hgn_ref, _, ob_ref, oraw_ref, st_ref, state), carried = _carried(
            exchange, refs, 7, 3, 1)
        _start_carried(exchange, carried, grid)
        state[...] = jnp.zeros_like(state)
        row, col = _iota2((C, C), 0), _iota2((C, C), 1)
        causal = row >= col
        incl = causal.astype(BF16)

        def trip(m, carry):
            units = []
            for u in range(HG_TRIP):
                c = m * HG_TRIP + u
                rows = pl.ds(pl.multiple_of(c * C, C), C)
                units += [(c, hh, rows, slice(hh * LANES, (hh + 1) * LANES)) for hh in range(HG_STEP)]
            qs_, kks, lfs = [], [], []
            for _, _, rows, hs in units:
                q = q_ref[rows, hs]
                _, f, kk, sq = _hg_gates(q, f_ref[rows, hs], lb_ref[:, hs])
                qs_.append(q * sq)
                kks.append(kk)
                lfs.append(jnp.log(f))
            decays = [_hg_chunk(None, None, lf, incl) for lf in lfs]
            ps = [jnp.where(causal, _dot(qs * e_t, kk * e_s, 1, 1), 0.0)
                  for qs, kk, (_, e_t, e_s, _, _) in zip(qs_, kks, decays)]
            outs = []
            for (c, hh, rows, hs), qs, kk, (bl, _, _, e_i, e_e), p in zip(units, qs_, kks, decays, ps):
                iv = i_ref[rows, hs]
                st = state[hh]
                st_ref[0, hh, c] = st
                outs.append(_dot(qs * e_i, st, 1, 1) + _dot(p, iv, 1, 0))
                state[hh] = st * jnp.exp(bl) + _dot(iv, kk * e_e, 0, 0)
            for (_, _, rows, hs), o in zip(units, outs):
                gv = g_ref[rows, hs]
                oraw_ref[rows, hs] = o
                r = lax.rsqrt(jnp.mean(o * o, axis=1, keepdims=True) + EPS)
                ob_ref[rows, hs] = o * r * hgn_ref[...] * (gv * _sig(gv))
            return carry

        lax.fori_loop(0, NC // HG_TRIP, trip, 0)
        _wait_carried(exchange, carried, grid)

    width = HG_STEP * LANES
    col0 = HG_COL0 * LANES // width
    nstep = HG_HEADS // HG_STEP

    def spec(off):
        return pl.BlockSpec((S, width), lambda b, h: (b, h + off))

    outs, targets = _carrier_call(
        body, name, grid,
        [spec(col0), spec(col0 + nstep), spec(col0 + 2 * nstep), spec(col0 + 3 * nstep),
         pl.BlockSpec((1, width), lambda b, h: (0, h)), pl.BlockSpec((1, LANES), lambda b, h: (0, 0)), ANY],
        [spec(nstep), spec(0), pl.BlockSpec((1, HG_STEP, NC, LANES, LANES), lambda b, h: (b, h, 0, 0, 0))],
        [jax.ShapeDtypeStruct(cat.shape, F32), jax.ShapeDtypeStruct((T, HG_HEADS * LANES), F32),
         jax.ShapeDtypeStruct((B, HG_HEADS, NC, LANES, LANES), F32)],
        [pltpu.VMEM((HG_STEP, LANES, LANES), F32)], {6: 0}, (proj, proj, proj, proj, lb, hgn, cat), exchange)
    return (*outs, targets)


def _hgrn_bwd(proj, oraw, dcat, states, lb, hgn, dproj, S, name):
    T = proj.shape[0]
    B = T // S
    C = HG_CHUNK
    NC = S // C

    def body(q_ref, f_ref, i_ref, g_ref, oraw_ref, dy_ref, st_ref, lb_ref, hgn_ref, dp_in,
             dp_hbm, dlb_ref, dhgn_ref, dstate, dq_s, df_s, di_s, dg_s, sems):
        del dp_in
        h, b = pl.program_id(0), pl.program_id(1)
        row, col = _iota2((C, C), 0), _iota2((C, C), 1)
        causal = row >= col
        incl = causal.astype(BF16)
        last_row = _iota2((C, LANES), 0) == C - 1
        hg = hgn_ref[...]
        dstate[...] = jnp.zeros_like(dstate)

        @pl.when(b == 0)
        def _():
            dlb_ref[...] = jnp.zeros_like(dlb_ref)

        @pl.when(jnp.logical_and(b == 0, h == 0))
        def _():
            dhgn_ref[...] = jnp.zeros_like(dhgn_ref)

        def trip(m, carry):
            units = []
            for u in range(HG_TRIP):
                c = NC - 1 - (m * HG_TRIP + u)
                rows = pl.ds(pl.multiple_of(c * C, C), C)
                units += [(c, hh, rows, slice(hh * LANES, (hh + 1) * LANES)) for hh in range(HG_STEP)]
            dos = []
            dhgn = jnp.zeros((1, LANES), F32)
            for _, _, rows, hs in units:
                gv, o, dy = g_ref[rows, hs], oraw_ref[rows, hs], dy_ref[rows, hs]
                r = lax.rsqrt(jnp.mean(o * o, axis=1, keepdims=True) + EPS)
                on = o * r
                sgv = _sig(gv)
                silu_g = gv * sgv
                dg_s[rows, hs] = dy * on * hg * (sgv * (1.0 + gv * (1.0 - sgv)))
                dhgn = dhgn + jnp.sum(dy * on * silu_g, axis=0, keepdims=True)
                dn = dy * hg * silu_g
                dos.append(r * dn - o * (r * r * r * jnp.mean(o * dn, axis=1, keepdims=True)))
            dhgn_ref[...] += dhgn
            pre = []
            for (_, _, rows, hs), do in zip(units, dos):
                q, iv = q_ref[rows, hs], i_ref[rows, hs]
                sg, f, kk, sq = _hg_gates(q, f_ref[rows, hs], lb_ref[:, hs])
                pre.append((q, iv, sg, f, kk, sq, q * sq))
            decays = [_hg_chunk(None, None, jnp.log(f), incl) for _, _, _, f, _, _, _ in pre]
            prods = [(qs * e_t, kk * e_s, qs * e_i, kk * e_e)
                     for (_, _, _, _, kk, _, qs), (_, e_t, e_s, e_i, e_e) in zip(pre, decays)]
            ps = [jnp.where(causal, _dot(qd, kd, 1, 1), 0.0) for qd, kd, _, _ in prods]
            dps = [jnp.where(causal, _dot(do, iv, 1, 1), 0.0) for do, (_, iv, _, _, _, _, _) in zip(dos, pre)]
            dqds = [_dot(dp, kd, 1, 0) for dp, (_, kd, _, _) in zip(dps, prods)]
            dkds = [_dot(dp, qd, 0, 0) for dp, (qd, _, _, _) in zip(dps, prods)]
            pdos = [_dot(p, do, 0, 0) for p, do in zip(ps, dos)]
            chain = []
            for (c, hh, rows, hs), do, (_, iv, _, _, _, _, _), (bl, _, _, _, _), (_, _, qi, ke), pdo in zip(
                    units, dos, pre, decays, prods, pdos):
                st = st_ref[0, hh, c]
                dst = dstate[hh]
                ebl = jnp.exp(bl)
                dqi = _dot(do, st, 1, 0)
                di_s[rows, hs] = pdo + _dot(ke, dst, 1, 1)
                dke = _dot(iv, dst, 1, 0)
                dbl = jnp.sum(st * dst, axis=0, keepdims=True) * ebl + jnp.sum(dke * ke, axis=0, keepdims=True)
                dstate[hh] = _dot(do, qi, 0, 0) + dst * ebl
                chain.append((dqi, dke, dbl))
            for (_, _, rows, hs), (q, _, sg, f, _, sq, _), (_, e_t, e_s, e_i, e_e), (qd, kd, qi, ke), dqd, dkd, (
                    dqi, dke, dbl) in zip(units, pre, decays, prods, dqds, dkds, chain):
                lbv = lb_ref[:, hs]
                db = dqd * qd - dkd * kd + dqi * qi - dke * ke + jnp.where(last_row, dbl, 0.0)
                dqs = dqd * e_t + dqi * e_i
                dkk = dkd * e_s + dke * e_e
                dlf = _dot_exact_r(incl, db, cm=0)
                oms = 1.0 - sg
                dfd = dlf / f
                df_s[rows, hs] = (dfd - dkk) * (1.0 - lbv) * sg * oms
                dq_s[rows, hs] = dqs * (sq * (1.0 + q * (1.0 - sq)))
                dlb_ref[:, hs] += jnp.sum((dfd - dkk) * oms, axis=0, keepdims=True)
            return carry

        lax.fori_loop(0, NC // HG_TRIP, trip, 0)
        r0 = pl.multiple_of(b * S, S)
        copies = []
        for n, buf in enumerate((dq_s, df_s, di_s, dg_s)):
            c0 = pl.multiple_of((col0 + n * nstep + h) * width, width)
            copies.append(pltpu.make_async_copy(buf, dp_hbm.at[pl.ds(r0, S), pl.ds(c0, width)], sems.at[n]))
        for cp in copies:
            cp.start()
        for cp in copies:
            cp.wait()

    width = HG_STEP * LANES
    col0 = HG_COL0 * LANES // width
    nstep = HG_HEADS // HG_STEP

    def spec(off):
        return pl.BlockSpec((S, width), lambda h, b: (b, h + off))

    return pl.pallas_call(
        body, name=name, grid=(nstep, B),
        in_specs=[spec(col0), spec(col0 + nstep), spec(col0 + 2 * nstep), spec(col0 + 3 * nstep), spec(0), spec(nstep),
                  pl.BlockSpec((1, HG_STEP, NC, LANES, LANES), lambda h, b: (b, h, 0, 0, 0)),
                  pl.BlockSpec((1, width), lambda h, b: (0, h)), pl.BlockSpec((1, LANES), lambda h, b: (0, 0)), ANY],
        out_specs=[ANY, pl.BlockSpec((1, width), lambda h, b: (0, h)), pl.BlockSpec((1, LANES), lambda h, b: (0, 0))],
        out_shape=[jax.ShapeDtypeStruct(dproj.shape, F32), jax.ShapeDtypeStruct((1, HG_HEADS * LANES), F32),
                   jax.ShapeDtypeStruct((1, LANES), F32)],
        scratch_shapes=[pltpu.VMEM((HG_STEP, LANES, LANES), F32)] + [pltpu.VMEM((S, width), F32)] * 4
        + [pltpu.SemaphoreType.DMA((4,))],
        input_output_aliases={9: 0},
        compiler_params=_cparams("arbitrary", "arbitrary"))(proj, proj, proj, proj, oraw, dcat, states, lb, hgn, dproj)


def _lower_bound_fwd(logits, name):
    assert logits.shape[0] == 2

    def body(l_ref, o_ref):
        l0, l1 = l_ref[0:1, :], l_ref[1:2, :]
        m = jnp.maximum(l0, l1)
        e0, e1 = jnp.exp(l0 - m), jnp.exp(l1 - m)
        o_ref[0:1, :] = jnp.zeros_like(l0)
        o_ref[1:2, :] = e1 / (e0 + e1)

    return pl.pallas_call(body, name=name, out_shape=jax.ShapeDtypeStruct(logits.shape, F32))(logits)


def _lower_bound_bwd(logits, dlb, name):
    def body(l_ref, d_ref, o_ref):
        l0, l1 = l_ref[0:1, :], l_ref[1:2, :]
        m = jnp.maximum(l0, l1)
        e0, e1 = jnp.exp(l0 - m), jnp.exp(l1 - m)
        s1 = e1 / (e0 + e1)
        t = s1 * (1.0 - s1) * d_ref[1:2, :]
        o_ref[0:1, :] = -t
        o_ref[1:2, :] = t

    return pl.pallas_call(body, name=name, out_shape=jax.ShapeDtypeStruct(logits.shape, F32))(logits, dlb)


def _bucket_thresholds():
    dist = np.arange(WINDOW)
    max_exact = N_BUCKETS // 2
    large = max_exact + (np.log(np.maximum(dist, max_exact) / max_exact) / math.log(MAX_DISTANCE / max_exact)
                         * (N_BUCKETS - max_exact)).astype(np.int32)
    bucket = np.where(dist < max_exact, dist, np.minimum(large, N_BUCKETS - 1))
    assert np.all(np.diff(bucket) >= 0)
    return [int(np.argmax(bucket >= k)) if np.any(bucket >= k) else 10 ** 6 for k in range(1, N_BUCKETS)]


def _band_bucket():
    dist = _iota2((WINDOW, 2 * WINDOW), 0) + WINDOW - _iota2((WINDOW, 2 * WINDOW), 1)
    bucket = jnp.zeros((WINDOW, 2 * WINDOW), jnp.int32)
    for thr in _bucket_thresholds():
        bucket = bucket + (dist >= thr).astype(jnp.int32)
    band = jnp.logical_and(dist >= 0, dist < WINDOW)
    return bucket, band


def _bias_build(rel_bias, name):
    def body(rb_ref, o_ref):
        h = pl.program_id(0)
        bucket, _ = _band_bucket()
        bias = jnp.zeros((WINDOW, 2 * WINDOW), F32)
        for k in range(N_BUCKETS):
            bias = jnp.where(bucket == k, rb_ref[k, h], bias)
        o_ref[0] = bias

    return pl.pallas_call(
        body, name=name, grid=(SW_HEADS,), in_specs=[pl.BlockSpec(memory_space=pltpu.SMEM)],
        out_specs=pl.BlockSpec((1, WINDOW, 2 * WINDOW), lambda h: (h, 0, 0)),
        out_shape=jax.ShapeDtypeStruct((SW_HEADS, WINDOW, 2 * WINDOW), F32),
        compiler_params=_cparams("parallel"))(rel_bias)


def _bias_reduce(dbias, name):
    def body(d_ref, o_ref):
        bucket, band = _band_bucket()
        d = jnp.where(band, d_ref[0], 0.0)
        lane = _iota2((1, LANES), 1)
        out = jnp.zeros((1, LANES), F32)
        for k in range(N_BUCKETS):
            out = jnp.where(lane == k, jnp.sum(jnp.where(bucket == k, d, 0.0)), out)
        o_ref[0] = out

    return pl.pallas_call(
        body, name=name, grid=(SW_HEADS,), in_specs=[pl.BlockSpec((1, WINDOW, 2 * WINDOW), lambda h: (h, 0, 0))],
        out_specs=pl.BlockSpec((1, 1, LANES), lambda h: (h, 0, 0)),
        out_shape=jax.ShapeDtypeStruct((SW_HEADS, 1, LANES), F32), compiler_params=_cparams("parallel"))(dbias)


SW_Q_COLS = SW_HEADS * SW_DIM
SW_K_BLOCK0 = SW_Q_COLS // LANES
SW_V_BLOCK0 = SW_K_BLOCK0 + SW_KV * SW_DIM // LANES
SW_STEP_HEADS = 8


def _swa_logits(qb, kprev, kcur, bias_ref, hl, mprev, mcur):
    scale = SW_DIM ** -0.5
    lp = jnp.where(mprev, _dot(qb, kprev, 1, 1) * scale + bias_ref[hl, :, 0:WINDOW], NEG)
    lc = jnp.where(mcur, _dot(qb, kcur, 1, 1) * scale + bias_ref[hl, :, WINDOW:2 * WINDOW], NEG)
    return lp, lc


def _swa_softmax(lp, lc, sink):
    m = jnp.maximum(jnp.maximum(jnp.max(lp, axis=1, keepdims=True), jnp.max(lc, axis=1, keepdims=True)), sink)
    ep, ec = jnp.exp(lp - m), jnp.exp(lc - m)
    es = jnp.exp(sink - m)
    den = jnp.sum(ep, axis=1, keepdims=True) + jnp.sum(ec, axis=1, keepdims=True) + es
    return ep, ec, es, den


def _swa_block_heads(n, q_ref, k_ref, v_ref, bias_ref, sink_ref, kp, above, mcur):
    W = WINDOW
    rows = pl.ds(pl.multiple_of(n * W, W), W)
    prow = pl.ds(pl.multiple_of(jnp.maximum(n - 1, 0) * W, W), W)
    mprev = jnp.logical_and(above, n > 0)
    heads = []
    for kvh in range(2):
        ksl = slice(kvh * SW_DIM, (kvh + 1) * SW_DIM)
        kv = (k_ref[rows, ksl], k_ref[prow, ksl], v_ref[rows, ksl], v_ref[prow, ksl])
        for g in range(4):
            hl = kvh * 4 + g
            heads.append((hl, ksl, slice(hl * SW_DIM, (hl + 1) * SW_DIM), kv))
    qbs = [q_ref[rows, qsl] for _, _, qsl, _ in heads]
    logits = [_swa_logits(qb, kv[1], kv[0], bias_ref, hl, mprev, mcur) for qb, (hl, _, _, kv) in zip(qbs, heads)]
    soft = [_swa_softmax(lp, lc, sink_ref[kp * SW_STEP_HEADS + hl]) for (lp, lc), (hl, _, _, _) in zip(logits, heads)]
    return rows, prow, heads, qbs, soft


def _swa_fwd(qkn, proj, bias, sinks, S, name, exchange=None):
    T = qkn.shape[0]
    W = WINDOW
    nb = S // W

    grid = (T // S, 2)

    def body(*refs):
        (q_ref, k_ref, v_ref, bias_ref, sink_ref, o_ref), carried = _carried(exchange, refs, 5, 1, 0)
        _start_carried(exchange, carried, grid)
        kp = pl.program_id(1)
        row, col = _iota2((W, W), 0), _iota2((W, W), 1)
        mcur = col <= row
        above = col > row

        def blk(n, carry):
            rows, _, heads, _, soft = _swa_block_heads(n, q_ref, k_ref, v_ref, bias_ref, sink_ref, kp, above, mcur)
            outs = [(_dot(ep, kv[3], 1, 0) + _dot(ec, kv[2], 1, 0)) / den
                    for (ep, ec, _, den), (_, _, _, kv) in zip(soft, heads)]
            for (_, _, qsl, _), o in zip(heads, outs):
                o_ref[rows, qsl] = o
            return carry

        lax.fori_loop(0, nb, blk, 0)
        _wait_carried(exchange, carried, grid)

    (o,), targets = _carrier_call(
        body, name, grid,
        [pl.BlockSpec((S, 4 * LANES), lambda b, kp: (b, kp)),
         pl.BlockSpec((S, LANES), lambda b, kp: (b, SW_K_BLOCK0 + kp)),
         pl.BlockSpec((S, LANES), lambda b, kp: (b, SW_V_BLOCK0 + kp)),
         pl.BlockSpec((SW_STEP_HEADS, W, 2 * W), lambda b, kp: (kp, 0, 0)),
         pl.BlockSpec(memory_space=pltpu.SMEM)],
        [pl.BlockSpec((S, 4 * LANES), lambda b, kp: (b, kp))], [jax.ShapeDtypeStruct((T, SW_Q_COLS), F32)],
        [], {}, (qkn, qkn, proj, bias, sinks), exchange)
    return o, targets


def _swa_bwd(qkn, proj, bias, sinks, do, S, name, exchange=None):
    T, width = proj.shape
    W = WINDOW
    nb = S // W
    scale = SW_DIM ** -0.5

    grid = (2, T // S)

    def body(*refs):
        (q_ref, k_ref, v_ref, bias_ref, sink_ref, do_ref, dp_hbm, dbias_ref, dsink_ref,
         dq_s, dk_s, dv_s, sems), carried = _carried(exchange, refs, 6, 3, 4)
        _start_carried(exchange, carried, grid)
        kp, b = pl.program_id(0), pl.program_id(1)
        row, col = _iota2((W, W), 0), _iota2((W, W), 1)
        mcur = col <= row
        above = col > row
        dk_s[...] = jnp.zeros_like(dk_s)
        dv_s[...] = jnp.zeros_like(dv_s)

        @pl.when(b == 0)
        def _():
            dbias_ref[...] = jnp.zeros_like(dbias_ref)
            dsink_ref[...] = jnp.zeros_like(dsink_ref)

        def blk(n, carry):
            rows, prow, heads, qbs, soft = _swa_block_heads(n, q_ref, k_ref, v_ref, bias_ref, sink_ref, kp, above,
                                                            mcur)
            dobs = [do_ref[rows, qsl] for _, _, qsl, _ in heads]
            dps = [(_dot(dob, kv[3], 1, 1), _dot(dob, kv[2], 1, 1)) for dob, (_, _, _, kv) in zip(dobs, heads)]
            grads = []
            for (hl, _, _, _), (ep, ec, es, den), (dpp, dpc) in zip(heads, soft, dps):
                inv = 1.0 / den
                pp, pc = ep * inv, ec * inv
                total = jnp.sum(pp * dpp, axis=1, keepdims=True) + jnp.sum(pc * dpc, axis=1, keepdims=True)
                dlp = pp * (dpp - total)
                dlc = pc * (dpc - total)
                dsink_ref[hl:hl + 1, :] += jnp.zeros((1, LANES), F32) - jnp.sum(es * inv * total)
                dbias_ref[hl, :, 0:W] += dlp
                dbias_ref[hl, :, W:2 * W] += dlc
                grads.append((pp, pc, dlp, dlc))
            for (_, _, qsl, kv), (_, _, dlp, dlc) in zip(heads, grads):
                dq_s[rows, qsl] = (_dot(dlp, kv[1], 1, 0) + _dot(dlc, kv[0], 1, 0)) * scale
            for kvh in range(2):
                group = range(4 * kvh, 4 * kvh + 4)
                ksl = heads[4 * kvh][1]
                dk_s[prow, ksl] += sum(_dot(grads[i][2], qbs[i], 0, 0) for i in group) * scale
                dk_s[rows, ksl] += sum(_dot(grads[i][3], qbs[i], 0, 0) for i in group) * scale
                dv_s[prow, ksl] += sum(_dot(grads[i][0], dobs[i], 0, 0) for i in group)
                dv_s[rows, ksl] += sum(_dot(grads[i][1], dobs[i], 0, 0) for i in group)
            return carry

        lax.fori_loop(0, nb, blk, 0)
        r0 = pl.multiple_of(b * S, S)
        cq = pl.multiple_of(kp * 4 * LANES, LANES)
        ck = pl.multiple_of((SW_K_BLOCK0 + kp) * LANES, LANES)
        cv = pl.multiple_of((SW_V_BLOCK0 + kp) * LANES, LANES)
        copies = [pltpu.make_async_copy(dq_s, dp_hbm.at[pl.ds(r0, S), pl.ds(cq, 4 * LANES)], sems.at[0]),
                  pltpu.make_async_copy(dk_s, dp_hbm.at[pl.ds(r0, S), pl.ds(ck, LANES)], sems.at[1]),
                  pltpu.make_async_copy(dv_s, dp_hbm.at[pl.ds(r0, S), pl.ds(cv, LANES)], sems.at[2])]
        for cp in copies:
            cp.start()
        for cp in copies:
            cp.wait()
        _wait_carried(exchange, carried, grid)

    qspec = pl.BlockSpec((S, 4 * LANES), lambda kp, b: (b, kp))
    outs, targets = _carrier_call(
        body, name, grid,
        [qspec, pl.BlockSpec((S, LANES), lambda kp, b: (b, SW_K_BLOCK0 + kp)),
         pl.BlockSpec((S, LANES), lambda kp, b: (b, SW_V_BLOCK0 + kp)),
         pl.BlockSpec((SW_STEP_HEADS, W, 2 * W), lambda kp, b: (kp, 0, 0)),
         pl.BlockSpec(memory_space=pltpu.SMEM), qspec],
        [ANY, pl.BlockSpec((SW_STEP_HEADS, W, 2 * W), lambda kp, b: (kp, 0, 0)),
         pl.BlockSpec((SW_STEP_HEADS, LANES), lambda kp, b: (kp, 0))],
        [jax.ShapeDtypeStruct((T, width), F32), jax.ShapeDtypeStruct((SW_HEADS, W, 2 * W), F32),
         jax.ShapeDtypeStruct((SW_HEADS, LANES), F32)],
        [pltpu.VMEM((S, 4 * LANES), F32), pltpu.VMEM((S, LANES), F32), pltpu.VMEM((S, LANES), F32),
         pltpu.SemaphoreType.DMA((3,))], {}, (qkn, qkn, proj, bias, sinks, do), exchange)
    return (*outs, targets)


CHIP_FLIPS = ((1, 0), (0, 1), (1, 1))


def _flip(v, f):
    return 1 - v if f else v


class _Exchange:
    def __init__(self, sources, targets, copies):
        self.sources, self.targets = list(sources), list(targets)
        self._copies = copies
        n = len(self.sources)
        self.scratch = [pltpu.SemaphoreType.DMA((n, 3)), pltpu.SemaphoreType.DMA((n, 3)),
                        pltpu.SemaphoreType.DMA((n,))]

    def _descriptors(self, srcs, dsts, sems, chip, peers):
        send, recv, loc = sems
        out = []
        for t, (local, remote) in enumerate(self._copies(srcs, dsts, chip)):
            out.append(pltpu.make_async_copy(local[0], local[1], loc.at[t]))
            for r, (src, dst) in enumerate(remote):
                out.append(pltpu.make_async_remote_copy(src, dst, send.at[t, r], recv.at[t, r],
                                                        device_id=peers[r], device_id_type=MESH))
        return out

    def start(self, srcs, dsts, sems):
        x, y, c = lax.axis_index("x"), lax.axis_index("y"), lax.axis_index("c")
        peers = [(_flip(x, fx), _flip(y, fy), c) for fx, fy in CHIP_FLIPS]
        for chip in range(4):
            @pl.when(2 * x + y == chip)
            def _():
                for cp in self._descriptors(srcs, dsts, sems, chip, peers):
                    cp.start()

    def wait(self, srcs, dsts, sems):
        me = (lax.axis_index("x"), lax.axis_index("y"), lax.axis_index("c"))
        for cp in self._descriptors(srcs, dsts, sems, 0, [me] * 3):
            cp.wait()

    def operands(self):
        return self.sources + self.targets

    def specs(self):
        ns, nt = len(self.sources), len(self.targets)
        return [ANY] * (ns + nt), [ANY] * nt, [jax.ShapeDtypeStruct(t.shape, t.dtype) for t in self.targets]

    def aliases(self, n_in, n_out):
        ns = len(self.sources)
        return {n_in + ns + i: n_out + i for i in range(len(self.targets))}

    def split(self, refs, n_in, n_out, n_scr):
        ns, nt = len(self.sources), len(self.targets)
        o0 = n_in + ns + nt
        s0 = o0 + n_out + nt
        own = list(refs[:n_in]) + list(refs[o0:o0 + n_out]) + list(refs[s0:s0 + n_scr])
        return own, (refs[n_in:n_in + ns], refs[o0 + n_out:o0 + n_out + nt], refs[s0 + n_scr:])


def _carried(exchange, refs, n_in, n_out, n_scr):
    if exchange is None:
        return list(refs), None
    return exchange.split(refs, n_in, n_out, n_scr)


def _grid_edge(grid, last):
    conds = [pl.program_id(d) == (n - 1 if last else 0) for d, n in enumerate(grid)]
    out = conds[0]
    for cnd in conds[1:]:
        out = jnp.logical_and(out, cnd)
    return out


def _start_carried(exchange, parts, grid):
    if parts is not None:
        @pl.when(_grid_edge(grid, False))
        def _():
            exchange.start(*parts)


def _wait_carried(exchange, parts, grid):
    if parts is not None:
        @pl.when(_grid_edge(grid, True))
        def _():
            exchange.wait(*parts)


def _carrier_call(body, name, grid, in_specs, out_specs, out_shape, scratch, aliases, operands, exchange):
    n_out = len(out_shape)
    aliases = dict(aliases)
    if exchange is not None:
        ex_in, ex_out, ex_shape = exchange.specs()
        aliases.update(exchange.aliases(len(in_specs), n_out))
        in_specs, out_specs, out_shape = in_specs + ex_in, out_specs + ex_out, out_shape + ex_shape
        scratch = scratch + exchange.scratch
        operands = list(operands) + exchange.operands()
    outs = pl.pallas_call(body, name=name, grid=grid, in_specs=in_specs, out_specs=out_specs, out_shape=out_shape,
                          scratch_shapes=scratch, input_output_aliases=aliases,
                          compiler_params=_cparams(*["arbitrary"] * len(grid)))(*operands)
    return outs[:n_out], outs[n_out:]


def _exchange_call(exchange, name):
    in_specs, out_specs, out_shape = exchange.specs()

    def body(*refs):
        _, parts = exchange.split(refs, 0, 0, 0)
        exchange.start(*parts)
        exchange.wait(*parts)

    return pl.pallas_call(body, name=name, in_specs=in_specs, out_specs=out_specs, out_shape=out_shape,
                          scratch_shapes=exchange.scratch, input_output_aliases=exchange.aliases(0, 0))(
        *exchange.operands())


def _gather_exchange(shards, fulls, axes, layers):
    sizes = [s.shape[a] for s, a in zip(shards, axes)]

    def copies(srcs, dsts, chip):
        out = []
        for src, full, axis, size, (l0, l1) in zip(srcs, dsts, axes, sizes, layers):
            part = src.at[l0:l1]
            cut = pl.ds(chip * size, size)
            dst = full.at[l0:l1, cut, :] if axis == 1 else full.at[l0:l1, :, cut]
            out.append(((part, dst), [(part, dst)] * 3))
        return out

    return _Exchange(shards, fulls, copies)


def _scatter_exchange(grads, stacks, axes, layers):
    sizes = [g.shape[a - 1] // 4 for g, a in zip(grads, axes)]

    def copies(srcs, dsts, chip):
        out = []
        for g, stack, axis, size, layer in zip(srcs, dsts, axes, sizes, layers):
            def cut(j, g=g, axis=axis, size=size):
                return g.at[pl.ds(j * size, size), :] if axis == 1 else g.at[:, pl.ds(j * size, size)]

            remote = [(cut(chip ^ (2 * fx + fy)), stack.at[r, layer]) for r, (fx, fy) in enumerate(CHIP_FLIPS)]
            out.append(((cut(chip), stack.at[3, layer]), remote))
        return out

    return _Exchange(grads, stacks, copies)


def _swap_with_sibling(parts, name):
    n = len(parts)

    def body(*refs):
        ins, outs = refs[:n], refs[n:2 * n]
        send, recv = refs[2 * n:]
        peer = (lax.axis_index("x"), lax.axis_index("y"), 1 - lax.axis_index("c"))
        copies = [pltpu.make_async_remote_copy(ins[t], outs[t], send.at[t], recv.at[t], device_id=peer,
                                               device_id_type=MESH) for t in range(n)]
        for cp in copies:
            cp.start()
        for cp in copies:
            cp.wait()

    return pl.pallas_call(
        body, name=name, in_specs=[ANY] * n, out_specs=[ANY] * n,
        out_shape=[jax.ShapeDtypeStruct(p.shape, p.dtype) for p in parts],
        scratch_shapes=[pltpu.SemaphoreType.DMA((n,)), pltpu.SemaphoreType.DMA((n,))])(*parts)


def _allreduce_small(v, name):
    R = v.shape[0]
    ND = 8

    def body(v_ref, o_ref, buf, send, recv):
        x, y, c = lax.axis_index("x"), lax.axis_index("y"), lax.axis_index("c")
        me = 4 * x + 2 * y + c
        copies = []
        for d in range(1, ND):
            peer = (_flip(x, d >> 2 & 1), _flip(y, d >> 1 & 1), _flip(c, d & 1))
            copies.append(pltpu.make_async_remote_copy(v_ref, buf.at[me], send.at[d], recv.at[me], device_id=peer,
                                                       device_id_type=MESH))
        for cp in copies:
            cp.start()
        buf[pl.ds(me, 1)] = v_ref[...][None]
        for k in range(ND):
            @pl.when(me != k)
            def _():
                pltpu.make_async_remote_copy(v_ref, buf.at[k], send.at[0], recv.at[k], device_id=(x, y, c),
                                             device_id_type=MESH).wait_recv()
        for cp in copies:
            cp.wait_send()
        total = buf[0]
        for k in range(1, ND):
            total = total + buf[k]
        o_ref[...] = total

    vm = pl.BlockSpec(memory_space=pltpu.VMEM)
    return pl.pallas_call(
        body, name=name, in_specs=[vm], out_specs=vm, out_shape=jax.ShapeDtypeStruct((R, LANES), F32),
        scratch_shapes=[pltpu.VMEM((ND, R, LANES), F32), pltpu.SemaphoreType.DMA((ND,)),
                        pltpu.SemaphoreType.DMA((ND,))],
        compiler_params=pltpu.CompilerParams(vmem_limit_bytes=VMEM_LIMIT))(v)


def _tile2(R, Cn):
    tc = _pick(Cn, 2048)
    tr = R
    for cand in (256, 128, 64, 32, 16, 8):
        if R % cand == 0:
            tr = cand
            break
    return tr, tc


def _sum4(stack, name):
    _, R, Cn = stack.shape
    tr, tc = _tile2(R, Cn)

    def body(s_ref, o_ref):
        o_ref[...] = ((s_ref[0].astype(F32) + s_ref[1].astype(F32)) + s_ref[2].astype(F32)) + s_ref[3].astype(F32)

    return pl.pallas_call(
        body, name=name, grid=(R // tr, Cn // tc), in_specs=[pl.BlockSpec((4, tr, tc), lambda i, j: (0, i, j))],
        out_specs=pl.BlockSpec((tr, tc), lambda i, j: (i, j)), out_shape=jax.ShapeDtypeStruct((R, Cn), F32),
        compiler_params=_cparams("parallel", "parallel"))(stack)


def _adamw(w, m, v, g_parts, name):
    R, Cn = w.shape
    tr, tc = _tile2(R, Cn)
    npart = len(g_parts)
    c1 = 1.0 / (1.0 - ADAM_B1 ** ADAM_STEP)
    c2 = 1.0 / (1.0 - ADAM_B2 ** ADAM_STEP)

    def body(*refs):
        w_ref, m_ref, v_ref = refs[:3]
        g_refs = refs[3:3 + npart]
        g_out, d_out, m_out, v_out = refs[3 + npart:]
        g = g_refs[0][...]
        for r in g_refs[1:]:
            g = g + r[...]
        mn = ADAM_B1 * m_ref[...] + (1.0 - ADAM_B1) * g
        vn = ADAM_B2 * v_ref[...] + (1.0 - ADAM_B2) * (g * g)
        g_out[...] = g
        m_out[...] = mn
        v_out[...] = vn
        d_out[...] = -ADAM_LR * ((mn * c1) / (jnp.sqrt(vn * c2) + ADAM_EPS) + ADAM_WD * w_ref[...])

    spec = pl.BlockSpec((tr, tc), lambda i, j: (i, j))
    return pl.pallas_call(
        body, name=name, grid=(R // tr, Cn // tc), in_specs=[spec] * (3 + npart), out_specs=[spec] * 4,
        out_shape=[jax.ShapeDtypeStruct((R, Cn), F32)] * 4,
        compiler_params=_cparams("parallel", "parallel"))(w, m, v, *g_parts)


SHARDED = (("ab_w_in", 2), ("ab_w_out", 1), ("c_w_in", 2), ("c_w_out", 1), ("ffn_up", 2), ("ffn_conv", 2),
           ("ffn_down", 1), ("ple_gate", 1), ("ple_proj", 2))
SMALL = ("mix_norm", "hg_lb_logits", "hg_out_norm", "q_norm", "k_norm", "sinks", "rel_bias", "ffn_norm",
         "ffn_conv_b", "ple_norm")
WEIGHTS = ("mix_norm", "ab_w_in", "hg_lb_logits", "hg_out_norm", "ab_w_out", "c_w_in", "q_norm", "k_norm", "sinks",
           "rel_bias", "c_w_out", "ffn_norm", "ffn_up", "ffn_conv", "ffn_conv_b", "ffn_down", "ple_norm", "ple_gate",
           "ple_proj")
PACK_ALIGN = 8 * LANES


def _pack(arrs):
    pieces = []
    for a in arrs:
        flat = a.reshape(-1)
        pad = -flat.shape[0] % PACK_ALIGN
        pieces.append(jnp.pad(flat, (0, pad)).reshape(-1, LANES))
    return jnp.concatenate(pieces, axis=0)


def _unpack(packed, like):
    out, r = [], 0
    for a in like:
        size = int(np.prod(a.shape))
        rows = (size + PACK_ALIGN - 1) // PACK_ALIGN * 8
        out.append(packed[r:r + rows].reshape(-1)[:size].reshape(a.shape))
        r += rows
    return out


def _family_range(name, lo, hi):
    if name.startswith("ab_"):
        idx = [i // 2 for i in range(lo, hi) if i % 2 == 0]
    elif name.startswith("c_"):
        idx = [i // 2 for i in range(lo, hi) if i % 2 == 1]
    else:
        idx = list(range(lo, hi))
    return (idx[0], idx[-1] + 1) if idx else None


W_IN = ("ab_w_in", "c_w_in")
REST = tuple(k for k, _ in SHARDED if k not in W_IN)
MIXER = ("ab_w_in", "ab_w_out", "c_w_in", "c_w_out")
FFN = ("ffn_up", "ffn_conv", "ffn_down")
PLE = ("ple_gate", "ple_proj")
GATHER_PLAN = {
    "gather_first": [(0, W_IN)],
    "sb_fwd_0": [(0, REST), (1, ("c_w_in",))],
    "hgrn_fwd_0": [(1, ("ffn_up", "ffn_conv"))],
    "ffn_up_0": [(1, ("ffn_down", "c_w_out"))],
    "ffn_down_0": [(1, PLE)],
    "swa_fwd_1": [(2, ("ab_w_in",))],
    "ffn_up_1": [(2, ("ab_w_out", "ffn_down"))],
    "ffn_down_1": [(2, PLE)],
    "sb_fwd_2": [(2, ("ffn_up", "ffn_conv")), (3, ("c_w_in", "c_w_out", "ffn_up", "ffn_conv"))],
    "hgrn_fwd_2": [(3, ("ffn_down",))],
    "ffn_up_2": [(3, PLE)],
}
SCATTER_PLAN = {
    "d_ffn_up_3": [(3, ("ffn_down", "ffn_conv"))],
    "d_hn2_3": [(3, PLE)],
    "swa_bwd_3": [(3, ("ffn_up", "c_w_out"))],
    "sb_bwd_2": [(3, W_IN), (2, REST)],
    "convglu_bwd_1": [(2, W_IN)],
    "d_ffn_up_1": [(1, ("ffn_down", "ffn_conv"))],
    "d_hn2_1": [(1, PLE)],
    "swa_bwd_1": [(1, ("ffn_up", "c_w_out"))],
    "sb_bwd_0": [(1, W_IN), (0, REST)],
    "d_hn_0": [(0, W_IN)],
}


class _StepExchanges:
    def __init__(self, shards, W):
        self.shards, self.W = shards, W
        self.stacks = {}
        for k, axis in SHARDED:
            shp = shards[k].shape
            self.W[k] = lax.empty(tuple(4 * d if i == axis else d for i, d in enumerate(shp)), shards[k].dtype)
            self.stacks[k] = lax.empty((4,) + shp, shards[k].dtype)

    @staticmethod
    def _select(plan):
        idx = {}
        for layer, fams in plan:
            for k, _ in SHARDED:
                r = _family_range(k, layer, layer + 1)
                if r is not None and (fams is None or k in fams):
                    idx.setdefault(k, []).append(r[0])
        return [(k, axis, sorted(idx[k])) for k, axis in SHARDED if k in idx]

    def gather(self, call):
        sel = self._select(GATHER_PLAN.get(call, ()))
        if not sel:
            return None, None
        for _, _, ii in sel:
            assert ii == list(range(ii[0], ii[-1] + 1)), "one copy per family takes a contiguous layer range"
        names = [k for k, _, _ in sel]
        ex = _gather_exchange([self.shards[k] for k in names], [self.W[k] for k in names],
                              [a for _, a, _ in sel], [(ii[0], ii[-1] + 1) for _, _, ii in sel])
        return ex, (self.W, names)

    def scatter(self, call, G):
        sel = self._select(SCATTER_PLAN.get(call, ()))
        if not sel:
            return None, None
        assert all(len(ii) == 1 for _, _, ii in sel)
        names = [k for k, _, _ in sel]
        ex = _scatter_exchange([G[k][ii[0]] for k, _, ii in sel], [self.stacks[k] for k in names],
                               [a for _, a, _ in sel], [ii[0] for _, _, ii in sel])
        return ex, (self.stacks, names)

    @staticmethod
    def adopt(where, targets):
        if where is not None:
            book, names = where
            for k, t in zip(names, targets):
                book[k] = t


def _forward_backward(x, p, target, W, S, exchanges=None):
    T = x.shape[0]
    depth = p.shape[0]
    lb = _lower_bound_fwd(W["hg_lb_logits"], "lower_bound_fwd")
    bias = _bias_build(W["rel_bias"], "bias_build")
    qk_gain = jnp.concatenate([jnp.tile(W["q_norm"], (1, SW_HEADS)), jnp.tile(W["k_norm"], (1, SW_KV))], axis=1)

    def gathering(call):
        return exchanges.gather(call) if exchanges else (None, None)

    def scattering(call):
        return exchanges.scatter(call, G) if exchanges else (None, None)

    def matmul(a, b, mode, name, **kw):
        ex, where = (gathering(name) if name in GATHER_PLAN else
                     scattering(name) if name in SCATTER_PLAN else (None, None))
        out, moved = _matmul(a, b, mode, name, exchange=ex, **kw)
        _StepExchanges.adopt(where, moved)
        return out

    def mm(a, wname, layer, mode, name, **kw):
        return matmul(a, W[wname], mode, name, b_layer=layer, **kw)

    saved = []
    h = x
    for i in range(depth):
        j = i // 2
        s = {"h0": h}
        s["hn"] = _rmsnorm_fwd(h, W["mix_norm"][i:i + 1], f"mix_norm_fwd_{i}") if i == 0 else hn_next
        if i % 2 == 0:
            s["proj"] = mm(s["hn"], "ab_w_in", j, "nn", f"ab_in_{i}")
            ex, where = gathering(f"sb_fwd_{i}")
            cat, s["sb_tot"], arrived = _sb_fwd(s["proj"], S, f"sb_fwd_{i}", ex)
            _StepExchanges.adopt(where, arrived)
            ex, where = gathering(f"hgrn_fwd_{i}")
            s["cat"], s["oraw"], s["states"], arrived = _hgrn_fwd(s["proj"], cat, lb[j:j + 1],
                                                                  W["hg_out_norm"][j:j + 1], S, f"hgrn_fwd_{i}", ex)
            _StepExchanges.adopt(where, arrived)
            h, s["hn2"] = mm(s["cat"], "ab_w_out", j, "nn", f"ab_out_{i}", res=h, norm_gain=W["ffn_norm"][i:i + 1])
        else:
            s["proj"] = mm(s["hn"], "c_w_in", j, "nn", f"c_in_{i}")
            s["qkn"] = _headnorm_fwd(s["proj"], qk_gain[j:j + 1], f"qk_norm_fwd_{i}")
            ex, where = gathering(f"swa_fwd_{i}")
            s["o"], arrived = _swa_fwd(s["qkn"], s["proj"], bias, W["sinks"][j], S, f"swa_fwd_{i}", ex)
            _StepExchanges.adopt(where, arrived)
            h, s["hn2"] = mm(s["o"], "c_w_out", j, "nn", f"c_out_{i}", res=h, norm_gain=W["ffn_norm"][i:i + 1])
        s["h1"] = h
        s["u"] = mm(s["hn2"], "ffn_up", i, "nn", f"ffn_up_{i}", tiles=(min(T, 1024), D_FF // 2, D_MODEL))
        s["a"] = _convglu_fwd(s["u"], W["ffn_conv"][i], W["ffn_conv_b"][i:i + 1], S, f"convglu_fwd_{i}")
        h, s["hn3"] = mm(s["a"], "ffn_down", i, "nn", f"ffn_down_{i}", res=h, norm_gain=W["ple_norm"][i:i + 1])
        s["h2"] = h
        s["z"] = mm(s["hn3"], "ple_gate", i, "nn", f"ple_gate_{i}")
        s["pp"] = mm(p, "ple_proj", i, "nn", f"ple_proj_{i}", a_layer=i)
        h, hn_next = _ple_fwd(h, s["z"], s["pp"], W["mix_norm"][i + 1:i + 2] if i + 1 < depth else None,
                              f"ple_fwd_{i}")
        saved.append(s)

    loss, dh = _loss_fwd_bwd(h, target, "loss")

    G = {k: [None] * depth for k in ("mix_norm", "ffn_norm", "ple_norm", "ffn_up", "ffn_conv", "ffn_conv_b",
                                     "ffn_down", "ple_gate", "ple_proj")}
    for k in ("ab_w_in", "ab_w_out", "c_w_in", "c_w_out", "hg_out_norm", "q_norm", "k_norm", "sinks", "lb"):
        G[k] = [None] * (depth // 2)
    dbias_total = None
    for i in reversed(range(depth)):
        j = i // 2
        s = saved[i]
        dz, dpp = _ple_bwd(dh, s["z"], s["pp"], f"ple_bwd_{i}")
        G["ple_proj"][i] = matmul(p, dpp, "tn", f"d_ple_proj_{i}", out_dtype=BF16, a_layer=i)
        G["ple_gate"][i] = matmul(s["hn3"], dz, "tn", f"d_ple_gate_{i}", out_dtype=BF16)
        dhn = mm(dz, "ple_gate", i, "nt", f"d_hn3_{i}")
        dh, G["ple_norm"][i] = _rmsnorm_bwd(s["h2"], W["ple_norm"][i:i + 1], dhn, dh, f"ple_norm_bwd_{i}")

        half_ff = D_FF // 2
        da = mm(dh, "ffn_down", i, "nt", f"d_a_{i}", tiles=(min(T, 1024), half_ff, D_MODEL))
        G["ffn_down"][i] = matmul(s["a"], dh, "tn", f"d_ffn_down_{i}", out_dtype=BF16,
                                   tiles=(half_ff, 512, min(T, 2048)))
        ex, where = scattering(f"convglu_bwd_{i}")
        du, dcw, dcb, sent = _convglu_bwd(s["u"], da, W["ffn_conv"][i], W["ffn_conv_b"][i:i + 1], S,
                                          f"convglu_bwd_{i}", ex)
        _StepExchanges.adopt(where, sent)
        G["ffn_conv"][i] = jnp.swapaxes(dcw, 0, 1).reshape(3, 2 * D_FF)
        G["ffn_conv_b"][i] = dcb.reshape(1, 2 * D_FF)
        G["ffn_up"][i] = matmul(s["hn2"], du, "tn", f"d_ffn_up_{i}", out_dtype=BF16,
                                 tiles=(D_MODEL, half_ff, min(T, 1024)))
        dhn = mm(du, "ffn_up", i, "nt", f"d_hn2_{i}")
        dh, G["ffn_norm"][i] = _rmsnorm_bwd(s["h1"], W["ffn_norm"][i:i + 1], dhn, dh, f"ffn_norm_bwd_{i}")

        if i % 2 == 0:
            dcat = mm(dh, "ab_w_out", j, "nt", f"d_cat_{i}")
            G["ab_w_out"][j] = matmul(s["cat"], dh, "tn", f"d_ab_out_{i}", out_dtype=BF16)
            ex, where = scattering(f"sb_bwd_{i}")
            dproj, sent = _sb_bwd(s["proj"], s["sb_tot"], dcat, S, f"sb_bwd_{i}", ex)
            _StepExchanges.adopt(where, sent)
            dproj, G["lb"][j], G["hg_out_norm"][j] = _hgrn_bwd(s["proj"], s["oraw"], dcat, s["states"], lb[j:j + 1],
                                                               W["hg_out_norm"][j:j + 1], dproj, S, f"hgrn_bwd_{i}")
            G["ab_w_in"][j] = matmul(s["hn"], dproj, "tn", f"d_ab_in_{i}", out_dtype=BF16)
            dhn = mm(dproj, "ab_w_in", j, "nt", f"d_hn_{i}")
        else:
            do = mm(dh, "c_w_out", j, "nt", f"d_o_{i}")
            G["c_w_out"][j] = matmul(s["o"], dh, "tn", f"d_c_out_{i}", out_dtype=BF16)
            ex, where = scattering(f"swa_bwd_{i}")
            dqkv, dbias, dsink, sent = _swa_bwd(s["qkn"], s["proj"], bias, W["sinks"][j], do, S, f"swa_bwd_{i}", ex)
            _StepExchanges.adopt(where, sent)
            dbias_total = dbias if dbias_total is None else dbias_total + dbias
            G["sinks"][j] = dsink[:, 0]
            dproj, dgain = _headnorm_bwd(s["proj"], qk_gain[j:j + 1], dqkv, f"qk_norm_bwd_{i}")
            G["q_norm"][j] = dgain[0, :SW_Q_COLS].reshape(SW_HEADS, SW_DIM).sum(axis=0)
            G["k_norm"][j] = dgain[0, SW_Q_COLS:].reshape(SW_KV, SW_DIM).sum(axis=0)
            G["c_w_in"][j] = matmul(s["hn"], dproj, "tn", f"d_c_in_{i}", out_dtype=BF16)
            dhn = mm(dproj, "c_w_in", j, "nt", f"d_hn_{i}")
        dh, G["mix_norm"][i] = _rmsnorm_bwd(s["h0"], W["mix_norm"][i:i + 1], dhn, dh, f"mix_norm_bwd_{i}")

    grads = {k: G[k] for k, _ in SHARDED}
    for k in ("q_norm", "k_norm", "sinks"):
        grads[k] = jnp.stack(G[k])
    for k in ("mix_norm", "ffn_norm", "ple_norm", "ffn_conv_b", "hg_out_norm"):
        grads[k] = jnp.concatenate(G[k], axis=0)
    grads["hg_lb_logits"] = _lower_bound_bwd(W["hg_lb_logits"], jnp.concatenate(G["lb"], axis=0), "lower_bound_bwd")
    grads["rel_bias"] = _bias_reduce(dbias_total, "bias_reduce")[:, 0, :N_BUCKETS].T
    return loss, dh, grads


def kernel(x, p, mix_norm, ab_w_in, hg_lb_logits, hg_out_norm, ab_w_out, c_w_in, q_norm, k_norm, sinks, rel_bias, c_w_out, ffn_norm, ffn_up, ffn_conv, ffn_conv_b, ffn_down, ple_norm, ple_gate, ple_proj, loss_target, m_mix_norm, m_ab_w_in, m_hg_lb_logits, m_hg_out_norm, m_ab_w_out, m_c_w_in, m_q_norm, m_k_norm, m_sinks, m_rel_bias, m_c_w_out, m_ffn_norm, m_ffn_up, m_ffn_conv, m_ffn_conv_b, m_ffn_down, m_ple_norm, m_ple_gate, m_ple_proj, v_mix_norm, v_ab_w_in, v_hg_lb_logits, v_hg_out_norm, v_ab_w_out, v_c_w_in, v_q_norm, v_k_norm, v_sinks, v_rel_bias, v_c_w_out, v_ffn_norm, v_ffn_up, v_ffn_conv, v_ffn_conv_b, v_ffn_down, v_ple_norm, v_ple_gate, v_ple_proj):
    args = dict(locals())
    w = {k: args[k] for k in WEIGHTS}
    m = {k: args["m_" + k] for k in WEIGHTS}
    v = {k: args["v_" + k] for k in WEIGHTS}
    B, S, Dm = x.shape
    T = B * S
    names = [k for k, _ in SHARDED]

    W = {k: w[k] for k in SMALL}
    exchanges = _StepExchanges({k: w[k].astype(F32 if k == "ffn_conv" else BF16) for k in names}, W)
    first, where = exchanges.gather("gather_first")
    exchanges.adopt(where, _exchange_call(first, "gather_first"))

    loss, dx, grads = _forward_backward(x.reshape(T, Dm), p.reshape(p.shape[0], T, p.shape[-1]),
                                        loss_target.reshape(T, Dm), W, S, exchanges)
    loss = lax.psum(loss[0, 0], ("x", "y", "c"))

    stacks = [exchanges.stacks[k] for k in names]
    partial = [_sum4(st.reshape(4, -1, st.shape[-1]), f"sum_chips_{k}") for k, st in zip(names, stacks)]
    other = _swap_with_sibling(partial, "swap_core_sums")
    small_sum = _allreduce_small(_pack([grads[k] for k in SMALL]), "allreduce_small")

    out_g, out_d, out_m, out_v = {}, {}, {}, {}
    for k, mine, theirs in zip(names, partial, other):
        shp = w[k].shape
        r = [a.reshape(shp) for a in _adamw(w[k].reshape(mine.shape), m[k].reshape(mine.shape),
                                            v[k].reshape(mine.shape), [mine, theirs], f"adamw_{k}")]
        out_g[k], out_d[k], out_m[k], out_v[k] = r
    sm = _adamw(_pack([w[k] for k in SMALL]), _pack([m[k] for k in SMALL]), _pack([v[k] for k in SMALL]),
                [small_sum], "adamw_small")
    like = [w[k] for k in SMALL]
    for dst, packed in zip((out_g, out_d, out_m, out_v), sm):
        for k, a in zip(SMALL, _unpack(packed, like)):
            dst[k] = a

    return (loss, dx.reshape(B, S, Dm), *[out_g[k] for k in WEIGHTS], *[out_d[k] for k in WEIGHTS],
            *[out_m[k] for k in WEIGHTS], *[out_v[k] for k in WEIGHTS])
```

```python
import math

import numpy as np
import jax
import jax.numpy as jnp
from jax import lax
from jax.experimental import pallas as pl
from jax.experimental.pallas import tpu as pltpu

F32 = jnp.float32
BF16 = jnp.bfloat16
MESH = pl.DeviceIdType.MESH
ANY = pl.BlockSpec(memory_space=pl.ANY)

D_MODEL = 1024
EPS = 1e-6
SB_HEADS, SB_DIM = 8, 64
HG_HEADS, HG_DK = 4, 128
HG_CHUNK = 32
HG_STEP = 2
HG_TRIP = 4
SW_HEADS, SW_KV, SW_DIM, WINDOW = 16, 4, 64, 128
N_BUCKETS, MAX_DISTANCE = 32, 128
D_FF = 2816
ATT_BLOCK = 128
SB_QBLOCK = 256
LANES = 128
NEG = -1e30

ADAM_LR, ADAM_B1, ADAM_B2, ADAM_EPS, ADAM_WD, ADAM_STEP = 0.001, 0.9, 0.999, 1e-08, 0.01, 10

VMEM_LIMIT = 56 * 1024 * 1024


def _cparams(*sem):
    return pltpu.CompilerParams(dimension_semantics=sem, vmem_limit_bytes=VMEM_LIMIT)


def _pick(n, cap):
    if n <= cap:
        return n
    best = None
    for d in range(LANES, cap + 1, LANES):
        if n % d == 0:
            best = d
    assert best is not None, (n, cap)
    return best


def _dot(a, b, ca, cb):
    return lax.dot_general(a.astype(BF16), b.astype(BF16), (((ca,), (cb,)), ((), ())),
                           preferred_element_type=F32)


def _split(x, terms):
    parts = []
    for _ in range(terms):
        hi = x.astype(BF16)
        parts.append(hi)
        x = x - hi.astype(F32)
    return parts


def _dot_exact_l(x, m, terms=2):
    out = None
    for p in _split(x, terms):
        t = lax.dot_general(p, m, (((1,), (0,)), ((), ())), preferred_element_type=F32)
        out = t if out is None else out + t
    return out


def _dot_exact_r(m, x, terms=3, cm=1):
    out = None
    for p in _split(x, terms):
        t = lax.dot_general(m, p, (((cm,), (0,)), ((), ())), preferred_element_type=F32)
        out = t if out is None else out + t
    return out


def _sig(x):
    return 1.0 / (1.0 + jnp.exp(-x))


def _iota2(shape, dim):
    return lax.broadcasted_iota(jnp.int32, shape, dim)


def _operand_spec(arr, layer, blk, index):
    if arr.ndim == 2:
        return pl.BlockSpec(blk, index)
    if layer is not None:
        return pl.BlockSpec((None,) + blk, lambda i, j, k: (layer,) + index(i, j, k))
    per_half = arr.shape[2] // blk[1]

    def halves(i, j, k):
        r, c = index(i, j, k)
        return (c // per_half, r, c % per_half)

    return pl.BlockSpec((None,) + blk, halves)


def _matmul(a, b, mode, name, out_dtype=F32, res=None, a_layer=None, b_layer=None, tiles=None, exchange=None,
            norm_gain=None, norm_bwd=None):
    def dims(arr, layer):
        if arr.ndim == 2:
            return arr.shape
        return arr.shape[1:] if layer is not None else (arr.shape[1], 2 * arr.shape[2])

    (a0, a1), (b0, b1) = dims(a, a_layer), dims(b, b_layer)
    if mode == "nn":
        M, K, N = a0, a1, b1
    elif mode == "nt":
        M, K, N = a0, a1, b0
    else:
        K, M, N = a0, a1, b1
    cap_m, cap_n, cap_k = 1024, 1024, (1024 if mode == "tn" else 2048)
    tm, tn, tk = _pick(M, cap_m), _pick(N, cap_n), _pick(K, cap_k)
    if a.ndim == 3 and a_layer is None:
        if mode == "tn":
            tm = _pick(a.shape[2], cap_m)
        else:
            tk = _pick(a.shape[2], cap_k)
    if b.ndim == 3 and b_layer is None:
        if mode == "nt":
            tk = _pick(b.shape[2], cap_k)
        else:
            tn = _pick(b.shape[2], cap_n)
    if tiles is not None:
        tm, tn, tk = tiles
    assert M % tm == 0 and N % tn == 0 and K % tk == 0, (name, M, N, K, tm, tn, tk)
    nk = K // tk
    if mode == "tn":
        a_spec = _operand_spec(a, a_layer, (tk, tm), lambda i, j, k: (k, i))
    else:
        a_spec = _operand_spec(a, a_layer, (tm, tk), lambda i, j, k: (i, k))
    if mode == "nt":
        b_spec = _operand_spec(b, b_layer, (tn, tk), lambda i, j, k: (j, k))
    else:
        b_spec = _operand_spec(b, b_layer, (tk, tn), lambda i, j, k: (k, j))
    ca, cb = {"nn": (1, 0), "nt": (1, 1), "tn": (0, 0)}[mode]
    o_spec = pl.BlockSpec((tm, tn), lambda i, j, k: (i, j))

    grid = (M // tm, N // tn, nk)
    normed = norm_gain is not None
    through = norm_bwd is not None
    assert not (normed or through) or tn == N, "the norm needs whole rows in one tile"
    assert not (through and (normed or res is not None))
    v_spec = pl.BlockSpec((1, tn), lambda i, j, k: (0, j))
    if through:
        x_in, gain_in, dres_in = norm_bwd
        extra_in, extra_specs = [x_in, gain_in, dres_in], [o_spec, v_spec, o_spec]
        out_specs = [o_spec, v_spec]
        out_shape = [jax.ShapeDtypeStruct((M, N), F32), jax.ShapeDtypeStruct((1, N), F32)]
    else:
        extra_in = ([] if res is None else [res]) + ([norm_gain] if normed else [])
        extra_specs = ([] if res is None else [o_spec]) + ([v_spec] if normed else [])
        out_specs = [o_spec] * (1 + normed)
        out_shape = ([jax.ShapeDtypeStruct((M, N), out_dtype)]
                     + ([jax.ShapeDtypeStruct((M, N), BF16)] if normed else []))
    n_in, n_out = 2 + len(extra_in), len(out_shape)

    def body(*refs):
        own, carried = _carried(exchange, refs, n_in, n_out, 1)
        _start_carried(exchange, carried, grid)
        a_ref, b_ref = own[:2]
        extras, outs, acc = own[2:n_in], own[n_in:n_in + n_out], own[-1]
        i, k = pl.program_id(0), pl.program_id(2)

        @pl.when(k == 0)
        def _():
            acc[...] = jnp.zeros_like(acc)

        acc[...] += _dot(a_ref[...], b_ref[...], ca, cb)

        @pl.when(k == nk - 1)
        def _():
            r = acc[...]
            if through:
                x, dy = extras[0][...], r
                scale = lax.rsqrt(jnp.mean(x * x, axis=1, keepdims=True) + EPS)
                gdy = dy * extras[1][...]
                m = jnp.mean(x * gdy, axis=1, keepdims=True)
                outs[0][...] = extras[2][...] + scale * gdy - x * (scale * scale * scale * m)
                part = jnp.sum(dy * x * scale, axis=0, keepdims=True)

                @pl.when(i == 0)
                def _():
                    outs[1][...] = part

                @pl.when(i > 0)
                def _():
                    outs[1][...] += part
            else:
                if res is not None:
                    r = r + extras[0][...]
                outs[0][...] = r.astype(out_dtype)
                if normed:
                    scale = lax.rsqrt(jnp.mean(r * r, axis=1, keepdims=True) + EPS)
                    outs[1][...] = (r * scale * extras[-1][...]).astype(BF16)

        _wait_carried(exchange, carried, grid)

    ins = [a, b] + extra_in
    in_specs = [a_spec, b_spec] + extra_specs
    scratch = [pltpu.VMEM((tm, tn), F32)]
    if exchange is None:
        sem = ("arbitrary" if through else "parallel", "parallel", "arbitrary")
        outs = pl.pallas_call(
            body, name=name, grid=grid, in_specs=in_specs, out_specs=out_specs, out_shape=out_shape,
            scratch_shapes=scratch, compiler_params=_cparams(*sem))(*ins)
        targets = ()
    else:
        outs, targets = _carrier_call(body, name, grid, in_specs, out_specs, out_shape, scratch, {}, ins, exchange)
    return (tuple(outs) if n_out > 1 else outs[0]), targets


ROW_TILE = 512
HEAD_ROWS = 2048


def _row_spec(width):
    return pl.BlockSpec((ROW_TILE, width), lambda i: (i, 0))


def _vec_spec(width):
    return pl.BlockSpec((1, width), lambda i: (0, 0))


def _rmsnorm_fwd(h, g, name):
    T, Dm = h.shape

    def body(h_ref, g_ref, o_ref):
        x = h_ref[...]
        r = lax.rsqrt(jnp.mean(x * x, axis=1, keepdims=True) + EPS)
        o_ref[...] = (x * r * g_ref[...]).astype(BF16)

    return pl.pallas_call(
        body, name=name, grid=(T // ROW_TILE,), in_specs=[_row_spec(Dm), _vec_spec(Dm)],
        out_specs=_row_spec(Dm), out_shape=jax.ShapeDtypeStruct((T, Dm), BF16),
        compiler_params=_cparams("parallel"))(h, g)


def _rmsnorm_bwd(h, g, dhn, dres, name):
    T, Dm = h.shape

    def body(h_ref, g_ref, dy_ref, dr_ref, dh_ref, dg_ref):
        i = pl.program_id(0)
        x = h_ref[...]
        dy = dy_ref[...]
        r = lax.rsqrt(jnp.mean(x * x, axis=1, keepdims=True) + EPS)
        gdy = dy * g_ref[...]
        m = jnp.mean(x * gdy, axis=1, keepdims=True)
        dh_ref[...] = dr_ref[...] + r * gdy - x * (r * r * r * m)
        part = jnp.sum(dy * x * r, axis=0, keepdims=True)

        @pl.when(i == 0)
        def _():
            dg_ref[...] = part

        @pl.when(i > 0)
        def _():
            dg_ref[...] += part

    return pl.pallas_call(
        body, name=name, grid=(T // ROW_TILE,),
        in_specs=[_row_spec(Dm), _vec_spec(Dm), _row_spec(Dm), _row_spec(Dm)],
        out_specs=[_row_spec(Dm), _vec_spec(Dm)],
        out_shape=[jax.ShapeDtypeStruct((T, Dm), F32), jax.ShapeDtypeStruct((1, Dm), F32)],
        compiler_params=_cparams("arbitrary"))(h, g, dhn, dres)


def _ple_fwd(h, z, pp, gain, name):
    T, Dm = h.shape
    normed = gain is not None

    def body(*refs):
        h_ref, z_ref, p_ref = refs[:3]
        y = h_ref[...] + _sig(z_ref[...]) * p_ref[...]
        refs[3 + normed][...] = y
        if normed:
            r = lax.rsqrt(jnp.mean(y * y, axis=1, keepdims=True) + EPS)
            refs[5][...] = (y * r * refs[3][...]).astype(BF16)

    outs = pl.pallas_call(
        body, name=name, grid=(T // ROW_TILE,), in_specs=[_row_spec(Dm)] * 3 + [_vec_spec(Dm)] * normed,
        out_specs=[_row_spec(Dm)] * (1 + normed),
        out_shape=[jax.ShapeDtypeStruct((T, Dm), F32)] + [jax.ShapeDtypeStruct((T, Dm), BF16)] * normed,
        compiler_params=_cparams("parallel"))(*([h, z, pp] + [gain] * normed))
    return outs if normed else (outs[0], None)


def _ple_bwd(dh, z, pp, name):
    T, Dm = dh.shape

    def body(dh_ref, z_ref, p_ref, dz_ref, dp_ref):
        s = _sig(z_ref[...])
        d = dh_ref[...]
        dz_ref[...] = d * p_ref[...] * s * (1.0 - s)
        dp_ref[...] = d * s

    return pl.pallas_call(
        body, name=name, grid=(T // ROW_TILE,), in_specs=[_row_spec(Dm)] * 3, out_specs=[_row_spec(Dm)] * 2,
        out_shape=[jax.ShapeDtypeStruct((T, Dm), F32)] * 2, compiler_params=_cparams("parallel"))(dh, z, pp)


def _loss_fwd_bwd(y, target, name):
    T, Dm = y.shape

    def body(y_ref, t_ref, l_ref, d_ref):
        i = pl.program_id(0)
        e = y_ref[...] - t_ref[...]
        d_ref[...] = e * (1.0 / Dm)
        part = jnp.full((8, LANES), 0.5 / Dm, F32) * jnp.sum(e * e)

        @pl.when(i == 0)
        def _():
            l_ref[...] = part

        @pl.when(i > 0)
        def _():
            l_ref[...] += part

    return pl.pallas_call(
        body, name=name, grid=(T // ROW_TILE,), in_specs=[_row_spec(Dm)] * 2,
        out_specs=[pl.BlockSpec((8, LANES), lambda i: (0, 0)), _row_spec(Dm)],
        out_shape=[jax.ShapeDtypeStruct((8, LANES), F32), jax.ShapeDtypeStruct((T, Dm), F32)],
        compiler_params=_cparams("arbitrary"))(y, target)


def _head_mean_matrix():
    r = _iota2((LANES, LANES), 0) >= SW_DIM
    c = _iota2((LANES, LANES), 1) >= SW_DIM
    return jnp.where(r == c, 1.0 / SW_DIM, 0.0).astype(BF16)


def _headnorm_fwd(x, g_lane, name):
    T = x.shape[0]
    C = g_lane.shape[1]

    def body(x_ref, g_ref, y_ref):
        xv = x_ref[...]
        ms = _dot_exact_l(xv * xv, _head_mean_matrix())
        y_ref[...] = xv * lax.rsqrt(ms + EPS) * g_ref[...]

    rows = min(T, HEAD_ROWS)
    spec = pl.BlockSpec((rows, LANES), lambda j, i: (i, j))
    return pl.pallas_call(
        body, name=name, grid=(C // LANES, T // rows),
        in_specs=[spec, pl.BlockSpec((1, LANES), lambda j, i: (0, j))], out_specs=spec,
        out_shape=jax.ShapeDtypeStruct((T, C), F32), compiler_params=_cparams("parallel", "parallel"))(x, g_lane)


def _headnorm_bwd(x, g_lane, dy_full, name):
    T = x.shape[0]
    C = g_lane.shape[1]

    def body(x_ref, g_ref, dy_ref, dx_ref, dg_ref):
        i = pl.program_id(1)
        xv = x_ref[...]
        dy = dy_ref[...]
        bd = _head_mean_matrix()
        r = lax.rsqrt(_dot_exact_l(xv * xv, bd) + EPS)
        gdy = dy * g_ref[...]
        m = _dot_exact_l(xv * gdy, bd)
        dx_ref[...] = r * gdy - xv * (r * r * r * m)
        part = jnp.sum(dy * xv * r, axis=0, keepdims=True)

        @pl.when(i == 0)
        def _():
            dg_ref[...] = part

        @pl.when(i > 0)
        def _():
            dg_ref[...] += part

    rows = min(T, HEAD_ROWS)
    spec = pl.BlockSpec((rows, LANES), lambda j, i: (i, j))
    vspec = pl.BlockSpec((1, LANES), lambda j, i: (0, j))
    return pl.pallas_call(
        body, name=name, grid=(C // LANES, T // rows), in_specs=[spec, vspec, spec],
        out_specs=[spec, vspec],
        out_shape=[jax.ShapeDtypeStruct(dy_full.shape, F32), jax.ShapeDtypeStruct((1, C), F32)],
        input_output_aliases={2: 0}, compiler_params=_cparams("parallel", "arbitrary"))(x, g_lane, dy_full)


CONV_TILE = 128


def _shift_down(x, k, rows):
    return jnp.where(rows >= k, pltpu.roll(x, k, 0), 0.0)


def _shift_up(x, k, rows):
    n = x.shape[0]
    return jnp.where(rows < n - k, pltpu.roll(x, n - k, 0), 0.0)


def _conv3(u, w_ref, b_ref, rows):
    return (w_ref[2:3, :] * u + w_ref[1:2, :] * _shift_down(u, 1, rows) + w_ref[0:1, :] * _shift_down(u, 2, rows)
            + b_ref[...])


def _convglu_fwd(u, cw, cb, S, name):
    T = u.shape[0]
    nf = D_FF // CONV_TILE

    def body(ug_ref, uu_ref, wg_ref, wu_ref, bg_ref, bu_ref, a_ref):
        rows = _iota2((S, CONV_TILE), 0)
        yg = _conv3(ug_ref[...], wg_ref, bg_ref, rows)
        yu = _conv3(uu_ref[...], wu_ref, bu_ref, rows)
        a_ref[...] = (yg * _sig(yg) * yu).astype(BF16)

    def blk(rows, off):
        return pl.BlockSpec((rows, CONV_TILE), (lambda b, j: (b, j + off)) if rows == S else (lambda b, j: (0, j + off)))

    return pl.pallas_call(
        body, name=name, grid=(T // S, nf),
        in_specs=[blk(S, 0), blk(S, nf), blk(3, 0), blk(3, nf), blk(1, 0), blk(1, nf)],
        out_specs=blk(S, 0), out_shape=jax.ShapeDtypeStruct((T, D_FF), BF16),
        compiler_params=_cparams("parallel", "parallel"))(u, u, cw, cw, cb, cb)


def _convglu_bwd(u, da, cw, cb, S, name, exchange=None):
    T = u.shape[0]
    nf = D_FF // CONV_TILE

    grid = (nf, T // S)

    def body(*refs):
        (ug_ref, uu_ref, da_ref, wg_ref, wu_ref, bg_ref, bu_ref, du_ref, dw_ref, db_ref), carried = _carried(
            exchange, refs, 7, 3, 0)
        _start_carried(exchange, carried, grid)
        b = pl.program_id(1)
        rows = _iota2((S, CONV_TILE), 0)
        ug, uu = ug_ref[...], uu_ref[...]
        yg = _conv3(ug, wg_ref, bg_ref, rows)
        yu = _conv3(uu, wu_ref, bu_ref, rows)
        s = _sig(yg)
        da_v = da_ref[...]
        for half, (uv, w_ref, dy) in enumerate(((ug, wg_ref, da_v * yu * (s * (1.0 + yg * (1.0 - s)))),
                                                (uu, wu_ref, da_v * yg * s))):
            up1, up2 = _shift_up(dy, 1, rows), _shift_up(dy, 2, rows)
            du_ref[half] = w_ref[2:3, :] * dy + w_ref[1:2, :] * up1 + w_ref[0:1, :] * up2
            dws = [jnp.sum(up2 * uv, axis=0, keepdims=True), jnp.sum(up1 * uv, axis=0, keepdims=True),
                   jnp.sum(dy * uv, axis=0, keepdims=True)]
            dbv = jnp.sum(dy, axis=0, keepdims=True)

            @pl.when(b == 0)
            def _():
                for k in range(3):
                    dw_ref[half, k:k + 1, :] = dws[k]
                db_ref[half] = dbv

            @pl.when(b > 0)
            def _():
                for k in range(3):
                    dw_ref[half, k:k + 1, :] += dws[k]
                db_ref[half] += dbv

        _wait_carried(exchange, carried, grid)

    def blk(rows, off):
        return pl.BlockSpec((rows, CONV_TILE), (lambda j, b: (b, j + off)) if rows == S else (lambda j, b: (0, j + off)))

    def both(rows):
        return pl.BlockSpec((2, rows, CONV_TILE), (lambda j, b: (0, b, j)) if rows == S else (lambda j, b: (0, 0, j)))

    outs, targets = _carrier_call(
        body, name, grid, [blk(S, 0), blk(S, nf), blk(S, 0), blk(3, 0), blk(3, nf), blk(1, 0), blk(1, nf)],
        [both(S), both(3), both(1)],
        [jax.ShapeDtypeStruct((2, T, D_FF), F32), jax.ShapeDtypeStruct((2, 3, D_FF), F32),
         jax.ShapeDtypeStruct((2, 1, D_FF), F32)], [], {}, (u, u, da, cw, cw, cb, cb), exchange)
    return (*outs, targets)


def _sb_scores(qb, kblk, on_diag_mask):
    z = _dot(qb, kblk, 1, 1) * (SB_DIM ** -0.5)
    l1 = jnp.log(1.0 + jnp.exp(-jnp.abs(z)))
    ls = jnp.minimum(z, 0.0) - l1
    lk = jnp.where(on_diag_mask, ls - z, 0.0)
    return ls, lk


def _sb_fwd(proj, S, name, exchange=None):
    T = proj.shape[0]
    BQ, BK = SB_QBLOCK, ATT_BLOCK
    unroll = BQ // BK
    nq = S // BQ
    nhp = SB_HEADS // 2
    heads = [slice(h * SB_DIM, (h + 1) * SB_DIM) for h in range(2)]

    grid = (T // S, nhp)

    def body(*refs):
        (q_ref, k_ref, v_ref, o_ref, tot_ref), carried = _carried(exchange, refs, 3, 2, 0)
        _start_carried(exchange, carried, grid)
        ahead = _iota2((BQ, BK), 1) - _iota2((BQ, BK), 0)
        upper = (_iota2((BK, BK), 0) > _iota2((BK, BK), 1)).astype(BF16)

        def qloop(iq, carry):
            q0 = pl.multiple_of(iq * BQ, BQ)
            rows = pl.ds(q0, BQ)
            qbs = [q_ref[rows, sl] for sl in heads]
            nkb = (iq + 1) * (BQ // BK)

            def kloop(jj, kc):
                blocks = []
                for u in range(unroll):
                    k0 = pl.multiple_of((nkb - 1 - unroll * jj - u) * BK, BK)
                    blocks.append((pl.ds(k0, BK), ahead < q0 - k0))
                units = [(h, krows, mask) for krows, mask in blocks for h in range(2)]
                scores = [_sb_scores(qbs[h], k_ref[krows, heads[h]], mask) for h, krows, mask in units]
                laters = [_dot_exact_l(lk, upper) for _, lk in scores]
                runs = [kc[h][0] for h in range(2)]
                accs = [kc[h][1] for h in range(2)]
                for (h, krows, mask), (ls, lk), later in zip(units, scores, laters):
                    w = jnp.where(mask, jnp.exp(ls + later + runs[h]), 0.0)
                    accs[h] = accs[h] + _dot(w, v_ref[krows, heads[h]], 1, 0)
                    runs[h] = runs[h] + jnp.sum(lk, axis=1, keepdims=True)
                return tuple((runs[h], accs[h]) for h in range(2))

            init = (jnp.zeros((BQ, 1), F32), jnp.zeros((BQ, SB_DIM), F32))
            res = lax.fori_loop(0, nkb // unroll, kloop, (init, init))
            for h, (sl, (run, acc)) in enumerate(zip(heads, res)):
                o_ref[rows, sl] = acc
                tot_ref[rows, h:h + 1] = run
            return carry

        lax.fori_loop(0, nq, qloop, 0)
        _wait_carried(exchange, carried, grid)

    def spec(off):
        return pl.BlockSpec((S, LANES), lambda b, hp: (b, hp + off))

    (cat, tot), targets = _carrier_call(
        body, name, grid, [spec(0), spec(nhp), spec(2 * nhp)],
        [spec(0), pl.BlockSpec((None, S, 2), lambda b, hp: (hp, b, 0))],
        [jax.ShapeDtypeStruct((T, 2 * SB_HEADS * SB_DIM), F32), jax.ShapeDtypeStruct((nhp, T, 2), F32)],
        [], {}, (proj, proj, proj), exchange)
    return cat, tot, targets


def _sb_bwd(proj, tot, dcat, S, name, exchange=None):
    T, width = proj.shape
    BQ, BK = SB_QBLOCK, ATT_BLOCK
    unroll = BQ // BK
    nq = S // BQ
    nhp = SB_HEADS // 2
    scale = SB_DIM ** -0.5
    heads = [slice(h * SB_DIM, (h + 1) * SB_DIM) for h in range(2)]

    grid = (T // S, nhp)

    def body(*refs):
        (q_ref, k_ref, v_ref, tot_ref, do_ref, dp_hbm, dq_s, dk_s, dv_s, sems), carried = _carried(
            exchange, refs, 5, 1, 4)
        _start_carried(exchange, carried, grid)
        b, hp = pl.program_id(0), pl.program_id(1)
        ahead = _iota2((BQ, BK), 1) - _iota2((BQ, BK), 0)
        r, c = _iota2((BK, BK), 0), _iota2((BK, BK), 1)
        upto = (r <= c).astype(BF16)
        earlier = (r < c).astype(BF16)
        dk_s[...] = jnp.zeros_like(dk_s)
        dv_s[...] = jnp.zeros_like(dv_s)

        def qloop(iq, carry):
            q0 = pl.multiple_of(iq * BQ, BQ)
            rows = pl.ds(q0, BQ)
            qbs = [q_ref[rows, sl] for sl in heads]
            dobs = [do_ref[rows, sl] for sl in heads]
            totals = [tot_ref[rows, h:h + 1] for h in range(2)]

            def kloop(m, kc):
                blocks = []
                for u in range(unroll):
                    k0 = pl.multiple_of((unroll * m + u) * BK, BK)
                    blocks.append((pl.ds(k0, BK), ahead < q0 - k0))
                units = [(h, krows, mask) for krows, mask in blocks for h in range(2)]
                kblks = [k_ref[krows, heads[h]] for h, krows, _ in units]
                scores = [_sb_scores(qbs[h], kblk, mask) for (h, _, mask), kblk in zip(units, kblks)]
                prefixes = [_dot_exact_l(lk, upto) for _, lk in scores]
                dws = [_dot(dobs[h], v_ref[krows, heads[h]], 1, 1) for h, krows, _ in units]
                runs = [kc[h][0] for h in range(2)]
                gruns = [kc[h][1] for h in range(2)]
                dqs = [kc[h][2] for h in range(2)]
                ws, gs = [], []
                for (h, _, mask), (ls, lk), prefix, dw in zip(units, scores, prefixes, dws):
                    w = jnp.where(mask, jnp.exp(ls + (totals[h] - (prefix + runs[h]))), 0.0)
                    runs[h] = runs[h] + jnp.sum(lk, axis=1, keepdims=True)
                    ws.append(w)
                    gs.append(w * dw)
                gprefixes = [_dot_exact_l(g, earlier) for g in gs]
                dzs = []
                for (h, _, mask), (ls, _), g, gprefix in zip(units, scores, gs, gprefixes):
                    beta = jnp.exp(ls)
                    dzs.append(jnp.where(mask, g * (1.0 - beta) - beta * (gprefix + gruns[h]), 0.0) * scale)
                    gruns[h] = gruns[h] + jnp.sum(g, axis=1, keepdims=True)
                for (h, krows, _), kblk, w, dz in zip(units, kblks, ws, dzs):
                    dv_s[krows, heads[h]] += _dot(w, dobs[h], 0, 0)
                    dk_s[krows, heads[h]] += _dot(dz, qbs[h], 0, 0)
                    dqs[h] = dqs[h] + _dot(dz, kblk, 1, 0)
                return tuple((runs[h], gruns[h], dqs[h]) for h in range(2))

            zero = jnp.zeros((BQ, 1), F32)
            init = (zero, zero, jnp.zeros((BQ, SB_DIM), F32))
            res = lax.fori_loop(0, (iq + 1) * (BQ // BK) // unroll, kloop, (init, init))
            for sl, (_, _, dq) in zip(heads, res):
                dq_s[rows, sl] = dq
            return carry

        lax.fori_loop(0, nq, qloop, 0)
        r0 = pl.multiple_of(b * S, S)
        copies = []
        for n, buf in enumerate((dq_s, dk_s, dv_s)):
            c0 = pl.multiple_of((hp + n * nhp) * LANES, LANES)
            copies.append(pltpu.make_async_copy(buf, dp_hbm.at[pl.ds(r0, S), pl.ds(c0, LANES)], sems.at[n]))
        for cp in copies:
            cp.start()
        for cp in copies:
            cp.wait()
        _wait_carried(exchange, carried, grid)

    def spec(off):
        return pl.BlockSpec((S, LANES), lambda b, hp: (b, hp + off))

    (dproj,), targets = _carrier_call(
        body, name, grid,
        [spec(0), spec(nhp), spec(2 * nhp), pl.BlockSpec((None, S, 2), lambda b, hp: (hp, b, 0)), spec(0)],
        [ANY], [jax.ShapeDtypeStruct((T, width), F32)],
        [pltpu.VMEM((S, LANES), F32)] * 3 + [pltpu.SemaphoreType.DMA((3,))], {}, (proj, proj, proj, tot, dcat),
        exchange)
    return dproj, targets


HG_COL0 = 3 * SB_HEADS * SB_DIM // LANES


def _hg_gates(q, fp, lbv):
    sg = _sig(fp)
    f = lbv + (1.0 - lbv) * sg
    kk = (1.0 - lbv) * _sig(-fp)
    sq = _sig(q)
    return sg, f, kk, sq


def _hg_chunk(qs, kk, lf, incl):
    C = HG_CHUNK
    b = _dot_exact_r(incl, lf)
    bl = b[C - 1:C, :]
    bm = b[C // 2 - 1:C // 2, :]
    e_t = jnp.exp(b - bm)
    e_s = jnp.exp(bm - b)
    e_i = jnp.exp(b)
    e_e = jnp.exp(bl - b)
    return bl, e_t, e_s, e_i, e_e


def _hgrn_fwd(proj, cat, lb, hgn, S, name, exchange=None):
    T = proj.shape[0]
    B = T // S
    C = HG_CHUNK
    NC = S // C

    grid = (B, HG_HEADS // HG_STEP)

    def body(*refs):
        (q_ref, f_ref, i_ref, g_ref, lb_ref, hgn_ref, _, ob_ref, oraw_ref, st_ref, state), carried = _carried(
            exchange, refs, 7, 3, 1)
        _start_carried(exchange, carried, grid)
        state[...] = jnp.zeros_like(state)
        row, col = _iota2((C, C), 0), _iota2((C, C), 1)
        causal = row >= col
        incl = causal.astype(BF16)

        def trip(m, carry):
            units = []
            for u in range(HG_TRIP):
                c = m * HG_TRIP + u
                rows = pl.ds(pl.multiple_of(c * C, C), C)
                units += [(c, hh, rows, slice(hh * LANES, (hh + 1) * LANES)) for hh in range(HG_STEP)]
            qs_, kks, lfs = [], [], []
            for _, _, rows, hs in units:
                q = q_ref[rows, hs]
                _, f, kk, sq = _hg_gates(q, f_ref[rows, hs], lb_ref[:, hs])
                qs_.append(q * sq)
                kks.append(kk)
                lfs.append(jnp.log(f))
            decays = [_hg_chunk(None, None, lf, incl) for lf in lfs]
            ps = [jnp.where(causal, _dot(qs * e_t, kk * e_s, 1, 1), 0.0)
                  for qs, kk, (_, e_t, e_s, _, _) in zip(qs_, kks, decays)]
            outs = []
            for (c, hh, rows, hs), qs, kk, (bl, _, _, e_i, e_e), p in zip(units, qs_, kks, decays, ps):
                iv = i_ref[rows, hs]
                st = state[hh]
                st_ref[0, hh, c] = st
                outs.append(_dot(qs * e_i, st, 1, 1) + _dot(p, iv, 1, 0))
                state[hh] = st * jnp.exp(bl) + _dot(iv, kk * e_e, 0, 0)
            for (_, _, rows, hs), o in zip(units, outs):
                gv = g_ref[rows, hs]
                oraw_ref[rows, hs] = o
                r = lax.rsqrt(jnp.mean(o * o, axis=1, keepdims=True) + EPS)
                ob_ref[rows, hs] = o * r * hgn_ref[...] * (gv * _sig(gv))
            return carry

        lax.fori_loop(0, NC // HG_TRIP, trip, 0)
        _wait_carried(exchange, carried, grid)

    width = HG_STEP * LANES
    col0 = HG_COL0 * LANES // width
    nstep = HG_HEADS // HG_STEP

    def spec(off):
        return pl.BlockSpec((S, width), lambda b, h: (b, h + off))

    outs, targets = _carrier_call(
        body, name, grid,
        [spec(col0), spec(col0 + nstep), spec(col0 + 2 * nstep), spec(col0 + 3 * nstep),
         pl.BlockSpec((1, width), lambda b, h: (0, h)), pl.BlockSpec((1, LANES), lambda b, h: (0, 0)), ANY],
        [spec(nstep), spec(0), pl.BlockSpec((1, HG_STEP, NC, LANES, LANES), lambda b, h: (b, h, 0, 0, 0))],
        [jax.ShapeDtypeStruct(cat.shape, F32), jax.ShapeDtypeStruct((T, HG_HEADS * LANES), F32),
         jax.ShapeDtypeStruct((B, HG_HEADS, NC, LANES, LANES), F32)],
        [pltpu.VMEM((HG_STEP, LANES, LANES), F32)], {6: 0}, (proj, proj, proj, proj, lb, hgn, cat), exchange)
    return (*outs, targets)


def _hgrn_bwd(proj, oraw, dcat, states, lb, hgn, dproj, S, name):
    T = proj.shape[0]
    B = T // S
    C = HG_CHUNK
    NC = S // C

    def body(q_ref, f_ref, i_ref, g_ref, oraw_ref, dy_ref, st_ref, lb_ref, hgn_ref, dp_in,
             dp_hbm, dlb_ref, dhgn_ref, dstate, dq_s, df_s, di_s, dg_s, sems):
        del dp_in
        h, b = pl.program_id(0), pl.program_id(1)
        row, col = _iota2((C, C), 0), _iota2((C, C), 1)
        causal = row >= col
        incl = causal.astype(BF16)
        last_row = _iota2((C, LANES), 0) == C - 1
        hg = hgn_ref[...]
        dstate[...] = jnp.zeros_like(dstate)

        @pl.when(b == 0)
        def _():
            dlb_ref[...] = jnp.zeros_like(dlb_ref)

        @pl.when(jnp.logical_and(b == 0, h == 0))
        def _():
            dhgn_ref[...] = jnp.zeros_like(dhgn_ref)

        def trip(m, carry):
            units = []
            for u in range(HG_TRIP):
                c = NC - 1 - (m * HG_TRIP + u)
                rows = pl.ds(pl.multiple_of(c * C, C), C)
                units += [(c, hh, rows, slice(hh * LANES, (hh + 1) * LANES)) for hh in range(HG_STEP)]
            dos = []
            dhgn = jnp.zeros((1, LANES), F32)
            for _, _, rows, hs in units:
                gv, o, dy = g_ref[rows, hs], oraw_ref[rows, hs], dy_ref[rows, hs]
                r = lax.rsqrt(jnp.mean(o * o, axis=1, keepdims=True) + EPS)
                on = o * r
                sgv = _sig(gv)
                silu_g = gv * sgv
                dg_s[rows, hs] = dy * on * hg * (sgv * (1.0 + gv * (1.0 - sgv)))
                dhgn = dhgn + jnp.sum(dy * on * silu_g, axis=0, keepdims=True)
                dn = dy * hg * silu_g
                dos.append(r * dn - o * (r * r * r * jnp.mean(o * dn, axis=1, keepdims=True)))
            dhgn_ref[...] += dhgn
            pre = []
            for (_, _, rows, hs), do in zip(units, dos):
                q, iv = q_ref[rows, hs], i_ref[rows, hs]
                sg, f, kk, sq = _hg_gates(q, f_ref[rows, hs], lb_ref[:, hs])
                pre.append((q, iv, sg, f, kk, sq, q * sq))
            decays = [_hg_chunk(None, None, jnp.log(f), incl) for _, _, _, f, _, _, _ in pre]
            prods = [(qs * e_t, kk * e_s, qs * e_i, kk * e_e)
                     for (_, _, _, _, kk, _, qs), (_, e_t, e_s, e_i, e_e) in zip(pre, decays)]
            ps = [jnp.where(causal, _dot(qd, kd, 1, 1), 0.0) for qd, kd, _, _ in prods]
            dps = [jnp.where(causal, _dot(do, iv, 1, 1), 0.0) for do, (_, iv, _, _, _, _, _) in zip(dos, pre)]
            dqds = [_dot(dp, kd, 1, 0) for dp, (_, kd, _, _) in zip(dps, prods)]
            dkds = [_dot(dp, qd, 0, 0) for dp, (qd, _, _, _) in zip(dps, prods)]
            pdos = [_dot(p, do, 0, 0) for p, do in zip(ps, dos)]
            chain = []
            for (c, hh, rows, hs), do, (_, iv, _, _, _, _, _), (bl, _, _, _, _), (_, _, qi, ke), pdo in zip(
                    units, dos, pre, decays, prods, pdos):
                st = st_ref[0, hh, c]
                dst = dstate[hh]
                ebl = jnp.exp(bl)
                dqi = _dot(do, st, 1, 0)
                di_s[rows, hs] = pdo + _dot(ke, dst, 1, 1)
                dke = _dot(iv, dst, 1, 0)
                dbl = jnp.sum(st * dst, axis=0, keepdims=True) * ebl + jnp.sum(dke * ke, axis=0, keepdims=True)
                dstate[hh] = _dot(do, qi, 0, 0) + dst * ebl
                chain.append((dqi, dke, dbl))
            for (_, _, rows, hs), (q, _, sg, f, _, sq, _), (_, e_t, e_s, e_i, e_e), (qd, kd, qi, ke), dqd, dkd, (
                    dqi, dke, dbl) in zip(units, pre, decays, prods, dqds, dkds, chain):
                lbv = lb_ref[:, hs]
                db = dqd * qd - dkd * kd + dqi * qi - dke * ke + jnp.where(last_row, dbl, 0.0)
                dqs = dqd * e_t + dqi * e_i
                dkk = dkd * e_s + dke * e_e
                dlf = _dot_exact_r(incl, db, cm=0)
                oms = 1.0 - sg
                dfd = dlf / f
                df_s[rows, hs] = (dfd - dkk) * (1.0 - lbv) * sg * oms
                dq_s[rows, hs] = dqs * (sq * (1.0 + q * (1.0 - sq)))
                dlb_ref[:, hs] += jnp.sum((dfd - dkk) * oms, axis=0, keepdims=True)
            return carry

        lax.fori_loop(0, NC // HG_TRIP, trip, 0)
        r0 = pl.multiple_of(b * S, S)
        copies = []
        for n, buf in enumerate((dq_s, df_s, di_s, dg_s)):
            c0 = pl.multiple_of((col0 + n * nstep + h) * width, width)
            copies.append(pltpu.make_async_copy(buf, dp_hbm.at[pl.ds(r0, S), pl.ds(c0, width)], sems.at[n]))
        for cp in copies:
            cp.start()
        for cp in copies:
            cp.wait()

    width = HG_STEP * LANES
    col0 = HG_COL0 * LANES // width
    nstep = HG_HEADS // HG_STEP

    def spec(off):
        return pl.BlockSpec((S, width), lambda h, b: (b, h + off))

    return pl.pallas_call(
        body, name=name, grid=(nstep, B),
        in_specs=[spec(col0), spec(col0 + nstep), spec(col0 + 2 * nstep), spec(col0 + 3 * nstep), spec(0), spec(nstep),
                  pl.BlockSpec((1, HG_STEP, NC, LANES, LANES), lambda h, b: (b, h, 0, 0, 0)),
                  pl.BlockSpec((1, width), lambda h, b: (0, h)), pl.BlockSpec((1, LANES), lambda h, b: (0, 0)), ANY],
        out_specs=[ANY, pl.BlockSpec((1, width), lambda h, b: (0, h)), pl.BlockSpec((1, LANES), lambda h, b: (0, 0))],
        out_shape=[jax.ShapeDtypeStruct(dproj.shape, F32), jax.ShapeDtypeStruct((1, HG_HEADS * LANES), F32),
                   jax.ShapeDtypeStruct((1, LANES), F32)],
        scratch_shapes=[pltpu.VMEM((HG_STEP, LANES, LANES), F32)] + [pltpu.VMEM((S, width), F32)] * 4
        + [pltpu.SemaphoreType.DMA((4,))],
        input_output_aliases={9: 0},
        compiler_params=_cparams("arbitrary", "arbitrary"))(proj, proj, proj, proj, oraw, dcat, states, lb, hgn, dproj)


def _lower_bound_fwd(logits, name):
    assert logits.shape[0] == 2

    def body(l_ref, o_ref):
        l0, l1 = l_ref[0:1, :], l_ref[1:2, :]
        m = jnp.maximum(l0, l1)
        e0, e1 = jnp.exp(l0 - m), jnp.exp(l1 - m)
        o_ref[0:1, :] = jnp.zeros_like(l0)
        o_ref[1:2, :] = e1 / (e0 + e1)

    return pl.pallas_call(body, name=name, out_shape=jax.ShapeDtypeStruct(logits.shape, F32))(logits)


def _lower_bound_bwd(logits, dlb, name):
    def body(l_ref, d_ref, o_ref):
        l0, l1 = l_ref[0:1, :], l_ref[1:2, :]
        m = jnp.maximum(l0, l1)
        e0, e1 = jnp.exp(l0 - m), jnp.exp(l1 - m)
        s1 = e1 / (e0 + e1)
        t = s1 * (1.0 - s1) * d_ref[1:2, :]
        o_ref[0:1, :] = -t
        o_ref[1:2, :] = t

    return pl.pallas_call(body, name=name, out_shape=jax.ShapeDtypeStruct(logits.shape, F32))(logits, dlb)


def _bucket_thresholds():
    dist = np.arange(WINDOW)
    max_exact = N_BUCKETS // 2
    large = max_exact + (np.log(np.maximum(dist, max_exact) / max_exact) / math.log(MAX_DISTANCE / max_exact)
                         * (N_BUCKETS - max_exact)).astype(np.int32)
    bucket = np.where(dist < max_exact, dist, np.minimum(large, N_BUCKETS - 1))
    assert np.all(np.diff(bucket) >= 0)
    return [int(np.argmax(bucket >= k)) if np.any(bucket >= k) else 10 ** 6 for k in range(1, N_BUCKETS)]


def _band_bucket():
    dist = _iota2((WINDOW, 2 * WINDOW), 0) + WINDOW - _iota2((WINDOW, 2 * WINDOW), 1)
    bucket = jnp.zeros((WINDOW, 2 * WINDOW), jnp.int32)
    for thr in _bucket_thresholds():
        bucket = bucket + (dist >= thr).astype(jnp.int32)
    band = jnp.logical_and(dist >= 0, dist < WINDOW)
    return bucket, band


def _bias_build(rel_bias, name):
    def body(rb_ref, o_ref):
        h = pl.program_id(0)
        bucket, _ = _band_bucket()
        bias = jnp.zeros((WINDOW, 2 * WINDOW), F32)
        for k in range(N_BUCKETS):
            bias = jnp.where(bucket == k, rb_ref[k, h], bias)
        o_ref[0] = bias

    return pl.pallas_call(
        body, name=name, grid=(SW_HEADS,), in_specs=[pl.BlockSpec(memory_space=pltpu.SMEM)],
        out_specs=pl.BlockSpec((1, WINDOW, 2 * WINDOW), lambda h: (h, 0, 0)),
        out_shape=jax.ShapeDtypeStruct((SW_HEADS, WINDOW, 2 * WINDOW), F32),
        compiler_params=_cparams("parallel"))(rel_bias)


def _bias_reduce(dbias, name):
    def body(d_ref, o_ref):
        bucket, band = _band_bucket()
        d = jnp.where(band, d_ref[0], 0.0)
        lane = _iota2((1, LANES), 1)
        out = jnp.zeros((1, LANES), F32)
        for k in range(N_BUCKETS):
            out = jnp.where(lane == k, jnp.sum(jnp.where(bucket == k, d, 0.0)), out)
        o_ref[0] = out

    return pl.pallas_call(
        body, name=name, grid=(SW_HEADS,), in_specs=[pl.BlockSpec((1, WINDOW, 2 * WINDOW), lambda h: (h, 0, 0))],
        out_specs=pl.BlockSpec((1, 1, LANES), lambda h: (h, 0, 0)),
        out_shape=jax.ShapeDtypeStruct((SW_HEADS, 1, LANES), F32), compiler_params=_cparams("parallel"))(dbias)


SW_Q_COLS = SW_HEADS * SW_DIM
SW_K_BLOCK0 = SW_Q_COLS // LANES
SW_V_BLOCK0 = SW_K_BLOCK0 + SW_KV * SW_DIM // LANES
SW_STEP_HEADS = 8
SW_TRIP = 1


def _swa_logits(qb, kprev, kcur, bias_ref, hl, mprev, mcur):
    scale = SW_DIM ** -0.5
    lp = jnp.where(mprev, _dot(qb, kprev, 1, 1) * scale + bias_ref[hl, :, 0:WINDOW], NEG)
    lc = jnp.where(mcur, _dot(qb, kcur, 1, 1) * scale + bias_ref[hl, :, WINDOW:2 * WINDOW], NEG)
    return lp, lc


def _swa_softmax(lp, lc, sink):
    m = jnp.maximum(jnp.maximum(jnp.max(lp, axis=1, keepdims=True), jnp.max(lc, axis=1, keepdims=True)), sink)
    ep, ec = jnp.exp(lp - m), jnp.exp(lc - m)
    es = jnp.exp(sink - m)
    den = jnp.sum(ep, axis=1, keepdims=True) + jnp.sum(ec, axis=1, keepdims=True) + es
    return ep, ec, es, den


def _swa_block_heads(n, q_ref, k_ref, v_ref, bias_ref, sink_ref, kp, above, mcur):
    W = WINDOW
    rows = pl.ds(pl.multiple_of(n * W, W), W)
    prow = pl.ds(pl.multiple_of(jnp.maximum(n - 1, 0) * W, W), W)
    mprev = jnp.logical_and(above, n > 0)
    heads = []
    for kvh in range(2):
        ksl = slice(kvh * SW_DIM, (kvh + 1) * SW_DIM)
        kv = (k_ref[rows, ksl], k_ref[prow, ksl], v_ref[rows, ksl], v_ref[prow, ksl])
        for g in range(4):
            hl = kvh * 4 + g
            heads.append((hl, ksl, slice(hl * SW_DIM, (hl + 1) * SW_DIM), kv))
    qbs = [q_ref[rows, qsl] for _, _, qsl, _ in heads]
    logits = [_swa_logits(qb, kv[1], kv[0], bias_ref, hl, mprev, mcur) for qb, (hl, _, _, kv) in zip(qbs, heads)]
    soft = [_swa_softmax(lp, lc, sink_ref[kp * SW_STEP_HEADS + hl]) for (lp, lc), (hl, _, _, _) in zip(logits, heads)]
    return rows, prow, heads, qbs, soft


def _swa_fwd(qkn, proj, bias, sinks, S, name, exchange=None):
    T = qkn.shape[0]
    W = WINDOW
    nb = S // W

    grid = (T // S, 2)

    def body(*refs):
        (q_ref, k_ref, v_ref, bias_ref, sink_ref, o_ref), carried = _carried(exchange, refs, 5, 1, 0)
        _start_carried(exchange, carried, grid)
        kp = pl.program_id(1)
        row, col = _iota2((W, W), 0), _iota2((W, W), 1)
        mcur = col <= row
        above = col > row

        def blk(m, carry):
            staged = [_swa_block_heads(m * SW_TRIP + u, q_ref, k_ref, v_ref, bias_ref, sink_ref, kp, above, mcur)
                      for u in range(SW_TRIP)]
            for rows, _, heads, _, soft in staged:
                outs = [(_dot(ep, kv[3], 1, 0) + _dot(ec, kv[2], 1, 0)) / den
                        for (ep, ec, _, den), (_, _, _, kv) in zip(soft, heads)]
                for (_, _, qsl, _), o in zip(heads, outs):
                    o_ref[rows, qsl] = o
            return carry

        lax.fori_loop(0, nb // SW_TRIP, blk, 0)
        _wait_carried(exchange, carried, grid)

    (o,), targets = _carrier_call(
        body, name, grid,
        [pl.BlockSpec((S, 4 * LANES), lambda b, kp: (b, kp)),
         pl.BlockSpec((S, LANES), lambda b, kp: (b, SW_K_BLOCK0 + kp)),
         pl.BlockSpec((S, LANES), lambda b, kp: (b, SW_V_BLOCK0 + kp)),
         pl.BlockSpec((SW_STEP_HEADS, W, 2 * W), lambda b, kp: (kp, 0, 0)),
         pl.BlockSpec(memory_space=pltpu.SMEM)],
        [pl.BlockSpec((S, 4 * LANES), lambda b, kp: (b, kp))], [jax.ShapeDtypeStruct((T, SW_Q_COLS), F32)],
        [], {}, (qkn, qkn, proj, bias, sinks), exchange)
    return o, targets


def _swa_bwd(qkn, proj, bias, sinks, do, S, name, exchange=None):
    T, width = proj.shape
    W = WINDOW
    nb = S // W
    scale = SW_DIM ** -0.5

    grid = (2, T // S)

    def body(*refs):
        (q_ref, k_ref, v_ref, bias_ref, sink_ref, do_ref, dp_hbm, dbias_ref, dsink_ref,
         dq_s, dk_s, dv_s, sems), carried = _carried(exchange, refs, 6, 3, 4)
        _start_carried(exchange, carried, grid)
        kp, b = pl.program_id(0), pl.program_id(1)
        row, col = _iota2((W, W), 0), _iota2((W, W), 1)
        mcur = col <= row
        above = col > row
        dk_s[...] = jnp.zeros_like(dk_s)
        dv_s[...] = jnp.zeros_like(dv_s)

        @pl.when(b == 0)
        def _():
            dbias_ref[...] = jnp.zeros_like(dbias_ref)
            dsink_ref[...] = jnp.zeros_like(dsink_ref)

        def blk(m, carry):
            staged = [_swa_block_heads(m * SW_TRIP + u, q_ref, k_ref, v_ref, bias_ref, sink_ref, kp, above, mcur)
                      for u in range(SW_TRIP)]
            for rows, prow, heads, qbs, soft in staged:
                one_block(rows, prow, heads, qbs, soft)
            return carry

        def one_block(rows, prow, heads, qbs, soft):
            dobs = [do_ref[rows, qsl] for _, _, qsl, _ in heads]
            dps = [(_dot(dob, kv[3], 1, 1), _dot(dob, kv[2], 1, 1)) for dob, (_, _, _, kv) in zip(dobs, heads)]
            grads = []
            for (hl, _, _, _), (ep, ec, es, den), (dpp, dpc) in zip(heads, soft, dps):
                inv = 1.0 / den
                pp, pc = ep * inv, ec * inv
                total = jnp.sum(pp * dpp, axis=1, keepdims=True) + jnp.sum(pc * dpc, axis=1, keepdims=True)
                dlp = pp * (dpp - total)
                dlc = pc * (dpc - total)
                dsink_ref[hl:hl + 1, :] += jnp.zeros((1, LANES), F32) - jnp.sum(es * inv * total)
                dbias_ref[hl, :, 0:W] += dlp
                dbias_ref[hl, :, W:2 * W] += dlc
                grads.append((pp, pc, dlp, dlc))
            for (_, _, qsl, kv), (_, _, dlp, dlc) in zip(heads, grads):
                dq_s[rows, qsl] = (_dot(dlp, kv[1], 1, 0) + _dot(dlc, kv[0], 1, 0)) * scale
            for kvh in range(2):
                group = range(4 * kvh, 4 * kvh + 4)
                ksl = heads[4 * kvh][1]
                dk_s[prow, ksl] += sum(_dot(grads[i][2], qbs[i], 0, 0) for i in group) * scale
                dk_s[rows, ksl] += sum(_dot(grads[i][3], qbs[i], 0, 0) for i in group) * scale
                dv_s[prow, ksl] += sum(_dot(grads[i][0], dobs[i], 0, 0) for i in group)
                dv_s[rows, ksl] += sum(_dot(grads[i][1], dobs[i], 0, 0) for i in group)

        lax.fori_loop(0, nb // SW_TRIP, blk, 0)
        r0 = pl.multiple_of(b * S, S)
        cq =pl.multiple_of(kp * 4 * LANES, LANES)
        ck = pl.multiple_of((SW_K_BLOCK0 + kp) * LANES, LANES)
        cv = pl.multiple_of((SW_V_BLOCK0 + kp) * LANES, LANES)
        copies = [pltpu.make_async_copy(dq_s, dp_hbm.at[pl.ds(r0, S), pl.ds(cq, 4 * LANES)], sems.at[0]),
                  pltpu.make_async_copy(dk_s, dp_hbm.at[pl.ds(r0, S), pl.ds(ck, LANES)], sems.at[1]),
                  pltpu.make_async_copy(dv_s, dp_hbm.at[pl.ds(r0, S), pl.ds(cv, LANES)], sems.at[2])]
        for cp in copies:
            cp.start()
        for cp in copies:
            cp.wait()
        _wait_carried(exchange, carried, grid)

    qspec = pl.BlockSpec((S, 4 * LANES), lambda kp, b: (b, kp))
    outs, targets = _carrier_call(
        body, name, grid,
        [qspec, pl.BlockSpec((S, LANES), lambda kp, b: (b, SW_K_BLOCK0 + kp)),
         pl.BlockSpec((S, LANES), lambda kp, b: (b, SW_V_BLOCK0 + kp)),
         pl.BlockSpec((SW_STEP_HEADS, W, 2 * W), lambda kp, b: (kp, 0, 0)),
         pl.BlockSpec(memory_space=pltpu.SMEM), qspec],
        [ANY, pl.BlockSpec((SW_STEP_HEADS, W, 2 * W), lambda kp, b: (kp, 0, 0)),
         pl.BlockSpec((SW_STEP_HEADS, LANES), lambda kp, b: (kp, 0))],
        [jax.ShapeDtypeStruct((T, width), F32), jax.ShapeDtypeStruct((SW_HEADS, W, 2 * W), F32),
         jax.ShapeDtypeStruct((SW_HEADS, LANES), F32)],
        [pltpu.VMEM((S, 4 * LANES), F32), pltpu.VMEM((S, LANES), F32), pltpu.VMEM((S, LANES), F32),
         pltpu.SemaphoreType.DMA((3,))], {}, (qkn, qkn, proj, bias, sinks, do), exchange)
    return (*outs, targets)


CHIP_FLIPS = ((1, 0), (0, 1), (1, 1))


def _flip(v, f):
    return 1 - v if f else v


class _Exchange:
    def __init__(self, sources, targets, copies):
        self.sources, self.targets = list(sources), list(targets)
        self._copies = copies
        n = len(self.sources)
        self.scratch = [pltpu.SemaphoreType.DMA((n, 3)), pltpu.SemaphoreType.DMA((n, 3)),
                        pltpu.SemaphoreType.DMA((n,))]

    def _descriptors(self, srcs, dsts, sems, chip, peers):
        send, recv, loc = sems
        out = []
        for t, (local, remote) in enumerate(self._copies(srcs, dsts, chip)):
            out.append(pltpu.make_async_copy(local[0], local[1], loc.at[t]))
            for r, (src, dst) in enumerate(remote):
                out.append(pltpu.make_async_remote_copy(src, dst, send.at[t, r], recv.at[t, r],
                                                        device_id=peers[r], device_id_type=MESH))
        return out

    def start(self, srcs, dsts, sems):
        x, y, c = lax.axis_index("x"), lax.axis_index("y"), lax.axis_index("c")
        peers = [(_flip(x, fx), _flip(y, fy), c) for fx, fy in CHIP_FLIPS]
        for chip in range(4):
            @pl.when(2 * x + y == chip)
            def _():
                for cp in self._descriptors(srcs, dsts, sems, chip, peers):
                    cp.start()

    def wait(self, srcs, dsts, sems):
        me = (lax.axis_index("x"), lax.axis_index("y"), lax.axis_index("c"))
        for cp in self._descriptors(srcs, dsts, sems, 0, [me] * 3):
            cp.wait()

    def operands(self):
        return self.sources + self.targets

    def specs(self):
        ns, nt = len(self.sources), len(self.targets)
        return [ANY] * (ns + nt), [ANY] * nt, [jax.ShapeDtypeStruct(t.shape, t.dtype) for t in self.targets]

    def aliases(self, n_in, n_out):
        ns = len(self.sources)
        return {n_in + ns + i: n_out + i for i in range(len(self.targets))}

    def split(self, refs, n_in, n_out, n_scr):
        ns, nt = len(self.sources), len(self.targets)
        o0 = n_in + ns + nt
        s0 = o0 + n_out + nt
        own = list(refs[:n_in]) + list(refs[o0:o0 + n_out]) + list(refs[s0:s0 + n_scr])
        return own, (refs[n_in:n_in + ns], refs[o0 + n_out:o0 + n_out + nt], refs[s0 + n_scr:])


def _carried(exchange, refs, n_in, n_out, n_scr):
    if exchange is None:
        return list(refs), None
    return exchange.split(refs, n_in, n_out, n_scr)


def _grid_edge(grid, last):
    conds = [pl.program_id(d) == (n - 1 if last else 0) for d, n in enumerate(grid)]
    out = conds[0]
    for cnd in conds[1:]:
        out = jnp.logical_and(out, cnd)
    return out


def _start_carried(exchange, parts, grid):
    if parts is not None:
        @pl.when(_grid_edge(grid, False))
        def _():
            exchange.start(*parts)


def _wait_carried(exchange, parts, grid):
    if parts is not None:
        @pl.when(_grid_edge(grid, True))
        def _():
            exchange.wait(*parts)


def _carrier_call(body, name, grid, in_specs, out_specs, out_shape, scratch, aliases, operands, exchange):
    n_out = len(out_shape)
    aliases = dict(aliases)
    if exchange is not None:
        ex_in, ex_out, ex_shape = exchange.specs()
        aliases.update(exchange.aliases(len(in_specs), n_out))
        in_specs, out_specs, out_shape = in_specs + ex_in, out_specs + ex_out, out_shape + ex_shape
        scratch = scratch + exchange.scratch
        operands = list(operands) + exchange.operands()
    outs = pl.pallas_call(body, name=name, grid=grid, in_specs=in_specs, out_specs=out_specs, out_shape=out_shape,
                          scratch_shapes=scratch, input_output_aliases=aliases,
                          compiler_params=_cparams(*["arbitrary"] * len(grid)))(*operands)
    return outs[:n_out], outs[n_out:]


def _exchange_call(exchange, name):
    in_specs, out_specs, out_shape = exchange.specs()

    def body(*refs):
        _, parts = exchange.split(refs, 0, 0, 0)
        exchange.start(*parts)
        exchange.wait(*parts)

    return pl.pallas_call(body, name=name, in_specs=in_specs, out_specs=out_specs, out_shape=out_shape,
                          scratch_shapes=exchange.scratch, input_output_aliases=exchange.aliases(0, 0))(
        *exchange.operands())


def _gather_exchange(shards, fulls, axes, layers):
    sizes = [s.shape[a] for s, a in zip(shards, axes)]

    def copies(srcs, dsts, chip):
        out = []
        for src, full, axis, size, (l0, l1) in zip(srcs, dsts, axes, sizes, layers):
            part = src.at[l0:l1]
            cut = pl.ds(chip * size, size)
            dst = full.at[l0:l1, cut, :] if axis == 1 else full.at[l0:l1, :, cut]
            out.append(((part, dst), [(part, dst)] * 3))
        return out

    return _Exchange(shards, fulls, copies)


def _scatter_exchange(grads, stacks, axes, layers):
    sizes = [g.shape[a - 1] // 4 for g, a in zip(grads, axes)]

    def copies(srcs, dsts, chip):
        out = []
        for g, stack, axis, size, layer in zip(srcs, dsts, axes, sizes, layers):
            def cut(j, g=g, axis=axis, size=size):
                return g.at[pl.ds(j * size, size), :] if axis == 1 else g.at[:, pl.ds(j * size, size)]

            remote = [(cut(chip ^ (2 * fx + fy)), stack.at[r, layer]) for r, (fx, fy) in enumerate(CHIP_FLIPS)]
            out.append(((cut(chip), stack.at[3, layer]), remote))
        return out

    return _Exchange(grads, stacks, copies)


def _swap_with_sibling(parts, name):
    n = len(parts)

    def body(*refs):
        ins, outs = refs[:n], refs[n:2 * n]
        send, recv = refs[2 * n:]
        peer = (lax.axis_index("x"), lax.axis_index("y"), 1 - lax.axis_index("c"))
        copies = [pltpu.make_async_remote_copy(ins[t], outs[t], send.at[t], recv.at[t], device_id=peer,
                                               device_id_type=MESH) for t in range(n)]
        for cp in copies:
            cp.start()
        for cp in copies:
            cp.wait()

    return pl.pallas_call(
        body, name=name, in_specs=[ANY] * n, out_specs=[ANY] * n,
        out_shape=[jax.ShapeDtypeStruct(p.shape, p.dtype) for p in parts],
        scratch_shapes=[pltpu.SemaphoreType.DMA((n,)), pltpu.SemaphoreType.DMA((n,))])(*parts)


def _allreduce_small(v, name):
    R = v.shape[0]
    ND = 8

    def body(v_ref, o_ref, buf, send, recv):
        x, y, c = lax.axis_index("x"), lax.axis_index("y"), lax.axis_index("c")
        me = 4 * x + 2 * y + c
        copies = []
        for d in range(1, ND):
            peer = (_flip(x, d >> 2 & 1), _flip(y, d >> 1 & 1), _flip(c, d & 1))
            copies.append(pltpu.make_async_remote_copy(v_ref, buf.at[me], send.at[d], recv.at[me], device_id=peer,
                                                       device_id_type=MESH))
        for cp in copies:
            cp.start()
        buf[pl.ds(me, 1)] = v_ref[...][None]
        for k in range(ND):
            @pl.when(me != k)
            def _():
                pltpu.make_async_remote_copy(v_ref, buf.at[k], send.at[0], recv.at[k], device_id=(x, y, c),
                                             device_id_type=MESH).wait_recv()
        for cp in copies:
            cp.wait_send()
        total = buf[0]
        for k in range(1, ND):
            total = total + buf[k]
        o_ref[...] = total

    vm = pl.BlockSpec(memory_space=pltpu.VMEM)
    return pl.pallas_call(
        body, name=name, in_specs=[vm], out_specs=vm, out_shape=jax.ShapeDtypeStruct((R, LANES), F32),
        scratch_shapes=[pltpu.VMEM((ND, R, LANES), F32), pltpu.SemaphoreType.DMA((ND,)),
                        pltpu.SemaphoreType.DMA((ND,))],
        compiler_params=pltpu.CompilerParams(vmem_limit_bytes=VMEM_LIMIT))(v)


def _tile2(R, Cn):
    tc = _pick(Cn, 2048)
    tr = R
    for cand in (256, 128, 64, 32, 16, 8):
        if R % cand == 0:
            tr = cand
            break
    return tr, tc


def _sum4(stack, name):
    _, R, Cn = stack.shape
    tr, tc = _tile2(R, Cn)

    def body(s_ref, o_ref):
        o_ref[...] = ((s_ref[0].astype(F32) + s_ref[1].astype(F32)) + s_ref[2].astype(F32)) + s_ref[3].astype(F32)

    return pl.pallas_call(
        body, name=name, grid=(R // tr, Cn // tc), in_specs=[pl.BlockSpec((4, tr, tc), lambda i, j: (0, i, j))],
        out_specs=pl.BlockSpec((tr, tc), lambda i, j: (i, j)), out_shape=jax.ShapeDtypeStruct((R, Cn), F32),
        compiler_params=_cparams("parallel", "parallel"))(stack)


def _adamw(w, m, v, g_parts, name):
    R, Cn = w.shape
    tr, tc = _tile2(R, Cn)
    npart = len(g_parts)
    c1 = 1.0 / (1.0 - ADAM_B1 ** ADAM_STEP)
    c2 = 1.0 / (1.0 - ADAM_B2 ** ADAM_STEP)

    def body(*refs):
        w_ref, m_ref, v_ref = refs[:3]
        g_refs = refs[3:3 + npart]
        g_out, d_out, m_out, v_out = refs[3 + npart:]
        g = g_refs[0][...]
        for r in g_refs[1:]:
            g = g + r[...]
        mn = ADAM_B1 * m_ref[...] + (1.0 - ADAM_B1) * g
        vn = ADAM_B2 * v_ref[...] + (1.0 - ADAM_B2) * (g * g)
        g_out[...] = g
        m_out[...] = mn
        v_out[...] = vn
        d_out[...] = -ADAM_LR * ((mn * c1) / (jnp.sqrt(vn * c2) + ADAM_EPS) + ADAM_WD * w_ref[...])

    spec = pl.BlockSpec((tr, tc), lambda i, j: (i, j))
    return pl.pallas_call(
        body, name=name, grid=(R // tr, Cn // tc), in_specs=[spec] * (3 + npart), out_specs=[spec] * 4,
        out_shape=[jax.ShapeDtypeStruct((R, Cn), F32)] * 4,
        compiler_params=_cparams("parallel", "parallel"))(w, m, v, *g_parts)


SHARDED = (("ab_w_in", 2), ("ab_w_out", 1), ("c_w_in", 2), ("c_w_out", 1), ("ffn_up", 2), ("ffn_conv", 2),
           ("ffn_down", 1), ("ple_gate", 1), ("ple_proj", 2))
SMALL = ("mix_norm", "hg_lb_logits", "hg_out_norm", "q_norm", "k_norm", "sinks", "rel_bias", "ffn_norm",
         "ffn_conv_b", "ple_norm")
WEIGHTS = ("mix_norm", "ab_w_in", "hg_lb_logits", "hg_out_norm", "ab_w_out", "c_w_in", "q_norm", "k_norm", "sinks",
           "rel_bias", "c_w_out", "ffn_norm", "ffn_up", "ffn_conv", "ffn_conv_b", "ffn_down", "ple_norm", "ple_gate",
           "ple_proj")
PACK_ALIGN = 8 * LANES


def _pack(arrs):
    pieces = []
    for a in arrs:
        flat = a.reshape(-1)
        pad = -flat.shape[0] % PACK_ALIGN
        pieces.append(jnp.pad(flat, (0, pad)).reshape(-1, LANES))
    return jnp.concatenate(pieces, axis=0)


def _unpack(packed, like):
    out, r = [], 0
    for a in like:
        size = int(np.prod(a.shape))
        rows = (size + PACK_ALIGN - 1) // PACK_ALIGN * 8
        out.append(packed[r:r + rows].reshape(-1)[:size].reshape(a.shape))
        r += rows
    return out


def _family_range(name, lo, hi):
    if name.startswith("ab_"):
        idx = [i // 2 for i in range(lo, hi) if i % 2 == 0]
    elif name.startswith("c_"):
        idx = [i // 2 for i in range(lo, hi) if i % 2 == 1]
    else:
        idx = list(range(lo, hi))
    return (idx[0], idx[-1] + 1) if idx else None


W_IN = ("ab_w_in", "c_w_in")
REST = tuple(k for k, _ in SHARDED if k not in W_IN)
MIXER = ("ab_w_in", "ab_w_out", "c_w_in", "c_w_out")
FFN = ("ffn_up", "ffn_conv", "ffn_down")
PLE = ("ple_gate", "ple_proj")
GATHER_PLAN = {
    "gather_first": [(0, W_IN)],
    "sb_fwd_0": [(0, REST), (1, ("c_w_in",))],
    "hgrn_fwd_0": [(1, ("ffn_up", "ffn_conv"))],
    "ffn_up_0": [(1, ("ffn_down", "c_w_out"))],
    "ffn_down_0": [(1, PLE)],
    "swa_fwd_1": [(2, ("ab_w_in",))],
    "ffn_up_1": [(2, ("ab_w_out", "ffn_down"))],
    "ffn_down_1": [(2, PLE)],
    "sb_fwd_2": [(2, ("ffn_up", "ffn_conv")), (3, ("c_w_in", "c_w_out", "ffn_up", "ffn_conv"))],
    "hgrn_fwd_2": [(3, ("ffn_down",))],
    "ffn_up_2": [(3, PLE)],
}
SCATTER_PLAN = {
    "d_ffn_up_3": [(3, ("ffn_down", "ffn_conv"))],
    "d_hn2_3": [(3, PLE)],
    "swa_bwd_3": [(3, ("ffn_up", "c_w_out"))],
    "sb_bwd_2": [(3, W_IN), (2, REST)],
    "convglu_bwd_1": [(2, W_IN)],
    "d_ffn_up_1": [(1, ("ffn_down", "ffn_conv"))],
    "d_hn2_1": [(1, PLE)],
    "swa_bwd_1": [(1, ("ffn_up", "c_w_out"))],
    "sb_bwd_0": [(1, W_IN), (0, REST)],
    "d_hn_0": [(0, W_IN)],
}


class _StepExchanges:
    def __init__(self, shards, W):
        self.shards, self.W = shards, W
        self.stacks = {}
        for k, axis in SHARDED:
            shp = shards[k].shape
            self.W[k] = lax.empty(tuple(4 * d if i == axis else d for i, d in enumerate(shp)), shards[k].dtype)
            self.stacks[k] = lax.empty((4,) + shp, shards[k].dtype)

    @staticmethod
    def _select(plan):
        idx = {}
        for layer, fams in plan:
            for k, _ in SHARDED:
                r = _family_range(k, layer, layer + 1)
                if r is not None and (fams is None or k in fams):
                    idx.setdefault(k, []).append(r[0])
        return [(k, axis, sorted(idx[k])) for k, axis in SHARDED if k in idx]

    def gather(self, call):
        sel = self._select(GATHER_PLAN.get(call, ()))
        if not sel:
            return None, None
        for _, _, ii in sel:
            assert ii == list(range(ii[0], ii[-1] + 1)), "one copy per family takes a contiguous layer range"
        names = [k for k, _, _ in sel]
        ex = _gather_exchange([self.shards[k] for k in names], [self.W[k] for k in names],
                              [a for _, a, _ in sel], [(ii[0], ii[-1] + 1) for _, _, ii in sel])
        return ex, (self.W, names)

    def scatter(self, call, G):
        sel = self._select(SCATTER_PLAN.get(call, ()))
        if not sel:
            return None, None
        assert all(len(ii) == 1 for _, _, ii in sel)
        names = [k for k, _, _ in sel]
        ex = _scatter_exchange([G[k][ii[0]] for k, _, ii in sel], [self.stacks[k] for k in names],
                               [a for _, a, _ in sel], [ii[0] for _, _, ii in sel])
        return ex, (self.stacks, names)

    @staticmethod
    def adopt(where, targets):
        if where is not None:
            book, names = where
            for k, t in zip(names, targets):
                book[k] = t


def _forward_backward(x, p, target, W, S, exchanges=None):
    T = x.shape[0]
    depth = p.shape[0]
    lb = _lower_bound_fwd(W["hg_lb_logits"], "lower_bound_fwd")
    bias = _bias_build(W["rel_bias"], "bias_build")
    qk_gain = jnp.concatenate([jnp.tile(W["q_norm"], (1, SW_HEADS)), jnp.tile(W["k_norm"], (1, SW_KV))], axis=1)

    def gathering(call):
        return exchanges.gather(call) if exchanges else (None, None)

    def scattering(call):
        return exchanges.scatter(call, G) if exchanges else (None, None)

    def matmul(a, b, mode, name, **kw):
        ex, where = (gathering(name) if name in GATHER_PLAN else
                     scattering(name) if name in SCATTER_PLAN else (None, None))
        out, moved = _matmul(a, b, mode, name, exchange=ex, **kw)
        _StepExchanges.adopt(where, moved)
        return out

    def mm(a, wname, layer, mode, name, **kw):
        return matmul(a, W[wname], mode, name, b_layer=layer, **kw)

    saved = []
    h = x
    for i in range(depth):
        j = i // 2
        s = {"h0": h}
        s["hn"] = _rmsnorm_fwd(h, W["mix_norm"][i:i + 1], f"mix_norm_fwd_{i}") if i == 0 else hn_next
        if i % 2 == 0:
            s["proj"] = mm(s["hn"], "ab_w_in", j, "nn", f"ab_in_{i}")
            ex, where = gathering(f"sb_fwd_{i}")
            cat, s["sb_tot"], arrived = _sb_fwd(s["proj"], S, f"sb_fwd_{i}", ex)
            _StepExchanges.adopt(where, arrived)
            ex, where = gathering(f"hgrn_fwd_{i}")
            s["cat"], s["oraw"], s["states"], arrived = _hgrn_fwd(s["proj"], cat, lb[j:j + 1],
                                                                  W["hg_out_norm"][j:j + 1], S, f"hgrn_fwd_{i}", ex)
            _StepExchanges.adopt(where, arrived)
            h, s["hn2"] = mm(s["cat"], "ab_w_out", j, "nn", f"ab_out_{i}", res=h, norm_gain=W["ffn_norm"][i:i + 1])
        else:
            s["proj"] = mm(s["hn"], "c_w_in", j, "nn", f"c_in_{i}")
            s["qkn"] = _headnorm_fwd(s["proj"], qk_gain[j:j + 1], f"qk_norm_fwd_{i}")
            ex, where = gathering(f"swa_fwd_{i}")
            s["o"], arrived = _swa_fwd(s["qkn"], s["proj"], bias, W["sinks"][j], S, f"swa_fwd_{i}", ex)
            _StepExchanges.adopt(where, arrived)
            h, s["hn2"] = mm(s["o"], "c_w_out", j, "nn", f"c_out_{i}", res=h, norm_gain=W["ffn_norm"][i:i + 1])
        s["h1"] = h
        s["u"] = mm(s["hn2"], "ffn_up", i, "nn", f"ffn_up_{i}", tiles=(min(T, 1024), D_FF // 2, D_MODEL))
        s["a"] = _convglu_fwd(s["u"], W["ffn_conv"][i], W["ffn_conv_b"][i:i + 1], S, f"convglu_fwd_{i}")
        h, s["hn3"] = mm(s["a"], "ffn_down", i, "nn", f"ffn_down_{i}", res=h, norm_gain=W["ple_norm"][i:i + 1])
        s["h2"] = h
        s["z"] = mm(s["hn3"], "ple_gate", i, "nn", f"ple_gate_{i}")
        s["pp"] = mm(p, "ple_proj", i, "nn", f"ple_proj_{i}", a_layer=i)
        h, hn_next = _ple_fwd(h, s["z"], s["pp"], W["mix_norm"][i + 1:i + 2] if i + 1 < depth else None,
                              f"ple_fwd_{i}")
        saved.append(s)

    loss, dh = _loss_fwd_bwd(h, target, "loss")

    G = {k: [None] * depth for k in ("mix_norm", "ffn_norm", "ple_norm", "ffn_up", "ffn_conv", "ffn_conv_b",
                                     "ffn_down", "ple_gate", "ple_proj")}
    for k in ("ab_w_in", "ab_w_out", "c_w_in", "c_w_out", "hg_out_norm", "q_norm", "k_norm", "sinks", "lb"):
        G[k] = [None] * (depth // 2)
    dbias_total = None
    for i in reversed(range(depth)):
        j = i // 2
        s = saved[i]
        dz, dpp = _ple_bwd(dh, s["z"], s["pp"], f"ple_bwd_{i}")
        G["ple_proj"][i] = matmul(p, dpp, "tn", f"d_ple_proj_{i}", out_dtype=BF16, a_layer=i)
        G["ple_gate"][i] = matmul(s["hn3"], dz, "tn", f"d_ple_gate_{i}", out_dtype=BF16)
        rt = min(T, 512)
        dh, G["ple_norm"][i] = mm(dz, "ple_gate", i, "nt", f"d_hn3_{i}", tiles=(rt, D_MODEL, D_MODEL),
                                  norm_bwd=(s["h2"], W["ple_norm"][i:i + 1], dh))

        half_ff = D_FF // 2
        da = mm(dh, "ffn_down", i, "nt", f"d_a_{i}", tiles=(min(T, 1024), half_ff, D_MODEL))
        G["ffn_down"][i] = matmul(s["a"], dh, "tn", f"d_ffn_down_{i}", out_dtype=BF16,
                                   tiles=(half_ff, 512, min(T, 2048)))
        ex, where = scattering(f"convglu_bwd_{i}")
        du, dcw, dcb, sent = _convglu_bwd(s["u"], da, W["ffn_conv"][i], W["ffn_conv_b"][i:i + 1], S,
                                          f"convglu_bwd_{i}", ex)
        _StepExchanges.adopt(where, sent)
        G["ffn_conv"][i] = jnp.swapaxes(dcw, 0, 1).reshape(3, 2 * D_FF)
        G["ffn_conv_b"][i] = dcb.reshape(1, 2 * D_FF)
        G["ffn_up"][i] = matmul(s["hn2"], du, "tn", f"d_ffn_up_{i}", out_dtype=BF16,
                                 tiles=(D_MODEL, half_ff, min(T, 1024)))
        dh, G["ffn_norm"][i] = mm(du, "ffn_up", i, "nt", f"d_hn2_{i}", tiles=(rt, D_MODEL, half_ff),
                                  norm_bwd=(s["h1"], W["ffn_norm"][i:i + 1], dh))

        if i % 2 == 0:
            dcat = mm(dh, "ab_w_out", j, "nt", f"d_cat_{i}")
            G["ab_w_out"][j] = matmul(s["cat"], dh, "tn", f"d_ab_out_{i}", out_dtype=BF16)
            ex, where = scattering(f"sb_bwd_{i}")
            dproj, sent = _sb_bwd(s["proj"], s["sb_tot"], dcat, S, f"sb_bwd_{i}", ex)
            _StepExchanges.adopt(where, sent)
            dproj, G["lb"][j], G["hg_out_norm"][j] = _hgrn_bwd(s["proj"], s["oraw"], dcat, s["states"], lb[j:j + 1],
                                                               W["hg_out_norm"][j:j + 1], dproj, S, f"hgrn_bwd_{i}")
            G["ab_w_in"][j] = matmul(s["hn"], dproj, "tn", f"d_ab_in_{i}", out_dtype=BF16)
            w_in, width = "ab_w_in", dproj.shape[1] // 2
        else:
            do = mm(dh, "c_w_out", j, "nt", f"d_o_{i}")
            G["c_w_out"][j] = matmul(s["o"], dh, "tn", f"d_c_out_{i}", out_dtype=BF16)
            ex, where = scattering(f"swa_bwd_{i}")
            dqkv, dbias, dsink, sent = _swa_bwd(s["qkn"], s["proj"], bias, W["sinks"][j], do, S, f"swa_bwd_{i}", ex)
            _StepExchanges.adopt(where, sent)
            dbias_total = dbias if dbias_total is None else dbias_total + dbias
            G["sinks"][j] = dsink[:, 0]
            dproj, dgain = _headnorm_bwd(s["proj"], qk_gain[j:j + 1], dqkv, f"qk_norm_bwd_{i}")
            G["q_norm"][j] = dgain[0, :SW_Q_COLS].reshape(SW_HEADS, SW_DIM).sum(axis=0)
            G["k_norm"][j] = dgain[0, SW_Q_COLS:].reshape(SW_KV, SW_DIM).sum(axis=0)
            G["c_w_in"][j] = matmul(s["hn"], dproj, "tn", f"d_c_in_{i}", out_dtype=BF16)
            w_in, width = "c_w_in", dproj.shape[1]
        dh, G["mix_norm"][i] = mm(dproj, w_in, j, "nt", f"d_hn_{i}", tiles=(rt, D_MODEL, width),
                                  norm_bwd=(s["h0"], W["mix_norm"][i:i + 1], dh))

    grads = {k: G[k] for k, _ in SHARDED}
    for k in ("q_norm", "k_norm", "sinks"):
        grads[k] = jnp.stack(G[k])
    for k in ("mix_norm", "ffn_norm", "ple_norm", "ffn_conv_b", "hg_out_norm"):
        grads[k] = jnp.concatenate(G[k], axis=0)
    grads["hg_lb_logits"] = _lower_bound_bwd(W["hg_lb_logits"], jnp.concatenate(G["lb"], axis=0), "lower_bound_bwd")
    grads["rel_bias"] = _bias_reduce(dbias_total, "bias_reduce")[:, 0, :N_BUCKETS].T
    return loss, dh, grads


def kernel(x, p, mix_norm, ab_w_in, hg_lb_logits, hg_out_norm, ab_w_out, c_w_in, q_norm, k_norm, sinks, rel_bias, c_w_out, ffn_norm, ffn_up, ffn_conv, ffn_conv_b, ffn_down, ple_norm, ple_gate, ple_proj, loss_target, m_mix_norm, m_ab_w_in, m_hg_lb_logits, m_hg_out_norm, m_ab_w_out, m_c_w_in, m_q_norm, m_k_norm, m_sinks, m_rel_bias, m_c_w_out, m_ffn_norm, m_ffn_up, m_ffn_conv, m_ffn_conv_b, m_ffn_down, m_ple_norm, m_ple_gate, m_ple_proj, v_mix_norm, v_ab_w_in, v_hg_lb_logits, v_hg_out_norm, v_ab_w_out, v_c_w_in, v_q_norm, v_k_norm, v_sinks, v_rel_bias, v_c_w_out, v_ffn_norm, v_ffn_up, v_ffn_conv, v_ffn_conv_b, v_ffn_down, v_ple_norm, v_ple_gate, v_ple_proj):
    args = dict(locals())
    w = {k: args[k] for k in WEIGHTS}
    m = {k: args["m_" + k] for k in WEIGHTS}
    v = {k: args["v_" + k] for k in WEIGHTS}
    B, S, Dm = x.shape
    T = B * S
    names = [k for k, _ in SHARDED]

    W = {k: w[k] for k in SMALL}
    exchanges = _StepExchanges({k: w[k].astype(F32 if k == "ffn_conv" else BF16) for k in names}, W)
    first, where = exchanges.gather("gather_first")
    exchanges.adopt(where, _exchange_call(first, "gather_first"))

    loss, dx, grads = _forward_backward(x.reshape(T, Dm), p.reshape(p.shape[0], T, p.shape[-1]),
                                        loss_target.reshape(T, Dm), W, S, exchanges)
    loss = lax.psum(loss[0, 0], ("x", "y", "c"))

    stacks = [exchanges.stacks[k] for k in names]
    partial = [_sum4(st.reshape(4, -1, st.shape[-1]), f"sum_chips_{k}") for k, st in zip(names, stacks)]
    other = _swap_with_sibling(partial, "swap_core_sums")
    small_sum = _allreduce_small(_pack([grads[k] for k in SMALL]), "allreduce_small")

    out_g, out_d, out_m, out_v = {}, {}, {}, {}
    for k, mine, theirs in zip(names, partial, other):
        shp = w[k].shape
        r = [a.reshape(shp) for a in _adamw(w[k].reshape(mine.shape), m[k].reshape(mine.shape),
                                            v[k].reshape(mine.shape), [mine, theirs], f"adamw_{k}")]
        out_g[k], out_d[k], out_m[k], out_v[k] = r
    sm = _adamw(_pack([w[k] for k in SMALL]), _pack([m[k] for k in SMALL]), _pack([v[k] for k in SMALL]),
                [small_sum], "adamw_small")
    like = [w[k] for k in SMALL]
    for dst, packed in zip((out_g, out_d, out_m, out_v), sm):
        for k, a in zip(SMALL, _unpack(packed, like)):
            dst[k] = a

    return (loss, dx.reshape(B, S, Dm), *[out_g[k] for k in WEIGHTS], *[out_d[k] for k in WEIGHTS],
            *[out_m[k] for k in WEIGHTS], *[out_v[k] for k in WEIGHTS])
```

```python
import math

import numpy as np
import jax
import jax.numpy as jnp
from jax import lax
from jax.experimental import pallas as pl
from jax.experimental.pallas import tpu as pltpu

F32 = jnp.float32
BF16 = jnp.bfloat16
MESH = pl.DeviceIdType.MESH
ANY = pl.BlockSpec(memory_space=pl.ANY)

D_MODEL = 1024
EPS = 1e-6
SB_HEADS, SB_DIM = 8, 64
HG_HEADS, HG_DK = 4, 128
HG_CHUNK = 32
HG_STEP = 2
HG_TRIP = 8
SW_HEADS, SW_KV, SW_DIM, WINDOW = 16, 4, 64, 128
N_BUCKETS, MAX_DISTANCE = 32, 128
D_FF = 2816
ATT_BLOCK = 128
SB_QBLOCK = 256
LANES = 128
NEG = -1e30

ADAM_LR, ADAM_B1, ADAM_B2, ADAM_EPS, ADAM_WD, ADAM_STEP = 0.001, 0.9, 0.999, 1e-08, 0.01, 10

VMEM_LIMIT = 56 * 1024 * 1024


def _cparams(*sem):
    return pltpu.CompilerParams(dimension_semantics=sem, vmem_limit_bytes=VMEM_LIMIT)


def _pick(n, cap):
    if n <= cap:
        return n
    best = None
    for d in range(LANES, cap + 1, LANES):
        if n % d == 0:
            best = d
    assert best is not None, (n, cap)
    return best


def _dot(a, b, ca, cb):
    return lax.dot_general(a.astype(BF16), b.astype(BF16), (((ca,), (cb,)), ((), ())),
                           preferred_element_type=F32)


def _split(x, terms):
    parts = []
    for _ in range(terms):
        hi = x.astype(BF16)
        parts.append(hi)
        x = x - hi.astype(F32)
    return parts


def _dot_exact_l(x, m, terms=2):
    out = None
    for p in _split(x, terms):
        t = lax.dot_general(p, m, (((1,), (0,)), ((), ())), preferred_element_type=F32)
        out = t if out is None else out + t
    return out


def _dot_exact_r(m, x, terms=3, cm=1):
    out = None
    for p in _split(x, terms):
        t = lax.dot_general(m, p, (((cm,), (0,)), ((), ())), preferred_element_type=F32)
        out = t if out is None else out + t
    return out


def _sig(x):
    return 1.0 / (1.0 + jnp.exp(-x))


def _iota2(shape, dim):
    return lax.broadcasted_iota(jnp.int32, shape, dim)


def _operand_spec(arr, layer, blk, index):
    if arr.ndim == 2:
        return pl.BlockSpec(blk, index)
    if layer is not None:
        return pl.BlockSpec((None,) + blk, lambda i, j, k: (layer,) + index(i, j, k))
    per_half = arr.shape[2] // blk[1]

    def halves(i, j, k):
        r, c = index(i, j, k)
        return (c // per_half, r, c % per_half)

    return pl.BlockSpec((None,) + blk, halves)


def _matmul(a, b, mode, name, out_dtype=F32, res=None, a_layer=None, b_layer=None, tiles=None, exchange=None,
            norm_gain=None, norm_bwd=None):
    def dims(arr, layer):
        if arr.ndim == 2:
            return arr.shape
        return arr.shape[1:] if layer is not None else (arr.shape[1], 2 * arr.shape[2])

    (a0, a1), (b0, b1) = dims(a, a_layer), dims(b, b_layer)
    if mode == "nn":
        M, K, N = a0, a1, b1
    elif mode == "nt":
        M, K, N = a0, a1, b0
    else:
        K, M, N = a0, a1, b1
    cap_m, cap_n, cap_k = 1024, 1024, (1024 if mode == "tn" else 2048)
    tm, tn, tk = _pick(M, cap_m), _pick(N, cap_n), _pick(K, cap_k)
    if a.ndim == 3 and a_layer is None:
        if mode == "tn":
            tm = _pick(a.shape[2], cap_m)
        else:
            tk = _pick(a.shape[2], cap_k)
    if b.ndim == 3 and b_layer is None:
        if mode == "nt":
            tk = _pick(b.shape[2], cap_k)
        else:
            tn = _pick(b.shape[2], cap_n)
    if tiles is not None:
        tm, tn, tk = tiles
    assert M % tm == 0 and N % tn == 0 and K % tk == 0, (name, M, N, K, tm, tn, tk)
    nk = K // tk
    if mode == "tn":
        a_spec = _operand_spec(a, a_layer, (tk, tm), lambda i, j, k: (k, i))
    else:
        a_spec = _operand_spec(a, a_layer, (tm, tk), lambda i, j, k: (i, k))
    if mode == "nt":
        b_spec = _operand_spec(b, b_layer, (tn, tk), lambda i, j, k: (j, k))
    else:
        b_spec = _operand_spec(b, b_layer, (tk, tn), lambda i, j, k: (k, j))
    ca, cb = {"nn": (1, 0), "nt": (1, 1), "tn": (0, 0)}[mode]
    o_spec = pl.BlockSpec((tm, tn), lambda i, j, k: (i, j))

    grid = (M // tm, N // tn, nk)
    normed = norm_gain is not None
    through = norm_bwd is not None
    assert not (normed or through) or tn == N, "the norm needs whole rows in one tile"
    assert not (through and (normed or res is not None))
    v_spec = pl.BlockSpec((1, tn), lambda i, j, k: (0, j))
    if through:
        x_in, gain_in, dres_in = norm_bwd
        extra_in, extra_specs = [x_in, gain_in, dres_in], [o_spec, v_spec, o_spec]
        out_specs = [o_spec, v_spec]
        out_shape = [jax.ShapeDtypeStruct((M, N), F32), jax.ShapeDtypeStruct((1, N), F32)]
    else:
        extra_in = ([] if res is None else [res]) + ([norm_gain] if normed else [])
        extra_specs = ([] if res is None else [o_spec]) + ([v_spec] if normed else [])
        out_specs = [o_spec] * (1 + normed)
        out_shape = ([jax.ShapeDtypeStruct((M, N), out_dtype)]
                     + ([jax.ShapeDtypeStruct((M, N), BF16)] if normed else []))
    n_in, n_out = 2 + len(extra_in), len(out_shape)

    def body(*refs):
        own, carried = _carried(exchange, refs, n_in, n_out, 1)
        _start_carried(exchange, carried, grid)
        a_ref, b_ref = own[:2]
        extras, outs, acc = own[2:n_in], own[n_in:n_in + n_out], own[-1]
        i, k = pl.program_id(0), pl.program_id(2)

        @pl.when(k == 0)
        def _():
            acc[...] = jnp.zeros_like(acc)

        acc[...] += _dot(a_ref[...], b_ref[...], ca, cb)

        @pl.when(k == nk - 1)
        def _():
            r = acc[...]
            if through:
                x, dy = extras[0][...], r
                scale = lax.rsqrt(jnp.mean(x * x, axis=1, keepdims=True) + EPS)
                gdy = dy * extras[1][...]
                m = jnp.mean(x * gdy, axis=1, keepdims=True)
                outs[0][...] = extras[2][...] + scale * gdy - x * (scale * scale * scale * m)
                part = jnp.sum(dy * x * scale, axis=0, keepdims=True)

                @pl.when(i == 0)
                def _():
                    outs[1][...] = part

                @pl.when(i > 0)
                def _():
                    outs[1][...] += part
            else:
                if res is not None:
                    r = r + extras[0][...]
                outs[0][...] = r.astype(out_dtype)
                if normed:
                    scale = lax.rsqrt(jnp.mean(r * r, axis=1, keepdims=True) + EPS)
                    outs[1][...] = (r * scale * extras[-1][...]).astype(BF16)

        _wait_carried(exchange, carried, grid)

    ins = [a, b] + extra_in
    in_specs = [a_spec, b_spec] + extra_specs
    scratch = [pltpu.VMEM((tm, tn), F32)]
    if exchange is None:
        sem = ("arbitrary" if through else "parallel", "parallel", "arbitrary")
        outs = pl.pallas_call(
            body, name=name, grid=grid, in_specs=in_specs, out_specs=out_specs, out_shape=out_shape,
            scratch_shapes=scratch, compiler_params=_cparams(*sem))(*ins)
        targets = ()
    else:
        outs, targets = _carrier_call(body, name, grid, in_specs, out_specs, out_shape, scratch, {}, ins, exchange)
    return (tuple(outs) if n_out > 1 else outs[0]), targets


ROW_TILE = 512
HEAD_ROWS = 2048


def _row_spec(width):
    return pl.BlockSpec((ROW_TILE, width), lambda i: (i, 0))


def _vec_spec(width):
    return pl.BlockSpec((1, width), lambda i: (0, 0))


def _rmsnorm_fwd(h, g, name):
    T, Dm = h.shape

    def body(h_ref, g_ref, o_ref):
        x = h_ref[...]
        r = lax.rsqrt(jnp.mean(x * x, axis=1, keepdims=True) + EPS)
        o_ref[...] = (x * r * g_ref[...]).astype(BF16)

    return pl.pallas_call(
        body, name=name, grid=(T // ROW_TILE,), in_specs=[_row_spec(Dm), _vec_spec(Dm)],
        out_specs=_row_spec(Dm), out_shape=jax.ShapeDtypeStruct((T, Dm), BF16),
        compiler_params=_cparams("parallel"))(h, g)


def _ple_fwd(h, z, pp, gain, name):
    T, Dm = h.shape
    normed = gain is not None

    def body(*refs):
        h_ref, z_ref, p_ref = refs[:3]
        y = h_ref[...] + _sig(z_ref[...]) * p_ref[...]
        refs[3 + normed][...] = y
        if normed:
            r = lax.rsqrt(jnp.mean(y * y, axis=1, keepdims=True) + EPS)
            refs[5][...] = (y * r * refs[3][...]).astype(BF16)

    outs = pl.pallas_call(
        body, name=name, grid=(T // ROW_TILE,), in_specs=[_row_spec(Dm)] * 3 + [_vec_spec(Dm)] * normed,
        out_specs=[_row_spec(Dm)] * (1 + normed),
        out_shape=[jax.ShapeDtypeStruct((T, Dm), F32)] + [jax.ShapeDtypeStruct((T, Dm), BF16)] * normed,
        compiler_params=_cparams("parallel"))(*([h, z, pp] + [gain] * normed))
    return outs if normed else (outs[0], None)


def _ple_bwd(dh, z, pp, name):
    T, Dm = dh.shape

    def body(dh_ref, z_ref, p_ref, dz_ref, dp_ref):
        s = _sig(z_ref[...])
        d = dh_ref[...]
        dz_ref[...] = d * p_ref[...] * s * (1.0 - s)
        dp_ref[...] = d * s

    return pl.pallas_call(
        body, name=name, grid=(T // ROW_TILE,), in_specs=[_row_spec(Dm)] * 3, out_specs=[_row_spec(Dm)] * 2,
        out_shape=[jax.ShapeDtypeStruct((T, Dm), F32)] * 2, compiler_params=_cparams("parallel"))(dh, z, pp)


def _loss_fwd_bwd(y, target, name):
    T, Dm = y.shape

    def body(y_ref, t_ref, l_ref, d_ref):
        i = pl.program_id(0)
        e = y_ref[...] - t_ref[...]
        d_ref[...] = e * (1.0 / Dm)
        part = jnp.full((8, LANES), 0.5 / Dm, F32) * jnp.sum(e * e)

        @pl.when(i == 0)
        def _():
            l_ref[...] = part

        @pl.when(i > 0)
        def _():
            l_ref[...] += part

    return pl.pallas_call(
        body, name=name, grid=(T // ROW_TILE,), in_specs=[_row_spec(Dm)] * 2,
        out_specs=[pl.BlockSpec((8, LANES), lambda i: (0, 0)), _row_spec(Dm)],
        out_shape=[jax.ShapeDtypeStruct((8, LANES), F32), jax.ShapeDtypeStruct((T, Dm), F32)],
        compiler_params=_cparams("arbitrary"))(y, target)


def _head_mean_matrix():
    r = _iota2((LANES, LANES), 0) >= SW_DIM
    c = _iota2((LANES, LANES), 1) >= SW_DIM
    return jnp.where(r == c, 1.0 / SW_DIM, 0.0).astype(BF16)


def _headnorm_fwd(x, g_lane, name):
    T = x.shape[0]
    C = g_lane.shape[1]

    def body(x_ref, g_ref, y_ref):
        xv = x_ref[...]
        ms = _dot_exact_l(xv * xv, _head_mean_matrix())
        y_ref[...] = xv * lax.rsqrt(ms + EPS) * g_ref[...]

    rows = min(T, HEAD_ROWS)
    spec = pl.BlockSpec((rows, LANES), lambda j, i: (i, j))
    return pl.pallas_call(
        body, name=name, grid=(C // LANES, T // rows),
        in_specs=[spec, pl.BlockSpec((1, LANES), lambda j, i: (0, j))], out_specs=spec,
        out_shape=jax.ShapeDtypeStruct((T, C), F32), compiler_params=_cparams("parallel", "parallel"))(x, g_lane)


def _headnorm_bwd(x, g_lane, dy_full, name):
    T = x.shape[0]
    C = g_lane.shape[1]

    def body(x_ref, g_ref, dy_ref, dx_ref, dg_ref):
        i = pl.program_id(1)
        xv = x_ref[...]
        dy = dy_ref[...]
        bd = _head_mean_matrix()
        r = lax.rsqrt(_dot_exact_l(xv * xv, bd) + EPS)
        gdy = dy * g_ref[...]
        m = _dot_exact_l(xv * gdy, bd)
        dx_ref[...] = r * gdy - xv * (r * r * r * m)
        part = jnp.sum(dy * xv * r, axis=0, keepdims=True)

        @pl.when(i == 0)
        def _():
            dg_ref[...] = part

        @pl.when(i > 0)
        def _():
            dg_ref[...] += part

    rows = min(T, HEAD_ROWS)
    spec = pl.BlockSpec((rows, LANES), lambda j, i: (i, j))
    vspec = pl.BlockSpec((1, LANES), lambda j, i: (0, j))
    return pl.pallas_call(
        body, name=name, grid=(C // LANES, T // rows), in_specs=[spec, vspec, spec],
        out_specs=[spec, vspec],
        out_shape=[jax.ShapeDtypeStruct(dy_full.shape, F32), jax.ShapeDtypeStruct((1, C), F32)],
        input_output_aliases={2: 0}, compiler_params=_cparams("parallel", "arbitrary"))(x, g_lane, dy_full)


CONV_TILE = 128


def _shift_down(x, k, rows):
    return jnp.where(rows >= k, pltpu.roll(x, k, 0), 0.0)


def _shift_up(x, k, rows):
    n = x.shape[0]
    return jnp.where(rows < n - k, pltpu.roll(x, n - k, 0), 0.0)


def _conv3(u, w_ref, b_ref, rows):
    return (w_ref[2:3, :] * u + w_ref[1:2, :] * _shift_down(u, 1, rows) + w_ref[0:1, :] * _shift_down(u, 2, rows)
            + b_ref[...])


def _convglu_fwd(u, cw, cb, S, name):
    T = u.shape[0]
    nf = D_FF // CONV_TILE

    def body(ug_ref, uu_ref, wg_ref, wu_ref, bg_ref, bu_ref, a_ref):
        rows = _iota2((S, CONV_TILE), 0)
        yg = _conv3(ug_ref[...], wg_ref, bg_ref, rows)
        yu = _conv3(uu_ref[...], wu_ref, bu_ref, rows)
        a_ref[...] = (yg * _sig(yg) * yu).astype(BF16)

    def blk(rows, off):
        return pl.BlockSpec((rows, CONV_TILE), (lambda b, j: (b, j + off)) if rows == S else (lambda b, j: (0, j + off)))

    return pl.pallas_call(
        body, name=name, grid=(T // S, nf),
        in_specs=[blk(S, 0), blk(S, nf), blk(3, 0), blk(3, nf), blk(1, 0), blk(1, nf)],
        out_specs=blk(S, 0), out_shape=jax.ShapeDtypeStruct((T, D_FF), BF16),
        compiler_params=_cparams("parallel", "parallel"))(u, u, cw, cw, cb, cb)


def _convglu_bwd(u, da, cw, cb, S, name, exchange=None):
    T = u.shape[0]
    nf = D_FF // CONV_TILE

    grid = (nf, T // S)

    def body(*refs):
        (ug_ref, uu_ref, da_ref, wg_ref, wu_ref, bg_ref, bu_ref, du_ref, dw_ref, db_ref), carried = _carried(
            exchange, refs, 7, 3, 0)
        _start_carried(exchange, carried, grid)
        b = pl.program_id(1)
        rows = _iota2((S, CONV_TILE), 0)
        ug, uu = ug_ref[...], uu_ref[...]
        yg = _conv3(ug, wg_ref, bg_ref, rows)
        yu = _conv3(uu, wu_ref, bu_ref, rows)
        s = _sig(yg)
        da_v = da_ref[...]
        for half, (uv, w_ref, dy) in enumerate(((ug, wg_ref, da_v * yu * (s * (1.0 + yg * (1.0 - s)))),
                                                (uu, wu_ref, da_v * yg * s))):
            up1, up2 = _shift_up(dy, 1, rows), _shift_up(dy, 2, rows)
            du_ref[half] = w_ref[2:3, :] * dy + w_ref[1:2, :] * up1 + w_ref[0:1, :] * up2
            dws = [jnp.sum(up2 * uv, axis=0, keepdims=True), jnp.sum(up1 * uv, axis=0, keepdims=True),
                   jnp.sum(dy * uv, axis=0, keepdims=True)]
            dbv = jnp.sum(dy, axis=0, keepdims=True)

            @pl.when(b == 0)
            def _():
                for k in range(3):
                    dw_ref[half, k:k + 1, :] = dws[k]
                db_ref[half] = dbv

            @pl.when(b > 0)
            def _():
                for k in range(3):
                    dw_ref[half, k:k + 1, :] += dws[k]
                db_ref[half] += dbv

        _wait_carried(exchange, carried, grid)

    def blk(rows, off):
        return pl.BlockSpec((rows, CONV_TILE), (lambda j, b: (b, j + off)) if rows == S else (lambda j, b: (0, j + off)))

    def both(rows):
        return pl.BlockSpec((2, rows, CONV_TILE), (lambda j, b: (0, b, j)) if rows == S else (lambda j, b: (0, 0, j)))

    outs, targets = _carrier_call(
        body, name, grid, [blk(S, 0), blk(S, nf), blk(S, 0), blk(3, 0), blk(3, nf), blk(1, 0), blk(1, nf)],
        [both(S), both(3), both(1)],
        [jax.ShapeDtypeStruct((2, T, D_FF), F32), jax.ShapeDtypeStruct((2, 3, D_FF), F32),
         jax.ShapeDtypeStruct((2, 1, D_FF), F32)], [], {}, (u, u, da, cw, cw, cb, cb), exchange)
    return (*outs, targets)


def _sb_scores(qb, kblk, on_diag_mask):
    z = _dot(qb, kblk, 1, 1) * (SB_DIM ** -0.5)
    l1 = jnp.log(1.0 + jnp.exp(-jnp.abs(z)))
    ls = jnp.minimum(z, 0.0) - l1
    lk = jnp.where(on_diag_mask, ls - z, 0.0)
    return ls, lk


def _sb_fwd(proj, S, name, exchange=None):
    T = proj.shape[0]
    BQ, BK = SB_QBLOCK, ATT_BLOCK
    unroll = BQ // BK
    nq = S // BQ
    nhp = SB_HEADS // 2
    heads = [slice(h * SB_DIM, (h + 1) * SB_DIM) for h in range(2)]

    grid = (T // S, nhp)

    def body(*refs):
        (q_ref, k_ref, v_ref, o_ref, tot_ref), carried = _carried(exchange, refs, 3, 2, 0)
        _start_carried(exchange, carried, grid)
        ahead = _iota2((BQ, BK), 1) - _iota2((BQ, BK), 0)
        upper = (_iota2((BK, BK), 0) > _iota2((BK, BK), 1)).astype(BF16)

        def qloop(iq, carry):
            q0 = pl.multiple_of(iq * BQ, BQ)
            rows = pl.ds(q0, BQ)
            qbs = [q_ref[rows, sl] for sl in heads]
            nkb = (iq + 1) * (BQ // BK)

            def kloop(jj, kc):
                blocks = []
                for u in range(unroll):
                    k0 = pl.multiple_of((nkb - 1 - unroll * jj - u) * BK, BK)
                    blocks.append((pl.ds(k0, BK), ahead < q0 - k0))
                units = [(h, krows, mask) for krows, mask in blocks for h in range(2)]
                scores = [_sb_scores(qbs[h], k_ref[krows, heads[h]], mask) for h, krows, mask in units]
                laters = [_dot_exact_l(lk, upper) for _, lk in scores]
                runs = [kc[h][0] for h in range(2)]
                accs = [kc[h][1] for h in range(2)]
                for (h, krows, mask), (ls, lk), later in zip(units, scores, laters):
                    w = jnp.where(mask, jnp.exp(ls + later + runs[h]), 0.0)
                    accs[h] = accs[h] + _dot(w, v_ref[krows, heads[h]], 1, 0)
                    runs[h] = runs[h] + jnp.sum(lk, axis=1, keepdims=True)
                return tuple((runs[h], accs[h]) for h in range(2))

            init = (jnp.zeros((BQ, 1), F32), jnp.zeros((BQ, SB_DIM), F32))
            res = lax.fori_loop(0, nkb // unroll, kloop, (init, init))
            for h, (sl, (run, acc)) in enumerate(zip(heads, res)):
                o_ref[rows, sl] = acc
                tot_ref[rows, h:h + 1] = run
            return carry

        lax.fori_loop(0, nq, qloop, 0)
        _wait_carried(exchange, carried, grid)

    def spec(off):
        return pl.BlockSpec((S, LANES), lambda b, hp: (b, hp + off))

    (cat, tot), targets = _carrier_call(
        body, name, grid, [spec(0), spec(nhp), spec(2 * nhp)],
        [spec(0), pl.BlockSpec((None, S, 2), lambda b, hp: (hp, b, 0))],
        [jax.ShapeDtypeStruct((T, 2 * SB_HEADS * SB_DIM), F32), jax.ShapeDtypeStruct((nhp, T, 2), F32)],
        [], {}, (proj, proj, proj), exchange)
    return cat, tot, targets


def _sb_bwd(proj, tot, dcat, S, name, exchange=None):
    T, width = proj.shape
    BQ, BK = SB_QBLOCK, ATT_BLOCK
    unroll = BQ // BK
    nq = S // BQ
    nhp = SB_HEADS // 2
    scale = SB_DIM ** -0.5
    heads = [slice(h * SB_DIM, (h + 1) * SB_DIM) for h in range(2)]

    grid = (T // S, nhp)

    def body(*refs):
        (q_ref, k_ref, v_ref, tot_ref, do_ref, dp_hbm, dq_s, dk_s, dv_s, sems), carried = _carried(
            exchange, refs, 5, 1, 4)
        _start_carried(exchange, carried, grid)
        b, hp = pl.program_id(0), pl.program_id(1)
        ahead = _iota2((BQ, BK), 1) - _iota2((BQ, BK), 0)
        r, c = _iota2((BK, BK), 0), _iota2((BK, BK), 1)
        upto = (r <= c).astype(BF16)
        earlier = (r < c).astype(BF16)
        dk_s[...] = jnp.zeros_like(dk_s)
        dv_s[...] = jnp.zeros_like(dv_s)

        def qloop(iq, carry):
            q0 = pl.multiple_of(iq * BQ, BQ)
            rows = pl.ds(q0, BQ)
            qbs = [q_ref[rows, sl] for sl in heads]
            dobs = [do_ref[rows, sl] for sl in heads]
            totals = [tot_ref[rows, h:h + 1] for h in range(2)]

            def kloop(m, kc):
                blocks = []
                for u in range(unroll):
                    k0 = pl.multiple_of((unroll * m + u) * BK, BK)
                    blocks.append((pl.ds(k0, BK), ahead < q0 - k0))
                units = [(h, krows, mask) for krows, mask in blocks for h in range(2)]
                kblks = [k_ref[krows, heads[h]] for h, krows, _ in units]
                scores = [_sb_scores(qbs[h], kblk, mask) for (h, _, mask), kblk in zip(units, kblks)]
                prefixes = [_dot_exact_l(lk, upto) for _, lk in scores]
                dws = [_dot(dobs[h], v_ref[krows, heads[h]], 1, 1) for h, krows, _ in units]
                runs = [kc[h][0] for h in range(2)]
                gruns = [kc[h][1] for h in range(2)]
                dqs = [kc[h][2] for h in range(2)]
                ws, gs = [], []
                for (h, _, mask), (ls, lk), prefix, dw in zip(units, scores, prefixes, dws):
                    w = jnp.where(mask, jnp.exp(ls + (totals[h] - (prefix + runs[h]))), 0.0)
                    runs[h] = runs[h] + jnp.sum(lk, axis=1, keepdims=True)
                    ws.append(w)
                    gs.append(w * dw)
                gprefixes = [_dot_exact_l(g, earlier) for g in gs]
                dzs = []
                for (h, _, mask), (ls, _), g, gprefix in zip(units, scores, gs, gprefixes):
                    beta = jnp.exp(ls)
                    dzs.append(jnp.where(mask, g * (1.0 - beta) - beta * (gprefix + gruns[h]), 0.0) * scale)
                    gruns[h] = gruns[h] + jnp.sum(g, axis=1, keepdims=True)
                for (h, krows, _), kblk, w, dz in zip(units, kblks, ws, dzs):
                    dv_s[krows, heads[h]] += _dot(w, dobs[h], 0, 0)
                    dk_s[krows, heads[h]] += _dot(dz, qbs[h], 0, 0)
                    dqs[h] = dqs[h] + _dot(dz, kblk, 1, 0)
                return tuple((runs[h], gruns[h], dqs[h]) for h in range(2))

            zero = jnp.zeros((BQ, 1), F32)
            init = (zero, zero, jnp.zeros((BQ, SB_DIM), F32))
            res = lax.fori_loop(0, (iq + 1) * (BQ // BK) // unroll, kloop, (init, init))
            for sl, (_, _, dq) in zip(heads, res):
                dq_s[rows, sl] = dq
            return carry

        lax.fori_loop(0, nq, qloop, 0)
        r0 = pl.multiple_of(b * S, S)
        copies = []
        for n, buf in enumerate((dq_s, dk_s, dv_s)):
            c0 = pl.multiple_of((hp + n * nhp) * LANES, LANES)
            copies.append(pltpu.make_async_copy(buf, dp_hbm.at[pl.ds(r0, S), pl.ds(c0, LANES)], sems.at[n]))
        for cp in copies:
            cp.start()
        for cp in copies:
            cp.wait()
        _wait_carried(exchange, carried, grid)

    def spec(off):
        return pl.BlockSpec((S, LANES), lambda b, hp: (b, hp + off))

    (dproj,), targets = _carrier_call(
        body, name, grid,
        [spec(0), spec(nhp), spec(2 * nhp), pl.BlockSpec((None, S, 2), lambda b, hp: (hp, b, 0)), spec(0)],
        [ANY], [jax.ShapeDtypeStruct((T, width), F32)],
        [pltpu.VMEM((S, LANES), F32)] * 3 + [pltpu.SemaphoreType.DMA((3,))], {}, (proj, proj, proj, tot, dcat),
        exchange)
    return dproj, targets


HG_COL0 = 3 * SB_HEADS * SB_DIM // LANES


def _hg_gates(q, fp, lbv):
    sg = _sig(fp)
    f = lbv + (1.0 - lbv) * sg
    kk = (1.0 - lbv) * _sig(-fp)
    sq = _sig(q)
    return sg, f, kk, sq


def _hg_chunk(qs, kk, lf, incl):
    C = HG_CHUNK
    b = _dot_exact_r(incl, lf)
    bl = b[C - 1:C, :]
    bm = b[C // 2 - 1:C // 2, :]
    e_t = jnp.exp(b - bm)
    e_s = jnp.exp(bm - b)
    e_i = jnp.exp(b)
    e_e = jnp.exp(bl - b)
    return bl, e_t, e_s, e_i, e_e


def _hgrn_fwd(proj, cat, lb, hgn, S, name, exchange=None):
    T = proj.shape[0]
    B = T // S
    C = HG_CHUNK
    NC = S // C

    grid = (B, HG_HEADS // HG_STEP)

    def body(*refs):
        (q_ref, f_ref, i_ref, g_ref, lb_ref, hgn_ref, _, ob_ref, oraw_ref, st_ref, state), carried = _carried(
            exchange, refs, 7, 3, 1)
        _start_carried(exchange, carried, grid)
        state[...] = jnp.zeros_like(state)
        row, col = _iota2((C, C), 0), _iota2((C, C), 1)
        causal = row >= col
        incl = causal.astype(BF16)

        def trip(m, carry):
            units = []
            for u in range(HG_TRIP):
                c = m * HG_TRIP + u
                rows = pl.ds(pl.multiple_of(c * C, C), C)
                units += [(c, hh, rows, slice(hh * LANES, (hh + 1) * LANES)) for hh in range(HG_STEP)]
            qs_, kks, lfs = [], [], []
            for _, _, rows, hs in units:
                q = q_ref[rows, hs]
                _, f, kk, sq = _hg_gates(q, f_ref[rows, hs], lb_ref[:, hs])
                qs_.append(q * sq)
                kks.append(kk)
                lfs.append(jnp.log(f))
            decays = [_hg_chunk(None, None, lf, incl) for lf in lfs]
            ps = [jnp.where(causal, _dot(qs * e_t, kk * e_s, 1, 1), 0.0)
                  for qs, kk, (_, e_t, e_s, _, _) in zip(qs_, kks, decays)]
            outs = []
            for (c, hh, rows, hs), qs, kk, (bl, _, _, e_i, e_e), p in zip(units, qs_, kks, decays, ps):
                iv = i_ref[rows, hs]
                st = state[hh]
                st_ref[0, hh, c] = st
                outs.append(_dot(qs * e_i, st, 1, 1) + _dot(p, iv, 1, 0))
                state[hh] = st * jnp.exp(bl) + _dot(iv, kk * e_e, 0, 0)
            for (_, _, rows, hs), o in zip(units, outs):
                gv = g_ref[rows, hs]
                oraw_ref[rows, hs] = o
                r = lax.rsqrt(jnp.mean(o * o, axis=1, keepdims=True) + EPS)
                ob_ref[rows, hs] = o * r * hgn_ref[...] * (gv * _sig(gv))
            return carry

        lax.fori_loop(0, NC // HG_TRIP, trip, 0)
        _wait_carried(exchange, carried, grid)

    width = HG_STEP * LANES
    col0 = HG_COL0 * LANES // width
    nstep = HG_HEADS // HG_STEP

    def spec(off):
        return pl.BlockSpec((S, width), lambda b, h: (b, h + off))

    outs, targets = _carrier_call(
        body, name, grid,
        [spec(col0), spec(col0 + nstep), spec(col0 + 2 * nstep), spec(col0 + 3 * nstep),
         pl.BlockSpec((1, width), lambda b, h: (0, h)), pl.BlockSpec((1, LANES), lambda b, h: (0, 0)), ANY],
        [spec(nstep), spec(0), pl.BlockSpec((1, HG_STEP, NC, LANES, LANES), lambda b, h: (b, h, 0, 0, 0))],
        [jax.ShapeDtypeStruct(cat.shape, F32), jax.ShapeDtypeStruct((T, HG_HEADS * LANES), F32),
         jax.ShapeDtypeStruct((B, HG_HEADS, NC, LANES, LANES), F32)],
        [pltpu.VMEM((HG_STEP, LANES, LANES), F32)], {6: 0}, (proj, proj, proj, proj, lb, hgn, cat), exchange)
    return (*outs, targets)


def _hgrn_bwd(proj, oraw, dcat, states, lb, hgn, dproj, S, name):
    T = proj.shape[0]
    B = T // S
    C = HG_CHUNK
    NC = S // C

    def body(q_ref, f_ref, i_ref, g_ref, oraw_ref, dy_ref, st_ref, lb_ref, hgn_ref, dp_in,
             dp_hbm, dlb_ref, dhgn_ref, dstate, dq_s, df_s, di_s, dg_s, sems):
        del dp_in
        h, b = pl.program_id(0), pl.program_id(1)
        row, col = _iota2((C, C), 0), _iota2((C, C), 1)
        causal = row >= col
        incl = causal.astype(BF16)
        last_row = _iota2((C, LANES), 0) == C - 1
        hg = hgn_ref[...]
        dstate[...] = jnp.zeros_like(dstate)

        @pl.when(b == 0)
        def _():
            dlb_ref[...] = jnp.zeros_like(dlb_ref)

        @pl.when(jnp.logical_and(b == 0, h == 0))
        def _():
            dhgn_ref[...] = jnp.zeros_like(dhgn_ref)

        def trip(m, carry):
            units = []
            for u in range(HG_TRIP):
                c = NC - 1 - (m * HG_TRIP + u)
                rows = pl.ds(pl.multiple_of(c * C, C), C)
                units += [(c, hh, rows, slice(hh * LANES, (hh + 1) * LANES)) for hh in range(HG_STEP)]
            dos = []
            dhgn = jnp.zeros((1, LANES), F32)
            for _, _, rows, hs in units:
                gv, o, dy = g_ref[rows, hs], oraw_ref[rows, hs], dy_ref[rows, hs]
                r = lax.rsqrt(jnp.mean(o * o, axis=1, keepdims=True) + EPS)
                on = o * r
                sgv = _sig(gv)
                silu_g = gv * sgv
                dg_s[rows, hs] = dy * on * hg * (sgv * (1.0 + gv * (1.0 - sgv)))
                dhgn = dhgn + jnp.sum(dy * on * silu_g, axis=0, keepdims=True)
                dn = dy * hg * silu_g
                dos.append(r * dn - o * (r * r * r * jnp.mean(o * dn, axis=1, keepdims=True)))
            dhgn_ref[...] += dhgn
            pre = []
            for (_, _, rows, hs), do in zip(units, dos):
                q, iv = q_ref[rows, hs], i_ref[rows, hs]
                sg, f, kk, sq = _hg_gates(q, f_ref[rows, hs], lb_ref[:, hs])
                pre.append((q, iv, sg, f, kk, sq, q * sq))
            decays = [_hg_chunk(None, None, jnp.log(f), incl) for _, _, _, f, _, _, _ in pre]
            prods = [(qs * e_t, kk * e_s, qs * e_i, kk * e_e)
                     for (_, _, _, _, kk, _, qs), (_, e_t, e_s, e_i, e_e) in zip(pre, decays)]
            ps = [jnp.where(causal, _dot(qd, kd, 1, 1), 0.0) for qd, kd, _, _ in prods]
            dps = [jnp.where(causal, _dot(do, iv, 1, 1), 0.0) for do, (_, iv, _, _, _, _, _) in zip(dos, pre)]
            dqds = [_dot(dp, kd, 1, 0) for dp, (_, kd, _, _) in zip(dps, prods)]
            dkds = [_dot(dp, qd, 0, 0) for dp, (qd, _, _, _) in zip(dps, prods)]
            pdos = [_dot(p, do, 0, 0) for p, do in zip(ps, dos)]
            chain = []
            for (c, hh, rows, hs), do, (_, iv, _, _, _, _, _), (bl, _, _, _, _), (_, _, qi, ke), pdo in zip(
                    units, dos, pre, decays, prods, pdos):
                st = st_ref[0, hh, c]
                dst = dstate[hh]
                ebl = jnp.exp(bl)
                dqi = _dot(do, st, 1, 0)
                di_s[rows, hs] = pdo + _dot(ke, dst, 1, 1)
                dke = _dot(iv, dst, 1, 0)
                dbl = jnp.sum(st * dst, axis=0, keepdims=True) * ebl + jnp.sum(dke * ke, axis=0, keepdims=True)
                dstate[hh] = _dot(do, qi, 0, 0) + dst * ebl
                chain.append((dqi, dke, dbl))
            for (_, _, rows, hs), (q, _, sg, f, _, sq, _), (_, e_t, e_s, e_i, e_e), (qd, kd, qi, ke), dqd, dkd, (
                    dqi, dke, dbl) in zip(units, pre, decays, prods, dqds, dkds, chain):
                lbv = lb_ref[:, hs]
                db = dqd * qd - dkd * kd + dqi * qi - dke * ke + jnp.where(last_row, dbl, 0.0)
                dqs = dqd * e_t + dqi * e_i
                dkk = dkd * e_s + dke * e_e
                dlf = _dot_exact_r(incl, db, cm=0)
                oms = 1.0 - sg
                dfd = dlf / f
                df_s[rows, hs] = (dfd - dkk) * (1.0 - lbv) * sg * oms
                dq_s[rows, hs] = dqs * (sq * (1.0 + q * (1.0 - sq)))
                dlb_ref[:, hs] += jnp.sum((dfd - dkk) * oms, axis=0, keepdims=True)
            return carry

        lax.fori_loop(0, NC // HG_TRIP, trip, 0)
        r0 = pl.multiple_of(b * S, S)
        copies = []
        for n, buf in enumerate((dq_s, df_s, di_s, dg_s)):
            c0 = pl.multiple_of((col0 + n * nstep + h) * width, width)
            copies.append(pltpu.make_async_copy(buf, dp_hbm.at[pl.ds(r0, S), pl.ds(c0, width)], sems.at[n]))
        for cp in copies:
            cp.start()
        for cp in copies:
            cp.wait()

    width = HG_STEP * LANES
    col0 = HG_COL0 * LANES // width
    nstep = HG_HEADS // HG_STEP

    def spec(off):
        return pl.BlockSpec((S, width), lambda h, b: (b, h + off))

    return pl.pallas_call(
        body, name=name, grid=(nstep, B),
        in_specs=[spec(col0), spec(col0 + nstep), spec(col0 + 2 * nstep), spec(col0 + 3 * nstep), spec(0), spec(nstep),
                  pl.BlockSpec((1, HG_STEP, NC, LANES, LANES), lambda h, b: (b, h, 0, 0, 0)),
                  pl.BlockSpec((1, width), lambda h, b: (0, h)), pl.BlockSpec((1, LANES), lambda h, b: (0, 0)), ANY],
        out_specs=[ANY, pl.BlockSpec((1, width), lambda h, b: (0, h)), pl.BlockSpec((1, LANES), lambda h, b: (0, 0))],
        out_shape=[jax.ShapeDtypeStruct(dproj.shape, F32), jax.ShapeDtypeStruct((1, HG_HEADS * LANES), F32),
                   jax.ShapeDtypeStruct((1, LANES), F32)],
        scratch_shapes=[pltpu.VMEM((HG_STEP, LANES, LANES), F32)] + [pltpu.VMEM((S, width), F32)] * 4
        + [pltpu.SemaphoreType.DMA((4,))],
        input_output_aliases={9: 0},
        compiler_params=_cparams("arbitrary", "arbitrary"))(proj, proj, proj, proj, oraw, dcat, states, lb, hgn, dproj)


def _lower_bound_fwd(logits, name):
    assert logits.shape[0] == 2

    def body(l_ref, o_ref):
        l0, l1 = l_ref[0:1, :], l_ref[1:2, :]
        m = jnp.maximum(l0, l1)
        e0, e1 = jnp.exp(l0 - m), jnp.exp(l1 - m)
        o_ref[0:1, :] = jnp.zeros_like(l0)
        o_ref[1:2, :] = e1 / (e0 + e1)

    return pl.pallas_call(body, name=name, out_shape=jax.ShapeDtypeStruct(logits.shape, F32))(logits)


def _lower_bound_bwd(logits, dlb, name):
    def body(l_ref, d_ref, o_ref):
        l0, l1 = l_ref[0:1, :], l_ref[1:2, :]
        m = jnp.maximum(l0, l1)
        e0, e1 = jnp.exp(l0 - m), jnp.exp(l1 - m)
        s1 = e1 / (e0 + e1)
        t = s1 * (1.0 - s1) * d_ref[1:2, :]
        o_ref[0:1, :] = -t
        o_ref[1:2, :] = t

    return pl.pallas_call(body, name=name, out_shape=jax.ShapeDtypeStruct(logits.shape, F32))(logits, dlb)


def _bucket_thresholds():
    dist = np.arange(WINDOW)
    max_exact = N_BUCKETS // 2
    large = max_exact + (np.log(np.maximum(dist, max_exact) / max_exact) / math.log(MAX_DISTANCE / max_exact)
                         * (N_BUCKETS - max_exact)).astype(np.int32)
    bucket = np.where(dist < max_exact, dist, np.minimum(large, N_BUCKETS - 1))
    assert np.all(np.diff(bucket) >= 0)
    return [int(np.argmax(bucket >= k)) if np.any(bucket >= k) else 10 ** 6 for k in range(1, N_BUCKETS)]


def _band_bucket():
    dist = _iota2((WINDOW, 2 * WINDOW), 0) + WINDOW - _iota2((WINDOW, 2 * WINDOW), 1)
    bucket = jnp.zeros((WINDOW, 2 * WINDOW), jnp.int32)
    for thr in _bucket_thresholds():
        bucket = bucket + (dist >= thr).astype(jnp.int32)
    band = jnp.logical_and(dist >= 0, dist < WINDOW)
    return bucket, band


def _bias_build(rel_bias, name):
    def body(rb_ref, o_ref):
        h = pl.program_id(0)
        bucket, _ = _band_bucket()
        bias = jnp.zeros((WINDOW, 2 * WINDOW), F32)
        for k in range(N_BUCKETS):
            bias = jnp.where(bucket == k, rb_ref[k, h], bias)
        o_ref[0] = bias

    return pl.pallas_call(
        body, name=name, grid=(SW_HEADS,), in_specs=[pl.BlockSpec(memory_space=pltpu.SMEM)],
        out_specs=pl.BlockSpec((1, WINDOW, 2 * WINDOW), lambda h: (h, 0, 0)),
        out_shape=jax.ShapeDtypeStruct((SW_HEADS, WINDOW, 2 * WINDOW), F32),
        compiler_params=_cparams("parallel"))(rel_bias)


def _bias_reduce(dbias, name):
    def body(d_ref, o_ref):
        bucket, band = _band_bucket()
        d = jnp.where(band, d_ref[0], 0.0)
        lane = _iota2((1, LANES), 1)
        out = jnp.zeros((1, LANES), F32)
        for k in range(N_BUCKETS):
            out = jnp.where(lane == k, jnp.sum(jnp.where(bucket == k, d, 0.0)), out)
        o_ref[0] = out

    return pl.pallas_call(
        body, name=name, grid=(SW_HEADS,), in_specs=[pl.BlockSpec((1, WINDOW, 2 * WINDOW), lambda h: (h, 0, 0))],
        out_specs=pl.BlockSpec((1, 1, LANES), lambda h: (h, 0, 0)),
        out_shape=jax.ShapeDtypeStruct((SW_HEADS, 1, LANES), F32), compiler_params=_cparams("parallel"))(dbias)


SW_Q_COLS = SW_HEADS * SW_DIM
SW_K_BLOCK0 = SW_Q_COLS // LANES
SW_V_BLOCK0 = SW_K_BLOCK0 + SW_KV * SW_DIM // LANES
SW_STEP_HEADS = 8
SW_TRIP = 1


def _swa_logits(qb, kprev, kcur, bias_ref, hl, mprev, mcur):
    scale = SW_DIM ** -0.5
    lp = jnp.where(mprev, _dot(qb, kprev, 1, 1) * scale + bias_ref[hl, :, 0:WINDOW], NEG)
    lc = jnp.where(mcur, _dot(qb, kcur, 1, 1) * scale + bias_ref[hl, :, WINDOW:2 * WINDOW], NEG)
    return lp, lc


def _swa_softmax(lp, lc, sink):
    m = jnp.maximum(jnp.maximum(jnp.max(lp, axis=1, keepdims=True), jnp.max(lc, axis=1, keepdims=True)), sink)
    ep, ec = jnp.exp(lp - m), jnp.exp(lc - m)
    es = jnp.exp(sink - m)
    den = jnp.sum(ep, axis=1, keepdims=True) + jnp.sum(ec, axis=1, keepdims=True) + es
    return ep, ec, es, den


def _swa_block_heads(n, q_ref, k_ref, v_ref, bias_ref, sink_ref, kp, above, mcur):
    W = WINDOW
    rows = pl.ds(pl.multiple_of(n * W, W), W)
    prow = pl.ds(pl.multiple_of(jnp.maximum(n - 1, 0) * W, W), W)
    mprev = jnp.logical_and(above, n > 0)
    heads = []
    for kvh in range(2):
        ksl = slice(kvh * SW_DIM, (kvh + 1) * SW_DIM)
        kv = (k_ref[rows, ksl], k_ref[prow, ksl], v_ref[rows, ksl], v_ref[prow, ksl])
        for g in range(4):
            hl = kvh * 4 + g
            heads.append((hl, ksl, slice(hl * SW_DIM, (hl + 1) * SW_DIM), kv))
    qbs = [q_ref[rows, qsl] for _, _, qsl, _ in heads]
    logits = [_swa_logits(qb, kv[1], kv[0], bias_ref, hl, mprev, mcur) for qb, (hl, _, _, kv) in zip(qbs, heads)]
    soft = [_swa_softmax(lp, lc, sink_ref[kp * SW_STEP_HEADS + hl]) for (lp, lc), (hl, _, _, _) in zip(logits, heads)]
    return rows, prow, heads, qbs, soft


def _swa_fwd(qkn, proj, bias, sinks, S, name, exchange=None):
    T = qkn.shape[0]
    W = WINDOW
    nb = S // W

    grid = (T // S, 2)

    def body(*refs):
        (q_ref, k_ref, v_ref, bias_ref, sink_ref, o_ref), carried = _carried(exchange, refs, 5, 1, 0)
        _start_carried(exchange, carried, grid)
        kp = pl.program_id(1)
        row, col = _iota2((W, W), 0), _iota2((W, W), 1)
        mcur = col <= row
        above = col > row

        def blk(m, carry):
            staged = [_swa_block_heads(m * SW_TRIP + u, q_ref, k_ref, v_ref, bias_ref, sink_ref, kp, above, mcur)
                      for u in range(SW_TRIP)]
            for rows, _, heads, _, soft in staged:
                outs = [(_dot(ep, kv[3], 1, 0) + _dot(ec, kv[2], 1, 0)) / den
                        for (ep, ec, _, den), (_, _, _, kv) in zip(soft, heads)]
                for (_, _, qsl, _), o in zip(heads, outs):
                    o_ref[rows, qsl] = o
            return carry

        lax.fori_loop(0, nb // SW_TRIP, blk, 0)
        _wait_carried(exchange, carried, grid)

    (o,), targets = _carrier_call(
        body, name, grid,
        [pl.BlockSpec((S, 4 * LANES), lambda b, kp: (b, kp)),
         pl.BlockSpec((S, LANES), lambda b, kp: (b, SW_K_BLOCK0 + kp)),
         pl.BlockSpec((S, LANES), lambda b, kp: (b, SW_V_BLOCK0 + kp)),
         pl.BlockSpec((SW_STEP_HEADS, W, 2 * W), lambda b, kp: (kp, 0, 0)),
         pl.BlockSpec(memory_space=pltpu.SMEM)],
        [pl.BlockSpec((S, 4 * LANES), lambda b, kp: (b, kp))], [jax.ShapeDtypeStruct((T, SW_Q_COLS), F32)],
        [], {}, (qkn, qkn, proj, bias, sinks), exchange)
    return o, targets


def _swa_bwd(qkn, proj, bias, sinks, do, S, name, exchange=None):
    T, width = proj.shape
    W = WINDOW
    nb = S // W
    scale = SW_DIM ** -0.5

    grid = (2, T // S)

    def body(*refs):
        (q_ref, k_ref, v_ref, bias_ref, sink_ref, do_ref, dp_hbm, dbias_ref, dsink_ref,
         dq_s, dk_s, dv_s, sems), carried = _carried(exchange, refs, 6, 3, 4)
        _start_carried(exchange, carried, grid)
        kp, b = pl.program_id(0), pl.program_id(1)
        row, col = _iota2((W, W), 0), _iota2((W, W), 1)
        mcur = col <= row
        above = col > row
        dk_s[...] = jnp.zeros_like(dk_s)
        dv_s[...] = jnp.zeros_like(dv_s)

        @pl.when(b == 0)
        def _():
            dbias_ref[...] = jnp.zeros_like(dbias_ref)
            dsink_ref[...] = jnp.zeros_like(dsink_ref)

        def blk(m, carry):
            staged = [_swa_block_heads(m * SW_TRIP + u, q_ref, k_ref, v_ref, bias_ref, sink_ref, kp, above, mcur)
                      for u in range(SW_TRIP)]
            for rows, prow, heads, qbs, soft in staged:
                one_block(rows, prow, heads, qbs, soft)
            return carry

        def one_block(rows, prow, heads, qbs, soft):
            dobs = [do_ref[rows, qsl] for _, _, qsl, _ in heads]
            dps = [(_dot(dob, kv[3], 1, 1), _dot(dob, kv[2], 1, 1)) for dob, (_, _, _, kv) in zip(dobs, heads)]
            grads = []
            for (hl, _, _, _), (ep, ec, es, den), (dpp, dpc) in zip(heads, soft, dps):
                inv = 1.0 / den
                pp, pc = ep * inv, ec * inv
                total = jnp.sum(pp * dpp, axis=1, keepdims=True) + jnp.sum(pc * dpc, axis=1, keepdims=True)
                dlp = pp * (dpp - total)
                dlc = pc * (dpc - total)
                dsink_ref[hl:hl + 1, :] += jnp.zeros((1, LANES), F32) - jnp.sum(es * inv * total)
                dbias_ref[hl, :, 0:W] += dlp
                dbias_ref[hl, :, W:2 * W] += dlc
                grads.append((pp, pc, dlp, dlc))
            for (_, _, qsl, kv), (_, _, dlp, dlc) in zip(heads, grads):
                dq_s[rows, qsl] = (_dot(dlp, kv[1], 1, 0) + _dot(dlc, kv[0], 1, 0)) * scale
            for kvh in range(2):
                group = range(4 * kvh, 4 * kvh + 4)
                ksl = heads[4 * kvh][1]
                dk_s[prow, ksl] += sum(_dot(grads[i][2], qbs[i], 0, 0) for i in group) * scale
                dk_s[rows, ksl] += sum(_dot(grads[i][3], qbs[i], 0, 0) for i in group) * scale
                dv_s[prow, ksl] += sum(_dot(grads[i][0], dobs[i], 0, 0) for i in group)
                dv_s[rows, ksl] += sum(_dot(grads[i][1], dobs[i], 0, 0) for i in group)

        lax.fori_loop(0, nb // SW_TRIP, blk, 0)
        r0 = pl.multiple_of(b * S, S)
        cq =pl.multiple_of(kp * 4 * LANES, LANES)
        ck = pl.multiple_of((SW_K_BLOCK0 + kp) * LANES, LANES)
        cv = pl.multiple_of((SW_V_BLOCK0 + kp) * LANES, LANES)
        copies = [pltpu.make_async_copy(dq_s, dp_hbm.at[pl.ds(r0, S), pl.ds(cq, 4 * LANES)], sems.at[0]),
                  pltpu.make_async_copy(dk_s, dp_hbm.at[pl.ds(r0, S), pl.ds(ck, LANES)], sems.at[1]),
                  pltpu.make_async_copy(dv_s, dp_hbm.at[pl.ds(r0, S), pl.ds(cv, LANES)], sems.at[2])]
        for cp in copies:
            cp.start()
        for cp in copies:
            cp.wait()
        _wait_carried(exchange, carried, grid)

    qspec = pl.BlockSpec((S, 4 * LANES), lambda kp, b: (b, kp))
    outs, targets = _carrier_call(
        body, name, grid,
        [qspec, pl.BlockSpec((S, LANES), lambda kp, b: (b, SW_K_BLOCK0 + kp)),
         pl.BlockSpec((S, LANES), lambda kp, b: (b, SW_V_BLOCK0 + kp)),
         pl.BlockSpec((SW_STEP_HEADS, W, 2 * W), lambda kp, b: (kp, 0, 0)),
         pl.BlockSpec(memory_space=pltpu.SMEM), qspec],
        [ANY, pl.BlockSpec((SW_STEP_HEADS, W, 2 * W), lambda kp, b: (kp, 0, 0)),
         pl.BlockSpec((SW_STEP_HEADS, LANES), lambda kp, b: (kp, 0))],
        [jax.ShapeDtypeStruct((T, width), F32), jax.ShapeDtypeStruct((SW_HEADS, W, 2 * W), F32),
         jax.ShapeDtypeStruct((SW_HEADS, LANES), F32)],
        [pltpu.VMEM((S, 4 * LANES), F32), pltpu.VMEM((S, LANES), F32), pltpu.VMEM((S, LANES), F32),
         pltpu.SemaphoreType.DMA((3,))], {}, (qkn, qkn, proj, bias, sinks, do), exchange)
    return (*outs, targets)


CHIP_FLIPS = ((1, 0), (0, 1), (1, 1))


def _flip(v, f):
    return 1 - v if f else v


class _Exchange:
    def __init__(self, sources, targets, copies):
        self.sources, self.targets = list(sources), list(targets)
        self._copies = copies
        n = len(self.sources)
        self.scratch = [pltpu.SemaphoreType.DMA((n, 3)), pltpu.SemaphoreType.DMA((n, 3)),
                        pltpu.SemaphoreType.DMA((n,))]

    def _descriptors(self, srcs, dsts, sems, chip, peers):
        send, recv, loc = sems
        out = []
        for t, (local, remote) in enumerate(self._copies(srcs, dsts, chip)):
            out.append(pltpu.make_async_copy(local[0], local[1], loc.at[t]))
            for r, (src, dst) in enumerate(remote):
                out.append(pltpu.make_async_remote_copy(src, dst, send.at[t, r], recv.at[t, r],
                                                        device_id=peers[r], device_id_type=MESH))
        return out

    def start(self, srcs, dsts, sems):
        x, y, c = lax.axis_index("x"), lax.axis_index("y"), lax.axis_index("c")
        peers = [(_flip(x, fx), _flip(y, fy), c) for fx, fy in CHIP_FLIPS]
        for chip in range(4):
            @pl.when(2 * x + y == chip)
            def _():
                for cp in self._descriptors(srcs, dsts, sems, chip, peers):
                    cp.start()

    def wait(self, srcs, dsts, sems):
        me = (lax.axis_index("x"), lax.axis_index("y"), lax.axis_index("c"))
        for cp in self._descriptors(srcs, dsts, sems, 0, [me] * 3):
            cp.wait()

    def operands(self):
        return self.sources + self.targets

    def specs(self):
        ns, nt = len(self.sources), len(self.targets)
        return [ANY] * (ns + nt), [ANY] * nt, [jax.ShapeDtypeStruct(t.shape, t.dtype) for t in self.targets]

    def aliases(self, n_in, n_out):
        ns = len(self.sources)
        return {n_in + ns + i: n_out + i for i in range(len(self.targets))}

    def split(self, refs, n_in, n_out, n_scr):
        ns, nt = len(self.sources), len(self.targets)
        o0 = n_in + ns + nt
        s0 = o0 + n_out + nt
        own = list(refs[:n_in]) + list(refs[o0:o0 + n_out]) + list(refs[s0:s0 + n_scr])
        return own, (refs[n_in:n_in + ns], refs[o0 + n_out:o0 + n_out + nt], refs[s0 + n_scr:])


def _carried(exchange, refs, n_in, n_out, n_scr):
    if exchange is None:
        return list(refs), None
    return exchange.split(refs, n_in, n_out, n_scr)


def _grid_edge(grid, last):
    conds = [pl.program_id(d) == (n - 1 if last else 0) for d, n in enumerate(grid)]
    out = conds[0]
    for cnd in conds[1:]:
        out = jnp.logical_and(out, cnd)
    return out


def _start_carried(exchange, parts, grid):
    if parts is not None:
        @pl.when(_grid_edge(grid, False))
        def _():
            exchange.start(*parts)


def _wait_carried(exchange, parts, grid):
    if parts is not None:
        @pl.when(_grid_edge(grid, True))
        def _():
            exchange.wait(*parts)


def _carrier_call(body, name, grid, in_specs, out_specs, out_shape, scratch, aliases, operands, exchange):
    n_out = len(out_shape)
    aliases = dict(aliases)
    if exchange is not None:
        ex_in, ex_out, ex_shape = exchange.specs()
        aliases.update(exchange.aliases(len(in_specs), n_out))
        in_specs, out_specs, out_shape = in_specs + ex_in, out_specs + ex_out, out_shape + ex_shape
        scratch = scratch + exchange.scratch
        operands = list(operands) + exchange.operands()
    outs = pl.pallas_call(body, name=name, grid=grid, in_specs=in_specs, out_specs=out_specs, out_shape=out_shape,
                          scratch_shapes=scratch, input_output_aliases=aliases,
                          compiler_params=_cparams(*["arbitrary"] * len(grid)))(*operands)
    return outs[:n_out], outs[n_out:]


def _exchange_call(exchange, name):
    in_specs, out_specs, out_shape = exchange.specs()

    def body(*refs):
        _, parts = exchange.split(refs, 0, 0, 0)
        exchange.start(*parts)
        exchange.wait(*parts)

    return pl.pallas_call(body, name=name, in_specs=in_specs, out_specs=out_specs, out_shape=out_shape,
                          scratch_shapes=exchange.scratch, input_output_aliases=exchange.aliases(0, 0))(
        *exchange.operands())


def _gather_exchange(shards, fulls, axes, layers):
    sizes = [s.shape[a] for s, a in zip(shards, axes)]

    def copies(srcs, dsts, chip):
        out = []
        for src, full, axis, size, (l0, l1) in zip(srcs, dsts, axes, sizes, layers):
            part = src.at[l0:l1]
            cut = pl.ds(chip * size, size)
            dst = full.at[l0:l1, cut, :] if axis == 1 else full.at[l0:l1, :, cut]
            out.append(((part, dst), [(part, dst)] * 3))
        return out

    return _Exchange(shards, fulls, copies)


def _scatter_exchange(grads, stacks, axes, layers):
    sizes = [g.shape[a - 1] // 4 for g, a in zip(grads, axes)]

    def copies(srcs, dsts, chip):
        out = []
        for g, stack, axis, size, layer in zip(srcs, dsts, axes, sizes, layers):
            def cut(j, g=g, axis=axis, size=size):
                return g.at[pl.ds(j * size, size), :] if axis == 1 else g.at[:, pl.ds(j * size, size)]

            remote = [(cut(chip ^ (2 * fx + fy)), stack.at[r, layer]) for r, (fx, fy) in enumerate(CHIP_FLIPS)]
            out.append(((cut(chip), stack.at[3, layer]), remote))
        return out

    return _Exchange(grads, stacks, copies)


def _swap_with_sibling(parts, name):
    n = len(parts)

    def body(*refs):
        ins, outs = refs[:n], refs[n:2 * n]
        send, recv = refs[2 * n:]
        peer = (lax.axis_index("x"), lax.axis_index("y"), 1 - lax.axis_index("c"))
        copies = [pltpu.make_async_remote_copy(ins[t], outs[t], send.at[t], recv.at[t], device_id=peer,
                                               device_id_type=MESH) for t in range(n)]
        for cp in copies:
            cp.start()
        for cp in copies:
            cp.wait()

    return pl.pallas_call(
        body, name=name, in_specs=[ANY] * n, out_specs=[ANY] * n,
        out_shape=[jax.ShapeDtypeStruct(p.shape, p.dtype) for p in parts],
        scratch_shapes=[pltpu.SemaphoreType.DMA((n,)), pltpu.SemaphoreType.DMA((n,))])(*parts)


def _allreduce_small(v, name):
    R = v.shape[0]
    ND = 8

    def body(v_ref, o_ref, buf, send, recv):
        x, y, c = lax.axis_index("x"), lax.axis_index("y"), lax.axis_index("c")
        me = 4 * x + 2 * y + c
        copies = []
        for d in range(1, ND):
            peer = (_flip(x, d >> 2 & 1), _flip(y, d >> 1 & 1), _flip(c, d & 1))
            copies.append(pltpu.make_async_remote_copy(v_ref, buf.at[me], send.at[d], recv.at[me], device_id=peer,
                                                       device_id_type=MESH))
        for cp in copies:
            cp.start()
        buf[pl.ds(me, 1)] = v_ref[...][None]
        for k in range(ND):
            @pl.when(me != k)
            def _():
                pltpu.make_async_remote_copy(v_ref, buf.at[k], send.at[0], recv.at[k], device_id=(x, y, c),
                                             device_id_type=MESH).wait_recv()
        for cp in copies:
            cp.wait_send()
        total = buf[0]
        for k in range(1, ND):
            total = total + buf[k]
        o_ref[...] = total

    vm = pl.BlockSpec(memory_space=pltpu.VMEM)
    return pl.pallas_call(
        body, name=name, in_specs=[vm], out_specs=vm, out_shape=jax.ShapeDtypeStruct((R, LANES), F32),
        scratch_shapes=[pltpu.VMEM((ND, R, LANES), F32), pltpu.SemaphoreType.DMA((ND,)),
                        pltpu.SemaphoreType.DMA((ND,))],
        compiler_params=pltpu.CompilerParams(vmem_limit_bytes=VMEM_LIMIT))(v)


def _tile2(R, Cn):
    tc = _pick(Cn, 2048)
    tr = R
    for cand in (256, 128, 64, 32, 16, 8):
        if R % cand == 0:
            tr = cand
            break
    return tr, tc


def _sum4(stack, name):
    _, R, Cn = stack.shape
    tr, tc = _tile2(R, Cn)

    def body(s_ref, o_ref):
        o_ref[...] = ((s_ref[0].astype(F32) + s_ref[1].astype(F32)) + s_ref[2].astype(F32)) + s_ref[3].astype(F32)

    return pl.pallas_call(
        body, name=name, grid=(R // tr, Cn // tc), in_specs=[pl.BlockSpec((4, tr, tc), lambda i, j: (0, i, j))],
        out_specs=pl.BlockSpec((tr, tc), lambda i, j: (i, j)), out_shape=jax.ShapeDtypeStruct((R, Cn), F32),
        compiler_params=_cparams("parallel", "parallel"))(stack)


def _adamw(w, m, v, g_parts, name):
    R, Cn = w.shape
    tr, tc = _tile2(R, Cn)
    npart = len(g_parts)
    c1 = 1.0 / (1.0 - ADAM_B1 ** ADAM_STEP)
    c2 = 1.0 / (1.0 - ADAM_B2 ** ADAM_STEP)

    def body(*refs):
        w_ref, m_ref, v_ref = refs[:3]
        g_refs = refs[3:3 + npart]
        g_out, d_out, m_out, v_out = refs[3 + npart:]
        g = g_refs[0][...]
        for r in g_refs[1:]:
            g = g + r[...]
        mn = ADAM_B1 * m_ref[...] + (1.0 - ADAM_B1) * g
        vn = ADAM_B2 * v_ref[...] + (1.0 - ADAM_B2) * (g * g)
        g_out[...] = g
        m_out[...] = mn
        v_out[...] = vn
        d_out[...] = -ADAM_LR * ((mn * c1) / (jnp.sqrt(vn * c2) + ADAM_EPS) + ADAM_WD * w_ref[...])

    spec = pl.BlockSpec((tr, tc), lambda i, j: (i, j))
    return pl.pallas_call(
        body, name=name, grid=(R // tr, Cn // tc), in_specs=[spec] * (3 + npart), out_specs=[spec] * 4,
        out_shape=[jax.ShapeDtypeStruct((R, Cn), F32)] * 4,
        compiler_params=_cparams("parallel", "parallel"))(w, m, v, *g_parts)


SHARDED = (("ab_w_in", 2), ("ab_w_out", 1), ("c_w_in", 2), ("c_w_out", 1), ("ffn_up", 2), ("ffn_conv", 2),
           ("ffn_down", 1), ("ple_gate", 1), ("ple_proj", 2))
SMALL = ("mix_norm", "hg_lb_logits", "hg_out_norm", "q_norm", "k_norm", "sinks", "rel_bias", "ffn_norm",
         "ffn_conv_b", "ple_norm")
WEIGHTS = ("mix_norm", "ab_w_in", "hg_lb_logits", "hg_out_norm", "ab_w_out", "c_w_in", "q_norm", "k_norm", "sinks",
           "rel_bias", "c_w_out", "ffn_norm", "ffn_up", "ffn_conv", "ffn_conv_b", "ffn_down", "ple_norm", "ple_gate",
           "ple_proj")
PACK_ALIGN = 8 * LANES


def _pack(arrs):
    pieces = []
    for a in arrs:
        flat = a.reshape(-1)
        pad = -flat.shape[0] % PACK_ALIGN
        pieces.append(jnp.pad(flat, (0, pad)).reshape(-1, LANES))
    return jnp.concatenate(pieces, axis=0)


def _unpack(packed, like):
    out, r = [], 0
    for a in like:
        size = int(np.prod(a.shape))
        rows = (size + PACK_ALIGN - 1) // PACK_ALIGN * 8
        out.append(packed[r:r + rows].reshape(-1)[:size].reshape(a.shape))
        r += rows
    return out


def _family_range(name, lo, hi):
    if name.startswith("ab_"):
        idx = [i // 2 for i in range(lo, hi) if i % 2 == 0]
    elif name.startswith("c_"):
        idx = [i // 2 for i in range(lo, hi) if i % 2 == 1]
    else:
        idx = list(range(lo, hi))
    return (idx[0], idx[-1] + 1) if idx else None


W_IN = ("ab_w_in", "c_w_in")
REST = tuple(k for k, _ in SHARDED if k not in W_IN)
MIXER = ("ab_w_in", "ab_w_out", "c_w_in", "c_w_out")
FFN = ("ffn_up", "ffn_conv", "ffn_down")
PLE = ("ple_gate", "ple_proj")
GATHER_PLAN = {
    "gather_first": [(0, W_IN)],
    "sb_fwd_0": [(0, REST), (1, ("c_w_in",))],
    "hgrn_fwd_0": [(1, ("ffn_up", "ffn_conv"))],
    "ffn_up_0": [(1, ("ffn_down", "c_w_out"))],
    "ffn_down_0": [(1, PLE)],
    "swa_fwd_1": [(2, ("ab_w_in",))],
    "ffn_up_1": [(2, ("ab_w_out", "ffn_down"))],
    "ffn_down_1": [(2, PLE)],
    "sb_fwd_2": [(2, ("ffn_up", "ffn_conv")), (3, ("c_w_in", "c_w_out", "ffn_up", "ffn_conv"))],
    "hgrn_fwd_2": [(3, ("ffn_down",))],
    "ffn_up_2": [(3, PLE)],
}
SCATTER_PLAN = {
    "d_ffn_up_3": [(3, ("ffn_down", "ffn_conv"))],
    "d_hn2_3": [(3, PLE)],
    "swa_bwd_3": [(3, ("ffn_up", "c_w_out"))],
    "sb_bwd_2": [(3, W_IN), (2, REST)],
    "convglu_bwd_1": [(2, W_IN)],
    "d_ffn_up_1": [(1, ("ffn_down", "ffn_conv"))],
    "d_hn2_1": [(1, PLE)],
    "swa_bwd_1": [(1, ("ffn_up", "c_w_out"))],
    "sb_bwd_0": [(1, W_IN), (0, REST)],
    "d_hn_0": [(0, W_IN)],
}


class _StepExchanges:
    def __init__(self, shards, W):
        self.shards, self.W = shards, W
        self.stacks = {}
        for k, axis in SHARDED:
            shp = shards[k].shape
            self.W[k] = lax.empty(tuple(4 * d if i == axis else d for i, d in enumerate(shp)), shards[k].dtype)
            self.stacks[k] = lax.empty((4,) + shp, shards[k].dtype)

    @staticmethod
    def _select(plan):
        idx = {}
        for layer, fams in plan:
            for k, _ in SHARDED:
                r = _family_range(k, layer, layer + 1)
                if r is not None and (fams is None or k in fams):
                    idx.setdefault(k, []).append(r[0])
        return [(k, axis, sorted(idx[k])) for k, axis in SHARDED if k in idx]

    def gather(self, call):
        sel = self._select(GATHER_PLAN.get(call, ()))
        if not sel:
            return None, None
        for _, _, ii in sel:
            assert ii == list(range(ii[0], ii[-1] + 1)), "one copy per family takes a contiguous layer range"
        names = [k for k, _, _ in sel]
        ex = _gather_exchange([self.shards[k] for k in names], [self.W[k] for k in names],
                              [a for _, a, _ in sel], [(ii[0], ii[-1] + 1) for _, _, ii in sel])
        return ex, (self.W, names)

    def scatter(self, call, G):
        sel = self._select(SCATTER_PLAN.get(call, ()))
        if not sel:
            return None, None
        assert all(len(ii) == 1 for _, _, ii in sel)
        names = [k for k, _, _ in sel]
        ex = _scatter_exchange([G[k][ii[0]] for k, _, ii in sel], [self.stacks[k] for k in names],
                               [a for _, a, _ in sel], [ii[0] for _, _, ii in sel])
        return ex, (self.stacks, names)

    @staticmethod
    def adopt(where, targets):
        if where is not None:
            book, names = where
            for k, t in zip(names, targets):
                book[k] = t


def _forward_backward(x, p, target, W, S, exchanges=None):
    T = x.shape[0]
    depth = p.shape[0]
    lb = _lower_bound_fwd(W["hg_lb_logits"], "lower_bound_fwd")
    bias = _bias_build(W["rel_bias"], "bias_build")
    qk_gain = jnp.concatenate([jnp.tile(W["q_norm"], (1, SW_HEADS)), jnp.tile(W["k_norm"], (1, SW_KV))], axis=1)

    def gathering(call):
        return exchanges.gather(call) if exchanges else (None, None)

    def scattering(call):
        return exchanges.scatter(call, G) if exchanges else (None, None)

    def matmul(a, b, mode, name, **kw):
        ex, where = (gathering(name) if name in GATHER_PLAN else
                     scattering(name) if name in SCATTER_PLAN else (None, None))
        out, moved = _matmul(a, b, mode, name, exchange=ex, **kw)
        _StepExchanges.adopt(where, moved)
        return out

    def mm(a, wname, layer, mode, name, **kw):
        return matmul(a, W[wname], mode, name, b_layer=layer, **kw)

    saved = []
    h = x
    for i in range(depth):
        j = i // 2
        s = {"h0": h}
        s["hn"] = _rmsnorm_fwd(h, W["mix_norm"][i:i + 1], f"mix_norm_fwd_{i}") if i == 0 else hn_next
        if i % 2 == 0:
            s["proj"] = mm(s["hn"], "ab_w_in", j, "nn", f"ab_in_{i}")
            ex, where = gathering(f"sb_fwd_{i}")
            cat, s["sb_tot"], arrived = _sb_fwd(s["proj"], S, f"sb_fwd_{i}", ex)
            _StepExchanges.adopt(where, arrived)
            ex, where = gathering(f"hgrn_fwd_{i}")
            s["cat"], s["oraw"], s["states"], arrived = _hgrn_fwd(s["proj"], cat, lb[j:j + 1],
                                                                  W["hg_out_norm"][j:j + 1], S, f"hgrn_fwd_{i}", ex)
            _StepExchanges.adopt(where, arrived)
            h, s["hn2"] = mm(s["cat"], "ab_w_out", j, "nn", f"ab_out_{i}", res=h, norm_gain=W["ffn_norm"][i:i + 1])
        else:
            s["proj"] = mm(s["hn"], "c_w_in", j, "nn", f"c_in_{i}")
            s["qkn"] = _headnorm_fwd(s["proj"], qk_gain[j:j + 1], f"qk_norm_fwd_{i}")
            ex, where = gathering(f"swa_fwd_{i}")
            s["o"], arrived = _swa_fwd(s["qkn"], s["proj"], bias, W["sinks"][j], S, f"swa_fwd_{i}", ex)
            _StepExchanges.adopt(where, arrived)
            h, s["hn2"] = mm(s["o"], "c_w_out", j, "nn", f"c_out_{i}", res=h, norm_gain=W["ffn_norm"][i:i + 1])
        s["h1"] = h
        s["u"] = mm(s["hn2"], "ffn_up", i, "nn", f"ffn_up_{i}", tiles=(min(T, 1024), D_FF // 2, D_MODEL))
        s["a"] = _convglu_fwd(s["u"], W["ffn_conv"][i], W["ffn_conv_b"][i:i + 1], S, f"convglu_fwd_{i}")
        h, s["hn3"] = mm(s["a"], "ffn_down", i, "nn", f"ffn_down_{i}", res=h, norm_gain=W["ple_norm"][i:i + 1])
        s["h2"] = h
        s["z"] = mm(s["hn3"], "ple_gate", i, "nn", f"ple_gate_{i}")
        s["pp"] = mm(p, "ple_proj", i, "nn", f"ple_proj_{i}", a_layer=i)
        h, hn_next = _ple_fwd(h, s["z"], s["pp"], W["mix_norm"][i + 1:i + 2] if i + 1 < depth else None,
                              f"ple_fwd_{i}")
        saved.append(s)

    loss, dh = _loss_fwd_bwd(h, target, "loss")

    G = {k: [None] * depth for k in ("mix_norm", "ffn_norm", "ple_norm", "ffn_up", "ffn_conv", "ffn_conv_b",
                                     "ffn_down", "ple_gate", "ple_proj")}
    for k in ("ab_w_in", "ab_w_out", "c_w_in", "c_w_out", "hg_out_norm", "q_norm", "k_norm", "sinks", "lb"):
        G[k] = [None] * (depth // 2)
    dbias_total = None
    for i in reversed(range(depth)):
        j = i // 2
        s = saved[i]
        dz, dpp = _ple_bwd(dh, s["z"], s["pp"], f"ple_bwd_{i}")
        G["ple_proj"][i] = matmul(p, dpp, "tn", f"d_ple_proj_{i}", out_dtype=BF16, a_layer=i)
        G["ple_gate"][i] = matmul(s["hn3"], dz, "tn", f"d_ple_gate_{i}", out_dtype=BF16)
        rt = min(T, 512)
        dh, G["ple_norm"][i] = mm(dz, "ple_gate", i, "nt", f"d_hn3_{i}", tiles=(rt, D_MODEL, D_MODEL),
                                  norm_bwd=(s["h2"], W["ple_norm"][i:i + 1], dh))

        half_ff = D_FF // 2
        da = mm(dh, "ffn_down", i, "nt", f"d_a_{i}", tiles=(min(T, 1024), half_ff, D_MODEL))
        G["ffn_down"][i] = matmul(s["a"], dh, "tn", f"d_ffn_down_{i}", out_dtype=BF16,
                                   tiles=(half_ff, 512, min(T, 2048)))
        ex, where = scattering(f"convglu_bwd_{i}")
        du, dcw, dcb, sent = _convglu_bwd(s["u"], da, W["ffn_conv"][i], W["ffn_conv_b"][i:i + 1], S,
                                          f"convglu_bwd_{i}", ex)
        _StepExchanges.adopt(where, sent)
        G["ffn_conv"][i] = jnp.swapaxes(dcw, 0, 1).reshape(3, 2 * D_FF)
        G["ffn_conv_b"][i] = dcb.reshape(1, 2 * D_FF)
        G["ffn_up"][i] = matmul(s["hn2"], du, "tn", f"d_ffn_up_{i}", out_dtype=BF16,
                                 tiles=(D_MODEL, half_ff, min(T, 1024)))
        dh, G["ffn_norm"][i] = mm(du, "ffn_up", i, "nt", f"d_hn2_{i}", tiles=(rt, D_MODEL, half_ff),
                                  norm_bwd=(s["h1"], W["ffn_norm"][i:i + 1], dh))

        if i % 2 == 0:
            dcat = mm(dh, "ab_w_out", j, "nt", f"d_cat_{i}")
            G["ab_w_out"][j] = matmul(s["cat"], dh, "tn", f"d_ab_out_{i}", out_dtype=BF16)
            ex, where = scattering(f"sb_bwd_{i}")
            dproj, sent = _sb_bwd(s["proj"], s["sb_tot"], dcat, S, f"sb_bwd_{i}", ex)
            _StepExchanges.adopt(where, sent)
            dproj, G["lb"][j], G["hg_out_norm"][j] = _hgrn_bwd(s["proj"], s["oraw"], dcat, s["states"], lb[j:j + 1],
                                                               W["hg_out_norm"][j:j + 1], dproj, S, f"hgrn_bwd_{i}")
            G["ab_w_in"][j] = matmul(s["hn"], dproj, "tn", f"d_ab_in_{i}", out_dtype=BF16)
            w_in, width = "ab_w_in", dproj.shape[1] // 2
        else:
            do = mm(dh, "c_w_out", j, "nt", f"d_o_{i}")
            G["c_w_out"][j] = matmul(s["o"], dh, "tn", f"d_c_out_{i}", out_dtype=BF16)
            ex, where = scattering(f"swa_bwd_{i}")
            dqkv, dbias, dsink, sent = _swa_bwd(s["qkn"], s["proj"], bias, W["sinks"][j], do, S, f"swa_bwd_{i}", ex)
            _StepExchanges.adopt(where, sent)
            dbias_total = dbias if dbias_total is None else dbias_total + dbias
            G["sinks"][j] = dsink[:, 0]
            dproj, dgain = _headnorm_bwd(s["proj"], qk_gain[j:j + 1], dqkv, f"qk_norm_bwd_{i}")
            G["q_norm"][j] = dgain[0, :SW_Q_COLS].reshape(SW_HEADS, SW_DIM).sum(axis=0)
            G["k_norm"][j] = dgain[0, SW_Q_COLS:].reshape(SW_KV, SW_DIM).sum(axis=0)
            G["c_w_in"][j] = matmul(s["hn"], dproj, "tn", f"d_c_in_{i}", out_dtype=BF16)
            w_in, width = "c_w_in", dproj.shape[1]
        dh, G["mix_norm"][i] = mm(dproj, w_in, j, "nt", f"d_hn_{i}", tiles=(rt, D_MODEL, width),
                                  norm_bwd=(s["h0"], W["mix_norm"][i:i + 1], dh))

    grads = {k: G[k] for k, _ in SHARDED}
    for k in ("q_norm", "k_norm", "sinks"):
        grads[k] = jnp.stack(G[k])
    for k in ("mix_norm", "ffn_norm", "ple_norm", "ffn_conv_b", "hg_out_norm"):
        grads[k] = jnp.concatenate(G[k], axis=0)
    grads["hg_lb_logits"] = _lower_bound_bwd(W["hg_lb_logits"], jnp.concatenate(G["lb"], axis=0), "lower_bound_bwd")
    grads["rel_bias"] = _bias_reduce(dbias_total, "bias_reduce")[:, 0, :N_BUCKETS].T
    return loss, dh, grads


def kernel(x, p, mix_norm, ab_w_in, hg_lb_logits, hg_out_norm, ab_w_out, c_w_in, q_norm, k_norm, sinks, rel_bias, c_w_out, ffn_norm, ffn_up, ffn_conv, ffn_conv_b, ffn_down, ple_norm, ple_gate, ple_proj, loss_target, m_mix_norm, m_ab_w_in, m_hg_lb_logits, m_hg_out_norm, m_ab_w_out, m_c_w_in, m_q_norm, m_k_norm, m_sinks, m_rel_bias, m_c_w_out, m_ffn_norm, m_ffn_up, m_ffn_conv, m_ffn_conv_b, m_ffn_down, m_ple_norm, m_ple_gate, m_ple_proj, v_mix_norm, v_ab_w_in, v_hg_lb_logits, v_hg_out_norm, v_ab_w_out, v_c_w_in, v_q_norm, v_k_norm, v_sinks, v_rel_bias, v_c_w_out, v_ffn_norm, v_ffn_up, v_ffn_conv, v_ffn_conv_b, v_ffn_down, v_ple_norm, v_ple_gate, v_ple_proj):
    args = dict(locals())
    w = {k: args[k] for k in WEIGHTS}
    m = {k: args["m_" + k] for k in WEIGHTS}
    v = {k: args["v_" + k] for k in WEIGHTS}
    B, S, Dm = x.shape
    T = B * S
    names = [k for k, _ in SHARDED]

    W = {k: w[k] for k in SMALL}
    exchanges = _StepExchanges({k: w[k].astype(F32 if k == "ffn_conv" else BF16) for k in names}, W)
    first, where = exchanges.gather("gather_first")
    exchanges.adopt(where, _exchange_call(first, "gather_first"))

    loss, dx, grads = _forward_backward(x.reshape(T, Dm), p.reshape(p.shape[0], T, p.shape[-1]),
                                        loss_target.reshape(T, Dm), W, S, exchanges)
    loss = lax.psum(loss[0, 0], ("x", "y", "c"))

    stacks = [exchanges.stacks[k] for k in names]
    partial = [_sum4(st.reshape(4, -1, st.shape[-1]), f"sum_chips_{k}") for k, st in zip(names, stacks)]
    other = _swap_with_sibling(partial, "swap_core_sums")
    small_sum = _allreduce_small(_pack([grads[k] for k in SMALL]), "allreduce_small")

    out_g, out_d, out_m, out_v = {}, {}, {}, {}
    for k, mine, theirs in zip(names, partial, other):
        shp = w[k].shape
        r = [a.reshape(shp) for a in _adamw(w[k].reshape(mine.shape), m[k].reshape(mine.shape),
                                            v[k].reshape(mine.shape), [mine, theirs], f"adamw_{k}")]
        out_g[k], out_d[k], out_m[k], out_v[k] = r
    sm = _adamw(_pack([w[k] for k in SMALL]), _pack([m[k] for k in SMALL]), _pack([v[k] for k in SMALL]),
                [small_sum], "adamw_small")
    like = [w[k] for k in SMALL]
    for dst, packed in zip((out_g, out_d, out_m, out_v), sm):
        for k, a in zip(SMALL, _unpack(packed, like)):
            dst[k] = a

    return (loss, dx.reshape(B, S, Dm), *[out_g[k] for k in WEIGHTS], *[out_d[k] for k in WEIGHTS],
            *[out_m[k] for k in WEIGHTS], *[out_v[k] for k in WEIGHTS])
```

```python
import math

import numpy as np
import jax
import jax.numpy as jnp
from jax import lax
from jax.experimental import pallas as pl
from jax.experimental.pallas import tpu as pltpu

F32 = jnp.float32
BF16 = jnp.bfloat16
MESH = pl.DeviceIdType.MESH
ANY = pl.BlockSpec(memory_space=pl.ANY)

D_MODEL = 1024
EPS = 1e-6
SB_HEADS, SB_DIM = 8, 64
HG_HEADS, HG_DK = 4, 128
HG_CHUNK = 32
HG_STEP = 2
HG_TRIP = 8
SW_HEADS, SW_KV, SW_DIM, WINDOW = 16, 4, 64, 128
N_BUCKETS, MAX_DISTANCE = 32, 128
D_FF = 2816
ATT_BLOCK = 128
SB_QBLOCK = 256
LANES = 128
NEG = -1e30

ADAM_LR, ADAM_B1, ADAM_B2, ADAM_EPS, ADAM_WD, ADAM_STEP = 0.001, 0.9, 0.999, 1e-08, 0.01, 10

VMEM_LIMIT = 56 * 1024 * 1024


def _cparams(*sem):
    return pltpu.CompilerParams(dimension_semantics=sem, vmem_limit_bytes=VMEM_LIMIT)


def _pick(n, cap):
    if n <= cap:
        return n
    best = None
    for d in range(LANES, cap + 1, LANES):
        if n % d == 0:
            best = d
    assert best is not None, (n, cap)
    return best


def _dot(a, b, ca, cb):
    return lax.dot_general(a.astype(BF16), b.astype(BF16), (((ca,), (cb,)), ((), ())),
                           preferred_element_type=F32)


def _split(x, terms):
    parts = []
    for _ in range(terms):
        hi = x.astype(BF16)
        parts.append(hi)
        x = x - hi.astype(F32)
    return parts


def _dot_exact_l(x, m, terms=2):
    out = None
    for p in _split(x, terms):
        t = lax.dot_general(p, m, (((1,), (0,)), ((), ())), preferred_element_type=F32)
        out = t if out is None else out + t
    return out


def _dot_exact_r(m, x, terms=3, cm=1):
    out = None
    for p in _split(x, terms):
        t = lax.dot_general(m, p, (((cm,), (0,)), ((), ())), preferred_element_type=F32)
        out = t if out is None else out + t
    return out


def _sig(x):
    return 1.0 / (1.0 + jnp.exp(-x))


def _iota2(shape, dim):
    return lax.broadcasted_iota(jnp.int32, shape, dim)


def _operand_spec(arr, layer, blk, index):
    if arr.ndim == 2:
        return pl.BlockSpec(blk, index)
    if layer is not None:
        return pl.BlockSpec((None,) + blk, lambda i, j, k: (layer,) + index(i, j, k))
    per_half = arr.shape[2] // blk[1]

    def halves(i, j, k):
        r, c = index(i, j, k)
        return (c // per_half, r, c % per_half)

    return pl.BlockSpec((None,) + blk, halves)


def _matmul(a, b, mode, name, out_dtype=F32, res=None, a_layer=None, b_layer=None, tiles=None, exchange=None,
            norm_gain=None, norm_bwd=None):
    def dims(arr, layer):
        if arr.ndim == 2:
            return arr.shape
        return arr.shape[1:] if layer is not None else (arr.shape[1], 2 * arr.shape[2])

    (a0, a1), (b0, b1) = dims(a, a_layer), dims(b, b_layer)
    if mode == "nn":
        M, K, N = a0, a1, b1
    elif mode == "nt":
        M, K, N = a0, a1, b0
    else:
        K, M, N = a0, a1, b1
    cap_m, cap_n, cap_k = 1024, 1024, (1024 if mode == "tn" else 2048)
    tm, tn, tk = _pick(M, cap_m), _pick(N, cap_n), _pick(K, cap_k)
    if a.ndim == 3 and a_layer is None:
        if mode == "tn":
            tm = _pick(a.shape[2], cap_m)
        else:
            tk = _pick(a.shape[2], cap_k)
    if b.ndim == 3 and b_layer is None:
        if mode == "nt":
            tk = _pick(b.shape[2], cap_k)
        else:
            tn = _pick(b.shape[2], cap_n)
    if tiles is not None:
        tm, tn, tk = tiles
    assert M % tm == 0 and N % tn == 0 and K % tk == 0, (name, M, N, K, tm, tn, tk)
    nk = K // tk
    if mode == "tn":
        a_spec = _operand_spec(a, a_layer, (tk, tm), lambda i, j, k: (k, i))
    else:
        a_spec = _operand_spec(a, a_layer, (tm, tk), lambda i, j, k: (i, k))
    if mode == "nt":
        b_spec = _operand_spec(b, b_layer, (tn, tk), lambda i, j, k: (j, k))
    else:
        b_spec = _operand_spec(b, b_layer, (tk, tn), lambda i, j, k: (k, j))
    ca, cb = {"nn": (1, 0), "nt": (1, 1), "tn": (0, 0)}[mode]
    o_spec = pl.BlockSpec((tm, tn), lambda i, j, k: (i, j))

    grid = (M // tm, N // tn, nk)
    normed = norm_gain is not None
    through = norm_bwd is not None
    assert not (normed or through) or tn == N, "the norm needs whole rows in one tile"
    assert not (through and (normed or res is not None))
    v_spec = pl.BlockSpec((1, tn), lambda i, j, k: (0, j))
    if through:
        x_in, gain_in, dres_in = norm_bwd
        extra_in, extra_specs = [x_in, gain_in, dres_in], [o_spec, v_spec, o_spec]
        out_specs = [o_spec, v_spec]
        out_shape = [jax.ShapeDtypeStruct((M, N), F32), jax.ShapeDtypeStruct((1, N), F32)]
    else:
        extra_in = ([] if res is None else [res]) + ([norm_gain] if normed else [])
        extra_specs = ([] if res is None else [o_spec]) + ([v_spec] if normed else [])
        out_specs = [o_spec] * (1 + normed)
        out_shape = ([jax.ShapeDtypeStruct((M, N), out_dtype)]
                     + ([jax.ShapeDtypeStruct((M, N), BF16)] if normed else []))
    n_in, n_out = 2 + len(extra_in), len(out_shape)

    def body(*refs):
        own, carried = _carried(exchange, refs, n_in, n_out, 1)
        _start_carried(exchange, carried, grid)
        a_ref, b_ref = own[:2]
        extras, outs, acc = own[2:n_in], own[n_in:n_in + n_out], own[-1]
        i, k = pl.program_id(0), pl.program_id(2)

        @pl.when(k == 0)
        def _():
            acc[...] = jnp.zeros_like(acc)

        acc[...] += _dot(a_ref[...], b_ref[...], ca, cb)

        @pl.when(k == nk - 1)
        def _():
            r = acc[...]
            if through:
                x, dy = extras[0][...], r
                scale = lax.rsqrt(jnp.mean(x * x, axis=1, keepdims=True) + EPS)
                gdy = dy * extras[1][...]
                m = jnp.mean(x * gdy, axis=1, keepdims=True)
                outs[0][...] = extras[2][...] + scale * gdy - x * (scale * scale * scale * m)
                part = jnp.sum(dy * x * scale, axis=0, keepdims=True)

                @pl.when(i == 0)
                def _():
                    outs[1][...] = part

                @pl.when(i > 0)
                def _():
                    outs[1][...] += part
            else:
                if res is not None:
                    r = r + extras[0][...]
                outs[0][...] = r.astype(out_dtype)
                if normed:
                    scale = lax.rsqrt(jnp.mean(r * r, axis=1, keepdims=True) + EPS)
                    outs[1][...] = (r * scale * extras[-1][...]).astype(BF16)

        _wait_carried(exchange, carried, grid)

    ins = [a, b] + extra_in
    in_specs = [a_spec, b_spec] + extra_specs
    scratch = [pltpu.VMEM((tm, tn), F32)]
    if exchange is None:
        sem = ("arbitrary" if through else "parallel", "parallel", "arbitrary")
        outs = pl.pallas_call(
            body, name=name, grid=grid, in_specs=in_specs, out_specs=out_specs, out_shape=out_shape,
            scratch_shapes=scratch, compiler_params=_cparams(*sem))(*ins)
        targets = ()
    else:
        outs, targets = _carrier_call(body, name, grid, in_specs, out_specs, out_shape, scratch, {}, ins, exchange)
    return (tuple(outs) if n_out > 1 else outs[0]), targets


ROW_TILE = 512
HEAD_ROWS = 2048


def _row_spec(width):
    return pl.BlockSpec((ROW_TILE, width), lambda i: (i, 0))


def _vec_spec(width):
    return pl.BlockSpec((1, width), lambda i: (0, 0))


def _rmsnorm_fwd(h, g, name):
    T, Dm = h.shape

    def body(h_ref, g_ref, o_ref):
        x = h_ref[...]
        r = lax.rsqrt(jnp.mean(x * x, axis=1, keepdims=True) + EPS)
        o_ref[...] = (x * r * g_ref[...]).astype(BF16)

    return pl.pallas_call(
        body, name=name, grid=(T // ROW_TILE,), in_specs=[_row_spec(Dm), _vec_spec(Dm)],
        out_specs=_row_spec(Dm), out_shape=jax.ShapeDtypeStruct((T, Dm), BF16),
        compiler_params=_cparams("parallel"))(h, g)


def _ple_fwd(h, z, pp, gain, name):
    T, Dm = h.shape
    normed = gain is not None

    def body(*refs):
        h_ref, z_ref, p_ref = refs[:3]
        y = h_ref[...] + _sig(z_ref[...]) * p_ref[...]
        refs[3 + normed][...] = y
        if normed:
            r = lax.rsqrt(jnp.mean(y * y, axis=1, keepdims=True) + EPS)
            refs[5][...] = (y * r * refs[3][...]).astype(BF16)

    outs = pl.pallas_call(
        body, name=name, grid=(T // ROW_TILE,), in_specs=[_row_spec(Dm)] * 3 + [_vec_spec(Dm)] * normed,
        out_specs=[_row_spec(Dm)] * (1 + normed),
        out_shape=[jax.ShapeDtypeStruct((T, Dm), F32)] + [jax.ShapeDtypeStruct((T, Dm), BF16)] * normed,
        compiler_params=_cparams("parallel"))(*([h, z, pp] + [gain] * normed))
    return outs if normed else (outs[0], None)


def _ple_bwd(dh, z, pp, name):
    T, Dm = dh.shape

    def body(dh_ref, z_ref, p_ref, dz_ref, dp_ref):
        s = _sig(z_ref[...])
        d = dh_ref[...]
        dz_ref[...] = d * p_ref[...] * s * (1.0 - s)
        dp_ref[...] = d * s

    return pl.pallas_call(
        body, name=name, grid=(T // ROW_TILE,), in_specs=[_row_spec(Dm)] * 3, out_specs=[_row_spec(Dm)] * 2,
        out_shape=[jax.ShapeDtypeStruct((T, Dm), F32)] * 2, compiler_params=_cparams("parallel"))(dh, z, pp)


def _loss_fwd_bwd(y, target, name):
    T, Dm = y.shape

    def body(y_ref, t_ref, l_ref, d_ref):
        i = pl.program_id(0)
        e = y_ref[...] - t_ref[...]
        d_ref[...] = e * (1.0 / Dm)
        part = jnp.full((8, LANES), 0.5 / Dm, F32) * jnp.sum(e * e)

        @pl.when(i == 0)
        def _():
            l_ref[...] = part

        @pl.when(i > 0)
        def _():
            l_ref[...] += part

    return pl.pallas_call(
        body, name=name, grid=(T // ROW_TILE,), in_specs=[_row_spec(Dm)] * 2,
        out_specs=[pl.BlockSpec((8, LANES), lambda i: (0, 0)), _row_spec(Dm)],
        out_shape=[jax.ShapeDtypeStruct((8, LANES), F32), jax.ShapeDtypeStruct((T, Dm), F32)],
        compiler_params=_cparams("arbitrary"))(y, target)


def _head_mean_matrix():
    r = _iota2((LANES, LANES), 0) >= SW_DIM
    c = _iota2((LANES, LANES), 1) >= SW_DIM
    return jnp.where(r == c, 1.0 / SW_DIM, 0.0).astype(BF16)


def _headnorm_fwd(x, g_lane, name):
    T = x.shape[0]
    C = g_lane.shape[1]

    def body(x_ref, g_ref, y_ref):
        xv = x_ref[...]
        ms = _dot_exact_l(xv * xv, _head_mean_matrix())
        y_ref[...] = xv * lax.rsqrt(ms + EPS) * g_ref[...]

    rows = min(T, HEAD_ROWS)
    spec = pl.BlockSpec((rows, LANES), lambda j, i: (i, j))
    return pl.pallas_call(
        body, name=name, grid=(C // LANES, T // rows),
        in_specs=[spec, pl.BlockSpec((1, LANES), lambda j, i: (0, j))], out_specs=spec,
        out_shape=jax.ShapeDtypeStruct((T, C), F32), compiler_params=_cparams("parallel", "parallel"))(x, g_lane)


def _headnorm_bwd(x, g_lane, dy_full, name):
    T = x.shape[0]
    C = g_lane.shape[1]

    def body(x_ref, g_ref, dy_ref, dx_ref, dg_ref):
        i = pl.program_id(1)
        xv = x_ref[...]
        dy = dy_ref[...]
        bd = _head_mean_matrix()
        r = lax.rsqrt(_dot_exact_l(xv * xv, bd) + EPS)
        gdy = dy * g_ref[...]
        m = _dot_exact_l(xv * gdy, bd)
        dx_ref[...] = r * gdy - xv * (r * r * r * m)
        part = jnp.sum(dy * xv * r, axis=0, keepdims=True)

        @pl.when(i == 0)
        def _():
            dg_ref[...] = part

        @pl.when(i > 0)
        def _():
            dg_ref[...] += part

    rows = min(T, HEAD_ROWS)
    spec = pl.BlockSpec((rows, LANES), lambda j, i: (i, j))
    vspec = pl.BlockSpec((1, LANES), lambda j, i: (0, j))
    return pl.pallas_call(
        body, name=name, grid=(C // LANES, T // rows), in_specs=[spec, vspec, spec],
        out_specs=[spec, vspec],
        out_shape=[jax.ShapeDtypeStruct(dy_full.shape, F32), jax.ShapeDtypeStruct((1, C), F32)],
        input_output_aliases={2: 0}, compiler_params=_cparams("parallel", "arbitrary"))(x, g_lane, dy_full)


CONV_TILE = 128


def _shift_down(x, k, rows):
    return jnp.where(rows >= k, pltpu.roll(x, k, 0), 0.0)


def _shift_up(x, k, rows):
    n = x.shape[0]
    return jnp.where(rows < n - k, pltpu.roll(x, n - k, 0), 0.0)


def _conv3(u, w_ref, b_ref, rows):
    return (w_ref[2:3, :] * u + w_ref[1:2, :] * _shift_down(u, 1, rows) + w_ref[0:1, :] * _shift_down(u, 2, rows)
            + b_ref[...])


def _convglu_fwd(u, cw, cb, S, name):
    T = u.shape[0]
    nf = D_FF // CONV_TILE

    def body(ug_ref, uu_ref, wg_ref, wu_ref, bg_ref, bu_ref, a_ref):
        rows = _iota2((S, CONV_TILE), 0)
        yg = _conv3(ug_ref[...], wg_ref, bg_ref, rows)
        yu = _conv3(uu_ref[...], wu_ref, bu_ref, rows)
        a_ref[...] = (yg * _sig(yg) * yu).astype(BF16)

    def blk(rows, off):
        return pl.BlockSpec((rows, CONV_TILE), (lambda b, j: (b, j + off)) if rows == S else (lambda b, j: (0, j + off)))

    return pl.pallas_call(
        body, name=name, grid=(T // S, nf),
        in_specs=[blk(S, 0), blk(S, nf), blk(3, 0), blk(3, nf), blk(1, 0), blk(1, nf)],
        out_specs=blk(S, 0), out_shape=jax.ShapeDtypeStruct((T, D_FF), BF16),
        compiler_params=_cparams("parallel", "parallel"))(u, u, cw, cw, cb, cb)


def _convglu_bwd(u, da, cw, cb, S, name, exchange=None):
    T = u.shape[0]
    nf = D_FF // CONV_TILE

    grid = (nf, T // S)

    def body(*refs):
        (ug_ref, uu_ref, da_ref, wg_ref, wu_ref, bg_ref, bu_ref, du_ref, dw_ref, db_ref), carried = _carried(
            exchange, refs, 7, 3, 0)
        _start_carried(exchange, carried, grid)
        b = pl.program_id(1)
        rows = _iota2((S, CONV_TILE), 0)
        ug, uu = ug_ref[...], uu_ref[...]
        yg = _conv3(ug, wg_ref, bg_ref, rows)
        yu = _conv3(uu, wu_ref, bu_ref, rows)
        s = _sig(yg)
        da_v = da_ref[...]
        for half, (uv, w_ref, dy) in enumerate(((ug, wg_ref, da_v * yu * (s * (1.0 + yg * (1.0 - s)))),
                                                (uu, wu_ref, da_v * yg * s))):
            up1, up2 = _shift_up(dy, 1, rows), _shift_up(dy, 2, rows)
            du_ref[half] = w_ref[2:3, :] * dy + w_ref[1:2, :] * up1 + w_ref[0:1, :] * up2
            dws = [jnp.sum(up2 * uv, axis=0, keepdims=True), jnp.sum(up1 * uv, axis=0, keepdims=True),
                   jnp.sum(dy * uv, axis=0, keepdims=True)]
            dbv = jnp.sum(dy, axis=0, keepdims=True)

            @pl.when(b == 0)
            def _():
                for k in range(3):
                    dw_ref[half, k:k + 1, :] = dws[k]
                db_ref[half] = dbv

            @pl.when(b > 0)
            def _():
                for k in range(3):
                    dw_ref[half, k:k + 1, :] += dws[k]
                db_ref[half] += dbv

        _wait_carried(exchange, carried, grid)

    def blk(rows, off):
        return pl.BlockSpec((rows, CONV_TILE), (lambda j, b: (b, j + off)) if rows == S else (lambda j, b: (0, j + off)))

    def both(rows):
        return pl.BlockSpec((2, rows, CONV_TILE), (lambda j, b: (0, b, j)) if rows == S else (lambda j, b: (0, 0, j)))

    outs, targets = _carrier_call(
        body, name, grid, [blk(S, 0), blk(S, nf), blk(S, 0), blk(3, 0), blk(3, nf), blk(1, 0), blk(1, nf)],
        [both(S), both(3), both(1)],
        [jax.ShapeDtypeStruct((2, T, D_FF), F32), jax.ShapeDtypeStruct((2, 3, D_FF), F32),
         jax.ShapeDtypeStruct((2, 1, D_FF), F32)], [], {}, (u, u, da, cw, cw, cb, cb), exchange)
    return (*outs, targets)


def _sb_scores(qb, kblk, on_diag_mask):
    z = _dot(qb, kblk, 1, 1) * (SB_DIM ** -0.5)
    l1 = jnp.log(1.0 + jnp.exp(-jnp.abs(z)))
    ls = jnp.minimum(z, 0.0) - l1
    lk = jnp.where(on_diag_mask, ls - z, 0.0)
    return ls, lk


def _sb_fwd(proj, S, name, exchange=None):
    T = proj.shape[0]
    BQ, BK = SB_QBLOCK, ATT_BLOCK
    unroll = BQ // BK
    nq = S // BQ
    nhp = SB_HEADS // 2
    heads = [slice(h * SB_DIM, (h + 1) * SB_DIM) for h in range(2)]

    grid = (T // S, nhp)

    def body(*refs):
        (q_ref, k_ref, v_ref, o_ref, tot_ref), carried = _carried(exchange, refs, 3, 2, 0)
        _start_carried(exchange, carried, grid)
        ahead = _iota2((BQ, BK), 1) - _iota2((BQ, BK), 0)
        upper = (_iota2((BK, BK), 0) > _iota2((BK, BK), 1)).astype(BF16)

        def qloop(iq, carry):
            q0 = pl.multiple_of(iq * BQ, BQ)
            rows = pl.ds(q0, BQ)
            qbs = [q_ref[rows, sl] for sl in heads]
            nkb = (iq + 1) * (BQ // BK)

            def kloop(jj, kc):
                blocks = []
                for u in range(unroll):
                    k0 = pl.multiple_of((nkb - 1 - unroll * jj - u) * BK, BK)
                    blocks.append((pl.ds(k0, BK), ahead < q0 - k0))
                units = [(h, krows, mask) for krows, mask in blocks for h in range(2)]
                scores = [_sb_scores(qbs[h], k_ref[krows, heads[h]], mask) for h, krows, mask in units]
                laters = [_dot_exact_l(lk, upper) for _, lk in scores]
                runs = [kc[h][0] for h in range(2)]
                accs = [kc[h][1] for h in range(2)]
                for (h, krows, mask), (ls, lk), later in zip(units, scores, laters):
                    w = jnp.where(mask, jnp.exp(ls + later + runs[h]), 0.0)
                    accs[h] = accs[h] + _dot(w, v_ref[krows, heads[h]], 1, 0)
                    runs[h] = runs[h] + jnp.sum(lk, axis=1, keepdims=True)
                return tuple((runs[h], accs[h]) for h in range(2))

            init = (jnp.zeros((BQ, 1), F32), jnp.zeros((BQ, SB_DIM), F32))
            res = lax.fori_loop(0, nkb // unroll, kloop, (init, init))
            for h, (sl, (run, acc)) in enumerate(zip(heads, res)):
                o_ref[rows, sl] = acc
                tot_ref[rows, h:h + 1] = run
            return carry

        lax.fori_loop(0, nq, qloop, 0)
        _wait_carried(exchange, carried, grid)

    def spec(off):
        return pl.BlockSpec((S, LANES), lambda b, hp: (b, hp + off))

    (cat, tot), targets = _carrier_call(
        body, name, grid, [spec(0), spec(nhp), spec(2 * nhp)],
        [spec(0), pl.BlockSpec((None, S, 2), lambda b, hp: (hp, b, 0))],
        [jax.ShapeDtypeStruct((T, 2 * SB_HEADS * SB_DIM), F32), jax.ShapeDtypeStruct((nhp, T, 2), F32)],
        [], {}, (proj, proj, proj), exchange)
    return cat, tot, targets


def _sb_bwd(proj, tot, dcat, S, name, exchange=None):
    T, width = proj.shape
    BQ, BK = SB_QBLOCK, ATT_BLOCK
    unroll = BQ // BK
    nq = S // BQ
    nhp = SB_HEADS // 2
    scale = SB_DIM ** -0.5
    heads = [slice(h * SB_DIM, (h + 1) * SB_DIM) for h in range(2)]

    grid = (T // S, nhp)

    def body(*refs):
        (q_ref, k_ref, v_ref, tot_ref, do_ref, dp_hbm, dq_s, dk_s, dv_s, sems), carried = _carried(
            exchange, refs, 5, 1, 4)
        _start_carried(exchange, carried, grid)
        b, hp = pl.program_id(0), pl.program_id(1)
        ahead = _iota2((BQ, BK), 1) - _iota2((BQ, BK), 0)
        r, c = _iota2((BK, BK), 0), _iota2((BK, BK), 1)
        upto = (r <= c).astype(BF16)
        earlier = (r < c).astype(BF16)
        dk_s[...] = jnp.zeros_like(dk_s)
        dv_s[...] = jnp.zeros_like(dv_s)

        def qloop(iq, carry):
            q0 = pl.multiple_of(iq * BQ, BQ)
            rows = pl.ds(q0, BQ)
            qbs = [q_ref[rows, sl] for sl in heads]
            dobs = [do_ref[rows, sl] for sl in heads]
            totals = [tot_ref[rows, h:h + 1] for h in range(2)]

            def kloop(m, kc):
                blocks = []
                for u in range(unroll):
                    k0 = pl.multiple_of((unroll * m + u) * BK, BK)
                    blocks.append((pl.ds(k0, BK), ahead < q0 - k0))
                units = [(h, krows, mask) for krows, mask in blocks for h in range(2)]
                kblks = [k_ref[krows, heads[h]] for h, krows, _ in units]
                scores = [_sb_scores(qbs[h], kblk, mask) for (h, _, mask), kblk in zip(units, kblks)]
                prefixes = [_dot_exact_l(lk, upto) for _, lk in scores]
                dws = [_dot(dobs[h], v_ref[krows, heads[h]], 1, 1) for h, krows, _ in units]
                runs = [kc[h][0] for h in range(2)]
                gruns = [kc[h][1] for h in range(2)]
                dqs = [kc[h][2] for h in range(2)]
                ws, gs = [], []
                for (h, _, mask), (ls, lk), prefix, dw in zip(units, scores, prefixes, dws):
                    w = jnp.where(mask, jnp.exp(ls + (totals[h] - (prefix + runs[h]))), 0.0)
                    runs[h] = runs[h] + jnp.sum(lk, axis=1, keepdims=True)
                    ws.append(w)
                    gs.append(w * dw)
                gprefixes = [_dot_exact_l(g, earlier) for g in gs]
                dzs = []
                for (h, _, mask), (ls, _), g, gprefix in zip(units, scores, gs, gprefixes):
                    beta = jnp.exp(ls)
                    dzs.append(jnp.where(mask, g * (1.0 - beta) - beta * (gprefix + gruns[h]), 0.0) * scale)
                    gruns[h] = gruns[h] + jnp.sum(g, axis=1, keepdims=True)
                for (h, krows, _), kblk, w, dz in zip(units, kblks, ws, dzs):
                    dv_s[krows, heads[h]] += _dot(w, dobs[h], 0, 0)
                    dk_s[krows, heads[h]] += _dot(dz, qbs[h], 0, 0)
                    dqs[h] = dqs[h] + _dot(dz, kblk, 1, 0)
                return tuple((runs[h], gruns[h], dqs[h]) for h in range(2))

            zero = jnp.zeros((BQ, 1), F32)
            init = (zero, zero, jnp.zeros((BQ, SB_DIM), F32))
            res = lax.fori_loop(0, (iq + 1) * (BQ // BK) // unroll, kloop, (init, init))
            for sl, (_, _, dq) in zip(heads, res):
                dq_s[rows, sl] = dq
            return carry

        lax.fori_loop(0, nq, qloop, 0)
        r0 = pl.multiple_of(b * S, S)
        copies = []
        for n, buf in enumerate((dq_s, dk_s, dv_s)):
            c0 = pl.multiple_of((hp + n * nhp) * LANES, LANES)
            copies.append(pltpu.make_async_copy(buf, dp_hbm.at[pl.ds(r0, S), pl.ds(c0, LANES)], sems.at[n]))
        for cp in copies:
            cp.start()
        for cp in copies:
            cp.wait()
        _wait_carried(exchange, carried, grid)

    def spec(off):
        return pl.BlockSpec((S, LANES), lambda b, hp: (b, hp + off))

    (dproj,), targets = _carrier_call(
        body, name, grid,
        [spec(0), spec(nhp), spec(2 * nhp), pl.BlockSpec((None, S, 2), lambda b, hp: (hp, b, 0)), spec(0)],
        [ANY], [jax.ShapeDtypeStruct((T, width), F32)],
        [pltpu.VMEM((S, LANES), F32)] * 3 + [pltpu.SemaphoreType.DMA((3,))], {}, (proj, proj, proj, tot, dcat),
        exchange)
    return dproj, targets


HG_COL0 = 3 * SB_HEADS * SB_DIM // LANES


def _hg_gates(q, fp, lbv):
    sg = _sig(fp)
    f = lbv + (1.0 - lbv) * sg
    kk = (1.0 - lbv) * _sig(-fp)
    sq = _sig(q)
    return sg, f, kk, sq


def _hg_chunk(qs, kk, lf, incl):
    C = HG_CHUNK
    b = _dot_exact_r(incl, lf)
    bl = b[C - 1:C, :]
    bm = b[C // 2 - 1:C // 2, :]
    e_t = jnp.exp(b - bm)
    e_s = jnp.exp(bm - b)
    e_i = jnp.exp(b)
    e_e = jnp.exp(bl - b)
    return bl, e_t, e_s, e_i, e_e


def _hgrn_fwd(proj, cat, lb, hgn, S, name, exchange=None):
    T = proj.shape[0]
    B = T // S
    C = HG_CHUNK
    NC = S // C

    grid = (B, HG_HEADS // HG_STEP)

    def body(*refs):
        (q_ref, f_ref, i_ref, g_ref, lb_ref, hgn_ref, _, ob_ref, oraw_ref, st_ref, state), carried = _carried(
            exchange, refs, 7, 3, 1)
        _start_carried(exchange, carried, grid)
        state[...] = jnp.zeros_like(state)
        row, col = _iota2((C, C), 0), _iota2((C, C), 1)
        causal = row >= col
        incl = causal.astype(BF16)

        def trip(m, carry):
            units = []
            for u in range(HG_TRIP):
                c = m * HG_TRIP + u
                rows = pl.ds(pl.multiple_of(c * C, C), C)
                units += [(c, hh, rows, slice(hh * LANES, (hh + 1) * LANES)) for hh in range(HG_STEP)]
            qs_, kks, lfs = [], [], []
            for _, _, rows, hs in units:
                q = q_ref[rows, hs]
                _, f, kk, sq = _hg_gates(q, f_ref[rows, hs], lb_ref[:, hs])
                qs_.append(q * sq)
                kks.append(kk)
                lfs.append(jnp.log(f))
            decays = [_hg_chunk(None, None, lf, incl) for lf in lfs]
            ps = [jnp.where(causal, _dot(qs * e_t, kk * e_s, 1, 1), 0.0)
                  for qs, kk, (_, e_t, e_s, _, _) in zip(qs_, kks, decays)]
            outs = []
            for (c, hh, rows, hs), qs, kk, (bl, _, _, e_i, e_e), p in zip(units, qs_, kks, decays, ps):
                iv = i_ref[rows, hs]
                st = state[hh]
                st_ref[0, hh, c] = st
                outs.append(_dot(qs * e_i, st, 1, 1) + _dot(p, iv, 1, 0))
                state[hh] = st * jnp.exp(bl) + _dot(iv, kk * e_e, 0, 0)
            for (_, _, rows, hs), o in zip(units, outs):
                gv = g_ref[rows, hs]
                oraw_ref[rows, hs] = o
                r = lax.rsqrt(jnp.mean(o * o, axis=1, keepdims=True) + EPS)
                ob_ref[rows, hs] = o * r * hgn_ref[...] * (gv * _sig(gv))
            return carry

        lax.fori_loop(0, NC // HG_TRIP, trip, 0)
        _wait_carried(exchange, carried, grid)

    width = HG_STEP * LANES
    col0 = HG_COL0 * LANES // width
    nstep = HG_HEADS // HG_STEP

    def spec(off):
        return pl.BlockSpec((S, width), lambda b, h: (b, h + off))

    outs, targets = _carrier_call(
        body, name, grid,
        [spec(col0), spec(col0 + nstep), spec(col0 + 2 * nstep), spec(col0 + 3 * nstep),
         pl.BlockSpec((1, width), lambda b, h: (0, h)), pl.BlockSpec((1, LANES), lambda b, h: (0, 0)), ANY],
        [spec(nstep), spec(0), pl.BlockSpec((1, HG_STEP, NC, LANES, LANES), lambda b, h: (b, h, 0, 0, 0))],
        [jax.ShapeDtypeStruct(cat.shape, F32), jax.ShapeDtypeStruct((T, HG_HEADS * LANES), F32),
         jax.ShapeDtypeStruct((B, HG_HEADS, NC, LANES, LANES), F32)],
        [pltpu.VMEM((HG_STEP, LANES, LANES), F32)], {6: 0}, (proj, proj, proj, proj, lb, hgn, cat), exchange)
    return (*outs, targets)


def _hgrn_bwd(proj, oraw, dcat, states, lb, hgn, dproj, S, name):
    T = proj.shape[0]
    B = T // S
    C = HG_CHUNK
    NC = S // C

    def body(q_ref, f_ref, i_ref, g_ref, oraw_ref, dy_ref, st_ref, lb_ref, hgn_ref, dp_in,
             dp_hbm, dlb_ref, dhgn_ref, dstate, dq_s, df_s, di_s, dg_s, sems):
        del dp_in
        h, b = pl.program_id(0), pl.program_id(1)
        row, col = _iota2((C, C), 0), _iota2((C, C), 1)
        causal = row >= col
        incl = causal.astype(BF16)
        last_row = _iota2((C, LANES), 0) == C - 1
        hg = hgn_ref[...]
        dstate[...] = jnp.zeros_like(dstate)

        @pl.when(b == 0)
        def _():
            dlb_ref[...] = jnp.zeros_like(dlb_ref)

        @pl.when(jnp.logical_and(b == 0, h == 0))
        def _():
            dhgn_ref[...] = jnp.zeros_like(dhgn_ref)

        def trip(m, carry):
            units = []
            for u in range(HG_TRIP):
                c = NC - 1 - (m * HG_TRIP + u)
                rows = pl.ds(pl.multiple_of(c * C, C), C)
                units += [(c, hh, rows, slice(hh * LANES, (hh + 1) * LANES)) for hh in range(HG_STEP)]
            dos = []
            dhgn = jnp.zeros((1, LANES), F32)
            for _, _, rows, hs in units:
                gv, o, dy = g_ref[rows, hs], oraw_ref[rows, hs], dy_ref[rows, hs]
                r = lax.rsqrt(jnp.mean(o * o, axis=1, keepdims=True) + EPS)
                on = o * r
                sgv = _sig(gv)
                silu_g = gv * sgv
                dg_s[rows, hs] = dy * on * hg * (sgv * (1.0 + gv * (1.0 - sgv)))
                dhgn = dhgn + jnp.sum(dy * on * silu_g, axis=0, keepdims=True)
                dn = dy * hg * silu_g
                dos.append(r * dn - o * (r * r * r * jnp.mean(o * dn, axis=1, keepdims=True)))
            dhgn_ref[...] += dhgn
            pre = []
            for (_, _, rows, hs), do in zip(units, dos):
                q, iv = q_ref[rows, hs], i_ref[rows, hs]
                sg, f, kk, sq = _hg_gates(q, f_ref[rows, hs], lb_ref[:, hs])
                pre.append((q, iv, sg, f, kk, sq, q * sq))
            decays = [_hg_chunk(None, None, jnp.log(f), incl) for _, _, _, f, _, _, _ in pre]
            prods = [(qs * e_t, kk * e_s, qs * e_i, kk * e_e)
                     for (_, _, _, _, kk, _, qs), (_, e_t, e_s, e_i, e_e) in zip(pre, decays)]
            ps = [jnp.where(causal, _dot(qd, kd, 1, 1), 0.0) for qd, kd, _, _ in prods]
            dps = [jnp.where(causal, _dot(do, iv, 1, 1), 0.0) for do, (_, iv, _, _, _, _, _) in zip(dos, pre)]
            dqds = [_dot(dp, kd, 1, 0) for dp, (_, kd, _, _) in zip(dps, prods)]
            dkds = [_dot(dp, qd, 0, 0) for dp, (qd, _, _, _) in zip(dps, prods)]
            pdos = [_dot(p, do, 0, 0) for p, do in zip(ps, dos)]
            chain = []
            for (c, hh, rows, hs), do, (_, iv, _, _, _, _, _), (bl, _, _, _, _), (_, _, qi, ke), pdo in zip(
                    units, dos, pre, decays, prods, pdos):
                st = st_ref[0, hh, c]
                dst = dstate[hh]
                ebl = jnp.exp(bl)
                dqi = _dot(do, st, 1, 0)
                di_s[rows, hs] = pdo + _dot(ke, dst, 1, 1)
                dke = _dot(iv, dst, 1, 0)
                dbl = jnp.sum(st * dst, axis=0, keepdims=True) * ebl + jnp.sum(dke * ke, axis=0, keepdims=True)
                dstate[hh] = _dot(do, qi, 0, 0) + dst * ebl
                chain.append((dqi, dke, dbl))
            for (_, _, rows, hs), (q, _, sg, f, _, sq, _), (_, e_t, e_s, e_i, e_e), (qd, kd, qi, ke), dqd, dkd, (
                    dqi, dke, dbl) in zip(units, pre, decays, prods, dqds, dkds, chain):
                lbv = lb_ref[:, hs]
                db = dqd * qd - dkd * kd + dqi * qi - dke * ke + jnp.where(last_row, dbl, 0.0)
                dqs = dqd * e_t + dqi * e_i
                dkk = dkd * e_s + dke * e_e
                dlf = _dot_exact_r(incl, db, cm=0)
                oms = 1.0 - sg
                dfd = dlf / f
                df_s[rows, hs] = (dfd - dkk) * (1.0 - lbv) * sg * oms
                dq_s[rows, hs] = dqs * (sq * (1.0 + q * (1.0 - sq)))
                dlb_ref[:, hs] += jnp.sum((dfd - dkk) * oms, axis=0, keepdims=True)
            return carry

        lax.fori_loop(0, NC // HG_TRIP, trip, 0)
        r0 = pl.multiple_of(b * S, S)
        copies = []
        for n, buf in enumerate((dq_s, df_s, di_s, dg_s)):
            c0 = pl.multiple_of((col0 + n * nstep + h) * width, width)
            copies.append(pltpu.make_async_copy(buf, dp_hbm.at[pl.ds(r0, S), pl.ds(c0, width)], sems.at[n]))
        for cp in copies:
            cp.start()
        for cp in copies:
            cp.wait()

    width = HG_STEP * LANES
    col0 = HG_COL0 * LANES // width
    nstep = HG_HEADS // HG_STEP

    def spec(off):
        return pl.BlockSpec((S, width), lambda h, b: (b, h + off))

    return pl.pallas_call(
        body, name=name, grid=(nstep, B),
        in_specs=[spec(col0), spec(col0 + nstep), spec(col0 + 2 * nstep), spec(col0 + 3 * nstep), spec(0), spec(nstep),
                  pl.BlockSpec((1, HG_STEP, NC, LANES, LANES), lambda h, b: (b, h, 0, 0, 0)),
                  pl.BlockSpec((1, width), lambda h, b: (0, h)), pl.BlockSpec((1, LANES), lambda h, b: (0, 0)), ANY],
        out_specs=[ANY, pl.BlockSpec((1, width), lambda h, b: (0, h)), pl.BlockSpec((1, LANES), lambda h, b: (0, 0))],
        out_shape=[jax.ShapeDtypeStruct(dproj.shape, F32), jax.ShapeDtypeStruct((1, HG_HEADS * LANES), F32),
                   jax.ShapeDtypeStruct((1, LANES), F32)],
        scratch_shapes=[pltpu.VMEM((HG_STEP, LANES, LANES), F32)] + [pltpu.VMEM((S, width), F32)] * 4
        + [pltpu.SemaphoreType.DMA((4,))],
        input_output_aliases={9: 0},
        compiler_params=_cparams("arbitrary", "arbitrary"))(proj, proj, proj, proj, oraw, dcat, states, lb, hgn, dproj)


def _lower_bound_fwd(logits, name):
    assert logits.shape[0] == 2

    def body(l_ref, o_ref):
        l0, l1 = l_ref[0:1, :], l_ref[1:2, :]
        m = jnp.maximum(l0, l1)
        e0, e1 = jnp.exp(l0 - m), jnp.exp(l1 - m)
        o_ref[0:1, :] = jnp.zeros_like(l0)
        o_ref[1:2, :] = e1 / (e0 + e1)

    return pl.pallas_call(body, name=name, out_shape=jax.ShapeDtypeStruct(logits.shape, F32))(logits)


def _lower_bound_bwd(logits, dlb, name):
    def body(l_ref, d_ref, o_ref):
        l0, l1 = l_ref[0:1, :], l_ref[1:2, :]
        m = jnp.maximum(l0, l1)
        e0, e1 = jnp.exp(l0 - m), jnp.exp(l1 - m)
        s1 = e1 / (e0 + e1)
        t = s1 * (1.0 - s1) * d_ref[1:2, :]
        o_ref[0:1, :] = -t
        o_ref[1:2, :] = t

    return pl.pallas_call(body, name=name, out_shape=jax.ShapeDtypeStruct(logits.shape, F32))(logits, dlb)


def _bucket_thresholds():
    dist = np.arange(WINDOW)
    max_exact = N_BUCKETS // 2
    large = max_exact + (np.log(np.maximum(dist, max_exact) / max_exact) / math.log(MAX_DISTANCE / max_exact)
                         * (N_BUCKETS - max_exact)).astype(np.int32)
    bucket = np.where(dist < max_exact, dist, np.minimum(large, N_BUCKETS - 1))
    assert np.all(np.diff(bucket) >= 0)
    return [int(np.argmax(bucket >= k)) if np.any(bucket >= k) else 10 ** 6 for k in range(1, N_BUCKETS)]


def _band_bucket():
    dist = _iota2((WINDOW, 2 * WINDOW), 0) + WINDOW - _iota2((WINDOW, 2 * WINDOW), 1)
    bucket = jnp.zeros((WINDOW, 2 * WINDOW), jnp.int32)
    for thr in _bucket_thresholds():
        bucket = bucket + (dist >= thr).astype(jnp.int32)
    band = jnp.logical_and(dist >= 0, dist < WINDOW)
    return bucket, band


def _bias_build(rel_bias, name):
    def body(rb_ref, o_ref):
        h = pl.program_id(0)
        bucket, _ = _band_bucket()
        bias = jnp.zeros((WINDOW, 2 * WINDOW), F32)
        for k in range(N_BUCKETS):
            bias = jnp.where(bucket == k, rb_ref[k, h], bias)
        o_ref[0] = bias

    return pl.pallas_call(
        body, name=name, grid=(SW_HEADS,), in_specs=[pl.BlockSpec(memory_space=pltpu.SMEM)],
        out_specs=pl.BlockSpec((1, WINDOW, 2 * WINDOW), lambda h: (h, 0, 0)),
        out_shape=jax.ShapeDtypeStruct((SW_HEADS, WINDOW, 2 * WINDOW), F32),
        compiler_params=_cparams("parallel"))(rel_bias)


def _bias_reduce(dbias, name):
    def body(d_ref, o_ref):
        bucket, band = _band_bucket()
        d = jnp.where(band, d_ref[0], 0.0)
        lane = _iota2((1, LANES), 1)
        out = jnp.zeros((1, LANES), F32)
        for k in range(N_BUCKETS):
            out = jnp.where(lane == k, jnp.sum(jnp.where(bucket == k, d, 0.0)), out)
        o_ref[0] = out

    return pl.pallas_call(
        body, name=name, grid=(SW_HEADS,), in_specs=[pl.BlockSpec((1, WINDOW, 2 * WINDOW), lambda h: (h, 0, 0))],
        out_specs=pl.BlockSpec((1, 1, LANES), lambda h: (h, 0, 0)),
        out_shape=jax.ShapeDtypeStruct((SW_HEADS, 1, LANES), F32), compiler_params=_cparams("parallel"))(dbias)


SW_Q_COLS = SW_HEADS * SW_DIM
SW_K_BLOCK0 = SW_Q_COLS // LANES
SW_V_BLOCK0 = SW_K_BLOCK0 + SW_KV * SW_DIM // LANES
SW_STEP_HEADS = 8
SW_TRIP = 1


def _swa_logits(qb, kprev, kcur, bias_ref, hl, mprev, mcur):
    scale = SW_DIM ** -0.5
    lp = jnp.where(mprev, _dot(qb, kprev, 1, 1) * scale + bias_ref[hl, :, 0:WINDOW], NEG)
    lc = jnp.where(mcur, _dot(qb, kcur, 1, 1) * scale + bias_ref[hl, :, WINDOW:2 * WINDOW], NEG)
    return lp, lc


def _swa_softmax(lp, lc, sink):
    m = jnp.maximum(jnp.maximum(jnp.max(lp, axis=1, keepdims=True), jnp.max(lc, axis=1, keepdims=True)), sink)
    ep, ec = jnp.exp(lp - m), jnp.exp(lc - m)
    es = jnp.exp(sink - m)
    den = jnp.sum(ep, axis=1, keepdims=True) + jnp.sum(ec, axis=1, keepdims=True) + es
    return ep, ec, es, den


def _swa_block_heads(n, q_ref, k_ref, v_ref, bias_ref, sink_ref, kp, above, mcur):
    W = WINDOW
    rows = pl.ds(pl.multiple_of(n * W, W), W)
    prow = pl.ds(pl.multiple_of(jnp.maximum(n - 1, 0) * W, W), W)
    mprev = jnp.logical_and(above, n > 0)
    heads = []
    for kvh in range(2):
        ksl = slice(kvh * SW_DIM, (kvh + 1) * SW_DIM)
        kv = (k_ref[rows, ksl], k_ref[prow, ksl], v_ref[rows, ksl], v_ref[prow, ksl])
        for g in range(4):
            hl = kvh * 4 + g
            heads.append((hl, ksl, slice(hl * SW_DIM, (hl + 1) * SW_DIM), kv))
    qbs = [q_ref[rows, qsl] for _, _, qsl, _ in heads]
    logits = [_swa_logits(qb, kv[1], kv[0], bias_ref, hl, mprev, mcur) for qb, (hl, _, _, kv) in zip(qbs, heads)]
    soft = [_swa_softmax(lp, lc, sink_ref[kp * SW_STEP_HEADS + hl]) for (lp, lc), (hl, _, _, _) in zip(logits, heads)]
    return rows, prow, heads, qbs, soft


def _swa_fwd(qkn, proj, bias, sinks, S, name, exchange=None):
    T = qkn.shape[0]
    W = WINDOW
    nb = S // W

    grid = (T // S, 2)

    def body(*refs):
        (q_ref, k_ref, v_ref, bias_ref, sink_ref, o_ref), carried = _carried(exchange, refs, 5, 1, 0)
        _start_carried(exchange, carried, grid)
        kp = pl.program_id(1)
        row, col = _iota2((W, W), 0), _iota2((W, W), 1)
        mcur = col <= row
        above = col > row

        def blk(m, carry):
            staged = [_swa_block_heads(m * SW_TRIP + u, q_ref, k_ref, v_ref, bias_ref, sink_ref, kp, above, mcur)
                      for u in range(SW_TRIP)]
            for rows, _, heads, _, soft in staged:
                outs = [(_dot(ep, kv[3], 1, 0) + _dot(ec, kv[2], 1, 0)) / den
                        for (ep, ec, _, den), (_, _, _, kv) in zip(soft, heads)]
                for (_, _, qsl, _), o in zip(heads, outs):
                    o_ref[rows, qsl] = o
            return carry

        lax.fori_loop(0, nb // SW_TRIP, blk, 0)
        _wait_carried(exchange, carried, grid)

    (o,), targets = _carrier_call(
        body, name, grid,
        [pl.BlockSpec((S, 4 * LANES), lambda b, kp: (b, kp)),
         pl.BlockSpec((S, LANES), lambda b, kp: (b, SW_K_BLOCK0 + kp)),
         pl.BlockSpec((S, LANES), lambda b, kp: (b, SW_V_BLOCK0 + kp)),
         pl.BlockSpec((SW_STEP_HEADS, W, 2 * W), lambda b, kp: (kp, 0, 0)),
         pl.BlockSpec(memory_space=pltpu.SMEM)],
        [pl.BlockSpec((S, 4 * LANES), lambda b, kp: (b, kp))], [jax.ShapeDtypeStruct((T, SW_Q_COLS), F32)],
        [], {}, (qkn, qkn, proj, bias, sinks), exchange)
    return o, targets


def _swa_bwd(qkn, proj, bias, sinks, do, S, name, exchange=None):
    T, width = proj.shape
    W = WINDOW
    nb = S // W
    scale = SW_DIM ** -0.5

    grid = (2, T // S)

    def body(*refs):
        (q_ref, k_ref, v_ref, bias_ref, sink_ref, do_ref, dp_hbm, dbias_ref, dsink_ref,
         dq_s, dk_s, dv_s, sems), carried = _carried(exchange, refs, 6, 3, 4)
        _start_carried(exchange, carried, grid)
        kp, b = pl.program_id(0), pl.program_id(1)
        row, col = _iota2((W, W), 0), _iota2((W, W), 1)
        mcur = col <= row
        above = col > row
        dk_s[...] = jnp.zeros_like(dk_s)
        dv_s[...] = jnp.zeros_like(dv_s)

        @pl.when(b == 0)
        def _():
            dbias_ref[...] = jnp.zeros_like(dbias_ref)
            dsink_ref[...] = jnp.zeros_like(dsink_ref)

        def blk(m, carry):
            staged = [_swa_block_heads(m * SW_TRIP + u, q_ref, k_ref, v_ref, bias_ref, sink_ref, kp, above, mcur)
                      for u in range(SW_TRIP)]
            for rows, prow, heads, qbs, soft in staged:
                one_block(rows, prow, heads, qbs, soft)
            return carry

        def one_block(rows, prow, heads, qbs, soft):
            dobs = [do_ref[rows, qsl] for _, _, qsl, _ in heads]
            dps = [(_dot(dob, kv[3], 1, 1), _dot(dob, kv[2], 1, 1)) for dob, (_, _, _, kv) in zip(dobs, heads)]
            grads = []
            for (hl, _, _, _), (ep, ec, es, den), (dpp, dpc) in zip(heads, soft, dps):
                inv = 1.0 / den
                pp, pc = ep * inv, ec * inv
                total = jnp.sum(pp * dpp, axis=1, keepdims=True) + jnp.sum(pc * dpc, axis=1, keepdims=True)
                dlp = pp * (dpp - total)
                dlc = pc * (dpc - total)
                dsink_ref[hl:hl + 1, :] += jnp.zeros((1, LANES), F32) - jnp.sum(es * inv * total)
                dbias_ref[hl, :, 0:W] += dlp
                dbias_ref[hl, :, W:2 * W] += dlc
                grads.append((pp, pc, dlp, dlc))
            for (_, _, qsl, kv), (_, _, dlp, dlc) in zip(heads, grads):
                dq_s[rows, qsl] = (_dot(dlp, kv[1], 1, 0) + _dot(dlc, kv[0], 1, 0)) * scale
            for kvh in range(2):
                group = range(4 * kvh, 4 * kvh + 4)
                ksl = heads[4 * kvh][1]
                dk_s[prow, ksl] += sum(_dot(grads[i][2], qbs[i], 0, 0) for i in group) * scale
                dk_s[rows, ksl] += sum(_dot(grads[i][3], qbs[i], 0, 0) for i in group) * scale
                dv_s[prow, ksl] += sum(_dot(grads[i][0], dobs[i], 0, 0) for i in group)
                dv_s[rows, ksl] += sum(_dot(grads[i][1], dobs[i], 0, 0) for i in group)

        lax.fori_loop(0, nb // SW_TRIP, blk, 0)
        r0 = pl.multiple_of(b * S, S)
        cq =pl.multiple_of(kp * 4 * LANES, LANES)
        ck = pl.multiple_of((SW_K_BLOCK0 + kp) * LANES, LANES)
        cv = pl.multiple_of((SW_V_BLOCK0 + kp) * LANES, LANES)
        copies = [pltpu.make_async_copy(dq_s, dp_hbm.at[pl.ds(r0, S), pl.ds(cq, 4 * LANES)], sems.at[0]),
                  pltpu.make_async_copy(dk_s, dp_hbm.at[pl.ds(r0, S), pl.ds(ck, LANES)], sems.at[1]),
                  pltpu.make_async_copy(dv_s, dp_hbm.at[pl.ds(r0, S), pl.ds(cv, LANES)], sems.at[2])]
        for cp in copies:
            cp.start()
        for cp in copies:
            cp.wait()
        _wait_carried(exchange, carried, grid)

    qspec = pl.BlockSpec((S, 4 * LANES), lambda kp, b: (b, kp))
    outs, targets = _carrier_call(
        body, name, grid,
        [qspec, pl.BlockSpec((S, LANES), lambda kp, b: (b, SW_K_BLOCK0 + kp)),
         pl.BlockSpec((S, LANES), lambda kp, b: (b, SW_V_BLOCK0 + kp)),
         pl.BlockSpec((SW_STEP_HEADS, W, 2 * W), lambda kp, b: (kp, 0, 0)),
         pl.BlockSpec(memory_space=pltpu.SMEM), qspec],
        [ANY, pl.BlockSpec((SW_STEP_HEADS, W, 2 * W), lambda kp, b: (kp, 0, 0)),
         pl.BlockSpec((SW_STEP_HEADS, LANES), lambda kp, b: (kp, 0))],
        [jax.ShapeDtypeStruct((T, width), F32), jax.ShapeDtypeStruct((SW_HEADS, W, 2 * W), F32),
         jax.ShapeDtypeStruct((SW_HEADS, LANES), F32)],
        [pltpu.VMEM((S, 4 * LANES), F32), pltpu.VMEM((S, LANES), F32), pltpu.VMEM((S, LANES), F32),
         pltpu.SemaphoreType.DMA((3,))], {}, (qkn, qkn, proj, bias, sinks, do), exchange)
    return (*outs, targets)


CHIP_FLIPS = ((1, 0), (0, 1), (1, 1))


def _flip(v, f):
    return 1 - v if f else v


class _Exchange:
    def __init__(self, sources, targets, copies):
        self.sources, self.targets = list(sources), list(targets)
        self._copies = copies
        n = len(self.sources)
        self.scratch = [pltpu.SemaphoreType.DMA((n, 3)), pltpu.SemaphoreType.DMA((n, 3)),
                        pltpu.SemaphoreType.DMA((n,))]

    def _descriptors(self, srcs, dsts, sems, chip, peers):
        send, recv, loc = sems
        out = []
        for t, (local, remote) in enumerate(self._copies(srcs, dsts, chip)):
            out.append(pltpu.make_async_copy(local[0], local[1], loc.at[t]))
            for r, (src, dst) in enumerate(remote):
                out.append(pltpu.make_async_remote_copy(src, dst, send.at[t, r], recv.at[t, r],
                                                        device_id=peers[r], device_id_type=MESH))
        return out

    def start(self, srcs, dsts, sems):
        x, y, c = lax.axis_index("x"), lax.axis_index("y"), lax.axis_index("c")
        peers = [(_flip(x, fx), _flip(y, fy), c) for fx, fy in CHIP_FLIPS]
        for chip in range(4):
            @pl.when(2 * x + y == chip)
            def _():
                for cp in self._descriptors(srcs, dsts, sems, chip, peers):
                    cp.start()

    def wait(self, srcs, dsts, sems):
        me = (lax.axis_index("x"), lax.axis_index("y"), lax.axis_index("c"))
        for cp in self._descriptors(srcs, dsts, sems, 0, [me] * 3):
            cp.wait()

    def operands(self):
        return self.sources + self.targets

    def specs(self):
        ns, nt = len(self.sources), len(self.targets)
        return [ANY] * (ns + nt), [ANY] * nt, [jax.ShapeDtypeStruct(t.shape, t.dtype) for t in self.targets]

    def aliases(self, n_in, n_out):
        ns = len(self.sources)
        return {n_in + ns + i: n_out + i for i in range(len(self.targets))}

    def split(self, refs, n_in, n_out, n_scr):
        ns, nt = len(self.sources), len(self.targets)
        o0 = n_in + ns + nt
        s0 = o0 + n_out + nt
        own = list(refs[:n_in]) + list(refs[o0:o0 + n_out]) + list(refs[s0:s0 + n_scr])
        return own, (refs[n_in:n_in + ns], refs[o0 + n_out:o0 + n_out + nt], refs[s0 + n_scr:])


def _carried(exchange, refs, n_in, n_out, n_scr):
    if exchange is None:
        return list(refs), None
    return exchange.split(refs, n_in, n_out, n_scr)


def _grid_edge(grid, last):
    conds = [pl.program_id(d) == (n - 1 if last else 0) for d, n in enumerate(grid)]
    out = conds[0]
    for cnd in conds[1:]:
        out = jnp.logical_and(out, cnd)
    return out


def _start_carried(exchange, parts, grid):
    if parts is not None:
        @pl.when(_grid_edge(grid, False))
        def _():
            exchange.start(*parts)


def _wait_carried(exchange, parts, grid):
    if parts is not None:
        @pl.when(_grid_edge(grid, True))
        def _():
            exchange.wait(*parts)


def _carrier_call(body, name, grid, in_specs, out_specs, out_shape, scratch, aliases, operands, exchange):
    n_out = len(out_shape)
    aliases = dict(aliases)
    if exchange is not None:
        ex_in, ex_out, ex_shape = exchange.specs()
        aliases.update(exchange.aliases(len(in_specs), n_out))
        in_specs, out_specs, out_shape = in_specs + ex_in, out_specs + ex_out, out_shape + ex_shape
        scratch = scratch + exchange.scratch
        operands = list(operands) + exchange.operands()
    outs = pl.pallas_call(body, name=name, grid=grid, in_specs=in_specs, out_specs=out_specs, out_shape=out_shape,
                          scratch_shapes=scratch, input_output_aliases=aliases,
                          compiler_params=_cparams(*["arbitrary"] * len(grid)))(*operands)
    return outs[:n_out], outs[n_out:]


def _exchange_call(exchange, name):
    in_specs, out_specs, out_shape = exchange.specs()

    def body(*refs):
        _, parts = exchange.split(refs, 0, 0, 0)
        exchange.start(*parts)
        exchange.wait(*parts)

    return pl.pallas_call(body, name=name, in_specs=in_specs, out_specs=out_specs, out_shape=out_shape,
                          scratch_shapes=exchange.scratch, input_output_aliases=exchange.aliases(0, 0))(
        *exchange.operands())


def _gather_exchange(shards, fulls, axes, layers):
    sizes = [s.shape[a] for s, a in zip(shards, axes)]

    def copies(srcs, dsts, chip):
        out = []
        for src, full, axis, size, (l0, l1) in zip(srcs, dsts, axes, sizes, layers):
            part = src.at[l0:l1]
            cut = pl.ds(chip * size, size)
            dst = full.at[l0:l1, cut, :] if axis == 1 else full.at[l0:l1, :, cut]
            out.append(((part, dst), [(part, dst)] * 3))
        return out

    return _Exchange(shards, fulls, copies)


def _scatter_exchange(grads, stacks, axes, layers):
    sizes = [g.shape[a - 1] // 4 for g, a in zip(grads, axes)]

    def copies(srcs, dsts, chip):
        out = []
        for g, stack, axis, size, layer in zip(srcs, dsts, axes, sizes, layers):
            def cut(j, g=g, axis=axis, size=size):
                return g.at[pl.ds(j * size, size), :] if axis == 1 else g.at[:, pl.ds(j * size, size)]

            remote = [(cut(chip ^ (2 * fx + fy)), stack.at[r, layer]) for r, (fx, fy) in enumerate(CHIP_FLIPS)]
            out.append(((cut(chip), stack.at[3, layer]), remote))
        return out

    return _Exchange(grads, stacks, copies)


def _swap_with_sibling(parts, name):
    n = len(parts)

    def body(*refs):
        ins, outs = refs[:n], refs[n:2 * n]
        send, recv = refs[2 * n:]
        peer = (lax.axis_index("x"), lax.axis_index("y"), 1 - lax.axis_index("c"))
        copies = [pltpu.make_async_remote_copy(ins[t], outs[t], send.at[t], recv.at[t], device_id=peer,
                                               device_id_type=MESH) for t in range(n)]
        for cp in copies:
            cp.start()
        for cp in copies:
            cp.wait()

    return pl.pallas_call(
        body, name=name, in_specs=[ANY] * n, out_specs=[ANY] * n,
        out_shape=[jax.ShapeDtypeStruct(p.shape, p.dtype) for p in parts],
        scratch_shapes=[pltpu.SemaphoreType.DMA((n,)), pltpu.SemaphoreType.DMA((n,))])(*parts)


def _allreduce_small(v, name):
    R = v.shape[0]
    ND = 8

    def body(v_ref, o_ref, buf, send, recv):
        x, y, c = lax.axis_index("x"), lax.axis_index("y"), lax.axis_index("c")
        me = 4 * x + 2 * y + c
        copies = []
        for d in range(1, ND):
            peer = (_flip(x, d >> 2 & 1), _flip(y, d >> 1 & 1), _flip(c, d & 1))
            copies.append(pltpu.make_async_remote_copy(v_ref, buf.at[me], send.at[d], recv.at[me], device_id=peer,
                                                       device_id_type=MESH))
        for cp in copies:
            cp.start()
        buf[pl.ds(me, 1)] = v_ref[...][None]
        for k in range(ND):
            @pl.when(me != k)
            def _():
                pltpu.make_async_remote_copy(v_ref, buf.at[k], send.at[0], recv.at[k], device_id=(x, y, c),
                                             device_id_type=MESH).wait_recv()
        for cp in copies:
            cp.wait_send()
        total = buf[0]
        for k in range(1, ND):
            total = total + buf[k]
        o_ref[...] = total

    vm = pl.BlockSpec(memory_space=pltpu.VMEM)
    return pl.pallas_call(
        body, name=name, in_specs=[vm], out_specs=vm, out_shape=jax.ShapeDtypeStruct((R, LANES), F32),
        scratch_shapes=[pltpu.VMEM((ND, R, LANES), F32), pltpu.SemaphoreType.DMA((ND,)),
                        pltpu.SemaphoreType.DMA((ND,))],
        compiler_params=pltpu.CompilerParams(vmem_limit_bytes=VMEM_LIMIT))(v)


def _tile2(R, Cn):
    tc = _pick(Cn, 2048)
    tr = R
    for cand in (256, 128, 64, 32, 16, 8):
        if R % cand == 0:
            tr = cand
            break
    return tr, tc


def _sum4(stack, name):
    _, R, Cn = stack.shape
    tr, tc = _tile2(R, Cn)

    def body(s_ref, o_ref):
        o_ref[...] = ((s_ref[0].astype(F32) + s_ref[1].astype(F32)) + s_ref[2].astype(F32)) + s_ref[3].astype(F32)

    return pl.pallas_call(
        body, name=name, grid=(R // tr, Cn // tc), in_specs=[pl.BlockSpec((4, tr, tc), lambda i, j: (0, i, j))],
        out_specs=pl.BlockSpec((tr, tc), lambda i, j: (i, j)), out_shape=jax.ShapeDtypeStruct((R, Cn), F32),
        compiler_params=_cparams("parallel", "parallel"))(stack)


def _adamw(w, m, v, g_parts, name):
    R, Cn = w.shape
    tr, tc = _tile2(R, Cn)
    npart = len(g_parts)
    c1 = 1.0 / (1.0 - ADAM_B1 ** ADAM_STEP)
    c2 = 1.0 / (1.0 - ADAM_B2 ** ADAM_STEP)

    def body(*refs):
        w_ref, m_ref, v_ref = refs[:3]
        g_refs = refs[3:3 + npart]
        g_out, d_out, m_out, v_out = refs[3 + npart:]
        g = g_refs[0][...]
        for r in g_refs[1:]:
            g = g + r[...]
        mn = ADAM_B1 * m_ref[...] + (1.0 - ADAM_B1) * g
        vn = ADAM_B2 * v_ref[...] + (1.0 - ADAM_B2) * (g * g)
        g_out[...] = g
        m_out[...] = mn
        v_out[...] = vn
        d_out[...] = -ADAM_LR * ((mn * c1) / (jnp.sqrt(vn * c2) + ADAM_EPS) + ADAM_WD * w_ref[...])

    spec = pl.BlockSpec((tr, tc), lambda i, j: (i, j))
    return pl.pallas_call(
        body, name=name, grid=(R // tr, Cn // tc), in_specs=[spec] * (3 + npart), out_specs=[spec] * 4,
        out_shape=[jax.ShapeDtypeStruct((R, Cn), F32)] * 4,
        compiler_params=_cparams("parallel", "parallel"))(w, m, v, *g_parts)


SHARDED = (("ab_w_in", 2), ("ab_w_out", 1), ("c_w_in", 2), ("c_w_out", 1), ("ffn_up", 2), ("ffn_conv", 2),
           ("ffn_down", 1), ("ple_gate", 1), ("ple_proj", 2))
SMALL = ("mix_norm", "hg_lb_logits", "hg_out_norm", "q_norm", "k_norm", "sinks", "rel_bias", "ffn_norm",
         "ffn_conv_b", "ple_norm")
WEIGHTS = ("mix_norm", "ab_w_in", "hg_lb_logits", "hg_out_norm", "ab_w_out", "c_w_in", "q_norm", "k_norm", "sinks",
           "rel_bias", "c_w_out", "ffn_norm", "ffn_up", "ffn_conv", "ffn_conv_b", "ffn_down", "ple_norm", "ple_gate",
           "ple_proj")
PACK_ALIGN = 8 * LANES


def _pack(arrs):
    pieces = []
    for a in arrs:
        flat = a.reshape(-1)
        pad = -flat.shape[0] % PACK_ALIGN
        pieces.append(jnp.pad(flat, (0, pad)).reshape(-1, LANES))
    return jnp.concatenate(pieces, axis=0)


def _unpack(packed, like):
    out, r = [], 0
    for a in like:
        size = int(np.prod(a.shape))
        rows = (size + PACK_ALIGN - 1) // PACK_ALIGN * 8
        out.append(packed[r:r + rows].reshape(-1)[:size].reshape(a.shape))
        r += rows
    return out


def _family_range(name, lo, hi):
    if name.startswith("ab_"):
        idx = [i // 2 for i in range(lo, hi) if i % 2 == 0]
    elif name.startswith("c_"):
        idx = [i // 2 for i in range(lo, hi) if i % 2 == 1]
    else:
        idx = list(range(lo, hi))
    return (idx[0], idx[-1] + 1) if idx else None


W_IN = ("ab_w_in", "c_w_in")
REST = tuple(k for k, _ in SHARDED if k not in W_IN)
MIXER = ("ab_w_in", "ab_w_out", "c_w_in", "c_w_out")
FFN = ("ffn_up", "ffn_conv", "ffn_down")
PLE = ("ple_gate", "ple_proj")
GATHER_PLAN = {
    "gather_first": [(0, W_IN)],
    "ab_in_0": [(0, ("ab_w_out",) + PLE)],
    "sb_fwd_0": [(0, FFN), (1, ("ffn_up", "ffn_conv"))],
    "hgrn_fwd_0": [(1, ("c_w_in",))],
    "ffn_up_0": [(1, ("ffn_down", "c_w_out"))],
    "ffn_down_0": [(1, PLE)],
    "swa_fwd_1": [(2, ("ab_w_in",))],
    "ffn_up_1": [(2, ("ab_w_out", "ffn_down"))],
    "ffn_down_1": [(2, PLE)],
    "sb_fwd_2": [(2, ("ffn_up", "ffn_conv")), (3, ("c_w_in", "c_w_out", "ffn_up", "ffn_conv"))],
    "hgrn_fwd_2": [(3, ("ffn_down",))],
    "ffn_up_2": [(3, PLE)],
}
SCATTER_PLAN = {
    "d_ffn_up_3": [(3, ("ffn_down", "ffn_conv"))],
    "d_hn2_3": [(3, PLE)],
    "swa_bwd_3": [(3, ("ffn_up", "c_w_out"))],
    "sb_bwd_2": [(3, W_IN), (2, REST)],
    "convglu_bwd_1": [(2, W_IN)],
    "d_ffn_up_1": [(1, ("ffn_down", "ffn_conv"))],
    "d_hn2_1": [(1, PLE)],
    "swa_bwd_1": [(1, ("ffn_up", "c_w_out"))],
    "sb_bwd_0": [(1, W_IN), (0, REST)],
    "d_hn_0": [(0, W_IN)],
}


class _StepExchanges:
    def __init__(self, shards, W):
        self.shards, self.W = shards, W
        self.stacks = {}
        for k, axis in SHARDED:
            shp = shards[k].shape
            self.W[k] = lax.empty(tuple(4 * d if i == axis else d for i, d in enumerate(shp)), shards[k].dtype)
            self.stacks[k] = lax.empty((4,) + shp, shards[k].dtype)

    @staticmethod
    def _select(plan):
        idx = {}
        for layer, fams in plan:
            for k, _ in SHARDED:
                r = _family_range(k, layer, layer + 1)
                if r is not None and (fams is None or k in fams):
                    idx.setdefault(k, []).append(r[0])
        return [(k, axis, sorted(idx[k])) for k, axis in SHARDED if k in idx]

    def gather(self, call):
        sel = self._select(GATHER_PLAN.get(call, ()))
        if not sel:
            return None, None
        for _, _, ii in sel:
            assert ii == list(range(ii[0], ii[-1] + 1)), "one copy per family takes a contiguous layer range"
        names = [k for k, _, _ in sel]
        ex = _gather_exchange([self.shards[k] for k in names], [self.W[k] for k in names],
                              [a for _, a, _ in sel], [(ii[0], ii[-1] + 1) for _, _, ii in sel])
        return ex, (self.W, names)

    def scatter(self, call, G):
        sel = self._select(SCATTER_PLAN.get(call, ()))
        if not sel:
            return None, None
        assert all(len(ii) == 1 for _, _, ii in sel)
        names = [k for k, _, _ in sel]
        ex = _scatter_exchange([G[k][ii[0]] for k, _, ii in sel], [self.stacks[k] for k in names],
                               [a for _, a, _ in sel], [ii[0] for _, _, ii in sel])
        return ex, (self.stacks, names)

    @staticmethod
    def adopt(where, targets):
        if where is not None:
            book, names = where
            for k, t in zip(names, targets):
                book[k] = t


def _forward_backward(x, p, target, W, S, exchanges=None):
    T = x.shape[0]
    depth = p.shape[0]
    lb = _lower_bound_fwd(W["hg_lb_logits"], "lower_bound_fwd")
    bias = _bias_build(W["rel_bias"], "bias_build")
    qk_gain = jnp.concatenate([jnp.tile(W["q_norm"], (1, SW_HEADS)), jnp.tile(W["k_norm"], (1, SW_KV))], axis=1)

    def gathering(call):
        return exchanges.gather(call) if exchanges else (None, None)

    def scattering(call):
        return exchanges.scatter(call, G) if exchanges else (None, None)

    def matmul(a, b, mode, name, **kw):
        ex, where = (gathering(name) if name in GATHER_PLAN else
                     scattering(name) if name in SCATTER_PLAN else (None, None))
        out, moved = _matmul(a, b, mode, name, exchange=ex, **kw)
        _StepExchanges.adopt(where, moved)
        return out

    def mm(a, wname, layer, mode, name, **kw):
        return matmul(a, W[wname], mode, name, b_layer=layer, **kw)

    saved = []
    h = x
    for i in range(depth):
        j = i // 2
        s = {"h0": h}
        s["hn"] = _rmsnorm_fwd(h, W["mix_norm"][i:i + 1], f"mix_norm_fwd_{i}") if i == 0 else hn_next
        if i % 2 == 0:
            s["proj"] = mm(s["hn"], "ab_w_in", j, "nn", f"ab_in_{i}")
            ex, where = gathering(f"sb_fwd_{i}")
            cat, s["sb_tot"], arrived = _sb_fwd(s["proj"], S, f"sb_fwd_{i}", ex)
            _StepExchanges.adopt(where, arrived)
            ex, where = gathering(f"hgrn_fwd_{i}")
            s["cat"], s["oraw"], s["states"], arrived = _hgrn_fwd(s["proj"], cat, lb[j:j + 1],
                                                                  W["hg_out_norm"][j:j + 1], S, f"hgrn_fwd_{i}", ex)
            _StepExchanges.adopt(where, arrived)
            h, s["hn2"] = mm(s["cat"], "ab_w_out", j, "nn", f"ab_out_{i}", res=h, norm_gain=W["ffn_norm"][i:i + 1])
        else:
            s["proj"] = mm(s["hn"], "c_w_in", j, "nn", f"c_in_{i}")
            s["qkn"] = _headnorm_fwd(s["proj"], qk_gain[j:j + 1], f"qk_norm_fwd_{i}")
            ex, where = gathering(f"swa_fwd_{i}")
            s["o"], arrived = _swa_fwd(s["qkn"], s["proj"], bias, W["sinks"][j], S, f"swa_fwd_{i}", ex)
            _StepExchanges.adopt(where, arrived)
            h, s["hn2"] = mm(s["o"], "c_w_out", j, "nn", f"c_out_{i}", res=h, norm_gain=W["ffn_norm"][i:i + 1])
        s["h1"] = h
        s["u"] = mm(s["hn2"], "ffn_up", i, "nn", f"ffn_up_{i}", tiles=(min(T, 1024), D_FF // 2, D_MODEL))
        s["a"] = _convglu_fwd(s["u"], W["ffn_conv"][i], W["ffn_conv_b"][i:i + 1], S, f"convglu_fwd_{i}")
        h, s["hn3"] = mm(s["a"], "ffn_down", i, "nn", f"ffn_down_{i}", res=h, norm_gain=W["ple_norm"][i:i + 1])
        s["h2"] = h
        s["z"] = mm(s["hn3"], "ple_gate", i, "nn", f"ple_gate_{i}")
        s["pp"] = mm(p, "ple_proj", i, "nn", f"ple_proj_{i}", a_layer=i)
        h, hn_next = _ple_fwd(h, s["z"], s["pp"], W["mix_norm"][i + 1:i + 2] if i + 1 < depth else None,
                              f"ple_fwd_{i}")
        saved.append(s)

    loss, dh = _loss_fwd_bwd(h, target, "loss")

    G = {k: [None] * depth for k in ("mix_norm", "ffn_norm", "ple_norm", "ffn_up", "ffn_conv", "ffn_conv_b",
                                     "ffn_down", "ple_gate", "ple_proj")}
    for k in ("ab_w_in", "ab_w_out", "c_w_in", "c_w_out", "hg_out_norm", "q_norm", "k_norm", "sinks", "lb"):
        G[k] = [None] * (depth // 2)
    dbias_total = None
    for i in reversed(range(depth)):
        j = i // 2
        s = saved[i]
        dz, dpp = _ple_bwd(dh, s["z"], s["pp"], f"ple_bwd_{i}")
        G["ple_proj"][i] = matmul(p, dpp, "tn", f"d_ple_proj_{i}", out_dtype=BF16, a_layer=i)
        G["ple_gate"][i] = matmul(s["hn3"], dz, "tn", f"d_ple_gate_{i}", out_dtype=BF16)
        rt = min(T, 512)
        dh, G["ple_norm"][i] = mm(dz, "ple_gate", i, "nt", f"d_hn3_{i}", tiles=(rt, D_MODEL, D_MODEL),
                                  norm_bwd=(s["h2"], W["ple_norm"][i:i + 1], dh))

        half_ff = D_FF // 2
        da = mm(dh, "ffn_down", i, "nt", f"d_a_{i}", tiles=(min(T, 1024), half_ff, D_MODEL))
        G["ffn_down"][i] = matmul(s["a"], dh, "tn", f"d_ffn_down_{i}", out_dtype=BF16,
                                   tiles=(half_ff, 512, min(T, 2048)))
        ex, where = scattering(f"convglu_bwd_{i}")
        du, dcw, dcb, sent = _convglu_bwd(s["u"], da, W["ffn_conv"][i], W["ffn_conv_b"][i:i + 1], S,
                                          f"convglu_bwd_{i}", ex)
        _StepExchanges.adopt(where, sent)
        G["ffn_conv"][i] = jnp.swapaxes(dcw, 0, 1).reshape(3, 2 * D_FF)
        G["ffn_conv_b"][i] = dcb.reshape(1, 2 * D_FF)
        G["ffn_up"][i] = matmul(s["hn2"], du, "tn", f"d_ffn_up_{i}", out_dtype=BF16,
                                 tiles=(D_MODEL, half_ff, min(T, 1024)))
        dh, G["ffn_norm"][i] = mm(du, "ffn_up", i, "nt", f"d_hn2_{i}", tiles=(rt, D_MODEL, half_ff),
                                  norm_bwd=(s["h1"], W["ffn_norm"][i:i + 1], dh))

        if i % 2 == 0:
            dcat = mm(dh, "ab_w_out", j, "nt", f"d_cat_{i}")
            G["ab_w_out"][j] = matmul(s["cat"], dh, "tn", f"d_ab_out_{i}", out_dtype=BF16)
            ex, where = scattering(f"sb_bwd_{i}")
            dproj, sent = _sb_bwd(s["proj"], s["sb_tot"], dcat, S, f"sb_bwd_{i}", ex)
            _StepExchanges.adopt(where, sent)
            dproj, G["lb"][j], G["hg_out_norm"][j] = _hgrn_bwd(s["proj"], s["oraw"], dcat, s["states"], lb[j:j + 1],
                                                               W["hg_out_norm"][j:j + 1], dproj, S, f"hgrn_bwd_{i}")
            G["ab_w_in"][j] = matmul(s["hn"], dproj, "tn", f"d_ab_in_{i}", out_dtype=BF16)
            w_in, width = "ab_w_in", dproj.shape[1] // 2
        else:
            do = mm(dh, "c_w_out", j, "nt", f"d_o_{i}")
            G["c_w_out"][j] = matmul(s["o"], dh, "tn", f"d_c_out_{i}", out_dtype=BF16)
            ex, where = scattering(f"swa_bwd_{i}")
            dqkv, dbias, dsink, sent = _swa_bwd(s["qkn"], s["proj"], bias, W["sinks"][j], do, S, f"swa_bwd_{i}", ex)
            _StepExchanges.adopt(where, sent)
            dbias_total = dbias if dbias_total is None else dbias_total + dbias
            G["sinks"][j] = dsink[:, 0]
            dproj, dgain = _headnorm_bwd(s["proj"], qk_gain[j:j + 1], dqkv, f"qk_norm_bwd_{i}")
            G["q_norm"][j] = dgain[0, :SW_Q_COLS].reshape(SW_HEADS, SW_DIM).sum(axis=0)
            G["k_norm"][j] = dgain[0, SW_Q_COLS:].reshape(SW_KV, SW_DIM).sum(axis=0)
            G["c_w_in"][j] = matmul(s["hn"], dproj, "tn", f"d_c_in_{i}", out_dtype=BF16)
            w_in, width = "c_w_in", dproj.shape[1]
        dh, G["mix_norm"][i] = mm(dproj, w_in, j, "nt", f"d_hn_{i}", tiles=(rt, D_MODEL, width),
                                  norm_bwd=(s["h0"], W["mix_norm"][i:i + 1], dh))

    grads = {k: G[k] for k, _ in SHARDED}
    for k in ("q_norm", "k_norm", "sinks"):
        grads[k] = jnp.stack(G[k])
    for k in ("mix_norm", "ffn_norm", "ple_norm", "ffn_conv_b", "hg_out_norm"):
        grads[k] = jnp.concatenate(G[k], axis=0)
    grads["hg_lb_logits"] = _lower_bound_bwd(W["hg_lb_logits"], jnp.concatenate(G["lb"], axis=0), "lower_bound_bwd")
    grads["rel_bias"] = _bias_reduce(dbias_total, "bias_reduce")[:, 0, :N_BUCKETS].T
    return loss, dh, grads


def kernel(x, p, mix_norm, ab_w_in, hg_lb_logits, hg_out_norm, ab_w_out, c_w_in, q_norm, k_norm, sinks, rel_bias, c_w_out, ffn_norm, ffn_up, ffn_conv, ffn_conv_b, ffn_down, ple_norm, ple_gate, ple_proj, loss_target, m_mix_norm, m_ab_w_in, m_hg_lb_logits, m_hg_out_norm, m_ab_w_out, m_c_w_in, m_q_norm, m_k_norm, m_sinks, m_rel_bias, m_c_w_out, m_ffn_norm, m_ffn_up, m_ffn_conv, m_ffn_conv_b, m_ffn_down, m_ple_norm, m_ple_gate, m_ple_proj, v_mix_norm, v_ab_w_in, v_hg_lb_logits, v_hg_out_norm, v_ab_w_out, v_c_w_in, v_q_norm, v_k_norm, v_sinks, v_rel_bias, v_c_w_out, v_ffn_norm, v_ffn_up, v_ffn_conv, v_ffn_conv_b, v_ffn_down, v_ple_norm, v_ple_gate, v_ple_proj):
    args = dict(locals())
    w = {k: args[k] for k in WEIGHTS}
    m = {k: args["m_" + k] for k in WEIGHTS}
    v = {k: args["v_" + k] for k in WEIGHTS}
    B, S, Dm = x.shape
    T = B * S
    names = [k for k, _ in SHARDED]

    W = {k: w[k] for k in SMALL}
    exchanges = _StepExchanges({k: w[k].astype(F32 if k == "ffn_conv" else BF16) for k in names}, W)
    first, where = exchanges.gather("gather_first")
    exchanges.adopt(where, _exchange_call(first, "gather_first"))

    loss, dx, grads = _forward_backward(x.reshape(T, Dm), p.reshape(p.shape[0], T, p.shape[-1]),
                                        loss_target.reshape(T, Dm), W, S, exchanges)
    loss = lax.psum(loss[0, 0], ("x", "y", "c"))

    stacks = [exchanges.stacks[k] for k in names]
    partial = [_sum4(st.reshape(4, -1, st.shape[-1]), f"sum_chips_{k}") for k, st in zip(names, stacks)]
    other = _swap_with_sibling(partial, "swap_core_sums")
    small_sum = _allreduce_small(_pack([grads[k] for k in SMALL]), "allreduce_small")

    out_g, out_d, out_m, out_v = {}, {}, {}, {}
    for k, mine, theirs in zip(names, partial, other):
        shp = w[k].shape
        r = [a.reshape(shp) for a in _adamw(w[k].reshape(mine.shape), m[k].reshape(mine.shape),
                                            v[k].reshape(mine.shape), [mine, theirs], f"adamw_{k}")]
        out_g[k], out_d[k], out_m[k], out_v[k] = r
    sm = _adamw(_pack([w[k] for k in SMALL]), _pack([m[k] for k in SMALL]), _pack([v[k] for k in SMALL]),
                [small_sum], "adamw_small")
    like = [w[k] for k in SMALL]
    for dst, packed in zip((out_g, out_d, out_m, out_v), sm):
        for k, a in zip(SMALL, _unpack(packed, like)):
            dst[k] = a

    return (loss, dx.reshape(B, S, Dm), *[out_g[k] for k in WEIGHTS], *[out_d[k] for k in WEIGHTS],
            *[out_m[k] for k in WEIGHTS], *[out_v[k] for k in WEIGHTS])
```

```python
import math

import numpy as np
import jax
import jax.numpy as jnp
from jax import lax
from jax.experimental import pallas as pl
from jax.experimental.pallas import tpu as pltpu

F32 = jnp.float32
BF16 = jnp.bfloat16
MESH = pl.DeviceIdType.MESH
ANY = pl.BlockSpec(memory_space=pl.ANY)

D_MODEL = 1024
EPS = 1e-6
SB_HEADS, SB_DIM = 8, 64
HG_HEADS, HG_DK = 4, 128
HG_CHUNK = 32
HG_STEP = 2
HG_TRIP = 8
SW_HEADS, SW_KV, SW_DIM, WINDOW = 16, 4, 64, 128
N_BUCKETS, MAX_DISTANCE = 32, 128
D_FF = 2816
ATT_BLOCK = 128
SB_QBLOCK = 256
LANES = 128
NEG = -1e30

ADAM_LR, ADAM_B1, ADAM_B2, ADAM_EPS, ADAM_WD, ADAM_STEP = 0.001, 0.9, 0.999, 1e-08, 0.01, 10

VMEM_LIMIT = 56 * 1024 * 1024


def _cparams(*sem):
    return pltpu.CompilerParams(dimension_semantics=sem, vmem_limit_bytes=VMEM_LIMIT)


def _pick(n, cap):
    if n <= cap:
        return n
    best = None
    for d in range(LANES, cap + 1, LANES):
        if n % d == 0:
            best = d
    assert best is not None, (n, cap)
    return best


def _dot(a, b, ca, cb):
    return lax.dot_general(a.astype(BF16), b.astype(BF16), (((ca,), (cb,)), ((), ())),
                           preferred_element_type=F32)


def _split(x, terms):
    parts = []
    for _ in range(terms):
        hi = x.astype(BF16)
        parts.append(hi)
        x = x - hi.astype(F32)
    return parts


def _dot_exact_l(x, m, terms=2):
    out = None
    for p in _split(x, terms):
        t = lax.dot_general(p, m, (((1,), (0,)), ((), ())), preferred_element_type=F32)
        out = t if out is None else out + t
    return out


def _dot_exact_r(m, x, terms=3, cm=1):
    out = None
    for p in _split(x, terms):
        t = lax.dot_general(m, p, (((cm,), (0,)), ((), ())), preferred_element_type=F32)
        out = t if out is None else out + t
    return out


def _sig(x):
    return 1.0 / (1.0 + jnp.exp(-x))


def _iota2(shape, dim):
    return lax.broadcasted_iota(jnp.int32, shape, dim)


def _operand_spec(arr, layer, blk, index):
    if arr.ndim == 2:
        return pl.BlockSpec(blk, index)
    if layer is not None:
        return pl.BlockSpec((None,) + blk, lambda i, j, k: (layer,) + index(i, j, k))
    per_half = arr.shape[2] // blk[1]

    def halves(i, j, k):
        r, c = index(i, j, k)
        return (c // per_half, r, c % per_half)

    return pl.BlockSpec((None,) + blk, halves)


def _matmul(a, b, mode, name, out_dtype=F32, res=None, a_layer=None, b_layer=None, tiles=None, exchange=None,
            norm_gain=None, norm_bwd=None):
    def dims(arr, layer):
        if arr.ndim == 2:
            return arr.shape
        return arr.shape[1:] if layer is not None else (arr.shape[1], 2 * arr.shape[2])

    (a0, a1), (b0, b1) = dims(a, a_layer), dims(b, b_layer)
    if mode == "nn":
        M, K, N = a0, a1, b1
    elif mode == "nt":
        M, K, N = a0, a1, b0
    else:
        K, M, N = a0, a1, b1
    cap_m, cap_n, cap_k = 1024, 1024, (1024 if mode == "tn" else 2048)
    tm, tn, tk = _pick(M, cap_m), _pick(N, cap_n), _pick(K, cap_k)
    if a.ndim == 3 and a_layer is None:
        if mode == "tn":
            tm = _pick(a.shape[2], cap_m)
        else:
            tk = _pick(a.shape[2], cap_k)
    if b.ndim == 3 and b_layer is None:
        if mode == "nt":
            tk = _pick(b.shape[2], cap_k)
        else:
            tn = _pick(b.shape[2], cap_n)
    if tiles is not None:
        tm, tn, tk = tiles
    assert M % tm == 0 and N % tn == 0 and K % tk == 0, (name, M, N, K, tm, tn, tk)
    nk = K // tk
    if mode == "tn":
        a_spec = _operand_spec(a, a_layer, (tk, tm), lambda i, j, k: (k, i))
    else:
        a_spec = _operand_spec(a, a_layer, (tm, tk), lambda i, j, k: (i, k))
    if mode == "nt":
        b_spec = _operand_spec(b, b_layer, (tn, tk), lambda i, j, k: (j, k))
    else:
        b_spec = _operand_spec(b, b_layer, (tk, tn), lambda i, j, k: (k, j))
    ca, cb = {"nn": (1, 0), "nt": (1, 1), "tn": (0, 0)}[mode]
    o_spec = pl.BlockSpec((tm, tn), lambda i, j, k: (i, j))

    grid = (M // tm, N // tn, nk)
    normed = norm_gain is not None
    through = norm_bwd is not None
    assert not (normed or through) or tn == N, "the norm needs whole rows in one tile"
    assert not (through and (normed or res is not None))
    v_spec = pl.BlockSpec((1, tn), lambda i, j, k: (0, j))
    if through:
        x_in, gain_in, dres_in = norm_bwd
        extra_in, extra_specs = [x_in, gain_in, dres_in], [o_spec, v_spec, o_spec]
        out_specs = [o_spec, v_spec]
        out_shape = [jax.ShapeDtypeStruct((M, N), F32), jax.ShapeDtypeStruct((1, N), F32)]
    else:
        extra_in = ([] if res is None else [res]) + ([norm_gain] if normed else [])
        extra_specs = ([] if res is None else [o_spec]) + ([v_spec] if normed else [])
        out_specs = [o_spec] * (1 + normed)
        out_shape = ([jax.ShapeDtypeStruct((M, N), out_dtype)]
                     + ([jax.ShapeDtypeStruct((M, N), BF16)] if normed else []))
    n_in, n_out = 2 + len(extra_in), len(out_shape)

    def body(*refs):
        own, carried = _carried(exchange, refs, n_in, n_out, 1)
        _start_carried(exchange, carried, grid)
        a_ref, b_ref = own[:2]
        extras, outs, acc = own[2:n_in], own[n_in:n_in + n_out], own[-1]
        i, k = pl.program_id(0), pl.program_id(2)

        @pl.when(k == 0)
        def _():
            acc[...] = jnp.zeros_like(acc)

        acc[...] += _dot(a_ref[...], b_ref[...], ca, cb)

        @pl.when(k == nk - 1)
        def _():
            r = acc[...]
            if through:
                x, dy = extras[0][...], r
                scale = lax.rsqrt(jnp.mean(x * x, axis=1, keepdims=True) + EPS)
                gdy = dy * extras[1][...]
                m = jnp.mean(x * gdy, axis=1, keepdims=True)
                outs[0][...] = extras[2][...] + scale * gdy - x * (scale * scale * scale * m)
                part = jnp.sum(dy * x * scale, axis=0, keepdims=True)

                @pl.when(i == 0)
                def _():
                    outs[1][...] = part

                @pl.when(i > 0)
                def _():
                    outs[1][...] += part
            else:
                if res is not None:
                    r = r + extras[0][...]
                outs[0][...] = r.astype(out_dtype)
                if normed:
                    scale = lax.rsqrt(jnp.mean(r * r, axis=1, keepdims=True) + EPS)
                    outs[1][...] = (r * scale * extras[-1][...]).astype(BF16)

        _wait_carried(exchange, carried, grid)

    ins = [a, b] + extra_in
    in_specs = [a_spec, b_spec] + extra_specs
    scratch = [pltpu.VMEM((tm, tn), F32)]
    if exchange is None:
        sem = ("arbitrary" if through else "parallel", "parallel", "arbitrary")
        outs = pl.pallas_call(
            body, name=name, grid=grid, in_specs=in_specs, out_specs=out_specs, out_shape=out_shape,
            scratch_shapes=scratch, compiler_params=_cparams(*sem))(*ins)
        targets = ()
    else:
        outs, targets = _carrier_call(body, name, grid, in_specs, out_specs, out_shape, scratch, {}, ins, exchange)
    return (tuple(outs) if n_out > 1 else outs[0]), targets


ROW_TILE = 512
HEAD_ROWS = 2048


def _row_spec(width):
    return pl.BlockSpec((ROW_TILE, width), lambda i: (i, 0))


def _vec_spec(width):
    return pl.BlockSpec((1, width), lambda i: (0, 0))


def _rmsnorm_fwd(h, g, name):
    T, Dm = h.shape

    def body(h_ref, g_ref, o_ref):
        x = h_ref[...]
        r = lax.rsqrt(jnp.mean(x * x, axis=1, keepdims=True) + EPS)
        o_ref[...] = (x * r * g_ref[...]).astype(BF16)

    return pl.pallas_call(
        body, name=name, grid=(T // ROW_TILE,), in_specs=[_row_spec(Dm), _vec_spec(Dm)],
        out_specs=_row_spec(Dm), out_shape=jax.ShapeDtypeStruct((T, Dm), BF16),
        compiler_params=_cparams("parallel"))(h, g)


def _ple_fwd(h, z, pp, gain, name):
    T, Dm = h.shape
    normed = gain is not None

    def body(*refs):
        h_ref, z_ref, p_ref = refs[:3]
        y = h_ref[...] + _sig(z_ref[...]) * p_ref[...]
        refs[3 + normed][...] = y
        if normed:
            r = lax.rsqrt(jnp.mean(y * y, axis=1, keepdims=True) + EPS)
            refs[5][...] = (y * r * refs[3][...]).astype(BF16)

    outs = pl.pallas_call(
        body, name=name, grid=(T // ROW_TILE,), in_specs=[_row_spec(Dm)] * 3 + [_vec_spec(Dm)] * normed,
        out_specs=[_row_spec(Dm)] * (1 + normed),
        out_shape=[jax.ShapeDtypeStruct((T, Dm), F32)] + [jax.ShapeDtypeStruct((T, Dm), BF16)] * normed,
        compiler_params=_cparams("parallel"))(*([h, z, pp] + [gain] * normed))
    return outs if normed else (outs[0], None)


def _ple_bwd(dh, z, pp, name):
    T, Dm = dh.shape

    def body(dh_ref, z_ref, p_ref, dz_ref, dp_ref):
        s = _sig(z_ref[...])
        d = dh_ref[...]
        dz_ref[...] = d * p_ref[...] * s * (1.0 - s)
        dp_ref[...] = d * s

    return pl.pallas_call(
        body, name=name, grid=(T // ROW_TILE,), in_specs=[_row_spec(Dm)] * 3, out_specs=[_row_spec(Dm)] * 2,
        out_shape=[jax.ShapeDtypeStruct((T, Dm), F32)] * 2, compiler_params=_cparams("parallel"))(dh, z, pp)


def _loss_fwd_bwd(y, target, name):
    T, Dm = y.shape

    def body(y_ref, t_ref, l_ref, d_ref):
        i = pl.program_id(0)
        e = y_ref[...] - t_ref[...]
        d_ref[...] = e * (1.0 / Dm)
        part = jnp.full((8, LANES), 0.5 / Dm, F32) * jnp.sum(e * e)

        @pl.when(i == 0)
        def _():
            l_ref[...] = part

        @pl.when(i > 0)
        def _():
            l_ref[...] += part

    return pl.pallas_call(
        body, name=name, grid=(T // ROW_TILE,), in_specs=[_row_spec(Dm)] * 2,
        out_specs=[pl.BlockSpec((8, LANES), lambda i: (0, 0)), _row_spec(Dm)],
        out_shape=[jax.ShapeDtypeStruct((8, LANES), F32), jax.ShapeDtypeStruct((T, Dm), F32)],
        compiler_params=_cparams("arbitrary"))(y, target)


def _head_mean_matrix():
    r = _iota2((LANES, LANES), 0) >= SW_DIM
    c = _iota2((LANES, LANES), 1) >= SW_DIM
    return jnp.where(r == c, 1.0 / SW_DIM, 0.0).astype(BF16)


def _headnorm_fwd(x, g_lane, name):
    T = x.shape[0]
    C = g_lane.shape[1]

    def body(x_ref, g_ref, y_ref):
        xv = x_ref[...]
        ms = _dot_exact_l(xv * xv, _head_mean_matrix())
        y_ref[...] = xv * lax.rsqrt(ms + EPS) * g_ref[...]

    rows = min(T, HEAD_ROWS)
    spec = pl.BlockSpec((rows, LANES), lambda j, i: (i, j))
    return pl.pallas_call(
        body, name=name, grid=(C // LANES, T // rows),
        in_specs=[spec, pl.BlockSpec((1, LANES), lambda j, i: (0, j))], out_specs=spec,
        out_shape=jax.ShapeDtypeStruct((T, C), F32), compiler_params=_cparams("parallel", "parallel"))(x, g_lane)


def _headnorm_bwd(x, g_lane, dy_full, name):
    T = x.shape[0]
    C = g_lane.shape[1]

    def body(x_ref, g_ref, dy_ref, dx_ref, dg_ref):
        i = pl.program_id(1)
        xv = x_ref[...]
        dy = dy_ref[...]
        bd = _head_mean_matrix()
        r = lax.rsqrt(_dot_exact_l(xv * xv, bd) + EPS)
        gdy = dy * g_ref[...]
        m = _dot_exact_l(xv * gdy, bd)
        dx_ref[...] = r * gdy - xv * (r * r * r * m)
        part = jnp.sum(dy * xv * r, axis=0, keepdims=True)

        @pl.when(i == 0)
        def _():
            dg_ref[...] = part

        @pl.when(i > 0)
        def _():
            dg_ref[...] += part

    rows = min(T, HEAD_ROWS)
    spec = pl.BlockSpec((rows, LANES), lambda j, i: (i, j))
    vspec = pl.BlockSpec((1, LANES), lambda j, i: (0, j))
    return pl.pallas_call(
        body, name=name, grid=(C // LANES, T // rows), in_specs=[spec, vspec, spec],
        out_specs=[spec, vspec],
        out_shape=[jax.ShapeDtypeStruct(dy_full.shape, F32), jax.ShapeDtypeStruct((1, C), F32)],
        input_output_aliases={2: 0}, compiler_params=_cparams("parallel", "arbitrary"))(x, g_lane, dy_full)


CONV_TILE = 128


def _shift_down(x, k, rows):
    return jnp.where(rows >= k, pltpu.roll(x, k, 0), 0.0)


def _shift_up(x, k, rows):
    n = x.shape[0]
    return jnp.where(rows < n - k, pltpu.roll(x, n - k, 0), 0.0)


def _conv3(u, w_ref, b_ref, rows):
    return (w_ref[2:3, :] * u + w_ref[1:2, :] * _shift_down(u, 1, rows) + w_ref[0:1, :] * _shift_down(u, 2, rows)
            + b_ref[...])


def _convglu_fwd(u, cw, cb, S, name):
    T = u.shape[0]
    nf = D_FF // CONV_TILE

    def body(ug_ref, uu_ref, wg_ref, wu_ref, bg_ref, bu_ref, a_ref):
        rows = _iota2((S, CONV_TILE), 0)
        yg = _conv3(ug_ref[...], wg_ref, bg_ref, rows)
        yu = _conv3(uu_ref[...], wu_ref, bu_ref, rows)
        a_ref[...] = (yg * _sig(yg) * yu).astype(BF16)

    def blk(rows, off):
        return pl.BlockSpec((rows, CONV_TILE), (lambda b, j: (b, j + off)) if rows == S else (lambda b, j: (0, j + off)))

    return pl.pallas_call(
        body, name=name, grid=(T // S, nf),
        in_specs=[blk(S, 0), blk(S, nf), blk(3, 0), blk(3, nf), blk(1, 0), blk(1, nf)],
        out_specs=blk(S, 0), out_shape=jax.ShapeDtypeStruct((T, D_FF), BF16),
        compiler_params=_cparams("parallel", "parallel"))(u, u, cw, cw, cb, cb)


def _convglu_bwd(u, da, cw, cb, S, name, exchange=None):
    T = u.shape[0]
    nf = D_FF // CONV_TILE

    grid = (nf, T // S)

    def body(*refs):
        (ug_ref, uu_ref, da_ref, wg_ref, wu_ref, bg_ref, bu_ref, du_ref, dw_ref, db_ref), carried = _carried(
            exchange, refs, 7, 3, 0)
        _start_carried(exchange, carried, grid)
        b = pl.program_id(1)
        rows = _iota2((S, CONV_TILE), 0)
        ug, uu = ug_ref[...], uu_ref[...]
        yg = _conv3(ug, wg_ref, bg_ref, rows)
        yu = _conv3(uu, wu_ref, bu_ref, rows)
        s = _sig(yg)
        da_v = da_ref[...]
        for half, (uv, w_ref, dy) in enumerate(((ug, wg_ref, da_v * yu * (s * (1.0 + yg * (1.0 - s)))),
                                                (uu, wu_ref, da_v * yg * s))):
            up1, up2 = _shift_up(dy, 1, rows), _shift_up(dy, 2, rows)
            du_ref[half] = w_ref[2:3, :] * dy + w_ref[1:2, :] * up1 + w_ref[0:1, :] * up2
            dws = [jnp.sum(up2 * uv, axis=0, keepdims=True), jnp.sum(up1 * uv, axis=0, keepdims=True),
                   jnp.sum(dy * uv, axis=0, keepdims=True)]
            dbv = jnp.sum(dy, axis=0, keepdims=True)

            @pl.when(b == 0)
            def _():
                for k in range(3):
                    dw_ref[half, k:k + 1, :] = dws[k]
                db_ref[half] = dbv

            @pl.when(b > 0)
            def _():
                for k in range(3):
                    dw_ref[half, k:k + 1, :] += dws[k]
                db_ref[half] += dbv

        _wait_carried(exchange, carried, grid)

    def blk(rows, off):
        return pl.BlockSpec((rows, CONV_TILE), (lambda j, b: (b, j + off)) if rows == S else (lambda j, b: (0, j + off)))

    def both(rows):
        return pl.BlockSpec((2, rows, CONV_TILE), (lambda j, b: (0, b, j)) if rows == S else (lambda j, b: (0, 0, j)))

    outs, targets = _carrier_call(
        body, name, grid, [blk(S, 0), blk(S, nf), blk(S, 0), blk(3, 0), blk(3, nf), blk(1, 0), blk(1, nf)],
        [both(S), both(3), both(1)],
        [jax.ShapeDtypeStruct((2, T, D_FF), F32), jax.ShapeDtypeStruct((2, 3, D_FF), F32),
         jax.ShapeDtypeStruct((2, 1, D_FF), F32)], [], {}, (u, u, da, cw, cw, cb, cb), exchange)
    return (*outs, targets)


def _sb_scores(qb, kblk, on_diag_mask):
    z = _dot(qb, kblk, 1, 1) * (SB_DIM ** -0.5)
    l1 = jnp.log(1.0 + jnp.exp(-jnp.abs(z)))
    ls = jnp.minimum(z, 0.0) - l1
    lk = jnp.where(on_diag_mask, ls - z, 0.0)
    return ls, lk


def _sb_fwd(proj, S, name, exchange=None):
    T = proj.shape[0]
    BQ, BK = SB_QBLOCK, ATT_BLOCK
    unroll = BQ // BK
    nq = S // BQ
    nhp = SB_HEADS // 2
    heads = [slice(h * SB_DIM, (h + 1) * SB_DIM) for h in range(2)]

    grid = (T // S, nhp)

    def body(*refs):
        (q_ref, k_ref, v_ref, o_ref, tot_ref), carried = _carried(exchange, refs, 3, 2, 0)
        _start_carried(exchange, carried, grid)
        ahead = _iota2((BQ, BK), 1) - _iota2((BQ, BK), 0)
        upper = (_iota2((BK, BK), 0) > _iota2((BK, BK), 1)).astype(BF16)

        def qloop(iq, carry):
            q0 = pl.multiple_of(iq * BQ, BQ)
            rows = pl.ds(q0, BQ)
            qbs = [q_ref[rows, sl] for sl in heads]
            nkb = (iq + 1) * (BQ // BK)

            def kloop(jj, kc):
                blocks = []
                for u in range(unroll):
                    k0 = pl.multiple_of((nkb - 1 - unroll * jj - u) * BK, BK)
                    blocks.append((pl.ds(k0, BK), ahead < q0 - k0))
                units = [(h, krows, mask) for krows, mask in blocks for h in range(2)]
                scores = [_sb_scores(qbs[h], k_ref[krows, heads[h]], mask) for h, krows, mask in units]
                laters = [_dot_exact_l(lk, upper) for _, lk in scores]
                runs = [kc[h][0] for h in range(2)]
                accs = [kc[h][1] for h in range(2)]
                for (h, krows, mask), (ls, lk), later in zip(units, scores, laters):
                    w = jnp.where(mask, jnp.exp(ls + later + runs[h]), 0.0)
                    accs[h] = accs[h] + _dot(w, v_ref[krows, heads[h]], 1, 0)
                    runs[h] = runs[h] + jnp.sum(lk, axis=1, keepdims=True)
                return tuple((runs[h], accs[h]) for h in range(2))

            init = (jnp.zeros((BQ, 1), F32), jnp.zeros((BQ, SB_DIM), F32))
            res = lax.fori_loop(0, nkb // unroll, kloop, (init, init))
            for h, (sl, (run, acc)) in enumerate(zip(heads, res)):
                o_ref[rows, sl] = acc.astype(BF16)
                tot_ref[rows, h:h + 1] = run
            return carry

        lax.fori_loop(0, nq, qloop, 0)
        _wait_carried(exchange, carried, grid)

    def spec(off):
        return pl.BlockSpec((S, LANES), lambda b, hp: (b, hp + off))

    (cat, tot), targets = _carrier_call(
        body, name, grid, [spec(0), spec(nhp), spec(2 * nhp)],
        [spec(0), pl.BlockSpec((None, S, 2), lambda b, hp: (hp, b, 0))],
        [jax.ShapeDtypeStruct((T, 2 * SB_HEADS * SB_DIM), BF16), jax.ShapeDtypeStruct((nhp, T, 2), F32)],
        [], {}, (proj, proj, proj), exchange)
    return cat, tot, targets


def _sb_bwd(proj, tot, dcat, S, name, exchange=None):
    T, width = proj.shape
    BQ, BK = SB_QBLOCK, ATT_BLOCK
    unroll = BQ // BK
    nq = S // BQ
    nhp = SB_HEADS // 2
    scale = SB_DIM ** -0.5
    heads = [slice(h * SB_DIM, (h + 1) * SB_DIM) for h in range(2)]

    grid = (T // S, nhp)

    def body(*refs):
        (q_ref, k_ref, v_ref, tot_ref, do_ref, dp_hbm, dq_s, dk_s, dv_s, sems), carried = _carried(
            exchange, refs, 5, 1, 4)
        _start_carried(exchange, carried, grid)
        b, hp = pl.program_id(0), pl.program_id(1)
        ahead = _iota2((BQ, BK), 1) - _iota2((BQ, BK), 0)
        r, c = _iota2((BK, BK), 0), _iota2((BK, BK), 1)
        upto = (r <= c).astype(BF16)
        earlier = (r < c).astype(BF16)
        dk_s[...] = jnp.zeros_like(dk_s)
        dv_s[...] = jnp.zeros_like(dv_s)

        def qloop(iq, carry):
            q0 = pl.multiple_of(iq * BQ, BQ)
            rows = pl.ds(q0, BQ)
            qbs = [q_ref[rows, sl] for sl in heads]
            dobs = [do_ref[rows, sl] for sl in heads]
            totals = [tot_ref[rows, h:h + 1] for h in range(2)]

            def kloop(m, kc):
                blocks = []
                for u in range(unroll):
                    k0 = pl.multiple_of((unroll * m + u) * BK, BK)
                    blocks.append((pl.ds(k0, BK), ahead < q0 - k0))
                units = [(h, krows, mask) for krows, mask in blocks for h in range(2)]
                kblks = [k_ref[krows, heads[h]] for h, krows, _ in units]
                scores = [_sb_scores(qbs[h], kblk, mask) for (h, _, mask), kblk in zip(units, kblks)]
                prefixes = [_dot_exact_l(lk, upto) for _, lk in scores]
                dws = [_dot(dobs[h], v_ref[krows, heads[h]], 1, 1) for h, krows, _ in units]
                runs = [kc[h][0] for h in range(2)]
                gruns = [kc[h][1] for h in range(2)]
                dqs = [kc[h][2] for h in range(2)]
                ws, gs = [], []
                for (h, _, mask), (ls, lk), prefix, dw in zip(units, scores, prefixes, dws):
                    w = jnp.where(mask, jnp.exp(ls + (totals[h] - (prefix + runs[h]))), 0.0)
                    runs[h] = runs[h] + jnp.sum(lk, axis=1, keepdims=True)
                    ws.append(w)
                    gs.append(w * dw)
                gprefixes = [_dot_exact_l(g, earlier) for g in gs]
                dzs = []
                for (h, _, mask), (ls, _), g, gprefix in zip(units, scores, gs, gprefixes):
                    beta = jnp.exp(ls)
                    dzs.append(jnp.where(mask, g * (1.0 - beta) - beta * (gprefix + gruns[h]), 0.0) * scale)
                    gruns[h] = gruns[h] + jnp.sum(g, axis=1, keepdims=True)
                for (h, krows, _), kblk, w, dz in zip(units, kblks, ws, dzs):
                    dv_s[krows, heads[h]] += _dot(w, dobs[h], 0, 0)
                    dk_s[krows, heads[h]] += _dot(dz, qbs[h], 0, 0)
                    dqs[h] = dqs[h] + _dot(dz, kblk, 1, 0)
                return tuple((runs[h], gruns[h], dqs[h]) for h in range(2))

            zero = jnp.zeros((BQ, 1), F32)
            init = (zero, zero, jnp.zeros((BQ, SB_DIM), F32))
            res = lax.fori_loop(0, (iq + 1) * (BQ // BK) // unroll, kloop, (init, init))
            for sl, (_, _, dq) in zip(heads, res):
                dq_s[rows, sl] = dq
            return carry

        lax.fori_loop(0, nq, qloop, 0)
        r0 = pl.multiple_of(b * S, S)
        copies = []
        for n, buf in enumerate((dq_s, dk_s, dv_s)):
            c0 = pl.multiple_of((hp + n * nhp) * LANES, LANES)
            copies.append(pltpu.make_async_copy(buf, dp_hbm.at[pl.ds(r0, S), pl.ds(c0, LANES)], sems.at[n]))
        for cp in copies:
            cp.start()
        for cp in copies:
            cp.wait()
        _wait_carried(exchange, carried, grid)

    def spec(off):
        return pl.BlockSpec((S, LANES), lambda b, hp: (b, hp + off))

    (dproj,), targets = _carrier_call(
        body, name, grid,
        [spec(0), spec(nhp), spec(2 * nhp), pl.BlockSpec((None, S, 2), lambda b, hp: (hp, b, 0)), spec(0)],
        [ANY], [jax.ShapeDtypeStruct((T, width), F32)],
        [pltpu.VMEM((S, LANES), F32)] * 3 + [pltpu.SemaphoreType.DMA((3,))], {}, (proj, proj, proj, tot, dcat),
        exchange)
    return dproj, targets


HG_COL0 = 3 * SB_HEADS * SB_DIM // LANES


def _hg_gates(q, fp, lbv):
    sg = _sig(fp)
    f = lbv + (1.0 - lbv) * sg
    kk = (1.0 - lbv) * _sig(-fp)
    sq = _sig(q)
    return sg, f, kk, sq


def _hg_chunk(qs, kk, lf, incl):
    C = HG_CHUNK
    b = _dot_exact_r(incl, lf)
    bl = b[C - 1:C, :]
    bm = b[C // 2 - 1:C // 2, :]
    e_t = jnp.exp(b - bm)
    e_s = jnp.exp(bm - b)
    e_i = jnp.exp(b)
    e_e = jnp.exp(bl - b)
    return bl, e_t, e_s, e_i, e_e


def _hgrn_fwd(proj, cat, lb, hgn, S, name, exchange=None):
    T = proj.shape[0]
    B = T // S
    C = HG_CHUNK
    NC = S // C

    grid = (B, HG_HEADS // HG_STEP)

    def body(*refs):
        (q_ref, f_ref, i_ref, g_ref, lb_ref, hgn_ref, _, ob_ref, oraw_ref, st_ref, state), carried = _carried(
            exchange, refs, 7, 3, 1)
        _start_carried(exchange, carried, grid)
        state[...] = jnp.zeros_like(state)
        row, col = _iota2((C, C), 0), _iota2((C, C), 1)
        causal = row >= col
        incl = causal.astype(BF16)

        def trip(m, carry):
            units = []
            for u in range(HG_TRIP):
                c = m * HG_TRIP + u
                rows = pl.ds(pl.multiple_of(c * C, C), C)
                units += [(c, hh, rows, slice(hh * LANES, (hh + 1) * LANES)) for hh in range(HG_STEP)]
            qs_, kks, lfs = [], [], []
            for _, _, rows, hs in units:
                q = q_ref[rows, hs]
                _, f, kk, sq = _hg_gates(q, f_ref[rows, hs], lb_ref[:, hs])
                qs_.append(q * sq)
                kks.append(kk)
                lfs.append(jnp.log(f))
            decays = [_hg_chunk(None, None, lf, incl) for lf in lfs]
            ps = [jnp.where(causal, _dot(qs * e_t, kk * e_s, 1, 1), 0.0)
                  for qs, kk, (_, e_t, e_s, _, _) in zip(qs_, kks, decays)]
            outs = []
            for (c, hh, rows, hs), qs, kk, (bl, _, _, e_i, e_e), p in zip(units, qs_, kks, decays, ps):
                iv = i_ref[rows, hs]
                st = state[hh]
                st_ref[0, hh, c] = st
                outs.append(_dot(qs * e_i, st, 1, 1) + _dot(p, iv, 1, 0))
                state[hh] = st * jnp.exp(bl) + _dot(iv, kk * e_e, 0, 0)
            for (_, _, rows, hs), o in zip(units, outs):
                gv = g_ref[rows, hs]
                oraw_ref[rows, hs] = o
                r = lax.rsqrt(jnp.mean(o * o, axis=1, keepdims=True) + EPS)
                ob_ref[rows, hs] = (o * r * hgn_ref[...] * (gv * _sig(gv))).astype(BF16)
            return carry

        lax.fori_loop(0, NC // HG_TRIP, trip, 0)
        _wait_carried(exchange, carried, grid)

    width = HG_STEP * LANES
    col0 = HG_COL0 * LANES // width
    nstep = HG_HEADS // HG_STEP

    def spec(off):
        return pl.BlockSpec((S, width), lambda b, h: (b, h + off))

    outs, targets = _carrier_call(
        body, name, grid,
        [spec(col0), spec(col0 + nstep), spec(col0 + 2 * nstep), spec(col0 + 3 * nstep),
         pl.BlockSpec((1, width), lambda b, h: (0, h)), pl.BlockSpec((1, LANES), lambda b, h: (0, 0)), ANY],
        [spec(nstep), spec(0), pl.BlockSpec((1, HG_STEP, NC, LANES, LANES), lambda b, h: (b, h, 0, 0, 0))],
        [jax.ShapeDtypeStruct(cat.shape, cat.dtype), jax.ShapeDtypeStruct((T, HG_HEADS * LANES), F32),
         jax.ShapeDtypeStruct((B, HG_HEADS, NC, LANES, LANES), F32)],
        [pltpu.VMEM((HG_STEP, LANES, LANES), F32)], {6: 0}, (proj, proj, proj, proj, lb, hgn, cat), exchange)
    return (*outs, targets)


def _hgrn_bwd(proj, oraw, dcat, states, lb, hgn, dproj, S, name):
    T = proj.shape[0]
    B = T // S
    C = HG_CHUNK
    NC = S // C

    def body(q_ref, f_ref, i_ref, g_ref, oraw_ref, dy_ref, st_ref, lb_ref, hgn_ref, dp_in,
             dp_hbm, dlb_ref, dhgn_ref, dstate, dq_s, df_s, di_s, dg_s, sems):
        del dp_in
        h, b = pl.program_id(0), pl.program_id(1)
        row, col = _iota2((C, C), 0), _iota2((C, C), 1)
        causal = row >= col
        incl = causal.astype(BF16)
        last_row = _iota2((C, LANES), 0) == C - 1
        hg = hgn_ref[...]
        dstate[...] = jnp.zeros_like(dstate)

        @pl.when(b == 0)
        def _():
            dlb_ref[...] = jnp.zeros_like(dlb_ref)

        @pl.when(jnp.logical_and(b == 0, h == 0))
        def _():
            dhgn_ref[...] = jnp.zeros_like(dhgn_ref)

        def trip(m, carry):
            units = []
            for u in range(HG_TRIP):
                c = NC - 1 - (m * HG_TRIP + u)
                rows = pl.ds(pl.multiple_of(c * C, C), C)
                units += [(c, hh, rows, slice(hh * LANES, (hh + 1) * LANES)) for hh in range(HG_STEP)]
            dos = []
            dhgn = jnp.zeros((1, LANES), F32)
            for _, _, rows, hs in units:
                gv, o, dy = g_ref[rows, hs], oraw_ref[rows, hs], dy_ref[rows, hs]
                r = lax.rsqrt(jnp.mean(o * o, axis=1, keepdims=True) + EPS)
                on = o * r
                sgv = _sig(gv)
                silu_g = gv * sgv
                dg_s[rows, hs] = dy * on * hg * (sgv * (1.0 + gv * (1.0 - sgv)))
                dhgn = dhgn + jnp.sum(dy * on * silu_g, axis=0, keepdims=True)
                dn = dy * hg * silu_g
                dos.append(r * dn - o * (r * r * r * jnp.mean(o * dn, axis=1, keepdims=True)))
            dhgn_ref[...] += dhgn
            pre = []
            for (_, _, rows, hs), do in zip(units, dos):
                q, iv = q_ref[rows, hs], i_ref[rows, hs]
                sg, f, kk, sq = _hg_gates(q, f_ref[rows, hs], lb_ref[:, hs])
                pre.append((q, iv, sg, f, kk, sq, q * sq))
            decays = [_hg_chunk(None, None, jnp.log(f), incl) for _, _, _, f, _, _, _ in pre]
            prods = [(qs * e_t, kk * e_s, qs * e_i, kk * e_e)
                     for (_, _, _, _, kk, _, qs), (_, e_t, e_s, e_i, e_e) in zip(pre, decays)]
            ps = [jnp.where(causal, _dot(qd, kd, 1, 1), 0.0) for qd, kd, _, _ in prods]
            dps = [jnp.where(causal, _dot(do, iv, 1, 1), 0.0) for do, (_, iv, _, _, _, _, _) in zip(dos, pre)]
            dqds = [_dot(dp, kd, 1, 0) for dp, (_, kd, _, _) in zip(dps, prods)]
            dkds = [_dot(dp, qd, 0, 0) for dp, (qd, _, _, _) in zip(dps, prods)]
            pdos = [_dot(p, do, 0, 0) for p, do in zip(ps, dos)]
            chain = []
            for (c, hh, rows, hs), do, (_, iv, _, _, _, _, _), (bl, _, _, _, _), (_, _, qi, ke), pdo in zip(
                    units, dos, pre, decays, prods, pdos):
                st = st_ref[0, hh, c]
                dst = dstate[hh]
                ebl = jnp.exp(bl)
                dqi = _dot(do, st, 1, 0)
                di_s[rows, hs] = pdo + _dot(ke, dst, 1, 1)
                dke = _dot(iv, dst, 1, 0)
                dbl = jnp.sum(st * dst, axis=0, keepdims=True) * ebl + jnp.sum(dke * ke, axis=0, keepdims=True)
                dstate[hh] = _dot(do, qi, 0, 0) + dst * ebl
                chain.append((dqi, dke, dbl))
            for (_, _, rows, hs), (q, _, sg, f, _, sq, _), (_, e_t, e_s, e_i, e_e), (qd, kd, qi, ke), dqd, dkd, (
                    dqi, dke, dbl) in zip(units, pre, decays, prods, dqds, dkds, chain):
                lbv = lb_ref[:, hs]
                db = dqd * qd - dkd * kd + dqi * qi - dke * ke + jnp.where(last_row, dbl, 0.0)
                dqs = dqd * e_t + dqi * e_i
                dkk = dkd * e_s + dke * e_e
                dlf = _dot_exact_r(incl, db, cm=0)
                oms = 1.0 - sg
                dfd = dlf / f
                df_s[rows, hs] = (dfd - dkk) * (1.0 - lbv) * sg * oms
                dq_s[rows, hs] = dqs * (sq * (1.0 + q * (1.0 - sq)))
                dlb_ref[:, hs] += jnp.sum((dfd - dkk) * oms, axis=0, keepdims=True)
            return carry

        lax.fori_loop(0, NC // HG_TRIP, trip, 0)
        r0 = pl.multiple_of(b * S, S)
        copies = []
        for n, buf in enumerate((dq_s, df_s, di_s, dg_s)):
            c0 = pl.multiple_of((col0 + n * nstep + h) * width, width)
            copies.append(pltpu.make_async_copy(buf, dp_hbm.at[pl.ds(r0, S), pl.ds(c0, width)], sems.at[n]))
        for cp in copies:
            cp.start()
        for cp in copies:
            cp.wait()

    width = HG_STEP * LANES
    col0 = HG_COL0 * LANES // width
    nstep = HG_HEADS // HG_STEP

    def spec(off):
        return pl.BlockSpec((S, width), lambda h, b: (b, h + off))

    return pl.pallas_call(
        body, name=name, grid=(nstep, B),
        in_specs=[spec(col0), spec(col0 + nstep), spec(col0 + 2 * nstep), spec(col0 + 3 * nstep), spec(0), spec(nstep),
                  pl.BlockSpec((1, HG_STEP, NC, LANES, LANES), lambda h, b: (b, h, 0, 0, 0)),
                  pl.BlockSpec((1, width), lambda h, b: (0, h)), pl.BlockSpec((1, LANES), lambda h, b: (0, 0)), ANY],
        out_specs=[ANY, pl.BlockSpec((1, width), lambda h, b: (0, h)), pl.BlockSpec((1, LANES), lambda h, b: (0, 0))],
        out_shape=[jax.ShapeDtypeStruct(dproj.shape, F32), jax.ShapeDtypeStruct((1, HG_HEADS * LANES), F32),
                   jax.ShapeDtypeStruct((1, LANES), F32)],
        scratch_shapes=[pltpu.VMEM((HG_STEP, LANES, LANES), F32)] + [pltpu.VMEM((S, width), F32)] * 4
        + [pltpu.SemaphoreType.DMA((4,))],
        input_output_aliases={9: 0},
        compiler_params=_cparams("arbitrary", "arbitrary"))(proj, proj, proj, proj, oraw, dcat, states, lb, hgn, dproj)


def _lower_bound_fwd(logits, name):
    assert logits.shape[0] == 2

    def body(l_ref, o_ref):
        l0, l1 = l_ref[0:1, :], l_ref[1:2, :]
        m = jnp.maximum(l0, l1)
        e0, e1 = jnp.exp(l0 - m), jnp.exp(l1 - m)
        o_ref[0:1, :] = jnp.zeros_like(l0)
        o_ref[1:2, :] = e1 / (e0 + e1)

    return pl.pallas_call(body, name=name, out_shape=jax.ShapeDtypeStruct(logits.shape, F32))(logits)


def _lower_bound_bwd(logits, dlb, name):
    def body(l_ref, d_ref, o_ref):
        l0, l1 = l_ref[0:1, :], l_ref[1:2, :]
        m = jnp.maximum(l0, l1)
        e0, e1 = jnp.exp(l0 - m), jnp.exp(l1 - m)
        s1 = e1 / (e0 + e1)
        t = s1 * (1.0 - s1) * d_ref[1:2, :]
        o_ref[0:1, :] = -t
        o_ref[1:2, :] = t

    return pl.pallas_call(body, name=name, out_shape=jax.ShapeDtypeStruct(logits.shape, F32))(logits, dlb)


def _bucket_thresholds():
    dist = np.arange(WINDOW)
    max_exact = N_BUCKETS // 2
    large = max_exact + (np.log(np.maximum(dist, max_exact) / max_exact) / math.log(MAX_DISTANCE / max_exact)
                         * (N_BUCKETS - max_exact)).astype(np.int32)
    bucket = np.where(dist < max_exact, dist, np.minimum(large, N_BUCKETS - 1))
    assert np.all(np.diff(bucket) >= 0)
    return [int(np.argmax(bucket >= k)) if np.any(bucket >= k) else 10 ** 6 for k in range(1, N_BUCKETS)]


def _band_bucket():
    dist = _iota2((WINDOW, 2 * WINDOW), 0) + WINDOW - _iota2((WINDOW, 2 * WINDOW), 1)
    bucket = jnp.zeros((WINDOW, 2 * WINDOW), jnp.int32)
    for thr in _bucket_thresholds():
        bucket = bucket + (dist >= thr).astype(jnp.int32)
    band = jnp.logical_and(dist >= 0, dist < WINDOW)
    return bucket, band


def _bias_build(rel_bias, name):
    def body(rb_ref, o_ref):
        h = pl.program_id(0)
        bucket, _ = _band_bucket()
        bias = jnp.zeros((WINDOW, 2 * WINDOW), F32)
        for k in range(N_BUCKETS):
            bias = jnp.where(bucket == k, rb_ref[k, h], bias)
        o_ref[0] = bias

    return pl.pallas_call(
        body, name=name, grid=(SW_HEADS,), in_specs=[pl.BlockSpec(memory_space=pltpu.SMEM)],
        out_specs=pl.BlockSpec((1, WINDOW, 2 * WINDOW), lambda h: (h, 0, 0)),
        out_shape=jax.ShapeDtypeStruct((SW_HEADS, WINDOW, 2 * WINDOW), F32),
        compiler_params=_cparams("parallel"))(rel_bias)


def _bias_reduce(dbias, name):
    def body(d_ref, o_ref):
        bucket, band = _band_bucket()
        d = jnp.where(band, d_ref[0], 0.0)
        lane = _iota2((1, LANES), 1)
        out = jnp.zeros((1, LANES), F32)
        for k in range(N_BUCKETS):
            out = jnp.where(lane == k, jnp.sum(jnp.where(bucket == k, d, 0.0)), out)
        o_ref[0] = out

    return pl.pallas_call(
        body, name=name, grid=(SW_HEADS,), in_specs=[pl.BlockSpec((1, WINDOW, 2 * WINDOW), lambda h: (h, 0, 0))],
        out_specs=pl.BlockSpec((1, 1, LANES), lambda h: (h, 0, 0)),
        out_shape=jax.ShapeDtypeStruct((SW_HEADS, 1, LANES), F32), compiler_params=_cparams("parallel"))(dbias)


SW_Q_COLS = SW_HEADS * SW_DIM
SW_K_BLOCK0 = SW_Q_COLS // LANES
SW_V_BLOCK0 = SW_K_BLOCK0 + SW_KV * SW_DIM // LANES
SW_STEP_HEADS = 8
SW_TRIP = 1


def _swa_logits(qb, kprev, kcur, bias_ref, hl, mprev, mcur):
    scale = SW_DIM ** -0.5
    lp = jnp.where(mprev, _dot(qb, kprev, 1, 1) * scale + bias_ref[hl, :, 0:WINDOW], NEG)
    lc = jnp.where(mcur, _dot(qb, kcur, 1, 1) * scale + bias_ref[hl, :, WINDOW:2 * WINDOW], NEG)
    return lp, lc


def _swa_softmax(lp, lc, sink):
    m = jnp.maximum(jnp.maximum(jnp.max(lp, axis=1, keepdims=True), jnp.max(lc, axis=1, keepdims=True)), sink)
    ep, ec = jnp.exp(lp - m), jnp.exp(lc - m)
    es = jnp.exp(sink - m)
    den = jnp.sum(ep, axis=1, keepdims=True) + jnp.sum(ec, axis=1, keepdims=True) + es
    return ep, ec, es, den


def _swa_block_heads(n, q_ref, k_ref, v_ref, bias_ref, sink_ref, kp, above, mcur):
    W = WINDOW
    rows = pl.ds(pl.multiple_of(n * W, W), W)
    prow = pl.ds(pl.multiple_of(jnp.maximum(n - 1, 0) * W, W), W)
    mprev = jnp.logical_and(above, n > 0)
    heads = []
    for kvh in range(2):
        ksl = slice(kvh * SW_DIM, (kvh + 1) * SW_DIM)
        kv = (k_ref[rows, ksl], k_ref[prow, ksl], v_ref[rows, ksl], v_ref[prow, ksl])
        for g in range(4):
            hl = kvh * 4 + g
            heads.append((hl, ksl, slice(hl * SW_DIM, (hl + 1) * SW_DIM), kv))
    qbs = [q_ref[rows, qsl] for _, _, qsl, _ in heads]
    logits = [_swa_logits(qb, kv[1], kv[0], bias_ref, hl, mprev, mcur) for qb, (hl, _, _, kv) in zip(qbs, heads)]
    soft = [_swa_softmax(lp, lc, sink_ref[kp * SW_STEP_HEADS + hl]) for (lp, lc), (hl, _, _, _) in zip(logits, heads)]
    return rows, prow, heads, qbs, soft


def _swa_fwd(qkn, proj, bias, sinks, S, name, exchange=None):
    T = qkn.shape[0]
    W = WINDOW
    nb = S // W

    grid = (T // S, 2)

    def body(*refs):
        (q_ref, k_ref, v_ref, bias_ref, sink_ref, o_ref), carried = _carried(exchange, refs, 5, 1, 0)
        _start_carried(exchange, carried, grid)
        kp = pl.program_id(1)
        row, col = _iota2((W, W), 0), _iota2((W, W), 1)
        mcur = col <= row
        above = col > row

        def blk(m, carry):
            staged = [_swa_block_heads(m * SW_TRIP + u, q_ref, k_ref, v_ref, bias_ref, sink_ref, kp, above, mcur)
                      for u in range(SW_TRIP)]
            for rows, _, heads, _, soft in staged:
                outs = [(_dot(ep, kv[3], 1, 0) + _dot(ec, kv[2], 1, 0)) / den
                        for (ep, ec, _, den), (_, _, _, kv) in zip(soft, heads)]
                for (_, _, qsl, _), o in zip(heads, outs):
                    o_ref[rows, qsl] = o.astype(BF16)
            return carry

        lax.fori_loop(0, nb // SW_TRIP, blk, 0)
        _wait_carried(exchange, carried, grid)

    (o,), targets = _carrier_call(
        body, name, grid,
        [pl.BlockSpec((S, 4 * LANES), lambda b, kp: (b, kp)),
         pl.BlockSpec((S, LANES), lambda b, kp: (b, SW_K_BLOCK0 + kp)),
         pl.BlockSpec((S, LANES), lambda b, kp: (b, SW_V_BLOCK0 + kp)),
         pl.BlockSpec((SW_STEP_HEADS, W, 2 * W), lambda b, kp: (kp, 0, 0)),
         pl.BlockSpec(memory_space=pltpu.SMEM)],
        [pl.BlockSpec((S, 4 * LANES), lambda b, kp: (b, kp))], [jax.ShapeDtypeStruct((T, SW_Q_COLS), BF16)],
        [], {}, (qkn, qkn, proj, bias, sinks), exchange)
    return o, targets


def _swa_bwd(qkn, proj, bias, sinks, do, S, name, exchange=None):
    T, width = proj.shape
    W = WINDOW
    nb = S // W
    scale = SW_DIM ** -0.5

    grid = (2, T // S)

    def body(*refs):
        (q_ref, k_ref, v_ref, bias_ref, sink_ref, do_ref, dp_hbm, dbias_ref, dsink_ref,
         dq_s, dk_s, dv_s, sems), carried = _carried(exchange, refs, 6, 3, 4)
        _start_carried(exchange, carried, grid)
        kp, b = pl.program_id(0), pl.program_id(1)
        row, col = _iota2((W, W), 0), _iota2((W, W), 1)
        mcur = col <= row
        above = col > row
        dk_s[...] = jnp.zeros_like(dk_s)
        dv_s[...] = jnp.zeros_like(dv_s)

        @pl.when(b == 0)
        def _():
            dbias_ref[...] = jnp.zeros_like(dbias_ref)
            dsink_ref[...] = jnp.zeros_like(dsink_ref)

        def blk(m, carry):
            staged = [_swa_block_heads(m * SW_TRIP + u, q_ref, k_ref, v_ref, bias_ref, sink_ref, kp, above, mcur)
                      for u in range(SW_TRIP)]
            for rows, prow, heads, qbs, soft in staged:
                one_block(rows, prow, heads, qbs, soft)
            return carry

        def one_block(rows, prow, heads, qbs, soft):
            dobs = [do_ref[rows, qsl] for _, _, qsl, _ in heads]
            dps = [(_dot(dob, kv[3], 1, 1), _dot(dob, kv[2], 1, 1)) for dob, (_, _, _, kv) in zip(dobs, heads)]
            grads = []
            for (hl, _, _, _), (ep, ec, es, den), (dpp, dpc) in zip(heads, soft, dps):
                inv = 1.0 / den
                pp, pc = ep * inv, ec * inv
                total = jnp.sum(pp * dpp, axis=1, keepdims=True) + jnp.sum(pc * dpc, axis=1, keepdims=True)
                dlp = pp * (dpp - total)
                dlc = pc * (dpc - total)
                dsink_ref[hl:hl + 1, :] += jnp.zeros((1, LANES), F32) - jnp.sum(es * inv * total)
                dbias_ref[hl, :, 0:W] += dlp
                dbias_ref[hl, :, W:2 * W] += dlc
                grads.append((pp, pc, dlp, dlc))
            for (_, _, qsl, kv), (_, _, dlp, dlc) in zip(heads, grads):
                dq_s[rows, qsl] = (_dot(dlp, kv[1], 1, 0) + _dot(dlc, kv[0], 1, 0)) * scale
            for kvh in range(2):
                group = range(4 * kvh, 4 * kvh + 4)
                ksl = heads[4 * kvh][1]
                dk_s[prow, ksl] += sum(_dot(grads[i][2], qbs[i], 0, 0) for i in group) * scale
                dk_s[rows, ksl] += sum(_dot(grads[i][3], qbs[i], 0, 0) for i in group) * scale
                dv_s[prow, ksl] += sum(_dot(grads[i][0], dobs[i], 0, 0) for i in group)
                dv_s[rows, ksl] += sum(_dot(grads[i][1], dobs[i], 0, 0) for i in group)

        lax.fori_loop(0, nb // SW_TRIP, blk, 0)
        r0 = pl.multiple_of(b * S, S)
        cq =pl.multiple_of(kp * 4 * LANES, LANES)
        ck = pl.multiple_of((SW_K_BLOCK0 + kp) * LANES, LANES)
        cv = pl.multiple_of((SW_V_BLOCK0 + kp) * LANES, LANES)
        copies = [pltpu.make_async_copy(dq_s, dp_hbm.at[pl.ds(r0, S), pl.ds(cq, 4 * LANES)], sems.at[0]),
                  pltpu.make_async_copy(dk_s, dp_hbm.at[pl.ds(r0, S), pl.ds(ck, LANES)], sems.at[1]),
                  pltpu.make_async_copy(dv_s, dp_hbm.at[pl.ds(r0, S), pl.ds(cv, LANES)], sems.at[2])]
        for cp in copies:
            cp.start()
        for cp in copies:
            cp.wait()
        _wait_carried(exchange, carried, grid)

    qspec = pl.BlockSpec((S, 4 * LANES), lambda kp, b: (b, kp))
    outs, targets = _carrier_call(
        body, name, grid,
        [qspec, pl.BlockSpec((S, LANES), lambda kp, b: (b, SW_K_BLOCK0 + kp)),
         pl.BlockSpec((S, LANES), lambda kp, b: (b, SW_V_BLOCK0 + kp)),
         pl.BlockSpec((SW_STEP_HEADS, W, 2 * W), lambda kp, b: (kp, 0, 0)),
         pl.BlockSpec(memory_space=pltpu.SMEM), qspec],
        [ANY, pl.BlockSpec((SW_STEP_HEADS, W, 2 * W), lambda kp, b: (kp, 0, 0)),
         pl.BlockSpec((SW_STEP_HEADS, LANES), lambda kp, b: (kp, 0))],
        [jax.ShapeDtypeStruct((T, width), F32), jax.ShapeDtypeStruct((SW_HEADS, W, 2 * W), F32),
         jax.ShapeDtypeStruct((SW_HEADS, LANES), F32)],
        [pltpu.VMEM((S, 4 * LANES), F32), pltpu.VMEM((S, LANES), F32), pltpu.VMEM((S, LANES), F32),
         pltpu.SemaphoreType.DMA((3,))], {}, (qkn, qkn, proj, bias, sinks, do), exchange)
    return (*outs, targets)


CHIP_FLIPS = ((1, 0), (0, 1), (1, 1))


def _flip(v, f):
    return 1 - v if f else v


class _Exchange:
    def __init__(self, sources, targets, copies):
        self.sources, self.targets = list(sources), list(targets)
        self._copies = copies
        n = len(self.sources)
        self.scratch = [pltpu.SemaphoreType.DMA((n, 3)), pltpu.SemaphoreType.DMA((n, 3)),
                        pltpu.SemaphoreType.DMA((n,))]

    def _descriptors(self, srcs, dsts, sems, chip, peers):
        send, recv, loc = sems
        out = []
        for t, (local, remote) in enumerate(self._copies(srcs, dsts, chip)):
            out.append(pltpu.make_async_copy(local[0], local[1], loc.at[t]))
            for r, (src, dst) in enumerate(remote):
                out.append(pltpu.make_async_remote_copy(src, dst, send.at[t, r], recv.at[t, r],
                                                        device_id=peers[r], device_id_type=MESH))
        return out

    def start(self, srcs, dsts, sems):
        x, y, c = lax.axis_index("x"), lax.axis_index("y"), lax.axis_index("c")
        peers = [(_flip(x, fx), _flip(y, fy), c) for fx, fy in CHIP_FLIPS]
        for chip in range(4):
            @pl.when(2 * x + y == chip)
            def _():
                for cp in self._descriptors(srcs, dsts, sems, chip, peers):
                    cp.start()

    def wait(self, srcs, dsts, sems):
        me = (lax.axis_index("x"), lax.axis_index("y"), lax.axis_index("c"))
        for cp in self._descriptors(srcs, dsts, sems, 0, [me] * 3):
            cp.wait()

    def operands(self):
        return self.sources + self.targets

    def specs(self):
        ns, nt = len(self.sources), len(self.targets)
        return [ANY] * (ns + nt), [ANY] * nt, [jax.ShapeDtypeStruct(t.shape, t.dtype) for t in self.targets]

    def aliases(self, n_in, n_out):
        ns = len(self.sources)
        return {n_in + ns + i: n_out + i for i in range(len(self.targets))}

    def split(self, refs, n_in, n_out, n_scr):
        ns, nt = len(self.sources), len(self.targets)
        o0 = n_in + ns + nt
        s0 = o0 + n_out + nt
        own = list(refs[:n_in]) + list(refs[o0:o0 + n_out]) + list(refs[s0:s0 + n_scr])
        return own, (refs[n_in:n_in + ns], refs[o0 + n_out:o0 + n_out + nt], refs[s0 + n_scr:])


def _carried(exchange, refs, n_in, n_out, n_scr):
    if exchange is None:
        return list(refs), None
    return exchange.split(refs, n_in, n_out, n_scr)


def _grid_edge(grid, last):
    conds = [pl.program_id(d) == (n - 1 if last else 0) for d, n in enumerate(grid)]
    out = conds[0]
    for cnd in conds[1:]:
        out = jnp.logical_and(out, cnd)
    return out


def _start_carried(exchange, parts, grid):
    if parts is not None:
        @pl.when(_grid_edge(grid, False))
        def _():
            exchange.start(*parts)


def _wait_carried(exchange, parts, grid):
    if parts is not None:
        @pl.when(_grid_edge(grid, True))
        def _():
            exchange.wait(*parts)


def _carrier_call(body, name, grid, in_specs, out_specs, out_shape, scratch, aliases, operands, exchange):
    n_out = len(out_shape)
    aliases = dict(aliases)
    if exchange is not None:
        ex_in, ex_out, ex_shape = exchange.specs()
        aliases.update(exchange.aliases(len(in_specs), n_out))
        in_specs, out_specs, out_shape = in_specs + ex_in, out_specs + ex_out, out_shape + ex_shape
        scratch = scratch + exchange.scratch
        operands = list(operands) + exchange.operands()
    outs = pl.pallas_call(body, name=name, grid=grid, in_specs=in_specs, out_specs=out_specs, out_shape=out_shape,
                          scratch_shapes=scratch, input_output_aliases=aliases,
                          compiler_params=_cparams(*["arbitrary"] * len(grid)))(*operands)
    return outs[:n_out], outs[n_out:]


def _exchange_call(exchange, name):
    in_specs, out_specs, out_shape = exchange.specs()

    def body(*refs):
        _, parts = exchange.split(refs, 0, 0, 0)
        exchange.start(*parts)
        exchange.wait(*parts)

    return pl.pallas_call(body, name=name, in_specs=in_specs, out_specs=out_specs, out_shape=out_shape,
                          scratch_shapes=exchange.scratch, input_output_aliases=exchange.aliases(0, 0))(
        *exchange.operands())


def _gather_exchange(shards, fulls, axes, layers):
    sizes = [s.shape[a] for s, a in zip(shards, axes)]

    def copies(srcs, dsts, chip):
        out = []
        for src, full, axis, size, (l0, l1) in zip(srcs, dsts, axes, sizes, layers):
            part = src.at[l0:l1]
            cut = pl.ds(chip * size, size)
            dst = full.at[l0:l1, cut, :] if axis == 1 else full.at[l0:l1, :, cut]
            out.append(((part, dst), [(part, dst)] * 3))
        return out

    return _Exchange(shards, fulls, copies)


def _scatter_exchange(grads, stacks, axes, layers):
    sizes = [g.shape[a - 1] // 4 for g, a in zip(grads, axes)]

    def copies(srcs, dsts, chip):
        out = []
        for g, stack, axis, size, layer in zip(srcs, dsts, axes, sizes, layers):
            def cut(j, g=g, axis=axis, size=size):
                return g.at[pl.ds(j * size, size), :] if axis == 1 else g.at[:, pl.ds(j * size, size)]

            remote = [(cut(chip ^ (2 * fx + fy)), stack.at[r, layer]) for r, (fx, fy) in enumerate(CHIP_FLIPS)]
            out.append(((cut(chip), stack.at[3, layer]), remote))
        return out

    return _Exchange(grads, stacks, copies)


def _swap_with_sibling(parts, name):
    n = len(parts)

    def body(*refs):
        ins, outs = refs[:n], refs[n:2 * n]
        send, recv = refs[2 * n:]
        peer = (lax.axis_index("x"), lax.axis_index("y"), 1 - lax.axis_index("c"))
        copies = [pltpu.make_async_remote_copy(ins[t], outs[t], send.at[t], recv.at[t], device_id=peer,
                                               device_id_type=MESH) for t in range(n)]
        for cp in copies:
            cp.start()
        for cp in copies:
            cp.wait()

    return pl.pallas_call(
        body, name=name, in_specs=[ANY] * n, out_specs=[ANY] * n,
        out_shape=[jax.ShapeDtypeStruct(p.shape, p.dtype) for p in parts],
        scratch_shapes=[pltpu.SemaphoreType.DMA((n,)), pltpu.SemaphoreType.DMA((n,))])(*parts)


def _allreduce_small(v, name):
    R = v.shape[0]
    ND = 8

    def body(v_ref, o_ref, buf, send, recv):
        x, y, c = lax.axis_index("x"), lax.axis_index("y"), lax.axis_index("c")
        me = 4 * x + 2 * y + c
        copies = []
        for d in range(1, ND):
            peer = (_flip(x, d >> 2 & 1), _flip(y, d >> 1 & 1), _flip(c, d & 1))
            copies.append(pltpu.make_async_remote_copy(v_ref, buf.at[me], send.at[d], recv.at[me], device_id=peer,
                                                       device_id_type=MESH))
        for cp in copies:
            cp.start()
        buf[pl.ds(me, 1)] = v_ref[...][None]
        for k in range(ND):
            @pl.when(me != k)
            def _():
                pltpu.make_async_remote_copy(v_ref, buf.at[k], send.at[0], recv.at[k], device_id=(x, y, c),
                                             device_id_type=MESH).wait_recv()
        for cp in copies:
            cp.wait_send()
        total = buf[0]
        for k in range(1, ND):
            total = total + buf[k]
        o_ref[...] = total

    vm = pl.BlockSpec(memory_space=pltpu.VMEM)
    return pl.pallas_call(
        body, name=name, in_specs=[vm], out_specs=vm, out_shape=jax.ShapeDtypeStruct((R, LANES), F32),
        scratch_shapes=[pltpu.VMEM((ND, R, LANES), F32), pltpu.SemaphoreType.DMA((ND,)),
                        pltpu.SemaphoreType.DMA((ND,))],
        compiler_params=pltpu.CompilerParams(vmem_limit_bytes=VMEM_LIMIT))(v)


def _tile2(R, Cn):
    tc = _pick(Cn, 2048)
    tr = R
    for cand in (256, 128, 64, 32, 16, 8):
        if R % cand == 0:
            tr = cand
            break
    return tr, tc


def _sum4(stack, name):
    _, R, Cn = stack.shape
    tr, tc = _tile2(R, Cn)

    def body(s_ref, o_ref):
        o_ref[...] = ((s_ref[0].astype(F32) + s_ref[1].astype(F32)) + s_ref[2].astype(F32)) + s_ref[3].astype(F32)

    return pl.pallas_call(
        body, name=name, grid=(R // tr, Cn // tc), in_specs=[pl.BlockSpec((4, tr, tc), lambda i, j: (0, i, j))],
        out_specs=pl.BlockSpec((tr, tc), lambda i, j: (i, j)), out_shape=jax.ShapeDtypeStruct((R, Cn), F32),
        compiler_params=_cparams("parallel", "parallel"))(stack)


def _adamw(w, m, v, g_parts, name):
    R, Cn = w.shape
    tr, tc = _tile2(R, Cn)
    npart = len(g_parts)
    c1 = 1.0 / (1.0 - ADAM_B1 ** ADAM_STEP)
    c2 = 1.0 / (1.0 - ADAM_B2 ** ADAM_STEP)

    def body(*refs):
        w_ref, m_ref, v_ref = refs[:3]
        g_refs = refs[3:3 + npart]
        g_out, d_out, m_out, v_out = refs[3 + npart:]
        g = g_refs[0][...]
        for r in g_refs[1:]:
            g = g + r[...]
        mn = ADAM_B1 * m_ref[...] + (1.0 - ADAM_B1) * g
        vn = ADAM_B2 * v_ref[...] + (1.0 - ADAM_B2) * (g * g)
        g_out[...] = g
        m_out[...] = mn
        v_out[...] = vn
        d_out[...] = -ADAM_LR * ((mn * c1) / (jnp.sqrt(vn * c2) + ADAM_EPS) + ADAM_WD * w_ref[...])

    spec = pl.BlockSpec((tr, tc), lambda i, j: (i, j))
    return pl.pallas_call(
        body, name=name, grid=(R // tr, Cn // tc), in_specs=[spec] * (3 + npart), out_specs=[spec] * 4,
        out_shape=[jax.ShapeDtypeStruct((R, Cn), F32)] * 4,
        compiler_params=_cparams("parallel", "parallel"))(w, m, v, *g_parts)


SHARDED = (("ab_w_in", 2), ("ab_w_out", 1), ("c_w_in", 2), ("c_w_out", 1), ("ffn_up", 2), ("ffn_conv", 2),
           ("ffn_down", 1), ("ple_gate", 1), ("ple_proj", 2))
SMALL = ("mix_norm", "hg_lb_logits", "hg_out_norm", "q_norm", "k_norm", "sinks", "rel_bias", "ffn_norm",
         "ffn_conv_b", "ple_norm")
WEIGHTS = ("mix_norm", "ab_w_in", "hg_lb_logits", "hg_out_norm", "ab_w_out", "c_w_in", "q_norm", "k_norm", "sinks",
           "rel_bias", "c_w_out", "ffn_norm", "ffn_up", "ffn_conv", "ffn_conv_b", "ffn_down", "ple_norm", "ple_gate",
           "ple_proj")
PACK_ALIGN = 8 * LANES


def _pack(arrs):
    pieces = []
    for a in arrs:
        flat = a.reshape(-1)
        pad = -flat.shape[0] % PACK_ALIGN
        pieces.append(jnp.pad(flat, (0, pad)).reshape(-1, LANES))
    return jnp.concatenate(pieces, axis=0)


def _unpack(packed, like):
    out, r = [], 0
    for a in like:
        size = int(np.prod(a.shape))
        rows = (size + PACK_ALIGN - 1) // PACK_ALIGN * 8
        out.append(packed[r:r + rows].reshape(-1)[:size].reshape(a.shape))
        r += rows
    return out


def _family_range(name, lo, hi):
    if name.startswith("ab_"):
        idx = [i // 2 for i in range(lo, hi) if i % 2 == 0]
    elif name.startswith("c_"):
        idx = [i // 2 for i in range(lo, hi) if i % 2 == 1]
    else:
        idx = list(range(lo, hi))
    return (idx[0], idx[-1] + 1) if idx else None


W_IN = ("ab_w_in", "c_w_in")
REST = tuple(k for k, _ in SHARDED if k not in W_IN)
MIXER = ("ab_w_in", "ab_w_out", "c_w_in", "c_w_out")
FFN = ("ffn_up", "ffn_conv", "ffn_down")
PLE = ("ple_gate", "ple_proj")
GATHER_PLAN = {
    "gather_first": [(0, W_IN)],
    "ab_in_0": [(0, ("ab_w_out",) + PLE)],
    "sb_fwd_0": [(0, FFN), (1, ("ffn_up", "ffn_conv"))],
    "hgrn_fwd_0": [(1, ("c_w_in",))],
    "ffn_up_0": [(1, ("ffn_down", "c_w_out"))],
    "ffn_down_0": [(1, PLE)],
    "swa_fwd_1": [(2, ("ab_w_in",))],
    "ffn_up_1": [(2, ("ab_w_out", "ffn_down"))],
    "ffn_down_1": [(2, PLE)],
    "sb_fwd_2": [(2, ("ffn_up", "ffn_conv")), (3, ("c_w_in", "c_w_out", "ffn_up", "ffn_conv"))],
    "hgrn_fwd_2": [(3, ("ffn_down",))],
    "ffn_up_2": [(3, PLE)],
}
SCATTER_PLAN = {
    "d_ffn_up_3": [(3, ("ffn_down", "ffn_conv"))],
    "d_hn2_3": [(3, PLE)],
    "swa_bwd_3": [(3, ("ffn_up", "c_w_out"))],
    "sb_bwd_2": [(3, W_IN), (2, REST)],
    "convglu_bwd_1": [(2, W_IN)],
    "d_ffn_up_1": [(1, ("ffn_down", "ffn_conv"))],
    "d_hn2_1": [(1, PLE)],
    "swa_bwd_1": [(1, ("ffn_up", "c_w_out"))],
    "sb_bwd_0": [(1, W_IN), (0, REST)],
    "d_hn_0": [(0, W_IN)],
}


class _StepExchanges:
    def __init__(self, shards, W):
        self.shards, self.W = shards, W
        self.stacks = {}
        for k, axis in SHARDED:
            shp = shards[k].shape
            self.W[k] = lax.empty(tuple(4 * d if i == axis else d for i, d in enumerate(shp)), shards[k].dtype)
            self.stacks[k] = lax.empty((4,) + shp, shards[k].dtype)

    @staticmethod
    def _select(plan):
        idx = {}
        for layer, fams in plan:
            for k, _ in SHARDED:
                r = _family_range(k, layer, layer + 1)
                if r is not None and (fams is None or k in fams):
                    idx.setdefault(k, []).append(r[0])
        return [(k, axis, sorted(idx[k])) for k, axis in SHARDED if k in idx]

    def gather(self, call):
        sel = self._select(GATHER_PLAN.get(call, ()))
        if not sel:
            return None, None
        for _, _, ii in sel:
            assert ii == list(range(ii[0], ii[-1] + 1)), "one copy per family takes a contiguous layer range"
        names = [k for k, _, _ in sel]
        ex = _gather_exchange([self.shards[k] for k in names], [self.W[k] for k in names],
                              [a for _, a, _ in sel], [(ii[0], ii[-1] + 1) for _, _, ii in sel])
        return ex, (self.W, names)

    def scatter(self, call, G):
        sel = self._select(SCATTER_PLAN.get(call, ()))
        if not sel:
            return None, None
        assert all(len(ii) == 1 for _, _, ii in sel)
        names = [k for k, _, _ in sel]
        ex = _scatter_exchange([G[k][ii[0]] for k, _, ii in sel], [self.stacks[k] for k in names],
                               [a for _, a, _ in sel], [ii[0] for _, _, ii in sel])
        return ex, (self.stacks, names)

    @staticmethod
    def adopt(where, targets):
        if where is not None:
            book, names = where
            for k, t in zip(names, targets):
                book[k] = t


def _forward_backward(x, p, target, W, S, exchanges=None):
    T = x.shape[0]
    depth = p.shape[0]
    lb = _lower_bound_fwd(W["hg_lb_logits"], "lower_bound_fwd")
    bias = _bias_build(W["rel_bias"], "bias_build")
    qk_gain = jnp.concatenate([jnp.tile(W["q_norm"], (1, SW_HEADS)), jnp.tile(W["k_norm"], (1, SW_KV))], axis=1)

    def gathering(call):
        return exchanges.gather(call) if exchanges else (None, None)

    def scattering(call):
        return exchanges.scatter(call, G) if exchanges else (None, None)

    def matmul(a, b, mode, name, **kw):
        ex, where = (gathering(name) if name in GATHER_PLAN else
                     scattering(name) if name in SCATTER_PLAN else (None, None))
        out, moved = _matmul(a, b, mode, name, exchange=ex, **kw)
        _StepExchanges.adopt(where, moved)
        return out

    def mm(a, wname, layer, mode, name, **kw):
        return matmul(a, W[wname], mode, name, b_layer=layer, **kw)

    saved = []
    h = x
    for i in range(depth):
        j = i // 2
        s = {"h0": h}
        s["hn"] = _rmsnorm_fwd(h, W["mix_norm"][i:i + 1], f"mix_norm_fwd_{i}") if i == 0 else hn_next
        if i % 2 == 0:
            s["proj"] = mm(s["hn"], "ab_w_in", j, "nn", f"ab_in_{i}")
            ex, where = gathering(f"sb_fwd_{i}")
            cat, s["sb_tot"], arrived = _sb_fwd(s["proj"], S, f"sb_fwd_{i}", ex)
            _StepExchanges.adopt(where, arrived)
            ex, where = gathering(f"hgrn_fwd_{i}")
            s["cat"], s["oraw"], s["states"], arrived = _hgrn_fwd(s["proj"], cat, lb[j:j + 1],
                                                                  W["hg_out_norm"][j:j + 1], S, f"hgrn_fwd_{i}", ex)
            _StepExchanges.adopt(where, arrived)
            h, s["hn2"] = mm(s["cat"], "ab_w_out", j, "nn", f"ab_out_{i}", res=h, norm_gain=W["ffn_norm"][i:i + 1])
        else:
            s["proj"] = mm(s["hn"], "c_w_in", j, "nn", f"c_in_{i}")
            s["qkn"] = _headnorm_fwd(s["proj"], qk_gain[j:j + 1], f"qk_norm_fwd_{i}")
            ex, where = gathering(f"swa_fwd_{i}")
            s["o"], arrived = _swa_fwd(s["qkn"], s["proj"], bias, W["sinks"][j], S, f"swa_fwd_{i}", ex)
            _StepExchanges.adopt(where, arrived)
            h, s["hn2"] = mm(s["o"], "c_w_out", j, "nn", f"c_out_{i}", res=h, norm_gain=W["ffn_norm"][i:i + 1])
        s["h1"] = h
        s["u"] = mm(s["hn2"], "ffn_up", i, "nn", f"ffn_up_{i}", tiles=(min(T, 1024), D_FF // 2, D_MODEL))
        s["a"] = _convglu_fwd(s["u"], W["ffn_conv"][i], W["ffn_conv_b"][i:i + 1], S, f"convglu_fwd_{i}")
        h, s["hn3"] = mm(s["a"], "ffn_down", i, "nn", f"ffn_down_{i}", res=h, norm_gain=W["ple_norm"][i:i + 1])
        s["h2"] = h
        s["z"] = mm(s["hn3"], "ple_gate", i, "nn", f"ple_gate_{i}")
        s["pp"] = mm(p, "ple_proj", i, "nn", f"ple_proj_{i}", a_layer=i)
        h, hn_next = _ple_fwd(h, s["z"], s["pp"], W["mix_norm"][i + 1:i + 2] if i + 1 < depth else None,
                              f"ple_fwd_{i}")
        saved.append(s)

    loss, dh = _loss_fwd_bwd(h, target, "loss")

    G = {k: [None] * depth for k in ("mix_norm", "ffn_norm", "ple_norm", "ffn_up", "ffn_conv", "ffn_conv_b",
                                     "ffn_down", "ple_gate", "ple_proj")}
    for k in ("ab_w_in", "ab_w_out", "c_w_in", "c_w_out", "hg_out_norm", "q_norm", "k_norm", "sinks", "lb"):
        G[k] = [None] * (depth // 2)
    dbias_total = None
    for i in reversed(range(depth)):
        j = i // 2
        s = saved[i]
        dz, dpp = _ple_bwd(dh, s["z"], s["pp"], f"ple_bwd_{i}")
        G["ple_proj"][i] = matmul(p, dpp, "tn", f"d_ple_proj_{i}", out_dtype=BF16, a_layer=i)
        G["ple_gate"][i] = matmul(s["hn3"], dz, "tn", f"d_ple_gate_{i}", out_dtype=BF16)
        rt = min(T, 512)
        dh, G["ple_norm"][i] = mm(dz, "ple_gate", i, "nt", f"d_hn3_{i}", tiles=(rt, D_MODEL, D_MODEL),
                                  norm_bwd=(s["h2"], W["ple_norm"][i:i + 1], dh))

        half_ff = D_FF // 2
        da = mm(dh, "ffn_down", i, "nt", f"d_a_{i}", tiles=(min(T, 1024), half_ff, D_MODEL))
        G["ffn_down"][i] = matmul(s["a"], dh, "tn", f"d_ffn_down_{i}", out_dtype=BF16,
                                   tiles=(half_ff, 512, min(T, 2048)))
        ex, where = scattering(f"convglu_bwd_{i}")
        du, dcw, dcb, sent = _convglu_bwd(s["u"], da, W["ffn_conv"][i], W["ffn_conv_b"][i:i + 1], S,
                                          f"convglu_bwd_{i}", ex)
        _StepExchanges.adopt(where, sent)
        G["ffn_conv"][i] = jnp.swapaxes(dcw, 0, 1).reshape(3, 2 * D_FF)
        G["ffn_conv_b"][i] = dcb.reshape(1, 2 * D_FF)
        G["ffn_up"][i] = matmul(s["hn2"], du, "tn", f"d_ffn_up_{i}", out_dtype=BF16,
                                 tiles=(D_MODEL, half_ff, min(T, 1024)))
        dh, G["ffn_norm"][i] = mm(du, "ffn_up", i, "nt", f"d_hn2_{i}", tiles=(rt, D_MODEL, half_ff),
                                  norm_bwd=(s["h1"], W["ffn_norm"][i:i + 1], dh))

        if i % 2 == 0:
            dcat = mm(dh, "ab_w_out", j, "nt", f"d_cat_{i}")
            G["ab_w_out"][j] = matmul(s["cat"], dh, "tn", f"d_ab_out_{i}", out_dtype=BF16)
            ex, where = scattering(f"sb_bwd_{i}")
            dproj, sent = _sb_bwd(s["proj"], s["sb_tot"], dcat, S, f"sb_bwd_{i}", ex)
            _StepExchanges.adopt(where, sent)
            dproj, G["lb"][j], G["hg_out_norm"][j] = _hgrn_bwd(s["proj"], s["oraw"], dcat, s["states"], lb[j:j + 1],
                                                               W["hg_out_norm"][j:j + 1], dproj, S, f"hgrn_bwd_{i}")
            G["ab_w_in"][j] = matmul(s["hn"], dproj, "tn", f"d_ab_in_{i}", out_dtype=BF16)
            w_in, width = "ab_w_in", dproj.shape[1] // 2
        else:
            do = mm(dh, "c_w_out", j, "nt", f"d_o_{i}")
            G["c_w_out"][j] = matmul(s["o"], dh, "tn", f"d_c_out_{i}", out_dtype=BF16)
            ex, where = scattering(f"swa_bwd_{i}")
            dqkv, dbias, dsink, sent = _swa_bwd(s["qkn"], s["proj"], bias, W["sinks"][j], do, S, f"swa_bwd_{i}", ex)
            _StepExchanges.adopt(where, sent)
            dbias_total = dbias if dbias_total is None else dbias_total + dbias
            G["sinks"][j] = dsink[:, 0]
            dproj, dgain = _headnorm_bwd(s["proj"], qk_gain[j:j + 1], dqkv, f"qk_norm_bwd_{i}")
            G["q_norm"][j] = dgain[0, :SW_Q_COLS].reshape(SW_HEADS, SW_DIM).sum(axis=0)
            G["k_norm"][j] = dgain[0, SW_Q_COLS:].reshape(SW_KV, SW_DIM).sum(axis=0)
            G["c_w_in"][j] = matmul(s["hn"], dproj, "tn", f"d_c_in_{i}", out_dtype=BF16)
            w_in, width = "c_w_in", dproj.shape[1]
        dh, G["mix_norm"][i] = mm(dproj, w_in, j, "nt", f"d_hn_{i}", tiles=(rt, D_MODEL, width),
                                  norm_bwd=(s["h0"], W["mix_norm"][i:i + 1], dh))

    grads = {k: G[k] for k, _ in SHARDED}
    for k in ("q_norm", "k_norm", "sinks"):
        grads[k] = jnp.stack(G[k])
    for k in ("mix_norm", "ffn_norm", "ple_norm", "ffn_conv_b", "hg_out_norm"):
        grads[k] = jnp.concatenate(G[k], axis=0)
    grads["hg_lb_logits"] = _lower_bound_bwd(W["hg_lb_logits"], jnp.concatenate(G["lb"], axis=0), "lower_bound_bwd")
    grads["rel_bias"] = _bias_reduce(dbias_total, "bias_reduce")[:, 0, :N_BUCKETS].T
    return loss, dh, grads


def kernel(x, p, mix_norm, ab_w_in, hg_lb_logits, hg_out_norm, ab_w_out, c_w_in, q_norm, k_norm, sinks, rel_bias, c_w_out, ffn_norm, ffn_up, ffn_conv, ffn_conv_b, ffn_down, ple_norm, ple_gate, ple_proj, loss_target, m_mix_norm, m_ab_w_in, m_hg_lb_logits, m_hg_out_norm, m_ab_w_out, m_c_w_in, m_q_norm, m_k_norm, m_sinks, m_rel_bias, m_c_w_out, m_ffn_norm, m_ffn_up, m_ffn_conv, m_ffn_conv_b, m_ffn_down, m_ple_norm, m_ple_gate, m_ple_proj, v_mix_norm, v_ab_w_in, v_hg_lb_logits, v_hg_out_norm, v_ab_w_out, v_c_w_in, v_q_norm, v_k_norm, v_sinks, v_rel_bias, v_c_w_out, v_ffn_norm, v_ffn_up, v_ffn_conv, v_ffn_conv_b, v_ffn_down, v_ple_norm, v_ple_gate, v_ple_proj):
    args = dict(locals())
    w = {k: args[k] for k in WEIGHTS}
    m = {k: args["m_" + k] for k in WEIGHTS}
    v = {k: args["v_" + k] for k in WEIGHTS}
    B, S, Dm = x.shape
    T = B * S
    names = [k for k, _ in SHARDED]

    W = {k: w[k] for k in SMALL}
    exchanges = _StepExchanges({k: w[k].astype(F32 if k == "ffn_conv" else BF16) for k in names}, W)
    first, where = exchanges.gather("gather_first")
    exchanges.adopt(where, _exchange_call(first, "gather_first"))

    loss, dx, grads = _forward_backward(x.reshape(T, Dm), p.reshape(p.shape[0], T, p.shape[-1]),
                                        loss_target.reshape(T, Dm), W, S, exchanges)
    loss = lax.psum(loss[0, 0], ("x", "y", "c"))

    stacks = [exchanges.stacks[k] for k in names]
    partial = [_sum4(st.reshape(4, -1, st.shape[-1]), f"sum_chips_{k}") for k, st in zip(names, stacks)]
    other = _swap_with_sibling(partial, "swap_core_sums")
    small_sum = _allreduce_small(_pack([grads[k] for k in SMALL]), "allreduce_small")

    out_g, out_d, out_m, out_v = {}, {}, {}, {}
    for k, mine, theirs in zip(names, partial, other):
        shp = w[k].shape
        r = [a.reshape(shp) for a in _adamw(w[k].reshape(mine.shape), m[k].reshape(mine.shape),
                                            v[k].reshape(mine.shape), [mine, theirs], f"adamw_{k}")]
        out_g[k], out_d[k], out_m[k], out_v[k] = r
    sm = _adamw(_pack([w[k] for k in SMALL]), _pack([m[k] for k in SMALL]), _pack([v[k] for k in SMALL]),
                [small_sum], "adamw_small")
    like = [w[k] for k in SMALL]
    for dst, packed in zip((out_g, out_d, out_m, out_v), sm):
        for k, a in zip(SMALL, _unpack(packed, like)):
            dst[k] = a

    return (loss, dx.reshape(B, S, Dm), *[out_g[k] for k in WEIGHTS], *[out_d[k] for k in WEIGHTS],
            *[out_m[k] for k in WEIGHTS], *[out_v[k] for k in WEIGHTS])
```
